```python
import math, functools
import jax, jax.numpy as jnp
from jax import lax
import numpy as np

D_MODEL = 1024
BATCH = 8
SEQ = 8192
DEPTH = 1

N_META = 16
CHUNK = 64
GDN_HEADS = 8
GDN_DK = 128
GDN_DV = 128
RET_HEADS = 8
RET_DK = 128
RET_DV = 128
CONV_K = 4
D_FF = 2816
ROPE_BASE = 10000.0
EPS = 1e-6

GDN_QK = GDN_HEADS * GDN_DK
GDN_V = GDN_HEADS * GDN_DV
GDN_CONV = 2 * GDN_QK + GDN_V
RET_QK = RET_HEADS * RET_DK
RET_V = RET_HEADS * RET_DV
PROJ_SIZES = (GDN_CONV, GDN_V, GDN_HEADS, GDN_HEADS, RET_QK, RET_QK, RET_V, RET_V, D_MODEL, D_MODEL)
D_PROJ = sum(PROJ_SIZES)

kernel_name = "hybrid_gdn_retention_macaron_layer"


def rms_norm(x, w):
    xf = x.astype(jnp.float32)
    y = xf * lax.rsqrt(jnp.mean(xf * xf, axis=-1, keepdims=True) + EPS)
    return (y * w.astype(jnp.float32)).astype(x.dtype)


def swiglu(x, w_in, w_out):
    gate, up = jnp.split(x @ w_in, 2, axis=-1)
    return (jax.nn.silu(gate) * up) @ w_out


def causal_depthwise_conv(x, w):
    c = x.shape[-1]
    return lax.conv_general_dilated(
        x, w[:, None, :].astype(x.dtype), window_strides=(1,), padding=[(CONV_K - 1, 0)],
        dimension_numbers=("NWC", "WIO", "NWC"), feature_group_count=c)


def to_heads(t, n_heads):
    b, l, _ = t.shape
    return t.reshape(b, l, n_heads, -1).transpose(0, 2, 1, 3).astype(jnp.float32)


def l2norm(t):
    return t * lax.rsqrt(jnp.sum(t * t, axis=-1, keepdims=True) + EPS)


def rotary(t, pos):
    d = t.shape[-1]
    inv = 1.0 / (ROPE_BASE ** jnp.linspace(0.0, 1.0, d // 2, dtype=jnp.float32))
    ang = pos[:, None] * inv[None, :]
    cos, sin = jnp.cos(ang), jnp.sin(ang)
    tp = t.reshape(*t.shape[:-1], d // 2, 2)
    t0, t1 = tp[..., 0], tp[..., 1]
    return jnp.stack([t0 * cos - t1 * sin, t1 * cos + t0 * sin], axis=-1).reshape(t.shape)


def gdn_chunk_scan(q, k, v, g, beta, state, chunk):
    b, h, l, dk = q.shape
    n = l // chunk
    split = lambda t: t.reshape(b, h, n, chunk, *t.shape[3:])
    q, k, v, g, beta = split(q), split(k), split(v), split(g), split(beta)
    g = jnp.cumsum(g, axis=-1)
    causal = jnp.tril(jnp.ones((chunk, chunk), dtype=bool))
    strict = jnp.tril(jnp.ones((chunk, chunk), dtype=bool), -1)
    diff = g[..., :, None] - g[..., None, :]
    decay = jnp.where(causal, jnp.exp(jnp.where(causal, diff, 0.0)), 0.0)
    k_beta = k * beta[..., None]
    a = jnp.where(strict, jnp.einsum("bhncd,bhnmd->bhncm", k_beta, k) * decay, 0.0) + jnp.eye(chunk, dtype=q.dtype)
    solve = functools.partial(lax.linalg.triangular_solve, left_side=True, lower=True)
    u = solve(a, v * beta[..., None])
    w = solve(a, k_beta * jnp.exp(g)[..., None])
    qk = jnp.einsum("bhncd,bhnmd->bhncm", q, k) * decay
    g_last = g[..., -1]
    q_dec = q * jnp.exp(g)[..., None]
    k_dec = k * jnp.exp(g_last[..., None] - g)[..., None]

    def step(s, xs):
        qk_c, u_c, w_c, qd_c, kd_c, gl_c = xs
        v_new = u_c - jnp.einsum("bhck,bhkv->bhcv", w_c, s)
        o = jnp.einsum("bhck,bhkv->bhcv", qd_c, s) + jnp.einsum("bhcm,bhmv->bhcv", qk_c, v_new)
        s = s * jnp.exp(gl_c)[..., None, None] + jnp.einsum("bhck,bhcv->bhkv", kd_c, v_new)
        return s, o

    xs = tuple(jnp.moveaxis(t, 2, 0) for t in (qk, u, w, q_dec, k_dec, g_last))
    state, o = lax.scan(step, state, xs)
    return jnp.moveaxis(o, 0, 2).reshape(b, h, l, -1), state


def retention_chunk_scan(q, k, v, log_gamma, state, chunk):
    b, h, l, dk = q.shape
    n = l // chunk
    split = lambda t: t.reshape(b, h, n, chunk, t.shape[-1])
    q, k, v = split(q), split(k), split(v)
    pos = jnp.arange(chunk, dtype=jnp.float32)
    lg = log_gamma[:, None]
    causal = jnp.tril(jnp.ones((chunk, chunk), dtype=bool))
    diff = pos[:, None] - pos[None, :]
    decay = jnp.where(causal, jnp.exp(jnp.where(causal, diff, 0.0) * log_gamma[:, None, None]), 0.0)
    scores = jnp.einsum("bhncd,bhnmd->bhncm", q, k) * decay[None, :, None]
    intra = jnp.einsum("bhncm,bhnmv->bhncv", scores, v)
    q_dec = q * jnp.exp((pos + 1.0) * lg)[None, :, None, :, None]
    k_dec = k * jnp.exp((chunk - 1.0 - pos) * lg)[None, :, None, :, None]
    chunk_decay = jnp.exp(chunk * log_gamma)[None, :, None, None]

    def step(s, xs):
        qd, kd, vc = xs
        o = jnp.einsum("bhck,bhkv->bhcv", qd, s)
        s = s * chunk_decay + jnp.einsum("bhck,bhcv->bhkv", kd, vc)
        return s, o

    xs = tuple(jnp.moveaxis(t, 2, 0) for t in (q_dec, k_dec, v))
    state, inter = lax.scan(step, state, xs)
    o = intra + jnp.moveaxis(inter, 0, 2)
    return o.reshape(b, h, l, -1), state


def hybrid_mixer(n, w_in, conv_w, a_log, dt_bias, gdn_norm, ret_norm, w_br_gdn, w_br_ret, w_out):
    b, l, _ = n.shape
    f32 = jnp.float32
    offs = [int(i) for i in np.cumsum(PROJ_SIZES)[:-1]]
    qkv, z, b_raw, a_raw, rq, rk, rv, rg, ga, gb = jnp.split(n @ w_in, offs, axis=-1)

    qkv = jax.nn.silu(causal_depthwise_conv(qkv, conv_w))
    q, k, v = jnp.split(qkv, [GDN_QK, 2 * GDN_QK], axis=-1)
    q = l2norm(to_heads(q, GDN_HEADS)) * (GDN_DK ** -0.5)
    k = l2norm(to_heads(k, GDN_HEADS))
    v = to_heads(v, GDN_HEADS)
    g = (-jnp.exp(a_log.astype(f32)) * jax.nn.softplus(a_raw.astype(f32) + dt_bias.astype(f32))).transpose(0, 2, 1)
    beta = jax.nn.sigmoid(b_raw.astype(f32)).transpose(0, 2, 1)
    s0 = jnp.zeros((b, GDN_HEADS, GDN_DK, GDN_DV), f32)
    o_m, s_m = gdn_chunk_scan(q[:, :, :N_META], k[:, :, :N_META], v[:, :, :N_META],
                              g[:, :, :N_META], beta[:, :, :N_META], s0, N_META)
    o_r, _ = gdn_chunk_scan(q[:, :, N_META:], k[:, :, N_META:], v[:, :, N_META:],
                            g[:, :, N_META:], beta[:, :, N_META:], s_m, CHUNK)
    o_a = jnp.concatenate([o_m, o_r], axis=2).transpose(0, 2, 1, 3)
    o_a = o_a * lax.rsqrt(jnp.mean(o_a * o_a, axis=-1, keepdims=True) + EPS) * gdn_norm.astype(f32)
    y_a = (o_a * jax.nn.silu(z.reshape(b, l, GDN_HEADS, GDN_DV).astype(f32))).reshape(b, l, GDN_V).astype(n.dtype)

    pos = jnp.arange(l, dtype=f32)
    rq = rotary(to_heads(rq, RET_HEADS), pos)
    rk = rotary(to_heads(rk, RET_HEADS), pos) * (RET_DK ** -0.5)
    rv = to_heads(rv, RET_HEADS)
    log_gamma = jnp.log1p(-jnp.exp2(-5.0 - jnp.arange(RET_HEADS, dtype=f32)))
    r0 = jnp.zeros((b, RET_HEADS, RET_DK, RET_DV), f32)
    p_m, r_m = retention_chunk_scan(rq[:, :, :N_META], rk[:, :, :N_META], rv[:, :, :N_META], log_gamma, r0, N_META)
    p_r, _ = retention_chunk_scan(rq[:, :, N_META:], rk[:, :, N_META:], rv[:, :, N_META:], log_gamma, r_m, CHUNK)
    o_b = jnp.concatenate([p_m, p_r], axis=2).transpose(0, 2, 1, 3)
    mu = jnp.mean(o_b, axis=-1, keepdims=True)
    var = jnp.mean(jnp.square(o_b - mu), axis=-1, keepdims=True)
    o_b = ((o_b - mu) * lax.rsqrt(var + EPS)).reshape(b, l, RET_V) * ret_norm.astype(f32)
    y_b = (jax.nn.silu(rg.astype(f32)) * o_b).astype(n.dtype)

    merged = jax.nn.sigmoid(ga) * (y_a @ w_br_gdn) + jax.nn.sigmoid(gb) * (y_b @ w_br_ret)
    return merged @ w_out


def _fwd_setup_inputs(seed: int = 0) -> dict:
    key = jax.random.key(seed)
    ks = jax.random.split(key, 20)
    f32 = jnp.float32
    nrm = lambda k, shape, scale: jax.random.normal(k, shape, f32) * scale
    gain = lambda k, shape: 1.0 + 0.02 * jax.random.normal(k, shape, f32)
    dt = jnp.exp(jax.random.uniform(ks[8], (DEPTH, GDN_HEADS), f32, math.log(1e-3), math.log(1e-1)))
    return {
        "x": nrm(ks[0], (BATCH, SEQ, D_MODEL), 1.0),
        "meta_tokens": nrm(ks[1], (N_META, D_MODEL), 1.0),
        "ffn1_norm": gain(ks[2], (DEPTH, D_MODEL)),
        "ffn1_w_in": nrm(ks[3], (DEPTH, D_MODEL, 2 * D_FF), D_MODEL ** -0.5),
        "ffn1_w_out": nrm(ks[4], (DEPTH, D_FF, D_MODEL), D_FF ** -0.5),
        "mix_norm": gain(ks[5], (DEPTH, D_MODEL)),
        "w_in": nrm(ks[6], (DEPTH, D_MODEL, D_PROJ), D_MODEL ** -0.5),
        "gdn_conv_w": nrm(ks[7], (DEPTH, CONV_K, GDN_CONV), CONV_K ** -0.5),
        "gdn_a_log": jnp.log(jax.random.uniform(ks[9], (DEPTH, GDN_HEADS), f32, 1.0, 16.0)),
        "gdn_dt_bias": dt + jnp.log(-jnp.expm1(-dt)),
        "gdn_out_norm": gain(ks[10], (DEPTH, GDN_DV)),
        "ret_out_norm": gain(ks[11], (DEPTH, RET_V)),
        "w_branch_gdn": nrm(ks[12], (DEPTH, GDN_V, D_MODEL), GDN_V ** -0.5),
        "w_branch_ret": nrm(ks[13], (DEPTH, RET_V, D_MODEL), RET_V ** -0.5),
        "w_out": nrm(ks[14], (DEPTH, D_MODEL, D_MODEL), D_MODEL ** -0.5),
        "ffn2_norm": gain(ks[15], (DEPTH, D_MODEL)),
        "ffn2_w_in": nrm(ks[16], (DEPTH, D_MODEL, 2 * D_FF), D_MODEL ** -0.5),
        "ffn2_w_out": nrm(ks[17], (DEPTH, D_FF, D_MODEL), D_FF ** -0.5),
        "final_norm": gain(ks[18], (D_MODEL,)),
    }


def _fwd_reference(x, meta_tokens, ffn1_norm, ffn1_w_in, ffn1_w_out, mix_norm, w_in, gdn_conv_w,
              gdn_a_log, gdn_dt_bias, gdn_out_norm, ret_out_norm, w_branch_gdn, w_branch_ret,
              w_out, ffn2_norm, ffn2_w_in, ffn2_w_out, final_norm):
    b = x.shape[0]
    meta = jnp.broadcast_to(meta_tokens[None].astype(x.dtype), (b, N_META, D_MODEL))
    h = jnp.concatenate([meta, x], axis=1)
    for i in range(DEPTH):
        h = h + 0.5 * swiglu(rms_norm(h, ffn1_norm[i]), ffn1_w_in[i], ffn1_w_out[i])
        h = h + hybrid_mixer(rms_norm(h, mix_norm[i]), w_in[i], gdn_conv_w[i], gdn_a_log[i],
                             gdn_dt_bias[i], gdn_out_norm[i], ret_out_norm[i],
                             w_branch_gdn[i], w_branch_ret[i], w_out[i])
        h = h + 0.5 * swiglu(rms_norm(h, ffn2_norm[i]), ffn2_w_in[i], ffn2_w_out[i])
    return rms_norm(h, final_norm)[:, N_META:]


import jax as _jax
import jax.numpy as _jnp

TWIN_FORMAT = 'train_step'
FWD_PARAMS = ['x', 'meta_tokens', 'ffn1_norm', 'ffn1_w_in', 'ffn1_w_out', 'mix_norm', 'w_in', 'gdn_conv_w', 'gdn_a_log', 'gdn_dt_bias', 'gdn_out_norm', 'ret_out_norm', 'w_branch_gdn', 'w_branch_ret', 'w_out', 'ffn2_norm', 'ffn2_w_in', 'ffn2_w_out', 'final_norm']
TWIN_WEIGHTS = ['meta_tokens', 'ffn1_norm', 'ffn1_w_in', 'ffn1_w_out', 'mix_norm', 'w_in', 'gdn_conv_w', 'gdn_a_log', 'gdn_dt_bias', 'gdn_out_norm', 'ret_out_norm', 'w_branch_gdn', 'w_branch_ret', 'w_out', 'ffn2_norm', 'ffn2_w_in', 'ffn2_w_out', 'final_norm']
TWIN_DIFF_INPUT = 'x'
TWIN_INPUTS = ['x', 'meta_tokens', 'ffn1_norm', 'ffn1_w_in', 'ffn1_w_out', 'mix_norm', 'w_in', 'gdn_conv_w', 'gdn_a_log', 'gdn_dt_bias', 'gdn_out_norm', 'ret_out_norm', 'w_branch_gdn', 'w_branch_ret', 'w_out', 'ffn2_norm', 'ffn2_w_in', 'ffn2_w_out', 'final_norm', 'loss_target', 'm_meta_tokens', 'm_ffn1_norm', 'm_ffn1_w_in', 'm_ffn1_w_out', 'm_mix_norm', 'm_w_in', 'm_gdn_conv_w', 'm_gdn_a_log', 'm_gdn_dt_bias', 'm_gdn_out_norm', 'm_ret_out_norm', 'm_w_branch_gdn', 'm_w_branch_ret', 'm_w_out', 'm_ffn2_norm', 'm_ffn2_w_in', 'm_ffn2_w_out', 'm_final_norm', 'v_meta_tokens', 'v_ffn1_norm', 'v_ffn1_w_in', 'v_ffn1_w_out', 'v_mix_norm', 'v_w_in', 'v_gdn_conv_w', 'v_gdn_a_log', 'v_gdn_dt_bias', 'v_gdn_out_norm', 'v_ret_out_norm', 'v_w_branch_gdn', 'v_w_branch_ret', 'v_w_out', 'v_ffn2_norm', 'v_ffn2_w_in', 'v_ffn2_w_out', 'v_final_norm']
TWIN_OUTPUTS = ['loss', 'grad_x', 'grad_meta_tokens', 'grad_ffn1_norm', 'grad_ffn1_w_in', 'grad_ffn1_w_out', 'grad_mix_norm', 'grad_w_in', 'grad_gdn_conv_w', 'grad_gdn_a_log', 'grad_gdn_dt_bias', 'grad_gdn_out_norm', 'grad_ret_out_norm', 'grad_w_branch_gdn', 'grad_w_branch_ret', 'grad_w_out', 'grad_ffn2_norm', 'grad_ffn2_w_in', 'grad_ffn2_w_out', 'grad_final_norm', 'delta_meta_tokens', 'delta_ffn1_norm', 'delta_ffn1_w_in', 'delta_ffn1_w_out', 'delta_mix_norm', 'delta_w_in', 'delta_gdn_conv_w', 'delta_gdn_a_log', 'delta_gdn_dt_bias', 'delta_gdn_out_norm', 'delta_ret_out_norm', 'delta_w_branch_gdn', 'delta_w_branch_ret', 'delta_w_out', 'delta_ffn2_norm', 'delta_ffn2_w_in', 'delta_ffn2_w_out', 'delta_final_norm', 'new_m_meta_tokens', 'new_m_ffn1_norm', 'new_m_ffn1_w_in', 'new_m_ffn1_w_out', 'new_m_mix_norm', 'new_m_w_in', 'new_m_gdn_conv_w', 'new_m_gdn_a_log', 'new_m_gdn_dt_bias', 'new_m_gdn_out_norm', 'new_m_ret_out_norm', 'new_m_w_branch_gdn', 'new_m_w_branch_ret', 'new_m_w_out', 'new_m_ffn2_norm', 'new_m_ffn2_w_in', 'new_m_ffn2_w_out', 'new_m_final_norm', 'new_v_meta_tokens', 'new_v_ffn1_norm', 'new_v_ffn1_w_in', 'new_v_ffn1_w_out', 'new_v_mix_norm', 'new_v_w_in', 'new_v_gdn_conv_w', 'new_v_gdn_a_log', 'new_v_gdn_dt_bias', 'new_v_gdn_out_norm', 'new_v_ret_out_norm', 'new_v_w_branch_gdn', 'new_v_w_branch_ret', 'new_v_w_out', 'new_v_ffn2_norm', 'new_v_ffn2_w_in', 'new_v_ffn2_w_out', 'new_v_final_norm']
TWIN_LEAF_KINDS = {'loss': 'loss', 'grad_x': 'grad_x', 'grad_meta_tokens': 'grad_w', 'grad_ffn1_norm': 'grad_w', 'grad_ffn1_w_in': 'grad_w', 'grad_ffn1_w_out': 'grad_w', 'grad_mix_norm': 'grad_w', 'grad_w_in': 'grad_w', 'grad_gdn_conv_w': 'grad_w', 'grad_gdn_a_log': 'grad_w', 'grad_gdn_dt_bias': 'grad_w', 'grad_gdn_out_norm': 'grad_w', 'grad_ret_out_norm': 'grad_w', 'grad_w_branch_gdn': 'grad_w', 'grad_w_branch_ret': 'grad_w', 'grad_w_out': 'grad_w', 'grad_ffn2_norm': 'grad_w', 'grad_ffn2_w_in': 'grad_w', 'grad_ffn2_w_out': 'grad_w', 'grad_final_norm': 'grad_w', 'delta_meta_tokens': 'delta_w', 'delta_ffn1_norm': 'delta_w', 'delta_ffn1_w_in': 'delta_w', 'delta_ffn1_w_out': 'delta_w', 'delta_mix_norm': 'delta_w', 'delta_w_in': 'delta_w', 'delta_gdn_conv_w': 'delta_w', 'delta_gdn_a_log': 'delta_w', 'delta_gdn_dt_bias': 'delta_w', 'delta_gdn_out_norm': 'delta_w', 'delta_ret_out_norm': 'delta_w', 'delta_w_branch_gdn': 'delta_w', 'delta_w_branch_ret': 'delta_w', 'delta_w_out': 'delta_w', 'delta_ffn2_norm': 'delta_w', 'delta_ffn2_w_in': 'delta_w', 'delta_ffn2_w_out': 'delta_w', 'delta_final_norm': 'delta_w', 'new_m_meta_tokens': 'new_m', 'new_m_ffn1_norm': 'new_m', 'new_m_ffn1_w_in': 'new_m', 'new_m_ffn1_w_out': 'new_m', 'new_m_mix_norm': 'new_m', 'new_m_w_in': 'new_m', 'new_m_gdn_conv_w': 'new_m', 'new_m_gdn_a_log': 'new_m', 'new_m_gdn_dt_bias': 'new_m', 'new_m_gdn_out_norm': 'new_m', 'new_m_ret_out_norm': 'new_m', 'new_m_w_branch_gdn': 'new_m', 'new_m_w_branch_ret': 'new_m', 'new_m_w_out': 'new_m', 'new_m_ffn2_norm': 'new_m', 'new_m_ffn2_w_in': 'new_m', 'new_m_ffn2_w_out': 'new_m', 'new_m_final_norm': 'new_m', 'new_v_meta_tokens': 'new_v', 'new_v_ffn1_norm': 'new_v', 'new_v_ffn1_w_in': 'new_v', 'new_v_ffn1_w_out': 'new_v', 'new_v_mix_norm': 'new_v', 'new_v_w_in': 'new_v', 'new_v_gdn_conv_w': 'new_v', 'new_v_gdn_a_log': 'new_v', 'new_v_gdn_dt_bias': 'new_v', 'new_v_gdn_out_norm': 'new_v', 'new_v_ret_out_norm': 'new_v', 'new_v_w_branch_gdn': 'new_v', 'new_v_w_branch_ret': 'new_v', 'new_v_w_out': 'new_v', 'new_v_ffn2_norm': 'new_v', 'new_v_ffn2_w_in': 'new_v', 'new_v_ffn2_w_out': 'new_v', 'new_v_final_norm': 'new_v'}


def _forward(args):
    return _fwd_reference(*[args[k] for k in FWD_PARAMS])


def _output_shape():
    out = _jax.eval_shape(lambda: _forward(_fwd_setup_inputs(0)))
    return out.shape, out.dtype

N_MICROBATCH = 1
ADAM_LR = 0.001
ADAM_B1 = 0.9
ADAM_B2 = 0.999
ADAM_EPS = 1e-08
ADAM_WD = 0.01
ADAM_STEP = 10
PER_EXAMPLE_BATCH_AXIS = {'x': 0, 'loss_target': 0}
SHARED_INPUTS = []
_WEIGHT_DTYPES = {'meta_tokens': _jnp.float32, 'ffn1_norm': _jnp.float32, 'ffn1_w_in': _jnp.float32, 'ffn1_w_out': _jnp.float32, 'mix_norm': _jnp.float32, 'w_in': _jnp.float32, 'gdn_conv_w': _jnp.float32, 'gdn_a_log': _jnp.float32, 'gdn_dt_bias': _jnp.float32, 'gdn_out_norm': _jnp.float32, 'ret_out_norm': _jnp.float32, 'w_branch_gdn': _jnp.float32, 'w_branch_ret': _jnp.float32, 'w_out': _jnp.float32, 'ffn2_norm': _jnp.float32, 'ffn2_w_in': _jnp.float32, 'ffn2_w_out': _jnp.float32, 'final_norm': _jnp.float32}
MOMENT_SCALE = {'meta_tokens': 1.053190e-02, 'ffn1_norm': 1.244797e-01, 'ffn1_w_in': 5.385159e-02, 'ffn1_w_out': 8.787777e-02, 'mix_norm': 1.982230e-01, 'w_in': 6.231006e-02, 'gdn_conv_w': 5.587816e-02, 'gdn_a_log': 2.942478e-01, 'gdn_dt_bias': 2.901509e-01, 'gdn_out_norm': 2.482255e-01, 'ret_out_norm': 7.921121e-02, 'w_branch_gdn': 7.238670e-02, 'w_branch_ret': 7.377744e-02, 'w_out': 1.032417e-01, 'ffn2_norm': 9.969741e-02, 'ffn2_w_in': 3.899277e-02, 'ffn2_w_out': 6.352174e-02, 'final_norm': 6.395090e+01}


def _to_microbatches(a, axis):
    t = _jnp.moveaxis(a, axis, 0)
    t = t.reshape((N_MICROBATCH, t.shape[0] // N_MICROBATCH) + t.shape[1:])
    return _jnp.moveaxis(t, 1, axis + 1)


def setup_inputs(seed: int = 0) -> dict:
    inp = _fwd_setup_inputs(seed)
    key = _jax.random.fold_in(_jax.random.key(seed), 7919)
    shape, _ = _output_shape()
    out = dict(inp)
    out["loss_target"] = _jax.random.normal(_jax.random.fold_in(key, 0), shape, _jnp.float32)
    for i, name in enumerate(TWIN_WEIGHTS):
        w = inp[name].astype(_jnp.float32)
        if MOMENT_SCALE is None:
            s = _jnp.sqrt(_jnp.mean(_jnp.square(w)) + 1e-30)
        else:
            s = MOMENT_SCALE[name]
        km, kv = _jax.random.split(_jax.random.fold_in(key, i + 1))
        out[name] = w
        out["m_" + name] = s * _jax.random.normal(km, w.shape, _jnp.float32)
        out["v_" + name] = (s * s) * _jax.random.uniform(kv, w.shape, _jnp.float32, 0.5, 1.5)
    if N_MICROBATCH > 1:
        for name, axis in PER_EXAMPLE_BATCH_AXIS.items():
            out[name] = _to_microbatches(out[name], axis)
    return {'x': out['x'], 'meta_tokens': out['meta_tokens'], 'ffn1_norm': out['ffn1_norm'], 'ffn1_w_in': out['ffn1_w_in'], 'ffn1_w_out': out['ffn1_w_out'], 'mix_norm': out['mix_norm'], 'w_in': out['w_in'], 'gdn_conv_w': out['gdn_conv_w'], 'gdn_a_log': out['gdn_a_log'], 'gdn_dt_bias': out['gdn_dt_bias'], 'gdn_out_norm': out['gdn_out_norm'], 'ret_out_norm': out['ret_out_norm'], 'w_branch_gdn': out['w_branch_gdn'], 'w_branch_ret': out['w_branch_ret'], 'w_out': out['w_out'], 'ffn2_norm': out['ffn2_norm'], 'ffn2_w_in': out['ffn2_w_in'], 'ffn2_w_out': out['ffn2_w_out'], 'final_norm': out['final_norm'], 'loss_target': out['loss_target'], 'm_meta_tokens': out['m_meta_tokens'], 'm_ffn1_norm': out['m_ffn1_norm'], 'm_ffn1_w_in': out['m_ffn1_w_in'], 'm_ffn1_w_out': out['m_ffn1_w_out'], 'm_mix_norm': out['m_mix_norm'], 'm_w_in': out['m_w_in'], 'm_gdn_conv_w': out['m_gdn_conv_w'], 'm_gdn_a_log': out['m_gdn_a_log'], 'm_gdn_dt_bias': out['m_gdn_dt_bias'], 'm_gdn_out_norm': out['m_gdn_out_norm'], 'm_ret_out_norm': out['m_ret_out_norm'], 'm_w_branch_gdn': out['m_w_branch_gdn'], 'm_w_branch_ret': out['m_w_branch_ret'], 'm_w_out': out['m_w_out'], 'm_ffn2_norm': out['m_ffn2_norm'], 'm_ffn2_w_in': out['m_ffn2_w_in'], 'm_ffn2_w_out': out['m_ffn2_w_out'], 'm_final_norm': out['m_final_norm'], 'v_meta_tokens': out['v_meta_tokens'], 'v_ffn1_norm': out['v_ffn1_norm'], 'v_ffn1_w_in': out['v_ffn1_w_in'], 'v_ffn1_w_out': out['v_ffn1_w_out'], 'v_mix_norm': out['v_mix_norm'], 'v_w_in': out['v_w_in'], 'v_gdn_conv_w': out['v_gdn_conv_w'], 'v_gdn_a_log': out['v_gdn_a_log'], 'v_gdn_dt_bias': out['v_gdn_dt_bias'], 'v_gdn_out_norm': out['v_gdn_out_norm'], 'v_ret_out_norm': out['v_ret_out_norm'], 'v_w_branch_gdn': out['v_w_branch_gdn'], 'v_w_branch_ret': out['v_w_branch_ret'], 'v_w_out': out['v_w_out'], 'v_ffn2_norm': out['v_ffn2_norm'], 'v_ffn2_w_in': out['v_ffn2_w_in'], 'v_ffn2_w_out': out['v_ffn2_w_out'], 'v_final_norm': out['v_final_norm']}


def _loss(weights, diff, rest, loss_target):
    with _jax.named_scope("forward"):
        args = {**rest, TWIN_DIFF_INPUT: diff, **{k: w.astype(_WEIGHT_DTYPES[k]) for k, w in weights.items()}}
        y = _forward(args)
    with _jax.named_scope("loss_head"):
        err = _jnp.square(y.astype(_jnp.float32) - loss_target)
        return 0.5 * _jnp.sum(_jnp.mean(err, axis=-1)) if err.ndim else 0.5 * err


def _adamw(w, g, m, v):
    m = ADAM_B1 * m + (1.0 - ADAM_B1) * g
    v = ADAM_B2 * v + (1.0 - ADAM_B2) * _jnp.square(g)
    m_hat = m / (1.0 - ADAM_B1 ** ADAM_STEP)
    v_hat = v / (1.0 - ADAM_B2 ** ADAM_STEP)
    delta = -ADAM_LR * (m_hat / (_jnp.sqrt(v_hat) + ADAM_EPS) + ADAM_WD * w)
    return delta, m, v


def reference(x, meta_tokens, ffn1_norm, ffn1_w_in, ffn1_w_out, mix_norm, w_in, gdn_conv_w, gdn_a_log, gdn_dt_bias, gdn_out_norm, ret_out_norm, w_branch_gdn, w_branch_ret, w_out, ffn2_norm, ffn2_w_in, ffn2_w_out, final_norm, loss_target, m_meta_tokens, m_ffn1_norm, m_ffn1_w_in, m_ffn1_w_out, m_mix_norm, m_w_in, m_gdn_conv_w, m_gdn_a_log, m_gdn_dt_bias, m_gdn_out_norm, m_ret_out_norm, m_w_branch_gdn, m_w_branch_ret, m_w_out, m_ffn2_norm, m_ffn2_w_in, m_ffn2_w_out, m_final_norm, v_meta_tokens, v_ffn1_norm, v_ffn1_w_in, v_ffn1_w_out, v_mix_norm, v_w_in, v_gdn_conv_w, v_gdn_a_log, v_gdn_dt_bias, v_gdn_out_norm, v_ret_out_norm, v_w_branch_gdn, v_w_branch_ret, v_w_out, v_ffn2_norm, v_ffn2_w_in, v_ffn2_w_out, v_final_norm):
    given = dict(x=x, meta_tokens=meta_tokens, ffn1_norm=ffn1_norm, ffn1_w_in=ffn1_w_in, ffn1_w_out=ffn1_w_out, mix_norm=mix_norm, w_in=w_in, gdn_conv_w=gdn_conv_w, gdn_a_log=gdn_a_log, gdn_dt_bias=gdn_dt_bias, gdn_out_norm=gdn_out_norm, ret_out_norm=ret_out_norm, w_branch_gdn=w_branch_gdn, w_branch_ret=w_branch_ret, w_out=w_out, ffn2_norm=ffn2_norm, ffn2_w_in=ffn2_w_in, ffn2_w_out=ffn2_w_out, final_norm=final_norm, loss_target=loss_target, m_meta_tokens=m_meta_tokens, m_ffn1_norm=m_ffn1_norm, m_ffn1_w_in=m_ffn1_w_in, m_ffn1_w_out=m_ffn1_w_out, m_mix_norm=m_mix_norm, m_w_in=m_w_in, m_gdn_conv_w=m_gdn_conv_w, m_gdn_a_log=m_gdn_a_log, m_gdn_dt_bias=m_gdn_dt_bias, m_gdn_out_norm=m_gdn_out_norm, m_ret_out_norm=m_ret_out_norm, m_w_branch_gdn=m_w_branch_gdn, m_w_branch_ret=m_w_branch_ret, m_w_out=m_w_out, m_ffn2_norm=m_ffn2_norm, m_ffn2_w_in=m_ffn2_w_in, m_ffn2_w_out=m_ffn2_w_out, m_final_norm=m_final_norm, v_meta_tokens=v_meta_tokens, v_ffn1_norm=v_ffn1_norm, v_ffn1_w_in=v_ffn1_w_in, v_ffn1_w_out=v_ffn1_w_out, v_mix_norm=v_mix_norm, v_w_in=v_w_in, v_gdn_conv_w=v_gdn_conv_w, v_gdn_a_log=v_gdn_a_log, v_gdn_dt_bias=v_gdn_dt_bias, v_gdn_out_norm=v_gdn_out_norm, v_ret_out_norm=v_ret_out_norm, v_w_branch_gdn=v_w_branch_gdn, v_w_branch_ret=v_w_branch_ret, v_w_out=v_w_out, v_ffn2_norm=v_ffn2_norm, v_ffn2_w_in=v_ffn2_w_in, v_ffn2_w_out=v_ffn2_w_out, v_final_norm=v_final_norm)
    weights = {n: given[n] for n in TWIN_WEIGHTS}
    shared = {n: given[n] for n in SHARED_INPUTS}
    per_example = {n: given[n] for n in ['x']}
    grad_fn = _jax.value_and_grad(_loss, argnums=(0, 1))

    def one_microbatch(ex, loss_target):
        ex = dict(ex)
        diff = ex.pop(TWIN_DIFF_INPUT)
        return grad_fn(weights, diff, {**shared, **ex}, loss_target)

    if N_MICROBATCH == 1:
        loss, (grad_w, grad_x) = one_microbatch(per_example, given["loss_target"])
    else:
        def body(carry, xs):
            loss_sum, grad_sum = carry
            l_k, (gw_k, gx_k) = one_microbatch(xs[0], xs[1])
            with _jax.named_scope("update"):
                return (loss_sum + l_k, _jax.tree.map(_jnp.add, grad_sum, gw_k)), gx_k

        init = (_jnp.zeros((), _jnp.float32), _jax.tree.map(_jnp.zeros_like, weights))
        (loss, grad_w), grad_x = _jax.lax.scan(body, init, (per_example, given["loss_target"]))
    with _jax.named_scope("update"):
        delta_w, new_m, new_v = {}, {}, {}
        for n in TWIN_WEIGHTS:
            delta_w[n], new_m[n], new_v[n] = _adamw(weights[n], grad_w[n], given["m_" + n], given["v_" + n])
    return (loss, grad_x, *[grad_w[n] for n in TWIN_WEIGHTS], *[delta_w[n] for n in TWIN_WEIGHTS],
            *[new_m[n] for n in TWIN_WEIGHTS], *[new_v[n] for n in TWIN_WEIGHTS])
```

```python
import functools
import math

import jax
import jax.numpy as jnp
from jax import lax
from jax.experimental import pallas as pl
from jax.experimental.pallas import tpu as pltpu

F32 = jnp.float32
BF16 = jnp.bfloat16
HI = lax.Precision.HIGHEST
MESH = pl.DeviceIdType.MESH

D = 1024
N_META = 16
PAD = 48
HEAD_ROWS = PAD + N_META
CH = 64
H = 8
DH = 128
DFF = 2816
CONV_K = 4
EPS = 1e-6
ROPE_BASE = 10000.0
LANES = 128

OFF_QKV, OFF_Z, OFF_RQ, OFF_RK, OFF_RV, OFF_RG, OFF_GA, OFF_GB, OFF_BA = 0, 3072, 4096, 5120, 6144, 7168, 8192, 9216, 10240
PW = 10368
D_PROJ = 10256

ADAM_LR, ADAM_B1, ADAM_B2, ADAM_EPS, ADAM_WD, ADAM_STEP = 0.001, 0.9, 0.999, 1e-08, 0.01, 10

PACK = (("ffn1_w_in", 1408), ("ffn1_w_out", 704), ("w_in", 2564), ("gdn_conv_w", 3), ("w_branch_gdn", 256),
        ("w_branch_ret", 256), ("w_out", 256), ("ffn2_w_in", 1408), ("ffn2_w_out", 704), ("meta_tokens", 4))
PACK_ROWS = 7568
SMALL_ROWS = 8


def _pick(n, cap, mult):
    best = None
    for t in range(mult, min(n, cap) + 1, mult):
        if n % t == 0:
            best = t
    return best if best is not None else n


def _dot(a, b, dims, prec=None):
    return lax.dot_general(a, b, (dims, ((), ())), precision=prec, preferred_element_type=F32)


def _mm(a, b):
    return _dot(a.astype(BF16), b.astype(BF16), ((1,), (0,)))


def _mm_nt(a, b):
    return _dot(a.astype(BF16), b.astype(BF16), ((1,), (1,)))


def _mm_tn(a, b):
    return _dot(a.astype(BF16), b.astype(BF16), ((0,), (0,)))


def _mmh(a, b):
    return _dot(a, b, ((1,), (0,)), HI)


def matmul(name, a, b, mode, *, out_dtype=F32, resid=None, alpha=1.0, ti_cap=688, tj_cap=512, tr_cap=1408):
    if mode == "nn":
        (I, R), (R2, J) = a.shape, b.shape
    elif mode == "nt":
        (I, R), (J, R2) = a.shape, b.shape
    else:
        (R, I), (R2, J) = a.shape, b.shape
    assert R == R2, (name, a.shape, b.shape)
    ti = _pick(I, ti_cap, 16) if mode != "tn" else _pick(I, ti_cap, LANES)
    tj = _pick(J, tj_cap, LANES)
    tr = _pick(R, tr_cap, LANES) if mode != "tn" else _pick(R, tr_cap, 16)
    nr = R // tr
    if mode == "nn":
        a_spec = pl.BlockSpec((ti, tr), lambda i, j, r: (i, r))
        b_spec = pl.BlockSpec((tr, tj), lambda i, j, r: (r, j))
        dims = ((1,), (0,))
    elif mode == "nt":
        a_spec = pl.BlockSpec((ti, tr), lambda i, j, r: (i, r))
        b_spec = pl.BlockSpec((tj, tr), lambda i, j, r: (j, r))
        dims = ((1,), (1,))
    else:
        a_spec = pl.BlockSpec((tr, ti), lambda i, j, r: (r, i))
        b_spec = pl.BlockSpec((tr, tj), lambda i, j, r: (r, j))
        dims = ((0,), (0,))
    o_spec = pl.BlockSpec((ti, tj), lambda i, j, r: (i, j))
    has_resid = resid is not None

    def body(*refs):
        if has_resid:
            a_ref, b_ref, r_ref, o_ref, acc = refs
        else:
            a_ref, b_ref, o_ref, acc = refs
        r = pl.program_id(2)

        @pl.when(r == 0)
        def _():
            acc[...] = jnp.zeros_like(acc)

        acc[...] += _dot(a_ref[...].astype(BF16), b_ref[...].astype(BF16), dims)

        @pl.when(r == nr - 1)
        def _():
            res = acc[...] * alpha if alpha != 1.0 else acc[...]
            if has_resid:
                res = r_ref[...] + res
            o_ref[...] = res.astype(o_ref.dtype)

    ins = [a, b] + ([resid] if has_resid else [])
    specs = [a_spec, b_spec] + ([o_spec] if has_resid else [])
    return pl.pallas_call(
        body, name=name, grid=(I // ti, J // tj, nr), in_specs=specs, out_specs=o_spec,
        out_shape=jax.ShapeDtypeStruct((I, J), out_dtype),
        scratch_shapes=[pltpu.VMEM((ti, tj), F32)],
        compiler_params=pltpu.CompilerParams(dimension_semantics=("parallel", "parallel", "arbitrary")),
    )(*ins)


def rowwise(name, fn, rows, pars, outs, accs=(), *, n_rows, tm, reverse=False):
    nt = n_rows // tm
    assert nt * tm == n_rows
    pos = (lambda i: nt - 1 - i) if reverse else (lambda i: i)
    in_specs, ins = [], []
    for arr, w, cb in rows:
        in_specs.append(pl.BlockSpec((tm, w), lambda i, cb=cb: (pos(i), cb)))
        ins.append(arr)
    for p in pars:
        in_specs.append(pl.BlockSpec(p.shape, lambda i, nd=p.ndim: (0,) * nd))
        ins.append(p)
    n_in = len(ins)
    out_specs, out_shapes, aliases = [], [], {}
    for k, o in enumerate(outs):
        if o[0] == "new":
            _, w, dt = o
            out_specs.append(pl.BlockSpec((tm, w), lambda i: (pos(i), 0)))
            out_shapes.append(jax.ShapeDtypeStruct((n_rows, w), dt))
        else:
            _, arr, w, cb = o
            in_specs.append(pl.BlockSpec(memory_space=pl.ANY))
            aliases[len(ins)] = k
            ins.append(arr)
            out_specs.append(pl.BlockSpec((tm, w), lambda i, cb=cb: (pos(i), cb)))
            out_shapes.append(jax.ShapeDtypeStruct(arr.shape, arr.dtype))
    for r, w in accs:
        out_specs.append(pl.BlockSpec((r, w), lambda i: (0, 0)))
        out_shapes.append(jax.ShapeDtypeStruct((r, w), F32))
    n_all_in, n_out, n_acc = len(ins), len(outs), len(accs)

    def body(*refs):
        i = pl.program_id(0)
        vals = [r[...] for r in refs[:n_in]]
        res = fn(pos(i), *vals)
        if not isinstance(res, (tuple, list)):
            res = (res,)
        o_refs = refs[n_all_in:n_all_in + n_out]
        a_refs = refs[n_all_in + n_out:]
        for r, v in zip(o_refs, res[:n_out]):
            r[...] = v.astype(r.dtype)
        if n_acc:
            @pl.when(i == 0)
            def _():
                for r in a_refs:
                    r[...] = jnp.zeros_like(r)
            for r, v in zip(a_refs, res[n_out:]):
                r[...] += v

    return pl.pallas_call(
        body, name=name, grid=(nt,), in_specs=in_specs, out_specs=out_specs, out_shape=out_shapes,
        input_output_aliases=aliases,
        compiler_params=pltpu.CompilerParams(dimension_semantics=("arbitrary",)),
    )(*ins)


def _row_ids(i, tm):
    return i * tm + lax.broadcasted_iota(jnp.int32, (tm, 1), 0)


def _sigmoid(x):
    return 1.0 / (1.0 + jnp.exp(-x))


def _silu(x):
    return x * _sigmoid(x)


def _softplus(x):
    return jnp.maximum(x, 0.0) + jnp.log(1.0 + jnp.exp(-jnp.abs(x)))


def _rms(x, w):
    return x * lax.rsqrt(jnp.mean(x * x, axis=-1, keepdims=True) + EPS) * w


def _heads(fn, *xs):
    return jnp.concatenate([fn(h, *[x[:, h * DH:(h + 1) * DH] for x in xs]) for h in range(H)], axis=1)


def _swiglu(gu):
    return _silu(gu[:, :DFF]) * gu[:, DFF:]


def _gdn_post(o, z, w):
    return _heads(lambda h, oh, zh: oh * lax.rsqrt(jnp.mean(oh * oh, axis=-1, keepdims=True) + EPS) * w * _silu(zh), o, z)


def _ret_post(o, rg, w):
    def one(h, oh, gh, wh):
        mu = jnp.mean(oh, axis=-1, keepdims=True)
        xc = oh - mu
        var = jnp.mean(xc * xc, axis=-1, keepdims=True)
        return _silu(gh) * (xc * lax.rsqrt(var + EPS) * wh)
    return _heads(one, o, rg, jnp.broadcast_to(w, (o.shape[0], D)))


def _merge(a, b, ga, gb):
    return _sigmoid(ga) * a + _sigmoid(gb) * b


def _select_matrix(first_lane):
    r = lax.broadcasted_iota(jnp.int32, (LANES, H * DH), 0)
    c = lax.broadcasted_iota(jnp.int32, (LANES, H * DH), 1)
    return (r == first_lane + (c >> 7)).astype(F32)


def _gdn_gates(ba, alog_row, dtb_row, mask):
    g = -jnp.exp(alog_row) * _softplus(ba + dtb_row) * mask
    beta = _sigmoid(ba) * mask
    return _mmh(g, _select_matrix(H)), _mmh(beta, _select_matrix(0))


def _gdn_qkv(c):
    a = _silu(c)

    def l2(scale):
        return lambda h, t: t * lax.rsqrt(jnp.sum(t * t, axis=-1, keepdims=True) + EPS) * scale
    q = _heads(l2(DH ** -0.5), a[:, :D])
    k = _heads(l2(1.0), a[:, D:2 * D])
    return q, k, a[:, 2 * D:]


def _conv_taps(xw, cws, tm):
    return sum(cws[i] * xw[5 + i:5 + i + tm] for i in range(CONV_K))


def _swap_pairs(t):
    n = t.shape[1]
    lane = lax.broadcasted_iota(jnp.int32, t.shape, 1)
    return jnp.where((lane & 1) == 0, pltpu.roll(t, n - 1, 1), pltpu.roll(t, 1, 1))


def _tile_heads(t):
    return jnp.concatenate([t] * H, axis=1)


@jax.custom_vjp
def _unit_lower_inv(a):
    n = -a
    eye = (lax.broadcasted_iota(jnp.int32, (CH, CH), 0) == lax.broadcasted_iota(jnp.int32, (CH, CH), 1)).astype(F32)
    p = eye + n
    for _ in range(5):
        n = _mmh(n, n)
        p = p + _mmh(p, n)
    return p


def _inv_fwd(a):
    t = _unit_lower_inv(a)
    return t, t


def _inv_bwd(t, dt):
    x = _dot(t, dt, ((0,), (0,)), HI)
    return (-_dot(x, t, ((1,), (1,)), HI),)


_unit_lower_inv.defvjp(_inv_fwd, _inv_bwd)


def _gdn_chunk(q, k, v, gb, bb, s):
    ri = lax.broadcasted_iota(jnp.int32, (CH, CH), 0)
    ci = lax.broadcasted_iota(jnp.int32, (CH, CH), 1)
    causal = ri >= ci
    g1 = jnp.sum(gb, axis=1, keepdims=True) * (1.0 / LANES)
    gc_col = _mmh(causal.astype(F32), gb)
    gc_row = _mmh(jnp.ones((CH, CH), F32), jnp.where(ri <= ci, jnp.broadcast_to(g1, (CH, CH)), 0.0))
    gc1 = jnp.sum(gc_col, axis=1, keepdims=True) * (1.0 / LANES)
    diff = jnp.broadcast_to(gc1, (CH, CH)) - gc_row
    decay = jnp.where(causal, jnp.exp(jnp.where(causal, diff, 0.0)), 0.0)
    kb = k * bb
    a = jnp.where(ri > ci, _mm_nt(kb, k) * decay, 0.0)
    t = _unit_lower_inv(a)
    eg = jnp.exp(gc_col)
    u = _mm(t, v * bb)
    w = _mm(t, kb * eg)
    qk = _mm_nt(q, k) * decay
    g_last = gc_col[CH - 1:CH, :]
    v_new = u - _mm(w, s)
    o = _mm(q * eg, s) + _mm(qk, v_new)
    s_new = s * jnp.exp(g_last) + _mm_tn(k * jnp.exp(g_last - gc_col), v_new)
    return o, s_new


def _ret_chunk(q, k, v, s, decay, xi, zeta, cd):
    scores = _mm_nt(q, k) * decay
    o = _mm(scores, v) + _mm(q * xi, s)
    s_new = s * cd + _mm_tn(k * zeta, v)
    return o, s_new


def _scan_fwd(name, chunk_fn, blocks, consts, nc):
    n_blk, n_c = len(blocks), len(consts)

    def body(*refs):
        blk = refs[:n_blk]
        cst = refs[n_blk:n_blk + n_c]
        o_ref, st_ref, s_scr = refs[n_blk + n_c:]

        @pl.when(pl.program_id(1) == 0)
        def _():
            s_scr[...] = jnp.zeros_like(s_scr)

        s = s_scr[...]
        st_ref[0, 0] = s
        o, s_new = chunk_fn(*[r[...] for r in blk], s, *[r[0] for r in cst])
        o_ref[...] = o
        s_scr[...] = s_new

    in_specs = [pl.BlockSpec((CH, DH), lambda h, c, f=f: (c, f + h)) for _, f in blocks]
    in_specs += [pl.BlockSpec((1,) + a.shape[1:], lambda h, c: (h, 0, 0)) for a in consts]
    return pl.pallas_call(
        body, name=name, grid=(H, nc), in_specs=in_specs,
        out_specs=[pl.BlockSpec((CH, DH), lambda h, c: (c, h)), pl.BlockSpec((1, 1, DH, DH), lambda h, c: (h, c, 0, 0))],
        out_shape=[jax.ShapeDtypeStruct((nc * CH, H * DH), F32), jax.ShapeDtypeStruct((H, nc, DH, DH), F32)],
        scratch_shapes=[pltpu.VMEM((DH, DH), F32)],
        compiler_params=pltpu.CompilerParams(dimension_semantics=("parallel", "arbitrary")),
    )(*[a for a, _ in blocks], *consts)


def _scan_bwd(name, chunk_fn, blocks, consts, states, do, outs, nc):
    n_blk, n_c, n_out = len(blocks), len(consts), len(outs)

    def body(*refs):
        blk = refs[:n_blk]
        cst = refs[n_blk:n_blk + n_c]
        st_ref, do_ref = refs[n_blk + n_c:n_blk + n_c + 2]
        n_in = n_blk + n_c + 2 + sum(1 for o in outs if o[0] == "into")
        o_refs = refs[n_in:n_in + n_out]
        ds_scr = refs[n_in + n_out]

        @pl.when(pl.program_id(1) == 0)
        def _():
            ds_scr[...] = jnp.zeros_like(ds_scr)

        cv = [r[0] for r in cst]
        _, vjp = jax.vjp(lambda *a: chunk_fn(*a, *cv), *[r[...] for r in blk], st_ref[0, 0])
        grads = vjp((do_ref[...], ds_scr[...]))
        for r, g in zip(o_refs, grads[:n_blk]):
            r[...] = g
        ds_scr[...] = grads[n_blk]

    rc = lambda c: nc - 1 - c
    in_specs = [pl.BlockSpec((CH, DH), lambda h, c, f=f: (rc(c), f + h)) for _, f in blocks]
    in_specs += [pl.BlockSpec((1,) + a.shape[1:], lambda h, c: (h, 0, 0)) for a in consts]
    in_specs += [pl.BlockSpec((1, 1, DH, DH), lambda h, c: (h, rc(c), 0, 0)), pl.BlockSpec((CH, DH), lambda h, c: (rc(c), h))]
    ins = [a for a, _ in blocks] + list(consts) + [states, do]
    out_specs, out_shapes, aliases = [], [], {}
    for k, o in enumerate(outs):
        if o[0] == "new":
            out_specs.append(pl.BlockSpec((CH, DH), lambda h, c: (rc(c), h)))
            out_shapes.append(jax.ShapeDtypeStruct((nc * CH, H * DH), F32))
        else:
            _, arr, f = o
            in_specs.append(pl.BlockSpec(memory_space=pl.ANY))
            aliases[len(ins)] = k
            ins.append(arr)
            out_specs.append(pl.BlockSpec((CH, DH), lambda h, c, f=f: (rc(c), f + h)))
            out_shapes.append(jax.ShapeDtypeStruct(arr.shape, arr.dtype))
    return pl.pallas_call(
        body, name=name, grid=(H, nc), in_specs=in_specs, out_specs=out_specs, out_shape=out_shapes,
        input_output_aliases=aliases, scratch_shapes=[pltpu.VMEM((DH, DH), F32)],
        compiler_params=pltpu.CompilerParams(dimension_semantics=("parallel", "arbitrary")),
    )(*ins)


def _gdn_pre_specs(p, cws, alog_row, dtb_row, tm, pos):
    sub = tm // 8
    rows = [pl.BlockSpec((tm, 3 * D), lambda i: (pos(i), 0)),
            pl.BlockSpec((8, 3 * D), lambda i: (jnp.maximum(pos(i) * sub - 1, 0), 0)),
            pl.BlockSpec((tm, LANES), lambda i: (pos(i), OFF_BA // LANES))]
    pars = [pl.BlockSpec(a.shape, lambda i: (0, 0)) for a in (*cws, alog_row, dtb_row)]
    return rows + pars, [p, p, p, *cws, alog_row, dtb_row]


def gdn_pre_fwd(p, cws, alog_row, dtb_row, lp, tm):
    def body(x_ref, prev_ref, ba_ref, c0, c1, c2, c3, al_ref, dt_ref, q_ref, k_ref, v_ref, g_ref, b_ref):
        i = pl.program_id(0)
        prev = jnp.where(i > 0, prev_ref[...], 0.0)
        xw = jnp.concatenate([prev, x_ref[...]], axis=0)
        c = _conv_taps(xw, [c0[...], c1[...], c2[...], c3[...]], tm)
        q, k, v = _gdn_qkv(c)
        mask = (_row_ids(i, tm) >= PAD).astype(F32)
        g, b = _gdn_gates(ba_ref[...], al_ref[...], dt_ref[...], mask)
        q_ref[...] = q
        k_ref[...] = k
        v_ref[...] = v
        g_ref[...] = g
        b_ref[...] = b

    in_specs, ins = _gdn_pre_specs(p, cws, alog_row, dtb_row, tm, lambda i: i)
    o_spec = pl.BlockSpec((tm, D), lambda i: (i, 0))
    return pl.pallas_call(
        body, name="gdn_pre_fwd", grid=(lp // tm,), in_specs=in_specs, out_specs=[o_spec] * 5,
        out_shape=[jax.ShapeDtypeStruct((lp, D), F32)] * 5,
        compiler_params=pltpu.CompilerParams(dimension_semantics=("parallel",)),
    )(*ins)


def gdn_pre_bwd(p, cws, alog_row, dtb_row, dq, dk, dv, dg, db, dp, lp, tm):
    nt = lp // tm
    pos = lambda i: nt - 1 - i

    def body(x_ref, prev_ref, ba_ref, c0, c1, c2, c3, al_ref, dt_ref, dq_ref, dk_ref, dv_ref, dg_ref, db_ref, dp_any,
             dx_ref, dba_ref, dcw_ref, dpar_ref, carry):
        i = pl.program_id(0)
        t = pos(i)

        @pl.when(i == 0)
        def _():
            carry[...] = jnp.zeros_like(carry)
            dcw_ref[...] = jnp.zeros_like(dcw_ref)
            dpar_ref[...] = jnp.zeros_like(dpar_ref)

        cws_v = [c0[...], c1[...], c2[...], c3[...]]
        prev = jnp.where(t > 0, prev_ref[...], 0.0)
        xw = jnp.concatenate([prev, x_ref[...]], axis=0)
        c = _conv_taps(xw, cws_v, tm)
        _, vjp_qkv = jax.vjp(_gdn_qkv, c)
        (dc,) = vjp_qkv((dq_ref[...], dk_ref[...], dv_ref[...]))
        zeros8 = jnp.zeros((8, 3 * D), F32)
        dcp = jnp.concatenate([zeros8, dc, zeros8], axis=0)
        dxw = sum(cws_v[j] * dcp[3 - j:3 - j + tm + 8] for j in range(CONV_K))
        dx_ref[...] = dxw[8:]
        dx_ref[tm - 8:tm, :] += carry[...]
        carry[...] = dxw[:8]
        for j in range(CONV_K):
            dcw_ref[j:j + 1, :] += jnp.sum(dc * xw[5 + j:5 + j + tm], axis=0, keepdims=True)
        mask = (_row_ids(t, tm) >= PAD).astype(F32)
        _, vjp_g = jax.vjp(lambda ba, al, dt: _gdn_gates(ba, al, dt, mask), ba_ref[...], al_ref[...], dt_ref[...])
        dba, dal, ddt = vjp_g((dg_ref[...], db_ref[...]))
        dba_ref[...] = dba
        dpar_ref[0:1, :] += dal
        dpar_ref[1:2, :] += ddt

    in_specs, ins = _gdn_pre_specs(p, cws, alog_row, dtb_row, tm, pos)
    g_spec = pl.BlockSpec((tm, D), lambda i: (pos(i), 0))
    in_specs += [g_spec] * 5 + [pl.BlockSpec(memory_space=pl.ANY)]
    ins += [dq, dk, dv, dg, db, dp]
    return pl.pallas_call(
        body, name="gdn_pre_bwd", grid=(nt,), in_specs=in_specs,
        out_specs=[pl.BlockSpec((tm, 3 * D), lambda i: (pos(i), 0)), pl.BlockSpec((tm, LANES), lambda i: (pos(i), 0)),
                   pl.BlockSpec((8, 3 * D), lambda i: (0, 0)), pl.BlockSpec((8, LANES), lambda i: (0, 0))],
        out_shape=[jax.ShapeDtypeStruct(dp.shape, F32), jax.ShapeDtypeStruct((lp, LANES), F32),
                   jax.ShapeDtypeStruct((8, 3 * D), F32), jax.ShapeDtypeStruct((8, LANES), F32)],
        input_output_aliases={len(ins) - 1: 0},
        scratch_shapes=[pltpu.VMEM((8, 3 * D), F32)],
        compiler_params=pltpu.CompilerParams(dimension_semantics=("arbitrary",)),
    )(*ins)


def _me():
    return lax.axis_index("x"), lax.axis_index("y"), lax.axis_index("c")


def allgather_chips(name, w):
    def body(w_ref, out_ref, send_sems, recv_sems, local_sem):
        x, y, c = _me()
        chips = [(1 - x, y), (x, 1 - y), (1 - x, 1 - y)]
        mine = pltpu.make_async_copy(w_ref, out_ref.at[2 * x + y], local_sem)
        mine.start()

        def copy(j, slot):
            px, py = chips[j]
            return pltpu.make_async_remote_copy(src_ref=w_ref, dst_ref=out_ref.at[slot], send_sem=send_sems.at[j],
                                                recv_sem=recv_sems.at[j], device_id=(px, py, c), device_id_type=MESH)
        sends = [copy(j, 2 * x + y) for j in range(3)]
        for cp in sends:
            cp.start()
        for j, (px, py) in enumerate(chips):
            copy(j, 2 * px + py).wait_recv()
        for cp in sends:
            cp.wait_send()
        mine.wait()

    return pl.pallas_call(
        body, name=name, out_shape=jax.ShapeDtypeStruct((4,) + w.shape, w.dtype),
        in_specs=[pl.BlockSpec(memory_space=pl.ANY)], out_specs=pl.BlockSpec(memory_space=pl.ANY),
        scratch_shapes=[pltpu.SemaphoreType.DMA((3,)), pltpu.SemaphoreType.DMA((3,)), pltpu.SemaphoreType.DMA(())],
    )(w)


def sibling_halves(name, g):
    def body(g_ref, out_ref, send_sems, recv_sems):
        x, y, c = _me()

        def copy(q):
            return pltpu.make_async_remote_copy(src_ref=g_ref.at[q, 1 - c], dst_ref=out_ref.at[q], send_sem=send_sems.at[q],
                                                recv_sem=recv_sems.at[q], device_id=(x, y, 1 - c), device_id_type=MESH)
        cps = [copy(q) for q in range(4)]
        for cp in cps:
            cp.start()
        for cp in cps:
            cp.wait_recv()
        for cp in cps:
            cp.wait_send()

    return pl.pallas_call(
        body, name=name, out_shape=jax.ShapeDtypeStruct((4,) + g.shape[2:], g.dtype),
        in_specs=[pl.BlockSpec(memory_space=pl.ANY)], out_specs=pl.BlockSpec(memory_space=pl.ANY),
        scratch_shapes=[pltpu.SemaphoreType.DMA((4,)), pltpu.SemaphoreType.DMA((4,))],
    )(g)


def scatter_chips(name, cs):
    def body(cs_ref, out_ref, send_sems, recv_sems):
        x, y, c = _me()
        chips = [(1 - x, y), (x, 1 - y), (1 - x, 1 - y)]

        def copy(j):
            px, py = chips[j]
            return pltpu.make_async_remote_copy(src_ref=cs_ref.at[2 * px + py], dst_ref=out_ref.at[j], send_sem=send_sems.at[j],
                                                recv_sem=recv_sems.at[j], device_id=(px, py, c), device_id_type=MESH)
        cps = [copy(j) for j in range(3)]
        for cp in cps:
            cp.start()
        for cp in cps:
            cp.wait_recv()
        for cp in cps:
            cp.wait_send()

    return pl.pallas_call(
        body, name=name, out_shape=jax.ShapeDtypeStruct((3,) + cs.shape[1:], cs.dtype),
        in_specs=[pl.BlockSpec(memory_space=pl.ANY)], out_specs=pl.BlockSpec(memory_space=pl.ANY),
        scratch_shapes=[pltpu.SemaphoreType.DMA((3,)), pltpu.SemaphoreType.DMA((3,))],
    )(cs)


def sibling_join(name, half):
    def body(h_ref, out_ref, send_sem, recv_sem, local_sem):
        x, y, c = _me()
        mine = pltpu.make_async_copy(h_ref, out_ref.at[c], local_sem)
        mine.start()
        cp = pltpu.make_async_remote_copy(src_ref=h_ref, dst_ref=out_ref.at[c], send_sem=send_sem, recv_sem=recv_sem,
                                          device_id=(x, y, 1 - c), device_id_type=MESH)
        cp.start()
        pltpu.make_async_remote_copy(src_ref=h_ref, dst_ref=out_ref.at[1 - c], send_sem=send_sem, recv_sem=recv_sem,
                                     device_id=(x, y, 1 - c), device_id_type=MESH).wait_recv()
        cp.wait_send()
        mine.wait()

    return pl.pallas_call(
        body, name=name, out_shape=jax.ShapeDtypeStruct((2,) + half.shape, half.dtype),
        in_specs=[pl.BlockSpec(memory_space=pl.ANY)], out_specs=pl.BlockSpec(memory_space=pl.ANY),
        scratch_shapes=[pltpu.SemaphoreType.DMA(()), pltpu.SemaphoreType.DMA(()), pltpu.SemaphoreType.DMA(())],
    )(half)


def allgather_all(name, s):
    def body(s_ref, out_ref, send_sems, recv_sems, local_sem):
        x, y, c = _me()
        peers = [(x ^ ((m >> 2) & 1), y ^ ((m >> 1) & 1), c ^ (m & 1)) for m in range(1, 8)]
        mine = pltpu.make_async_copy(s_ref, out_ref.at[4 * x + 2 * y + c], local_sem)
        mine.start()

        def copy(j, slot):
            return pltpu.make_async_remote_copy(src_ref=s_ref, dst_ref=out_ref.at[slot], send_sem=send_sems.at[j],
                                                recv_sem=recv_sems.at[j], device_id=peers[j], device_id_type=MESH)
        sends = [copy(j, 4 * x + 2 * y + c) for j in range(7)]
        for cp in sends:
            cp.start()
        for j, (px, py, pc) in enumerate(peers):
            copy(j, 4 * px + 2 * py + pc).wait_recv()
        for cp in sends:
            cp.wait_send()
        mine.wait()

    return pl.pallas_call(
        body, name=name, out_shape=jax.ShapeDtypeStruct((8,) + s.shape, s.dtype),
        in_specs=[pl.BlockSpec(memory_space=pl.ANY)], out_specs=pl.BlockSpec(memory_space=pl.ANY),
        scratch_shapes=[pltpu.SemaphoreType.DMA((7,)), pltpu.SemaphoreType.DMA((7,)), pltpu.SemaphoreType.DMA(())],
    )(s)


def sum_slots(name, parts, tm):
    n = len(parts)
    R, W = parts[0][0].shape[1:]
    idx = jnp.stack([jnp.asarray(s, jnp.int32) for _, s in parts])

    def body(idx_ref, *refs):
        acc = refs[0][0].astype(F32)
        for r in refs[1:n]:
            acc = acc + r[0].astype(F32)
        refs[n][...] = acc

    grid_spec = pltpu.PrefetchScalarGridSpec(
        num_scalar_prefetch=1, grid=(R // tm,),
        in_specs=[pl.BlockSpec((1, tm, W), lambda i, idx, k=k: (idx[k], i, 0)) for k in range(n)],
        out_specs=pl.BlockSpec((tm, W), lambda i, idx: (i, 0)))
    return pl.pallas_call(body, name=name, grid_spec=grid_spec, out_shape=jax.ShapeDtypeStruct((R, W), F32),
                          compiler_params=pltpu.CompilerParams(dimension_semantics=("parallel",)))(idx, *[a for a, _ in parts])


def adamw(name, w, g, m, v, tm):
    def fn(i, w, g, m, v):
        m = ADAM_B1 * m + (1.0 - ADAM_B1) * g
        v = ADAM_B2 * v + (1.0 - ADAM_B2) * (g * g)
        m_hat = m / (1.0 - ADAM_B1 ** ADAM_STEP)
        v_hat = v / (1.0 - ADAM_B2 ** ADAM_STEP)
        return -ADAM_LR * (m_hat / (jnp.sqrt(v_hat) + ADAM_EPS) + ADAM_WD * w), m, v
    W = w.shape[1]
    return rowwise(name, fn, [(a, W, 0) for a in (w, g, m, v)], [], [("new", W, F32)] * 3, n_rows=w.shape[0], tm=tm)


def pack_shards(shards, dtype):
    rows = [shards[n].reshape(-1, D).astype(dtype) for n, _ in PACK]
    n = sum(r.shape[0] for r in rows)
    return jnp.concatenate(rows + [jnp.zeros((PACK_ROWS - n, D), dtype)], axis=0)


def unpack_shards(packed, shapes):
    out, r0 = {}, 0
    for n, r in PACK:
        out[n] = packed[r0:r0 + r].reshape(shapes[n])
        r0 += r
    return out


SHARD_AXIS = {"ffn1_w_in": 1, "ffn1_w_out": 0, "w_in": 1, "gdn_conv_w": 1, "w_branch_gdn": 0, "w_branch_ret": 0,
              "w_out": 0, "ffn2_w_in": 1, "ffn2_w_out": 0, "meta_tokens": 1}


def full_from_gathered(gathered, shard_shapes):
    out, r0 = {}, 0
    for n, r in PACK:
        shp = shard_shapes[n][-2:]
        blocks = gathered[:, r0:r0 + r].reshape((4,) + shp)
        out[n] = jnp.concatenate([blocks[q] for q in range(4)], axis=SHARD_AXIS[n])
        r0 += r
    return out


def shards_from_full(grads):
    rows = []
    for n, r in PACK:
        g = grads[n]
        parts = jnp.split(g, 4, axis=SHARD_AXIS[n])
        rows.append(jnp.stack([p.reshape(-1, D) for p in parts]))
    n = sum(r.shape[1] for r in rows)
    return jnp.concatenate(rows + [jnp.zeros((4, PACK_ROWS - n, D), F32)], axis=1)


def _w_in_padded(w):
    return jnp.concatenate([w[:, :4096], w[:, 4112:], w[:, 4096:4112], jnp.zeros((D, PW - D_PROJ), w.dtype)], axis=1)


def _w_in_unpadded(g):
    return jnp.concatenate([g[:, :4096], g[:, OFF_BA:OFF_BA + 16], g[:, 4096:OFF_BA]], axis=1)


def _ret_consts():
    f = F32
    log_gamma = jnp.log1p(-jnp.exp2(-5.0 - jnp.arange(H, dtype=f)))
    pos = jnp.arange(CH, dtype=f)
    causal = jnp.tril(jnp.ones((CH, CH), dtype=bool))
    diff = pos[:, None] - pos[None, :]
    decay = jnp.where(causal, jnp.exp(jnp.where(causal, diff, 0.0) * log_gamma[:, None, None]), 0.0)
    xi = jnp.broadcast_to(jnp.exp((pos + 1.0) * log_gamma[:, None])[:, :, None], (H, CH, DH))
    zeta = jnp.broadcast_to(jnp.exp((CH - 1.0 - pos) * log_gamma[:, None])[:, :, None], (H, CH, DH))
    cd = jnp.broadcast_to(jnp.exp(CH * log_gamma)[:, None, None], (H, 1, DH))
    return decay, xi, zeta, cd


def _rope_tables(lp):
    pos = jnp.arange(lp, dtype=F32) - float(PAD)
    inv = 1.0 / (ROPE_BASE ** jnp.linspace(0.0, 1.0, DH // 2, dtype=F32))
    ang = pos[:, None] * inv[None, :]
    cos, sin = jnp.cos(ang), jnp.sin(ang)
    ct = jnp.repeat(cos, 2, axis=1)
    st = jnp.stack([-sin, sin], axis=-1).reshape(lp, DH)
    return ct, st


def local_step(x, tgt, w, small):
    seq = x.shape[0]
    lp = HEAD_ROWS + seq
    nc = lp // CH
    tm = _pick(lp, 192, 16)
    row = functools.partial(rowwise, n_rows=lp, tm=tm)

    h0 = jnp.concatenate([jnp.zeros((PAD, D), F32), w["meta_tokens"].astype(F32), x], axis=0)
    tgt_p = jnp.concatenate([jnp.zeros((HEAD_ROWS, D), F32), tgt], axis=0)

    def ffn_fwd(tag, h, wn, w_in, w_out):
        n = row(f"{tag}_norm", lambda i, h, wn: _rms(h, wn), [(h, D, 0)], [wn], [("new", D, BF16)])[0]
        gu = matmul(f"{tag}_up", n, w_in, "nn")
        mid = row(f"{tag}_act", lambda i, gu: _swiglu(gu), [(gu, 2 * DFF, 0)], [], [("new", DFF, BF16)])[0]
        out = matmul(f"{tag}_down", mid, w_out, "nn", resid=h, alpha=0.5)
        return out, (h, n, gu, mid)

    def ffn_bwd(tag, dh, saved, wn, w_in, w_out):
        h, n, gu, mid = saved
        dmid = matmul(f"{tag}_dmid", dh, w_out, "nt", alpha=0.5)
        dw_out = matmul(f"{tag}_dwout", mid, dh, "tn", alpha=0.5, ti_cap=1408)

        def act_bwd(i, gu, dmid):
            _, vjp = jax.vjp(_swiglu, gu)
            return vjp(dmid)[0]
        dgu = row(f"{tag}_dact", act_bwd, [(gu, 2 * DFF, 0), (dmid, DFF, 0)], [], [("new", 2 * DFF, F32)])[0]
        dn = matmul(f"{tag}_dn", dgu, w_in, "nt", tj_cap=1024)
        dw_in = matmul(f"{tag}_dwin", n, dgu, "tn", ti_cap=1024)
        dh_in, dwn = norm_bwd(f"{tag}_dnorm", h, wn, dn, dh)
        return dh_in, dw_in, dw_out, dwn

    def norm_bwd(name, h, wn, dn, dres):
        def fn(i, h, dn, dres, wn):
            _, vjp = jax.vjp(_rms, h, wn)
            dh, dw = vjp(dn)
            return dres + dh, dw
        return row(name, fn, [(h, D, 0), (dn, D, 0), (dres, D, 0)], [wn], [("new", D, F32)], [(1, D)])

    h1, ffn1_saved = ffn_fwd("ffn1", h0, small["ffn1_norm"], w["ffn1_w_in"], w["ffn1_w_out"])
    n2 = row("mix_norm", lambda i, h, wn: _rms(h, wn), [(h1, D, 0)], [small["mix_norm"]], [("new", D, BF16)])[0]
    w_in_p = _w_in_padded(w["w_in"])
    p = matmul("mix_proj", n2, w_in_p, "nn", tj_cap=384)

    cws = [w["gdn_conv_w"][j:j + 1].astype(F32) for j in range(CONV_K)]
    alog_row = jnp.zeros((1, LANES), F32).at[:, H:2 * H].set(small["gdn_a_log"])
    dtb_row = jnp.zeros((1, LANES), F32).at[:, H:2 * H].set(small["gdn_dt_bias"])
    q, k, v, gb, bb = gdn_pre_fwd(p, cws, alog_row, dtb_row, lp, tm)
    gdn_blocks = [(q, 0), (k, 0), (v, 0), (gb, 0), (bb, 0)]
    o_a, gdn_states = _scan_fwd("gdn_scan_fwd", _gdn_chunk, gdn_blocks, [], nc)
    y_a = row("gdn_post", lambda i, o, z, wn: _gdn_post(o, z, wn), [(o_a, D, 0), (p, D, OFF_Z // D)], [small["gdn_out_norm"]],
              [("new", D, BF16)])[0]

    ct, st = _rope_tables(lp)

    def rope_fwd(i, rq, rk, ct, st):
        c8, s8 = _tile_heads(ct), _tile_heads(st)
        return rq * c8 + _swap_pairs(rq) * s8, (rk * c8 + _swap_pairs(rk) * s8) * (DH ** -0.5)
    rq, rk = row("ret_pre", rope_fwd, [(p, D, OFF_RQ // D), (p, D, OFF_RK // D), (ct, DH, 0), (st, DH, 0)], [],
                 [("new", D, F32)] * 2)
    ret_consts = list(_ret_consts())
    ret_blocks = [(rq, 0), (rk, 0), (p, OFF_RV // DH)]
    ret_fn = lambda q, k, v, s, *cst: _ret_chunk(q, k, v, s, *cst)
    o_b, ret_states = _scan_fwd("ret_scan_fwd", ret_fn, ret_blocks, ret_consts, nc)
    y_b = row("ret_post", lambda i, o, g, wn: _ret_post(o, g, wn), [(o_b, D, 0), (p, D, OFF_RG // D)], [small["ret_out_norm"]],
              [("new", D, BF16)])[0]

    br_a = matmul("branch_gdn", y_a, w["w_branch_gdn"], "nn")
    br_b = matmul("branch_ret", y_b, w["w_branch_ret"], "nn")
    merged = row("merge", lambda i, a, b, ga, gb_: _merge(a, b, ga, gb_),
                 [(br_a, D, 0), (br_b, D, 0), (p, D, OFF_GA // D), (p, D, OFF_GB // D)], [], [("new", D, BF16)])[0]
    h2 = matmul("mix_out", merged, w["w_out"], "nn", resid=h1)
    h3, ffn2_saved = ffn_fwd("ffn2", h2, small["ffn2_norm"], w["ffn2_w_in"], w["ffn2_w_out"])

    def head(i, h, t, wn):
        mask = (_row_ids(i, tm) >= HEAD_ROWS).astype(F32)
        y, vjp = jax.vjp(_rms, h, wn)
        err = (y - t) * mask
        dh, dw = vjp(err * (1.0 / D))
        return dh, dw, jnp.sum(err * err, keepdims=True).reshape(1, 1) * (0.5 / D) * jnp.ones((1, LANES), F32)
    dh3, d_final, loss_row = row("loss_head", head, [(h3, D, 0), (tgt_p, D, 0)], [small["final_norm"]], [("new", D, F32)],
                                 [(1, D), (1, LANES)])

    gw, gs = {}, {"final_norm": d_final}
    dh2, gw["ffn2_w_in"], gw["ffn2_w_out"], gs["ffn2_norm"] = ffn_bwd("ffn2", dh3, ffn2_saved, small["ffn2_norm"],
                                                                      w["ffn2_w_in"], w["ffn2_w_out"])
    dmerged = matmul("mix_out_dx", dh2, w["w_out"], "nt", tj_cap=1024)
    gw["w_out"] = matmul("mix_out_dw", merged, dh2, "tn", ti_cap=1024)
    dp = lax.empty((lp, PW), F32)

    def merge_bwd(i, dm, a, b, ga, gb_):
        _, vjp = jax.vjp(_merge, a, b, ga, gb_)
        da, db, dga, dgb = vjp(dm)
        return da, db, jnp.concatenate([dga, dgb], axis=1)
    da, db_, dp = row("merge_bwd", merge_bwd,
                      [(dmerged, D, 0), (br_a, D, 0), (br_b, D, 0), (p, D, OFF_GA // D), (p, D, OFF_GB // D)], [],
                      [("new", D, F32), ("new", D, F32), ("into", dp, 2 * D, OFF_GA // (2 * D))])
    dy_a = matmul("branch_gdn_dx", da, w["w_branch_gdn"], "nt", tj_cap=1024)
    gw["w_branch_gdn"] = matmul("branch_gdn_dw", y_a, da, "tn", ti_cap=1024)
    dy_b = matmul("branch_ret_dx", db_, w["w_branch_ret"], "nt", tj_cap=1024)
    gw["w_branch_ret"] = matmul("branch_ret_dw", y_b, db_, "tn", ti_cap=1024)

    def gdn_post_bwd(i, o, z, dy, wn):
        _, vjp = jax.vjp(_gdn_post, o, z, wn)
        return vjp(dy)
    do_a, dp, gs["gdn_out_norm"] = row("gdn_post_bwd", gdn_post_bwd, [(o_a, D, 0), (p, D, OFF_Z // D), (dy_a, D, 0)],
                                       [small["gdn_out_norm"]], [("new", D, F32), ("into", dp, D, OFF_Z // D)], [(1, DH)])

    def ret_post_bwd(i, o, g, dy, wn):
        _, vjp = jax.vjp(_ret_post, o, g, wn)
        return vjp(dy)
    do_b, dp, gs["ret_out_norm"] = row("ret_post_bwd", ret_post_bwd, [(o_b, D, 0), (p, D, OFF_RG // D), (dy_b, D, 0)],
                                       [small["ret_out_norm"]], [("new", D, F32), ("into", dp, D, OFF_RG // D)], [(1, D)])

    drq, drk, dp = _scan_bwd("ret_scan_bwd", ret_fn, ret_blocks, ret_consts, ret_states, do_b,
                             [("new",), ("new",), ("into", dp, OFF_RV // DH)], nc)

    def rope_bwd(i, drq, drk, ct, st):
        c8, s8 = _tile_heads(ct), _tile_heads(st)
        drk = drk * (DH ** -0.5)
        return jnp.concatenate([drq * c8 + _swap_pairs(drq * s8), drk * c8 + _swap_pairs(drk * s8)], axis=1)
    dp = row("ret_pre_bwd", rope_bwd, [(drq, D, 0), (drk, D, 0), (ct, DH, 0), (st, DH, 0)], [],
             [("into", dp, 2 * D, OFF_RQ // (2 * D))])[0]

    dq, dk, dv, dgb, dbb = _scan_bwd("gdn_scan_bwd", _gdn_chunk, gdn_blocks, [], gdn_states, do_a, [("new",)] * 5, nc)
    dp, dba, dcw, dgate = gdn_pre_bwd(p, cws, alog_row, dtb_row, dq, dk, dv, dgb, dbb, dp, lp, tm)
    dp = row("dp_ba", lambda i, t: t, [(dba, LANES, 0)], [], [("into", dp, LANES, OFF_BA // LANES)])[0]
    gw["gdn_conv_w"] = dcw[:CONV_K]
    gs["gdn_a_log"] = dgate[0:1, H:2 * H]
    gs["gdn_dt_bias"] = dgate[1:2, H:2 * H]

    dn2 = matmul("mix_proj_dx", dp, w_in_p, "nt", tj_cap=1024, tr_cap=1152)
    gw["w_in"] = _w_in_unpadded(matmul("mix_proj_dw", n2, dp, "tn", ti_cap=1024, tj_cap=384))
    dh1, gs["mix_norm"] = norm_bwd("mix_dnorm", h1, small["mix_norm"], dn2, dh2)
    dh0, gw["ffn1_w_in"], gw["ffn1_w_out"], gs["ffn1_norm"] = ffn_bwd("ffn1", dh1, ffn1_saved, small["ffn1_norm"],
                                                                      w["ffn1_w_in"], w["ffn1_w_out"])
    gw["meta_tokens"] = dh0[PAD:HEAD_ROWS]
    return loss_row, dh0[HEAD_ROWS:], gw, gs


SHARDED = tuple(n for n, _ in PACK)
SMALL = ("ffn1_norm", "mix_norm", "ret_out_norm", "ffn2_norm", "final_norm", "gdn_out_norm", "gdn_a_log", "gdn_dt_bias")
WEIGHTS = ("meta_tokens", "ffn1_norm", "ffn1_w_in", "ffn1_w_out", "mix_norm", "w_in", "gdn_conv_w", "gdn_a_log", "gdn_dt_bias",
           "gdn_out_norm", "ret_out_norm", "w_branch_gdn", "w_branch_ret", "w_out", "ffn2_norm", "ffn2_w_in", "ffn2_w_out",
           "final_norm")
LOSS_ROW = 6


def pack_small(vals):
    rows = [vals[n].reshape(1, D) for n in SMALL[:5]]
    r5 = jnp.concatenate([vals["gdn_out_norm"].reshape(1, DH), vals["gdn_a_log"].reshape(1, H), vals["gdn_dt_bias"].reshape(1, H),
                          jnp.zeros((1, D - DH - 2 * H), F32)], axis=1)
    return jnp.concatenate(rows + [r5, jnp.zeros((2, D), F32)], axis=0)


def unpack_small(packed, shapes):
    out = {n: packed[j].reshape(shapes[n]) for j, n in enumerate(SMALL[:5])}
    out["gdn_out_norm"] = packed[5, :DH].reshape(shapes["gdn_out_norm"])
    out["gdn_a_log"] = packed[5, DH:DH + H].reshape(shapes["gdn_a_log"])
    out["gdn_dt_bias"] = packed[5, DH + H:DH + 2 * H].reshape(shapes["gdn_dt_bias"])
    return out


def kernel(x, meta_tokens, ffn1_norm, ffn1_w_in, ffn1_w_out, mix_norm, w_in, gdn_conv_w, gdn_a_log, gdn_dt_bias, gdn_out_norm, ret_out_norm, w_branch_gdn, w_branch_ret, w_out, ffn2_norm, ffn2_w_in, ffn2_w_out, final_norm, loss_target, m_meta_tokens, m_ffn1_norm, m_ffn1_w_in, m_ffn1_w_out, m_mix_norm, m_w_in, m_gdn_conv_w, m_gdn_a_log, m_gdn_dt_bias, m_gdn_out_norm, m_ret_out_norm, m_w_branch_gdn, m_w_branch_ret, m_w_out, m_ffn2_norm, m_ffn2_w_in, m_ffn2_w_out, m_final_norm, v_meta_tokens, v_ffn1_norm, v_ffn1_w_in, v_ffn1_w_out, v_mix_norm, v_w_in, v_gdn_conv_w, v_gdn_a_log, v_gdn_dt_bias, v_gdn_out_norm, v_ret_out_norm, v_w_branch_gdn, v_w_branch_ret, v_w_out, v_ffn2_norm, v_ffn2_w_in, v_ffn2_w_out, v_final_norm):
    a = dict(locals())
    wts = {n: a[n] for n in WEIGHTS}
    mom_m = {n: a["m_" + n] for n in WEIGHTS}
    mom_v = {n: a["v_" + n] for n in WEIGHTS}
    shapes = {n: wts[n].shape for n in WEIGHTS}
    c = lax.axis_index("c")
    chip = 2 * lax.axis_index("x") + lax.axis_index("y")

    gathered = allgather_chips("gather_weights", pack_shards({n: wts[n] for n in SHARDED}, BF16))
    full = full_from_gathered(gathered, shapes)
    exact = allgather_chips("gather_exact", jnp.concatenate(
        [wts["gdn_conv_w"].reshape(3, D), wts["meta_tokens"].reshape(4, D), jnp.zeros((1, D), F32)], axis=0))
    full["gdn_conv_w"] = jnp.concatenate([exact[q, 0:3].reshape(CONV_K, 3 * D // 4) for q in range(4)], axis=1)
    full["meta_tokens"] = jnp.concatenate([exact[q, 3:7].reshape(N_META, D // 4) for q in range(4)], axis=1)
    small = {n: wts[n].reshape(1, -1) for n in SMALL}

    loss_row, gx, gw, gs = local_step(x[0], loss_target[0], full, small)

    rh = PACK_ROWS // 2
    g5 = shards_from_full(gw).reshape(4, 2, rh, D)
    from_sib = sibling_halves("grads_sibling", g5)
    tsum = _pick(rh, 1024, 8)
    chip_sum = jnp.stack([sum_slots(f"grads_chip_sum{q}", [(g5[q], c), (from_sib, q)], tsum) for q in range(4)])
    from_chips = scatter_chips("grads_scatter", chip_sum)
    half = sum_slots("grads_sum", [(chip_sum, chip), (from_chips, 0), (from_chips, 1), (from_chips, 2)], tsum)
    g_shard = sibling_join("grads_join", half).reshape(PACK_ROWS, D)

    tad = _pick(PACK_ROWS, 688, 8)
    delta_p, m_p, v_p = adamw("adamw", pack_shards({n: wts[n] for n in SHARDED}, F32), g_shard,
                              pack_shards({n: mom_m[n] for n in SHARDED}, F32), pack_shards({n: mom_v[n] for n in SHARDED}, F32), tad)
    grads = unpack_shards(g_shard, shapes)
    delta = unpack_shards(delta_p, shapes)
    new_m = unpack_shards(m_p, shapes)
    new_v = unpack_shards(v_p, shapes)

    sm = pack_small(gs).at[LOSS_ROW, :LANES].set(loss_row[0])
    every = allgather_all("small_gather", sm)
    sm_sum = sum_slots("small_sum", [(every, s) for s in range(8)], SMALL_ROWS)
    d_s, m_s, v_s = adamw("adamw_small", pack_small(small), sm_sum, pack_small({n: mom_m[n].reshape(1, -1) for n in SMALL}),
                          pack_small({n: mom_v[n].reshape(1, -1) for n in SMALL}), SMALL_ROWS)
    grads.update(unpack_small(sm_sum, shapes))
    delta.update(unpack_small(d_s, shapes))
    new_m.update(unpack_small(m_s, shapes))
    new_v.update(unpack_small(v_s, shapes))
    loss = sm_sum[LOSS_ROW, 0]

    return (loss, gx[None], *[grads[n] for n in WEIGHTS], *[delta[n] for n in WEIGHTS], *[new_m[n] for n in WEIGHTS],
            *[new_v[n] for n in WEIGHTS])
```

```python
import functools

import jax
import jax.numpy as jnp
from jax import lax
from jax.experimental import pallas as pl
from jax.experimental.pallas import tpu as pltpu

F32 = jnp.float32
BF16 = jnp.bfloat16
HI = lax.Precision.HIGHEST
MESH = pl.DeviceIdType.MESH

D = 1024
N_META = 16
PAD = 48
HEAD_ROWS = PAD + N_META
CH = 64
H = 8
DH = 128
DFF = 2816
CONV_K = 4
EPS = 1e-6
ROPE_BASE = 10000.0
LANES = 128
N_CHIPS = 4
VMEM_LIMIT = 56 * 2 ** 20

OFF_QKV, OFF_Z, OFF_RQ, OFF_RK, OFF_RV, OFF_RG, OFF_GA, OFF_GB, OFF_BA = 0, 3072, 4096, 5120, 6144, 7168, 8192, 9216, 10240
PW = 10368
D_PROJ = 10256

ADAM_LR, ADAM_B1, ADAM_B2, ADAM_EPS, ADAM_WD, ADAM_STEP = 0.001, 0.9, 0.999, 1e-08, 0.01, 10


def _pick(n, cap, mult):
    best = None
    for t in range(mult, min(n, cap) + 1, mult):
        if n % t == 0:
            best = t
    return best if best is not None else n


def _tile_rows(rows, cols, block_bytes=3 * 2 ** 19):
    return _pick(rows, max(8, block_bytes // (4 * cols)), 8)


def _dot(a, b, dims, prec=None):
    return lax.dot_general(a, b, (dims, ((), ())), precision=prec, preferred_element_type=F32)


def _mm(a, b):
    return _dot(a.astype(BF16), b.astype(BF16), ((1,), (0,)))


def _mm_nt(a, b):
    return _dot(a.astype(BF16), b.astype(BF16), ((1,), (1,)))


def _mm_tn(a, b):
    return _dot(a.astype(BF16), b.astype(BF16), ((0,), (0,)))


def _mmh(a, b):
    return _dot(a, b, ((1,), (0,)), HI)


def _split(a):
    hi = a.astype(BF16)
    return hi, (a - hi.astype(F32)).astype(BF16)


def _dot3(a, b, dims):
    ah, al = _split(a)
    bh, bl = _split(b)
    return _dot(ah, bh, dims) + (_dot(ah, bl, dims) + _dot(al, bh, dims))


def matmul(name, a, b, mode, *, out_dtype=F32, resid=None, alpha=1.0, ti_cap=688, tj_cap=512, tr_cap=1408,
           b_split=False, o_split=False):
    if mode == "nn":
        I, R = a.shape
        J = N_CHIPS * b.shape[2] if b_split else b.shape[1]
    elif mode == "nt":
        I, R = a.shape
        J = b.shape[1] if b_split else b.shape[0]
    else:
        R, I = a.shape
        J = b.shape[1]
    ti = _pick(I, ti_cap, 16) if mode != "tn" else _pick(I, ti_cap, LANES)
    tj = _pick(J, tj_cap, LANES)
    tr = _pick(R, tr_cap, LANES) if mode != "tn" else _pick(R, tr_cap, 16)
    if mode == "nn":
        a_spec = pl.BlockSpec((ti, tr), lambda i, j, r: (i, r))
        if b_split:
            tj = J // N_CHIPS
            b_spec = pl.BlockSpec((None, tr, tj), lambda i, j, r: (j, r, 0))
        else:
            b_spec = pl.BlockSpec((tr, tj), lambda i, j, r: (r, j))
        dims = ((1,), (0,))
    elif mode == "nt":
        if b_split:
            tr = R // N_CHIPS
            b_spec = pl.BlockSpec((None, tj, tr), lambda i, j, r: (r, j, 0))
        else:
            b_spec = pl.BlockSpec((tj, tr), lambda i, j, r: (j, r))
        a_spec = pl.BlockSpec((ti, tr), lambda i, j, r: (i, r))
        dims = ((1,), (1,))
    else:
        if o_split:
            tj = J // N_CHIPS
        a_spec = pl.BlockSpec((tr, ti), lambda i, j, r: (r, i))
        b_spec = pl.BlockSpec((tr, tj), lambda i, j, r: (r, j))
        dims = ((0,), (0,))
    nr = R // tr
    assert I % ti == 0 and J % tj == 0 and R % tr == 0, (name, I, J, R, ti, tj, tr)
    if o_split:
        o_spec = pl.BlockSpec((None, ti, tj), lambda i, j, r: (j, i, 0))
        out_shape = jax.ShapeDtypeStruct((N_CHIPS, I, tj), out_dtype)
    else:
        o_spec = pl.BlockSpec((ti, tj), lambda i, j, r: (i, j))
        out_shape = jax.ShapeDtypeStruct((I, J), out_dtype)
    has_resid = resid is not None

    def body(*refs):
        if has_resid:
            a_ref, b_ref, r_ref, o_ref, acc = refs
        else:
            a_ref, b_ref, o_ref, acc = refs
        r = pl.program_id(2)

        @pl.when(r == 0)
        def _():
            acc[...] = jnp.zeros_like(acc)

        acc[...] += _dot(a_ref[...].astype(BF16), b_ref[...].astype(BF16), dims)

        @pl.when(r == nr - 1)
        def _():
            res = acc[...] * alpha if alpha != 1.0 else acc[...]
            if has_resid:
                res = r_ref[...] + res
            o_ref[...] = res.astype(o_ref.dtype)

    ins = [a, b] + ([resid] if has_resid else [])
    specs = [a_spec, b_spec] + ([o_spec] if has_resid else [])
    return pl.pallas_call(
        body, name=name, grid=(I // ti, J // tj, nr), in_specs=specs, out_specs=o_spec, out_shape=out_shape,
        scratch_shapes=[pltpu.VMEM((ti, tj), F32)],
        compiler_params=pltpu.CompilerParams(dimension_semantics=("parallel", "parallel", "arbitrary"),
                                             vmem_limit_bytes=VMEM_LIMIT),
    )(*ins)


def rowwise(name, fn, rows, pars, outs, accs=(), *, n_rows, tm):
    nt = n_rows // tm
    assert nt * tm == n_rows
    in_specs, ins = [], []
    for arr, w, cb in rows:
        in_specs.append(pl.BlockSpec((tm, w), lambda i, cb=cb: (i, cb)))
        ins.append(arr)
    for p in pars:
        in_specs.append(pl.BlockSpec(p.shape, lambda i, nd=p.ndim: (0,) * nd))
        ins.append(p)
    n_in = len(ins)
    out_specs, out_shapes, aliases = [], [], {}
    for k, o in enumerate(outs):
        if o[0] == "new":
            _, w, dt = o
            out_specs.append(pl.BlockSpec((tm, w), lambda i: (i, 0)))
            out_shapes.append(jax.ShapeDtypeStruct((n_rows, w), dt))
        else:
            _, arr, w, cb = o
            in_specs.append(pl.BlockSpec(memory_space=pl.ANY))
            aliases[len(ins)] = k
            ins.append(arr)
            out_specs.append(pl.BlockSpec((tm, w), lambda i, cb=cb: (i, cb)))
            out_shapes.append(jax.ShapeDtypeStruct(arr.shape, arr.dtype))
    for r, w in accs:
        out_specs.append(pl.BlockSpec((r, w), lambda i: (0, 0)))
        out_shapes.append(jax.ShapeDtypeStruct((r, w), F32))
    n_all_in, n_out, n_acc = len(ins), len(outs), len(accs)

    def body(*refs):
        i = pl.program_id(0)
        vals = [r[...] for r in refs[:n_in]]
        res = fn(i, *vals)
        if not isinstance(res, (tuple, list)):
            res = (res,)
        o_refs = refs[n_all_in:n_all_in + n_out]
        a_refs = refs[n_all_in + n_out:]
        for r, v in zip(o_refs, res[:n_out]):
            r[...] = v.astype(r.dtype)
        if n_acc:
            @pl.when(i == 0)
            def _():
                for r in a_refs:
                    r[...] = jnp.zeros_like(r)
            for r, v in zip(a_refs, res[n_out:]):
                r[...] += v

    return pl.pallas_call(
        body, name=name, grid=(nt,), in_specs=in_specs, out_specs=out_specs, out_shape=out_shapes,
        input_output_aliases=aliases,
        compiler_params=pltpu.CompilerParams(dimension_semantics=("arbitrary",), vmem_limit_bytes=VMEM_LIMIT),
    )(*ins)


def _row_ids(i, tm):
    return i * tm + lax.broadcasted_iota(jnp.int32, (tm, 1), 0)


def _sigmoid(x):
    return 1.0 / (1.0 + jnp.exp(-x))


def _silu(x):
    return x * _sigmoid(x)


def _softplus(x):
    return jnp.maximum(x, 0.0) + jnp.log(1.0 + jnp.exp(-jnp.abs(x)))


def _rms(x, w):
    return x * lax.rsqrt(jnp.mean(x * x, axis=-1, keepdims=True) + EPS) * w


def _heads(fn, *xs):
    return jnp.concatenate([fn(h, *[x[:, h * DH:(h + 1) * DH] for x in xs]) for h in range(H)], axis=1)


def _swiglu(gu):
    return _silu(gu[:, :DFF]) * gu[:, DFF:]


def _gdn_post(o, z, w):
    return _heads(lambda h, oh, zh: oh * lax.rsqrt(jnp.mean(oh * oh, axis=-1, keepdims=True) + EPS) * w * _silu(zh), o, z)


def _ret_post(o, rg, w):
    def one(h, oh, gh, wh):
        mu = jnp.mean(oh, axis=-1, keepdims=True)
        xc = oh - mu
        var = jnp.mean(xc * xc, axis=-1, keepdims=True)
        return _silu(gh) * (xc * lax.rsqrt(var + EPS) * wh)
    return _heads(one, o, rg, jnp.broadcast_to(w, (o.shape[0], D)))


def _merge(a, b, ga, gb):
    return _sigmoid(ga) * a + _sigmoid(gb) * b


def _select_matrix(first_lane):
    r = lax.broadcasted_iota(jnp.int32, (LANES, H * DH), 0)
    c = lax.broadcasted_iota(jnp.int32, (LANES, H * DH), 1)
    return (r == first_lane + (c >> 7)).astype(F32)


def _chunk_tri(tm):
    r = lax.broadcasted_iota(jnp.int32, (tm, tm), 0)
    c = lax.broadcasted_iota(jnp.int32, (tm, tm), 1)
    return jnp.logical_and((r >> 6) == (c >> 6), r >= c).astype(F32)


def _gdn_gates(ba, alog_row, dtb_row, mask):
    g = -jnp.exp(alog_row) * _softplus(ba + dtb_row) * mask
    gc = _mmh(_chunk_tri(ba.shape[0]), g)
    beta = _sigmoid(ba) * mask
    return _mmh(gc, _select_matrix(H)), _mmh(beta, _select_matrix(0)), gc


def _gdn_qkv(c):
    a = _silu(c)

    def l2(scale):
        return lambda h, t: t * lax.rsqrt(jnp.sum(t * t, axis=-1, keepdims=True) + EPS) * scale
    q = _heads(l2(DH ** -0.5), a[:, :D])
    k = _heads(l2(1.0), a[:, D:2 * D])
    return q, k, a[:, 2 * D:]


def _conv_taps(xw, cws, tm):
    return sum(cws[i] * xw[5 + i:5 + i + tm] for i in range(CONV_K))


def _swap_pairs(t):
    n = t.shape[1]
    lane = lax.broadcasted_iota(jnp.int32, t.shape, 1)
    return jnp.where((lane & 1) == 0, pltpu.roll(t, n - 1, 1), pltpu.roll(t, 1, 1))


def _tile_heads(t):
    return jnp.concatenate([t] * H, axis=1)


@jax.custom_vjp
def _unit_lower_inv(a):
    n = -a
    eye = (lax.broadcasted_iota(jnp.int32, (CH, CH), 0) == lax.broadcasted_iota(jnp.int32, (CH, CH), 1)).astype(F32)
    p = eye + n
    for _ in range(5):
        n = _dot3(n, n, ((1,), (0,)))
        p = p + _dot3(p, n, ((1,), (0,)))
    return p


def _inv_fwd(a):
    t = _unit_lower_inv(a)
    return t, t


def _inv_bwd(t, dt):
    x = _dot3(t, dt, ((0,), (0,)))
    return (-_dot3(x, t, ((1,), (1,))),)


_unit_lower_inv.defvjp(_inv_fwd, _inv_bwd)


def _gdn_chunk(q, k, v, gc, bb, gr, s):
    ri = lax.broadcasted_iota(jnp.int32, (CH, CH), 0)
    ci = lax.broadcasted_iota(jnp.int32, (CH, CH), 1)
    causal = ri >= ci
    gc1 = jnp.sum(gc, axis=1, keepdims=True) * (1.0 / LANES)
    diff = jnp.broadcast_to(gc1, (CH, CH)) - jnp.broadcast_to(gr, (CH, CH))
    decay = jnp.where(causal, jnp.exp(jnp.where(causal, diff, 0.0)), 0.0)
    kb = k * bb
    sc = _mm_nt(jnp.concatenate([q, kb], axis=0), k)
    qk = sc[:CH] * decay
    a = jnp.where(ri > ci, sc[CH:] * decay, 0.0)
    t = _unit_lower_inv(a)
    eg = jnp.exp(gc)
    uw = _mm(t, jnp.concatenate([v * bb, kb * eg], axis=1))
    g_last = gc[CH - 1:CH, :]
    ws = _mm(jnp.concatenate([uw[:, DH:], q * eg], axis=0), s)
    v_new = uw[:, :DH] - ws[:CH]
    o = ws[CH:] + _mm(qk, v_new)
    s_new = s * jnp.exp(g_last) + _mm_tn(k * jnp.exp(g_last - gc), v_new)
    return o, s_new


def _ret_chunk(q, k, v, s, decay, xi, zeta, cd):
    scores = _mm_nt(q, k) * decay
    o = _mm(scores, v) + _mm(q * xi, s)
    s_new = s * cd + _mm_tn(k * zeta, v)
    return o, s_new


def _scan_fwd(name, chunk_fn, blocks, rowvecs, consts, nc):
    n_blk, n_rv, n_c = len(blocks), len(rowvecs), len(consts)

    def body(*refs):
        blk = refs[:n_blk]
        rvs = refs[n_blk:n_blk + n_rv]
        cst = refs[n_blk + n_rv:n_blk + n_rv + n_c]
        o_ref, st_ref, s_scr = refs[n_blk + n_rv + n_c:]

        @pl.when(pl.program_id(0) == 0)
        def _():
            s_scr[...] = jnp.zeros_like(s_scr)

        for h in range(H):
            sl = slice(h * DH, (h + 1) * DH)
            s = s_scr[h]
            st_ref[0, h] = s
            o, s_new = chunk_fn(*[r[:, sl] for r in blk], *[r[0, h:h + 1, :] for r in rvs], s, *[r[h] for r in cst])
            o_ref[:, sl] = o
            s_scr[h] = s_new

    in_specs = [pl.BlockSpec((CH, H * DH), lambda c, f=f: (c, f)) for _, f in blocks]
    in_specs += [pl.BlockSpec((1, H, CH), lambda c: (c, 0, 0)) for _ in rowvecs]
    in_specs += [pl.BlockSpec(a.shape, lambda c: (0, 0, 0)) for a in consts]
    return pl.pallas_call(
        body, name=name, grid=(nc,), in_specs=in_specs,
        out_specs=[pl.BlockSpec((CH, H * DH), lambda c: (c, 0)), pl.BlockSpec((1, H, DH, DH), lambda c: (c, 0, 0, 0))],
        out_shape=[jax.ShapeDtypeStruct((nc * CH, H * DH), F32), jax.ShapeDtypeStruct((nc, H, DH, DH), F32)],
        scratch_shapes=[pltpu.VMEM((H, DH, DH), F32)],
        compiler_params=pltpu.CompilerParams(dimension_semantics=("arbitrary",), vmem_limit_bytes=VMEM_LIMIT),
    )(*[a for a, _ in blocks], *rowvecs, *consts)


def _scan_bwd(name, chunk_fn, blocks, rowvecs, consts, states, do, outs, nc):
    n_blk, n_rv, n_c = len(blocks), len(rowvecs), len(consts)
    n_into = sum(1 for o in outs if o[0] == "into")

    def body(*refs):
        blk = refs[:n_blk]
        rvs = refs[n_blk:n_blk + n_rv]
        cst = refs[n_blk + n_rv:n_blk + n_rv + n_c]
        st_ref, do_ref = refs[n_blk + n_rv + n_c:n_blk + n_rv + n_c + 2]
        n_in = n_blk + n_rv + n_c + 2 + n_into
        o_refs = refs[n_in:n_in + n_blk]
        rv_refs = refs[n_in + n_blk:n_in + n_blk + n_rv]
        ds_scr = refs[n_in + n_blk + n_rv]

        @pl.when(pl.program_id(0) == 0)
        def _():
            ds_scr[...] = jnp.zeros_like(ds_scr)

        for h in range(H):
            sl = slice(h * DH, (h + 1) * DH)
            cv = [r[h] for r in cst]
            _, vjp = jax.vjp(lambda *a: chunk_fn(*a, *cv), *[r[:, sl] for r in blk], *[r[0, h:h + 1, :] for r in rvs],
                             st_ref[0, h])
            grads = vjp((do_ref[:, sl], ds_scr[h]))
            for r, g in zip(o_refs, grads[:n_blk]):
                r[:, sl] = g
            for r, g in zip(rv_refs, grads[n_blk:n_blk + n_rv]):
                r[0, h:h + 1, :] = g
            ds_scr[h] = grads[n_blk + n_rv]

    rc = lambda c: nc - 1 - c
    in_specs = [pl.BlockSpec((CH, H * DH), lambda c, f=f: (rc(c), f)) for _, f in blocks]
    in_specs += [pl.BlockSpec((1, H, CH), lambda c: (rc(c), 0, 0)) for _ in rowvecs]
    in_specs += [pl.BlockSpec(a.shape, lambda c: (0, 0, 0)) for a in consts]
    in_specs += [pl.BlockSpec((1, H, DH, DH), lambda c: (rc(c), 0, 0, 0)), pl.BlockSpec((CH, H * DH), lambda c: (rc(c), 0))]
    ins = [a for a, _ in blocks] + list(rowvecs) + list(consts) + [states, do]
    out_specs, out_shapes, aliases = [], [], {}
    for k, o in enumerate(outs):
        if o[0] == "new":
            out_specs.append(pl.BlockSpec((CH, H * DH), lambda c: (rc(c), 0)))
            out_shapes.append(jax.ShapeDtypeStruct((nc * CH, H * DH), F32))
        else:
            _, arr, f = o
            in_specs.append(pl.BlockSpec(memory_space=pl.ANY))
            aliases[len(ins)] = k
            ins.append(arr)
            out_specs.append(pl.BlockSpec((CH, H * DH), lambda c, f=f: (rc(c), f)))
            out_shapes.append(jax.ShapeDtypeStruct(arr.shape, arr.dtype))
    for _ in rowvecs:
        out_specs.append(pl.BlockSpec((1, H, CH), lambda c: (rc(c), 0, 0)))
        out_shapes.append(jax.ShapeDtypeStruct((nc, H, CH), F32))
    return pl.pallas_call(
        body, name=name, grid=(nc,), in_specs=in_specs, out_specs=out_specs, out_shape=out_shapes,
        input_output_aliases=aliases, scratch_shapes=[pltpu.VMEM((H, DH, DH), F32)],
        compiler_params=pltpu.CompilerParams(dimension_semantics=("arbitrary",), vmem_limit_bytes=VMEM_LIMIT),
    )(*ins)


def _gdn_pre_specs(p, cws, alog_row, dtb_row, tm, pos):
    sub = tm // 8
    rows = [pl.BlockSpec((tm, 3 * D), lambda i: (pos(i), 0)),
            pl.BlockSpec((8, 3 * D), lambda i: (jnp.maximum(pos(i) * sub - 1, 0), 0)),
            pl.BlockSpec((tm, LANES), lambda i: (pos(i), OFF_BA // LANES))]
    pars = [pl.BlockSpec(a.shape, lambda i: (0, 0)) for a in (*cws, alog_row, dtb_row)]
    return rows + pars, [p, p, p, *cws, alog_row, dtb_row]


def gdn_pre_fwd(p, cws, alog_row, dtb_row, lp, tm):
    def body(x_ref, prev_ref, ba_ref, c0, c1, c2, c3, al_ref, dt_ref, q_ref, k_ref, v_ref, g_ref, b_ref, gc_ref):
        i = pl.program_id(0)
        prev = jnp.where(i > 0, prev_ref[...], 0.0)
        xw = jnp.concatenate([prev, x_ref[...]], axis=0)
        c = _conv_taps(xw, [c0[...], c1[...], c2[...], c3[...]], tm)
        q, k, v = _gdn_qkv(c)
        mask = (_row_ids(i, tm) >= PAD).astype(F32)
        g, b, gc = _gdn_gates(ba_ref[...], al_ref[...], dt_ref[...], mask)
        q_ref[...] = q
        k_ref[...] = k
        v_ref[...] = v
        g_ref[...] = g
        b_ref[...] = b
        gc_ref[...] = gc

    in_specs, ins = _gdn_pre_specs(p, cws, alog_row, dtb_row, tm, lambda i: i)
    o_spec = pl.BlockSpec((tm, D), lambda i: (i, 0))
    return pl.pallas_call(
        body, name="gdn_pre_fwd", grid=(lp // tm,), in_specs=in_specs,
        out_specs=[o_spec] * 5 + [pl.BlockSpec((tm, LANES), lambda i: (i, 0))],
        out_shape=[jax.ShapeDtypeStruct((lp, D), F32)] * 5 + [jax.ShapeDtypeStruct((lp, LANES), F32)],
        compiler_params=pltpu.CompilerParams(dimension_semantics=("parallel",), vmem_limit_bytes=VMEM_LIMIT),
    )(*ins)


def gdn_pre_bwd(p, cws, alog_row, dtb_row, dq, dk, dv, dg, db, dgc, dp, lp, tm):
    nt = lp // tm
    pos = lambda i: nt - 1 - i

    def body(x_ref, prev_ref, ba_ref, c0, c1, c2, c3, al_ref, dt_ref, dq_ref, dk_ref, dv_ref, dg_ref, db_ref, dgc_ref,
             dp_any, dx_ref, dba_ref, dcw_ref, dpar_ref, carry):
        i = pl.program_id(0)
        t = pos(i)

        @pl.when(i == 0)
        def _():
            carry[...] = jnp.zeros_like(carry)
            dcw_ref[...] = jnp.zeros_like(dcw_ref)
            dpar_ref[...] = jnp.zeros_like(dpar_ref)

        cws_v = [c0[...], c1[...], c2[...], c3[...]]
        prev = jnp.where(t > 0, prev_ref[...], 0.0)
        xw = jnp.concatenate([prev, x_ref[...]], axis=0)
        c = _conv_taps(xw, cws_v, tm)
        _, vjp_qkv = jax.vjp(_gdn_qkv, c)
        (dc,) = vjp_qkv((dq_ref[...], dk_ref[...], dv_ref[...]))
        zeros8 = jnp.zeros((8, 3 * D), F32)
        dcp = jnp.concatenate([zeros8, dc, zeros8], axis=0)
        dxw = sum(cws_v[j] * dcp[3 - j:3 - j + tm + 8] for j in range(CONV_K))
        dx_ref[...] = dxw[8:]
        dx_ref[tm - 8:tm, :] += carry[...]
        carry[...] = dxw[:8]
        for j in range(CONV_K):
            dcw_ref[j:j + 1, :] += jnp.sum(dc * xw[5 + j:5 + j + tm], axis=0, keepdims=True)
        mask = (_row_ids(t, tm) >= PAD).astype(F32)
        _, vjp_g = jax.vjp(lambda ba, al, dt: _gdn_gates(ba, al, dt, mask), ba_ref[...], al_ref[...], dt_ref[...])
        dba, dal, ddt = vjp_g((dg_ref[...], db_ref[...], dgc_ref[...]))
        dba_ref[...] = dba
        dpar_ref[0:1, :] += dal
        dpar_ref[1:2, :] += ddt

    in_specs, ins = _gdn_pre_specs(p, cws, alog_row, dtb_row, tm, pos)
    g_spec = pl.BlockSpec((tm, D), lambda i: (pos(i), 0))
    s_spec = pl.BlockSpec((tm, LANES), lambda i: (pos(i), 0))
    in_specs += [g_spec] * 5 + [s_spec, pl.BlockSpec(memory_space=pl.ANY)]
    ins += [dq, dk, dv, dg, db, dgc, dp]
    return pl.pallas_call(
        body, name="gdn_pre_bwd", grid=(nt,), in_specs=in_specs,
        out_specs=[pl.BlockSpec((tm, 3 * D), lambda i: (pos(i), 0)), s_spec,
                   pl.BlockSpec((8, 3 * D), lambda i: (0, 0)), pl.BlockSpec((8, LANES), lambda i: (0, 0))],
        out_shape=[jax.ShapeDtypeStruct(dp.shape, F32), jax.ShapeDtypeStruct((lp, LANES), F32),
                   jax.ShapeDtypeStruct((8, 3 * D), F32), jax.ShapeDtypeStruct((8, LANES), F32)],
        input_output_aliases={len(ins) - 1: 0},
        scratch_shapes=[pltpu.VMEM((8, 3 * D), F32)],
        compiler_params=pltpu.CompilerParams(dimension_semantics=("arbitrary",), vmem_limit_bytes=VMEM_LIMIT),
    )(*ins)


def _me():
    return lax.axis_index("x"), lax.axis_index("y"), lax.axis_index("c")


def _any_specs(n):
    return [pl.BlockSpec(memory_space=pl.ANY)] * n


def allgather_chips(name, ws):
    n = len(ws)

    def body(*refs):
        w_refs, o_refs = refs[:n], refs[n:2 * n]
        send_sems, recv_sems, local_sems = refs[2 * n:]
        x, y, c = _me()
        me = 2 * x + y
        chips = [(1 - x, y), (x, 1 - y), (1 - x, 1 - y)]
        mine = [pltpu.make_async_copy(w_refs[a], o_refs[a].at[me], local_sems.at[a]) for a in range(n)]
        for cp in mine:
            cp.start()

        def copy(a, j, slot):
            px, py = chips[j]
            return pltpu.make_async_remote_copy(src_ref=w_refs[a], dst_ref=o_refs[a].at[slot], send_sem=send_sems.at[a, j],
                                                recv_sem=recv_sems.at[a, j], device_id=(px, py, c), device_id_type=MESH)
        sends = [copy(a, j, me) for a in range(n) for j in range(3)]
        for cp in sends:
            cp.start()
        for a in range(n):
            for j, (px, py) in enumerate(chips):
                copy(a, j, 2 * px + py).wait_recv()
        for cp in sends:
            cp.wait_send()
        for cp in mine:
            cp.wait()

    return pl.pallas_call(
        body, name=name, out_shape=[jax.ShapeDtypeStruct((N_CHIPS,) + w.shape, w.dtype) for w in ws],
        in_specs=_any_specs(n), out_specs=_any_specs(n),
        scratch_shapes=[pltpu.SemaphoreType.DMA((n, 3)), pltpu.SemaphoreType.DMA((n, 3)), pltpu.SemaphoreType.DMA((n,))],
    )(*ws)


def sibling_halves(name, gs):
    n = len(gs)

    def body(*refs):
        g_refs, o_refs = refs[:n], refs[n:2 * n]
        send_sems, recv_sems = refs[2 * n:]
        x, y, c = _me()
        cps = []
        for a in range(n):
            rh = gs[a].shape[1] // 2
            for q in range(N_CHIPS):
                cps.append(pltpu.make_async_remote_copy(
                    src_ref=g_refs[a].at[q, pl.ds(pl.multiple_of((1 - c) * rh, 8), rh)], dst_ref=o_refs[a].at[q],
                    send_sem=send_sems.at[a, q], recv_sem=recv_sems.at[a, q], device_id=(x, y, 1 - c), device_id_type=MESH))
        for cp in cps:
            cp.start()
        for cp in cps:
            cp.wait_recv()
        for cp in cps:
            cp.wait_send()

    return pl.pallas_call(
        body, name=name, out_shape=[jax.ShapeDtypeStruct((N_CHIPS, g.shape[1] // 2, g.shape[2]), g.dtype) for g in gs],
        in_specs=_any_specs(n), out_specs=_any_specs(n),
        scratch_shapes=[pltpu.SemaphoreType.DMA((n, N_CHIPS)), pltpu.SemaphoreType.DMA((n, N_CHIPS))],
    )(*gs)


def scatter_chips(name, css):
    n = len(css)

    def body(*refs):
        c_refs, o_refs = refs[:n], refs[n:2 * n]
        send_sems, recv_sems = refs[2 * n:]
        x, y, c = _me()
        chips = [(1 - x, y), (x, 1 - y), (1 - x, 1 - y)]
        cps = []
        for a in range(n):
            for j, (px, py) in enumerate(chips):
                cps.append(pltpu.make_async_remote_copy(
                    src_ref=c_refs[a].at[2 * px + py], dst_ref=o_refs[a].at[j], send_sem=send_sems.at[a, j],
                    recv_sem=recv_sems.at[a, j], device_id=(px, py, c), device_id_type=MESH))
        for cp in cps:
            cp.start()
        for cp in cps:
            cp.wait_recv()
        for cp in cps:
            cp.wait_send()

    return pl.pallas_call(
        body, name=name, out_shape=[jax.ShapeDtypeStruct((3,) + cs.shape[1:], cs.dtype) for cs in css],
        in_specs=_any_specs(n), out_specs=_any_specs(n),
        scratch_shapes=[pltpu.SemaphoreType.DMA((n, 3)), pltpu.SemaphoreType.DMA((n, 3))],
    )(*css)


def sibling_join(name, halves):
    n = len(halves)

    def body(*refs):
        h_refs, o_refs = refs[:n], refs[n:2 * n]
        send_sems, recv_sems, local_sems = refs[2 * n:]
        x, y, c = _me()
        mine = [pltpu.make_async_copy(h_refs[a], o_refs[a].at[c], local_sems.at[a]) for a in range(n)]
        for cp in mine:
            cp.start()

        def copy(a, slot):
            return pltpu.make_async_remote_copy(src_ref=h_refs[a], dst_ref=o_refs[a].at[slot], send_sem=send_sems.at[a],
                                                recv_sem=recv_sems.at[a], device_id=(x, y, 1 - c), device_id_type=MESH)
        sends = [copy(a, c) for a in range(n)]
        for cp in sends:
            cp.start()
        for a in range(n):
            copy(a, 1 - c).wait_recv()
        for cp in sends:
            cp.wait_send()
        for cp in mine:
            cp.wait()

    return pl.pallas_call(
        body, name=name, out_shape=[jax.ShapeDtypeStruct((2,) + h.shape, h.dtype) for h in halves],
        in_specs=_any_specs(n), out_specs=_any_specs(n),
        scratch_shapes=[pltpu.SemaphoreType.DMA((n,)), pltpu.SemaphoreType.DMA((n,)), pltpu.SemaphoreType.DMA((n,))],
    )(*halves)


def allgather_all(name, s):
    def body(s_ref, out_ref, send_sems, recv_sems, local_sem):
        x, y, c = _me()
        peers = [(x ^ ((m >> 2) & 1), y ^ ((m >> 1) & 1), c ^ (m & 1)) for m in range(1, 8)]
        mine = pltpu.make_async_copy(s_ref, out_ref.at[4 * x + 2 * y + c], local_sem)
        mine.start()

        def copy(j, slot):
            return pltpu.make_async_remote_copy(src_ref=s_ref, dst_ref=out_ref.at[slot], send_sem=send_sems.at[j],
                                                recv_sem=recv_sems.at[j], device_id=peers[j], device_id_type=MESH)
        sends = [copy(j, 4 * x + 2 * y + c) for j in range(7)]
        for cp in sends:
            cp.start()
        for j, (px, py, pc) in enumerate(peers):
            copy(j, 4 * px + 2 * py + pc).wait_recv()
        for cp in sends:
            cp.wait_send()
        mine.wait()

    return pl.pallas_call(
        body, name=name, out_shape=jax.ShapeDtypeStruct((8,) + s.shape, s.dtype),
        in_specs=_any_specs(1), out_specs=pl.BlockSpec(memory_space=pl.ANY),
        scratch_shapes=[pltpu.SemaphoreType.DMA((7,)), pltpu.SemaphoreType.DMA((7,)), pltpu.SemaphoreType.DMA(())],
    )(s)


def sum_slots(name, parts):
    n = len(parts)
    R, W = parts[0][0].shape[1:]
    tm = _tile_rows(R, W)
    idx = jnp.stack([jnp.asarray(s, jnp.int32) for _, s in parts])

    def body(idx_ref, *refs):
        acc = refs[0][...].astype(F32)
        for r in refs[1:n]:
            acc = acc + r[...].astype(F32)
        refs[n][...] = acc

    grid_spec = pltpu.PrefetchScalarGridSpec(
        num_scalar_prefetch=1, grid=(R // tm,),
        in_specs=[pl.BlockSpec((None, tm, W), lambda i, idx, k=k: (idx[k], i, 0)) for k in range(n)],
        out_specs=pl.BlockSpec((tm, W), lambda i, idx: (i, 0)))
    return pl.pallas_call(body, name=name, grid_spec=grid_spec, out_shape=jax.ShapeDtypeStruct((R, W), F32),
                          compiler_params=pltpu.CompilerParams(dimension_semantics=("parallel",)))(idx, *[a for a, _ in parts])


def chip_sums(name, g, recv, c):
    _, R, W = g.shape
    rh = R // 2
    tm = _tile_rows(rh, W)
    g8 = g.reshape(2 * N_CHIPS, rh, W)
    idx = jnp.asarray(c, jnp.int32).reshape(1)

    def body(idx_ref, g_ref, r_ref, o_ref):
        o_ref[...] = g_ref[...] + r_ref[...]

    grid_spec = pltpu.PrefetchScalarGridSpec(
        num_scalar_prefetch=1, grid=(N_CHIPS, rh // tm),
        in_specs=[pl.BlockSpec((None, tm, W), lambda q, i, idx: (2 * q + idx[0], i, 0)),
                  pl.BlockSpec((None, tm, W), lambda q, i, idx: (q, i, 0))],
        out_specs=pl.BlockSpec((None, tm, W), lambda q, i, idx: (q, i, 0)))
    return pl.pallas_call(body, name=name, grid_spec=grid_spec, out_shape=jax.ShapeDtypeStruct((N_CHIPS, rh, W), F32),
                          compiler_params=pltpu.CompilerParams(dimension_semantics=("parallel", "parallel")))(idx, g8, recv)


def adamw(name, w, g, m, v):
    def fn(i, w, g, m, v):
        m = ADAM_B1 * m + (1.0 - ADAM_B1) * g
        v = ADAM_B2 * v + (1.0 - ADAM_B2) * (g * g)
        m_hat = m / (1.0 - ADAM_B1 ** ADAM_STEP)
        v_hat = v / (1.0 - ADAM_B2 ** ADAM_STEP)
        return -ADAM_LR * (m_hat / (jnp.sqrt(v_hat) + ADAM_EPS) + ADAM_WD * w), m, v
    R, W = w.shape
    return rowwise(name, fn, [(a, W, 0) for a in (w, g, m, v)], [], [("new", W, F32)] * 3, n_rows=R, tm=_tile_rows(R, W, 2 ** 20))


def _w_in_padded(w):
    return jnp.concatenate([w[:, :4096], w[:, 4112:], w[:, 4096:4112], jnp.zeros((D, PW - D_PROJ), w.dtype)], axis=1)


def _w_in_unpadded(g):
    return jnp.concatenate([g[:, :4096], g[:, OFF_BA:OFF_BA + 16], g[:, 4096:OFF_BA]], axis=1)


def _ret_consts():
    f = F32
    log_gamma = jnp.log1p(-jnp.exp2(-5.0 - jnp.arange(H, dtype=f)))
    pos = jnp.arange(CH, dtype=f)
    causal = jnp.tril(jnp.ones((CH, CH), dtype=bool))
    diff = pos[:, None] - pos[None, :]
    decay = jnp.where(causal, jnp.exp(jnp.where(causal, diff, 0.0) * log_gamma[:, None, None]), 0.0)
    xi = jnp.broadcast_to(jnp.exp((pos + 1.0) * log_gamma[:, None])[:, :, None], (H, CH, DH))
    zeta = jnp.broadcast_to(jnp.exp((CH - 1.0 - pos) * log_gamma[:, None])[:, :, None], (H, CH, DH))
    cd = jnp.broadcast_to(jnp.exp(CH * log_gamma)[:, None, None], (H, 1, DH))
    return decay, xi, zeta, cd


def _rope_tables(lp):
    pos = jnp.arange(lp, dtype=F32) - float(PAD)
    inv = 1.0 / (ROPE_BASE ** jnp.linspace(0.0, 1.0, DH // 2, dtype=F32))
    ang = pos[:, None] * inv[None, :]
    cos, sin = jnp.cos(ang), jnp.sin(ang)
    ct = jnp.repeat(cos, 2, axis=1)
    st = jnp.stack([-sin, sin], axis=-1).reshape(lp, DH)
    return ct, st


def local_step(x, tgt, w, small):
    seq = x.shape[0]
    lp = HEAD_ROWS + seq
    nc = lp // CH
    tm = _pick(lp, 192, CH)
    row = functools.partial(rowwise, n_rows=lp, tm=tm)

    h0 = jnp.concatenate([jnp.zeros((PAD, D), F32), w["meta_tokens"], x], axis=0)
    tgt_p = jnp.concatenate([jnp.zeros((HEAD_ROWS, D), F32), tgt], axis=0)

    def ffn_fwd(tag, h, wn, w_in, w_out):
        n = row(f"{tag}_norm", lambda i, h, wn: _rms(h, wn), [(h, D, 0)], [wn], [("new", D, BF16)])[0]
        gu = matmul(f"{tag}_up", n, w_in, "nn", b_split=True)
        mid = row(f"{tag}_act", lambda i, gu: _swiglu(gu), [(gu, 2 * DFF, 0)], [], [("new", DFF, BF16)])[0]
        out = matmul(f"{tag}_down", mid, w_out, "nn", resid=h, alpha=0.5)
        return out, (h, n, gu, mid)

    def ffn_bwd(tag, dh, saved, wn, w_in, w_out):
        h, n, gu, mid = saved
        dmid = matmul(f"{tag}_dmid", dh, w_out, "nt", alpha=0.5)
        dw_out = matmul(f"{tag}_dwout", mid, dh, "tn", alpha=0.5, ti_cap=1408)

        def act_bwd(i, gu, dmid):
            _, vjp = jax.vjp(_swiglu, gu)
            return vjp(dmid)[0]
        dgu = row(f"{tag}_dact", act_bwd, [(gu, 2 * DFF, 0), (dmid, DFF, 0)], [], [("new", 2 * DFF, F32)])[0]
        dn = matmul(f"{tag}_dn", dgu, w_in, "nt", tj_cap=1024, b_split=True)
        dw_in = matmul(f"{tag}_dwin", n, dgu, "tn", ti_cap=1024, tr_cap=688, o_split=True)
        dh_in, dwn = norm_bwd(f"{tag}_dnorm", h, wn, dn, dh)
        return dh_in, dw_in, dw_out, dwn

    def norm_bwd(name, h, wn, dn, dres):
        def fn(i, h, dn, dres, wn):
            _, vjp = jax.vjp(_rms, h, wn)
            dh, dw = vjp(dn)
            return dres + dh, dw
        return row(name, fn, [(h, D, 0), (dn, D, 0), (dres, D, 0)], [wn], [("new", D, F32)], [(1, D)])

    h1, ffn1_saved = ffn_fwd("ffn1", h0, small["ffn1_norm"], w["ffn1_w_in"], w["ffn1_w_out"])
    n2 = row("mix_norm", lambda i, h, wn: _rms(h, wn), [(h1, D, 0)], [small["mix_norm"]], [("new", D, BF16)])[0]
    p = matmul("mix_proj", n2, w["w_in_p"], "nn", tj_cap=384)

    cws = [w["gdn_conv_w"][j:j + 1] for j in range(CONV_K)]
    alog_row = jnp.zeros((1, LANES), F32).at[:, H:2 * H].set(small["gdn_a_log"])
    dtb_row = jnp.zeros((1, LANES), F32).at[:, H:2 * H].set(small["gdn_dt_bias"])
    q, k, v, gcb, bb, gc = gdn_pre_fwd(p, cws, alog_row, dtb_row, lp, tm)
    gc_rows = gc[:, H:2 * H].reshape(nc, CH, H).transpose(0, 2, 1)
    gdn_blocks = [(q, 0), (k, 0), (v, 0), (gcb, 0), (bb, 0)]
    o_a, gdn_states = _scan_fwd("gdn_scan_fwd", _gdn_chunk, gdn_blocks, [gc_rows], [], nc)
    y_a = row("gdn_post", lambda i, o, z, wn: _gdn_post(o, z, wn), [(o_a, D, 0), (p, D, OFF_Z // D)], [small["gdn_out_norm"]],
              [("new", D, BF16)])[0]

    ct, st = _rope_tables(lp)

    def rope_fwd(i, rq, rk, ct, st):
        c8, s8 = _tile_heads(ct), _tile_heads(st)
        return rq * c8 + _swap_pairs(rq) * s8, (rk * c8 + _swap_pairs(rk) * s8) * (DH ** -0.5)
    rq, rk = row("ret_pre", rope_fwd, [(p, D, OFF_RQ // D), (p, D, OFF_RK // D), (ct, DH, 0), (st, DH, 0)], [],
                 [("new", D, F32)] * 2)
    ret_consts = list(_ret_consts())
    ret_blocks = [(rq, 0), (rk, 0), (p, OFF_RV // D)]
    o_b, ret_states = _scan_fwd("ret_scan_fwd", _ret_chunk, ret_blocks, [], ret_consts, nc)
    y_b = row("ret_post", lambda i, o, g, wn: _ret_post(o, g, wn), [(o_b, D, 0), (p, D, OFF_RG // D)], [small["ret_out_norm"]],
              [("new", D, BF16)])[0]

    br_a = matmul("branch_gdn", y_a, w["w_branch_gdn"], "nn")
    br_b = matmul("branch_ret", y_b, w["w_branch_ret"], "nn")
    merged = row("merge", lambda i, a, b, ga, gb_: _merge(a, b, ga, gb_),
                 [(br_a, D, 0), (br_b, D, 0), (p, D, OFF_GA // D), (p, D, OFF_GB // D)], [], [("new", D, BF16)])[0]
    h2 = matmul("mix_out", merged, w["w_out"], "nn", resid=h1)
    h3, ffn2_saved = ffn_fwd("ffn2", h2, small["ffn2_norm"], w["ffn2_w_in"], w["ffn2_w_out"])

    def head(i, h, t, wn):
        mask = (_row_ids(i, tm) >= HEAD_ROWS).astype(F32)
        y, vjp = jax.vjp(_rms, h, wn)
        err = (y - t) * mask
        dh, dw = vjp(err * (1.0 / D))
        return dh, dw, jnp.sum(err * err, keepdims=True).reshape(1, 1) * (0.5 / D) * jnp.ones((1, LANES), F32)
    dh3, d_final, loss_row = row("loss_head", head, [(h3, D, 0), (tgt_p, D, 0)], [small["final_norm"]], [("new", D, F32)],
                                 [(1, D), (1, LANES)])

    gw, gs = {}, {"final_norm": d_final}
    dh2, gw["ffn2_w_in"], gw["ffn2_w_out"], gs["ffn2_norm"] = ffn_bwd("ffn2", dh3, ffn2_saved, small["ffn2_norm"],
                                                                      w["ffn2_w_in"], w["ffn2_w_out"])
    dmerged = matmul("mix_out_dx", dh2, w["w_out"], "nt", tj_cap=1024)
    gw["w_out"] = matmul("mix_out_dw", merged, dh2, "tn", ti_cap=1024)
    dp = lax.empty((lp, PW), F32)

    def merge_bwd(i, dm, a, b, ga, gb_):
        _, vjp = jax.vjp(_merge, a, b, ga, gb_)
        da, db, dga, dgb = vjp(dm)
        return da, db, jnp.concatenate([dga, dgb], axis=1)
    da, db_, dp = row("merge_bwd", merge_bwd,
                      [(dmerged, D, 0), (br_a, D, 0), (br_b, D, 0), (p, D, OFF_GA // D), (p, D, OFF_GB // D)], [],
                      [("new", D, F32), ("new", D, F32), ("into", dp, 2 * D, OFF_GA // (2 * D))])
    dy_a = matmul("branch_gdn_dx", da, w["w_branch_gdn"], "nt", tj_cap=1024)
    gw["w_branch_gdn"] = matmul("branch_gdn_dw", y_a, da, "tn", ti_cap=1024)
    dy_b = matmul("branch_ret_dx", db_, w["w_branch_ret"], "nt", tj_cap=1024)
    gw["w_branch_ret"] = matmul("branch_ret_dw", y_b, db_, "tn", ti_cap=1024)

    def gdn_post_bwd(i, o, z, dy, wn):
        _, vjp = jax.vjp(_gdn_post, o, z, wn)
        return vjp(dy)
    do_a, dp, gs["gdn_out_norm"] = row("gdn_post_bwd", gdn_post_bwd, [(o_a, D, 0), (p, D, OFF_Z // D), (dy_a, D, 0)],
                                       [small["gdn_out_norm"]], [("new", D, F32), ("into", dp, D, OFF_Z // D)], [(1, DH)])

    def ret_post_bwd(i, o, g, dy, wn):
        _, vjp = jax.vjp(_ret_post, o, g, wn)
        return vjp(dy)
    do_b, dp, gs["ret_out_norm"] = row("ret_post_bwd", ret_post_bwd, [(o_b, D, 0), (p, D, OFF_RG // D), (dy_b, D, 0)],
                                       [small["ret_out_norm"]], [("new", D, F32), ("into", dp, D, OFF_RG // D)], [(1, D)])

    drq, drk, dp = _scan_bwd("ret_scan_bwd", _ret_chunk, ret_blocks, [], ret_consts, ret_states, do_b,
                             [("new",), ("new",), ("into", dp, OFF_RV // D)], nc)

    def rope_bwd(i, drq, drk, ct, st):
        c8, s8 = _tile_heads(ct), _tile_heads(st)
        drk = drk * (DH ** -0.5)
        return jnp.concatenate([drq * c8 + _swap_pairs(drq * s8), drk * c8 + _swap_pairs(drk * s8)], axis=1)
    dp = row("ret_pre_bwd", rope_bwd, [(drq, D, 0), (drk, D, 0), (ct, DH, 0), (st, DH, 0)], [],
             [("into", dp, 2 * D, OFF_RQ // (2 * D))])[0]

    dq, dk, dv, dgcb, dbb, dgc_rows = _scan_bwd("gdn_scan_bwd", _gdn_chunk, gdn_blocks, [gc_rows], [], gdn_states, do_a,
                                                [("new",)] * 5, nc)
    dgc = jnp.pad(dgc_rows.transpose(0, 2, 1).reshape(lp, H), ((0, 0), (H, LANES - 2 * H)))
    dp, dba, dcw, dgate = gdn_pre_bwd(p, cws, alog_row, dtb_row, dq, dk, dv, dgcb, dbb, dgc, dp, lp, tm)
    dp = row("dp_ba", lambda i, t: t, [(dba, LANES, 0)], [], [("into", dp, LANES, OFF_BA // LANES)])[0]
    gw["gdn_conv_w"] = dcw[:CONV_K]
    gs["gdn_a_log"] = dgate[0:1, H:2 * H]
    gs["gdn_dt_bias"] = dgate[1:2, H:2 * H]

    dn2 = matmul("mix_proj_dx", dp, w["w_in_p"], "nt", tj_cap=1024, tr_cap=1152)
    gw["w_in_p"] = matmul("mix_proj_dw", n2, dp, "tn", ti_cap=1024, tj_cap=384)
    dh1, gs["mix_norm"] = norm_bwd("mix_dnorm", h1, small["mix_norm"], dn2, dh2)
    dh0, gw["ffn1_w_in"], gw["ffn1_w_out"], gs["ffn1_norm"] = ffn_bwd("ffn1", dh1, ffn1_saved, small["ffn1_norm"],
                                                                      w["ffn1_w_in"], w["ffn1_w_out"])
    gw["meta_tokens"] = dh0[PAD:HEAD_ROWS]
    return loss_row, dh0[HEAD_ROWS:], gw, gs


BIG = ("ffn1_w_in", "ffn1_w_out", "w_in", "w_branch_gdn", "w_branch_ret", "w_out", "ffn2_w_in", "ffn2_w_out")
COL_SHARDED = ("ffn1_w_in", "w_in", "ffn2_w_in")
SMALL = ("ffn1_norm", "mix_norm", "ret_out_norm", "ffn2_norm", "final_norm", "gdn_out_norm", "gdn_a_log", "gdn_dt_bias")
WEIGHTS = ("meta_tokens", "ffn1_norm", "ffn1_w_in", "ffn1_w_out", "mix_norm", "w_in", "gdn_conv_w", "gdn_a_log", "gdn_dt_bias",
           "gdn_out_norm", "ret_out_norm", "w_branch_gdn", "w_branch_ret", "w_out", "ffn2_norm", "ffn2_w_in", "ffn2_w_out",
           "final_norm")
LOSS_ROW = 6
CONV_ROW0, META_ROW0, SMALL_ROWS = 8, 24, 40


def pack_small(vals):
    rows = [vals[n].reshape(1, D) for n in SMALL[:5]]
    r5 = jnp.concatenate([vals["gdn_out_norm"].reshape(1, DH), vals["gdn_a_log"].reshape(1, H), vals["gdn_dt_bias"].reshape(1, H),
                          jnp.zeros((1, D - DH - 2 * H), F32)], axis=1)
    return jnp.concatenate(rows + [r5, jnp.zeros((2, D), F32)], axis=0)


def unpack_small(packed, shapes):
    out = {n: packed[j].reshape(shapes[n]) for j, n in enumerate(SMALL[:5])}
    out["gdn_out_norm"] = packed[5, :DH].reshape(shapes["gdn_out_norm"])
    out["gdn_a_log"] = packed[5, DH:DH + H].reshape(shapes["gdn_a_log"])
    out["gdn_dt_bias"] = packed[5, DH + H:DH + 2 * H].reshape(shapes["gdn_dt_bias"])
    return out


def kernel(x, meta_tokens, ffn1_norm, ffn1_w_in, ffn1_w_out, mix_norm, w_in, gdn_conv_w, gdn_a_log, gdn_dt_bias, gdn_out_norm, ret_out_norm, w_branch_gdn, w_branch_ret, w_out, ffn2_norm, ffn2_w_in, ffn2_w_out, final_norm, loss_target, m_meta_tokens, m_ffn1_norm, m_ffn1_w_in, m_ffn1_w_out, m_mix_norm, m_w_in, m_gdn_conv_w, m_gdn_a_log, m_gdn_dt_bias, m_gdn_out_norm, m_ret_out_norm, m_w_branch_gdn, m_w_branch_ret, m_w_out, m_ffn2_norm, m_ffn2_w_in, m_ffn2_w_out, m_final_norm, v_meta_tokens, v_ffn1_norm, v_ffn1_w_in, v_ffn1_w_out, v_mix_norm, v_w_in, v_gdn_conv_w, v_gdn_a_log, v_gdn_dt_bias, v_gdn_out_norm, v_ret_out_norm, v_w_branch_gdn, v_w_branch_ret, v_w_out, v_ffn2_norm, v_ffn2_w_in, v_ffn2_w_out, v_final_norm):
    a = dict(locals())
    wts = {n: a[n] for n in WEIGHTS}
    mom_m = {n: a["m_" + n] for n in WEIGHTS}
    mom_v = {n: a["v_" + n] for n in WEIGHTS}
    shapes = {n: wts[n].shape for n in WEIGHTS}
    flat = lambda t: t.reshape(t.shape[-2:])
    c = lax.axis_index("c")
    chip = 2 * lax.axis_index("x") + lax.axis_index("y")

    exact = jnp.concatenate([wts["gdn_conv_w"].reshape(3, D), wts["meta_tokens"].reshape(4, D), jnp.zeros((1, D), F32)], axis=0)
    gathered = allgather_chips("gather_weights", [flat(wts[n]).astype(BF16) for n in BIG] + [exact])
    gat = dict(zip(BIG, gathered[:-1]))
    exact = gathered[-1]
    w = {n: gat[n].reshape(N_CHIPS * gat[n].shape[1], D) for n in BIG if n not in COL_SHARDED}
    w["ffn1_w_in"], w["ffn2_w_in"] = gat["ffn1_w_in"], gat["ffn2_w_in"]
    w["w_in_p"] = _w_in_padded(jnp.concatenate([gat["w_in"][q] for q in range(N_CHIPS)], axis=1))
    w["gdn_conv_w"] = jnp.concatenate([exact[q, 0:3].reshape(CONV_K, 3 * D // 4) for q in range(N_CHIPS)], axis=1)
    w["meta_tokens"] = jnp.concatenate([exact[q, 3:7].reshape(N_META, D // 4) for q in range(N_CHIPS)], axis=1)
    small = {n: wts[n].reshape(1, -1) for n in SMALL}

    loss_row, gx, gw, gs = local_step(x[0], loss_target[0], w, small)

    gblk = {n: gw[n].reshape(N_CHIPS, gw[n].shape[0] // N_CHIPS, D) for n in BIG if n not in COL_SHARDED}
    gblk["ffn1_w_in"], gblk["ffn2_w_in"] = gw["ffn1_w_in"], gw["ffn2_w_in"]
    gblk["w_in"] = jnp.stack(jnp.split(_w_in_unpadded(gw["w_in_p"]), N_CHIPS, axis=1))
    gl = [gblk[n] for n in BIG]
    from_sib = sibling_halves("grads_sibling", gl)
    chip_sum = [chip_sums(f"grads_chip_sum_{n}", g, r, c) for n, g, r in zip(BIG, gl, from_sib)]
    from_chips = scatter_chips("grads_scatter", chip_sum)
    halves = [sum_slots(f"grads_sum_{n}", [(cs, chip), (fc, 0), (fc, 1), (fc, 2)]) for n, cs, fc in zip(BIG, chip_sum, from_chips)]
    joined = sibling_join("grads_join", halves)
    grads, delta, new_m, new_v = {}, {}, {}, {}
    for n, j in zip(BIG, joined):
        g = j.reshape(shapes[n][-2:])
        d_, m_, v_ = adamw(f"adamw_{n}", flat(wts[n]), g, flat(mom_m[n]), flat(mom_v[n]))
        grads[n], delta[n], new_m[n], new_v[n] = (t.reshape(shapes[n]) for t in (g, d_, m_, v_))

    sm = jnp.concatenate([pack_small(gs).at[LOSS_ROW, :LANES].set(loss_row[0]),
                          gw["gdn_conv_w"].reshape(3 * CONV_K, D), jnp.zeros((META_ROW0 - CONV_ROW0 - 3 * CONV_K, D), F32),
                          gw["meta_tokens"]], axis=0)
    every = allgather_all("small_gather", sm)
    sm_sum = sum_slots("small_sum", [(every, s) for s in range(8)])
    d_s, m_s, v_s = adamw("adamw_small", pack_small(small), sm_sum[:8], pack_small({n: mom_m[n].reshape(1, -1) for n in SMALL}),
                          pack_small({n: mom_v[n].reshape(1, -1) for n in SMALL}))
    grads.update(unpack_small(sm_sum, shapes))
    delta.update(unpack_small(d_s, shapes))
    new_m.update(unpack_small(m_s, shapes))
    new_v.update(unpack_small(v_s, shapes))
    g_conv = lax.dynamic_slice_in_dim(sm_sum[CONV_ROW0:CONV_ROW0 + 3 * CONV_K].reshape(CONV_K, 3 * D), chip * (3 * D // 4), 3 * D // 4, 1)
    g_meta = lax.dynamic_slice_in_dim(sm_sum[META_ROW0:META_ROW0 + N_META], chip * (D // 4), D // 4, 1)
    for n, g in (("gdn_conv_w", g_conv), ("meta_tokens", g_meta)):
        d_, m_, v_ = adamw(f"adamw_{n}", flat(wts[n]), g, flat(mom_m[n]), flat(mom_v[n]))
        grads[n], delta[n], new_m[n], new_v[n] = (t.reshape(shapes[n]) for t in (g, d_, m_, v_))
    loss = sm_sum[LOSS_ROW, 0]

    return (loss, gx[None], *[grads[n] for n in WEIGHTS], *[delta[n] for n in WEIGHTS], *[new_m[n] for n in WEIGHTS],
            *[new_v[n] for n in WEIGHTS])
```

```python
import functools

import jax
import jax.numpy as jnp
from jax import lax
from jax.experimental import pallas as pl
from jax.experimental.pallas import tpu as pltpu

F32 = jnp.float32
BF16 = jnp.bfloat16
HI = lax.Precision.HIGHEST
MESH = pl.DeviceIdType.MESH

D = 1024
N_META = 16
PAD = 48
HEAD_ROWS = PAD + N_META
CH = 64
H = 8
DH = 128
DFF = 2816
CONV_K = 4
EPS = 1e-6
ROPE_BASE = 10000.0
LANES = 128
N_CHIPS = 4
VMEM_LIMIT = 56 * 2 ** 20

OFF_QKV, OFF_Z, OFF_RQ, OFF_RK, OFF_RV, OFF_RG, OFF_GA, OFF_GB, OFF_BA = 0, 3072, 4096, 5120, 6144, 7168, 8192, 9216, 10240
PW = 10368
D_PROJ = 10256

ADAM_LR, ADAM_B1, ADAM_B2, ADAM_EPS, ADAM_WD, ADAM_STEP = 0.001, 0.9, 0.999, 1e-08, 0.01, 10


def _pick(n, cap, mult):
    best = None
    for t in range(mult, min(n, cap) + 1, mult):
        if n % t == 0:
            best = t
    return best if best is not None else n


def _tile_rows(rows, cols, block_bytes=3 * 2 ** 19):
    return _pick(rows, max(8, block_bytes // (4 * cols)), 8)


def _dot(a, b, dims, prec=None):
    return lax.dot_general(a, b, (dims, ((), ())), precision=prec, preferred_element_type=F32)


def _mm(a, b):
    return _dot(a.astype(BF16), b.astype(BF16), ((1,), (0,)))


def _mm_nt(a, b):
    return _dot(a.astype(BF16), b.astype(BF16), ((1,), (1,)))


def _mm_tn(a, b):
    return _dot(a.astype(BF16), b.astype(BF16), ((0,), (0,)))


def _mmh(a, b):
    return _dot(a, b, ((1,), (0,)), HI)


def _split(a):
    hi = a.astype(BF16)
    return hi, (a - hi.astype(F32)).astype(BF16)


_NN, _NT, _TN = ((2,), (1,)), ((2,), (2,)), ((1,), (1,))


def _bdot(a, b, dims):
    return lax.dot_general(a, b, (dims, ((0,), (0,))), preferred_element_type=F32)


def _bmm(a, b, dims):
    return _bdot(a.astype(BF16), b.astype(BF16), dims)


def _bdot3(a, b, dims):
    ah, al = _split(a)
    bh, bl = _split(b)
    return _bdot(ah, bh, dims) + (_bdot(ah, bl, dims) + _bdot(al, bh, dims))


def matmul(name, a, b, mode, *, out_dtype=F32, resid=None, alpha=1.0, ti_cap=688, tj_cap=512, tr_cap=1408,
           b_split=False, o_split=False):
    if mode == "nn":
        I, R = a.shape
        J = N_CHIPS * b.shape[2] if b_split else b.shape[1]
    elif mode == "nt":
        I, R = a.shape
        J = b.shape[1] if b_split else b.shape[0]
    else:
        R, I = a.shape
        J = b.shape[1]
    ti = _pick(I, ti_cap, 16) if mode != "tn" else _pick(I, ti_cap, LANES)
    tj = _pick(J, tj_cap, LANES)
    tr = _pick(R, tr_cap, LANES) if mode != "tn" else _pick(R, tr_cap, 16)
    if mode == "nn":
        a_spec = pl.BlockSpec((ti, tr), lambda i, j, r: (i, r))
        if b_split:
            tj = J // N_CHIPS
            b_spec = pl.BlockSpec((None, tr, tj), lambda i, j, r: (j, r, 0))
        else:
            b_spec = pl.BlockSpec((tr, tj), lambda i, j, r: (r, j))
        dims = ((1,), (0,))
    elif mode == "nt":
        if b_split:
            tr = R // N_CHIPS
            b_spec = pl.BlockSpec((None, tj, tr), lambda i, j, r: (r, j, 0))
        else:
            b_spec = pl.BlockSpec((tj, tr), lambda i, j, r: (j, r))
        a_spec = pl.BlockSpec((ti, tr), lambda i, j, r: (i, r))
        dims = ((1,), (1,))
    else:
        if o_split:
            tj = J // N_CHIPS
        a_spec = pl.BlockSpec((tr, ti), lambda i, j, r: (r, i))
        b_spec = pl.BlockSpec((tr, tj), lambda i, j, r: (r, j))
        dims = ((0,), (0,))
    nr = R // tr
    assert I % ti == 0 and J % tj == 0 and R % tr == 0, (name, I, J, R, ti, tj, tr)
    if o_split:
        o_spec = pl.BlockSpec((None, ti, tj), lambda i, j, r: (j, i, 0))
        out_shape = jax.ShapeDtypeStruct((N_CHIPS, I, tj), out_dtype)
    else:
        o_spec = pl.BlockSpec((ti, tj), lambda i, j, r: (i, j))
        out_shape = jax.ShapeDtypeStruct((I, J), out_dtype)
    has_resid = resid is not None

    def finish(o_ref, r_ref, res):
        res = res * alpha if alpha != 1.0 else res
        if has_resid:
            res = r_ref[...] + res
        o_ref[...] = res.astype(o_ref.dtype)

    def body(*refs):
        a_ref, b_ref = refs[:2]
        r_ref = refs[2] if has_resid else None
        o_ref = refs[3] if has_resid else refs[2]
        prod = _dot(a_ref[...].astype(BF16), b_ref[...].astype(BF16), dims)
        if nr == 1:
            finish(o_ref, r_ref, prod)
            return
        acc = refs[-1]
        r = pl.program_id(2)

        @pl.when(r == 0)
        def _():
            acc[...] = prod

        @pl.when(r > 0)
        def _():
            acc[...] += prod

        @pl.when(r == nr - 1)
        def _():
            finish(o_ref, r_ref, acc[...])

    ins = [a, b] + ([resid] if has_resid else [])
    specs = [a_spec, b_spec] + ([o_spec] if has_resid else [])
    return pl.pallas_call(
        body, name=name, grid=(I // ti, J // tj, nr), in_specs=specs, out_specs=o_spec, out_shape=out_shape,
        scratch_shapes=[pltpu.VMEM((ti, tj), F32)] if nr > 1 else [],
        compiler_params=pltpu.CompilerParams(dimension_semantics=("parallel", "parallel", "arbitrary"),
                                             vmem_limit_bytes=VMEM_LIMIT),
    )(*ins)


def rowwise(name, fn, rows, pars, outs, accs=(), *, n_rows, tm):
    nt = n_rows // tm
    assert nt * tm == n_rows
    in_specs, ins = [], []
    for arr, w, cb in rows:
        in_specs.append(pl.BlockSpec((tm, w), lambda i, cb=cb: (i, cb)))
        ins.append(arr)
    for p in pars:
        in_specs.append(pl.BlockSpec(p.shape, lambda i, nd=p.ndim: (0,) * nd))
        ins.append(p)
    n_in = len(ins)
    out_specs, out_shapes, aliases = [], [], {}
    for k, o in enumerate(outs):
        if o[0] == "new":
            _, w, dt = o
            out_specs.append(pl.BlockSpec((tm, w), lambda i: (i, 0)))
            out_shapes.append(jax.ShapeDtypeStruct((n_rows, w), dt))
        else:
            _, arr, w, cb = o
            in_specs.append(pl.BlockSpec(memory_space=pl.ANY))
            aliases[len(ins)] = k
            ins.append(arr)
            out_specs.append(pl.BlockSpec((tm, w), lambda i, cb=cb: (i, cb)))
            out_shapes.append(jax.ShapeDtypeStruct(arr.shape, arr.dtype))
    for r, w in accs:
        out_specs.append(pl.BlockSpec((r, w), lambda i: (0, 0)))
        out_shapes.append(jax.ShapeDtypeStruct((r, w), F32))
    n_all_in, n_out, n_acc = len(ins), len(outs), len(accs)

    def body(*refs):
        i = pl.program_id(0)
        vals = [r[...] for r in refs[:n_in]]
        res = fn(i, *vals)
        if not isinstance(res, (tuple, list)):
            res = (res,)
        o_refs = refs[n_all_in:n_all_in + n_out]
        a_refs = refs[n_all_in + n_out:]
        for r, v in zip(o_refs, res[:n_out]):
            r[...] = v.astype(r.dtype)
        if n_acc:
            @pl.when(i == 0)
            def _():
                for r in a_refs:
                    r[...] = jnp.zeros_like(r)
            for r, v in zip(a_refs, res[n_out:]):
                r[...] += v

    return pl.pallas_call(
        body, name=name, grid=(nt,), in_specs=in_specs, out_specs=out_specs, out_shape=out_shapes,
        input_output_aliases=aliases,
        compiler_params=pltpu.CompilerParams(dimension_semantics=("arbitrary",), vmem_limit_bytes=VMEM_LIMIT),
    )(*ins)


def _row_ids(i, tm):
    return i * tm + lax.broadcasted_iota(jnp.int32, (tm, 1), 0)


def _sigmoid(x):
    return 1.0 / (1.0 + jnp.exp(-x))


def _silu(x):
    return x * _sigmoid(x)


def _softplus(x):
    return jnp.maximum(x, 0.0) + jnp.log(1.0 + jnp.exp(-jnp.abs(x)))


def _rms(x, w):
    return x * lax.rsqrt(jnp.mean(x * x, axis=-1, keepdims=True) + EPS) * w


def _heads(fn, *xs):
    return jnp.concatenate([fn(h, *[x[:, h * DH:(h + 1) * DH] for x in xs]) for h in range(H)], axis=1)


def _swiglu(gu):
    return _silu(gu[:, :DFF]) * gu[:, DFF:]


def _gdn_post(o, z, w):
    return _heads(lambda h, oh, zh: oh * lax.rsqrt(jnp.mean(oh * oh, axis=-1, keepdims=True) + EPS) * w * _silu(zh), o, z)


def _ret_post(o, rg, w):
    def one(h, oh, gh, wh):
        mu = jnp.mean(oh, axis=-1, keepdims=True)
        xc = oh - mu
        var = jnp.mean(xc * xc, axis=-1, keepdims=True)
        return _silu(gh) * (xc * lax.rsqrt(var + EPS) * wh)
    return _heads(one, o, rg, jnp.broadcast_to(w, (o.shape[0], D)))


def _merge(a, b, ga, gb):
    return _sigmoid(ga) * a + _sigmoid(gb) * b


def _select_matrix(first_lane):
    r = lax.broadcasted_iota(jnp.int32, (LANES, H * DH), 0)
    c = lax.broadcasted_iota(jnp.int32, (LANES, H * DH), 1)
    return (r == first_lane + (c >> 7)).astype(F32)


def _chunk_tri(tm):
    r = lax.broadcasted_iota(jnp.int32, (tm, tm), 0)
    c = lax.broadcasted_iota(jnp.int32, (tm, tm), 1)
    return jnp.logical_and((r >> 6) == (c >> 6), r >= c).astype(F32)


def _gdn_gates(ba, alog_row, dtb_row, mask):
    g = -jnp.exp(alog_row) * _softplus(ba + dtb_row) * mask
    gc = _mmh(_chunk_tri(ba.shape[0]), g)
    beta = _sigmoid(ba) * mask
    return _mmh(gc, _select_matrix(H)), _mmh(beta, _select_matrix(0)), gc


def _gdn_qkv(c):
    a = _silu(c)

    def l2(scale):
        return lambda h, t: t * lax.rsqrt(jnp.sum(t * t, axis=-1, keepdims=True) + EPS) * scale
    q = _heads(l2(DH ** -0.5), a[:, :D])
    k = _heads(l2(1.0), a[:, D:2 * D])
    return q, k, a[:, 2 * D:]


def _conv_taps(xw, cws, tm):
    return sum(cws[i] * xw[5 + i:5 + i + tm] for i in range(CONV_K))


def _swap_pairs(t):
    n = t.shape[1]
    lane = lax.broadcasted_iota(jnp.int32, t.shape, 1)
    return jnp.where((lane & 1) == 0, pltpu.roll(t, n - 1, 1), pltpu.roll(t, 1, 1))


def _tile_heads(t):
    return jnp.concatenate([t] * H, axis=1)


@jax.custom_vjp
def _unit_lower_inv(a):
    n = -a
    eye = (lax.broadcasted_iota(jnp.int32, a.shape, 1) == lax.broadcasted_iota(jnp.int32, a.shape, 2)).astype(F32)
    p = eye + n
    for _ in range(5):
        n = _bdot3(n, n, _NN)
        p = p + _bdot3(p, n, _NN)
    return p


def _inv_fwd(a):
    t = _unit_lower_inv(a)
    return t, t


def _inv_bwd(t, dt):
    x = _bdot3(t, dt, _TN)
    return (-_bdot3(x, t, _NT),)


_unit_lower_inv.defvjp(_inv_fwd, _inv_bwd)


def _gdn_chunk(q, k, v, gc, bb, gr, s):
    ri = lax.broadcasted_iota(jnp.int32, (H, CH, CH), 1)
    ci = lax.broadcasted_iota(jnp.int32, (H, CH, CH), 2)
    causal = ri >= ci
    gc1 = jnp.sum(gc, axis=2, keepdims=True) * (1.0 / LANES)
    diff = jnp.broadcast_to(gc1, (H, CH, CH)) - jnp.broadcast_to(gr, (H, CH, CH))
    decay = jnp.where(causal, jnp.exp(jnp.where(causal, diff, 0.0)), 0.0)
    kb = k * bb
    sc = _bmm(jnp.concatenate([q, kb], axis=1), k, _NT)
    qk = sc[:, :CH] * decay
    a = jnp.where(ri > ci, sc[:, CH:] * decay, 0.0)
    t = _unit_lower_inv(a)
    eg = jnp.exp(gc)
    uw = _bmm(t, jnp.concatenate([v * bb, kb * eg], axis=2), _NN)
    g_last = gc[:, CH - 1:CH, :]
    ws = _bmm(jnp.concatenate([uw[:, :, DH:], q * eg], axis=1), s, _NN)
    v_new = uw[:, :, :DH] - ws[:, :CH]
    o = ws[:, CH:] + _bmm(qk, v_new, _NN)
    s_new = s * jnp.exp(g_last) + _bmm(k * jnp.exp(g_last - gc), v_new, _TN)
    return o, s_new


def _ret_chunk(q, k, v, s, decay, xi, zeta, cd):
    scores = _bmm(q, k, _NT) * decay
    o = _bmm(scores, v, _NN) + _bmm(q * xi, s, _NN)
    s_new = s * cd + _bmm(k * zeta, v, _TN)
    return o, s_new


def _head_blocks(ref):
    return jnp.stack([ref[:, h * DH:(h + 1) * DH] for h in range(H)])


def _head_rows(ref):
    return jnp.stack([ref[0, h:h + 1, :] for h in range(H)])


def _scan_fwd(name, chunk_fn, blocks, rowvecs, consts, nc):
    n_blk, n_rv, n_c = len(blocks), len(rowvecs), len(consts)

    def body(*refs):
        blk = refs[:n_blk]
        rvs = refs[n_blk:n_blk + n_rv]
        cst = refs[n_blk + n_rv:n_blk + n_rv + n_c]
        o_ref, st_ref, s_scr = refs[n_blk + n_rv + n_c:]

        @pl.when(pl.program_id(0) == 0)
        def _():
            s_scr[...] = jnp.zeros_like(s_scr)

        s = s_scr[...]
        st_ref[0] = s
        o, s_new = chunk_fn(*[_head_blocks(r) for r in blk], *[_head_rows(r) for r in rvs], s, *[r[...] for r in cst])
        for h in range(H):
            o_ref[:, h * DH:(h + 1) * DH] = o[h]
        s_scr[...] = s_new

    in_specs = [pl.BlockSpec((CH, H * DH), lambda c, f=f: (c, f)) for _, f in blocks]
    in_specs += [pl.BlockSpec((1, H, CH), lambda c: (c, 0, 0)) for _ in rowvecs]
    in_specs += [pl.BlockSpec(a.shape, lambda c: (0, 0, 0)) for a in consts]
    return pl.pallas_call(
        body, name=name, grid=(nc,), in_specs=in_specs,
        out_specs=[pl.BlockSpec((CH, H * DH), lambda c: (c, 0)), pl.BlockSpec((1, H, DH, DH), lambda c: (c, 0, 0, 0))],
        out_shape=[jax.ShapeDtypeStruct((nc * CH, H * DH), F32), jax.ShapeDtypeStruct((nc, H, DH, DH), F32)],
        scratch_shapes=[pltpu.VMEM((H, DH, DH), F32)],
        compiler_params=pltpu.CompilerParams(dimension_semantics=("arbitrary",), vmem_limit_bytes=VMEM_LIMIT),
    )(*[a for a, _ in blocks], *rowvecs, *consts)


def _scan_bwd(name, chunk_fn, blocks, rowvecs, consts, states, do, outs, nc):
    n_blk, n_rv, n_c = len(blocks), len(rowvecs), len(consts)
    n_into = sum(1 for o in outs if o[0] == "into")

    def body(*refs):
        blk = refs[:n_blk]
        rvs = refs[n_blk:n_blk + n_rv]
        cst = refs[n_blk + n_rv:n_blk + n_rv + n_c]
        st_ref, do_ref = refs[n_blk + n_rv + n_c:n_blk + n_rv + n_c + 2]
        n_in = n_blk + n_rv + n_c + 2 + n_into
        o_refs = refs[n_in:n_in + n_blk]
        rv_refs = refs[n_in + n_blk:n_in + n_blk + n_rv]
        ds_scr = refs[n_in + n_blk + n_rv]

        @pl.when(pl.program_id(0) == 0)
        def _():
            ds_scr[...] = jnp.zeros_like(ds_scr)

        cv = [r[...] for r in cst]
        _, vjp = jax.vjp(lambda *a: chunk_fn(*a, *cv), *[_head_blocks(r) for r in blk], *[_head_rows(r) for r in rvs], st_ref[0])
        grads = vjp((_head_blocks(do_ref), ds_scr[...]))
        for h in range(H):
            for r, g in zip(o_refs, grads[:n_blk]):
                r[:, h * DH:(h + 1) * DH] = g[h]
            for r, g in zip(rv_refs, grads[n_blk:n_blk + n_rv]):
                r[0, h:h + 1, :] = g[h]
        ds_scr[...] = grads[n_blk + n_rv]

    rc = lambda c: nc - 1 - c
    in_specs = [pl.BlockSpec((CH, H * DH), lambda c, f=f: (rc(c), f)) for _, f in blocks]
    in_specs += [pl.BlockSpec((1, H, CH), lambda c: (rc(c), 0, 0)) for _ in rowvecs]
    in_specs += [pl.BlockSpec(a.shape, lambda c: (0, 0, 0)) for a in consts]
    in_specs += [pl.BlockSpec((1, H, DH, DH), lambda c: (rc(c), 0, 0, 0)), pl.BlockSpec((CH, H * DH), lambda c: (rc(c), 0))]
    ins = [a for a, _ in blocks] + list(rowvecs) + list(consts) + [states, do]
    out_specs, out_shapes, aliases = [], [], {}
    for k, o in enumerate(outs):
        if o[0] == "new":
            out_specs.append(pl.BlockSpec((CH, H * DH), lambda c: (rc(c), 0)))
            out_shapes.append(jax.ShapeDtypeStruct((nc * CH, H * DH), F32))
        else:
            _, arr, f = o
            in_specs.append(pl.BlockSpec(memory_space=pl.ANY))
            aliases[len(ins)] = k
            ins.append(arr)
            out_specs.append(pl.BlockSpec((CH, H * DH), lambda c, f=f: (rc(c), f)))
            out_shapes.append(jax.ShapeDtypeStruct(arr.shape, arr.dtype))
    for _ in rowvecs:
        out_specs.append(pl.BlockSpec((1, H, CH), lambda c: (rc(c), 0, 0)))
        out_shapes.append(jax.ShapeDtypeStruct((nc, H, CH), F32))
    return pl.pallas_call(
        body, name=name, grid=(nc,), in_specs=in_specs, out_specs=out_specs, out_shape=out_shapes,
        input_output_aliases=aliases, scratch_shapes=[pltpu.VMEM((H, DH, DH), F32)],
        compiler_params=pltpu.CompilerParams(dimension_semantics=("arbitrary",), vmem_limit_bytes=VMEM_LIMIT),
    )(*ins)


def _gdn_pre_specs(p, cws, alog_row, dtb_row, tm, pos):
    sub = tm // 8
    rows = [pl.BlockSpec((tm, 3 * D), lambda i: (pos(i), 0)),
            pl.BlockSpec((8, 3 * D), lambda i: (jnp.maximum(pos(i) * sub - 1, 0), 0)),
            pl.BlockSpec((tm, LANES), lambda i: (pos(i), OFF_BA // LANES))]
    pars = [pl.BlockSpec(a.shape, lambda i: (0, 0)) for a in (*cws, alog_row, dtb_row)]
    return rows + pars, [p, p, p, *cws, alog_row, dtb_row]


def gdn_pre_fwd(p, cws, alog_row, dtb_row, lp, tm):
    def body(x_ref, prev_ref, ba_ref, c0, c1, c2, c3, al_ref, dt_ref, q_ref, k_ref, v_ref, g_ref, b_ref, gc_ref):
        i = pl.program_id(0)
        prev = jnp.where(i > 0, prev_ref[...], 0.0)
        xw = jnp.concatenate([prev, x_ref[...]], axis=0)
        c = _conv_taps(xw, [c0[...], c1[...], c2[...], c3[...]], tm)
        q, k, v = _gdn_qkv(c)
        mask = (_row_ids(i, tm) >= PAD).astype(F32)
        g, b, gc = _gdn_gates(ba_ref[...], al_ref[...], dt_ref[...], mask)
        q_ref[...] = q
        k_ref[...] = k
        v_ref[...] = v
        g_ref[...] = g
        b_ref[...] = b
        gc_ref[...] = gc

    in_specs, ins = _gdn_pre_specs(p, cws, alog_row, dtb_row, tm, lambda i: i)
    o_spec = pl.BlockSpec((tm, D), lambda i: (i, 0))
    return pl.pallas_call(
        body, name="gdn_pre_fwd", grid=(lp // tm,), in_specs=in_specs,
        out_specs=[o_spec] * 5 + [pl.BlockSpec((tm, LANES), lambda i: (i, 0))],
        out_shape=[jax.ShapeDtypeStruct((lp, D), F32)] * 5 + [jax.ShapeDtypeStruct((lp, LANES), F32)],
        compiler_params=pltpu.CompilerParams(dimension_semantics=("parallel",), vmem_limit_bytes=VMEM_LIMIT),
    )(*ins)


def gdn_pre_bwd(p, cws, alog_row, dtb_row, dq, dk, dv, dg, db, dgc, dp, lp, tm):
    nt = lp // tm
    pos = lambda i: nt - 1 - i

    def body(x_ref, prev_ref, ba_ref, c0, c1, c2, c3, al_ref, dt_ref, dq_ref, dk_ref, dv_ref, dg_ref, db_ref, dgc_ref,
             dp_any, dx_ref, dba_ref, dcw_ref, dpar_ref, carry):
        i = pl.program_id(0)
        t = pos(i)

        @pl.when(i == 0)
        def _():
            carry[...] = jnp.zeros_like(carry)
            dcw_ref[...] = jnp.zeros_like(dcw_ref)
            dpar_ref[...] = jnp.zeros_like(dpar_ref)

        cws_v = [c0[...], c1[...], c2[...], c3[...]]
        prev = jnp.where(t > 0, prev_ref[...], 0.0)
        xw = jnp.concatenate([prev, x_ref[...]], axis=0)
        c = _conv_taps(xw, cws_v, tm)
        _, vjp_qkv = jax.vjp(_gdn_qkv, c)
        (dc,) = vjp_qkv((dq_ref[...], dk_ref[...], dv_ref[...]))
        zeros8 = jnp.zeros((8, 3 * D), F32)
        dcp = jnp.concatenate([zeros8, dc, zeros8], axis=0)
        dxw = sum(cws_v[j] * dcp[3 - j:3 - j + tm + 8] for j in range(CONV_K))
        dx_ref[...] = dxw[8:]
        dx_ref[tm - 8:tm, :] += carry[...]
        carry[...] = dxw[:8]
        for j in range(CONV_K):
            dcw_ref[j:j + 1, :] += jnp.sum(dc * xw[5 + j:5 + j + tm], axis=0, keepdims=True)
        mask = (_row_ids(t, tm) >= PAD).astype(F32)
        _, vjp_g = jax.vjp(lambda ba, al, dt: _gdn_gates(ba, al, dt, mask), ba_ref[...], al_ref[...], dt_ref[...])
        dba, dal, ddt = vjp_g((dg_ref[...], db_ref[...], dgc_ref[...]))
        dba_ref[...] = dba
        dpar_ref[0:1, :] += dal
        dpar_ref[1:2, :] += ddt

    in_specs, ins = _gdn_pre_specs(p, cws, alog_row, dtb_row, tm, pos)
    g_spec = pl.BlockSpec((tm, D), lambda i: (pos(i), 0))
    s_spec = pl.BlockSpec((tm, LANES), lambda i: (pos(i), 0))
    in_specs += [g_spec] * 5 + [s_spec, pl.BlockSpec(memory_space=pl.ANY)]
    ins += [dq, dk, dv, dg, db, dgc, dp]
    return pl.pallas_call(
        body, name="gdn_pre_bwd", grid=(nt,), in_specs=in_specs,
        out_specs=[pl.BlockSpec((tm, 3 * D), lambda i: (pos(i), 0)), s_spec,
                   pl.BlockSpec((8, 3 * D), lambda i: (0, 0)), pl.BlockSpec((8, LANES), lambda i: (0, 0))],
        out_shape=[jax.ShapeDtypeStruct(dp.shape, F32), jax.ShapeDtypeStruct((lp, LANES), F32),
                   jax.ShapeDtypeStruct((8, 3 * D), F32), jax.ShapeDtypeStruct((8, LANES), F32)],
        input_output_aliases={len(ins) - 1: 0},
        scratch_shapes=[pltpu.VMEM((8, 3 * D), F32)],
        compiler_params=pltpu.CompilerParams(dimension_semantics=("arbitrary",), vmem_limit_bytes=VMEM_LIMIT),
    )(*ins)


def _me():
    return lax.axis_index("x"), lax.axis_index("y"), lax.axis_index("c")


def _any_specs(n):
    return [pl.BlockSpec(memory_space=pl.ANY)] * n


def allgather_chips(name, ws):
    n = len(ws)

    def body(*refs):
        w_refs, o_refs = refs[:n], refs[n:2 * n]
        send_sems, recv_sems, local_sems = refs[2 * n:]
        x, y, c = _me()
        me = 2 * x + y
        chips = [(1 - x, y), (x, 1 - y), (1 - x, 1 - y)]
        mine = [pltpu.make_async_copy(w_refs[a], o_refs[a].at[me], local_sems.at[a]) for a in range(n)]
        for cp in mine:
            cp.start()

        def copy(a, j, slot):
            px, py = chips[j]
            return pltpu.make_async_remote_copy(src_ref=w_refs[a], dst_ref=o_refs[a].at[slot], send_sem=send_sems.at[a, j],
                                                recv_sem=recv_sems.at[a, j], device_id=(px, py, c), device_id_type=MESH)
        sends = [copy(a, j, me) for a in range(n) for j in range(3)]
        for cp in sends:
            cp.start()
        for a in range(n):
            for j, (px, py) in enumerate(chips):
                copy(a, j, 2 * px + py).wait_recv()
        for cp in sends:
            cp.wait_send()
        for cp in mine:
            cp.wait()

    return pl.pallas_call(
        body, name=name, out_shape=[jax.ShapeDtypeStruct((N_CHIPS,) + w.shape, w.dtype) for w in ws],
        in_specs=_any_specs(n), out_specs=_any_specs(n),
        scratch_shapes=[pltpu.SemaphoreType.DMA((n, 3)), pltpu.SemaphoreType.DMA((n, 3)), pltpu.SemaphoreType.DMA((n,))],
    )(*ws)


def allgather_chips_two_level(name, ws):
    n = len(ws)

    def body(*refs):
        w_refs, o_refs = refs[:n], refs[n:2 * n]
        send_sems, recv_sems, fsend_sems, frecv_sems, local_sems = refs[2 * n:]
        x, y, c = _me()
        me = 2 * x + y
        chips = [(1 - x, y), (x, 1 - y), (1 - x, 1 - y)]
        mine = [pltpu.make_async_copy(w_refs[a], o_refs[a].at[me], local_sems.at[a]) for a in range(n)]
        for cp in mine:
            cp.start()

        def rows(a, cc):
            rh = ws[a].shape[0] // 2
            return pl.ds(pl.multiple_of(cc * rh, 8), rh)

        def ici(a, j, slot):
            px, py = chips[j]
            return pltpu.make_async_remote_copy(
                src_ref=w_refs[a].at[rows(a, c)], dst_ref=o_refs[a].at[slot, rows(a, c)], send_sem=send_sems.at[a, j],
                recv_sem=recv_sems.at[a, j], device_id=(px, py, c), device_id_type=MESH)

        def d2d(a, j, cc):
            px, py = chips[j]
            blk = o_refs[a].at[2 * px + py, rows(a, cc)]
            return pltpu.make_async_remote_copy(src_ref=blk, dst_ref=blk, send_sem=fsend_sems.at[a, j],
                                                recv_sem=frecv_sems.at[a, j], device_id=(x, y, 1 - c), device_id_type=MESH)
        sends = [ici(a, j, me) for a in range(n) for j in range(3)]
        for cp in sends:
            cp.start()
        passed = []
        for a in range(n):
            for j, (px, py) in enumerate(chips):
                ici(a, j, 2 * px + py).wait_recv()
                cp = d2d(a, j, c)
                cp.start()
                passed.append(cp)
        for a in range(n):
            for j in range(3):
                d2d(a, j, 1 - c).wait_recv()
        for cp in sends + passed:
            cp.wait_send()
        for cp in mine:
            cp.wait()

    sem = lambda *s: pltpu.SemaphoreType.DMA(s)
    return pl.pallas_call(
        body, name=name, out_shape=[jax.ShapeDtypeStruct((N_CHIPS,) + w.shape, w.dtype) for w in ws],
        in_specs=_any_specs(n), out_specs=_any_specs(n),
        scratch_shapes=[sem(n, 3), sem(n, 3), sem(n, 3), sem(n, 3), sem(n)],
    )(*ws)


def sibling_halves(name, gs):
    n = len(gs)

    def body(*refs):
        g_refs, o_refs = refs[:n], refs[n:2 * n]
        send_sems, recv_sems = refs[2 * n:]
        x, y, c = _me()
        cps = []
        for a in range(n):
            rh = gs[a].shape[1] // 2
            for q in range(N_CHIPS):
                cps.append(pltpu.make_async_remote_copy(
                    src_ref=g_refs[a].at[q, pl.ds(pl.multiple_of((1 - c) * rh, 8), rh)], dst_ref=o_refs[a].at[q],
                    send_sem=send_sems.at[a, q], recv_sem=recv_sems.at[a, q], device_id=(x, y, 1 - c), device_id_type=MESH))
        for cp in cps:
            cp.start()
        for cp in cps:
            cp.wait_recv()
        for cp in cps:
            cp.wait_send()

    return pl.pallas_call(
        body, name=name, out_shape=[jax.ShapeDtypeStruct((N_CHIPS, g.shape[1] // 2, g.shape[2]), g.dtype) for g in gs],
        in_specs=_any_specs(n), out_specs=_any_specs(n),
        scratch_shapes=[pltpu.SemaphoreType.DMA((n, N_CHIPS)), pltpu.SemaphoreType.DMA((n, N_CHIPS))],
    )(*gs)


def scatter_chips(name, css):
    n = len(css)

    def body(*refs):
        c_refs, o_refs = refs[:n], refs[n:2 * n]
        send_sems, recv_sems = refs[2 * n:]
        x, y, c = _me()
        chips = [(1 - x, y), (x, 1 - y), (1 - x, 1 - y)]
        cps = []
        for a in range(n):
            for j, (px, py) in enumerate(chips):
                cps.append(pltpu.make_async_remote_copy(
                    src_ref=c_refs[a].at[2 * px + py], dst_ref=o_refs[a].at[j], send_sem=send_sems.at[a, j],
                    recv_sem=recv_sems.at[a, j], device_id=(px, py, c), device_id_type=MESH))
        for cp in cps:
            cp.start()
        for cp in cps:
            cp.wait_recv()
        for cp in cps:
            cp.wait_send()

    return pl.pallas_call(
        body, name=name, out_shape=[jax.ShapeDtypeStruct((3,) + cs.shape[1:], cs.dtype) for cs in css],
        in_specs=_any_specs(n), out_specs=_any_specs(n),
        scratch_shapes=[pltpu.SemaphoreType.DMA((n, 3)), pltpu.SemaphoreType.DMA((n, 3))],
    )(*css)


def sibling_join(name, halves):
    n = len(halves)

    def body(*refs):
        h_refs, o_refs = refs[:n], refs[n:2 * n]
        send_sems, recv_sems, local_sems = refs[2 * n:]
        x, y, c = _me()
        mine = [pltpu.make_async_copy(h_refs[a], o_refs[a].at[c], local_sems.at[a]) for a in range(n)]
        for cp in mine:
            cp.start()

        def copy(a, slot):
            return pltpu.make_async_remote_copy(src_ref=h_refs[a], dst_ref=o_refs[a].at[slot], send_sem=send_sems.at[a],
                                                recv_sem=recv_sems.at[a], device_id=(x, y, 1 - c), device_id_type=MESH)
        sends = [copy(a, c) for a in range(n)]
        for cp in sends:
            cp.start()
        for a in range(n):
            copy(a, 1 - c).wait_recv()
        for cp in sends:
            cp.wait_send()
        for cp in mine:
            cp.wait()

    return pl.pallas_call(
        body, name=name, out_shape=[jax.ShapeDtypeStruct((2,) + h.shape, h.dtype) for h in halves],
        in_specs=_any_specs(n), out_specs=_any_specs(n),
        scratch_shapes=[pltpu.SemaphoreType.DMA((n,)), pltpu.SemaphoreType.DMA((n,)), pltpu.SemaphoreType.DMA((n,))],
    )(*halves)


def allgather_all(name, s):
    def body(s_ref, out_ref, send_sems, recv_sems, local_sem):
        x, y, c = _me()
        peers = [(x ^ ((m >> 2) & 1), y ^ ((m >> 1) & 1), c ^ (m & 1)) for m in range(1, 8)]
        mine = pltpu.make_async_copy(s_ref, out_ref.at[4 * x + 2 * y + c], local_sem)
        mine.start()

        def copy(j, slot):
            return pltpu.make_async_remote_copy(src_ref=s_ref, dst_ref=out_ref.at[slot], send_sem=send_sems.at[j],
                                                recv_sem=recv_sems.at[j], device_id=peers[j], device_id_type=MESH)
        sends = [copy(j, 4 * x + 2 * y + c) for j in range(7)]
        for cp in sends:
            cp.start()
        for j, (px, py, pc) in enumerate(peers):
            copy(j, 4 * px + 2 * py + pc).wait_recv()
        for cp in sends:
            cp.wait_send()
        mine.wait()

    return pl.pallas_call(
        body, name=name, out_shape=jax.ShapeDtypeStruct((8,) + s.shape, s.dtype),
        in_specs=_any_specs(1), out_specs=pl.BlockSpec(memory_space=pl.ANY),
        scratch_shapes=[pltpu.SemaphoreType.DMA((7,)), pltpu.SemaphoreType.DMA((7,)), pltpu.SemaphoreType.DMA(())],
    )(s)


def sum_slots(name, parts):
    n = len(parts)
    R, W = parts[0][0].shape[1:]
    tm = _tile_rows(R, W)
    idx = jnp.stack([jnp.asarray(s, jnp.int32) for _, s in parts])

    def body(idx_ref, *refs):
        acc = refs[0][...].astype(F32)
        for r in refs[1:n]:
            acc = acc + r[...].astype(F32)
        refs[n][...] = acc

    grid_spec = pltpu.PrefetchScalarGridSpec(
        num_scalar_prefetch=1, grid=(R // tm,),
        in_specs=[pl.BlockSpec((None, tm, W), lambda i, idx, k=k: (idx[k], i, 0)) for k in range(n)],
        out_specs=pl.BlockSpec((tm, W), lambda i, idx: (i, 0)))
    return pl.pallas_call(body, name=name, grid_spec=grid_spec, out_shape=jax.ShapeDtypeStruct((R, W), F32),
                          compiler_params=pltpu.CompilerParams(dimension_semantics=("parallel",)))(idx, *[a for a, _ in parts])


def chip_sums(name, g, recv, c):
    _, R, W = g.shape
    rh = R // 2
    tm = _tile_rows(rh, W)
    g8 = g.reshape(2 * N_CHIPS, rh, W)
    idx = jnp.asarray(c, jnp.int32).reshape(1)

    def body(idx_ref, g_ref, r_ref, o_ref):
        o_ref[...] = (g_ref[...] + r_ref[...]).astype(o_ref.dtype)

    grid_spec = pltpu.PrefetchScalarGridSpec(
        num_scalar_prefetch=1, grid=(N_CHIPS, rh // tm),
        in_specs=[pl.BlockSpec((None, tm, W), lambda q, i, idx: (2 * q + idx[0], i, 0)),
                  pl.BlockSpec((None, tm, W), lambda q, i, idx: (q, i, 0))],
        out_specs=pl.BlockSpec((None, tm, W), lambda q, i, idx: (q, i, 0)))
    return pl.pallas_call(body, name=name, grid_spec=grid_spec, out_shape=jax.ShapeDtypeStruct((N_CHIPS, rh, W), BF16),
                          compiler_params=pltpu.CompilerParams(dimension_semantics=("parallel", "parallel")))(idx, g8, recv)


def adamw(name, w, g, m, v):
    def fn(i, w, g, m, v):
        m = ADAM_B1 * m + (1.0 - ADAM_B1) * g
        v = ADAM_B2 * v + (1.0 - ADAM_B2) * (g * g)
        m_hat = m / (1.0 - ADAM_B1 ** ADAM_STEP)
        v_hat = v / (1.0 - ADAM_B2 ** ADAM_STEP)
        return -ADAM_LR * (m_hat / (jnp.sqrt(v_hat) + ADAM_EPS) + ADAM_WD * w), m, v
    R, W = w.shape
    return rowwise(name, fn, [(a, W, 0) for a in (w, g, m, v)], [], [("new", W, F32)] * 3, n_rows=R, tm=_tile_rows(R, W, 2 ** 20))


def _w_in_padded(w):
    return jnp.concatenate([w[:, :4096], w[:, 4112:], w[:, 4096:4112], jnp.zeros((D, PW - D_PROJ), w.dtype)], axis=1)


def _w_in_unpadded(g):
    return jnp.concatenate([g[:, :4096], g[:, OFF_BA:OFF_BA + 16], g[:, 4096:OFF_BA]], axis=1)


def _ret_consts():
    f = F32
    log_gamma = jnp.log1p(-jnp.exp2(-5.0 - jnp.arange(H, dtype=f)))
    pos = jnp.arange(CH, dtype=f)
    causal = jnp.tril(jnp.ones((CH, CH), dtype=bool))
    diff = pos[:, None] - pos[None, :]
    decay = jnp.where(causal, jnp.exp(jnp.where(causal, diff, 0.0) * log_gamma[:, None, None]), 0.0)
    xi = jnp.broadcast_to(jnp.exp((pos + 1.0) * log_gamma[:, None])[:, :, None], (H, CH, DH))
    zeta = jnp.broadcast_to(jnp.exp((CH - 1.0 - pos) * log_gamma[:, None])[:, :, None], (H, CH, DH))
    cd = jnp.broadcast_to(jnp.exp(CH * log_gamma)[:, None, None], (H, 1, DH))
    return decay, xi, zeta, cd


def _rope_tables(lp):
    pos = jnp.arange(lp, dtype=F32) - float(PAD)
    inv = 1.0 / (ROPE_BASE ** jnp.linspace(0.0, 1.0, DH // 2, dtype=F32))
    ang = pos[:, None] * inv[None, :]
    cos, sin = jnp.cos(ang), jnp.sin(ang)
    ct = jnp.repeat(cos, 2, axis=1)
    st = jnp.stack([-sin, sin], axis=-1).reshape(lp, DH)
    return ct, st


def local_step(x, tgt, w, small):
    seq = x.shape[0]
    lp = HEAD_ROWS + seq
    nc = lp // CH
    tm = _pick(lp, 192, CH)
    row = functools.partial(rowwise, n_rows=lp, tm=tm)

    h0 = jnp.concatenate([jnp.zeros((PAD, D), F32), w["meta_tokens"], x], axis=0)
    tgt_p = jnp.concatenate([jnp.zeros((HEAD_ROWS, D), F32), tgt], axis=0)

    def ffn_fwd(tag, h, wn, w_in, w_out):
        n = row(f"{tag}_norm", lambda i, h, wn: _rms(h, wn), [(h, D, 0)], [wn], [("new", D, BF16)])[0]
        gu = matmul(f"{tag}_up", n, w_in, "nn", b_split=True)
        mid = row(f"{tag}_act", lambda i, gu: _swiglu(gu), [(gu, 2 * DFF, 0)], [], [("new", DFF, BF16)])[0]
        out = matmul(f"{tag}_down", mid, w_out, "nn", resid=h, alpha=0.5)
        return out, (h, n, gu, mid)

    def ffn_bwd(tag, dh, saved, wn, w_in, w_out):
        h, n, gu, mid = saved
        dmid = matmul(f"{tag}_dmid", dh, w_out, "nt", alpha=0.5)
        dw_out = matmul(f"{tag}_dwout", mid, dh, "tn", alpha=0.5, ti_cap=1408)

        def act_bwd(i, gu, dmid):
            _, vjp = jax.vjp(_swiglu, gu)
            return vjp(dmid)[0]
        dgu = row(f"{tag}_dact", act_bwd, [(gu, 2 * DFF, 0), (dmid, DFF, 0)], [], [("new", 2 * DFF, F32)])[0]
        dn = matmul(f"{tag}_dn", dgu, w_in, "nt", tj_cap=1024, b_split=True)
        dw_in = matmul(f"{tag}_dwin", n, dgu, "tn", ti_cap=1024, tr_cap=688, o_split=True)
        dh_in, dwn = norm_bwd(f"{tag}_dnorm", h, wn, dn, dh)
        return dh_in, dw_in, dw_out, dwn

    def norm_bwd(name, h, wn, dn, dres):
        def fn(i, h, dn, dres, wn):
            _, vjp = jax.vjp(_rms, h, wn)
            dh, dw = vjp(dn)
            return dres + dh, dw
        return row(name, fn, [(h, D, 0), (dn, D, 0), (dres, D, 0)], [wn], [("new", D, F32)], [(1, D)])

    h1, ffn1_saved = ffn_fwd("ffn1", h0, small["ffn1_norm"], w["ffn1_w_in"], w["ffn1_w_out"])
    n2 = row("mix_norm", lambda i, h, wn: _rms(h, wn), [(h1, D, 0)], [small["mix_norm"]], [("new", D, BF16)])[0]
    p = matmul("mix_proj", n2, w["w_in_p"], "nn", tj_cap=1152)

    cws = [w["gdn_conv_w"][j:j + 1] for j in range(CONV_K)]
    alog_row = jnp.zeros((1, LANES), F32).at[:, H:2 * H].set(small["gdn_a_log"])
    dtb_row = jnp.zeros((1, LANES), F32).at[:, H:2 * H].set(small["gdn_dt_bias"])
    q, k, v, gcb, bb, gc = gdn_pre_fwd(p, cws, alog_row, dtb_row, lp, tm)
    gc_rows = gc[:, H:2 * H].reshape(nc, CH, H).transpose(0, 2, 1)
    gdn_blocks = [(q, 0), (k, 0), (v, 0), (gcb, 0), (bb, 0)]
    o_a, gdn_states = _scan_fwd("gdn_scan_fwd", _gdn_chunk, gdn_blocks, [gc_rows], [], nc)
    y_a = row("gdn_post", lambda i, o, z, wn: _gdn_post(o, z, wn), [(o_a, D, 0), (p, D, OFF_Z // D)], [small["gdn_out_norm"]],
              [("new", D, BF16)])[0]

    ct, st = _rope_tables(lp)

    def rope_fwd(i, rq, rk, ct, st):
        c8, s8 = _tile_heads(ct), _tile_heads(st)
        return rq * c8 + _swap_pairs(rq) * s8, (rk * c8 + _swap_pairs(rk) * s8) * (DH ** -0.5)
    rq, rk = row("ret_pre", rope_fwd, [(p, D, OFF_RQ // D), (p, D, OFF_RK // D), (ct, DH, 0), (st, DH, 0)], [],
                 [("new", D, F32)] * 2)
    ret_consts = list(_ret_consts())
    ret_blocks = [(rq, 0), (rk, 0), (p, OFF_RV // D)]
    o_b, ret_states = _scan_fwd("ret_scan_fwd", _ret_chunk, ret_blocks, [], ret_consts, nc)
    y_b = row("ret_post", lambda i, o, g, wn: _ret_post(o, g, wn), [(o_b, D, 0), (p, D, OFF_RG // D)], [small["ret_out_norm"]],
              [("new", D, BF16)])[0]

    br_a = matmul("branch_gdn", y_a, w["w_branch_gdn"], "nn")
    br_b = matmul("branch_ret", y_b, w["w_branch_ret"], "nn")
    merged = row("merge", lambda i, a, b, ga, gb_: _merge(a, b, ga, gb_),
                 [(br_a, D, 0), (br_b, D, 0), (p, D, OFF_GA // D), (p, D, OFF_GB // D)], [], [("new", D, BF16)])[0]
    h2 = matmul("mix_out", merged, w["w_out"], "nn", resid=h1)
    h3, ffn2_saved = ffn_fwd("ffn2", h2, small["ffn2_norm"], w["ffn2_w_in"], w["ffn2_w_out"])

    def head(i, h, t, wn):
        mask = (_row_ids(i, tm) >= HEAD_ROWS).astype(F32)
        y, vjp = jax.vjp(_rms, h, wn)
        err = (y - t) * mask
        dh, dw = vjp(err * (1.0 / D))
        return dh, dw, jnp.sum(err * err, keepdims=True).reshape(1, 1) * (0.5 / D) * jnp.ones((1, LANES), F32)
    dh3, d_final, loss_row = row("loss_head", head, [(h3, D, 0), (tgt_p, D, 0)], [small["final_norm"]], [("new", D, F32)],
                                 [(1, D), (1, LANES)])

    gw, gs = {}, {"final_norm": d_final}
    dh2, gw["ffn2_w_in"], gw["ffn2_w_out"], gs["ffn2_norm"] = ffn_bwd("ffn2", dh3, ffn2_saved, small["ffn2_norm"],
                                                                      w["ffn2_w_in"], w["ffn2_w_out"])
    dmerged = matmul("mix_out_dx", dh2, w["w_out"], "nt", tj_cap=1024)
    gw["w_out"] = matmul("mix_out_dw", merged, dh2, "tn", ti_cap=1024)
    dp = lax.empty((lp, PW), F32)

    def merge_bwd(i, dm, a, b, ga, gb_):
        _, vjp = jax.vjp(_merge, a, b, ga, gb_)
        da, db, dga, dgb = vjp(dm)
        return da, db, jnp.concatenate([dga, dgb], axis=1)
    da, db_, dp = row("merge_bwd", merge_bwd,
                      [(dmerged, D, 0), (br_a, D, 0), (br_b, D, 0), (p, D, OFF_GA // D), (p, D, OFF_GB // D)], [],
                      [("new", D, F32), ("new", D, F32), ("into", dp, 2 * D, OFF_GA // (2 * D))])
    dy_a = matmul("branch_gdn_dx", da, w["w_branch_gdn"], "nt", tj_cap=1024)
    gw["w_branch_gdn"] = matmul("branch_gdn_dw", y_a, da, "tn", ti_cap=1024)
    dy_b = matmul("branch_ret_dx", db_, w["w_branch_ret"], "nt", tj_cap=1024)
    gw["w_branch_ret"] = matmul("branch_ret_dw", y_b, db_, "tn", ti_cap=1024)

    def gdn_post_bwd(i, o, z, dy, wn):
        _, vjp = jax.vjp(_gdn_post, o, z, wn)
        return vjp(dy)
    do_a, dp, gs["gdn_out_norm"] = row("gdn_post_bwd", gdn_post_bwd, [(o_a, D, 0), (p, D, OFF_Z // D), (dy_a, D, 0)],
                                       [small["gdn_out_norm"]], [("new", D, F32), ("into", dp, D, OFF_Z // D)], [(1, DH)])

    def ret_post_bwd(i, o, g, dy, wn):
        _, vjp = jax.vjp(_ret_post, o, g, wn)
        return vjp(dy)
    do_b, dp, gs["ret_out_norm"] = row("ret_post_bwd", ret_post_bwd, [(o_b, D, 0), (p, D, OFF_RG // D), (dy_b, D, 0)],
                                       [small["ret_out_norm"]], [("new", D, F32), ("into", dp, D, OFF_RG // D)], [(1, D)])

    drq, drk, dp = _scan_bwd("ret_scan_bwd", _ret_chunk, ret_blocks, [], ret_consts, ret_states, do_b,
                             [("new",), ("new",), ("into", dp, OFF_RV // D)], nc)

    def rope_bwd(i, drq, drk, ct, st):
        c8, s8 = _tile_heads(ct), _tile_heads(st)
        drk = drk * (DH ** -0.5)
        return jnp.concatenate([drq * c8 + _swap_pairs(drq * s8), drk * c8 + _swap_pairs(drk * s8)], axis=1)
    dp = row("ret_pre_bwd", rope_bwd, [(drq, D, 0), (drk, D, 0), (ct, DH, 0), (st, DH, 0)], [],
             [("into", dp, 2 * D, OFF_RQ // (2 * D))])[0]

    dq, dk, dv, dgcb, dbb, dgc_rows = _scan_bwd("gdn_scan_bwd", _gdn_chunk, gdn_blocks, [gc_rows], [], gdn_states, do_a,
                                                [("new",)] * 5, nc)
    dgc = jnp.pad(dgc_rows.transpose(0, 2, 1).reshape(lp, H), ((0, 0), (H, LANES - 2 * H)))
    dp, dba, dcw, dgate = gdn_pre_bwd(p, cws, alog_row, dtb_row, dq, dk, dv, dgcb, dbb, dgc, dp, lp, tm)
    dp = row("dp_ba", lambda i, t: t, [(dba, LANES, 0)], [], [("into", dp, LANES, OFF_BA // LANES)])[0]
    gw["gdn_conv_w"] = dcw[:CONV_K]
    gs["gdn_a_log"] = dgate[0:1, H:2 * H]
    gs["gdn_dt_bias"] = dgate[1:2, H:2 * H]

    dn2 = matmul("mix_proj_dx", dp, w["w_in_p"], "nt", tj_cap=1024, tr_cap=1152)
    gw["w_in_p"] = matmul("mix_proj_dw", n2, dp, "tn", ti_cap=1024, tj_cap=1152, tr_cap=688)
    dh1, gs["mix_norm"] = norm_bwd("mix_dnorm", h1, small["mix_norm"], dn2, dh2)
    dh0, gw["ffn1_w_in"], gw["ffn1_w_out"], gs["ffn1_norm"] = ffn_bwd("ffn1", dh1, ffn1_saved, small["ffn1_norm"],
                                                                      w["ffn1_w_in"], w["ffn1_w_out"])
    gw["meta_tokens"] = dh0[PAD:HEAD_ROWS]
    return loss_row, dh0[HEAD_ROWS:], gw, gs


BIG = ("ffn1_w_in", "ffn1_w_out", "w_in", "w_branch_gdn", "w_branch_ret", "w_out", "ffn2_w_in", "ffn2_w_out")
COL_SHARDED = ("ffn1_w_in", "w_in", "ffn2_w_in")
SMALL = ("ffn1_norm", "mix_norm", "ret_out_norm", "ffn2_norm", "final_norm", "gdn_out_norm", "gdn_a_log", "gdn_dt_bias")
WEIGHTS = ("meta_tokens", "ffn1_norm", "ffn1_w_in", "ffn1_w_out", "mix_norm", "w_in", "gdn_conv_w", "gdn_a_log", "gdn_dt_bias",
           "gdn_out_norm", "ret_out_norm", "w_branch_gdn", "w_branch_ret", "w_out", "ffn2_norm", "ffn2_w_in", "ffn2_w_out",
           "final_norm")
LOSS_ROW = 6
CONV_ROW0, META_ROW0, SMALL_ROWS = 8, 24, 40


def pack_small(vals):
    rows = [vals[n].reshape(1, D) for n in SMALL[:5]]
    r5 = jnp.concatenate([vals["gdn_out_norm"].reshape(1, DH), vals["gdn_a_log"].reshape(1, H), vals["gdn_dt_bias"].reshape(1, H),
                          jnp.zeros((1, D - DH - 2 * H), F32)], axis=1)
    return jnp.concatenate(rows + [r5, jnp.zeros((2, D), F32)], axis=0)


def unpack_small(packed, shapes):
    out = {n: packed[j].reshape(shapes[n]) for j, n in enumerate(SMALL[:5])}
    out["gdn_out_norm"] = packed[5, :DH].reshape(shapes["gdn_out_norm"])
    out["gdn_a_log"] = packed[5, DH:DH + H].reshape(shapes["gdn_a_log"])
    out["gdn_dt_bias"] = packed[5, DH + H:DH + 2 * H].reshape(shapes["gdn_dt_bias"])
    return out


def kernel(x, meta_tokens, ffn1_norm, ffn1_w_in, ffn1_w_out, mix_norm, w_in, gdn_conv_w, gdn_a_log, gdn_dt_bias, gdn_out_norm, ret_out_norm, w_branch_gdn, w_branch_ret, w_out, ffn2_norm, ffn2_w_in, ffn2_w_out, final_norm, loss_target, m_meta_tokens, m_ffn1_norm, m_ffn1_w_in, m_ffn1_w_out, m_mix_norm, m_w_in, m_gdn_conv_w, m_gdn_a_log, m_gdn_dt_bias, m_gdn_out_norm, m_ret_out_norm, m_w_branch_gdn, m_w_branch_ret, m_w_out, m_ffn2_norm, m_ffn2_w_in, m_ffn2_w_out, m_final_norm, v_meta_tokens, v_ffn1_norm, v_ffn1_w_in, v_ffn1_w_out, v_mix_norm, v_w_in, v_gdn_conv_w, v_gdn_a_log, v_gdn_dt_bias, v_gdn_out_norm, v_ret_out_norm, v_w_branch_gdn, v_w_branch_ret, v_w_out, v_ffn2_norm, v_ffn2_w_in, v_ffn2_w_out, v_final_norm):
    a = dict(locals())
    wts = {n: a[n] for n in WEIGHTS}
    mom_m = {n: a["m_" + n] for n in WEIGHTS}
    mom_v = {n: a["v_" + n] for n in WEIGHTS}
    shapes = {n: wts[n].shape for n in WEIGHTS}
    flat = lambda t: t.reshape(t.shape[-2:])
    c = lax.axis_index("c")
    chip = 2 * lax.axis_index("x") + lax.axis_index("y")

    exact = jnp.zeros((16, D), F32).at[0:3].set(wts["gdn_conv_w"].reshape(3, D)).at[3:7].set(wts["meta_tokens"].reshape(4, D))
    gathered = allgather_chips_two_level("gather_weights", [flat(wts[n]).astype(BF16) for n in BIG] + [exact])
    gat = dict(zip(BIG, gathered[:-1]))
    exact = gathered[-1]
    w = {n: gat[n].reshape(N_CHIPS * gat[n].shape[1], D) for n in BIG if n not in COL_SHARDED}
    w["ffn1_w_in"], w["ffn2_w_in"] = gat["ffn1_w_in"], gat["ffn2_w_in"]
    w["w_in_p"] = _w_in_padded(jnp.concatenate([gat["w_in"][q] for q in range(N_CHIPS)], axis=1))
    w["gdn_conv_w"] = jnp.concatenate([exact[q, 0:3].reshape(CONV_K, 3 * D // 4) for q in range(N_CHIPS)], axis=1)
    w["meta_tokens"] = jnp.concatenate([exact[q, 3:7].reshape(N_META, D // 4) for q in range(N_CHIPS)], axis=1)
    small = {n: wts[n].reshape(1, -1) for n in SMALL}

    loss_row, gx, gw, gs = local_step(x[0], loss_target[0], w, small)

    gblk = {n: gw[n].reshape(N_CHIPS, gw[n].shape[0] // N_CHIPS, D) for n in BIG if n not in COL_SHARDED}
    gblk["ffn1_w_in"], gblk["ffn2_w_in"] = gw["ffn1_w_in"], gw["ffn2_w_in"]
    gblk["w_in"] = jnp.stack(jnp.split(_w_in_unpadded(gw["w_in_p"]), N_CHIPS, axis=1))
    gl = [gblk[n] for n in BIG]
    from_sib = sibling_halves("grads_sibling", gl)
    chip_sum = [chip_sums(f"grads_chip_sum_{n}", g, r, c) for n, g, r in zip(BIG, gl, from_sib)]
    from_chips = scatter_chips("grads_scatter", chip_sum)
    halves = [sum_slots(f"grads_sum_{n}", [(cs, chip), (fc, 0), (fc, 1), (fc, 2)]) for n, cs, fc in zip(BIG, chip_sum, from_chips)]
    joined = sibling_join("grads_join", halves)
    grads, delta, new_m, new_v = {}, {}, {}, {}
    for n, j in zip(BIG, joined):
        g = j.reshape(shapes[n][-2:])
        d_, m_, v_ = adamw(f"adamw_{n}", flat(wts[n]), g, flat(mom_m[n]), flat(mom_v[n]))
        grads[n], delta[n], new_m[n], new_v[n] = (t.reshape(shapes[n]) for t in (g, d_, m_, v_))

    sm = jnp.concatenate([pack_small(gs).at[LOSS_ROW, :LANES].set(loss_row[0]),
                          gw["gdn_conv_w"].reshape(3 * CONV_K, D), jnp.zeros((META_ROW0 - CONV_ROW0 - 3 * CONV_K, D), F32),
                          gw["meta_tokens"]], axis=0)
    every = allgather_all("small_gather", sm)
    sm_sum = sum_slots("small_sum", [(every, s) for s in range(8)])
    d_s, m_s, v_s = adamw("adamw_small", pack_small(small), sm_sum[:8], pack_small({n: mom_m[n].reshape(1, -1) for n in SMALL}),
                          pack_small({n: mom_v[n].reshape(1, -1) for n in SMALL}))
    grads.update(unpack_small(sm_sum, shapes))
    delta.update(unpack_small(d_s, shapes))
    new_m.update(unpack_small(m_s, shapes))
    new_v.update(unpack_small(v_s, shapes))
    g_conv = lax.dynamic_slice_in_dim(sm_sum[CONV_ROW0:CONV_ROW0 + 3 * CONV_K].reshape(CONV_K, 3 * D), chip * (3 * D // 4), 3 * D // 4, 1)
    g_meta = lax.dynamic_slice_in_dim(sm_sum[META_ROW0:META_ROW0 + N_META], chip * (D // 4), D // 4, 1)
    for n, g in (("gdn_conv_w", g_conv), ("meta_tokens", g_meta)):
        d_, m_, v_ = adamw(f"adamw_{n}", flat(wts[n]), g, flat(mom_m[n]), flat(mom_v[n]))
        grads[n], delta[n], new_m[n], new_v[n] = (t.reshape(shapes[n]) for t in (g, d_, m_, v_))
    loss = sm_sum[LOSS_ROW, 0]

    return (loss, gx[None], *[grads[n] for n in WEIGHTS], *[delta[n] for n in WEIGHTS], *[new_m[n] for n in WEIGHTS],
            *[new_v[n] for n in WEIGHTS])
```

```python
import functools

import jax
import jax.numpy as jnp
from jax import lax
from jax.experimental import pallas as pl
from jax.experimental.pallas import tpu as pltpu

F32 = jnp.float32
BF16 = jnp.bfloat16
HI = lax.Precision.HIGHEST
MESH = pl.DeviceIdType.MESH

D = 1024
N_META = 16
PAD = 48
HEAD_ROWS = PAD + N_META
CH = 64
H = 8
DH = 128
DFF = 2816
CONV_K = 4
EPS = 1e-6
ROPE_BASE = 10000.0
LANES = 128
N_CHIPS = 4
VMEM_LIMIT = 56 * 2 ** 20

OFF_QKV, OFF_Z, OFF_RQ, OFF_RK, OFF_RV, OFF_RG, OFF_GA, OFF_GB, OFF_BA = 0, 3072, 4096, 5120, 6144, 7168, 8192, 9216, 10240
PW = 10368
D_PROJ = 10256

ADAM_LR, ADAM_B1, ADAM_B2, ADAM_EPS, ADAM_WD, ADAM_STEP = 0.001, 0.9, 0.999, 1e-08, 0.01, 10


def _pick(n, cap, mult):
    best = None
    for t in range(mult, min(n, cap) + 1, mult):
        if n % t == 0:
            best = t
    return best if best is not None else n


def _tile_rows(rows, cols, block_bytes=3 * 2 ** 19):
    return _pick(rows, max(8, block_bytes // (4 * cols)), 8)


def _dot(a, b, dims, prec=None):
    return lax.dot_general(a, b, (dims, ((), ())), precision=prec, preferred_element_type=F32)


def _mm(a, b):
    return _dot(a.astype(BF16), b.astype(BF16), ((1,), (0,)))


def _mm_nt(a, b):
    return _dot(a.astype(BF16), b.astype(BF16), ((1,), (1,)))


def _mm_tn(a, b):
    return _dot(a.astype(BF16), b.astype(BF16), ((0,), (0,)))


def _mmh(a, b):
    return _dot(a, b, ((1,), (0,)), HI)


def _split(a):
    hi = a.astype(BF16)
    return hi, (a - hi.astype(F32)).astype(BF16)


_NN, _NT, _TN = ((2,), (1,)), ((2,), (2,)), ((1,), (1,))


def _bdot(a, b, dims):
    return lax.dot_general(a, b, (dims, ((0,), (0,))), preferred_element_type=F32)


def _bmm(a, b, dims):
    return _bdot(a.astype(BF16), b.astype(BF16), dims)


def _bdot3(a, b, dims):
    ah, al = _split(a)
    bh, bl = _split(b)
    return _bdot(ah, bh, dims) + (_bdot(ah, bl, dims) + _bdot(al, bh, dims))


def matmul(name, a, b, mode, *, out_dtype=F32, resid=None, alpha=1.0, ti_cap=688, tj_cap=512, tr_cap=1408,
           b_split=False, o_split=False):
    if mode == "nn":
        I, R = a.shape
        J = N_CHIPS * b.shape[2] if b_split else b.shape[1]
    elif mode == "nt":
        I, R = a.shape
        J = b.shape[1] if b_split else b.shape[0]
    else:
        R, I = a.shape
        J = b.shape[1]
    ti = _pick(I, ti_cap, 16) if mode != "tn" else _pick(I, ti_cap, LANES)
    tj = _pick(J, tj_cap, LANES)
    tr = _pick(R, tr_cap, LANES) if mode != "tn" else _pick(R, tr_cap, 16)
    if mode == "nn":
        a_spec = pl.BlockSpec((ti, tr), lambda i, j, r: (i, r))
        if b_split:
            tj = J // N_CHIPS
            b_spec = pl.BlockSpec((None, tr, tj), lambda i, j, r: (j, r, 0))
        else:
            b_spec = pl.BlockSpec((tr, tj), lambda i, j, r: (r, j))
        dims = ((1,), (0,))
    elif mode == "nt":
        if b_split:
            tr = R // N_CHIPS
            b_spec = pl.BlockSpec((None, tj, tr), lambda i, j, r: (r, j, 0))
        else:
            b_spec = pl.BlockSpec((tj, tr), lambda i, j, r: (j, r))
        a_spec = pl.BlockSpec((ti, tr), lambda i, j, r: (i, r))
        dims = ((1,), (1,))
    else:
        if o_split:
            tj = J // N_CHIPS
        a_spec = pl.BlockSpec((tr, ti), lambda i, j, r: (r, i))
        b_spec = pl.BlockSpec((tr, tj), lambda i, j, r: (r, j))
        dims = ((0,), (0,))
    nr = R // tr
    assert I % ti == 0 and J % tj == 0 and R % tr == 0, (name, I, J, R, ti, tj, tr)
    if o_split:
        o_spec = pl.BlockSpec((None, ti, tj), lambda i, j, r: (j, i, 0))
        out_shape = jax.ShapeDtypeStruct((N_CHIPS, I, tj), out_dtype)
    else:
        o_spec = pl.BlockSpec((ti, tj), lambda i, j, r: (i, j))
        out_shape = jax.ShapeDtypeStruct((I, J), out_dtype)
    has_resid = resid is not None

    def finish(o_ref, r_ref, res):
        res = res * alpha if alpha != 1.0 else res
        if has_resid:
            res = r_ref[...] + res
        o_ref[...] = res.astype(o_ref.dtype)

    def body(*refs):
        a_ref, b_ref = refs[:2]
        r_ref = refs[2] if has_resid else None
        o_ref = refs[3] if has_resid else refs[2]
        prod = _dot(a_ref[...].astype(BF16), b_ref[...].astype(BF16), dims)
        if nr == 1:
            finish(o_ref, r_ref, prod)
            return
        acc = refs[-1]
        r = pl.program_id(2)

        @pl.when(r == 0)
        def _():
            acc[...] = prod

        @pl.when(r > 0)
        def _():
            acc[...] += prod

        @pl.when(r == nr - 1)
        def _():
            finish(o_ref, r_ref, acc[...])

    ins = [a, b] + ([resid] if has_resid else [])
    specs = [a_spec, b_spec] + ([o_spec] if has_resid else [])
    return pl.pallas_call(
        body, name=name, grid=(I // ti, J // tj, nr), in_specs=specs, out_specs=o_spec, out_shape=out_shape,
        scratch_shapes=[pltpu.VMEM((ti, tj), F32)] if nr > 1 else [],
        compiler_params=pltpu.CompilerParams(dimension_semantics=("parallel", "parallel", "arbitrary"),
                                             vmem_limit_bytes=VMEM_LIMIT),
    )(*ins)


def rowwise(name, fn, rows, pars, outs, accs=(), *, n_rows, tm):
    nt = n_rows // tm
    assert nt * tm == n_rows
    in_specs, ins = [], []
    for arr, w, cb in rows:
        in_specs.append(pl.BlockSpec((tm, w), lambda i, cb=cb: (i, cb)))
        ins.append(arr)
    for p in pars:
        in_specs.append(pl.BlockSpec(p.shape, lambda i, nd=p.ndim: (0,) * nd))
        ins.append(p)
    n_in = len(ins)
    out_specs, out_shapes, aliases = [], [], {}
    for k, o in enumerate(outs):
        if o[0] == "new":
            _, w, dt = o
            out_specs.append(pl.BlockSpec((tm, w), lambda i: (i, 0)))
            out_shapes.append(jax.ShapeDtypeStruct((n_rows, w), dt))
        else:
            _, arr, w, cb = o
            in_specs.append(pl.BlockSpec(memory_space=pl.ANY))
            aliases[len(ins)] = k
            ins.append(arr)
            out_specs.append(pl.BlockSpec((tm, w), lambda i, cb=cb: (i, cb)))
            out_shapes.append(jax.ShapeDtypeStruct(arr.shape, arr.dtype))
    for r, w in accs:
        out_specs.append(pl.BlockSpec((r, w), lambda i: (0, 0)))
        out_shapes.append(jax.ShapeDtypeStruct((r, w), F32))
    n_all_in, n_out, n_acc = len(ins), len(outs), len(accs)

    def body(*refs):
        i = pl.program_id(0)
        vals = [r[...] for r in refs[:n_in]]
        res = fn(i, *vals)
        if not isinstance(res, (tuple, list)):
            res = (res,)
        o_refs = refs[n_all_in:n_all_in + n_out]
        a_refs = refs[n_all_in + n_out:]
        for r, v in zip(o_refs, res[:n_out]):
            r[...] = v.astype(r.dtype)
        if n_acc:
            @pl.when(i == 0)
            def _():
                for r in a_refs:
                    r[...] = jnp.zeros_like(r)
            for r, v in zip(a_refs, res[n_out:]):
                r[...] += v

    return pl.pallas_call(
        body, name=name, grid=(nt,), in_specs=in_specs, out_specs=out_specs, out_shape=out_shapes,
        input_output_aliases=aliases,
        compiler_params=pltpu.CompilerParams(dimension_semantics=("arbitrary",), vmem_limit_bytes=VMEM_LIMIT),
    )(*ins)


def _row_ids(i, tm):
    return i * tm + lax.broadcasted_iota(jnp.int32, (tm, 1), 0)


def _sigmoid(x):
    return 1.0 / (1.0 + jnp.exp(-x))


def _silu(x):
    return x * _sigmoid(x)


def _softplus(x):
    return jnp.maximum(x, 0.0) + jnp.log(1.0 + jnp.exp(-jnp.abs(x)))


def _rms(x, w):
    return x * lax.rsqrt(jnp.mean(x * x, axis=-1, keepdims=True) + EPS) * w


def _heads(fn, *xs):
    return jnp.concatenate([fn(h, *[x[:, h * DH:(h + 1) * DH] for x in xs]) for h in range(H)], axis=1)


def _swiglu(gu):
    return _silu(gu[:, :DFF]) * gu[:, DFF:]


def _gdn_post(o, z, w):
    return _heads(lambda h, oh, zh: oh * lax.rsqrt(jnp.mean(oh * oh, axis=-1, keepdims=True) + EPS) * w * _silu(zh), o, z)


def _ret_post(o, rg, w):
    def one(h, oh, gh, wh):
        mu = jnp.mean(oh, axis=-1, keepdims=True)
        xc = oh - mu
        var = jnp.mean(xc * xc, axis=-1, keepdims=True)
        return _silu(gh) * (xc * lax.rsqrt(var + EPS) * wh)
    return _heads(one, o, rg, jnp.broadcast_to(w, (o.shape[0], D)))


def _merge(a, b, ga, gb):
    return _sigmoid(ga) * a + _sigmoid(gb) * b


def _select_matrix(first_lane):
    r = lax.broadcasted_iota(jnp.int32, (LANES, H * DH), 0)
    c = lax.broadcasted_iota(jnp.int32, (LANES, H * DH), 1)
    return (r == first_lane + (c >> 7)).astype(F32)


def _chunk_tri(tm):
    r = lax.broadcasted_iota(jnp.int32, (tm, tm), 0)
    c = lax.broadcasted_iota(jnp.int32, (tm, tm), 1)
    return jnp.logical_and((r >> 6) == (c >> 6), r >= c).astype(F32)


def _gdn_gates(ba, alog_row, dtb_row, mask):
    g = -jnp.exp(alog_row) * _softplus(ba + dtb_row) * mask
    gc = _mmh(_chunk_tri(ba.shape[0]), g)
    beta = _sigmoid(ba) * mask
    return _mmh(gc, _select_matrix(H)), _mmh(beta, _select_matrix(0)), gc


def _gdn_qkv(c):
    a = _silu(c)

    def l2(scale):
        return lambda h, t: t * lax.rsqrt(jnp.sum(t * t, axis=-1, keepdims=True) + EPS) * scale
    q = _heads(l2(DH ** -0.5), a[:, :D])
    k = _heads(l2(1.0), a[:, D:2 * D])
    return q, k, a[:, 2 * D:]


def _conv_taps(xw, cws, tm):
    return sum(cws[i] * xw[5 + i:5 + i + tm] for i in range(CONV_K))


def _swap_pairs(t):
    n = t.shape[1]
    lane = lax.broadcasted_iota(jnp.int32, t.shape, 1)
    return jnp.where((lane & 1) == 0, pltpu.roll(t, n - 1, 1), pltpu.roll(t, 1, 1))


def _tile_heads(t):
    return jnp.concatenate([t] * H, axis=1)


@jax.custom_vjp
def _unit_lower_inv(a):
    n = -a
    eye = (lax.broadcasted_iota(jnp.int32, a.shape, 1) == lax.broadcasted_iota(jnp.int32, a.shape, 2)).astype(F32)
    p = eye + n
    for _ in range(5):
        n = _bdot3(n, n, _NN)
        p = p + _bdot3(p, n, _NN)
    return p


def _inv_fwd(a):
    t = _unit_lower_inv(a)
    return t, t


def _inv_bwd(t, dt):
    x = _bdot3(t, dt, _TN)
    return (-_bdot3(x, t, _NT),)


_unit_lower_inv.defvjp(_inv_fwd, _inv_bwd)


def _gdn_chunk(q, k, v, gc, bb, gr, s):
    ri = lax.broadcasted_iota(jnp.int32, (H, CH, CH), 1)
    ci = lax.broadcasted_iota(jnp.int32, (H, CH, CH), 2)
    causal = ri >= ci
    gc1 = jnp.sum(gc, axis=2, keepdims=True) * (1.0 / LANES)
    diff = jnp.broadcast_to(gc1, (H, CH, CH)) - jnp.broadcast_to(gr, (H, CH, CH))
    decay = jnp.where(causal, jnp.exp(jnp.where(causal, diff, 0.0)), 0.0)
    kb = k * bb
    sc = _bmm(jnp.concatenate([q, kb], axis=1), k, _NT)
    qk = sc[:, :CH] * decay
    a = jnp.where(ri > ci, sc[:, CH:] * decay, 0.0)
    t = _unit_lower_inv(a)
    eg = jnp.exp(gc)
    uw = _bmm(t, jnp.concatenate([v * bb, kb * eg], axis=2), _NN)
    g_last = gc[:, CH - 1:CH, :]
    ws = _bmm(jnp.concatenate([uw[:, :, DH:], q * eg], axis=1), s, _NN)
    v_new = uw[:, :, :DH] - ws[:, :CH]
    o = ws[:, CH:] + _bmm(qk, v_new, _NN)
    s_new = s * jnp.exp(g_last) + _bmm(k * jnp.exp(g_last - gc), v_new, _TN)
    return o, s_new


def _ret_chunk(q, k, v, s, decay, xi, zeta, cd):
    scores = _bmm(q, k, _NT) * decay
    o = _bmm(scores, v, _NN) + _bmm(q * xi, s, _NN)
    s_new = s * cd + _bmm(k * zeta, v, _TN)
    return o, s_new


def _head_blocks(ref):
    return jnp.stack([ref[:, h * DH:(h + 1) * DH] for h in range(H)])


def _head_rows(ref):
    return jnp.stack([ref[0, h:h + 1, :] for h in range(H)])


def _scan_fwd(name, chunk_fn, blocks, rowvecs, consts, nc):
    n_blk, n_rv, n_c = len(blocks), len(rowvecs), len(consts)

    def body(*refs):
        blk = refs[:n_blk]
        rvs = refs[n_blk:n_blk + n_rv]
        cst = refs[n_blk + n_rv:n_blk + n_rv + n_c]
        o_ref, st_ref, s_scr = refs[n_blk + n_rv + n_c:]

        @pl.when(pl.program_id(0) == 0)
        def _():
            s_scr[...] = jnp.zeros_like(s_scr)

        s = s_scr[...]
        st_ref[0] = s
        o, s_new = chunk_fn(*[_head_blocks(r) for r in blk], *[_head_rows(r) for r in rvs], s, *[r[...] for r in cst])
        for h in range(H):
            o_ref[:, h * DH:(h + 1) * DH] = o[h]
        s_scr[...] = s_new

    in_specs = [pl.BlockSpec((CH, H * DH), lambda c, f=f: (c, f)) for _, f in blocks]
    in_specs += [pl.BlockSpec((1, H, CH), lambda c: (c, 0, 0)) for _ in rowvecs]
    in_specs += [pl.BlockSpec(a.shape, lambda c: (0, 0, 0)) for a in consts]
    return pl.pallas_call(
        body, name=name, grid=(nc,), in_specs=in_specs,
        out_specs=[pl.BlockSpec((CH, H * DH), lambda c: (c, 0)), pl.BlockSpec((1, H, DH, DH), lambda c: (c, 0, 0, 0))],
        out_shape=[jax.ShapeDtypeStruct((nc * CH, H * DH), F32), jax.ShapeDtypeStruct((nc, H, DH, DH), F32)],
        scratch_shapes=[pltpu.VMEM((H, DH, DH), F32)],
        compiler_params=pltpu.CompilerParams(dimension_semantics=("arbitrary",), vmem_limit_bytes=VMEM_LIMIT),
    )(*[a for a, _ in blocks], *rowvecs, *consts)


def _scan_bwd(name, chunk_fn, blocks, rowvecs, consts, states, do, outs, nc):
    n_blk, n_rv, n_c = len(blocks), len(rowvecs), len(consts)
    n_into = sum(1 for o in outs if o[0] == "into")

    def body(*refs):
        blk = refs[:n_blk]
        rvs = refs[n_blk:n_blk + n_rv]
        cst = refs[n_blk + n_rv:n_blk + n_rv + n_c]
        st_ref, do_ref = refs[n_blk + n_rv + n_c:n_blk + n_rv + n_c + 2]
        n_in = n_blk + n_rv + n_c + 2 + n_into
        o_refs = refs[n_in:n_in + n_blk]
        rv_refs = refs[n_in + n_blk:n_in + n_blk + n_rv]
        ds_scr = refs[n_in + n_blk + n_rv]

        @pl.when(pl.program_id(0) == 0)
        def _():
            ds_scr[...] = jnp.zeros_like(ds_scr)

        cv = [r[...] for r in cst]
        _, vjp = jax.vjp(lambda *a: chunk_fn(*a, *cv), *[_head_blocks(r) for r in blk], *[_head_rows(r) for r in rvs], st_ref[0])
        grads = vjp((_head_blocks(do_ref), ds_scr[...]))
        for h in range(H):
            for r, g in zip(o_refs, grads[:n_blk]):
                r[:, h * DH:(h + 1) * DH] = g[h].astype(r.dtype)
            for r, g in zip(rv_refs, grads[n_blk:n_blk + n_rv]):
                r[0, h:h + 1, :] = g[h]
        ds_scr[...] = grads[n_blk + n_rv]

    rc = lambda c: nc - 1 - c
    in_specs = [pl.BlockSpec((CH, H * DH), lambda c, f=f: (rc(c), f)) for _, f in blocks]
    in_specs += [pl.BlockSpec((1, H, CH), lambda c: (rc(c), 0, 0)) for _ in rowvecs]
    in_specs += [pl.BlockSpec(a.shape, lambda c: (0, 0, 0)) for a in consts]
    in_specs += [pl.BlockSpec((1, H, DH, DH), lambda c: (rc(c), 0, 0, 0)), pl.BlockSpec((CH, H * DH), lambda c: (rc(c), 0))]
    ins = [a for a, _ in blocks] + list(rowvecs) + list(consts) + [states, do]
    out_specs, out_shapes, aliases = [], [], {}
    for k, o in enumerate(outs):
        if o[0] == "new":
            out_specs.append(pl.BlockSpec((CH, H * DH), lambda c: (rc(c), 0)))
            out_shapes.append(jax.ShapeDtypeStruct((nc * CH, H * DH), F32))
        else:
            _, arr, f = o
            in_specs.append(pl.BlockSpec(memory_space=pl.ANY))
            aliases[len(ins)] = k
            ins.append(arr)
            out_specs.append(pl.BlockSpec((CH, H * DH), lambda c, f=f: (rc(c), f)))
            out_shapes.append(jax.ShapeDtypeStruct(arr.shape, arr.dtype))
    for _ in rowvecs:
        out_specs.append(pl.BlockSpec((1, H, CH), lambda c: (rc(c), 0, 0)))
        out_shapes.append(jax.ShapeDtypeStruct((nc, H, CH), F32))
    return pl.pallas_call(
        body, name=name, grid=(nc,), in_specs=in_specs, out_specs=out_specs, out_shape=out_shapes,
        input_output_aliases=aliases, scratch_shapes=[pltpu.VMEM((H, DH, DH), F32)],
        compiler_params=pltpu.CompilerParams(dimension_semantics=("arbitrary",), vmem_limit_bytes=VMEM_LIMIT),
    )(*ins)


def _gdn_pre_specs(p, cws, alog_row, dtb_row, tm, pos):
    sub = tm // 8
    rows = [pl.BlockSpec((tm, 3 * D), lambda i: (pos(i), 0)),
            pl.BlockSpec((8, 3 * D), lambda i: (jnp.maximum(pos(i) * sub - 1, 0), 0)),
            pl.BlockSpec((tm, LANES), lambda i: (pos(i), OFF_BA // LANES))]
    pars = [pl.BlockSpec(a.shape, lambda i: (0, 0)) for a in (*cws, alog_row, dtb_row)]
    return rows + pars, [p, p, p, *cws, alog_row, dtb_row]


def gdn_pre_fwd(p, cws, alog_row, dtb_row, lp, tm):
    def body(x_ref, prev_ref, ba_ref, c0, c1, c2, c3, al_ref, dt_ref, q_ref, k_ref, v_ref, g_ref, b_ref, gc_ref):
        i = pl.program_id(0)
        prev = jnp.where(i > 0, prev_ref[...], 0.0)
        xw = jnp.concatenate([prev, x_ref[...]], axis=0)
        c = _conv_taps(xw, [c0[...], c1[...], c2[...], c3[...]], tm)
        q, k, v = _gdn_qkv(c)
        mask = (_row_ids(i, tm) >= PAD).astype(F32)
        g, b, gc = _gdn_gates(ba_ref[...], al_ref[...], dt_ref[...], mask)
        q_ref[...] = q
        k_ref[...] = k
        v_ref[...] = v
        g_ref[...] = g
        b_ref[...] = b
        gc_ref[...] = gc

    in_specs, ins = _gdn_pre_specs(p, cws, alog_row, dtb_row, tm, lambda i: i)
    o_spec = pl.BlockSpec((tm, D), lambda i: (i, 0))
    return pl.pallas_call(
        body, name="gdn_pre_fwd", grid=(lp // tm,), in_specs=in_specs,
        out_specs=[o_spec] * 5 + [pl.BlockSpec((tm, LANES), lambda i: (i, 0))],
        out_shape=[jax.ShapeDtypeStruct((lp, D), F32)] * 5 + [jax.ShapeDtypeStruct((lp, LANES), F32)],
        compiler_params=pltpu.CompilerParams(dimension_semantics=("parallel",), vmem_limit_bytes=VMEM_LIMIT),
    )(*ins)


def gdn_pre_bwd(p, cws, alog_row, dtb_row, dq, dk, dv, dg, db, dgc, dp, lp, tm):
    nt = lp // tm
    pos = lambda i: nt - 1 - i

    def body(x_ref, prev_ref, ba_ref, c0, c1, c2, c3, al_ref, dt_ref, dq_ref, dk_ref, dv_ref, dg_ref, db_ref, dgc_ref,
             dp_any, dx_ref, dba_ref, dcw_ref, dpar_ref, carry):
        i = pl.program_id(0)
        t = pos(i)

        @pl.when(i == 0)
        def _():
            carry[...] = jnp.zeros_like(carry)
            dcw_ref[...] = jnp.zeros_like(dcw_ref)
            dpar_ref[...] = jnp.zeros_like(dpar_ref)

        cws_v = [c0[...], c1[...], c2[...], c3[...]]
        prev = jnp.where(t > 0, prev_ref[...], 0.0)
        xw = jnp.concatenate([prev, x_ref[...]], axis=0)
        xs = [xw[5 + j:5 + j + tm] for j in range(CONV_K)]
        c = sum(cws_v[j] * xs[j] for j in range(CONV_K))
        _, vjp_qkv = jax.vjp(_gdn_qkv, c)
        (dc,) = vjp_qkv((dq_ref[...], dk_ref[...], dv_ref[...]))
        zeros8 = jnp.zeros((8, 3 * D), F32)
        dcp = jnp.concatenate([zeros8, dc, zeros8], axis=0)
        dxw = sum(cws_v[j] * dcp[3 - j:3 - j + tm + 8] for j in range(CONV_K))
        dx_ref[...] = jnp.concatenate([dxw[8:tm], dxw[tm:] + carry[...]], axis=0).astype(dx_ref.dtype)
        carry[...] = dxw[:8]
        for j in range(CONV_K):
            dcw_ref[j:j + 1, :] += jnp.sum(dc * xs[j], axis=0, keepdims=True)
        mask = (_row_ids(t, tm) >= PAD).astype(F32)
        _, vjp_g = jax.vjp(lambda ba, al, dt: _gdn_gates(ba, al, dt, mask), ba_ref[...], al_ref[...], dt_ref[...])
        dba, dal, ddt = vjp_g((dg_ref[...], db_ref[...], dgc_ref[...]))
        dba_ref[...] = dba
        dpar_ref[0:1, :] += dal
        dpar_ref[1:2, :] += ddt

    in_specs, ins = _gdn_pre_specs(p, cws, alog_row, dtb_row, tm, pos)
    g_spec = pl.BlockSpec((tm, D), lambda i: (pos(i), 0))
    s_spec = pl.BlockSpec((tm, LANES), lambda i: (pos(i), 0))
    in_specs += [g_spec] * 5 + [s_spec, pl.BlockSpec(memory_space=pl.ANY)]
    ins += [dq, dk, dv, dg, db, dgc, dp]
    return pl.pallas_call(
        body, name="gdn_pre_bwd", grid=(nt,), in_specs=in_specs,
        out_specs=[pl.BlockSpec((tm, 3 * D), lambda i: (pos(i), 0)), s_spec,
                   pl.BlockSpec((8, 3 * D), lambda i: (0, 0)), pl.BlockSpec((8, LANES), lambda i: (0, 0))],
        out_shape=[jax.ShapeDtypeStruct(dp.shape, dp.dtype), jax.ShapeDtypeStruct((lp, LANES), F32),
                   jax.ShapeDtypeStruct((8, 3 * D), F32), jax.ShapeDtypeStruct((8, LANES), F32)],
        input_output_aliases={len(ins) - 1: 0},
        scratch_shapes=[pltpu.VMEM((8, 3 * D), F32)],
        compiler_params=pltpu.CompilerParams(dimension_semantics=("arbitrary",), vmem_limit_bytes=VMEM_LIMIT),
    )(*ins)


def _me():
    return lax.axis_index("x"), lax.axis_index("y"), lax.axis_index("c")


def _any_specs(n):
    return [pl.BlockSpec(memory_space=pl.ANY)] * n


def allgather_chips_two_level(name, ws):
    n = len(ws)

    def body(*refs):
        w_refs, o_refs = refs[:n], refs[n:2 * n]
        send_sems, recv_sems, fsend_sems, frecv_sems, osend_sems, orecv_sems = refs[2 * n:]
        x, y, c = _me()
        me = 2 * x + y
        chips = [(1 - x, y), (x, 1 - y), (1 - x, 1 - y)]

        def own(a):
            return pltpu.make_async_remote_copy(src_ref=w_refs[a], dst_ref=o_refs[a].at[me], send_sem=osend_sems.at[a],
                                                recv_sem=orecv_sems.at[a], device_id=(x, y, 1 - c), device_id_type=MESH)
        mine = [own(a) for a in range(n)]
        for cp in mine:
            cp.start()

        def rows(a, cc):
            rh = ws[a].shape[0] // 2
            return pl.ds(pl.multiple_of(cc * rh, 8), rh)

        def ici(a, j, slot):
            px, py = chips[j]
            return pltpu.make_async_remote_copy(
                src_ref=w_refs[a].at[rows(a, c)], dst_ref=o_refs[a].at[slot, rows(a, c)], send_sem=send_sems.at[a, j],
                recv_sem=recv_sems.at[a, j], device_id=(px, py, c), device_id_type=MESH)

        def d2d(a, j, cc):
            px, py = chips[j]
            blk = o_refs[a].at[2 * px + py, rows(a, cc)]
            return pltpu.make_async_remote_copy(src_ref=blk, dst_ref=blk, send_sem=fsend_sems.at[a, j],
                                                recv_sem=frecv_sems.at[a, j], device_id=(x, y, 1 - c), device_id_type=MESH)
        sends = [ici(a, j, me) for a in range(n) for j in range(3)]
        for cp in sends:
            cp.start()
        passed = []
        for a in range(n):
            for j, (px, py) in enumerate(chips):
                ici(a, j, 2 * px + py).wait_recv()
                cp = d2d(a, j, c)
                cp.start()
                passed.append(cp)
        for a in range(n):
            for j in range(3):
                d2d(a, j, 1 - c).wait_recv()
        for cp in mine:
            cp.wait_recv()
        for cp in sends + passed + mine:
            cp.wait_send()

    sem = lambda *s: pltpu.SemaphoreType.DMA(s)
    return pl.pallas_call(
        body, name=name, out_shape=[jax.ShapeDtypeStruct((N_CHIPS,) + w.shape, w.dtype) for w in ws],
        in_specs=_any_specs(n), out_specs=_any_specs(n),
        scratch_shapes=[sem(n, 3), sem(n, 3), sem(n, 3), sem(n, 3), sem(n), sem(n)],
    )(*ws)


def sibling_halves(name, gs):
    n = len(gs)

    def body(*refs):
        g_refs, o_refs = refs[:n], refs[n:2 * n]
        send_sems, recv_sems = refs[2 * n:]
        x, y, c = _me()
        cps = []
        for a in range(n):
            rh = gs[a].shape[1] // 2
            for q in range(N_CHIPS):
                cps.append(pltpu.make_async_remote_copy(
                    src_ref=g_refs[a].at[q, pl.ds(pl.multiple_of((1 - c) * rh, 8), rh)], dst_ref=o_refs[a].at[q],
                    send_sem=send_sems.at[a, q], recv_sem=recv_sems.at[a, q], device_id=(x, y, 1 - c), device_id_type=MESH))
        for cp in cps:
            cp.start()
        for cp in cps:
            cp.wait_recv()
        for cp in cps:
            cp.wait_send()

    return pl.pallas_call(
        body, name=name, out_shape=[jax.ShapeDtypeStruct((N_CHIPS, g.shape[1] // 2, g.shape[2]), g.dtype) for g in gs],
        in_specs=_any_specs(n), out_specs=_any_specs(n),
        scratch_shapes=[pltpu.SemaphoreType.DMA((n, N_CHIPS)), pltpu.SemaphoreType.DMA((n, N_CHIPS))],
    )(*gs)


def scatter_chips(name, css):
    n = len(css)

    def body(*refs):
        c_refs, o_refs = refs[:n], refs[n:2 * n]
        send_sems, recv_sems = refs[2 * n:]
        x, y, c = _me()
        chips = [(1 - x, y), (x, 1 - y), (1 - x, 1 - y)]
        cps = []
        for a in range(n):
            for j, (px, py) in enumerate(chips):
                cps.append(pltpu.make_async_remote_copy(
                    src_ref=c_refs[a].at[2 * px + py], dst_ref=o_refs[a].at[j], send_sem=send_sems.at[a, j],
                    recv_sem=recv_sems.at[a, j], device_id=(px, py, c), device_id_type=MESH))
        for cp in cps:
            cp.start()
        for cp in cps:
            cp.wait_recv()
        for cp in cps:
            cp.wait_send()

    return pl.pallas_call(
        body, name=name, out_shape=[jax.ShapeDtypeStruct((3,) + cs.shape[1:], cs.dtype) for cs in css],
        in_specs=_any_specs(n), out_specs=_any_specs(n),
        scratch_shapes=[pltpu.SemaphoreType.DMA((n, 3)), pltpu.SemaphoreType.DMA((n, 3))],
    )(*css)


def sibling_swap(name, halves):
    n = len(halves)

    def body(*refs):
        h_refs, o_refs = refs[:n], refs[n:2 * n]
        send_sems, recv_sems = refs[2 * n:]
        x, y, c = _me()
        cps = [pltpu.make_async_remote_copy(src_ref=h_refs[a], dst_ref=o_refs[a], send_sem=send_sems.at[a],
                                            recv_sem=recv_sems.at[a], device_id=(x, y, 1 - c), device_id_type=MESH)
               for a in range(n)]
        for cp in cps:
            cp.start()
        for cp in cps:
            cp.wait_recv()
        for cp in cps:
            cp.wait_send()

    return pl.pallas_call(
        body, name=name, out_shape=[jax.ShapeDtypeStruct(h.shape, h.dtype) for h in halves],
        in_specs=_any_specs(n), out_specs=_any_specs(n),
        scratch_shapes=[pltpu.SemaphoreType.DMA((n,)), pltpu.SemaphoreType.DMA((n,))],
    )(*halves)


def allgather_all(name, s):
    def body(s_ref, out_ref, send_sems, recv_sems, local_sem):
        x, y, c = _me()
        peers = [(x ^ ((m >> 2) & 1), y ^ ((m >> 1) & 1), c ^ (m & 1)) for m in range(1, 8)]
        mine = pltpu.make_async_copy(s_ref, out_ref.at[4 * x + 2 * y + c], local_sem)
        mine.start()

        def copy(j, slot):
            return pltpu.make_async_remote_copy(src_ref=s_ref, dst_ref=out_ref.at[slot], send_sem=send_sems.at[j],
                                                recv_sem=recv_sems.at[j], device_id=peers[j], device_id_type=MESH)
        sends = [copy(j, 4 * x + 2 * y + c) for j in range(7)]
        for cp in sends:
            cp.start()
        for j, (px, py, pc) in enumerate(peers):
            copy(j, 4 * px + 2 * py + pc).wait_recv()
        for cp in sends:
            cp.wait_send()
        mine.wait()

    return pl.pallas_call(
        body, name=name, out_shape=jax.ShapeDtypeStruct((8,) + s.shape, s.dtype),
        in_specs=_any_specs(1), out_specs=pl.BlockSpec(memory_space=pl.ANY),
        scratch_shapes=[pltpu.SemaphoreType.DMA((7,)), pltpu.SemaphoreType.DMA((7,)), pltpu.SemaphoreType.DMA(())],
    )(s)


def sum_slots(name, parts):
    n = len(parts)
    R, W = parts[0][0].shape[1:]
    tm = _tile_rows(R, W)
    idx = jnp.stack([jnp.asarray(s, jnp.int32) for _, s in parts])

    def body(idx_ref, *refs):
        acc = refs[0][...].astype(F32)
        for r in refs[1:n]:
            acc = acc + r[...].astype(F32)
        refs[n][...] = acc

    grid_spec = pltpu.PrefetchScalarGridSpec(
        num_scalar_prefetch=1, grid=(R // tm,),
        in_specs=[pl.BlockSpec((None, tm, W), lambda i, idx, k=k: (idx[k], i, 0)) for k in range(n)],
        out_specs=pl.BlockSpec((tm, W), lambda i, idx: (i, 0)))
    return pl.pallas_call(body, name=name, grid_spec=grid_spec, out_shape=jax.ShapeDtypeStruct((R, W), F32),
                          compiler_params=pltpu.CompilerParams(dimension_semantics=("parallel",)))(idx, *[a for a, _ in parts])


def chip_sums(name, g, recv, c):
    _, R, W = g.shape
    rh = R // 2
    tm = _tile_rows(rh, W)
    g8 = g.reshape(2 * N_CHIPS, rh, W)
    idx = jnp.asarray(c, jnp.int32).reshape(1)

    def body(idx_ref, g_ref, r_ref, o_ref):
        o_ref[...] = (g_ref[...] + r_ref[...]).astype(o_ref.dtype)

    grid_spec = pltpu.PrefetchScalarGridSpec(
        num_scalar_prefetch=1, grid=(N_CHIPS, rh // tm),
        in_specs=[pl.BlockSpec((None, tm, W), lambda q, i, idx: (2 * q + idx[0], i, 0)),
                  pl.BlockSpec((None, tm, W), lambda q, i, idx: (q, i, 0))],
        out_specs=pl.BlockSpec((None, tm, W), lambda q, i, idx: (q, i, 0)))
    return pl.pallas_call(body, name=name, grid_spec=grid_spec, out_shape=jax.ShapeDtypeStruct((N_CHIPS, rh, W), BF16),
                          compiler_params=pltpu.CompilerParams(dimension_semantics=("parallel", "parallel")))(idx, g8, recv)


def _adamw_update(w, g, m, v):
    m = ADAM_B1 * m + (1.0 - ADAM_B1) * g
    v = ADAM_B2 * v + (1.0 - ADAM_B2) * (g * g)
    m_hat = m / (1.0 - ADAM_B1 ** ADAM_STEP)
    v_hat = v / (1.0 - ADAM_B2 ** ADAM_STEP)
    return -ADAM_LR * (m_hat / (jnp.sqrt(v_hat) + ADAM_EPS) + ADAM_WD * w), m, v


def adamw(name, w, g, m, v):
    R, W = w.shape
    return rowwise(name, lambda i, w, g, m, v: _adamw_update(w, g, m, v), [(a, W, 0) for a in (w, g, m, v)], [],
                   [("new", W, F32)] * 3, n_rows=R, tm=_tile_rows(R, W, 2 ** 20))


def adamw_halves(name, w, g_mine, g_other, m, v, c):
    R, W = w.shape
    rh = R // 2
    tm = _tile_rows(rh, W, 2 ** 20)
    nh = rh // tm
    idx = jnp.asarray(c, jnp.int32).reshape(1)

    def body(idx_ref, w_ref, ga_ref, gb_ref, m_ref, v_ref, g_ref, d_ref, mo_ref, vo_ref):
        mine = (pl.program_id(0) // nh) == idx_ref[0]
        g = jnp.where(mine, ga_ref[...], gb_ref[...])
        d, m, v = _adamw_update(w_ref[...], g, m_ref[...], v_ref[...])
        g_ref[...] = g
        d_ref[...] = d
        mo_ref[...] = m
        vo_ref[...] = v

    full = pl.BlockSpec((tm, W), lambda i, idx: (i, 0))
    half = pl.BlockSpec((tm, W), lambda i, idx: (i % nh, 0))
    grid_spec = pltpu.PrefetchScalarGridSpec(num_scalar_prefetch=1, grid=(R // tm,), in_specs=[full, half, half, full, full],
                                             out_specs=[full] * 4)
    return pl.pallas_call(body, name=name, grid_spec=grid_spec, out_shape=[jax.ShapeDtypeStruct((R, W), F32)] * 4,
                          compiler_params=pltpu.CompilerParams(dimension_semantics=("parallel",)))(idx, w, g_mine, g_other, m, v)


def _w_in_padded(w):
    return jnp.concatenate([w[:, :4096], w[:, 4112:], w[:, 4096:4112], jnp.zeros((D, PW - D_PROJ), w.dtype)], axis=1)


def _w_in_unpadded(g):
    return jnp.concatenate([g[:, :4096], g[:, OFF_BA:OFF_BA + 16], g[:, 4096:OFF_BA]], axis=1)


def _ret_consts():
    f = F32
    log_gamma = jnp.log1p(-jnp.exp2(-5.0 - jnp.arange(H, dtype=f)))
    pos = jnp.arange(CH, dtype=f)
    causal = jnp.tril(jnp.ones((CH, CH), dtype=bool))
    diff = pos[:, None] - pos[None, :]
    decay = jnp.where(causal, jnp.exp(jnp.where(causal, diff, 0.0) * log_gamma[:, None, None]), 0.0)
    xi = jnp.broadcast_to(jnp.exp((pos + 1.0) * log_gamma[:, None])[:, :, None], (H, CH, DH))
    zeta = jnp.broadcast_to(jnp.exp((CH - 1.0 - pos) * log_gamma[:, None])[:, :, None], (H, CH, DH))
    cd = jnp.broadcast_to(jnp.exp(CH * log_gamma)[:, None, None], (H, 1, DH))
    return decay, xi, zeta, cd


def _rope_tables(lp):
    pos = jnp.arange(lp, dtype=F32) - float(PAD)
    inv = 1.0 / (ROPE_BASE ** jnp.linspace(0.0, 1.0, DH // 2, dtype=F32))
    ang = pos[:, None] * inv[None, :]
    cos, sin = jnp.cos(ang), jnp.sin(ang)
    ct = jnp.repeat(cos, 2, axis=1)
    st = jnp.stack([-sin, sin], axis=-1).reshape(lp, DH)
    return ct, st


def local_step(x, tgt, w, small):
    seq = x.shape[0]
    lp = HEAD_ROWS + seq
    nc = lp // CH
    tm = _pick(lp, 192, CH)
    row = functools.partial(rowwise, n_rows=lp, tm=tm)

    h0 = jnp.concatenate([jnp.zeros((PAD, D), F32), w["meta_tokens"], x], axis=0)
    tgt_p = jnp.concatenate([jnp.zeros((HEAD_ROWS, D), F32), tgt], axis=0)

    def ffn_fwd(tag, h, wn, w_in, w_out):
        n = row(f"{tag}_norm", lambda i, h, wn: _rms(h, wn), [(h, D, 0)], [wn], [("new", D, BF16)])[0]
        gu = matmul(f"{tag}_up", n, w_in, "nn", b_split=True)
        mid = row(f"{tag}_act", lambda i, gu: _swiglu(gu), [(gu, 2 * DFF, 0)], [], [("new", DFF, BF16)])[0]
        out = matmul(f"{tag}_down", mid, w_out, "nn", resid=h, alpha=0.5, tj_cap=1024)
        return out, (h, n, gu, mid)

    def ffn_bwd(tag, dh, saved, wn, w_in, w_out):
        h, n, gu, mid = saved
        dmid = matmul(f"{tag}_dmid", dh, w_out, "nt", alpha=0.5, tj_cap=1408)
        dw_out = matmul(f"{tag}_dwout", mid, dh, "tn", alpha=0.5, ti_cap=1408)

        def act_bwd(i, gu, dmid):
            _, vjp = jax.vjp(_swiglu, gu)
            return vjp(dmid)[0]
        dgu = row(f"{tag}_dact", act_bwd, [(gu, 2 * DFF, 0), (dmid, DFF, 0)], [], [("new", 2 * DFF, BF16)])[0]
        dn = matmul(f"{tag}_dn", dgu, w_in, "nt", tj_cap=1024, b_split=True)
        dw_in = matmul(f"{tag}_dwin", n, dgu, "tn", ti_cap=1024, tr_cap=688, o_split=True)
        dh_in, dwn = norm_bwd(f"{tag}_dnorm", h, wn, dn, dh)
        return dh_in, dw_in, dw_out, dwn

    def norm_bwd(name, h, wn, dn, dres):
        def fn(i, h, dn, dres, wn):
            _, vjp = jax.vjp(_rms, h, wn)
            dh, dw = vjp(dn)
            return dres + dh, dw
        return row(name, fn, [(h, D, 0), (dn, D, 0), (dres, D, 0)], [wn], [("new", D, F32)], [(1, D)])

    h1, ffn1_saved = ffn_fwd("ffn1", h0, small["ffn1_norm"], w["ffn1_w_in"], w["ffn1_w_out"])
    n2 = row("mix_norm", lambda i, h, wn: _rms(h, wn), [(h1, D, 0)], [small["mix_norm"]], [("new", D, BF16)])[0]
    p = matmul("mix_proj", n2, w["w_in_p"], "nn", tj_cap=1152)

    cws = [w["gdn_conv_w"][j:j + 1] for j in range(CONV_K)]
    alog_row = jnp.zeros((1, LANES), F32).at[:, H:2 * H].set(small["gdn_a_log"])
    dtb_row = jnp.zeros((1, LANES), F32).at[:, H:2 * H].set(small["gdn_dt_bias"])
    q, k, v, gcb, bb, gc = gdn_pre_fwd(p, cws, alog_row, dtb_row, lp, tm)
    gc_rows = gc[:, H:2 * H].reshape(nc, CH, H).transpose(0, 2, 1)
    gdn_blocks = [(q, 0), (k, 0), (v, 0), (gcb, 0), (bb, 0)]
    o_a, gdn_states = _scan_fwd("gdn_scan_fwd", _gdn_chunk, gdn_blocks, [gc_rows], [], nc)
    y_a = row("gdn_post", lambda i, o, z, wn: _gdn_post(o, z, wn), [(o_a, D, 0), (p, D, OFF_Z // D)], [small["gdn_out_norm"]],
              [("new", D, BF16)])[0]

    ct, st = _rope_tables(lp)

    def rope_fwd(i, rq, rk, ct, st):
        c8, s8 = _tile_heads(ct), _tile_heads(st)
        return rq * c8 + _swap_pairs(rq) * s8, (rk * c8 + _swap_pairs(rk) * s8) * (DH ** -0.5)
    rq, rk = row("ret_pre", rope_fwd, [(p, D, OFF_RQ // D), (p, D, OFF_RK // D), (ct, DH, 0), (st, DH, 0)], [],
                 [("new", D, F32)] * 2)
    ret_consts = list(_ret_consts())
    ret_blocks = [(rq, 0), (rk, 0), (p, OFF_RV // D)]
    o_b, ret_states = _scan_fwd("ret_scan_fwd", _ret_chunk, ret_blocks, [], ret_consts, nc)
    y_b = row("ret_post", lambda i, o, g, wn: _ret_post(o, g, wn), [(o_b, D, 0), (p, D, OFF_RG // D)], [small["ret_out_norm"]],
              [("new", D, BF16)])[0]

    br_a = matmul("branch_gdn", y_a, w["w_branch_gdn"], "nn")
    br_b = matmul("branch_ret", y_b, w["w_branch_ret"], "nn")
    merged = row("merge", lambda i, a, b, ga, gb_: _merge(a, b, ga, gb_),
                 [(br_a, D, 0), (br_b, D, 0), (p, D, OFF_GA // D), (p, D, OFF_GB // D)], [], [("new", D, BF16)])[0]
    h2 = matmul("mix_out", merged, w["w_out"], "nn", resid=h1)
    h3, ffn2_saved = ffn_fwd("ffn2", h2, small["ffn2_norm"], w["ffn2_w_in"], w["ffn2_w_out"])

    def head(i, h, t, wn):
        mask = (_row_ids(i, tm) >= HEAD_ROWS).astype(F32)
        y, vjp = jax.vjp(_rms, h, wn)
        err = (y - t) * mask
        dh, dw = vjp(err * (1.0 / D))
        return dh, dw, jnp.sum(err * err, keepdims=True).reshape(1, 1) * (0.5 / D) * jnp.ones((1, LANES), F32)
    dh3, d_final, loss_row = row("loss_head", head, [(h3, D, 0), (tgt_p, D, 0)], [small["final_norm"]], [("new", D, F32)],
                                 [(1, D), (1, LANES)])

    gw, gs = {}, {"final_norm": d_final}
    dh2, gw["ffn2_w_in"], gw["ffn2_w_out"], gs["ffn2_norm"] = ffn_bwd("ffn2", dh3, ffn2_saved, small["ffn2_norm"],
                                                                      w["ffn2_w_in"], w["ffn2_w_out"])
    dmerged = matmul("mix_out_dx", dh2, w["w_out"], "nt", tj_cap=1024)
    gw["w_out"] = matmul("mix_out_dw", merged, dh2, "tn", ti_cap=1024)
    dp = lax.empty((lp, PW), BF16)

    def merge_bwd(i, dm, a, b, ga, gb_):
        _, vjp = jax.vjp(_merge, a, b, ga, gb_)
        da, db, dga, dgb = vjp(dm)
        return da, db, jnp.concatenate([dga, dgb], axis=1)
    da, db_, dp = row("merge_bwd", merge_bwd,
                      [(dmerged, D, 0), (br_a, D, 0), (br_b, D, 0), (p, D, OFF_GA // D), (p, D, OFF_GB // D)], [],
                      [("new", D, BF16), ("new", D, BF16), ("into", dp, 2 * D, OFF_GA // (2 * D))])
    dy_a = matmul("branch_gdn_dx", da, w["w_branch_gdn"], "nt", tj_cap=1024)
    gw["w_branch_gdn"] = matmul("branch_gdn_dw", y_a, da, "tn", ti_cap=1024)
    dy_b = matmul("branch_ret_dx", db_, w["w_branch_ret"], "nt", tj_cap=1024)
    gw["w_branch_ret"] = matmul("branch_ret_dw", y_b, db_, "tn", ti_cap=1024)

    def gdn_post_bwd(i, o, z, dy, wn):
        _, vjp = jax.vjp(_gdn_post, o, z, wn)
        return vjp(dy)
    do_a, dp, gs["gdn_out_norm"] = row("gdn_post_bwd", gdn_post_bwd, [(o_a, D, 0), (p, D, OFF_Z // D), (dy_a, D, 0)],
                                       [small["gdn_out_norm"]], [("new", D, F32), ("into", dp, D, OFF_Z // D)], [(1, DH)])

    def ret_post_bwd(i, o, g, dy, wn):
        _, vjp = jax.vjp(_ret_post, o, g, wn)
        return vjp(dy)
    do_b, dp, gs["ret_out_norm"] = row("ret_post_bwd", ret_post_bwd, [(o_b, D, 0), (p, D, OFF_RG // D), (dy_b, D, 0)],
                                       [small["ret_out_norm"]], [("new", D, F32), ("into", dp, D, OFF_RG // D)], [(1, D)])

    drq, drk, dp = _scan_bwd("ret_scan_bwd", _ret_chunk, ret_blocks, [], ret_consts, ret_states, do_b,
                             [("new",), ("new",), ("into", dp, OFF_RV // D)], nc)

    def rope_bwd(i, drq, drk, ct, st):
        c8, s8 = _tile_heads(ct), _tile_heads(st)
        drk = drk * (DH ** -0.5)
        return jnp.concatenate([drq * c8 + _swap_pairs(drq * s8), drk * c8 + _swap_pairs(drk * s8)], axis=1)
    dp = row("ret_pre_bwd", rope_bwd, [(drq, D, 0), (drk, D, 0), (ct, DH, 0), (st, DH, 0)], [],
             [("into", dp, 2 * D, OFF_RQ // (2 * D))])[0]

    dq, dk, dv, dgcb, dbb, dgc_rows = _scan_bwd("gdn_scan_bwd", _gdn_chunk, gdn_blocks, [gc_rows], [], gdn_states, do_a,
                                                [("new",)] * 5, nc)
    dgc = jnp.pad(dgc_rows.transpose(0, 2, 1).reshape(lp, H), ((0, 0), (H, LANES - 2 * H)))
    dp, dba, dcw, dgate = gdn_pre_bwd(p, cws, alog_row, dtb_row, dq, dk, dv, dgcb, dbb, dgc, dp, lp, tm)
    dp = row("dp_ba", lambda i, t: t, [(dba, LANES, 0)], [], [("into", dp, LANES, OFF_BA // LANES)])[0]
    gw["gdn_conv_w"] = dcw[:CONV_K]
    gs["gdn_a_log"] = dgate[0:1, H:2 * H]
    gs["gdn_dt_bias"] = dgate[1:2, H:2 * H]

    dn2 = matmul("mix_proj_dx", dp, w["w_in_p"], "nt", tj_cap=1024, tr_cap=1152)
    gw["w_in_p"] = matmul("mix_proj_dw", n2, dp, "tn", ti_cap=1024, tj_cap=1152, tr_cap=688)
    dh1, gs["mix_norm"] = norm_bwd("mix_dnorm", h1, small["mix_norm"], dn2, dh2)
    dh0, gw["ffn1_w_in"], gw["ffn1_w_out"], gs["ffn1_norm"] = ffn_bwd("ffn1", dh1, ffn1_saved, small["ffn1_norm"],
                                                                      w["ffn1_w_in"], w["ffn1_w_out"])
    gw["meta_tokens"] = dh0[PAD:HEAD_ROWS]
    return loss_row, dh0[HEAD_ROWS:], gw, gs


BIG = ("ffn1_w_in", "ffn1_w_out", "w_in", "w_branch_gdn", "w_branch_ret", "w_out", "ffn2_w_in", "ffn2_w_out")
COL_SHARDED = ("ffn1_w_in", "w_in", "ffn2_w_in")
SMALL = ("ffn1_norm", "mix_norm", "ret_out_norm", "ffn2_norm", "final_norm", "gdn_out_norm", "gdn_a_log", "gdn_dt_bias")
WEIGHTS = ("meta_tokens", "ffn1_norm", "ffn1_w_in", "ffn1_w_out", "mix_norm", "w_in", "gdn_conv_w", "gdn_a_log", "gdn_dt_bias",
           "gdn_out_norm", "ret_out_norm", "w_branch_gdn", "w_branch_ret", "w_out", "ffn2_norm", "ffn2_w_in", "ffn2_w_out",
           "final_norm")
LOSS_ROW = 6
CONV_ROW0, META_ROW0, SMALL_ROWS = 8, 24, 40


def pack_small(vals):
    rows = [vals[n].reshape(1, D) for n in SMALL[:5]]
    r5 = jnp.concatenate([vals["gdn_out_norm"].reshape(1, DH), vals["gdn_a_log"].reshape(1, H), vals["gdn_dt_bias"].reshape(1, H),
                          jnp.zeros((1, D - DH - 2 * H), F32)], axis=1)
    return jnp.concatenate(rows + [r5, jnp.zeros((2, D), F32)], axis=0)


def unpack_small(packed, shapes):
    out = {n: packed[j].reshape(shapes[n]) for j, n in enumerate(SMALL[:5])}
    out["gdn_out_norm"] = packed[5, :DH].reshape(shapes["gdn_out_norm"])
    out["gdn_a_log"] = packed[5, DH:DH + H].reshape(shapes["gdn_a_log"])
    out["gdn_dt_bias"] = packed[5, DH + H:DH + 2 * H].reshape(shapes["gdn_dt_bias"])
    return out


def kernel(x, meta_tokens, ffn1_norm, ffn1_w_in, ffn1_w_out, mix_norm, w_in, gdn_conv_w, gdn_a_log, gdn_dt_bias, gdn_out_norm, ret_out_norm, w_branch_gdn, w_branch_ret, w_out, ffn2_norm, ffn2_w_in, ffn2_w_out, final_norm, loss_target, m_meta_tokens, m_ffn1_norm, m_ffn1_w_in, m_ffn1_w_out, m_mix_norm, m_w_in, m_gdn_conv_w, m_gdn_a_log, m_gdn_dt_bias, m_gdn_out_norm, m_ret_out_norm, m_w_branch_gdn, m_w_branch_ret, m_w_out, m_ffn2_norm, m_ffn2_w_in, m_ffn2_w_out, m_final_norm, v_meta_tokens, v_ffn1_norm, v_ffn1_w_in, v_ffn1_w_out, v_mix_norm, v_w_in, v_gdn_conv_w, v_gdn_a_log, v_gdn_dt_bias, v_gdn_out_norm, v_ret_out_norm, v_w_branch_gdn, v_w_branch_ret, v_w_out, v_ffn2_norm, v_ffn2_w_in, v_ffn2_w_out, v_final_norm):
    a = dict(locals())
    wts = {n: a[n] for n in WEIGHTS}
    mom_m = {n: a["m_" + n] for n in WEIGHTS}
    mom_v = {n: a["v_" + n] for n in WEIGHTS}
    shapes = {n: wts[n].shape for n in WEIGHTS}
    flat = lambda t: t.reshape(t.shape[-2:])
    c = lax.axis_index("c")
    chip = 2 * lax.axis_index("x") + lax.axis_index("y")

    exact = jnp.zeros((16, D), F32).at[0:3].set(wts["gdn_conv_w"].reshape(3, D)).at[3:7].set(wts["meta_tokens"].reshape(4, D))
    gathered = allgather_chips_two_level("gather_weights", [flat(wts[n]).astype(BF16) for n in BIG] + [exact])
    gat = dict(zip(BIG, gathered[:-1]))
    exact = gathered[-1]
    w = {n: gat[n].reshape(N_CHIPS * gat[n].shape[1], D) for n in BIG if n not in COL_SHARDED}
    w["ffn1_w_in"], w["ffn2_w_in"] = gat["ffn1_w_in"], gat["ffn2_w_in"]
    w["w_in_p"] = _w_in_padded(jnp.concatenate([gat["w_in"][q] for q in range(N_CHIPS)], axis=1))
    w["gdn_conv_w"] = jnp.concatenate([exact[q, 0:3].reshape(CONV_K, 3 * D // 4) for q in range(N_CHIPS)], axis=1)
    w["meta_tokens"] = jnp.concatenate([exact[q, 3:7].reshape(N_META, D // 4) for q in range(N_CHIPS)], axis=1)
    small = {n: wts[n].reshape(1, -1) for n in SMALL}

    loss_row, gx, gw, gs = local_step(x[0], loss_target[0], w, small)

    gblk = {n: gw[n].reshape(N_CHIPS, gw[n].shape[0] // N_CHIPS, D) for n in BIG if n not in COL_SHARDED}
    gblk["ffn1_w_in"], gblk["ffn2_w_in"] = gw["ffn1_w_in"], gw["ffn2_w_in"]
    gblk["w_in"] = jnp.stack(jnp.split(_w_in_unpadded(gw["w_in_p"]), N_CHIPS, axis=1))
    gl = [gblk[n] for n in BIG]
    from_sib = sibling_halves("grads_sibling", gl)
    chip_sum = [chip_sums(f"grads_chip_sum_{n}", g, r, c) for n, g, r in zip(BIG, gl, from_sib)]
    from_chips = scatter_chips("grads_scatter", chip_sum)
    halves = [sum_slots(f"grads_sum_{n}", [(cs, chip), (fc, 0), (fc, 1), (fc, 2)]) for n, cs, fc in zip(BIG, chip_sum, from_chips)]
    others = sibling_swap("grads_swap", halves)
    grads, delta, new_m, new_v = {}, {}, {}, {}
    for n, mine, other in zip(BIG, halves, others):
        res = adamw_halves(f"adamw_{n}", flat(wts[n]), mine, other, flat(mom_m[n]), flat(mom_v[n]), c)
        grads[n], delta[n], new_m[n], new_v[n] = (t.reshape(shapes[n]) for t in res)

    sm = jnp.concatenate([pack_small(gs).at[LOSS_ROW, :LANES].set(loss_row[0]),
                          gw["gdn_conv_w"].reshape(3 * CONV_K, D), jnp.zeros((META_ROW0 - CONV_ROW0 - 3 * CONV_K, D), F32),
                          gw["meta_tokens"]], axis=0)
    every = allgather_all("small_gather", sm)
    sm_sum = sum_slots("small_sum", [(every, s) for s in range(8)])
    d_s, m_s, v_s = adamw("adamw_small", pack_small(small), sm_sum[:8], pack_small({n: mom_m[n].reshape(1, -1) for n in SMALL}),
                          pack_small({n: mom_v[n].reshape(1, -1) for n in SMALL}))
    grads.update(unpack_small(sm_sum, shapes))
    delta.update(unpack_small(d_s, shapes))
    new_m.update(unpack_small(m_s, shapes))
    new_v.update(unpack_small(v_s, shapes))
    g_conv = lax.dynamic_slice_in_dim(sm_sum[CONV_ROW0:CONV_ROW0 + 3 * CONV_K].reshape(CONV_K, 3 * D), chip * (3 * D // 4), 3 * D // 4, 1)
    g_meta = lax.dynamic_slice_in_dim(sm_sum[META_ROW0:META_ROW0 + N_META], chip * (D // 4), D // 4, 1)
    for n, g in (("gdn_conv_w", g_conv), ("meta_tokens", g_meta)):
        d_, m_, v_ = adamw(f"adamw_{n}", flat(wts[n]), g, flat(mom_m[n]), flat(mom_v[n]))
        grads[n], delta[n], new_m[n], new_v[n] = (t.reshape(shapes[n]) for t in (g, d_, m_, v_))
    loss = sm_sum[LOSS_ROW, 0]

    return (loss, gx[None], *[grads[n] for n in WEIGHTS], *[delta[n] for n in WEIGHTS], *[new_m[n] for n in WEIGHTS],
            *[new_v[n] for n in WEIGHTS])
```

```python
import functools

import jax
import jax.numpy as jnp
from jax import lax
from jax.experimental import pallas as pl
from jax.experimental.pallas import tpu as pltpu

F32 = jnp.float32
BF16 = jnp.bfloat16
HI = lax.Precision.HIGHEST
MESH = pl.DeviceIdType.MESH

D = 1024
N_META = 16
PAD = 48
HEAD_ROWS = PAD + N_META
CH = 64
H = 8
DH = 128
DFF = 2816
CONV_K = 4
EPS = 1e-6
ROPE_BASE = 10000.0
LANES = 128
N_CHIPS = 4
VMEM_LIMIT = 56 * 2 ** 20

OFF_QKV, OFF_RQ, OFF_RK, OFF_RV, OFF_Z, OFF_RG, OFF_GA, OFF_GB, OFF_BA = 0, 3072, 4096, 5120, 6144, 7168, 8192, 9216, 10240
PW = 10368
D_PROJ = 10256

ADAM_LR, ADAM_B1, ADAM_B2, ADAM_EPS, ADAM_WD, ADAM_STEP = 0.001, 0.9, 0.999, 1e-08, 0.01, 10


def _pick(n, cap, mult):
    best = None
    for t in range(mult, min(n, cap) + 1, mult):
        if n % t == 0:
            best = t
    return best if best is not None else n


def _tile_rows(rows, cols, block_bytes=3 * 2 ** 19):
    return _pick(rows, max(8, block_bytes // (4 * cols)), 8)


def _dot(a, b, dims, prec=None):
    return lax.dot_general(a, b, (dims, ((), ())), precision=prec, preferred_element_type=F32)


def _mmh(a, b):
    return _dot(a, b, ((1,), (0,)), HI)


def _split(a):
    hi = a.astype(BF16)
    return hi, (a - hi.astype(F32)).astype(BF16)


_NN, _NT, _TN = ((2,), (1,)), ((2,), (2,)), ((1,), (1,))


def _bdot(a, b, dims):
    return lax.dot_general(a, b, (dims, ((0,), (0,))), preferred_element_type=F32)


def _bmm(a, b, dims):
    return _bdot(a.astype(BF16), b.astype(BF16), dims)


def _bdot3(a, b, dims):
    ah, al = _split(a)
    bh, bl = _split(b)
    if dims == _NN:
        m = a.shape[1]
        both = _bdot(jnp.concatenate([ah, al], axis=1), bh, dims)
        return both[:, :m] + (both[:, m:] + _bdot(ah, bl, dims))
    return _bdot(ah, bh, dims) + (_bdot(ah, bl, dims) + _bdot(al, bh, dims))


def matmul(name, a, b, mode, *, out_dtype=F32, resid=None, alpha=1.0, ti_cap=688, tj_cap=512, tr_cap=1408,
           b_split=False, o_split=False):
    if mode == "nn":
        I, R = a.shape
        J = N_CHIPS * b.shape[2] if b_split else b.shape[1]
    elif mode == "nt":
        I, R = a.shape
        J = b.shape[1] if b_split else b.shape[0]
    else:
        R, I = a.shape
        J = b.shape[1]
    ti = _pick(I, ti_cap, 16) if mode != "tn" else _pick(I, ti_cap, LANES)
    tj = _pick(J, tj_cap, LANES)
    tr = _pick(R, tr_cap, LANES) if mode != "tn" else _pick(R, tr_cap, 16)
    if mode == "nn":
        a_spec = pl.BlockSpec((ti, tr), lambda i, j, r: (i, r))
        if b_split:
            tj = J // N_CHIPS
            b_spec = pl.BlockSpec((None, tr, tj), lambda i, j, r: (j, r, 0))
        else:
            b_spec = pl.BlockSpec((tr, tj), lambda i, j, r: (r, j))
        dims = ((1,), (0,))
    elif mode == "nt":
        if b_split:
            tr = R // N_CHIPS
            b_spec = pl.BlockSpec((None, tj, tr), lambda i, j, r: (r, j, 0))
        else:
            b_spec = pl.BlockSpec((tj, tr), lambda i, j, r: (j, r))
        a_spec = pl.BlockSpec((ti, tr), lambda i, j, r: (i, r))
        dims = ((1,), (1,))
    else:
        if o_split:
            tj = J // N_CHIPS
        a_spec = pl.BlockSpec((tr, ti), lambda i, j, r: (r, i))
        b_spec = pl.BlockSpec((tr, tj), lambda i, j, r: (r, j))
        dims = ((0,), (0,))
    nr = R // tr
    assert I % ti == 0 and J % tj == 0 and R % tr == 0, (name, I, J, R, ti, tj, tr)
    if o_split:
        o_spec = pl.BlockSpec((None, ti, tj), lambda i, j, r: (j, i, 0))
        out_shape = jax.ShapeDtypeStruct((N_CHIPS, I, tj), out_dtype)
    else:
        o_spec = pl.BlockSpec((ti, tj), lambda i, j, r: (i, j))
        out_shape = jax.ShapeDtypeStruct((I, J), out_dtype)
    has_resid = resid is not None

    def finish(o_ref, r_ref, res):
        res = res * alpha if alpha != 1.0 else res
        if has_resid:
            res = r_ref[...] + res
        o_ref[...] = res.astype(o_ref.dtype)

    def body(*refs):
        a_ref, b_ref = refs[:2]
        r_ref = refs[2] if has_resid else None
        o_ref = refs[3] if has_resid else refs[2]
        prod = _dot(a_ref[...].astype(BF16), b_ref[...].astype(BF16), dims)
        if nr == 1:
            finish(o_ref, r_ref, prod)
            return
        acc = refs[-1]
        r = pl.program_id(2)

        @pl.when(r == 0)
        def _():
            acc[...] = prod

        @pl.when(r > 0)
        def _():
            acc[...] += prod

        @pl.when(r == nr - 1)
        def _():
            finish(o_ref, r_ref, acc[...])

    ins = [a, b] + ([resid] if has_resid else [])
    specs = [a_spec, b_spec] + ([o_spec] if has_resid else [])
    return pl.pallas_call(
        body, name=name, grid=(I // ti, J // tj, nr), in_specs=specs, out_specs=o_spec, out_shape=out_shape,
        scratch_shapes=[pltpu.VMEM((ti, tj), F32)] if nr > 1 else [],
        compiler_params=pltpu.CompilerParams(dimension_semantics=("parallel", "parallel", "arbitrary"),
                                             vmem_limit_bytes=VMEM_LIMIT),
    )(*ins)


def rowwise(name, fn, rows, pars, outs, accs=(), *, n_rows, tm):
    nt = n_rows // tm
    assert nt * tm == n_rows
    in_specs, ins = [], []
    for arr, w, cb in rows:
        in_specs.append(pl.BlockSpec((tm, w), lambda i, cb=cb: (i, cb)))
        ins.append(arr)
    for p in pars:
        in_specs.append(pl.BlockSpec(p.shape, lambda i, nd=p.ndim: (0,) * nd))
        ins.append(p)
    n_in = len(ins)
    out_specs, out_shapes, aliases = [], [], {}
    for k, o in enumerate(outs):
        if o[0] == "new":
            _, w, dt = o
            out_specs.append(pl.BlockSpec((tm, w), lambda i: (i, 0)))
            out_shapes.append(jax.ShapeDtypeStruct((n_rows, w), dt))
        else:
            _, arr, w, cb = o
            in_specs.append(pl.BlockSpec(memory_space=pl.ANY))
            aliases[len(ins)] = k
            ins.append(arr)
            out_specs.append(pl.BlockSpec((tm, w), lambda i, cb=cb: (i, cb)))
            out_shapes.append(jax.ShapeDtypeStruct(arr.shape, arr.dtype))
    for r, w in accs:
        out_specs.append(pl.BlockSpec((r, w), lambda i: (0, 0)))
        out_shapes.append(jax.ShapeDtypeStruct((r, w), F32))
    n_all_in, n_out, n_acc = len(ins), len(outs), len(accs)

    def body(*refs):
        i = pl.program_id(0)
        vals = [r[...] for r in refs[:n_in]]
        res = fn(i, *vals)
        if not isinstance(res, (tuple, list)):
            res = (res,)
        o_refs = refs[n_all_in:n_all_in + n_out]
        a_refs = refs[n_all_in + n_out:]
        for r, v in zip(o_refs, res[:n_out]):
            r[...] = v.astype(r.dtype)
        if n_acc:
            @pl.when(i == 0)
            def _():
                for r in a_refs:
                    r[...] = jnp.zeros_like(r)
            for r, v in zip(a_refs, res[n_out:]):
                r[...] += v

    return pl.pallas_call(
        body, name=name, grid=(nt,), in_specs=in_specs, out_specs=out_specs, out_shape=out_shapes,
        input_output_aliases=aliases,
        compiler_params=pltpu.CompilerParams(dimension_semantics=("arbitrary",), vmem_limit_bytes=VMEM_LIMIT),
    )(*ins)


def _row_ids(i, tm):
    return i * tm + lax.broadcasted_iota(jnp.int32, (tm, 1), 0)


def _sigmoid(x):
    return 1.0 / (1.0 + jnp.exp(-x))


def _silu(x):
    return x * _sigmoid(x)


def _softplus(x):
    return jnp.maximum(x, 0.0) + jnp.log(1.0 + jnp.exp(-jnp.abs(x)))


def _rms(x, w):
    return x * lax.rsqrt(jnp.mean(x * x, axis=-1, keepdims=True) + EPS) * w


def _heads(fn, *xs):
    return jnp.concatenate([fn(h, *[x[:, h * DH:(h + 1) * DH] for x in xs]) for h in range(H)], axis=1)


def _swiglu(gu):
    return _silu(gu[:, :DFF]) * gu[:, DFF:]


def _gdn_post(o, z, w):
    return _heads(lambda h, oh, zh: oh * lax.rsqrt(jnp.mean(oh * oh, axis=-1, keepdims=True) + EPS) * w * _silu(zh), o, z)


def _ret_post(o, rg, w):
    def one(h, oh, gh, wh):
        mu = jnp.mean(oh, axis=-1, keepdims=True)
        xc = oh - mu
        var = jnp.mean(xc * xc, axis=-1, keepdims=True)
        return _silu(gh) * (xc * lax.rsqrt(var + EPS) * wh)
    return _heads(one, o, rg, jnp.broadcast_to(w, (o.shape[0], D)))


def _merge(a, b, ga, gb):
    return _sigmoid(ga) * a + _sigmoid(gb) * b


def _select_matrix(first_lane):
    r = lax.broadcasted_iota(jnp.int32, (LANES, H * DH), 0)
    c = lax.broadcasted_iota(jnp.int32, (LANES, H * DH), 1)
    return (r == first_lane + (c >> 7)).astype(F32)


def _chunk_tri(tm):
    r = lax.broadcasted_iota(jnp.int32, (tm, tm), 0)
    c = lax.broadcasted_iota(jnp.int32, (tm, tm), 1)
    return jnp.logical_and((r >> 6) == (c >> 6), r >= c).astype(F32)


def _gdn_gates(ba, alog_row, dtb_row, mask):
    g = -jnp.exp(alog_row) * _softplus(ba + dtb_row) * mask
    gc = _mmh(_chunk_tri(ba.shape[0]), g)
    beta = _sigmoid(ba) * mask
    return _mmh(gc, _select_matrix(H)), _mmh(beta, _select_matrix(0)), gc


def _gdn_qkv(c):
    a = _silu(c)

    def l2(scale):
        return lambda h, t: t * lax.rsqrt(jnp.sum(t * t, axis=-1, keepdims=True) + EPS) * scale
    q = _heads(l2(DH ** -0.5), a[:, :D])
    k = _heads(l2(1.0), a[:, D:2 * D])
    return q, k, a[:, 2 * D:]


def _conv_taps(xw, cws, tm):
    return sum(cws[i] * xw[5 + i:5 + i + tm] for i in range(CONV_K))


def _swap_pairs(t):
    n = t.shape[1]
    lane = lax.broadcasted_iota(jnp.int32, t.shape, 1)
    return jnp.where((lane & 1) == 0, pltpu.roll(t, n - 1, 1), pltpu.roll(t, 1, 1))


@jax.custom_vjp
def _unit_lower_inv(a):
    n = -a
    eye = (lax.broadcasted_iota(jnp.int32, a.shape, 1) == lax.broadcasted_iota(jnp.int32, a.shape, 2)).astype(F32)
    p = eye + n
    for _ in range(5):
        n = _bdot3(n, n, _NN)
        p = p + _bdot3(p, n, _NN)
    return p


def _inv_fwd(a):
    t = _unit_lower_inv(a)
    return t, t


def _inv_bwd(t, dt):
    x = _bdot3(t, dt, _TN)
    return (-_bdot3(x, t, _NT),)


_unit_lower_inv.defvjp(_inv_fwd, _inv_bwd)


def _gdn_chunk(q, k, v, gc, bb, gr, s):
    ri = lax.broadcasted_iota(jnp.int32, (H, CH, CH), 1)
    ci = lax.broadcasted_iota(jnp.int32, (H, CH, CH), 2)
    causal = ri >= ci
    gc1 = jnp.sum(gc, axis=2, keepdims=True) * (1.0 / LANES)
    diff = jnp.broadcast_to(gc1, (H, CH, CH)) - jnp.broadcast_to(gr, (H, CH, CH))
    decay = jnp.where(causal, jnp.exp(jnp.where(causal, diff, 0.0)), 0.0)
    kb = k * bb
    sc = _bmm(jnp.concatenate([q, kb], axis=1), k, _NT)
    qk = sc[:, :CH] * decay
    a = jnp.where(ri > ci, sc[:, CH:] * decay, 0.0)
    t = _unit_lower_inv(a)
    eg = jnp.exp(gc)
    uw = _bmm(t, jnp.concatenate([v * bb, kb * eg], axis=2), _NN)
    g_last = gc[:, CH - 1:CH, :]
    ws = _bmm(jnp.concatenate([uw[:, :, DH:], q * eg], axis=1), s, _NN)
    v_new = uw[:, :, :DH] - ws[:, :CH]
    o = ws[:, CH:] + _bmm(qk, v_new, _NN)
    s_new = s * jnp.exp(g_last) + _bmm(k * jnp.exp(g_last - gc), v_new, _TN)
    return o, s_new


def _swap_pairs_heads(t):
    return _swap_pairs(t.reshape(t.shape[0] * CH, DH)).reshape(t.shape)


@jax.custom_vjp
def _rope(t, ct, st):
    return t * ct + _swap_pairs_heads(t) * st


def _rope_fwd(t, ct, st):
    return _rope(t, ct, st), (ct, st)


def _rope_bwd(res, d):
    ct, st = res
    return d * ct + _swap_pairs_heads(d * st), jnp.zeros_like(ct), jnp.zeros_like(st)


_rope.defvjp(_rope_fwd, _rope_bwd)


def _ret_chunk(q, k, v, s, decay, xi, zeta, cd, ct, st):
    q = _rope(q, ct, st)
    k = _rope(k, ct, st) * (DH ** -0.5)
    scores = _bmm(q, k, _NT) * decay
    o = _bmm(scores, v, _NN) + _bmm(q * xi, s, _NN)
    s_new = s * cd + _bmm(k * zeta, v, _TN)
    return o, s_new


def _head_blocks(ref):
    return jnp.stack([ref[:, h * DH:(h + 1) * DH] for h in range(H)])


def _head_rows(ref):
    return jnp.stack([ref[0, h:h + 1, :] for h in range(H)])


def _scan_fwd(name, chunk_fn, blocks, rowvecs, consts, nc, shared=()):
    n_blk, n_rv, n_c = len(blocks), len(rowvecs), len(consts) + len(shared)

    def body(*refs):
        blk = refs[:n_blk]
        rvs = refs[n_blk:n_blk + n_rv]
        cst = refs[n_blk + n_rv:n_blk + n_rv + n_c]
        o_ref, st_ref, s_scr = refs[n_blk + n_rv + n_c:]

        @pl.when(pl.program_id(0) == 0)
        def _():
            s_scr[...] = jnp.zeros_like(s_scr)

        s = s_scr[...]
        st_ref[0] = s
        o, s_new = chunk_fn(*[_head_blocks(r) for r in blk], *[_head_rows(r) for r in rvs], s, *[r[...] for r in cst])
        for h in range(H):
            o_ref[:, h * DH:(h + 1) * DH] = o[h]
        s_scr[...] = s_new

    in_specs = [pl.BlockSpec((CH, H * DH), lambda c, f=f: (c, f)) for _, f in blocks]
    in_specs += [pl.BlockSpec((1, H, CH), lambda c: (c, 0, 0)) for _ in rowvecs]
    in_specs += [pl.BlockSpec(a.shape, lambda c: (0, 0, 0)) for a in consts]
    in_specs += [pl.BlockSpec((CH, a.shape[1]), lambda c: (c, 0)) for a in shared]
    return pl.pallas_call(
        body, name=name, grid=(nc,), in_specs=in_specs,
        out_specs=[pl.BlockSpec((CH, H * DH), lambda c: (c, 0)), pl.BlockSpec((1, H, DH, DH), lambda c: (c, 0, 0, 0))],
        out_shape=[jax.ShapeDtypeStruct((nc * CH, H * DH), F32), jax.ShapeDtypeStruct((nc, H, DH, DH), F32)],
        scratch_shapes=[pltpu.VMEM((H, DH, DH), F32)],
        compiler_params=pltpu.CompilerParams(dimension_semantics=("arbitrary",), vmem_limit_bytes=VMEM_LIMIT),
    )(*[a for a, _ in blocks], *rowvecs, *consts, *shared)


def _scan_bwd(name, chunk_fn, blocks, rowvecs, consts, states, do, into, nc, shared=()):
    n_blk, n_rv, n_c = len(blocks), len(rowvecs), len(consts) + len(shared)
    n_bo = 1 if into else n_blk

    def body(*refs):
        blk = refs[:n_blk]
        rvs = refs[n_blk:n_blk + n_rv]
        cst = refs[n_blk + n_rv:n_blk + n_rv + n_c]
        st_ref, do_ref = refs[n_blk + n_rv + n_c:n_blk + n_rv + n_c + 2]
        n_in = n_blk + n_rv + n_c + 2 + (1 if into else 0)
        o_refs = refs[n_in:n_in + n_bo]
        rv_refs = refs[n_in + n_bo:n_in + n_bo + n_rv]
        ds_scr = refs[n_in + n_bo + n_rv]

        @pl.when(pl.program_id(0) == 0)
        def _():
            ds_scr[...] = jnp.zeros_like(ds_scr)

        cv = [r[...] for r in cst]
        _, vjp = jax.vjp(lambda *a: chunk_fn(*a, *cv), *[_head_blocks(r) for r in blk], *[_head_rows(r) for r in rvs], st_ref[0])
        grads = vjp((_head_blocks(do_ref), ds_scr[...]))
        for h in range(H):
            for k, g in enumerate(grads[:n_blk]):
                r, col = (o_refs[0], k * H * DH) if into else (o_refs[k], 0)
                r[:, col + h * DH:col + (h + 1) * DH] = g[h].astype(r.dtype)
            for r, g in zip(rv_refs, grads[n_blk:n_blk + n_rv]):
                r[0, h:h + 1, :] = g[h]
        ds_scr[...] = grads[n_blk + n_rv]

    rc = lambda c: nc - 1 - c
    in_specs = [pl.BlockSpec((CH, H * DH), lambda c, f=f: (rc(c), f)) for _, f in blocks]
    in_specs += [pl.BlockSpec((1, H, CH), lambda c: (rc(c), 0, 0)) for _ in rowvecs]
    in_specs += [pl.BlockSpec(a.shape, lambda c: (0, 0, 0)) for a in consts]
    in_specs += [pl.BlockSpec((CH, a.shape[1]), lambda c: (rc(c), 0)) for a in shared]
    in_specs += [pl.BlockSpec((1, H, DH, DH), lambda c: (rc(c), 0, 0, 0)), pl.BlockSpec((CH, H * DH), lambda c: (rc(c), 0))]
    ins = [a for a, _ in blocks] + list(rowvecs) + list(consts) + list(shared) + [states, do]
    out_specs, out_shapes, aliases = [], [], {}
    if into:
        arr, f = into
        in_specs.append(pl.BlockSpec(memory_space=pl.ANY))
        aliases[len(ins)] = 0
        ins.append(arr)
        out_specs.append(pl.BlockSpec((CH, n_blk * H * DH), lambda c: (rc(c), f)))
        out_shapes.append(jax.ShapeDtypeStruct(arr.shape, arr.dtype))
    else:
        for _ in blocks:
            out_specs.append(pl.BlockSpec((CH, H * DH), lambda c: (rc(c), 0)))
            out_shapes.append(jax.ShapeDtypeStruct((nc * CH, H * DH), F32))
    for _ in rowvecs:
        out_specs.append(pl.BlockSpec((1, H, CH), lambda c: (rc(c), 0, 0)))
        out_shapes.append(jax.ShapeDtypeStruct((nc, H, CH), F32))
    return pl.pallas_call(
        body, name=name, grid=(nc,), in_specs=in_specs, out_specs=out_specs, out_shape=out_shapes,
        input_output_aliases=aliases, scratch_shapes=[pltpu.VMEM((H, DH, DH), F32)],
        compiler_params=pltpu.CompilerParams(dimension_semantics=("arbitrary",), vmem_limit_bytes=VMEM_LIMIT),
    )(*ins)


def _gdn_pre_specs(p, cws, alog_row, dtb_row, tm, pos):
    sub = tm // 8
    rows = [pl.BlockSpec((tm, 3 * D), lambda i: (pos(i), 0)),
            pl.BlockSpec((8, 3 * D), lambda i: (jnp.maximum(pos(i) * sub - 1, 0), 0)),
            pl.BlockSpec((tm, LANES), lambda i: (pos(i), OFF_BA // LANES))]
    pars = [pl.BlockSpec(a.shape, lambda i: (0, 0)) for a in (*cws, alog_row, dtb_row)]
    return rows + pars, [p, p, p, *cws, alog_row, dtb_row]


def gdn_pre_fwd(p, cws, alog_row, dtb_row, lp, tm):
    def body(x_ref, prev_ref, ba_ref, c0, c1, c2, c3, al_ref, dt_ref, q_ref, k_ref, v_ref, g_ref, b_ref, gc_ref):
        i = pl.program_id(0)
        prev = jnp.where(i > 0, prev_ref[...], 0.0)
        xw = jnp.concatenate([prev, x_ref[...]], axis=0)
        c = _conv_taps(xw, [c0[...], c1[...], c2[...], c3[...]], tm)
        q, k, v = _gdn_qkv(c)
        mask = (_row_ids(i, tm) >= PAD).astype(F32)
        g, b, gc = _gdn_gates(ba_ref[...], al_ref[...], dt_ref[...], mask)
        q_ref[...] = q
        k_ref[...] = k
        v_ref[...] = v
        g_ref[...] = g
        b_ref[...] = b
        gc_ref[...] = gc

    in_specs, ins = _gdn_pre_specs(p, cws, alog_row, dtb_row, tm, lambda i: i)
    o_spec = pl.BlockSpec((tm, D), lambda i: (i, 0))
    return pl.pallas_call(
        body, name="gdn_pre_fwd", grid=(lp // tm,), in_specs=in_specs,
        out_specs=[o_spec] * 5 + [pl.BlockSpec((tm, LANES), lambda i: (i, 0))],
        out_shape=[jax.ShapeDtypeStruct((lp, D), F32)] * 5 + [jax.ShapeDtypeStruct((lp, LANES), F32)],
        compiler_params=pltpu.CompilerParams(dimension_semantics=("parallel",), vmem_limit_bytes=VMEM_LIMIT),
    )(*ins)


def gdn_pre_bwd(p, cws, alog_row, dtb_row, dq, dk, dv, dg, db, dgc, dp, lp, tm):
    nt = lp // tm
    pos = lambda i: nt - 1 - i

    def body(x_ref, prev_ref, ba_ref, c0, c1, c2, c3, al_ref, dt_ref, dq_ref, dk_ref, dv_ref, dg_ref, db_ref, dgc_ref,
             dp_any, dx_ref, dba_ref, dcw_ref, dpar_ref, carry):
        i = pl.program_id(0)
        t = pos(i)

        @pl.when(i == 0)
        def _():
            carry[...] = jnp.zeros_like(carry)
            dcw_ref[...] = jnp.zeros_like(dcw_ref)
            dpar_ref[...] = jnp.zeros_like(dpar_ref)

        cws_v = [c0[...], c1[...], c2[...], c3[...]]
        prev = jnp.where(t > 0, prev_ref[...], 0.0)
        xw = jnp.concatenate([prev, x_ref[...]], axis=0)
        xs = [xw[5 + j:5 + j + tm] for j in range(CONV_K)]
        c = sum(cws_v[j] * xs[j] for j in range(CONV_K))
        _, vjp_qkv = jax.vjp(_gdn_qkv, c)
        (dc,) = vjp_qkv((dq_ref[...], dk_ref[...], dv_ref[...]))
        zeros8 = jnp.zeros((8, 3 * D), F32)
        dcp = jnp.concatenate([zeros8, dc, zeros8], axis=0)
        dxw = sum(cws_v[j] * dcp[3 - j:3 - j + tm + 8] for j in range(CONV_K))
        dx_ref[...] = jnp.concatenate([dxw[8:tm], dxw[tm:] + carry[...]], axis=0).astype(dx_ref.dtype)
        carry[...] = dxw[:8]
        for j in range(CONV_K):
            dcw_ref[j:j + 1, :] += jnp.sum(dc * xs[j], axis=0, keepdims=True)
        mask = (_row_ids(t, tm) >= PAD).astype(F32)
        _, vjp_g = jax.vjp(lambda ba, al, dt: _gdn_gates(ba, al, dt, mask), ba_ref[...], al_ref[...], dt_ref[...])
        dba, dal, ddt = vjp_g((dg_ref[...], db_ref[...], dgc_ref[...]))
        dba_ref[...] = dba
        dpar_ref[0:1, :] += dal
        dpar_ref[1:2, :] += ddt

    in_specs, ins = _gdn_pre_specs(p, cws, alog_row, dtb_row, tm, pos)
    g_spec = pl.BlockSpec((tm, D), lambda i: (pos(i), 0))
    s_spec = pl.BlockSpec((tm, LANES), lambda i: (pos(i), 0))
    in_specs += [g_spec] * 5 + [s_spec, pl.BlockSpec(memory_space=pl.ANY)]
    ins += [dq, dk, dv, dg, db, dgc, dp]
    return pl.pallas_call(
        body, name="gdn_pre_bwd", grid=(nt,), in_specs=in_specs,
        out_specs=[pl.BlockSpec((tm, 3 * D), lambda i: (pos(i), 0)), s_spec,
                   pl.BlockSpec((8, 3 * D), lambda i: (0, 0)), pl.BlockSpec((8, LANES), lambda i: (0, 0))],
        out_shape=[jax.ShapeDtypeStruct(dp.shape, dp.dtype), jax.ShapeDtypeStruct((lp, LANES), F32),
                   jax.ShapeDtypeStruct((8, 3 * D), F32), jax.ShapeDtypeStruct((8, LANES), F32)],
        input_output_aliases={len(ins) - 1: 0},
        scratch_shapes=[pltpu.VMEM((8, 3 * D), F32)],
        compiler_params=pltpu.CompilerParams(dimension_semantics=("arbitrary",), vmem_limit_bytes=VMEM_LIMIT),
    )(*ins)


def _me():
    return lax.axis_index("x"), lax.axis_index("y"), lax.axis_index("c")


def _any_specs(n):
    return [pl.BlockSpec(memory_space=pl.ANY)] * n


def allgather_chips_two_level(name, ws):
    n = len(ws)

    def body(*refs):
        w_refs, o_refs = refs[:n], refs[n:2 * n]
        send_sems, recv_sems, fsend_sems, frecv_sems, osend_sems, orecv_sems = refs[2 * n:]
        x, y, c = _me()
        me = 2 * x + y
        chips = [(1 - x, y), (x, 1 - y), (1 - x, 1 - y)]

        def own(a):
            return pltpu.make_async_remote_copy(src_ref=w_refs[a], dst_ref=o_refs[a].at[me], send_sem=osend_sems.at[a],
                                                recv_sem=orecv_sems.at[a], device_id=(x, y, 1 - c), device_id_type=MESH)
        mine = [own(a) for a in range(n)]
        for cp in mine:
            cp.start()

        def rows(a, cc):
            rh = ws[a].shape[0] // 2
            return pl.ds(pl.multiple_of(cc * rh, 8), rh)

        def ici(a, j, slot):
            px, py = chips[j]
            return pltpu.make_async_remote_copy(
                src_ref=w_refs[a].at[rows(a, c)], dst_ref=o_refs[a].at[slot, rows(a, c)], send_sem=send_sems.at[a, j],
                recv_sem=recv_sems.at[a, j], device_id=(px, py, c), device_id_type=MESH)

        def d2d(a, j, cc):
            px, py = chips[j]
            blk = o_refs[a].at[2 * px + py, rows(a, cc)]
            return pltpu.make_async_remote_copy(src_ref=blk, dst_ref=blk, send_sem=fsend_sems.at[a, j],
                                                recv_sem=frecv_sems.at[a, j], device_id=(x, y, 1 - c), device_id_type=MESH)
        sends = [ici(a, j, me) for a in range(n) for j in range(3)]
        for cp in sends:
            cp.start()
        passed = []
        for a in range(n):
            for j, (px, py) in enumerate(chips):
                ici(a, j, 2 * px + py).wait_recv()
                cp = d2d(a, j, c)
                cp.start()
                passed.append(cp)
        for a in range(n):
            for j in range(3):
                d2d(a, j, 1 - c).wait_recv()
        for cp in mine:
            cp.wait_recv()
        for cp in sends + passed + mine:
            cp.wait_send()

    sem = lambda *s: pltpu.SemaphoreType.DMA(s)
    return pl.pallas_call(
        body, name=name, out_shape=[jax.ShapeDtypeStruct((N_CHIPS,) + w.shape, w.dtype) for w in ws],
        in_specs=_any_specs(n), out_specs=_any_specs(n),
        scratch_shapes=[sem(n, 3), sem(n, 3), sem(n, 3), sem(n, 3), sem(n), sem(n)],
    )(*ws)


def sibling_halves(name, gs):
    n = len(gs)

    def body(*refs):
        g_refs, o_refs = refs[:n], refs[n:2 * n]
        send_sems, recv_sems = refs[2 * n:]
        x, y, c = _me()
        cps = []
        for a in range(n):
            rh = gs[a].shape[1] // 2
            for q in range(N_CHIPS):
                cps.append(pltpu.make_async_remote_copy(
                    src_ref=g_refs[a].at[q, pl.ds(pl.multiple_of((1 - c) * rh, 8), rh)], dst_ref=o_refs[a].at[q],
                    send_sem=send_sems.at[a, q], recv_sem=recv_sems.at[a, q], device_id=(x, y, 1 - c), device_id_type=MESH))
        for cp in cps:
            cp.start()
        for cp in cps:
            cp.wait_recv()
        for cp in cps:
            cp.wait_send()

    return pl.pallas_call(
        body, name=name, out_shape=[jax.ShapeDtypeStruct((N_CHIPS, g.shape[1] // 2, g.shape[2]), g.dtype) for g in gs],
        in_specs=_any_specs(n), out_specs=_any_specs(n),
        scratch_shapes=[pltpu.SemaphoreType.DMA((n, N_CHIPS)), pltpu.SemaphoreType.DMA((n, N_CHIPS))],
    )(*gs)


def scatter_chips(name, css):
    n = len(css)

    def body(*refs):
        c_refs, o_refs = refs[:n], refs[n:2 * n]
        send_sems, recv_sems = refs[2 * n:]
        x, y, c = _me()
        chips = [(1 - x, y), (x, 1 - y), (1 - x, 1 - y)]
        cps = []
        for a in range(n):
            for j, (px, py) in enumerate(chips):
                cps.append(pltpu.make_async_remote_copy(
                    src_ref=c_refs[a].at[2 * px + py], dst_ref=o_refs[a].at[j], send_sem=send_sems.at[a, j],
                    recv_sem=recv_sems.at[a, j], device_id=(px, py, c), device_id_type=MESH))
        for cp in cps:
            cp.start()
        for cp in cps:
            cp.wait_recv()
        for cp in cps:
            cp.wait_send()

    return pl.pallas_call(
        body, name=name, out_shape=[jax.ShapeDtypeStruct((3,) + cs.shape[1:], cs.dtype) for cs in css],
        in_specs=_any_specs(n), out_specs=_any_specs(n),
        scratch_shapes=[pltpu.SemaphoreType.DMA((n, 3)), pltpu.SemaphoreType.DMA((n, 3))],
    )(*css)


def sibling_swap(name, halves):
    n = len(halves)

    def body(*refs):
        h_refs, o_refs = refs[:n], refs[n:2 * n]
        send_sems, recv_sems = refs[2 * n:]
        x, y, c = _me()
        cps = [pltpu.make_async_remote_copy(src_ref=h_refs[a], dst_ref=o_refs[a], send_sem=send_sems.at[a],
                                            recv_sem=recv_sems.at[a], device_id=(x, y, 1 - c), device_id_type=MESH)
               for a in range(n)]
        for cp in cps:
            cp.start()
        for cp in cps:
            cp.wait_recv()
        for cp in cps:
            cp.wait_send()

    return pl.pallas_call(
        body, name=name, out_shape=[jax.ShapeDtypeStruct(h.shape, h.dtype) for h in halves],
        in_specs=_any_specs(n), out_specs=_any_specs(n),
        scratch_shapes=[pltpu.SemaphoreType.DMA((n,)), pltpu.SemaphoreType.DMA((n,))],
    )(*halves)


def allgather_all(name, s):
    def body(s_ref, out_ref, send_sems, recv_sems, local_sem):
        x, y, c = _me()
        peers = [(x ^ ((m >> 2) & 1), y ^ ((m >> 1) & 1), c ^ (m & 1)) for m in range(1, 8)]
        mine = pltpu.make_async_copy(s_ref, out_ref.at[4 * x + 2 * y + c], local_sem)
        mine.start()

        def copy(j, slot):
            return pltpu.make_async_remote_copy(src_ref=s_ref, dst_ref=out_ref.at[slot], send_sem=send_sems.at[j],
                                                recv_sem=recv_sems.at[j], device_id=peers[j], device_id_type=MESH)
        sends = [copy(j, 4 * x + 2 * y + c) for j in range(7)]
        for cp in sends:
            cp.start()
        for j, (px, py, pc) in enumerate(peers):
            copy(j, 4 * px + 2 * py + pc).wait_recv()
        for cp in sends:
            cp.wait_send()
        mine.wait()

    return pl.pallas_call(
        body, name=name, out_shape=jax.ShapeDtypeStruct((8,) + s.shape, s.dtype),
        in_specs=_any_specs(1), out_specs=pl.BlockSpec(memory_space=pl.ANY),
        scratch_shapes=[pltpu.SemaphoreType.DMA((7,)), pltpu.SemaphoreType.DMA((7,)), pltpu.SemaphoreType.DMA(())],
    )(s)


def sum_slots(name, parts):
    n = len(parts)
    R, W = parts[0][0].shape[1:]
    tm = _tile_rows(R, W)
    idx = jnp.stack([jnp.asarray(s, jnp.int32) for _, s in parts])

    def body(idx_ref, *refs):
        acc = refs[0][...].astype(F32)
        for r in refs[1:n]:
            acc = acc + r[...].astype(F32)
        refs[n][...] = acc

    grid_spec = pltpu.PrefetchScalarGridSpec(
        num_scalar_prefetch=1, grid=(R // tm,),
        in_specs=[pl.BlockSpec((None, tm, W), lambda i, idx, k=k: (idx[k], i, 0)) for k in range(n)],
        out_specs=pl.BlockSpec((tm, W), lambda i, idx: (i, 0)))
    return pl.pallas_call(body, name=name, grid_spec=grid_spec, out_shape=jax.ShapeDtypeStruct((R, W), F32),
                          compiler_params=pltpu.CompilerParams(dimension_semantics=("parallel",)))(idx, *[a for a, _ in parts])


def chip_sums(name, g, recv, c):
    _, R, W = g.shape
    rh = R // 2
    tm = _tile_rows(rh, W)
    g8 = g.reshape(2 * N_CHIPS, rh, W)
    idx = jnp.asarray(c, jnp.int32).reshape(1)

    def body(idx_ref, g_ref, r_ref, o_ref):
        o_ref[...] = (g_ref[...] + r_ref[...]).astype(o_ref.dtype)

    grid_spec = pltpu.PrefetchScalarGridSpec(
        num_scalar_prefetch=1, grid=(N_CHIPS, rh // tm),
        in_specs=[pl.BlockSpec((None, tm, W), lambda q, i, idx: (2 * q + idx[0], i, 0)),
                  pl.BlockSpec((None, tm, W), lambda q, i, idx: (q, i, 0))],
        out_specs=pl.BlockSpec((None, tm, W), lambda q, i, idx: (q, i, 0)))
    return pl.pallas_call(body, name=name, grid_spec=grid_spec, out_shape=jax.ShapeDtypeStruct((N_CHIPS, rh, W), BF16),
                          compiler_params=pltpu.CompilerParams(dimension_semantics=("parallel", "parallel")))(idx, g8, recv)


def _adamw_update(w, g, m, v):
    m = ADAM_B1 * m + (1.0 - ADAM_B1) * g
    v = ADAM_B2 * v + (1.0 - ADAM_B2) * (g * g)
    m_hat = m / (1.0 - ADAM_B1 ** ADAM_STEP)
    v_hat = v / (1.0 - ADAM_B2 ** ADAM_STEP)
    return -ADAM_LR * (m_hat / (jnp.sqrt(v_hat) + ADAM_EPS) + ADAM_WD * w), m, v


def adamw(name, w, g, m, v):
    R, W = w.shape
    return rowwise(name, lambda i, w, g, m, v: _adamw_update(w, g, m, v), [(a, W, 0) for a in (w, g, m, v)], [],
                   [("new", W, F32)] * 3, n_rows=R, tm=_tile_rows(R, W, 2 ** 20))


def adamw_halves(name, w, g_mine, g_other, m, v, c):
    R, W = w.shape
    rh = R // 2
    tm = _tile_rows(rh, W, 2 ** 20)
    nh = rh // tm
    idx = jnp.asarray(c, jnp.int32).reshape(1)

    def body(idx_ref, w_ref, ga_ref, gb_ref, m_ref, v_ref, g_ref, d_ref, mo_ref, vo_ref):
        mine = (pl.program_id(0) // nh) == idx_ref[0]
        g = jnp.where(mine, ga_ref[...], gb_ref[...])
        d, m, v = _adamw_update(w_ref[...], g, m_ref[...], v_ref[...])
        g_ref[...] = g
        d_ref[...] = d
        mo_ref[...] = m
        vo_ref[...] = v

    full = pl.BlockSpec((tm, W), lambda i, idx: (i, 0))
    half = pl.BlockSpec((tm, W), lambda i, idx: (i % nh, 0))
    grid_spec = pltpu.PrefetchScalarGridSpec(num_scalar_prefetch=1, grid=(R // tm,), in_specs=[full, half, half, full, full],
                                             out_specs=[full] * 4)
    return pl.pallas_call(body, name=name, grid_spec=grid_spec, out_shape=[jax.ShapeDtypeStruct((R, W), F32)] * 4,
                          compiler_params=pltpu.CompilerParams(dimension_semantics=("parallel",)))(idx, w, g_mine, g_other, m, v)


def _w_in_padded(w):
    return jnp.concatenate([w[:, :3072], w[:, 4112:7184], w[:, 3072:4096], w[:, 7184:], w[:, 4096:4112],
                            jnp.zeros((D, PW - D_PROJ), w.dtype)], axis=1)


def _w_in_unpadded(g):
    return jnp.concatenate([g[:, :3072], g[:, OFF_Z:OFF_Z + D], g[:, OFF_BA:OFF_BA + 16], g[:, OFF_RQ:OFF_Z], g[:, OFF_RG:OFF_BA]],
                           axis=1)


def _ret_consts():
    f = F32
    log_gamma = jnp.log1p(-jnp.exp2(-5.0 - jnp.arange(H, dtype=f)))
    pos = jnp.arange(CH, dtype=f)
    causal = jnp.tril(jnp.ones((CH, CH), dtype=bool))
    diff = pos[:, None] - pos[None, :]
    decay = jnp.where(causal, jnp.exp(jnp.where(causal, diff, 0.0) * log_gamma[:, None, None]), 0.0)
    xi = jnp.broadcast_to(jnp.exp((pos + 1.0) * log_gamma[:, None])[:, :, None], (H, CH, DH))
    zeta = jnp.broadcast_to(jnp.exp((CH - 1.0 - pos) * log_gamma[:, None])[:, :, None], (H, CH, DH))
    cd = jnp.broadcast_to(jnp.exp(CH * log_gamma)[:, None, None], (H, 1, DH))
    return decay, xi, zeta, cd


def _rope_tables(lp):
    pos = jnp.arange(lp, dtype=F32) - float(PAD)
    inv = 1.0 / (ROPE_BASE ** jnp.linspace(0.0, 1.0, DH // 2, dtype=F32))
    ang = pos[:, None] * inv[None, :]
    cos, sin = jnp.cos(ang), jnp.sin(ang)
    ct = jnp.repeat(cos, 2, axis=1)
    st = jnp.stack([-sin, sin], axis=-1).reshape(lp, DH)
    return ct, st


def local_step(x, tgt, w, small):
    seq = x.shape[0]
    lp = HEAD_ROWS + seq
    nc = lp // CH
    tm = _pick(lp, 192, CH)
    row = functools.partial(rowwise, n_rows=lp, tm=tm)

    h0 = jnp.concatenate([jnp.zeros((PAD, D), F32), w["meta_tokens"], x], axis=0)
    tgt_p = jnp.concatenate([jnp.zeros((HEAD_ROWS, D), F32), tgt], axis=0)

    def ffn_fwd(tag, h, wn, w_in, w_out):
        n = row(f"{tag}_norm", lambda i, h, wn: _rms(h, wn), [(h, D, 0)], [wn], [("new", D, BF16)])[0]
        gu = matmul(f"{tag}_up", n, w_in, "nn", b_split=True, ti_cap=1376)
        mid = row(f"{tag}_act", lambda i, gu: _swiglu(gu), [(gu, 2 * DFF, 0)], [], [("new", DFF, BF16)])[0]
        out = matmul(f"{tag}_down", mid, w_out, "nn", resid=h, alpha=0.5, tj_cap=1024, tr_cap=2816)
        return out, (h, n, gu, mid)

    def ffn_bwd(tag, dh, saved, wn, w_in, w_in_full, w_out):
        h, n, gu, mid = saved
        dmid = matmul(f"{tag}_dmid", dh, w_out, "nt", alpha=0.5, ti_cap=1376, tj_cap=1408)
        dw_out = matmul(f"{tag}_dwout", mid, dh, "tn", alpha=0.5, ti_cap=1408, tj_cap=1024, tr_cap=1376)

        def act_bwd(i, gu, dmid):
            _, vjp = jax.vjp(_swiglu, gu)
            return vjp(dmid)[0]
        dgu = row(f"{tag}_dact", act_bwd, [(gu, 2 * DFF, 0), (dmid, DFF, 0)], [], [("new", 2 * DFF, BF16)])[0]
        dn = matmul(f"{tag}_dn", dgu, w_in_full, "nt", tj_cap=1024, tr_cap=2 * DFF)
        dw_in = matmul(f"{tag}_dwin", n, dgu, "tn", ti_cap=1024, tr_cap=2752, o_split=True)
        dh_in, dwn = norm_bwd(f"{tag}_dnorm", h, wn, dn, dh)
        return dh_in, dw_in, dw_out, dwn

    def norm_bwd(name, h, wn, dn, dres):
        def fn(i, h, dn, dres, wn):
            _, vjp = jax.vjp(_rms, h, wn)
            dh, dw = vjp(dn)
            return dres + dh, dw
        return row(name, fn, [(h, D, 0), (dn, D, 0), (dres, D, 0)], [wn], [("new", D, F32)], [(1, D)])

    h1, ffn1_saved = ffn_fwd("ffn1", h0, small["ffn1_norm"], w["ffn1_w_in"], w["ffn1_w_out"])
    n2 = row("mix_norm", lambda i, h, wn: _rms(h, wn), [(h1, D, 0)], [small["mix_norm"]], [("new", D, BF16)])[0]
    p = matmul("mix_proj", n2, w["w_in_p"], "nn", ti_cap=1376, tj_cap=1152)

    cws = [w["gdn_conv_w"][j:j + 1] for j in range(CONV_K)]
    alog_row = jnp.zeros((1, LANES), F32).at[:, H:2 * H].set(small["gdn_a_log"])
    dtb_row = jnp.zeros((1, LANES), F32).at[:, H:2 * H].set(small["gdn_dt_bias"])
    q, k, v, gcb, bb, gc = gdn_pre_fwd(p, cws, alog_row, dtb_row, lp, tm)
    gc_rows = gc[:, H:2 * H].reshape(nc, CH, H).transpose(0, 2, 1)
    gdn_blocks = [(q, 0), (k, 0), (v, 0), (gcb, 0), (bb, 0)]
    o_a, gdn_states = _scan_fwd("gdn_scan_fwd", _gdn_chunk, gdn_blocks, [gc_rows], [], nc)
    y_a = row("gdn_post", lambda i, o, z, wn: _gdn_post(o, z, wn), [(o_a, D, 0), (p, D, OFF_Z // D)], [small["gdn_out_norm"]],
              [("new", D, BF16)])[0]

    ct, st = _rope_tables(lp)

    ret_consts = list(_ret_consts())
    ret_blocks = [(p, OFF_RQ // D), (p, OFF_RK // D), (p, OFF_RV // D)]
    o_b, ret_states = _scan_fwd("ret_scan_fwd", _ret_chunk, ret_blocks, [], ret_consts, nc, shared=[ct, st])
    y_b = row("ret_post", lambda i, o, g, wn: _ret_post(o, g, wn), [(o_b, D, 0), (p, D, OFF_RG // D)], [small["ret_out_norm"]],
              [("new", D, BF16)])[0]

    br_a = matmul("branch_gdn", y_a, w["w_branch_gdn"], "nn", ti_cap=1376, tj_cap=1024)
    br_b = matmul("branch_ret", y_b, w["w_branch_ret"], "nn", ti_cap=1376, tj_cap=1024)
    merged = row("merge", lambda i, a, b, ga, gb_: _merge(a, b, ga, gb_),
                 [(br_a, D, 0), (br_b, D, 0), (p, D, OFF_GA // D), (p, D, OFF_GB // D)], [], [("new", D, BF16)])[0]
    h2 = matmul("mix_out", merged, w["w_out"], "nn", resid=h1, ti_cap=1376, tj_cap=1024)
    h3, ffn2_saved = ffn_fwd("ffn2", h2, small["ffn2_norm"], w["ffn2_w_in"], w["ffn2_w_out"])

    def head(i, h, t, wn):
        mask = (_row_ids(i, tm) >= HEAD_ROWS).astype(F32)
        y, vjp = jax.vjp(_rms, h, wn)
        err = (y - t) * mask
        dh, dw = vjp(err * (1.0 / D))
        return dh, dw, jnp.sum(err * err, keepdims=True).reshape(1, 1) * (0.5 / D) * jnp.ones((1, LANES), F32)
    dh3, d_final, loss_row = row("loss_head", head, [(h3, D, 0), (tgt_p, D, 0)], [small["final_norm"]], [("new", D, F32)],
                                 [(1, D), (1, LANES)])

    gw, gs = {}, {"final_norm": d_final}
    dh2, gw["ffn2_w_in"], gw["ffn2_w_out"], gs["ffn2_norm"] = ffn_bwd("ffn2", dh3, ffn2_saved, small["ffn2_norm"],
                                                                      w["ffn2_w_in"], w["ffn2_w_in_full"], w["ffn2_w_out"])
    dmerged = matmul("mix_out_dx", dh2, w["w_out"], "nt", ti_cap=1376, tj_cap=1024)
    gw["w_out"] = matmul("mix_out_dw", merged, dh2, "tn", ti_cap=1024, tj_cap=1024, tr_cap=2752)
    dp = lax.empty((lp, PW), BF16)

    def merge_bwd(i, dm, a, b, ga, gb_):
        _, vjp = jax.vjp(_merge, a, b, ga, gb_)
        da, db, dga, dgb = vjp(dm)
        return da, db, jnp.concatenate([dga, dgb], axis=1)
    da, db_, dp = row("merge_bwd", merge_bwd,
                      [(dmerged, D, 0), (br_a, D, 0), (br_b, D, 0), (p, D, OFF_GA // D), (p, D, OFF_GB // D)], [],
                      [("new", D, BF16), ("new", D, BF16), ("into", dp, 2 * D, OFF_GA // (2 * D))])
    dy_a = matmul("branch_gdn_dx", da, w["w_branch_gdn"], "nt", ti_cap=1376, tj_cap=1024)
    gw["w_branch_gdn"] = matmul("branch_gdn_dw", y_a, da, "tn", ti_cap=1024, tj_cap=1024, tr_cap=2752)
    dy_b = matmul("branch_ret_dx", db_, w["w_branch_ret"], "nt", ti_cap=1376, tj_cap=1024)
    gw["w_branch_ret"] = matmul("branch_ret_dw", y_b, db_, "tn", ti_cap=1024, tj_cap=1024, tr_cap=2752)

    def gdn_post_bwd(i, o, z, dy, wn):
        _, vjp = jax.vjp(_gdn_post, o, z, wn)
        return vjp(dy)
    do_a, dp, gs["gdn_out_norm"] = row("gdn_post_bwd", gdn_post_bwd, [(o_a, D, 0), (p, D, OFF_Z // D), (dy_a, D, 0)],
                                       [small["gdn_out_norm"]], [("new", D, F32), ("into", dp, D, OFF_Z // D)], [(1, DH)])

    def ret_post_bwd(i, o, g, dy, wn):
        _, vjp = jax.vjp(_ret_post, o, g, wn)
        return vjp(dy)
    do_b, dp, gs["ret_out_norm"] = row("ret_post_bwd", ret_post_bwd, [(o_b, D, 0), (p, D, OFF_RG // D), (dy_b, D, 0)],
                                       [small["ret_out_norm"]], [("new", D, F32), ("into", dp, D, OFF_RG // D)], [(1, D)])

    (dp,) = _scan_bwd("ret_scan_bwd", _ret_chunk, ret_blocks, [], ret_consts, ret_states, do_b, (dp, OFF_RQ // (3 * D)), nc,
                      shared=[ct, st])

    dq, dk, dv, dgcb, dbb, dgc_rows = _scan_bwd("gdn_scan_bwd", _gdn_chunk, gdn_blocks, [gc_rows], [], gdn_states, do_a,
                                                None, nc)
    dgc = jnp.pad(dgc_rows.transpose(0, 2, 1).reshape(lp, H), ((0, 0), (H, LANES - 2 * H)))
    dp, dba, dcw, dgate = gdn_pre_bwd(p, cws, alog_row, dtb_row, dq, dk, dv, dgcb, dbb, dgc, dp, lp, tm)
    dp = row("dp_ba", lambda i, t: t, [(dba, LANES, 0)], [], [("into", dp, LANES, OFF_BA // LANES)])[0]
    gw["gdn_conv_w"] = dcw[:CONV_K]
    gs["gdn_a_log"] = dgate[0:1, H:2 * H]
    gs["gdn_dt_bias"] = dgate[1:2, H:2 * H]

    dn2 = matmul("mix_proj_dx", dp, w["w_in_p"], "nt", tj_cap=1024, tr_cap=3456)
    gw["w_in_p"] = matmul("mix_proj_dw", n2, dp, "tn", ti_cap=1024, tj_cap=1152, tr_cap=2752)
    dh1, gs["mix_norm"] = norm_bwd("mix_dnorm", h1, small["mix_norm"], dn2, dh2)
    dh0, gw["ffn1_w_in"], gw["ffn1_w_out"], gs["ffn1_norm"] = ffn_bwd("ffn1", dh1, ffn1_saved, small["ffn1_norm"],
                                                                      w["ffn1_w_in"], w["ffn1_w_in_full"], w["ffn1_w_out"])
    gw["meta_tokens"] = dh0[PAD:HEAD_ROWS]
    return loss_row, dh0[HEAD_ROWS:], gw, gs


BIG = ("ffn1_w_in", "ffn1_w_out", "w_in", "w_branch_gdn", "w_branch_ret", "w_out", "ffn2_w_in", "ffn2_w_out")
COL_SHARDED = ("ffn1_w_in", "w_in", "ffn2_w_in")
SMALL = ("ffn1_norm", "mix_norm", "ret_out_norm", "ffn2_norm", "final_norm", "gdn_out_norm", "gdn_a_log", "gdn_dt_bias")
WEIGHTS = ("meta_tokens", "ffn1_norm", "ffn1_w_in", "ffn1_w_out", "mix_norm", "w_in", "gdn_conv_w", "gdn_a_log", "gdn_dt_bias",
           "gdn_out_norm", "ret_out_norm", "w_branch_gdn", "w_branch_ret", "w_out", "ffn2_norm", "ffn2_w_in", "ffn2_w_out",
           "final_norm")
LOSS_ROW = 6
CONV_ROW0, META_ROW0, SMALL_ROWS = 8, 24, 40


def pack_small(vals):
    rows = [vals[n].reshape(1, D) for n in SMALL[:5]]
    r5 = jnp.concatenate([vals["gdn_out_norm"].reshape(1, DH), vals["gdn_a_log"].reshape(1, H), vals["gdn_dt_bias"].reshape(1, H),
                          jnp.zeros((1, D - DH - 2 * H), F32)], axis=1)
    return jnp.concatenate(rows + [r5, jnp.zeros((2, D), F32)], axis=0)


def unpack_small(packed, shapes):
    out = {n: packed[j].reshape(shapes[n]) for j, n in enumerate(SMALL[:5])}
    out["gdn_out_norm"] = packed[5, :DH].reshape(shapes["gdn_out_norm"])
    out["gdn_a_log"] = packed[5, DH:DH + H].reshape(shapes["gdn_a_log"])
    out["gdn_dt_bias"] = packed[5, DH + H:DH + 2 * H].reshape(shapes["gdn_dt_bias"])
    return out


def kernel(x, meta_tokens, ffn1_norm, ffn1_w_in, ffn1_w_out, mix_norm, w_in, gdn_conv_w, gdn_a_log, gdn_dt_bias, gdn_out_norm, ret_out_norm, w_branch_gdn, w_branch_ret, w_out, ffn2_norm, ffn2_w_in, ffn2_w_out, final_norm, loss_target, m_meta_tokens, m_ffn1_norm, m_ffn1_w_in, m_ffn1_w_out, m_mix_norm, m_w_in, m_gdn_conv_w, m_gdn_a_log, m_gdn_dt_bias, m_gdn_out_norm, m_ret_out_norm, m_w_branch_gdn, m_w_branch_ret, m_w_out, m_ffn2_norm, m_ffn2_w_in, m_ffn2_w_out, m_final_norm, v_meta_tokens, v_ffn1_norm, v_ffn1_w_in, v_ffn1_w_out, v_mix_norm, v_w_in, v_gdn_conv_w, v_gdn_a_log, v_gdn_dt_bias, v_gdn_out_norm, v_ret_out_norm, v_w_branch_gdn, v_w_branch_ret, v_w_out, v_ffn2_norm, v_ffn2_w_in, v_ffn2_w_out, v_final_norm):
    a = dict(locals())
    wts = {n: a[n] for n in WEIGHTS}
    mom_m = {n: a["m_" + n] for n in WEIGHTS}
    mom_v = {n: a["v_" + n] for n in WEIGHTS}
    shapes = {n: wts[n].shape for n in WEIGHTS}
    flat = lambda t: t.reshape(t.shape[-2:])
    c = lax.axis_index("c")
    chip = 2 * lax.axis_index("x") + lax.axis_index("y")

    exact = jnp.zeros((16, D), F32).at[0:3].set(wts["gdn_conv_w"].reshape(3, D)).at[3:7].set(wts["meta_tokens"].reshape(4, D))
    gathered = allgather_chips_two_level("gather_weights", [flat(wts[n]).astype(BF16) for n in BIG] + [exact])
    gat = dict(zip(BIG, gathered[:-1]))
    exact = gathered[-1]
    w = {n: gat[n].reshape(N_CHIPS * gat[n].shape[1], D) for n in BIG if n not in COL_SHARDED}
    w["ffn1_w_in"], w["ffn2_w_in"] = gat["ffn1_w_in"], gat["ffn2_w_in"]
    for n in ("ffn1_w_in", "ffn2_w_in"):
        w[n + "_full"] = jnp.concatenate([gat[n][q] for q in range(N_CHIPS)], axis=1)
    w["w_in_p"] = _w_in_padded(jnp.concatenate([gat["w_in"][q] for q in range(N_CHIPS)], axis=1))
    w["gdn_conv_w"] = jnp.concatenate([exact[q, 0:3].reshape(CONV_K, 3 * D // 4) for q in range(N_CHIPS)], axis=1)
    w["meta_tokens"] = jnp.concatenate([exact[q, 3:7].reshape(N_META, D // 4) for q in range(N_CHIPS)], axis=1)
    small = {n: wts[n].reshape(1, -1) for n in SMALL}

    loss_row, gx, gw, gs = local_step(x[0], loss_target[0], w, small)

    gblk = {n: gw[n].reshape(N_CHIPS, gw[n].shape[0] // N_CHIPS, D) for n in BIG if n not in COL_SHARDED}
    gblk["ffn1_w_in"], gblk["ffn2_w_in"] = gw["ffn1_w_in"], gw["ffn2_w_in"]
    gblk["w_in"] = jnp.stack(jnp.split(_w_in_unpadded(gw["w_in_p"]), N_CHIPS, axis=1))
    gl = [gblk[n] for n in BIG]
    from_sib = sibling_halves("grads_sibling", gl)
    chip_sum = [chip_sums(f"grads_chip_sum_{n}", g, r, c) for n, g, r in zip(BIG, gl, from_sib)]
    from_chips = scatter_chips("grads_scatter", chip_sum)
    halves = [sum_slots(f"grads_sum_{n}", [(cs, chip), (fc, 0), (fc, 1), (fc, 2)]) for n, cs, fc in zip(BIG, chip_sum, from_chips)]
    others = sibling_swap("grads_swap", halves)
    grads, delta, new_m, new_v = {}, {}, {}, {}
    for n, mine, other in zip(BIG, halves, others):
        res = adamw_halves(f"adamw_{n}", flat(wts[n]), mine, other, flat(mom_m[n]), flat(mom_v[n]), c)
        grads[n], delta[n], new_m[n], new_v[n] = (t.reshape(shapes[n]) for t in res)

    sm = jnp.concatenate([pack_small(gs).at[LOSS_ROW, :LANES].set(loss_row[0]),
                          gw["gdn_conv_w"].reshape(3 * CONV_K, D), jnp.zeros((META_ROW0 - CONV_ROW0 - 3 * CONV_K, D), F32),
                          gw["meta_tokens"]], axis=0)
    every = allgather_all("small_gather", sm)
    sm_sum = sum_slots("small_sum", [(every, s) for s in range(8)])
    d_s, m_s, v_s = adamw("adamw_small", pack_small(small), sm_sum[:8], pack_small({n: mom_m[n].reshape(1, -1) for n in SMALL}),
                          pack_small({n: mom_v[n].reshape(1, -1) for n in SMALL}))
    grads.update(unpack_small(sm_sum, shapes))
    delta.update(unpack_small(d_s, shapes))
    new_m.update(unpack_small(m_s, shapes))
    new_v.update(unpack_small(v_s, shapes))
    g_conv = lax.dynamic_slice_in_dim(sm_sum[CONV_ROW0:CONV_ROW0 + 3 * CONV_K].reshape(CONV_K, 3 * D), chip * (3 * D // 4), 3 * D // 4, 1)
    g_meta = lax.dynamic_slice_in_dim(sm_sum[META_ROW0:META_ROW0 + N_META], chip * (D // 4), D // 4, 1)
    for n, g in (("gdn_conv_w", g_conv), ("meta_tokens", g_meta)):
        d_, m_, v_ = adamw(f"adamw_{n}", flat(wts[n]), g, flat(mom_m[n]), flat(mom_v[n]))
        grads[n], delta[n], new_m[n], new_v[n] = (t.reshape(shapes[n]) for t in (g, d_, m_, v_))
    loss = sm_sum[LOSS_ROW, 0]

    return (loss, gx[None], *[grads[n] for n in WEIGHTS], *[delta[n] for n in WEIGHTS], *[new_m[n] for n in WEIGHTS],
            *[new_v[n] for n in WEIGHTS])
```

```python
import collections
import functools

import jax
import jax.numpy as jnp
from jax import lax
from jax.experimental import pallas as pl
from jax.experimental.pallas import tpu as pltpu

F32 = jnp.float32
BF16 = jnp.bfloat16
HI = lax.Precision.HIGHEST
MESH = pl.DeviceIdType.MESH

D = 1024
N_META = 16
PAD = 48
HEAD_ROWS = PAD + N_META
CH = 64
H = 8
DH = 128
DFF = 2816
CONV_K = 4
EPS = 1e-6
ROPE_BASE = 10000.0
LANES = 128
N_CHIPS = 4
VMEM_LIMIT = 56 * 2 ** 20

OFF_QKV, OFF_RQ, OFF_RK, OFF_RV, OFF_Z, OFF_RG, OFF_GA, OFF_GB, OFF_BA = 0, 3072, 4096, 5120, 6144, 7168, 8192, 9216, 10240
PW = 10368
D_PROJ = 10256

ADAM_LR, ADAM_B1, ADAM_B2, ADAM_EPS, ADAM_WD, ADAM_STEP = 0.001, 0.9, 0.999, 1e-08, 0.01, 10


def _pick(n, cap, mult):
    best = None
    for t in range(mult, min(n, cap) + 1, mult):
        if n % t == 0:
            best = t
    return best if best is not None else n


def _tile_rows(rows, cols, block_bytes=3 * 2 ** 19):
    return _pick(rows, max(8, block_bytes // (4 * cols)), 8)


def _dot(a, b, dims, prec=None):
    return lax.dot_general(a, b, (dims, ((), ())), precision=prec, preferred_element_type=F32)


def _mmh(a, b):
    return _dot(a, b, ((1,), (0,)), HI)


def _split(a):
    hi = a.astype(BF16)
    return hi, (a - hi.astype(F32)).astype(BF16)


_NN, _NT, _TN = ((2,), (1,)), ((2,), (2,)), ((1,), (1,))


def _bdot(a, b, dims):
    return lax.dot_general(a, b, (dims, ((0,), (0,))), preferred_element_type=F32)


def _bmm(a, b, dims):
    return _bdot(a.astype(BF16), b.astype(BF16), dims)


def _bdot3(a, b, dims):
    ah, al = _split(a)
    bh, bl = _split(b)
    if dims == _NN:
        m = a.shape[1]
        both = _bdot(jnp.concatenate([ah, al], axis=1), bh, dims)
        return both[:, :m] + (both[:, m:] + _bdot(ah, bl, dims))
    return _bdot(ah, bh, dims) + (_bdot(ah, bl, dims) + _bdot(al, bh, dims))


def matmul(name, a, b, mode, *, out_dtype=F32, resid=None, alpha=1.0, ti_cap=688, tj_cap=512, tr_cap=1408,
           b_split=False, o_split=False, side=None):
    if mode == "nn":
        I, R = a.shape
        J = N_CHIPS * b.shape[2] if b_split else b.shape[1]
    elif mode == "nt":
        I, R = a.shape
        J = b.shape[1] if b_split else b.shape[0]
    else:
        R, I = a.shape
        J = b.shape[1]
    ti = _pick(I, ti_cap, 16) if mode != "tn" else _pick(I, ti_cap, LANES)
    tj = _pick(J, tj_cap, LANES)
    tr = _pick(R, tr_cap, LANES) if mode != "tn" else _pick(R, tr_cap, 16)
    if mode == "nn":
        a_spec = pl.BlockSpec((ti, tr), lambda i, j, r: (i, r))
        if b_split:
            tj = J // N_CHIPS
            b_spec = pl.BlockSpec((None, tr, tj), lambda i, j, r: (j, r, 0))
        else:
            b_spec = pl.BlockSpec((tr, tj), lambda i, j, r: (r, j))
        dims = ((1,), (0,))
    elif mode == "nt":
        if b_split:
            tr = R // N_CHIPS
            b_spec = pl.BlockSpec((None, tj, tr), lambda i, j, r: (r, j, 0))
        else:
            b_spec = pl.BlockSpec((tj, tr), lambda i, j, r: (j, r))
        a_spec = pl.BlockSpec((ti, tr), lambda i, j, r: (i, r))
        dims = ((1,), (1,))
    else:
        if o_split:
            tj = J // N_CHIPS
        a_spec = pl.BlockSpec((tr, ti), lambda i, j, r: (r, i))
        b_spec = pl.BlockSpec((tr, tj), lambda i, j, r: (r, j))
        dims = ((0,), (0,))
    nr = R // tr
    assert I % ti == 0 and J % tj == 0 and R % tr == 0, (name, I, J, R, ti, tj, tr)
    if o_split:
        o_spec = pl.BlockSpec((None, ti, tj), lambda i, j, r: (j, i, 0))
        out_shape = jax.ShapeDtypeStruct((N_CHIPS, I, tj), out_dtype)
    else:
        o_spec = pl.BlockSpec((ti, tj), lambda i, j, r: (i, j))
        out_shape = jax.ShapeDtypeStruct((I, J), out_dtype)
    has_resid = resid is not None

    def finish(o_ref, r_ref, res):
        res = res * alpha if alpha != 1.0 else res
        if has_resid:
            res = r_ref[...] + res
        o_ref[...] = res.astype(o_ref.dtype)

    n_mm_in = 3 if has_resid else 2
    n_si = len(side.ins) if side else 0
    n_so = len(side.out_shapes) if side else 0
    grid = (I // ti, J // tj, nr)

    def body(*refs):
        a_ref, b_ref = refs[:2]
        r_ref = refs[2] if has_resid else None
        o_ref = refs[n_mm_in + n_si]
        acc = refs[n_mm_in + n_si + 1 + n_so] if nr > 1 else None
        step = (pl.program_id(0) * grid[1] + pl.program_id(1)) * grid[2] + pl.program_id(2)
        if side:
            s_refs = (refs[n_mm_in:n_mm_in + n_si], refs[n_mm_in + n_si + 1:n_mm_in + n_si + 1 + n_so],
                      refs[n_mm_in + n_si + 1 + n_so + (1 if nr > 1 else 0):])

            @pl.when(step == 0)
            def _():
                side.start(*s_refs)

        prod = _dot(a_ref[...].astype(BF16), b_ref[...].astype(BF16), dims)
        if nr == 1:
            finish(o_ref, r_ref, prod)
        else:
            r = pl.program_id(2)

            @pl.when(r == 0)
            def _():
                acc[...] = prod

            @pl.when(r > 0)
            def _():
                acc[...] += prod

            @pl.when(r == nr - 1)
            def _():
                finish(o_ref, r_ref, acc[...])

        if side:
            @pl.when(step == grid[0] * grid[1] * grid[2] - 1)
            def _():
                side.finish(*s_refs)

    ins = [a, b] + ([resid] if has_resid else [])
    specs = [a_spec, b_spec] + ([o_spec] if has_resid else [])
    scratch = [pltpu.VMEM((ti, tj), F32)] if nr > 1 else []
    if not side:
        return pl.pallas_call(
            body, name=name, grid=grid, in_specs=specs, out_specs=o_spec, out_shape=out_shape, scratch_shapes=scratch,
            compiler_params=pltpu.CompilerParams(dimension_semantics=("parallel", "parallel", "arbitrary"),
                                                 vmem_limit_bytes=VMEM_LIMIT),
        )(*ins)
    res = pl.pallas_call(
        body, name=name, grid=grid, in_specs=specs + _any_specs(n_si), out_specs=[o_spec] + _any_specs(n_so),
        out_shape=[out_shape] + list(side.out_shapes), scratch_shapes=scratch + list(side.scratch),
        compiler_params=pltpu.CompilerParams(dimension_semantics=("arbitrary", "arbitrary", "arbitrary"),
                                             vmem_limit_bytes=VMEM_LIMIT),
    )(*ins, *side.ins)
    return res[0], res[1:]


def rowwise(name, fn, rows, pars, outs, accs=(), *, n_rows, tm):
    nt = n_rows // tm
    assert nt * tm == n_rows
    in_specs, ins = [], []
    for arr, w, cb in rows:
        in_specs.append(pl.BlockSpec((tm, w), lambda i, cb=cb: (i, cb)))
        ins.append(arr)
    for p in pars:
        in_specs.append(pl.BlockSpec(p.shape, lambda i, nd=p.ndim: (0,) * nd))
        ins.append(p)
    n_in = len(ins)
    out_specs, out_shapes, aliases = [], [], {}
    for k, o in enumerate(outs):
        if o[0] == "new":
            _, w, dt = o
            out_specs.append(pl.BlockSpec((tm, w), lambda i: (i, 0)))
            out_shapes.append(jax.ShapeDtypeStruct((n_rows, w), dt))
        else:
            _, arr, w, cb = o
            in_specs.append(pl.BlockSpec(memory_space=pl.ANY))
            aliases[len(ins)] = k
            ins.append(arr)
            out_specs.append(pl.BlockSpec((tm, w), lambda i, cb=cb: (i, cb)))
            out_shapes.append(jax.ShapeDtypeStruct(arr.shape, arr.dtype))
    for r, w in accs:
        out_specs.append(pl.BlockSpec((r, w), lambda i: (0, 0)))
        out_shapes.append(jax.ShapeDtypeStruct((r, w), F32))
    n_all_in, n_out, n_acc = len(ins), len(outs), len(accs)

    def body(*refs):
        i = pl.program_id(0)
        vals = [r[...] for r in refs[:n_in]]
        res = fn(i, *vals)
        if not isinstance(res, (tuple, list)):
            res = (res,)
        o_refs = refs[n_all_in:n_all_in + n_out]
        a_refs = refs[n_all_in + n_out:]
        for r, v in zip(o_refs, res[:n_out]):
            r[...] = v.astype(r.dtype)
        if n_acc:
            @pl.when(i == 0)
            def _():
                for r in a_refs:
                    r[...] = jnp.zeros_like(r)
            for r, v in zip(a_refs, res[n_out:]):
                r[...] += v

    return pl.pallas_call(
        body, name=name, grid=(nt,), in_specs=in_specs, out_specs=out_specs, out_shape=out_shapes,
        input_output_aliases=aliases,
        compiler_params=pltpu.CompilerParams(dimension_semantics=("arbitrary",), vmem_limit_bytes=VMEM_LIMIT),
    )(*ins)


def _row_ids(i, tm):
    return i * tm + lax.broadcasted_iota(jnp.int32, (tm, 1), 0)


def _sigmoid(x):
    return 1.0 / (1.0 + jnp.exp(-x))


def _silu(x):
    return x * _sigmoid(x)


def _softplus(x):
    return jnp.maximum(x, 0.0) + jnp.log(1.0 + jnp.exp(-jnp.abs(x)))


def _rms(x, w):
    return x * lax.rsqrt(jnp.mean(x * x, axis=-1, keepdims=True) + EPS) * w


def _heads(fn, *xs):
    return jnp.concatenate([fn(h, *[x[:, h * DH:(h + 1) * DH] for x in xs]) for h in range(H)], axis=1)


def _swiglu(gu):
    return _silu(gu[:, :DFF]) * gu[:, DFF:]


def _gdn_post(o, z, w):
    return _heads(lambda h, oh, zh: oh * lax.rsqrt(jnp.mean(oh * oh, axis=-1, keepdims=True) + EPS) * w * _silu(zh), o, z)


def _ret_post(o, rg, w):
    def one(h, oh, gh, wh):
        mu = jnp.mean(oh, axis=-1, keepdims=True)
        xc = oh - mu
        var = jnp.mean(xc * xc, axis=-1, keepdims=True)
        return _silu(gh) * (xc * lax.rsqrt(var + EPS) * wh)
    return _heads(one, o, rg, jnp.broadcast_to(w, (o.shape[0], D)))


def _merge(a, b, ga, gb):
    return _sigmoid(ga) * a + _sigmoid(gb) * b


def _select_matrix(first_lane):
    r = lax.broadcasted_iota(jnp.int32, (LANES, H * DH), 0)
    c = lax.broadcasted_iota(jnp.int32, (LANES, H * DH), 1)
    return (r == first_lane + (c >> 7)).astype(F32)


def _chunk_tri(tm):
    r = lax.broadcasted_iota(jnp.int32, (tm, tm), 0)
    c = lax.broadcasted_iota(jnp.int32, (tm, tm), 1)
    return jnp.logical_and((r >> 6) == (c >> 6), r >= c).astype(F32)


def _gdn_gates(ba, alog_row, dtb_row, mask):
    g = -jnp.exp(alog_row) * _softplus(ba + dtb_row) * mask
    gc = _mmh(_chunk_tri(ba.shape[0]), g)
    beta = _sigmoid(ba) * mask
    return _mmh(gc, _select_matrix(H)), _mmh(beta, _select_matrix(0)), gc


def _gdn_qkv(c):
    a = _silu(c)

    def l2(scale):
        return lambda h, t: t * lax.rsqrt(jnp.sum(t * t, axis=-1, keepdims=True) + EPS) * scale
    q = _heads(l2(DH ** -0.5), a[:, :D])
    k = _heads(l2(1.0), a[:, D:2 * D])
    return q, k, a[:, 2 * D:]


def _conv_taps(xw, cws, tm):
    return sum(cws[i] * xw[5 + i:5 + i + tm] for i in range(CONV_K))


def _swap_pairs(t):
    n = t.shape[1]
    lane = lax.broadcasted_iota(jnp.int32, t.shape, 1)
    return jnp.where((lane & 1) == 0, pltpu.roll(t, n - 1, 1), pltpu.roll(t, 1, 1))


@jax.custom_vjp
def _unit_lower_inv(a):
    n = -a
    eye = (lax.broadcasted_iota(jnp.int32, a.shape, 1) == lax.broadcasted_iota(jnp.int32, a.shape, 2)).astype(F32)
    p = eye + n
    for _ in range(5):
        n = _bdot3(n, n, _NN)
        p = p + _bdot3(p, n, _NN)
    return p


def _inv_fwd(a):
    t = _unit_lower_inv(a)
    return t, t


def _inv_bwd(t, dt):
    x = _bdot3(t, dt, _TN)
    return (-_bdot3(x, t, _NT),)


_unit_lower_inv.defvjp(_inv_fwd, _inv_bwd)


def _gdn_chunk(q, k, v, gc, bb, gr, s):
    ri = lax.broadcasted_iota(jnp.int32, (H, CH, CH), 1)
    ci = lax.broadcasted_iota(jnp.int32, (H, CH, CH), 2)
    causal = ri >= ci
    gc1 = jnp.sum(gc, axis=2, keepdims=True) * (1.0 / LANES)
    diff = jnp.broadcast_to(gc1, (H, CH, CH)) - jnp.broadcast_to(gr, (H, CH, CH))
    decay = jnp.where(causal, jnp.exp(jnp.where(causal, diff, 0.0)), 0.0)
    kb = k * bb
    sc = _bmm(jnp.concatenate([q, kb], axis=1), k, _NT)
    qk = sc[:, :CH] * decay
    a = jnp.where(ri > ci, sc[:, CH:] * decay, 0.0)
    t = _unit_lower_inv(a)
    eg = jnp.exp(gc)
    uw = _bmm(t, jnp.concatenate([v * bb, kb * eg], axis=2), _NN)
    g_last = gc[:, CH - 1:CH, :]
    ws = _bmm(jnp.concatenate([uw[:, :, DH:], q * eg], axis=1), s, _NN)
    v_new = uw[:, :, :DH] - ws[:, :CH]
    o = ws[:, CH:] + _bmm(qk, v_new, _NN)
    s_new = s * jnp.exp(g_last) + _bmm(k * jnp.exp(g_last - gc), v_new, _TN)
    return o, s_new


def _swap_pairs_heads(t):
    return _swap_pairs(t.reshape(t.shape[0] * CH, DH)).reshape(t.shape)


@jax.custom_vjp
def _rope(t, ct, st):
    return t * ct + _swap_pairs_heads(t) * st


def _rope_fwd(t, ct, st):
    return _rope(t, ct, st), (ct, st)


def _rope_bwd(res, d):
    ct, st = res
    return d * ct + _swap_pairs_heads(d * st), jnp.zeros_like(ct), jnp.zeros_like(st)


_rope.defvjp(_rope_fwd, _rope_bwd)


def _ret_chunk(q, k, v, s, decay, xi, zeta, cd, ct, st):
    q = _rope(q, ct, st)
    k = _rope(k, ct, st) * (DH ** -0.5)
    scores = _bmm(q, k, _NT) * decay
    o = _bmm(scores, v, _NN) + _bmm(q * xi, s, _NN)
    s_new = s * cd + _bmm(k * zeta, v, _TN)
    return o, s_new


def _head_blocks(ref):
    return jnp.stack([ref[:, h * DH:(h + 1) * DH] for h in range(H)])


def _head_rows(ref):
    return jnp.stack([ref[0, h:h + 1, :] for h in range(H)])


def _scan_fwd(name, chunk_fn, blocks, rowvecs, consts, nc, shared=()):
    n_blk, n_rv, n_c = len(blocks), len(rowvecs), len(consts) + len(shared)

    def body(*refs):
        blk = refs[:n_blk]
        rvs = refs[n_blk:n_blk + n_rv]
        cst = refs[n_blk + n_rv:n_blk + n_rv + n_c]
        o_ref, st_ref, s_scr = refs[n_blk + n_rv + n_c:]

        @pl.when(pl.program_id(0) == 0)
        def _():
            s_scr[...] = jnp.zeros_like(s_scr)

        s = s_scr[...]
        st_ref[0] = s
        o, s_new = chunk_fn(*[_head_blocks(r) for r in blk], *[_head_rows(r) for r in rvs], s, *[r[...] for r in cst])
        for h in range(H):
            o_ref[:, h * DH:(h + 1) * DH] = o[h]
        s_scr[...] = s_new

    in_specs = [pl.BlockSpec((CH, H * DH), lambda c, f=f: (c, f)) for _, f in blocks]
    in_specs += [pl.BlockSpec((1, H, CH), lambda c: (c, 0, 0)) for _ in rowvecs]
    in_specs += [pl.BlockSpec(a.shape, lambda c: (0, 0, 0)) for a in consts]
    in_specs += [pl.BlockSpec((CH, a.shape[1]), lambda c: (c, 0)) for a in shared]
    return pl.pallas_call(
        body, name=name, grid=(nc,), in_specs=in_specs,
        out_specs=[pl.BlockSpec((CH, H * DH), lambda c: (c, 0)), pl.BlockSpec((1, H, DH, DH), lambda c: (c, 0, 0, 0))],
        out_shape=[jax.ShapeDtypeStruct((nc * CH, H * DH), F32), jax.ShapeDtypeStruct((nc, H, DH, DH), F32)],
        scratch_shapes=[pltpu.VMEM((H, DH, DH), F32)],
        compiler_params=pltpu.CompilerParams(dimension_semantics=("arbitrary",), vmem_limit_bytes=VMEM_LIMIT),
    )(*[a for a, _ in blocks], *rowvecs, *consts, *shared)


def _scan_bwd(name, chunk_fn, blocks, rowvecs, consts, states, do, into, nc, shared=()):
    n_blk, n_rv, n_c = len(blocks), len(rowvecs), len(consts) + len(shared)
    n_bo = 1 if into else n_blk

    def body(*refs):
        blk = refs[:n_blk]
        rvs = refs[n_blk:n_blk + n_rv]
        cst = refs[n_blk + n_rv:n_blk + n_rv + n_c]
        st_ref, do_ref = refs[n_blk + n_rv + n_c:n_blk + n_rv + n_c + 2]
        n_in = n_blk + n_rv + n_c + 2 + (1 if into else 0)
        o_refs = refs[n_in:n_in + n_bo]
        rv_refs = refs[n_in + n_bo:n_in + n_bo + n_rv]
        ds_scr = refs[n_in + n_bo + n_rv]

        @pl.when(pl.program_id(0) == 0)
        def _():
            ds_scr[...] = jnp.zeros_like(ds_scr)

        cv = [r[...] for r in cst]
        _, vjp = jax.vjp(lambda *a: chunk_fn(*a, *cv), *[_head_blocks(r) for r in blk], *[_head_rows(r) for r in rvs], st_ref[0])
        grads = vjp((_head_blocks(do_ref), ds_scr[...]))
        for h in range(H):
            for k, g in enumerate(grads[:n_blk]):
                r, col = (o_refs[0], k * H * DH) if into else (o_refs[k], 0)
                r[:, col + h * DH:col + (h + 1) * DH] = g[h].astype(r.dtype)
            for r, g in zip(rv_refs, grads[n_blk:n_blk + n_rv]):
                r[0, h:h + 1, :] = g[h]
        ds_scr[...] = grads[n_blk + n_rv]

    rc = lambda c: nc - 1 - c
    in_specs = [pl.BlockSpec((CH, H * DH), lambda c, f=f: (rc(c), f)) for _, f in blocks]
    in_specs += [pl.BlockSpec((1, H, CH), lambda c: (rc(c), 0, 0)) for _ in rowvecs]
    in_specs += [pl.BlockSpec(a.shape, lambda c: (0, 0, 0)) for a in consts]
    in_specs += [pl.BlockSpec((CH, a.shape[1]), lambda c: (rc(c), 0)) for a in shared]
    in_specs += [pl.BlockSpec((1, H, DH, DH), lambda c: (rc(c), 0, 0, 0)), pl.BlockSpec((CH, H * DH), lambda c: (rc(c), 0))]
    ins = [a for a, _ in blocks] + list(rowvecs) + list(consts) + list(shared) + [states, do]
    out_specs, out_shapes, aliases = [], [], {}
    if into:
        arr, f = into
        in_specs.append(pl.BlockSpec(memory_space=pl.ANY))
        aliases[len(ins)] = 0
        ins.append(arr)
        out_specs.append(pl.BlockSpec((CH, n_blk * H * DH), lambda c: (rc(c), f)))
        out_shapes.append(jax.ShapeDtypeStruct(arr.shape, arr.dtype))
    else:
        for _ in blocks:
            out_specs.append(pl.BlockSpec((CH, H * DH), lambda c: (rc(c), 0)))
            out_shapes.append(jax.ShapeDtypeStruct((nc * CH, H * DH), F32))
    for _ in rowvecs:
        out_specs.append(pl.BlockSpec((1, H, CH), lambda c: (rc(c), 0, 0)))
        out_shapes.append(jax.ShapeDtypeStruct((nc, H, CH), F32))
    return pl.pallas_call(
        body, name=name, grid=(nc,), in_specs=in_specs, out_specs=out_specs, out_shape=out_shapes,
        input_output_aliases=aliases, scratch_shapes=[pltpu.VMEM((H, DH, DH), F32)],
        compiler_params=pltpu.CompilerParams(dimension_semantics=("arbitrary",), vmem_limit_bytes=VMEM_LIMIT),
    )(*ins)


def _gdn_pre_specs(p, cws, alog_row, dtb_row, tm, pos):
    sub = tm // 8
    rows = [pl.BlockSpec((tm, 3 * D), lambda i: (pos(i), 0)),
            pl.BlockSpec((8, 3 * D), lambda i: (jnp.maximum(pos(i) * sub - 1, 0), 0)),
            pl.BlockSpec((tm, LANES), lambda i: (pos(i), OFF_BA // LANES))]
    pars = [pl.BlockSpec(a.shape, lambda i: (0, 0)) for a in (*cws, alog_row, dtb_row)]
    return rows + pars, [p, p, p, *cws, alog_row, dtb_row]


def gdn_pre_fwd(p, cws, alog_row, dtb_row, lp, tm):
    def body(x_ref, prev_ref, ba_ref, c0, c1, c2, c3, al_ref, dt_ref, q_ref, k_ref, v_ref, g_ref, b_ref, gc_ref):
        i = pl.program_id(0)
        prev = jnp.where(i > 0, prev_ref[...], 0.0)
        xw = jnp.concatenate([prev, x_ref[...]], axis=0)
        c = _conv_taps(xw, [c0[...], c1[...], c2[...], c3[...]], tm)
        q, k, v = _gdn_qkv(c)
        mask = (_row_ids(i, tm) >= PAD).astype(F32)
        g, b, gc = _gdn_gates(ba_ref[...], al_ref[...], dt_ref[...], mask)
        q_ref[...] = q
        k_ref[...] = k
        v_ref[...] = v
        g_ref[...] = g
        b_ref[...] = b
        gc_ref[...] = gc

    in_specs, ins = _gdn_pre_specs(p, cws, alog_row, dtb_row, tm, lambda i: i)
    o_spec = pl.BlockSpec((tm, D), lambda i: (i, 0))
    return pl.pallas_call(
        body, name="gdn_pre_fwd", grid=(lp // tm,), in_specs=in_specs,
        out_specs=[o_spec] * 5 + [pl.BlockSpec((tm, LANES), lambda i: (i, 0))],
        out_shape=[jax.ShapeDtypeStruct((lp, D), F32)] * 5 + [jax.ShapeDtypeStruct((lp, LANES), F32)],
        compiler_params=pltpu.CompilerParams(dimension_semantics=("parallel",), vmem_limit_bytes=VMEM_LIMIT),
    )(*ins)


def gdn_pre_bwd(p, cws, alog_row, dtb_row, dq, dk, dv, dg, db, dgc, dp, lp, tm):
    nt = lp // tm
    pos = lambda i: nt - 1 - i

    def body(x_ref, prev_ref, ba_ref, c0, c1, c2, c3, al_ref, dt_ref, dq_ref, dk_ref, dv_ref, dg_ref, db_ref, dgc_ref,
             dp_any, dx_ref, dba_ref, dcw_ref, dpar_ref, carry):
        i = pl.program_id(0)
        t = pos(i)

        @pl.when(i == 0)
        def _():
            carry[...] = jnp.zeros_like(carry)
            dcw_ref[...] = jnp.zeros_like(dcw_ref)
            dpar_ref[...] = jnp.zeros_like(dpar_ref)

        cws_v = [c0[...], c1[...], c2[...], c3[...]]
        prev = jnp.where(t > 0, prev_ref[...], 0.0)
        xw = jnp.concatenate([prev, x_ref[...]], axis=0)
        xs = [xw[5 + j:5 + j + tm] for j in range(CONV_K)]
        c = sum(cws_v[j] * xs[j] for j in range(CONV_K))
        _, vjp_qkv = jax.vjp(_gdn_qkv, c)
        (dc,) = vjp_qkv((dq_ref[...], dk_ref[...], dv_ref[...]))
        zeros8 = jnp.zeros((8, 3 * D), F32)
        dcp = jnp.concatenate([zeros8, dc, zeros8], axis=0)
        dxw = sum(cws_v[j] * dcp[3 - j:3 - j + tm + 8] for j in range(CONV_K))
        dx_ref[...] = jnp.concatenate([dxw[8:tm], dxw[tm:] + carry[...]], axis=0).astype(dx_ref.dtype)
        carry[...] = dxw[:8]
        for j in range(CONV_K):
            dcw_ref[j:j + 1, :] += jnp.sum(dc * xs[j], axis=0, keepdims=True)
        mask = (_row_ids(t, tm) >= PAD).astype(F32)
        _, vjp_g = jax.vjp(lambda ba, al, dt: _gdn_gates(ba, al, dt, mask), ba_ref[...], al_ref[...], dt_ref[...])
        dba, dal, ddt = vjp_g((dg_ref[...], db_ref[...], dgc_ref[...]))
        dba_ref[...] = dba
        dpar_ref[0:1, :] += dal
        dpar_ref[1:2, :] += ddt

    in_specs, ins = _gdn_pre_specs(p, cws, alog_row, dtb_row, tm, pos)
    g_spec = pl.BlockSpec((tm, D), lambda i: (pos(i), 0))
    s_spec = pl.BlockSpec((tm, LANES), lambda i: (pos(i), 0))
    in_specs += [g_spec] * 5 + [s_spec, pl.BlockSpec(memory_space=pl.ANY)]
    ins += [dq, dk, dv, dg, db, dgc, dp]
    return pl.pallas_call(
        body, name="gdn_pre_bwd", grid=(nt,), in_specs=in_specs,
        out_specs=[pl.BlockSpec((tm, 3 * D), lambda i: (pos(i), 0)), s_spec,
                   pl.BlockSpec((8, 3 * D), lambda i: (0, 0)), pl.BlockSpec((8, LANES), lambda i: (0, 0))],
        out_shape=[jax.ShapeDtypeStruct(dp.shape, dp.dtype), jax.ShapeDtypeStruct((lp, LANES), F32),
                   jax.ShapeDtypeStruct((8, 3 * D), F32), jax.ShapeDtypeStruct((8, LANES), F32)],
        input_output_aliases={len(ins) - 1: 0},
        scratch_shapes=[pltpu.VMEM((8, 3 * D), F32)],
        compiler_params=pltpu.CompilerParams(dimension_semantics=("arbitrary",), vmem_limit_bytes=VMEM_LIMIT),
    )(*ins)


def _me():
    return lax.axis_index("x"), lax.axis_index("y"), lax.axis_index("c")


def _any_specs(n):
    return [pl.BlockSpec(memory_space=pl.ANY)] * n


Exchange = collections.namedtuple("Exchange", "ins out_shapes scratch start finish")


def _dma_sems(*shape):
    return pltpu.SemaphoreType.DMA(shape)


def run_exchange(name, ex):
    n_i, n_o = len(ex.ins), len(ex.out_shapes)

    def body(*refs):
        parts = (refs[:n_i], refs[n_i:n_i + n_o], refs[n_i + n_o:])
        ex.start(*parts)
        ex.finish(*parts)

    return pl.pallas_call(body, name=name, out_shape=list(ex.out_shapes), in_specs=_any_specs(n_i), out_specs=_any_specs(n_o),
                          scratch_shapes=list(ex.scratch))(*ex.ins)


def gather_exchange(ws):
    n = len(ws)

    def copies(w_refs, o_refs, sems):
        send_sems, recv_sems, fsend_sems, frecv_sems, osend_sems, orecv_sems = sems
        x, y, c = _me()
        me = 2 * x + y
        chips = [(1 - x, y), (x, 1 - y), (1 - x, 1 - y)]

        def own(a):
            return pltpu.make_async_remote_copy(src_ref=w_refs[a], dst_ref=o_refs[a].at[me], send_sem=osend_sems.at[a],
                                                recv_sem=orecv_sems.at[a], device_id=(x, y, 1 - c), device_id_type=MESH)

        def rows(a, cc):
            rh = ws[a].shape[0] // 2
            return pl.ds(pl.multiple_of(cc * rh, 8), rh)

        def ici(a, j, slot):
            px, py = chips[j]
            return pltpu.make_async_remote_copy(
                src_ref=w_refs[a].at[rows(a, c)], dst_ref=o_refs[a].at[slot, rows(a, c)], send_sem=send_sems.at[a, j],
                recv_sem=recv_sems.at[a, j], device_id=(px, py, c), device_id_type=MESH)

        def d2d(a, j, cc):
            px, py = chips[j]
            blk = o_refs[a].at[2 * px + py, rows(a, cc)]
            return pltpu.make_async_remote_copy(src_ref=blk, dst_ref=blk, send_sem=fsend_sems.at[a, j],
                                                recv_sem=frecv_sems.at[a, j], device_id=(x, y, 1 - c), device_id_type=MESH)
        mine = [own(a) for a in range(n)]
        sends = [ici(a, j, me) for a in range(n) for j in range(3)]
        arrivals = [ici(a, j, 2 * px + py) for a in range(n) for j, (px, py) in enumerate(chips)]
        passes = [d2d(a, j, c) for a in range(n) for j in range(3)]
        passed_to_me = [d2d(a, j, 1 - c) for a in range(n) for j in range(3)]
        return mine, sends, arrivals, passes, passed_to_me

    def start(w_refs, o_refs, sems):
        mine, sends, _, _, _ = copies(w_refs, o_refs, sems)
        for cp in mine + sends:
            cp.start()

    def finish(w_refs, o_refs, sems):
        mine, sends, arrivals, passes, passed_to_me = copies(w_refs, o_refs, sems)
        for arrival, onward in zip(arrivals, passes):
            arrival.wait_recv()
            onward.start()
        for cp in passed_to_me + mine:
            cp.wait_recv()
        for cp in sends + passes + mine:
            cp.wait_send()

    return Exchange(list(ws), [jax.ShapeDtypeStruct((N_CHIPS,) + w.shape, w.dtype) for w in ws],
                    [_dma_sems(n, 3), _dma_sems(n, 3), _dma_sems(n, 3), _dma_sems(n, 3), _dma_sems(n), _dma_sems(n)], start, finish)


def _simple_exchange(ins, out_shapes, sem_shape, copies):
    def start(i_refs, o_refs, sems):
        for cp in copies(i_refs, o_refs, *sems):
            cp.start()

    def finish(i_refs, o_refs, sems):
        cps = copies(i_refs, o_refs, *sems)
        for cp in cps:
            cp.wait_recv()
        for cp in cps:
            cp.wait_send()

    return Exchange(list(ins), out_shapes, [_dma_sems(*sem_shape), _dma_sems(*sem_shape)], start, finish)


def sibling_halves_exchange(gs):
    def copies(g_refs, o_refs, send_sems, recv_sems):
        x, y, c = _me()
        cps = []
        for a in range(len(gs)):
            rh = gs[a].shape[1] // 2
            for q in range(N_CHIPS):
                cps.append(pltpu.make_async_remote_copy(
                    src_ref=g_refs[a].at[q, pl.ds(pl.multiple_of((1 - c) * rh, 8), rh)], dst_ref=o_refs[a].at[q],
                    send_sem=send_sems.at[a, q], recv_sem=recv_sems.at[a, q], device_id=(x, y, 1 - c), device_id_type=MESH))
        return cps

    return _simple_exchange(gs, [jax.ShapeDtypeStruct((N_CHIPS, g.shape[1] // 2, g.shape[2]), g.dtype) for g in gs],
                            (len(gs), N_CHIPS), copies)


def scatter_chips_exchange(css):
    def copies(c_refs, o_refs, send_sems, recv_sems):
        x, y, c = _me()
        chips = [(1 - x, y), (x, 1 - y), (1 - x, 1 - y)]
        return [pltpu.make_async_remote_copy(
            src_ref=c_refs[a].at[2 * px + py], dst_ref=o_refs[a].at[j], send_sem=send_sems.at[a, j],
            recv_sem=recv_sems.at[a, j], device_id=(px, py, c), device_id_type=MESH)
            for a in range(len(css)) for j, (px, py) in enumerate(chips)]

    return _simple_exchange(css, [jax.ShapeDtypeStruct((3,) + cs.shape[1:], cs.dtype) for cs in css], (len(css), 3), copies)


def sibling_swap(name, halves):
    n = len(halves)

    def body(*refs):
        h_refs, o_refs = refs[:n], refs[n:2 * n]
        send_sems, recv_sems = refs[2 * n:]
        x, y, c = _me()
        cps = [pltpu.make_async_remote_copy(src_ref=h_refs[a], dst_ref=o_refs[a], send_sem=send_sems.at[a],
                                            recv_sem=recv_sems.at[a], device_id=(x, y, 1 - c), device_id_type=MESH)
               for a in range(n)]
        for cp in cps:
            cp.start()
        for cp in cps:
            cp.wait_recv()
        for cp in cps:
            cp.wait_send()

    return pl.pallas_call(
        body, name=name, out_shape=[jax.ShapeDtypeStruct(h.shape, h.dtype) for h in halves],
        in_specs=_any_specs(n), out_specs=_any_specs(n),
        scratch_shapes=[pltpu.SemaphoreType.DMA((n,)), pltpu.SemaphoreType.DMA((n,))],
    )(*halves)


def allgather_all(name, s):
    def body(s_ref, out_ref, send_sems, recv_sems, local_sem):
        x, y, c = _me()
        peers = [(x ^ ((m >> 2) & 1), y ^ ((m >> 1) & 1), c ^ (m & 1)) for m in range(1, 8)]
        mine = pltpu.make_async_copy(s_ref, out_ref.at[4 * x + 2 * y + c], local_sem)
        mine.start()

        def copy(j, slot):
            return pltpu.make_async_remote_copy(src_ref=s_ref, dst_ref=out_ref.at[slot], send_sem=send_sems.at[j],
                                                recv_sem=recv_sems.at[j], device_id=peers[j], device_id_type=MESH)
        sends = [copy(j, 4 * x + 2 * y + c) for j in range(7)]
        for cp in sends:
            cp.start()
        for j, (px, py, pc) in enumerate(peers):
            copy(j, 4 * px + 2 * py + pc).wait_recv()
        for cp in sends:
            cp.wait_send()
        mine.wait()

    return pl.pallas_call(
        body, name=name, out_shape=jax.ShapeDtypeStruct((8,) + s.shape, s.dtype),
        in_specs=_any_specs(1), out_specs=pl.BlockSpec(memory_space=pl.ANY),
        scratch_shapes=[pltpu.SemaphoreType.DMA((7,)), pltpu.SemaphoreType.DMA((7,)), pltpu.SemaphoreType.DMA(())],
    )(s)


def sum_slots(name, parts):
    n = len(parts)
    R, W = parts[0][0].shape[1:]
    tm = _tile_rows(R, W)
    idx = jnp.stack([jnp.asarray(s, jnp.int32) for _, s in parts])

    def body(idx_ref, *refs):
        acc = refs[0][...].astype(F32)
        for r in refs[1:n]:
            acc = acc + r[...].astype(F32)
        refs[n][...] = acc

    grid_spec = pltpu.PrefetchScalarGridSpec(
        num_scalar_prefetch=1, grid=(R // tm,),
        in_specs=[pl.BlockSpec((None, tm, W), lambda i, idx, k=k: (idx[k], i, 0)) for k in range(n)],
        out_specs=pl.BlockSpec((tm, W), lambda i, idx: (i, 0)))
    return pl.pallas_call(body, name=name, grid_spec=grid_spec, out_shape=jax.ShapeDtypeStruct((R, W), F32),
                          compiler_params=pltpu.CompilerParams(dimension_semantics=("parallel",)))(idx, *[a for a, _ in parts])


def chip_sums(name, g, recv, c):
    _, R, W = g.shape
    rh = R // 2
    tm = _tile_rows(rh, W)
    g8 = g.reshape(2 * N_CHIPS, rh, W)
    idx = jnp.asarray(c, jnp.int32).reshape(1)

    def body(idx_ref, g_ref, r_ref, o_ref):
        o_ref[...] = (g_ref[...] + r_ref[...]).astype(o_ref.dtype)

    grid_spec = pltpu.PrefetchScalarGridSpec(
        num_scalar_prefetch=1, grid=(N_CHIPS, rh // tm),
        in_specs=[pl.BlockSpec((None, tm, W), lambda q, i, idx: (2 * q + idx[0], i, 0)),
                  pl.BlockSpec((None, tm, W), lambda q, i, idx: (q, i, 0))],
        out_specs=pl.BlockSpec((None, tm, W), lambda q, i, idx: (q, i, 0)))
    return pl.pallas_call(body, name=name, grid_spec=grid_spec, out_shape=jax.ShapeDtypeStruct((N_CHIPS, rh, W), BF16),
                          compiler_params=pltpu.CompilerParams(dimension_semantics=("parallel", "parallel")))(idx, g8, recv)


def _adamw_update(w, g, m, v):
    m = ADAM_B1 * m + (1.0 - ADAM_B1) * g
    v = ADAM_B2 * v + (1.0 - ADAM_B2) * (g * g)
    m_hat = m / (1.0 - ADAM_B1 ** ADAM_STEP)
    v_hat = v / (1.0 - ADAM_B2 ** ADAM_STEP)
    return -ADAM_LR * (m_hat / (jnp.sqrt(v_hat) + ADAM_EPS) + ADAM_WD * w), m, v


def adamw(name, w, g, m, v):
    R, W = w.shape
    return rowwise(name, lambda i, w, g, m, v: _adamw_update(w, g, m, v), [(a, W, 0) for a in (w, g, m, v)], [],
                   [("new", W, F32)] * 3, n_rows=R, tm=_tile_rows(R, W, 2 ** 20))


def adamw_halves(name, w, g_mine, g_other, m, v, c):
    R, W = w.shape
    rh = R // 2
    tm = _tile_rows(rh, W, 2 ** 20)
    nh = rh // tm
    idx = jnp.asarray(c, jnp.int32).reshape(1)

    def body(idx_ref, w_ref, ga_ref, gb_ref, m_ref, v_ref, g_ref, d_ref, mo_ref, vo_ref):
        mine = (pl.program_id(0) // nh) == idx_ref[0]
        g = jnp.where(mine, ga_ref[...], gb_ref[...])
        d, m, v = _adamw_update(w_ref[...], g, m_ref[...], v_ref[...])
        g_ref[...] = g
        d_ref[...] = d
        mo_ref[...] = m
        vo_ref[...] = v

    full = pl.BlockSpec((tm, W), lambda i, idx: (i, 0))
    half = pl.BlockSpec((tm, W), lambda i, idx: (i % nh, 0))
    grid_spec = pltpu.PrefetchScalarGridSpec(num_scalar_prefetch=1, grid=(R // tm,), in_specs=[full, half, half, full, full],
                                             out_specs=[full] * 4)
    return pl.pallas_call(body, name=name, grid_spec=grid_spec, out_shape=[jax.ShapeDtypeStruct((R, W), F32)] * 4,
                          compiler_params=pltpu.CompilerParams(dimension_semantics=("parallel",)))(idx, w, g_mine, g_other, m, v)


def _w_in_padded(w):
    return jnp.concatenate([w[:, :3072], w[:, 4112:7184], w[:, 3072:4096], w[:, 7184:], w[:, 4096:4112],
                            jnp.zeros((D, PW - D_PROJ), w.dtype)], axis=1)


def _w_in_unpadded(g):
    return jnp.concatenate([g[:, :3072], g[:, OFF_Z:OFF_Z + D], g[:, OFF_BA:OFF_BA + 16], g[:, OFF_RQ:OFF_Z], g[:, OFF_RG:OFF_BA]],
                           axis=1)


def _ret_consts():
    f = F32
    log_gamma = jnp.log1p(-jnp.exp2(-5.0 - jnp.arange(H, dtype=f)))
    pos = jnp.arange(CH, dtype=f)
    causal = jnp.tril(jnp.ones((CH, CH), dtype=bool))
    diff = pos[:, None] - pos[None, :]
    decay = jnp.where(causal, jnp.exp(jnp.where(causal, diff, 0.0) * log_gamma[:, None, None]), 0.0)
    xi = jnp.broadcast_to(jnp.exp((pos + 1.0) * log_gamma[:, None])[:, :, None], (H, CH, DH))
    zeta = jnp.broadcast_to(jnp.exp((CH - 1.0 - pos) * log_gamma[:, None])[:, :, None], (H, CH, DH))
    cd = jnp.broadcast_to(jnp.exp(CH * log_gamma)[:, None, None], (H, 1, DH))
    return decay, xi, zeta, cd


def _rope_tables(lp):
    pos = jnp.arange(lp, dtype=F32) - float(PAD)
    inv = 1.0 / (ROPE_BASE ** jnp.linspace(0.0, 1.0, DH // 2, dtype=F32))
    ang = pos[:, None] * inv[None, :]
    cos, sin = jnp.cos(ang), jnp.sin(ang)
    ct = jnp.repeat(cos, 2, axis=1)
    st = jnp.stack([-sin, sin], axis=-1).reshape(lp, DH)
    return ct, st


def local_step(x, tgt, w, small, hooks=None):
    hooks = hooks or {}
    seq = x.shape[0]
    lp = HEAD_ROWS + seq
    nc = lp // CH
    tm = _pick(lp, 192, CH)
    row = functools.partial(rowwise, n_rows=lp, tm=tm)
    gw = {}

    def mm(name, *args, **kw):
        if name not in hooks:
            return matmul(name, *args, **kw)
        make_exchange, take = hooks[name]
        out, results = matmul(name, *args, side=make_exchange(gw), **kw)
        take(results)
        return out

    h0 = jnp.concatenate([jnp.zeros((PAD, D), F32), w["meta_tokens"], x], axis=0)
    tgt_p = jnp.concatenate([jnp.zeros((HEAD_ROWS, D), F32), tgt], axis=0)

    def ffn_fwd(tag, h, wn):
        n = row(f"{tag}_norm", lambda i, h, wn: _rms(h, wn), [(h, D, 0)], [wn], [("new", D, BF16)])[0]
        gu = mm(f"{tag}_up", n, w[f"{tag}_w_in"], "nn", b_split=True, ti_cap=1376)
        mid = row(f"{tag}_act", lambda i, gu: _swiglu(gu), [(gu, 2 * DFF, 0)], [], [("new", DFF, BF16)])[0]
        out = mm(f"{tag}_down", mid, w[f"{tag}_w_out"], "nn", resid=h, alpha=0.5, tj_cap=1024, tr_cap=2816)
        return out, (h, n, gu, mid)

    def ffn_bwd(tag, dh, saved, wn):
        h, n, gu, mid = saved
        w_in_full, w_out = w[f"{tag}_w_in_full"], w[f"{tag}_w_out"]
        dmid = mm(f"{tag}_dmid", dh, w_out, "nt", alpha=0.5, ti_cap=1376, tj_cap=1408)
        dw_out = mm(f"{tag}_dwout", mid, dh, "tn", alpha=0.5, ti_cap=1408, tj_cap=1024, tr_cap=1376)

        def act_bwd(i, gu, dmid):
            _, vjp = jax.vjp(_swiglu, gu)
            return vjp(dmid)[0]
        dgu = row(f"{tag}_dact", act_bwd, [(gu, 2 * DFF, 0), (dmid, DFF, 0)], [], [("new", 2 * DFF, BF16)])[0]
        dn = mm(f"{tag}_dn", dgu, w_in_full, "nt", tj_cap=1024, tr_cap=2 * DFF)
        dw_in = mm(f"{tag}_dwin", n, dgu, "tn", ti_cap=1024, tr_cap=2752, o_split=True)
        dh_in, dwn = norm_bwd(f"{tag}_dnorm", h, wn, dn, dh)
        return dh_in, dw_in, dw_out, dwn

    def norm_bwd(name, h, wn, dn, dres):
        def fn(i, h, dn, dres, wn):
            _, vjp = jax.vjp(_rms, h, wn)
            dh, dw = vjp(dn)
            return dres + dh, dw
        return row(name, fn, [(h, D, 0), (dn, D, 0), (dres, D, 0)], [wn], [("new", D, F32)], [(1, D)])

    h1, ffn1_saved = ffn_fwd("ffn1", h0, small["ffn1_norm"])
    n2 = row("mix_norm", lambda i, h, wn: _rms(h, wn), [(h1, D, 0)], [small["mix_norm"]], [("new", D, BF16)])[0]
    p = mm("mix_proj", n2, w["w_in_p"], "nn", ti_cap=1376, tj_cap=1152)

    cws = [w["gdn_conv_w"][j:j + 1] for j in range(CONV_K)]
    alog_row = jnp.zeros((1, LANES), F32).at[:, H:2 * H].set(small["gdn_a_log"])
    dtb_row = jnp.zeros((1, LANES), F32).at[:, H:2 * H].set(small["gdn_dt_bias"])
    q, k, v, gcb, bb, gc = gdn_pre_fwd(p, cws, alog_row, dtb_row, lp, tm)
    gc_rows = gc[:, H:2 * H].reshape(nc, CH, H).transpose(0, 2, 1)
    gdn_blocks = [(q, 0), (k, 0), (v, 0), (gcb, 0), (bb, 0)]
    o_a, gdn_states = _scan_fwd("gdn_scan_fwd", _gdn_chunk, gdn_blocks, [gc_rows], [], nc)
    y_a = row("gdn_post", lambda i, o, z, wn: _gdn_post(o, z, wn), [(o_a, D, 0), (p, D, OFF_Z // D)], [small["gdn_out_norm"]],
              [("new", D, BF16)])[0]

    ct, st = _rope_tables(lp)

    ret_consts = list(_ret_consts())
    ret_blocks = [(p, OFF_RQ // D), (p, OFF_RK // D), (p, OFF_RV // D)]
    o_b, ret_states = _scan_fwd("ret_scan_fwd", _ret_chunk, ret_blocks, [], ret_consts, nc, shared=[ct, st])
    y_b = row("ret_post", lambda i, o, g, wn: _ret_post(o, g, wn), [(o_b, D, 0), (p, D, OFF_RG // D)], [small["ret_out_norm"]],
              [("new", D, BF16)])[0]

    br_a = matmul("branch_gdn", y_a, w["w_branch_gdn"], "nn", ti_cap=1376, tj_cap=1024)
    br_b = matmul("branch_ret", y_b, w["w_branch_ret"], "nn", ti_cap=1376, tj_cap=1024)
    merged = row("merge", lambda i, a, b, ga, gb_: _merge(a, b, ga, gb_),
                 [(br_a, D, 0), (br_b, D, 0), (p, D, OFF_GA // D), (p, D, OFF_GB // D)], [], [("new", D, BF16)])[0]
    h2 = matmul("mix_out", merged, w["w_out"], "nn", resid=h1, ti_cap=1376, tj_cap=1024)
    h3, ffn2_saved = ffn_fwd("ffn2", h2, small["ffn2_norm"])

    def head(i, h, t, wn):
        mask = (_row_ids(i, tm) >= HEAD_ROWS).astype(F32)
        y, vjp = jax.vjp(_rms, h, wn)
        err = (y - t) * mask
        dh, dw = vjp(err * (1.0 / D))
        return dh, dw, jnp.sum(err * err, keepdims=True).reshape(1, 1) * (0.5 / D) * jnp.ones((1, LANES), F32)
    dh3, d_final, loss_row = row("loss_head", head, [(h3, D, 0), (tgt_p, D, 0)], [small["final_norm"]], [("new", D, F32)],
                                 [(1, D), (1, LANES)])

    gs = {"final_norm": d_final}
    dh2, gw["ffn2_w_in"], gw["ffn2_w_out"], gs["ffn2_norm"] = ffn_bwd("ffn2", dh3, ffn2_saved, small["ffn2_norm"])
    dmerged = matmul("mix_out_dx", dh2, w["w_out"], "nt", ti_cap=1376, tj_cap=1024)
    gw["w_out"] = matmul("mix_out_dw", merged, dh2, "tn", ti_cap=1024, tj_cap=1024, tr_cap=2752)
    dp = lax.empty((lp, PW), BF16)

    def merge_bwd(i, dm, a, b, ga, gb_):
        _, vjp = jax.vjp(_merge, a, b, ga, gb_)
        da, db, dga, dgb = vjp(dm)
        return da, db, jnp.concatenate([dga, dgb], axis=1)
    da, db_, dp = row("merge_bwd", merge_bwd,
                      [(dmerged, D, 0), (br_a, D, 0), (br_b, D, 0), (p, D, OFF_GA // D), (p, D, OFF_GB // D)], [],
                      [("new", D, BF16), ("new", D, BF16), ("into", dp, 2 * D, OFF_GA // (2 * D))])
    dy_a = matmul("branch_gdn_dx", da, w["w_branch_gdn"], "nt", ti_cap=1376, tj_cap=1024)
    gw["w_branch_gdn"] = matmul("branch_gdn_dw", y_a, da, "tn", ti_cap=1024, tj_cap=1024, tr_cap=2752)
    dy_b = matmul("branch_ret_dx", db_, w["w_branch_ret"], "nt", ti_cap=1376, tj_cap=1024)
    gw["w_branch_ret"] = matmul("branch_ret_dw", y_b, db_, "tn", ti_cap=1024, tj_cap=1024, tr_cap=2752)

    def gdn_post_bwd(i, o, z, dy, wn):
        _, vjp = jax.vjp(_gdn_post, o, z, wn)
        return vjp(dy)
    do_a, dp, gs["gdn_out_norm"] = row("gdn_post_bwd", gdn_post_bwd, [(o_a, D, 0), (p, D, OFF_Z // D), (dy_a, D, 0)],
                                       [small["gdn_out_norm"]], [("new", D, F32), ("into", dp, D, OFF_Z // D)], [(1, DH)])

    def ret_post_bwd(i, o, g, dy, wn):
        _, vjp = jax.vjp(_ret_post, o, g, wn)
        return vjp(dy)
    do_b, dp, gs["ret_out_norm"] = row("ret_post_bwd", ret_post_bwd, [(o_b, D, 0), (p, D, OFF_RG // D), (dy_b, D, 0)],
                                       [small["ret_out_norm"]], [("new", D, F32), ("into", dp, D, OFF_RG // D)], [(1, D)])

    (dp,) = _scan_bwd("ret_scan_bwd", _ret_chunk, ret_blocks, [], ret_consts, ret_states, do_b, (dp, OFF_RQ // (3 * D)), nc,
                      shared=[ct, st])

    dq, dk, dv, dgcb, dbb, dgc_rows = _scan_bwd("gdn_scan_bwd", _gdn_chunk, gdn_blocks, [gc_rows], [], gdn_states, do_a,
                                                None, nc)
    dgc = jnp.pad(dgc_rows.transpose(0, 2, 1).reshape(lp, H), ((0, 0), (H, LANES - 2 * H)))
    dp, dba, dcw, dgate = gdn_pre_bwd(p, cws, alog_row, dtb_row, dq, dk, dv, dgcb, dbb, dgc, dp, lp, tm)
    dp = row("dp_ba", lambda i, t: t, [(dba, LANES, 0)], [], [("into", dp, LANES, OFF_BA // LANES)])[0]
    gw["gdn_conv_w"] = dcw[:CONV_K]
    gs["gdn_a_log"] = dgate[0:1, H:2 * H]
    gs["gdn_dt_bias"] = dgate[1:2, H:2 * H]

    dn2 = matmul("mix_proj_dx", dp, w["w_in_p"], "nt", tj_cap=1024, tr_cap=3456)
    gw["w_in_p"] = matmul("mix_proj_dw", n2, dp, "tn", ti_cap=1024, tj_cap=1152, tr_cap=2752)
    dh1, gs["mix_norm"] = norm_bwd("mix_dnorm", h1, small["mix_norm"], dn2, dh2)
    dh0, gw["ffn1_w_in"], gw["ffn1_w_out"], gs["ffn1_norm"] = ffn_bwd("ffn1", dh1, ffn1_saved, small["ffn1_norm"])
    gw["meta_tokens"] = dh0[PAD:HEAD_ROWS]
    return loss_row, dh0[HEAD_ROWS:], gw, gs


BIG = ("ffn1_w_in", "ffn1_w_out", "w_in", "w_branch_gdn", "w_branch_ret", "w_out", "ffn2_w_in", "ffn2_w_out")
COL_SHARDED = ("ffn1_w_in", "w_in", "ffn2_w_in")
WITH_FFN1_UP = ("ffn1_w_out", "w_in")
WITH_MIX_PROJ = ("w_branch_gdn", "w_branch_ret", "w_out", "ffn2_w_in", "ffn2_w_out")
EARLY_GRADS = ("ffn2_w_in", "ffn2_w_out", "w_in", "w_branch_gdn", "w_branch_ret", "w_out")
LATE_GRADS = ("ffn1_w_in", "ffn1_w_out")
SMALL = ("ffn1_norm", "mix_norm", "ret_out_norm", "ffn2_norm", "final_norm", "gdn_out_norm", "gdn_a_log", "gdn_dt_bias")
WEIGHTS = ("meta_tokens", "ffn1_norm", "ffn1_w_in", "ffn1_w_out", "mix_norm", "w_in", "gdn_conv_w", "gdn_a_log", "gdn_dt_bias",
           "gdn_out_norm", "ret_out_norm", "w_branch_gdn", "w_branch_ret", "w_out", "ffn2_norm", "ffn2_w_in", "ffn2_w_out",
           "final_norm")
LOSS_ROW = 6
CONV_ROW0, META_ROW0, SMALL_ROWS = 8, 24, 40


def pack_small(vals):
    rows = [vals[n].reshape(1, D) for n in SMALL[:5]]
    r5 = jnp.concatenate([vals["gdn_out_norm"].reshape(1, DH), vals["gdn_a_log"].reshape(1, H), vals["gdn_dt_bias"].reshape(1, H),
                          jnp.zeros((1, D - DH - 2 * H), F32)], axis=1)
    return jnp.concatenate(rows + [r5, jnp.zeros((2, D), F32)], axis=0)


def unpack_small(packed, shapes):
    out = {n: packed[j].reshape(shapes[n]) for j, n in enumerate(SMALL[:5])}
    out["gdn_out_norm"] = packed[5, :DH].reshape(shapes["gdn_out_norm"])
    out["gdn_a_log"] = packed[5, DH:DH + H].reshape(shapes["gdn_a_log"])
    out["gdn_dt_bias"] = packed[5, DH + H:DH + 2 * H].reshape(shapes["gdn_dt_bias"])
    return out


def kernel(x, meta_tokens, ffn1_norm, ffn1_w_in, ffn1_w_out, mix_norm, w_in, gdn_conv_w, gdn_a_log, gdn_dt_bias, gdn_out_norm, ret_out_norm, w_branch_gdn, w_branch_ret, w_out, ffn2_norm, ffn2_w_in, ffn2_w_out, final_norm, loss_target, m_meta_tokens, m_ffn1_norm, m_ffn1_w_in, m_ffn1_w_out, m_mix_norm, m_w_in, m_gdn_conv_w, m_gdn_a_log, m_gdn_dt_bias, m_gdn_out_norm, m_ret_out_norm, m_w_branch_gdn, m_w_branch_ret, m_w_out, m_ffn2_norm, m_ffn2_w_in, m_ffn2_w_out, m_final_norm, v_meta_tokens, v_ffn1_norm, v_ffn1_w_in, v_ffn1_w_out, v_mix_norm, v_w_in, v_gdn_conv_w, v_gdn_a_log, v_gdn_dt_bias, v_gdn_out_norm, v_ret_out_norm, v_w_branch_gdn, v_w_branch_ret, v_w_out, v_ffn2_norm, v_ffn2_w_in, v_ffn2_w_out, v_final_norm):
    a = dict(locals())
    wts = {n: a[n] for n in WEIGHTS}
    mom_m = {n: a["m_" + n] for n in WEIGHTS}
    mom_v = {n: a["v_" + n] for n in WEIGHTS}
    shapes = {n: wts[n].shape for n in WEIGHTS}
    flat = lambda t: t.reshape(t.shape[-2:])
    c = lax.axis_index("c")
    chip = 2 * lax.axis_index("x") + lax.axis_index("y")

    exact = jnp.zeros((16, D), F32).at[0:3].set(wts["gdn_conv_w"].reshape(3, D)).at[3:7].set(wts["meta_tokens"].reshape(4, D))
    bf16_block = lambda n: flat(wts[n]).astype(BF16)
    w = {}

    def take_weights(names):
        def take(gathered):
            for n, g in zip(names, gathered):
                if n in COL_SHARDED:
                    whole = jnp.concatenate([g[q] for q in range(N_CHIPS)], axis=1)
                    if n == "w_in":
                        w["w_in_p"] = _w_in_padded(whole)
                    else:
                        w[n], w[n + "_full"] = g, whole
                else:
                    w[n] = g.reshape(N_CHIPS * g.shape[1], D)
        return take

    first = run_exchange("gather_first", gather_exchange([bf16_block("ffn1_w_in"), exact]))
    take_weights(["ffn1_w_in"])(first[:1])
    exact = first[1]
    w["gdn_conv_w"] = jnp.concatenate([exact[q, 0:3].reshape(CONV_K, 3 * D // 4) for q in range(N_CHIPS)], axis=1)
    w["meta_tokens"] = jnp.concatenate([exact[q, 3:7].reshape(N_META, D // 4) for q in range(N_CHIPS)], axis=1)
    small = {n: wts[n].reshape(1, -1) for n in SMALL}
    hooks = {"ffn1_up": (lambda gw: gather_exchange([bf16_block(n) for n in WITH_FFN1_UP]), take_weights(WITH_FFN1_UP)),
             "mix_proj": (lambda gw: gather_exchange([bf16_block(n) for n in WITH_MIX_PROJ]), take_weights(WITH_MIX_PROJ))}

    def blocks(gw, n):
        if n == "w_in":
            return jnp.stack(jnp.split(_w_in_unpadded(gw["w_in_p"]), N_CHIPS, axis=1))
        return gw[n] if n in COL_SHARDED else gw[n].reshape(N_CHIPS, gw[n].shape[0] // N_CHIPS, D)

    chip_sum, from_chips = {}, {}

    def swap_early(gw):
        chip_sum["g"] = [blocks(gw, n) for n in EARLY_GRADS]
        return sibling_halves_exchange(chip_sum["g"])

    def sum_early(from_sib):
        for n, g, r in zip(EARLY_GRADS, chip_sum.pop("g"), from_sib):
            chip_sum[n] = chip_sums(f"grads_chip_sum_{n}", g, r, c)

    def scatter_of(names):
        return (lambda gw: scatter_chips_exchange([chip_sum[n] for n in names])), (lambda got: from_chips.update(zip(names, got)))
    hooks["ffn1_dmid"] = (swap_early, sum_early)
    hooks["ffn1_dn"] = scatter_of(["w_in"])
    hooks["ffn1_dwin"] = scatter_of([n for n in EARLY_GRADS if n != "w_in"])

    loss_row, gx, gw, gs = local_step(x[0], loss_target[0], w, small, hooks)

    late = [blocks(gw, n) for n in LATE_GRADS]
    from_sib = run_exchange("grads_sibling_late", sibling_halves_exchange(late))
    for n, g, r in zip(LATE_GRADS, late, from_sib):
        chip_sum[n] = chip_sums(f"grads_chip_sum_{n}", g, r, c)
    from_chips.update(zip(LATE_GRADS, run_exchange("grads_scatter_late", scatter_chips_exchange([chip_sum[n] for n in LATE_GRADS]))))
    halves = [sum_slots(f"grads_sum_{n}", [(chip_sum[n], chip), (from_chips[n], 0), (from_chips[n], 1), (from_chips[n], 2)])
              for n in BIG]
    others = sibling_swap("grads_swap", halves)
    grads, delta, new_m, new_v = {}, {}, {}, {}
    for n, mine, other in zip(BIG, halves, others):
        res = adamw_halves(f"adamw_{n}", flat(wts[n]), mine, other, flat(mom_m[n]), flat(mom_v[n]), c)
        grads[n], delta[n], new_m[n], new_v[n] = (t.reshape(shapes[n]) for t in res)

    sm = jnp.concatenate([pack_small(gs).at[LOSS_ROW, :LANES].set(loss_row[0]),
                          gw["gdn_conv_w"].reshape(3 * CONV_K, D), jnp.zeros((META_ROW0 - CONV_ROW0 - 3 * CONV_K, D), F32),
                          gw["meta_tokens"]], axis=0)
    every = allgather_all("small_gather", sm)
    sm_sum = sum_slots("small_sum", [(every, s) for s in range(8)])
    d_s, m_s, v_s = adamw("adamw_small", pack_small(small), sm_sum[:8], pack_small({n: mom_m[n].reshape(1, -1) for n in SMALL}),
                          pack_small({n: mom_v[n].reshape(1, -1) for n in SMALL}))
    grads.update(unpack_small(sm_sum, shapes))
    delta.update(unpack_small(d_s, shapes))
    new_m.update(unpack_small(m_s, shapes))
    new_v.update(unpack_small(v_s, shapes))
    g_conv = lax.dynamic_slice_in_dim(sm_sum[CONV_ROW0:CONV_ROW0 + 3 * CONV_K].reshape(CONV_K, 3 * D), chip * (3 * D // 4), 3 * D // 4, 1)
    g_meta = lax.dynamic_slice_in_dim(sm_sum[META_ROW0:META_ROW0 + N_META], chip * (D // 4), D // 4, 1)
    for n, g in (("gdn_conv_w", g_conv), ("meta_tokens", g_meta)):
        d_, m_, v_ = adamw(f"adamw_{n}", flat(wts[n]), g, flat(mom_m[n]), flat(mom_v[n]))
        grads[n], delta[n], new_m[n], new_v[n] = (t.reshape(shapes[n]) for t in (g, d_, m_, v_))
    loss = sm_sum[LOSS_ROW, 0]

    return (loss, gx[None], *[grads[n] for n in WEIGHTS], *[delta[n] for n in WEIGHTS], *[new_m[n] for n in WEIGHTS],
            *[new_v[n] for n in WEIGHTS])
```

```python
import collections
import functools

import jax
import jax.numpy as jnp
from jax import lax
from jax.experimental import pallas as pl
from jax.experimental.pallas import tpu as pltpu

F32 = jnp.float32
BF16 = jnp.bfloat16
HI = lax.Precision.HIGHEST
MESH = pl.DeviceIdType.MESH

D = 1024
N_META = 16
PAD = 48
HEAD_ROWS = PAD + N_META
CH = 64
H = 8
DH = 128
DFF = 2816
CONV_K = 4
EPS = 1e-6
ROPE_BASE = 10000.0
LANES = 128
N_CHIPS = 4
VMEM_LIMIT = 56 * 2 ** 20

OFF_QKV, OFF_Z, OFF_RQ, OFF_RK, OFF_RV, OFF_RG, OFF_GA, OFF_GB, OFF_BA = 0, 3072, 4096, 5120, 6144, 7168, 8192, 9216, 10240
PW = 10368
D_PROJ = 10256

ADAM_LR, ADAM_B1, ADAM_B2, ADAM_EPS, ADAM_WD, ADAM_STEP = 0.001, 0.9, 0.999, 1e-08, 0.01, 10


def _pick(n, cap, mult):
    best = None
    for t in range(mult, min(n, cap) + 1, mult):
        if n % t == 0:
            best = t
    return best if best is not None else n


def _tile_rows(rows, cols, block_bytes=3 * 2 ** 19):
    return _pick(rows, max(8, block_bytes // (4 * cols)), 8)


def _dot(a, b, dims, prec=None):
    return lax.dot_general(a, b, (dims, ((), ())), precision=prec, preferred_element_type=F32)


def _mmh(a, b):
    return _dot(a, b, ((1,), (0,)), HI)


def _split(a):
    hi = a.astype(BF16)
    return hi, (a - hi.astype(F32)).astype(BF16)


_NN, _NT, _TN = ((2,), (1,)), ((2,), (2,)), ((1,), (1,))


def _bdot(a, b, dims):
    return lax.dot_general(a, b, (dims, ((0,), (0,))), preferred_element_type=F32)


def _bmm(a, b, dims):
    return _bdot(a.astype(BF16), b.astype(BF16), dims)


def _bdot3(a, b, dims):
    ah, al = _split(a)
    bh, bl = _split(b)
    if dims == _NN:
        m = a.shape[1]
        both = _bdot(jnp.concatenate([ah, al], axis=1), bh, dims)
        return both[:, :m] + (both[:, m:] + _bdot(ah, bl, dims))
    return _bdot(ah, bh, dims) + (_bdot(ah, bl, dims) + _bdot(al, bh, dims))


def matmul(name, a, b, mode, *, out_dtype=F32, resid=None, alpha=1.0, ti_cap=688, tj_cap=512, tr_cap=1408,
           b_split=False, o_split=False, side=None, norm_w=None):
    if mode == "nn":
        I, R = a.shape
        J = N_CHIPS * b.shape[2] if b_split else b.shape[1]
    elif mode == "nt":
        I, R = a.shape
        J = b.shape[1] if b_split else b.shape[0]
    else:
        R, I = a.shape
        J = b.shape[1]
    ti = _pick(I, ti_cap, 16) if mode != "tn" else _pick(I, ti_cap, LANES)
    tj = _pick(J, tj_cap, LANES)
    tr = _pick(R, tr_cap, LANES) if mode != "tn" else _pick(R, tr_cap, 16)
    if mode == "nn":
        a_spec = pl.BlockSpec((ti, tr), lambda i, j, r: (i, r))
        if b_split:
            tj = J // N_CHIPS
            b_spec = pl.BlockSpec((None, tr, tj), lambda i, j, r: (j, r, 0))
        else:
            b_spec = pl.BlockSpec((tr, tj), lambda i, j, r: (r, j))
        dims = ((1,), (0,))
    elif mode == "nt":
        if b_split:
            tr = R // N_CHIPS
            b_spec = pl.BlockSpec((None, tj, tr), lambda i, j, r: (r, j, 0))
        else:
            b_spec = pl.BlockSpec((tj, tr), lambda i, j, r: (j, r))
        a_spec = pl.BlockSpec((ti, tr), lambda i, j, r: (i, r))
        dims = ((1,), (1,))
    else:
        if o_split:
            tj = J // N_CHIPS
        a_spec = pl.BlockSpec((tr, ti), lambda i, j, r: (r, i))
        b_spec = pl.BlockSpec((tr, tj), lambda i, j, r: (r, j))
        dims = ((0,), (0,))
    nr = R // tr
    assert I % ti == 0 and J % tj == 0 and R % tr == 0, (name, I, J, R, ti, tj, tr)
    if o_split:
        o_spec = pl.BlockSpec((None, ti, tj), lambda i, j, r: (j, i, 0))
        out_shape = jax.ShapeDtypeStruct((N_CHIPS, I, tj), out_dtype)
    else:
        o_spec = pl.BlockSpec((ti, tj), lambda i, j, r: (i, j))
        out_shape = jax.ShapeDtypeStruct((I, J), out_dtype)
    has_resid = resid is not None

    has_norm = norm_w is not None
    assert not (has_norm and (side or tj != J or o_split)), name

    def finish(refs, res):
        o_ref = refs[n_mm_in + n_si]
        res = res * alpha if alpha != 1.0 else res
        if has_resid:
            res = refs[2][...] + res
        o_ref[...] = res.astype(o_ref.dtype)
        if has_norm:
            refs[n_mm_in + 1][...] = _rms(res, refs[n_mm_in - 1][...]).astype(BF16)

    n_mm_in = 2 + int(has_resid) + int(has_norm)
    n_si = len(side.ins) if side else 0
    n_so = len(side.out_shapes) if side else 0
    grid = (I // ti, J // tj, nr)

    def body(*refs):
        a_ref, b_ref = refs[:2]
        r_ref = refs[2] if has_resid else None
        acc = refs[n_mm_in + n_si + 1 + n_so + int(has_norm)] if nr > 1 else None
        step = (pl.program_id(0) * grid[1] + pl.program_id(1)) * grid[2] + pl.program_id(2)
        if side:
            s_refs = (refs[n_mm_in:n_mm_in + n_si], refs[n_mm_in + n_si + 1:n_mm_in + n_si + 1 + n_so],
                      refs[n_mm_in + n_si + 1 + n_so + (1 if nr > 1 else 0):])

            @pl.when(step == 0)
            def _():
                side.start(*s_refs)

        prod = _dot(a_ref[...].astype(BF16), b_ref[...].astype(BF16), dims)
        if nr == 1:
            finish(refs, prod)
        else:
            r = pl.program_id(2)

            @pl.when(r == 0)
            def _():
                acc[...] = prod

            @pl.when(r > 0)
            def _():
                acc[...] += prod

            @pl.when(r == nr - 1)
            def _():
                finish(refs, acc[...])

        if side:
            @pl.when(step == grid[0] * grid[1] * grid[2] - 1)
            def _():
                side.finish(*s_refs)

    ins = [a, b] + ([resid] if has_resid else []) + ([norm_w] if has_norm else [])
    specs = [a_spec, b_spec] + ([o_spec] if has_resid else [])
    specs += [pl.BlockSpec(norm_w.shape, lambda i, j, r: (0, 0))] if has_norm else []
    scratch = [pltpu.VMEM((ti, tj), F32)] if nr > 1 else []
    if not side:
        return pl.pallas_call(
            body, name=name, grid=grid, in_specs=specs, out_specs=[o_spec, o_spec] if has_norm else o_spec,
            out_shape=[out_shape, jax.ShapeDtypeStruct((I, J), BF16)] if has_norm else out_shape, scratch_shapes=scratch,
            compiler_params=pltpu.CompilerParams(dimension_semantics=("parallel", "parallel", "arbitrary"),
                                                 vmem_limit_bytes=VMEM_LIMIT),
        )(*ins)
    res = pl.pallas_call(
        body, name=name, grid=grid, in_specs=specs + _any_specs(n_si), out_specs=[o_spec] + _any_specs(n_so),
        out_shape=[out_shape] + list(side.out_shapes), scratch_shapes=scratch + list(side.scratch),
        compiler_params=pltpu.CompilerParams(dimension_semantics=("arbitrary", "arbitrary", "arbitrary"),
                                             vmem_limit_bytes=VMEM_LIMIT),
    )(*ins, *side.ins)
    return res[0], res[1:]


def rowwise(name, fn, rows, pars, outs, accs=(), *, n_rows, tm):
    nt = n_rows // tm
    assert nt * tm == n_rows
    in_specs, ins = [], []
    for arr, w, cb in rows:
        in_specs.append(pl.BlockSpec((tm, w), lambda i, cb=cb: (i, cb)))
        ins.append(arr)
    for p in pars:
        in_specs.append(pl.BlockSpec(p.shape, lambda i, nd=p.ndim: (0,) * nd))
        ins.append(p)
    n_in = len(ins)
    out_specs, out_shapes, aliases = [], [], {}
    for k, o in enumerate(outs):
        if o[0] == "new":
            _, w, dt = o
            out_specs.append(pl.BlockSpec((tm, w), lambda i: (i, 0)))
            out_shapes.append(jax.ShapeDtypeStruct((n_rows, w), dt))
        else:
            _, arr, w, cb = o
            in_specs.append(pl.BlockSpec(memory_space=pl.ANY))
            aliases[len(ins)] = k
            ins.append(arr)
            out_specs.append(pl.BlockSpec((tm, w), lambda i, cb=cb: (i, cb)))
            out_shapes.append(jax.ShapeDtypeStruct(arr.shape, arr.dtype))
    for r, w in accs:
        out_specs.append(pl.BlockSpec((r, w), lambda i: (0, 0)))
        out_shapes.append(jax.ShapeDtypeStruct((r, w), F32))
    n_all_in, n_out, n_acc = len(ins), len(outs), len(accs)

    def body(*refs):
        i = pl.program_id(0)
        vals = [r[...] for r in refs[:n_in]]
        res = fn(i, *vals)
        if not isinstance(res, (tuple, list)):
            res = (res,)
        o_refs = refs[n_all_in:n_all_in + n_out]
        a_refs = refs[n_all_in + n_out:]
        for r, v in zip(o_refs, res[:n_out]):
            r[...] = v.astype(r.dtype)
        if n_acc:
            @pl.when(i == 0)
            def _():
                for r in a_refs:
                    r[...] = jnp.zeros_like(r)
            for r, v in zip(a_refs, res[n_out:]):
                r[...] += v

    return pl.pallas_call(
        body, name=name, grid=(nt,), in_specs=in_specs, out_specs=out_specs, out_shape=out_shapes,
        input_output_aliases=aliases,
        compiler_params=pltpu.CompilerParams(dimension_semantics=("arbitrary",), vmem_limit_bytes=VMEM_LIMIT),
    )(*ins)


def _row_ids(i, tm):
    return i * tm + lax.broadcasted_iota(jnp.int32, (tm, 1), 0)


def _sigmoid(x):
    return 1.0 / (1.0 + jnp.exp(-x))


def _silu(x):
    return x * _sigmoid(x)


def _softplus(x):
    return jnp.maximum(x, 0.0) + jnp.log(1.0 + jnp.exp(-jnp.abs(x)))


def _rms(x, w):
    return x * lax.rsqrt(jnp.mean(x * x, axis=-1, keepdims=True) + EPS) * w


def _heads(fn, *xs):
    return jnp.concatenate([fn(h, *[x[:, h * DH:(h + 1) * DH] for x in xs]) for h in range(H)], axis=1)


def _swiglu(gu):
    return _silu(gu[:, :DFF]) * gu[:, DFF:]


def _merge(a, b, ga, gb):
    return _sigmoid(ga) * a + _sigmoid(gb) * b


def _select_matrix(first_lane):
    r = lax.broadcasted_iota(jnp.int32, (LANES, H * DH), 0)
    c = lax.broadcasted_iota(jnp.int32, (LANES, H * DH), 1)
    return (r == first_lane + (c >> 7)).astype(F32)


def _chunk_tri(tm):
    r = lax.broadcasted_iota(jnp.int32, (tm, tm), 0)
    c = lax.broadcasted_iota(jnp.int32, (tm, tm), 1)
    return jnp.logical_and((r >> 6) == (c >> 6), r >= c).astype(F32)


def _gdn_gates(ba, alog_row, dtb_row, mask):
    g = -jnp.exp(alog_row) * _softplus(ba + dtb_row) * mask
    gc = _mmh(_chunk_tri(ba.shape[0]), g)
    beta = _sigmoid(ba) * mask
    return _mmh(gc, _select_matrix(H)), _mmh(beta, _select_matrix(0)), gc


def _gdn_qkv(c):
    a = _silu(c)

    def l2(scale):
        return lambda h, t: t * lax.rsqrt(jnp.sum(t * t, axis=-1, keepdims=True) + EPS) * scale
    q = _heads(l2(DH ** -0.5), a[:, :D])
    k = _heads(l2(1.0), a[:, D:2 * D])
    return q, k, a[:, 2 * D:]


def _conv_taps(xw, cws, tm):
    return sum(cws[i] * xw[5 + i:5 + i + tm] for i in range(CONV_K))


def _swap_pairs(t):
    n = t.shape[1]
    lane = lax.broadcasted_iota(jnp.int32, t.shape, 1)
    return jnp.where((lane & 1) == 0, pltpu.roll(t, n - 1, 1), pltpu.roll(t, 1, 1))


@jax.custom_vjp
def _unit_lower_inv(a):
    n = -a
    eye = (lax.broadcasted_iota(jnp.int32, a.shape, 1) == lax.broadcasted_iota(jnp.int32, a.shape, 2)).astype(F32)
    p = eye + n
    for _ in range(5):
        n = _bdot3(n, n, _NN)
        p = p + _bdot3(p, n, _NN)
    return p


def _inv_fwd(a):
    t = _unit_lower_inv(a)
    return t, t


def _inv_bwd(t, dt):
    x = _bdot3(t, dt, _TN)
    return (-_bdot3(x, t, _NT),)


_unit_lower_inv.defvjp(_inv_fwd, _inv_bwd)


def _gdn_chunk(q, k, v, gc, bb, z, gr, s, norm_w):
    ri = lax.broadcasted_iota(jnp.int32, (H, CH, CH), 1)
    ci = lax.broadcasted_iota(jnp.int32, (H, CH, CH), 2)
    causal = ri >= ci
    gc1 = jnp.sum(gc, axis=2, keepdims=True) * (1.0 / LANES)
    diff = jnp.broadcast_to(gc1, (H, CH, CH)) - jnp.broadcast_to(gr, (H, CH, CH))
    decay = jnp.where(causal, jnp.exp(jnp.where(causal, diff, 0.0)), 0.0)
    kb = k * bb
    sc = _bmm(jnp.concatenate([q, kb], axis=1), k, _NT)
    qk = sc[:, :CH] * decay
    a = jnp.where(ri > ci, sc[:, CH:] * decay, 0.0)
    t = _unit_lower_inv(a)
    eg = jnp.exp(gc)
    uw = _bmm(t, jnp.concatenate([v * bb, kb * eg], axis=2), _NN)
    g_last = gc[:, CH - 1:CH, :]
    ws = _bmm(jnp.concatenate([uw[:, :, DH:], q * eg], axis=1), s, _NN)
    v_new = uw[:, :, :DH] - ws[:, :CH]
    o = ws[:, CH:] + _bmm(qk, v_new, _NN)
    s_new = s * jnp.exp(g_last) + _bmm(k * jnp.exp(g_last - gc), v_new, _TN)
    y = o * lax.rsqrt(jnp.mean(o * o, axis=-1, keepdims=True) + EPS) * norm_w * _silu(z)
    return y, s_new


def _swap_pairs_heads(t):
    return _swap_pairs(t.reshape(t.shape[0] * CH, DH)).reshape(t.shape)


@jax.custom_vjp
def _rope(t, ct, st):
    return t * ct + _swap_pairs_heads(t) * st


def _rope_fwd(t, ct, st):
    return _rope(t, ct, st), (ct, st)


def _rope_bwd(res, d):
    ct, st = res
    return d * ct + _swap_pairs_heads(d * st), jnp.zeros_like(ct), jnp.zeros_like(st)


_rope.defvjp(_rope_fwd, _rope_bwd)


def _ret_chunk(q, k, v, rg, s, decay, xi, zeta, cd, ct, st, norm_w):
    q = _rope(q, ct, st)
    k = _rope(k, ct, st) * (DH ** -0.5)
    scores = _bmm(q, k, _NT) * decay
    o = _bmm(scores, v, _NN) + _bmm(q * xi, s, _NN)
    s_new = s * cd + _bmm(k * zeta, v, _TN)
    xc = o - jnp.mean(o, axis=-1, keepdims=True)
    var = jnp.mean(xc * xc, axis=-1, keepdims=True)
    w_heads = jnp.stack([norm_w[:, h * DH:(h + 1) * DH] for h in range(H)])
    y = _silu(rg) * (xc * lax.rsqrt(var + EPS) * w_heads)
    return y, s_new


def _head_blocks(ref):
    return jnp.stack([ref[:, h * DH:(h + 1) * DH] for h in range(H)])


def _head_rows(ref):
    return jnp.stack([ref[0, h:h + 1, :] for h in range(H)])


def _scan_fwd(name, chunk_fn, blocks, rowvecs, consts, nc, shared=(), params=(), out_dtype=F32):
    n_blk, n_rv, n_c = len(blocks), len(rowvecs), len(consts) + len(shared) + len(params)

    def body(*refs):
        blk = refs[:n_blk]
        rvs = refs[n_blk:n_blk + n_rv]
        cst = refs[n_blk + n_rv:n_blk + n_rv + n_c]
        o_ref, st_ref, s_scr = refs[n_blk + n_rv + n_c:]

        @pl.when(pl.program_id(0) == 0)
        def _():
            s_scr[...] = jnp.zeros_like(s_scr)

        s = s_scr[...]
        st_ref[0] = s
        o, s_new = chunk_fn(*[_head_blocks(r) for r in blk], *[_head_rows(r) for r in rvs], s, *[r[...] for r in cst])
        for h in range(H):
            o_ref[:, h * DH:(h + 1) * DH] = o[h].astype(o_ref.dtype)
        s_scr[...] = s_new

    in_specs = [pl.BlockSpec((CH, H * DH), lambda c, f=f: (c, f)) for _, f in blocks]
    in_specs += [pl.BlockSpec((1, H, CH), lambda c: (c, 0, 0)) for _ in rowvecs]
    in_specs += [pl.BlockSpec(a.shape, lambda c: (0, 0, 0)) for a in consts]
    in_specs += [pl.BlockSpec((CH, a.shape[1]), lambda c: (c, 0)) for a in shared]
    in_specs += [pl.BlockSpec(a.shape, lambda c: (0, 0)) for a in params]
    return pl.pallas_call(
        body, name=name, grid=(nc,), in_specs=in_specs,
        out_specs=[pl.BlockSpec((CH, H * DH), lambda c: (c, 0)), pl.BlockSpec((1, H, DH, DH), lambda c: (c, 0, 0, 0))],
        out_shape=[jax.ShapeDtypeStruct((nc * CH, H * DH), out_dtype), jax.ShapeDtypeStruct((nc, H, DH, DH), F32)],
        scratch_shapes=[pltpu.VMEM((H, DH, DH), F32)],
        compiler_params=pltpu.CompilerParams(dimension_semantics=("arbitrary",), vmem_limit_bytes=VMEM_LIMIT),
    )(*[a for a, _ in blocks], *rowvecs, *consts, *shared, *params)


def _scan_bwd(name, chunk_fn, blocks, rowvecs, consts, states, do, into, nc, shared=(), params=()):
    n_blk, n_rv, n_c, n_p = len(blocks), len(rowvecs), len(consts) + len(shared), len(params)
    packed = list(into[2]) if into else []
    fresh = [k for k in range(n_blk) if k not in packed]
    n_bo = len(fresh) + (1 if into else 0)

    def body(*refs):
        blk = refs[:n_blk]
        rvs = refs[n_blk:n_blk + n_rv]
        cst = refs[n_blk + n_rv:n_blk + n_rv + n_c]
        par = refs[n_blk + n_rv + n_c:n_blk + n_rv + n_c + n_p]
        st_ref, do_ref = refs[n_blk + n_rv + n_c + n_p:n_blk + n_rv + n_c + n_p + 2]
        n_in = n_blk + n_rv + n_c + n_p + 2 + (1 if into else 0)
        o_refs = refs[n_in:n_in + n_bo]
        rv_refs = refs[n_in + n_bo:n_in + n_bo + n_rv]
        p_refs = refs[n_in + n_bo + n_rv:n_in + n_bo + n_rv + n_p]
        ds_scr = refs[n_in + n_bo + n_rv + n_p]

        @pl.when(pl.program_id(0) == 0)
        def _():
            ds_scr[...] = jnp.zeros_like(ds_scr)
            for r in p_refs:
                r[...] = jnp.zeros_like(r)

        cv = [r[...] for r in cst]
        n_d = n_blk + n_rv + 1
        _, vjp = jax.vjp(lambda *a: chunk_fn(*a[:n_d], *cv, *a[n_d:]), *[_head_blocks(r) for r in blk],
                         *[_head_rows(r) for r in rvs], st_ref[0], *[r[...] for r in par])
        grads = vjp((_head_blocks(do_ref), ds_scr[...]))
        for h in range(H):
            for pos, k in enumerate(fresh):
                o_refs[pos][:, h * DH:(h + 1) * DH] = grads[k][h]
            for pos, k in enumerate(packed):
                col = pos * H * DH + h * DH
                o_refs[-1][:, col:col + DH] = grads[k][h].astype(o_refs[-1].dtype)
            for r, g in zip(rv_refs, grads[n_blk:n_blk + n_rv]):
                r[0, h:h + 1, :] = g[h]
        ds_scr[...] = grads[n_blk + n_rv]
        for r, g in zip(p_refs, grads[n_d:]):
            r[...] += g

    rc = lambda c: nc - 1 - c
    in_specs = [pl.BlockSpec((CH, H * DH), lambda c, f=f: (rc(c), f)) for _, f in blocks]
    in_specs += [pl.BlockSpec((1, H, CH), lambda c: (rc(c), 0, 0)) for _ in rowvecs]
    in_specs += [pl.BlockSpec(a.shape, lambda c: (0, 0, 0)) for a in consts]
    in_specs += [pl.BlockSpec((CH, a.shape[1]), lambda c: (rc(c), 0)) for a in shared]
    in_specs += [pl.BlockSpec(a.shape, lambda c: (0, 0)) for a in params]
    in_specs += [pl.BlockSpec((1, H, DH, DH), lambda c: (rc(c), 0, 0, 0)), pl.BlockSpec((CH, H * DH), lambda c: (rc(c), 0))]
    ins = [a for a, _ in blocks] + list(rowvecs) + list(consts) + list(shared) + list(params) + [states, do]
    out_specs, out_shapes, aliases = [], [], {}
    for _ in fresh:
        out_specs.append(pl.BlockSpec((CH, H * DH), lambda c: (rc(c), 0)))
        out_shapes.append(jax.ShapeDtypeStruct((nc * CH, H * DH), F32))
    if into:
        arr, f, _ = into
        in_specs.append(pl.BlockSpec(memory_space=pl.ANY))
        aliases[len(ins)] = len(fresh)
        ins.append(arr)
        out_specs.append(pl.BlockSpec((CH, len(packed) * H * DH), lambda c: (rc(c), f)))
        out_shapes.append(jax.ShapeDtypeStruct(arr.shape, arr.dtype))
    for _ in rowvecs:
        out_specs.append(pl.BlockSpec((1, H, CH), lambda c: (rc(c), 0, 0)))
        out_shapes.append(jax.ShapeDtypeStruct((nc, H, CH), F32))
    for a in params:
        out_specs.append(pl.BlockSpec(a.shape, lambda c: (0, 0)))
        out_shapes.append(jax.ShapeDtypeStruct(a.shape, F32))
    return pl.pallas_call(
        body, name=name, grid=(nc,), in_specs=in_specs, out_specs=out_specs, out_shape=out_shapes,
        input_output_aliases=aliases, scratch_shapes=[pltpu.VMEM((H, DH, DH), F32)],
        compiler_params=pltpu.CompilerParams(dimension_semantics=("arbitrary",), vmem_limit_bytes=VMEM_LIMIT),
    )(*ins)


def _gdn_pre_specs(p, cws, alog_row, dtb_row, tm, pos):
    sub = tm // 8
    rows = [pl.BlockSpec((tm, 3 * D), lambda i: (pos(i), 0)),
            pl.BlockSpec((8, 3 * D), lambda i: (jnp.maximum(pos(i) * sub - 1, 0), 0)),
            pl.BlockSpec((tm, LANES), lambda i: (pos(i), OFF_BA // LANES))]
    pars = [pl.BlockSpec(a.shape, lambda i: (0, 0)) for a in (*cws, alog_row, dtb_row)]
    return rows + pars, [p, p, p, *cws, alog_row, dtb_row]


def gdn_pre_fwd(p, cws, alog_row, dtb_row, lp, tm):
    def body(x_ref, prev_ref, ba_ref, c0, c1, c2, c3, al_ref, dt_ref, q_ref, k_ref, v_ref, g_ref, b_ref, gc_ref):
        i = pl.program_id(0)
        prev = jnp.where(i > 0, prev_ref[...], 0.0)
        xw = jnp.concatenate([prev, x_ref[...]], axis=0)
        c = _conv_taps(xw, [c0[...], c1[...], c2[...], c3[...]], tm)
        q, k, v = _gdn_qkv(c)
        mask = (_row_ids(i, tm) >= PAD).astype(F32)
        g, b, gc = _gdn_gates(ba_ref[...], al_ref[...], dt_ref[...], mask)
        q_ref[...] = q
        k_ref[...] = k
        v_ref[...] = v
        g_ref[...] = g
        b_ref[...] = b
        gc_ref[...] = gc

    in_specs, ins = _gdn_pre_specs(p, cws, alog_row, dtb_row, tm, lambda i: i)
    o_spec = pl.BlockSpec((tm, D), lambda i: (i, 0))
    return pl.pallas_call(
        body, name="gdn_pre_fwd", grid=(lp // tm,), in_specs=in_specs,
        out_specs=[o_spec] * 5 + [pl.BlockSpec((tm, LANES), lambda i: (i, 0))],
        out_shape=[jax.ShapeDtypeStruct((lp, D), F32)] * 5 + [jax.ShapeDtypeStruct((lp, LANES), F32)],
        compiler_params=pltpu.CompilerParams(dimension_semantics=("parallel",), vmem_limit_bytes=VMEM_LIMIT),
    )(*ins)


def gdn_pre_bwd(p, cws, alog_row, dtb_row, dq, dk, dv, dg, db, dgc, dp, lp, tm):
    nt = lp // tm
    pos = lambda i: nt - 1 - i

    def body(x_ref, prev_ref, ba_ref, c0, c1, c2, c3, al_ref, dt_ref, dq_ref, dk_ref, dv_ref, dg_ref, db_ref, dgc_ref,
             dp_any, dx_ref, dba_ref, dcw_ref, dpar_ref, carry):
        i = pl.program_id(0)
        t = pos(i)

        @pl.when(i == 0)
        def _():
            carry[...] = jnp.zeros_like(carry)
            dcw_ref[...] = jnp.zeros_like(dcw_ref)
            dpar_ref[...] = jnp.zeros_like(dpar_ref)

        cws_v = [c0[...], c1[...], c2[...], c3[...]]
        prev = jnp.where(t > 0, prev_ref[...], 0.0)
        xw = jnp.concatenate([prev, x_ref[...]], axis=0)
        xs = [xw[5 + j:5 + j + tm] for j in range(CONV_K)]
        c = sum(cws_v[j] * xs[j] for j in range(CONV_K))
        _, vjp_qkv = jax.vjp(_gdn_qkv, c)
        (dc,) = vjp_qkv((dq_ref[...], dk_ref[...], dv_ref[...]))
        zeros8 = jnp.zeros((8, 3 * D), F32)
        dcp = jnp.concatenate([zeros8, dc, zeros8], axis=0)
        dxw = sum(cws_v[j] * dcp[3 - j:3 - j + tm + 8] for j in range(CONV_K))
        dx_ref[...] = jnp.concatenate([dxw[8:tm], dxw[tm:] + carry[...]], axis=0).astype(dx_ref.dtype)
        carry[...] = dxw[:8]
        for j in range(CONV_K):
            dcw_ref[j:j + 1, :] += jnp.sum(dc * xs[j], axis=0, keepdims=True)
        mask = (_row_ids(t, tm) >= PAD).astype(F32)
        _, vjp_g = jax.vjp(lambda ba, al, dt: _gdn_gates(ba, al, dt, mask), ba_ref[...], al_ref[...], dt_ref[...])
        dba, dal, ddt = vjp_g((dg_ref[...], db_ref[...], dgc_ref[...]))
        dba_ref[...] = dba
        dpar_ref[0:1, :] += dal
        dpar_ref[1:2, :] += ddt

    in_specs, ins = _gdn_pre_specs(p, cws, alog_row, dtb_row, tm, pos)
    g_spec = pl.BlockSpec((tm, D), lambda i: (pos(i), 0))
    s_spec = pl.BlockSpec((tm, LANES), lambda i: (pos(i), 0))
    in_specs += [g_spec] * 5 + [s_spec, pl.BlockSpec(memory_space=pl.ANY)]
    ins += [dq, dk, dv, dg, db, dgc, dp]
    return pl.pallas_call(
        body, name="gdn_pre_bwd", grid=(nt,), in_specs=in_specs,
        out_specs=[pl.BlockSpec((tm, 3 * D), lambda i: (pos(i), 0)), s_spec,
                   pl.BlockSpec((8, 3 * D), lambda i: (0, 0)), pl.BlockSpec((8, LANES), lambda i: (0, 0))],
        out_shape=[jax.ShapeDtypeStruct(dp.shape, dp.dtype), jax.ShapeDtypeStruct((lp, LANES), F32),
                   jax.ShapeDtypeStruct((8, 3 * D), F32), jax.ShapeDtypeStruct((8, LANES), F32)],
        input_output_aliases={len(ins) - 1: 0},
        scratch_shapes=[pltpu.VMEM((8, 3 * D), F32)],
        compiler_params=pltpu.CompilerParams(dimension_semantics=("arbitrary",), vmem_limit_bytes=VMEM_LIMIT),
    )(*ins)


def _me():
    return lax.axis_index("x"), lax.axis_index("y"), lax.axis_index("c")


def _any_specs(n):
    return [pl.BlockSpec(memory_space=pl.ANY)] * n


Exchange = collections.namedtuple("Exchange", "ins out_shapes scratch start finish")


def _dma_sems(*shape):
    return pltpu.SemaphoreType.DMA(shape)


def run_exchange(name, ex):
    n_i, n_o = len(ex.ins), len(ex.out_shapes)

    def body(*refs):
        parts = (refs[:n_i], refs[n_i:n_i + n_o], refs[n_i + n_o:])
        ex.start(*parts)
        ex.finish(*parts)

    return pl.pallas_call(body, name=name, out_shape=list(ex.out_shapes), in_specs=_any_specs(n_i), out_specs=_any_specs(n_o),
                          scratch_shapes=list(ex.scratch))(*ex.ins)


def gather_exchange(ws):
    n = len(ws)

    def copies(w_refs, o_refs, sems):
        send_sems, recv_sems, fsend_sems, frecv_sems, osend_sems, orecv_sems = sems
        x, y, c = _me()
        me = 2 * x + y
        chips = [(1 - x, y), (x, 1 - y), (1 - x, 1 - y)]

        def own(a):
            return pltpu.make_async_remote_copy(src_ref=w_refs[a], dst_ref=o_refs[a].at[me], send_sem=osend_sems.at[a],
                                                recv_sem=orecv_sems.at[a], device_id=(x, y, 1 - c), device_id_type=MESH)

        def rows(a, cc):
            rh = ws[a].shape[0] // 2
            return pl.ds(pl.multiple_of(cc * rh, 8), rh)

        def ici(a, j, slot):
            px, py = chips[j]
            return pltpu.make_async_remote_copy(
                src_ref=w_refs[a].at[rows(a, c)], dst_ref=o_refs[a].at[slot, rows(a, c)], send_sem=send_sems.at[a, j],
                recv_sem=recv_sems.at[a, j], device_id=(px, py, c), device_id_type=MESH)

        def d2d(a, j, cc):
            px, py = chips[j]
            blk = o_refs[a].at[2 * px + py, rows(a, cc)]
            return pltpu.make_async_remote_copy(src_ref=blk, dst_ref=blk, send_sem=fsend_sems.at[a, j],
                                                recv_sem=frecv_sems.at[a, j], device_id=(x, y, 1 - c), device_id_type=MESH)
        mine = [own(a) for a in range(n)]
        sends = [ici(a, j, me) for a in range(n) for j in range(3)]
        arrivals = [ici(a, j, 2 * px + py) for a in range(n) for j, (px, py) in enumerate(chips)]
        passes = [d2d(a, j, c) for a in range(n) for j in range(3)]
        passed_to_me = [d2d(a, j, 1 - c) for a in range(n) for j in range(3)]
        return mine, sends, arrivals, passes, passed_to_me

    def start(w_refs, o_refs, sems):
        mine, sends, _, _, _ = copies(w_refs, o_refs, sems)
        for cp in mine + sends:
            cp.start()

    def finish(w_refs, o_refs, sems):
        mine, sends, arrivals, passes, passed_to_me = copies(w_refs, o_refs, sems)
        for arrival, onward in zip(arrivals, passes):
            arrival.wait_recv()
            onward.start()
        for cp in passed_to_me + mine:
            cp.wait_recv()
        for cp in sends + passes + mine:
            cp.wait_send()

    return Exchange(list(ws), [jax.ShapeDtypeStruct((N_CHIPS,) + w.shape, w.dtype) for w in ws],
                    [_dma_sems(n, 3), _dma_sems(n, 3), _dma_sems(n, 3), _dma_sems(n, 3), _dma_sems(n), _dma_sems(n)], start, finish)


def _simple_exchange(ins, out_shapes, sem_shape, copies):
    def start(i_refs, o_refs, sems):
        for cp in copies(i_refs, o_refs, *sems):
            cp.start()

    def finish(i_refs, o_refs, sems):
        cps = copies(i_refs, o_refs, *sems)
        for cp in cps:
            cp.wait_recv()
        for cp in cps:
            cp.wait_send()

    return Exchange(list(ins), out_shapes, [_dma_sems(*sem_shape), _dma_sems(*sem_shape)], start, finish)


def sibling_halves_exchange(gs):
    def copies(g_refs, o_refs, send_sems, recv_sems):
        x, y, c = _me()
        cps = []
        for a in range(len(gs)):
            rh = gs[a].shape[1] // 2
            for q in range(N_CHIPS):
                cps.append(pltpu.make_async_remote_copy(
                    src_ref=g_refs[a].at[q, pl.ds(pl.multiple_of((1 - c) * rh, 8), rh)], dst_ref=o_refs[a].at[q],
                    send_sem=send_sems.at[a, q], recv_sem=recv_sems.at[a, q], device_id=(x, y, 1 - c), device_id_type=MESH))
        return cps

    return _simple_exchange(gs, [jax.ShapeDtypeStruct((N_CHIPS, g.shape[1] // 2, g.shape[2]), g.dtype) for g in gs],
                            (len(gs), N_CHIPS), copies)


def scatter_chips_exchange(css):
    def copies(c_refs, o_refs, send_sems, recv_sems):
        x, y, c = _me()
        chips = [(1 - x, y), (x, 1 - y), (1 - x, 1 - y)]
        return [pltpu.make_async_remote_copy(
            src_ref=c_refs[a].at[2 * px + py], dst_ref=o_refs[a].at[j], send_sem=send_sems.at[a, j],
            recv_sem=recv_sems.at[a, j], device_id=(px, py, c), device_id_type=MESH)
            for a in range(len(css)) for j, (px, py) in enumerate(chips)]

    return _simple_exchange(css, [jax.ShapeDtypeStruct((3,) + cs.shape[1:], cs.dtype) for cs in css], (len(css), 3), copies)


def sibling_swap(name, halves):
    n = len(halves)

    def body(*refs):
        h_refs, o_refs = refs[:n], refs[n:2 * n]
        send_sems, recv_sems = refs[2 * n:]
        x, y, c = _me()
        cps = [pltpu.make_async_remote_copy(src_ref=h_refs[a], dst_ref=o_refs[a], send_sem=send_sems.at[a],
                                            recv_sem=recv_sems.at[a], device_id=(x, y, 1 - c), device_id_type=MESH)
               for a in range(n)]
        for cp in cps:
            cp.start()
        for cp in cps:
            cp.wait_recv()
        for cp in cps:
            cp.wait_send()

    return pl.pallas_call(
        body, name=name, out_shape=[jax.ShapeDtypeStruct(h.shape, h.dtype) for h in halves],
        in_specs=_any_specs(n), out_specs=_any_specs(n),
        scratch_shapes=[pltpu.SemaphoreType.DMA((n,)), pltpu.SemaphoreType.DMA((n,))],
    )(*halves)


def allgather_all(name, s):
    def body(s_ref, out_ref, send_sems, recv_sems, local_sem):
        x, y, c = _me()
        peers = [(x ^ ((m >> 2) & 1), y ^ ((m >> 1) & 1), c ^ (m & 1)) for m in range(1, 8)]
        mine = pltpu.make_async_copy(s_ref, out_ref.at[4 * x + 2 * y + c], local_sem)
        mine.start()

        def copy(j, slot):
            return pltpu.make_async_remote_copy(src_ref=s_ref, dst_ref=out_ref.at[slot], send_sem=send_sems.at[j],
                                                recv_sem=recv_sems.at[j], device_id=peers[j], device_id_type=MESH)
        sends = [copy(j, 4 * x + 2 * y + c) for j in range(7)]
        for cp in sends:
            cp.start()
        for j, (px, py, pc) in enumerate(peers):
            copy(j, 4 * px + 2 * py + pc).wait_recv()
        for cp in sends:
            cp.wait_send()
        mine.wait()

    return pl.pallas_call(
        body, name=name, out_shape=jax.ShapeDtypeStruct((8,) + s.shape, s.dtype),
        in_specs=_any_specs(1), out_specs=pl.BlockSpec(memory_space=pl.ANY),
        scratch_shapes=[pltpu.SemaphoreType.DMA((7,)), pltpu.SemaphoreType.DMA((7,)), pltpu.SemaphoreType.DMA(())],
    )(s)


def sum_slots(name, parts):
    n = len(parts)
    R, W = parts[0][0].shape[1:]
    tm = _tile_rows(R, W)
    idx = jnp.stack([jnp.asarray(s, jnp.int32) for _, s in parts])

    def body(idx_ref, *refs):
        acc = refs[0][...].astype(F32)
        for r in refs[1:n]:
            acc = acc + r[...].astype(F32)
        refs[n][...] = acc

    grid_spec = pltpu.PrefetchScalarGridSpec(
        num_scalar_prefetch=1, grid=(R // tm,),
        in_specs=[pl.BlockSpec((None, tm, W), lambda i, idx, k=k: (idx[k], i, 0)) for k in range(n)],
        out_specs=pl.BlockSpec((tm, W), lambda i, idx: (i, 0)))
    return pl.pallas_call(body, name=name, grid_spec=grid_spec, out_shape=jax.ShapeDtypeStruct((R, W), F32),
                          compiler_params=pltpu.CompilerParams(dimension_semantics=("parallel",)))(idx, *[a for a, _ in parts])


def chip_sums(name, g, recv, c):
    _, R, W = g.shape
    rh = R // 2
    tm = _tile_rows(rh, W)
    g8 = g.reshape(2 * N_CHIPS, rh, W)
    idx = jnp.asarray(c, jnp.int32).reshape(1)

    def body(idx_ref, g_ref, r_ref, o_ref):
        o_ref[...] = (g_ref[...] + r_ref[...]).astype(o_ref.dtype)

    grid_spec = pltpu.PrefetchScalarGridSpec(
        num_scalar_prefetch=1, grid=(N_CHIPS, rh // tm),
        in_specs=[pl.BlockSpec((None, tm, W), lambda q, i, idx: (2 * q + idx[0], i, 0)),
                  pl.BlockSpec((None, tm, W), lambda q, i, idx: (q, i, 0))],
        out_specs=pl.BlockSpec((None, tm, W), lambda q, i, idx: (q, i, 0)))
    return pl.pallas_call(body, name=name, grid_spec=grid_spec, out_shape=jax.ShapeDtypeStruct((N_CHIPS, rh, W), BF16),
                          compiler_params=pltpu.CompilerParams(dimension_semantics=("parallel", "parallel")))(idx, g8, recv)


def _adamw_update(w, g, m, v):
    m = ADAM_B1 * m + (1.0 - ADAM_B1) * g
    v = ADAM_B2 * v + (1.0 - ADAM_B2) * (g * g)
    m_hat = m / (1.0 - ADAM_B1 ** ADAM_STEP)
    v_hat = v / (1.0 - ADAM_B2 ** ADAM_STEP)
    return -ADAM_LR * (m_hat / (jnp.sqrt(v_hat) + ADAM_EPS) + ADAM_WD * w), m, v


def adamw(name, w, g, m, v):
    R, W = w.shape
    return rowwise(name, lambda i, w, g, m, v: _adamw_update(w, g, m, v), [(a, W, 0) for a in (w, g, m, v)], [],
                   [("new", W, F32)] * 3, n_rows=R, tm=_tile_rows(R, W, 2 ** 20))


def adamw_halves(name, w, g_mine, g_other, m, v, c):
    R, W = w.shape
    rh = R // 2
    tm = _tile_rows(rh, W, 2 ** 20)
    nh = rh // tm
    idx = jnp.asarray(c, jnp.int32).reshape(1)

    def body(idx_ref, w_ref, ga_ref, gb_ref, m_ref, v_ref, g_ref, d_ref, mo_ref, vo_ref):
        mine = (pl.program_id(0) // nh) == idx_ref[0]
        g = jnp.where(mine, ga_ref[...], gb_ref[...])
        d, m, v = _adamw_update(w_ref[...], g, m_ref[...], v_ref[...])
        g_ref[...] = g
        d_ref[...] = d
        mo_ref[...] = m
        vo_ref[...] = v

    full = pl.BlockSpec((tm, W), lambda i, idx: (i, 0))
    half = pl.BlockSpec((tm, W), lambda i, idx: (i % nh, 0))
    grid_spec = pltpu.PrefetchScalarGridSpec(num_scalar_prefetch=1, grid=(R // tm,), in_specs=[full, half, half, full, full],
                                             out_specs=[full] * 4)
    return pl.pallas_call(body, name=name, grid_spec=grid_spec, out_shape=[jax.ShapeDtypeStruct((R, W), F32)] * 4,
                          compiler_params=pltpu.CompilerParams(dimension_semantics=("parallel",)))(idx, w, g_mine, g_other, m, v)


def _w_in_padded(w):
    return jnp.concatenate([w[:, :4096], w[:, 4112:], w[:, 4096:4112], jnp.zeros((D, PW - D_PROJ), w.dtype)], axis=1)


def _w_in_unpadded(g):
    return jnp.concatenate([g[:, :4096], g[:, OFF_BA:OFF_BA + 16], g[:, 4096:OFF_BA]], axis=1)


def _ret_consts():
    f = F32
    log_gamma = jnp.log1p(-jnp.exp2(-5.0 - jnp.arange(H, dtype=f)))
    pos = jnp.arange(CH, dtype=f)
    causal = jnp.tril(jnp.ones((CH, CH), dtype=bool))
    diff = pos[:, None] - pos[None, :]
    decay = jnp.where(causal, jnp.exp(jnp.where(causal, diff, 0.0) * log_gamma[:, None, None]), 0.0)
    xi = jnp.broadcast_to(jnp.exp((pos + 1.0) * log_gamma[:, None])[:, :, None], (H, CH, DH))
    zeta = jnp.broadcast_to(jnp.exp((CH - 1.0 - pos) * log_gamma[:, None])[:, :, None], (H, CH, DH))
    cd = jnp.broadcast_to(jnp.exp(CH * log_gamma)[:, None, None], (H, 1, DH))
    return decay, xi, zeta, cd


def _rope_tables(lp):
    pos = jnp.arange(lp, dtype=F32) - float(PAD)
    inv = 1.0 / (ROPE_BASE ** jnp.linspace(0.0, 1.0, DH // 2, dtype=F32))
    ang = pos[:, None] * inv[None, :]
    cos, sin = jnp.cos(ang), jnp.sin(ang)
    ct = jnp.repeat(cos, 2, axis=1)
    st = jnp.stack([-sin, sin], axis=-1).reshape(lp, DH)
    return ct, st


def local_step(x, tgt, w, small, hooks=None):
    hooks = hooks or {}
    seq = x.shape[0]
    lp = HEAD_ROWS + seq
    nc = lp // CH
    tm = _pick(lp, 192, CH)
    row = functools.partial(rowwise, n_rows=lp, tm=tm)
    gw = {}

    def mm(name, *args, **kw):
        if name not in hooks:
            return matmul(name, *args, **kw)
        make_exchange, take = hooks[name]
        out, results = matmul(name, *args, side=make_exchange(gw), **kw)
        take(results)
        return out

    h0 = jnp.concatenate([jnp.zeros((PAD, D), F32), w["meta_tokens"], x], axis=0)
    tgt_p = jnp.concatenate([jnp.zeros((HEAD_ROWS, D), F32), tgt], axis=0)

    def ffn_fwd(tag, h, wn, n=None, next_norm=None):
        if n is None:
            n = row(f"{tag}_norm", lambda i, h, wn: _rms(h, wn), [(h, D, 0)], [wn], [("new", D, BF16)])[0]
        gu = mm(f"{tag}_up", n, w[f"{tag}_w_in"], "nn", b_split=True, ti_cap=1376)
        mid = row(f"{tag}_act", lambda i, gu: _swiglu(gu), [(gu, 2 * DFF, 0)], [], [("new", DFF, BF16)])[0]
        out = mm(f"{tag}_down", mid, w[f"{tag}_w_out"], "nn", resid=h, alpha=0.5, tj_cap=1024, tr_cap=2816, norm_w=next_norm)
        out, n_next = out if next_norm is not None else (out, None)
        return out, (h, n, gu, mid), n_next

    def ffn_bwd(tag, dh, saved, wn):
        h, n, gu, mid = saved
        w_in_full, w_out = w[f"{tag}_w_in_full"], w[f"{tag}_w_out"]
        dmid = mm(f"{tag}_dmid", dh, w_out, "nt", alpha=0.5, ti_cap=1376, tj_cap=1408)
        dw_out = mm(f"{tag}_dwout", mid, dh, "tn", alpha=0.5, ti_cap=1408, tj_cap=1024, tr_cap=1376)

        def act_bwd(i, gu, dmid):
            _, vjp = jax.vjp(_swiglu, gu)
            return vjp(dmid)[0]
        dgu = row(f"{tag}_dact", act_bwd, [(gu, 2 * DFF, 0), (dmid, DFF, 0)], [], [("new", 2 * DFF, BF16)])[0]
        dn = mm(f"{tag}_dn", dgu, w_in_full, "nt", tj_cap=1024, tr_cap=2 * DFF)
        dw_in = mm(f"{tag}_dwin", n, dgu, "tn", ti_cap=1024, tr_cap=2752, o_split=True)
        dh_in, dwn = norm_bwd(f"{tag}_dnorm", h, wn, dn, dh)
        return dh_in, dw_in, dw_out, dwn

    def norm_bwd(name, h, wn, dn, dres):
        def fn(i, h, dn, dres, wn):
            _, vjp = jax.vjp(_rms, h, wn)
            dh, dw = vjp(dn)
            return dres + dh, dw
        return row(name, fn, [(h, D, 0), (dn, D, 0), (dres, D, 0)], [wn], [("new", D, F32)], [(1, D)])

    h1, ffn1_saved, n2 = ffn_fwd("ffn1", h0, small["ffn1_norm"], next_norm=small["mix_norm"])
    p = mm("mix_proj", n2, w["w_in_p"], "nn", ti_cap=1376, tj_cap=1152)

    cws = [w["gdn_conv_w"][j:j + 1] for j in range(CONV_K)]
    alog_row = jnp.zeros((1, LANES), F32).at[:, H:2 * H].set(small["gdn_a_log"])
    dtb_row = jnp.zeros((1, LANES), F32).at[:, H:2 * H].set(small["gdn_dt_bias"])
    q, k, v, gcb, bb, gc = gdn_pre_fwd(p, cws, alog_row, dtb_row, lp, tm)
    gc_rows = gc[:, H:2 * H].reshape(nc, CH, H).transpose(0, 2, 1)
    gdn_blocks = [(q, 0), (k, 0), (v, 0), (gcb, 0), (bb, 0), (p, OFF_Z // D)]
    y_a, gdn_states = _scan_fwd("gdn_scan_fwd", _gdn_chunk, gdn_blocks, [gc_rows], [], nc, params=[small["gdn_out_norm"]],
                                out_dtype=BF16)

    ct, st = _rope_tables(lp)
    ret_consts = list(_ret_consts())
    ret_blocks = [(p, OFF_RQ // D), (p, OFF_RK // D), (p, OFF_RV // D), (p, OFF_RG // D)]
    y_b, ret_states = _scan_fwd("ret_scan_fwd", _ret_chunk, ret_blocks, [], ret_consts, nc, shared=[ct, st],
                                params=[small["ret_out_norm"]], out_dtype=BF16)

    br_a = matmul("branch_gdn", y_a, w["w_branch_gdn"], "nn", ti_cap=1376, tj_cap=1024)
    br_b = matmul("branch_ret", y_b, w["w_branch_ret"], "nn", ti_cap=1376, tj_cap=1024)
    merged = row("merge", lambda i, a, b, ga, gb_: _merge(a, b, ga, gb_),
                 [(br_a, D, 0), (br_b, D, 0), (p, D, OFF_GA // D), (p, D, OFF_GB // D)], [], [("new", D, BF16)])[0]
    h2, n3 = matmul("mix_out", merged, w["w_out"], "nn", resid=h1, ti_cap=1376, tj_cap=1024, norm_w=small["ffn2_norm"])
    h3, ffn2_saved, _ = ffn_fwd("ffn2", h2, small["ffn2_norm"], n=n3)

    def head(i, h, t, wn):
        mask = (_row_ids(i, tm) >= HEAD_ROWS).astype(F32)
        y, vjp = jax.vjp(_rms, h, wn)
        err = (y - t) * mask
        dh, dw = vjp(err * (1.0 / D))
        return dh, dw, jnp.sum(err * err, keepdims=True).reshape(1, 1) * (0.5 / D) * jnp.ones((1, LANES), F32)
    dh3, d_final, loss_row = row("loss_head", head, [(h3, D, 0), (tgt_p, D, 0)], [small["final_norm"]], [("new", D, F32)],
                                 [(1, D), (1, LANES)])

    gs = {"final_norm": d_final}
    dh2, gw["ffn2_w_in"], gw["ffn2_w_out"], gs["ffn2_norm"] = ffn_bwd("ffn2", dh3, ffn2_saved, small["ffn2_norm"])
    dmerged = matmul("mix_out_dx", dh2, w["w_out"], "nt", ti_cap=1376, tj_cap=1024)
    gw["w_out"] = matmul("mix_out_dw", merged, dh2, "tn", ti_cap=1024, tj_cap=1024, tr_cap=2752)
    dp = lax.empty((lp, PW), BF16)

    def merge_bwd(i, dm, a, b, ga, gb_):
        _, vjp = jax.vjp(_merge, a, b, ga, gb_)
        da, db, dga, dgb = vjp(dm)
        return da, db, jnp.concatenate([dga, dgb], axis=1)
    da, db_, dp = row("merge_bwd", merge_bwd,
                      [(dmerged, D, 0), (br_a, D, 0), (br_b, D, 0), (p, D, OFF_GA // D), (p, D, OFF_GB // D)], [],
                      [("new", D, BF16), ("new", D, BF16), ("into", dp, 2 * D, OFF_GA // (2 * D))])
    dy_a = matmul("branch_gdn_dx", da, w["w_branch_gdn"], "nt", ti_cap=1376, tj_cap=1024)
    gw["w_branch_gdn"] = matmul("branch_gdn_dw", y_a, da, "tn", ti_cap=1024, tj_cap=1024, tr_cap=2752)
    dy_b = matmul("branch_ret_dx", db_, w["w_branch_ret"], "nt", ti_cap=1376, tj_cap=1024)
    gw["w_branch_ret"] = matmul("branch_ret_dw", y_b, db_, "tn", ti_cap=1024, tj_cap=1024, tr_cap=2752)

    dp, gs["ret_out_norm"] = _scan_bwd("ret_scan_bwd", _ret_chunk, ret_blocks, [], ret_consts, ret_states, dy_b,
                                       (dp, OFF_RQ // (4 * D), [0, 1, 2, 3]), nc, shared=[ct, st], params=[small["ret_out_norm"]])
    dq, dk, dv, dgcb, dbb, dp, dgc_rows, gs["gdn_out_norm"] = _scan_bwd(
        "gdn_scan_bwd", _gdn_chunk, gdn_blocks, [gc_rows], [], gdn_states, dy_a, (dp, OFF_Z // D, [5]), nc,
        params=[small["gdn_out_norm"]])
    dgc = jnp.pad(dgc_rows.transpose(0, 2, 1).reshape(lp, H), ((0, 0), (H, LANES - 2 * H)))
    dp, dba, dcw, dgate = gdn_pre_bwd(p, cws, alog_row, dtb_row, dq, dk, dv, dgcb, dbb, dgc, dp, lp, tm)
    dp = row("dp_ba", lambda i, t: t, [(dba, LANES, 0)], [], [("into", dp, LANES, OFF_BA // LANES)])[0]
    gw["gdn_conv_w"] = dcw[:CONV_K]
    gs["gdn_a_log"] = dgate[0:1, H:2 * H]
    gs["gdn_dt_bias"] = dgate[1:2, H:2 * H]

    dn2 = matmul("mix_proj_dx", dp, w["w_in_p"], "nt", tj_cap=1024, tr_cap=3456)
    gw["w_in_p"] = matmul("mix_proj_dw", n2, dp, "tn", ti_cap=1024, tj_cap=1152, tr_cap=2752)
    dh1, gs["mix_norm"] = norm_bwd("mix_dnorm", h1, small["mix_norm"], dn2, dh2)
    dh0, gw["ffn1_w_in"], gw["ffn1_w_out"], gs["ffn1_norm"] = ffn_bwd("ffn1", dh1, ffn1_saved, small["ffn1_norm"])
    gw["meta_tokens"] = dh0[PAD:HEAD_ROWS]
    return loss_row, dh0[HEAD_ROWS:], gw, gs


BIG = ("ffn1_w_in", "ffn1_w_out", "w_in", "w_branch_gdn", "w_branch_ret", "w_out", "ffn2_w_in", "ffn2_w_out")
COL_SHARDED = ("ffn1_w_in", "w_in", "ffn2_w_in")
WITH_FFN1_UP = ("ffn1_w_out", "w_in")
WITH_MIX_PROJ = ("w_branch_gdn", "w_branch_ret", "w_out", "ffn2_w_in", "ffn2_w_out")
EARLY_GRADS = ("ffn2_w_in", "ffn2_w_out", "w_in", "w_branch_gdn", "w_branch_ret", "w_out")
LATE_GRADS = ("ffn1_w_in", "ffn1_w_out")
SMALL = ("ffn1_norm", "mix_norm", "ret_out_norm", "ffn2_norm", "final_norm", "gdn_out_norm", "gdn_a_log", "gdn_dt_bias")
WEIGHTS = ("meta_tokens", "ffn1_norm", "ffn1_w_in", "ffn1_w_out", "mix_norm", "w_in", "gdn_conv_w", "gdn_a_log", "gdn_dt_bias",
           "gdn_out_norm", "ret_out_norm", "w_branch_gdn", "w_branch_ret", "w_out", "ffn2_norm", "ffn2_w_in", "ffn2_w_out",
           "final_norm")
LOSS_ROW = 6
CONV_ROW0, META_ROW0, SMALL_ROWS = 8, 24, 40


def pack_small(vals):
    rows = [vals[n].reshape(1, D) for n in SMALL[:5]]
    r5 = jnp.concatenate([vals["gdn_out_norm"].reshape(1, DH), vals["gdn_a_log"].reshape(1, H), vals["gdn_dt_bias"].reshape(1, H),
                          jnp.zeros((1, D - DH - 2 * H), F32)], axis=1)
    return jnp.concatenate(rows + [r5, jnp.zeros((2, D), F32)], axis=0)


def unpack_small(packed, shapes):
    out = {n: packed[j].reshape(shapes[n]) for j, n in enumerate(SMALL[:5])}
    out["gdn_out_norm"] = packed[5, :DH].reshape(shapes["gdn_out_norm"])
    out["gdn_a_log"] = packed[5, DH:DH + H].reshape(shapes["gdn_a_log"])
    out["gdn_dt_bias"] = packed[5, DH + H:DH + 2 * H].reshape(shapes["gdn_dt_bias"])
    return out


def kernel(x, meta_tokens, ffn1_norm, ffn1_w_in, ffn1_w_out, mix_norm, w_in, gdn_conv_w, gdn_a_log, gdn_dt_bias, gdn_out_norm, ret_out_norm, w_branch_gdn, w_branch_ret, w_out, ffn2_norm, ffn2_w_in, ffn2_w_out, final_norm, loss_target, m_meta_tokens, m_ffn1_norm, m_ffn1_w_in, m_ffn1_w_out, m_mix_norm, m_w_in, m_gdn_conv_w, m_gdn_a_log, m_gdn_dt_bias, m_gdn_out_norm, m_ret_out_norm, m_w_branch_gdn, m_w_branch_ret, m_w_out, m_ffn2_norm, m_ffn2_w_in, m_ffn2_w_out, m_final_norm, v_meta_tokens, v_ffn1_norm, v_ffn1_w_in, v_ffn1_w_out, v_mix_norm, v_w_in, v_gdn_conv_w, v_gdn_a_log, v_gdn_dt_bias, v_gdn_out_norm, v_ret_out_norm, v_w_branch_gdn, v_w_branch_ret, v_w_out, v_ffn2_norm, v_ffn2_w_in, v_ffn2_w_out, v_final_norm):
    a = dict(locals())
    wts = {n: a[n] for n in WEIGHTS}
    mom_m = {n: a["m_" + n] for n in WEIGHTS}
    mom_v = {n: a["v_" + n] for n in WEIGHTS}
    shapes = {n: wts[n].shape for n in WEIGHTS}
    flat = lambda t: t.reshape(t.shape[-2:])
    c = lax.axis_index("c")
    chip = 2 * lax.axis_index("x") + lax.axis_index("y")

    exact = jnp.zeros((16, D), F32).at[0:3].set(wts["gdn_conv_w"].reshape(3, D)).at[3:7].set(wts["meta_tokens"].reshape(4, D))
    bf16_block = lambda n: flat(wts[n]).astype(BF16)
    w = {}

    def take_weights(names):
        def take(gathered):
            for n, g in zip(names, gathered):
                if n in COL_SHARDED:
                    whole = jnp.concatenate([g[q] for q in range(N_CHIPS)], axis=1)
                    if n == "w_in":
                        w["w_in_p"] = _w_in_padded(whole)
                    else:
                        w[n], w[n + "_full"] = g, whole
                else:
                    w[n] = g.reshape(N_CHIPS * g.shape[1], D)
        return take

    first = run_exchange("gather_first", gather_exchange([bf16_block("ffn1_w_in"), exact]))
    take_weights(["ffn1_w_in"])(first[:1])
    exact = first[1]
    w["gdn_conv_w"] = jnp.concatenate([exact[q, 0:3].reshape(CONV_K, 3 * D // 4) for q in range(N_CHIPS)], axis=1)
    w["meta_tokens"] = jnp.concatenate([exact[q, 3:7].reshape(N_META, D // 4) for q in range(N_CHIPS)], axis=1)
    small = {n: wts[n].reshape(1, -1) for n in SMALL}
    hooks = {"ffn1_up": (lambda gw: gather_exchange([bf16_block(n) for n in WITH_FFN1_UP]), take_weights(WITH_FFN1_UP)),
             "mix_proj": (lambda gw: gather_exchange([bf16_block(n) for n in WITH_MIX_PROJ]), take_weights(WITH_MIX_PROJ))}

    def blocks(gw, n):
        if n == "w_in":
            return jnp.stack(jnp.split(_w_in_unpadded(gw["w_in_p"]), N_CHIPS, axis=1))
        return gw[n] if n in COL_SHARDED else gw[n].reshape(N_CHIPS, gw[n].shape[0] // N_CHIPS, D)

    chip_sum, from_chips = {}, {}

    def swap_early(gw):
        chip_sum["g"] = [blocks(gw, n) for n in EARLY_GRADS]
        return sibling_halves_exchange(chip_sum["g"])

    def sum_early(from_sib):
        for n, g, r in zip(EARLY_GRADS, chip_sum.pop("g"), from_sib):
            chip_sum[n] = chip_sums(f"grads_chip_sum_{n}", g, r, c)

    def scatter_of(names):
        return (lambda gw: scatter_chips_exchange([chip_sum[n] for n in names])), (lambda got: from_chips.update(zip(names, got)))
    hooks["ffn1_dmid"] = (swap_early, sum_early)
    hooks["ffn1_dn"] = scatter_of(["w_in"])
    hooks["ffn1_dwin"] = scatter_of([n for n in EARLY_GRADS if n != "w_in"])

    loss_row, gx, gw, gs = local_step(x[0], loss_target[0], w, small, hooks)

    late = [blocks(gw, n) for n in LATE_GRADS]
    from_sib = run_exchange("grads_sibling_late", sibling_halves_exchange(late))
    for n, g, r in zip(LATE_GRADS, late, from_sib):
        chip_sum[n] = chip_sums(f"grads_chip_sum_{n}", g, r, c)
    from_chips.update(zip(LATE_GRADS, run_exchange("grads_scatter_late", scatter_chips_exchange([chip_sum[n] for n in LATE_GRADS]))))
    halves = [sum_slots(f"grads_sum_{n}", [(chip_sum[n], chip), (from_chips[n], 0), (from_chips[n], 1), (from_chips[n], 2)])
              for n in BIG]
    others = sibling_swap("grads_swap", halves)
    grads, delta, new_m, new_v = {}, {}, {}, {}
    for n, mine, other in zip(BIG, halves, others):
        res = adamw_halves(f"adamw_{n}", flat(wts[n]), mine, other, flat(mom_m[n]), flat(mom_v[n]), c)
        grads[n], delta[n], new_m[n], new_v[n] = (t.reshape(shapes[n]) for t in res)

    sm = jnp.concatenate([pack_small(gs).at[LOSS_ROW, :LANES].set(loss_row[0]),
                          gw["gdn_conv_w"].reshape(3 * CONV_K, D), jnp.zeros((META_ROW0 - CONV_ROW0 - 3 * CONV_K, D), F32),
                          gw["meta_tokens"]], axis=0)
    every = allgather_all("small_gather", sm)
    sm_sum = sum_slots("small_sum", [(every, s) for s in range(8)])
    d_s, m_s, v_s = adamw("adamw_small", pack_small(small), sm_sum[:8], pack_small({n: mom_m[n].reshape(1, -1) for n in SMALL}),
                          pack_small({n: mom_v[n].reshape(1, -1) for n in SMALL}))
    grads.update(unpack_small(sm_sum, shapes))
    delta.update(unpack_small(d_s, shapes))
    new_m.update(unpack_small(m_s, shapes))
    new_v.update(unpack_small(v_s, shapes))
    g_conv = lax.dynamic_slice_in_dim(sm_sum[CONV_ROW0:CONV_ROW0 + 3 * CONV_K].reshape(CONV_K, 3 * D), chip * (3 * D // 4), 3 * D // 4, 1)
    g_meta = lax.dynamic_slice_in_dim(sm_sum[META_ROW0:META_ROW0 + N_META], chip * (D // 4), D // 4, 1)
    for n, g in (("gdn_conv_w", g_conv), ("meta_tokens", g_meta)):
        d_, m_, v_ = adamw(f"adamw_{n}", flat(wts[n]), g, flat(mom_m[n]), flat(mom_v[n]))
        grads[n], delta[n], new_m[n], new_v[n] = (t.reshape(shapes[n]) for t in (g, d_, m_, v_))
    loss = sm_sum[LOSS_ROW, 0]

    return (loss, gx[None], *[grads[n] for n in WEIGHTS], *[delta[n] for n in WEIGHTS], *[new_m[n] for n in WEIGHTS],
            *[new_v[n] for n in WEIGHTS])
```

```python
import collections
import functools

import jax
import jax.numpy as jnp
from jax import lax
from jax.experimental import pallas as pl
from jax.experimental.pallas import tpu as pltpu

F32 = jnp.float32
BF16 = jnp.bfloat16
HI = lax.Precision.HIGHEST
MESH = pl.DeviceIdType.MESH

D = 1024
N_META = 16
PAD = 48
HEAD_ROWS = PAD + N_META
CH = 64
H = 8
DH = 128
DFF = 2816
CONV_K = 4
EPS = 1e-6
ROPE_BASE = 10000.0
LANES = 128
N_CHIPS = 4
VMEM_LIMIT = 56 * 2 ** 20

OFF_QKV, OFF_Z, OFF_RQ, OFF_RK, OFF_RV, OFF_RG, OFF_GA, OFF_GB, OFF_BA = 0, 3072, 4096, 5120, 6144, 7168, 8192, 9216, 10240
PW = 10368
D_PROJ = 10256

ADAM_LR, ADAM_B1, ADAM_B2, ADAM_EPS, ADAM_WD, ADAM_STEP = 0.001, 0.9, 0.999, 1e-08, 0.01, 10


def _pick(n, cap, mult):
    best = None
    for t in range(mult, min(n, cap) + 1, mult):
        if n % t == 0:
            best = t
    return best if best is not None else n


def _tile_rows(rows, cols, block_bytes=3 * 2 ** 19):
    return _pick(rows, max(8, block_bytes // (4 * cols)), 8)


def _dot(a, b, dims, prec=None):
    return lax.dot_general(a, b, (dims, ((), ())), precision=prec, preferred_element_type=F32)


def _mmh(a, b):
    return _dot(a, b, ((1,), (0,)), HI)


def _split(a):
    hi = a.astype(BF16)
    return hi, (a - hi.astype(F32)).astype(BF16)


_NN, _NT, _TN = ((2,), (1,)), ((2,), (2,)), ((1,), (1,))


def _bdot(a, b, dims):
    return lax.dot_general(a, b, (dims, ((0,), (0,))), preferred_element_type=F32)


def _bmm(a, b, dims):
    return _bdot(a.astype(BF16), b.astype(BF16), dims)


def _bdot3(a, b, dims):
    ah, al = _split(a)
    bh, bl = _split(b)
    if dims == _NN:
        m = a.shape[1]
        both = _bdot(jnp.concatenate([ah, al], axis=1), bh, dims)
        return both[:, :m] + (both[:, m:] + _bdot(ah, bl, dims))
    return _bdot(ah, bh, dims) + (_bdot(ah, bl, dims) + _bdot(al, bh, dims))


def matmul(name, a, b, mode, *, out_dtype=F32, resid=None, alpha=1.0, ti_cap=688, tj_cap=512, tr_cap=1408,
           b_split=False, o_split=False, side=None, norm_w=None):
    if mode == "nn":
        I, R = a.shape
        J = N_CHIPS * b.shape[2] if b_split else b.shape[1]
    elif mode == "nt":
        I, R = a.shape
        J = b.shape[1] if b_split else b.shape[0]
    else:
        R, I = a.shape
        J = b.shape[1]
    ti = _pick(I, ti_cap, 16) if mode != "tn" else _pick(I, ti_cap, LANES)
    tj = _pick(J, tj_cap, LANES)
    tr = _pick(R, tr_cap, LANES) if mode != "tn" else _pick(R, tr_cap, 16)
    if mode == "nn":
        a_spec = pl.BlockSpec((ti, tr), lambda i, j, r: (i, r))
        if b_split:
            tj = J // N_CHIPS
            b_spec = pl.BlockSpec((None, tr, tj), lambda i, j, r: (j, r, 0))
        else:
            b_spec = pl.BlockSpec((tr, tj), lambda i, j, r: (r, j))
        dims = ((1,), (0,))
    elif mode == "nt":
        if b_split:
            tr = R // N_CHIPS
            b_spec = pl.BlockSpec((None, tj, tr), lambda i, j, r: (r, j, 0))
        else:
            b_spec = pl.BlockSpec((tj, tr), lambda i, j, r: (j, r))
        a_spec = pl.BlockSpec((ti, tr), lambda i, j, r: (i, r))
        dims = ((1,), (1,))
    else:
        if o_split:
            tj = J // N_CHIPS
        a_spec = pl.BlockSpec((tr, ti), lambda i, j, r: (r, i))
        b_spec = pl.BlockSpec((tr, tj), lambda i, j, r: (r, j))
        dims = ((0,), (0,))
    nr = R // tr
    assert I % ti == 0 and J % tj == 0 and R % tr == 0, (name, I, J, R, ti, tj, tr)
    if o_split:
        o_spec = pl.BlockSpec((None, ti, tj), lambda i, j, r: (j, i, 0))
        out_shape = jax.ShapeDtypeStruct((N_CHIPS, I, tj), out_dtype)
    else:
        o_spec = pl.BlockSpec((ti, tj), lambda i, j, r: (i, j))
        out_shape = jax.ShapeDtypeStruct((I, J), out_dtype)
    has_resid = resid is not None

    has_norm = norm_w is not None
    assert not (has_norm and (side or tj != J or o_split)), name

    def finish(refs, res):
        o_ref = refs[n_mm_in + n_si]
        res = res * alpha if alpha != 1.0 else res
        if has_resid:
            res = refs[2][...] + res
        o_ref[...] = res.astype(o_ref.dtype)
        if has_norm:
            refs[n_mm_in + 1][...] = _rms(res, refs[n_mm_in - 1][...]).astype(BF16)

    n_mm_in = 2 + int(has_resid) + int(has_norm)
    n_si = len(side.ins) if side else 0
    n_so = len(side.out_shapes) if side else 0
    grid = (I // ti, J // tj, nr)

    def body(*refs):
        a_ref, b_ref = refs[:2]
        r_ref = refs[2] if has_resid else None
        acc = refs[n_mm_in + n_si + 1 + n_so + int(has_norm)] if nr > 1 else None
        step = (pl.program_id(0) * grid[1] + pl.program_id(1)) * grid[2] + pl.program_id(2)
        if side:
            s_refs = (refs[n_mm_in:n_mm_in + n_si], refs[n_mm_in + n_si + 1:n_mm_in + n_si + 1 + n_so],
                      refs[n_mm_in + n_si + 1 + n_so + (1 if nr > 1 else 0):])

            @pl.when(step == 0)
            def _():
                side.start(*s_refs)

        prod = _dot(a_ref[...].astype(BF16), b_ref[...].astype(BF16), dims)
        if nr == 1:
            finish(refs, prod)
        else:
            r = pl.program_id(2)

            @pl.when(r == 0)
            def _():
                acc[...] = prod

            @pl.when(r > 0)
            def _():
                acc[...] += prod

            @pl.when(r == nr - 1)
            def _():
                finish(refs, acc[...])

        if side:
            @pl.when(step == grid[0] * grid[1] * grid[2] - 1)
            def _():
                side.finish(*s_refs)

    ins = [a, b] + ([resid] if has_resid else []) + ([norm_w] if has_norm else [])
    specs = [a_spec, b_spec] + ([o_spec] if has_resid else [])
    specs += [pl.BlockSpec(norm_w.shape, lambda i, j, r: (0, 0))] if has_norm else []
    scratch = [pltpu.VMEM((ti, tj), F32)] if nr > 1 else []
    if not side:
        return pl.pallas_call(
            body, name=name, grid=grid, in_specs=specs, out_specs=[o_spec, o_spec] if has_norm else o_spec,
            out_shape=[out_shape, jax.ShapeDtypeStruct((I, J), BF16)] if has_norm else out_shape, scratch_shapes=scratch,
            compiler_params=pltpu.CompilerParams(dimension_semantics=("parallel", "parallel", "arbitrary"),
                                                 vmem_limit_bytes=VMEM_LIMIT),
        )(*ins)
    res = pl.pallas_call(
        body, name=name, grid=grid, in_specs=specs + _any_specs(n_si), out_specs=[o_spec] + _any_specs(n_so),
        out_shape=[out_shape] + list(side.out_shapes), scratch_shapes=scratch + list(side.scratch),
        compiler_params=pltpu.CompilerParams(dimension_semantics=("arbitrary", "arbitrary", "arbitrary"),
                                             vmem_limit_bytes=VMEM_LIMIT),
    )(*ins, *side.ins)
    return res[0], res[1:]


def rowwise(name, fn, rows, pars, outs, accs=(), *, n_rows, tm):
    nt = n_rows // tm
    assert nt * tm == n_rows
    in_specs, ins = [], []
    for arr, w, cb in rows:
        in_specs.append(pl.BlockSpec((tm, w), lambda i, cb=cb: (i, cb)))
        ins.append(arr)
    for p in pars:
        in_specs.append(pl.BlockSpec(p.shape, lambda i, nd=p.ndim: (0,) * nd))
        ins.append(p)
    n_in = len(ins)
    out_specs, out_shapes, aliases = [], [], {}
    for k, o in enumerate(outs):
        if o[0] == "new":
            _, w, dt = o
            out_specs.append(pl.BlockSpec((tm, w), lambda i: (i, 0)))
            out_shapes.append(jax.ShapeDtypeStruct((n_rows, w), dt))
        else:
            _, arr, w, cb = o
            in_specs.append(pl.BlockSpec(memory_space=pl.ANY))
            aliases[len(ins)] = k
            ins.append(arr)
            out_specs.append(pl.BlockSpec((tm, w), lambda i, cb=cb: (i, cb)))
            out_shapes.append(jax.ShapeDtypeStruct(arr.shape, arr.dtype))
    for r, w in accs:
        out_specs.append(pl.BlockSpec((r, w), lambda i: (0, 0)))
        out_shapes.append(jax.ShapeDtypeStruct((r, w), F32))
    n_all_in, n_out, n_acc = len(ins), len(outs), len(accs)

    def body(*refs):
        i = pl.program_id(0)
        vals = [r[...] for r in refs[:n_in]]
        res = fn(i, *vals)
        if not isinstance(res, (tuple, list)):
            res = (res,)
        o_refs = refs[n_all_in:n_all_in + n_out]
        a_refs = refs[n_all_in + n_out:]
        for r, v in zip(o_refs, res[:n_out]):
            r[...] = v.astype(r.dtype)
        if n_acc:
            @pl.when(i == 0)
            def _():
                for r in a_refs:
                    r[...] = jnp.zeros_like(r)
            for r, v in zip(a_refs, res[n_out:]):
                r[...] += v

    return pl.pallas_call(
        body, name=name, grid=(nt,), in_specs=in_specs, out_specs=out_specs, out_shape=out_shapes,
        input_output_aliases=aliases,
        compiler_params=pltpu.CompilerParams(dimension_semantics=("arbitrary",), vmem_limit_bytes=VMEM_LIMIT),
    )(*ins)


def _row_ids(i, tm):
    return i * tm + lax.broadcasted_iota(jnp.int32, (tm, 1), 0)


def _sigmoid(x):
    return 1.0 / (1.0 + jnp.exp(-x))


def _silu(x):
    return x * _sigmoid(x)


def _softplus(x):
    return jnp.maximum(x, 0.0) + jnp.log(1.0 + jnp.exp(-jnp.abs(x)))


def _rms(x, w):
    return x * lax.rsqrt(jnp.mean(x * x, axis=-1, keepdims=True) + EPS) * w


def _heads(fn, *xs):
    return jnp.concatenate([fn(h, *[x[:, h * DH:(h + 1) * DH] for x in xs]) for h in range(H)], axis=1)


def _swiglu(gu):
    return _silu(gu[:, :DFF]) * gu[:, DFF:]


def _merge(a, b, ga, gb):
    return _sigmoid(ga) * a + _sigmoid(gb) * b


def _select_matrix(first_lane):
    r = lax.broadcasted_iota(jnp.int32, (LANES, H * DH), 0)
    c = lax.broadcasted_iota(jnp.int32, (LANES, H * DH), 1)
    return (r == first_lane + (c >> 7)).astype(F32)


def _chunk_tri(tm):
    r = lax.broadcasted_iota(jnp.int32, (tm, tm), 0)
    c = lax.broadcasted_iota(jnp.int32, (tm, tm), 1)
    return jnp.logical_and((r >> 6) == (c >> 6), r >= c).astype(F32)


def _gdn_gates(ba, alog_row, dtb_row, mask):
    g = -jnp.exp(alog_row) * _softplus(ba + dtb_row) * mask
    gc = _mmh(_chunk_tri(ba.shape[0]), g)
    beta = _sigmoid(ba) * mask
    return _mmh(gc, _select_matrix(H)), _mmh(beta, _select_matrix(0)), gc


def _gdn_qkv(c):
    a = _silu(c)

    def l2(scale):
        return lambda h, t: t * lax.rsqrt(jnp.sum(t * t, axis=-1, keepdims=True) + EPS) * scale
    q = _heads(l2(DH ** -0.5), a[:, :D])
    k = _heads(l2(1.0), a[:, D:2 * D])
    return q, k, a[:, 2 * D:]


def _conv_taps(xw, cws, tm):
    return sum(cws[i] * xw[5 + i:5 + i + tm] for i in range(CONV_K))


def _swap_pairs(t):
    n = t.shape[1]
    lane = lax.broadcasted_iota(jnp.int32, t.shape, 1)
    return jnp.where((lane & 1) == 0, pltpu.roll(t, n - 1, 1), pltpu.roll(t, 1, 1))


@jax.custom_vjp
def _unit_lower_inv(a):
    n = -a
    eye = (lax.broadcasted_iota(jnp.int32, a.shape, 1) == lax.broadcasted_iota(jnp.int32, a.shape, 2)).astype(F32)
    p = eye + n
    for _ in range(5):
        n = _bdot3(n, n, _NN)
        p = p + _bdot3(p, n, _NN)
    return p


def _inv_fwd(a):
    t = _unit_lower_inv(a)
    return t, t


def _inv_bwd(t, dt):
    x = _bdot3(t, dt, _TN)
    return (-_bdot3(x, t, _NT),)


_unit_lower_inv.defvjp(_inv_fwd, _inv_bwd)


@jax.custom_vjp
def _unit_lower_inv_known(a, t):
    return t


_unit_lower_inv_known.defvjp(lambda a, t: (t, t), lambda t, dt: (_inv_bwd(t, dt)[0], jnp.zeros_like(t)))


def _gdn_chunk(q, k, v, gc, bb, z, gr, s, norm_w, t_known=None):
    ri = lax.broadcasted_iota(jnp.int32, (H, CH, CH), 1)
    ci = lax.broadcasted_iota(jnp.int32, (H, CH, CH), 2)
    causal = ri >= ci
    gc1 = jnp.sum(gc, axis=2, keepdims=True) * (1.0 / LANES)
    diff = jnp.broadcast_to(gc1, (H, CH, CH)) - jnp.broadcast_to(gr, (H, CH, CH))
    decay = jnp.where(causal, jnp.exp(jnp.where(causal, diff, 0.0)), 0.0)
    kb = k * bb
    sc = _bmm(jnp.concatenate([q, kb], axis=1), k, _NT)
    qk = sc[:, :CH] * decay
    a = jnp.where(ri > ci, sc[:, CH:] * decay, 0.0)
    t = _unit_lower_inv(a) if t_known is None else _unit_lower_inv_known(a, t_known)
    eg = jnp.exp(gc)
    uw = _bmm(t, jnp.concatenate([v * bb, kb * eg], axis=2), _NN)
    g_last = gc[:, CH - 1:CH, :]
    ws = _bmm(jnp.concatenate([uw[:, :, DH:], q * eg], axis=1), s, _NN)
    v_new = uw[:, :, :DH] - ws[:, :CH]
    o = ws[:, CH:] + _bmm(qk, v_new, _NN)
    s_new = s * jnp.exp(g_last) + _bmm(k * jnp.exp(g_last - gc), v_new, _TN)
    y = o * lax.rsqrt(jnp.mean(o * o, axis=-1, keepdims=True) + EPS) * norm_w * _silu(z)
    return y, s_new, t


def _swap_pairs_heads(t):
    return _swap_pairs(t.reshape(t.shape[0] * CH, DH)).reshape(t.shape)


@jax.custom_vjp
def _rope(t, ct, st):
    return t * ct + _swap_pairs_heads(t) * st


def _rope_fwd(t, ct, st):
    return _rope(t, ct, st), (ct, st)


def _rope_bwd(res, d):
    ct, st = res
    return d * ct + _swap_pairs_heads(d * st), jnp.zeros_like(ct), jnp.zeros_like(st)


_rope.defvjp(_rope_fwd, _rope_bwd)


def _ret_chunk(q, k, v, rg, s, decay, xi, zeta, cd, ct, st, norm_w):
    q = _rope(q, ct, st)
    k = _rope(k, ct, st) * (DH ** -0.5)
    scores = _bmm(q, k, _NT) * decay
    o = _bmm(scores, v, _NN) + _bmm(q * xi, s, _NN)
    s_new = s * cd + _bmm(k * zeta, v, _TN)
    xc = o - jnp.mean(o, axis=-1, keepdims=True)
    var = jnp.mean(xc * xc, axis=-1, keepdims=True)
    y = _silu(rg) * (xc * lax.rsqrt(var + EPS) * norm_w)
    return y, s_new


def _head_blocks(ref):
    return jnp.stack([ref[:, h * DH:(h + 1) * DH] for h in range(H)])


def _head_rows(ref):
    return jnp.stack([ref[0, h:h + 1, :] for h in range(H)])


def _scan_fwd(name, chunk_fn, blocks, rowvecs, consts, nc, shared=(), params=(), out_dtype=F32, keep=None):
    n_blk, n_rv, n_c = len(blocks), len(rowvecs), len(consts) + len(shared) + len(params)

    def body(*refs):
        blk = refs[:n_blk]
        rvs = refs[n_blk:n_blk + n_rv]
        cst = refs[n_blk + n_rv:n_blk + n_rv + n_c]
        o_ref, st_ref = refs[n_blk + n_rv + n_c:n_blk + n_rv + n_c + 2]
        s_scr = refs[-1]

        @pl.when(pl.program_id(0) == 0)
        def _():
            s_scr[...] = jnp.zeros_like(s_scr)

        s = s_scr[...]
        st_ref[0] = s
        res = chunk_fn(*[_head_blocks(r) for r in blk], *[_head_rows(r) for r in rvs], s, *[r[...] for r in cst])
        for h in range(H):
            o_ref[:, h * DH:(h + 1) * DH] = res[0][h].astype(o_ref.dtype)
        s_scr[...] = res[1]
        if keep is not None:
            refs[-2][0] = res[2]

    in_specs = [pl.BlockSpec((CH, H * DH), lambda c, f=f: (c, f)) for _, f in blocks]
    in_specs += [pl.BlockSpec((1, H, CH), lambda c: (c, 0, 0)) for _ in rowvecs]
    in_specs += [pl.BlockSpec(a.shape, lambda c: (0, 0, 0)) for a in consts]
    in_specs += [pl.BlockSpec((CH, a.shape[1]), lambda c: (c, 0)) for a in shared]
    in_specs += [pl.BlockSpec(a.shape, lambda c, nd=a.ndim: (0,) * nd) for a in params]
    out_specs = [pl.BlockSpec((CH, H * DH), lambda c: (c, 0)), pl.BlockSpec((1, H, DH, DH), lambda c: (c, 0, 0, 0))]
    out_shape = [jax.ShapeDtypeStruct((nc * CH, H * DH), out_dtype), jax.ShapeDtypeStruct((nc, H, DH, DH), F32)]
    if keep is not None:
        out_specs.append(pl.BlockSpec((1, H) + keep, lambda c: (c, 0, 0, 0)))
        out_shape.append(jax.ShapeDtypeStruct((nc, H) + keep, F32))
    return pl.pallas_call(
        body, name=name, grid=(nc,), in_specs=in_specs, out_specs=out_specs, out_shape=out_shape,
        scratch_shapes=[pltpu.VMEM((H, DH, DH), F32)],
        compiler_params=pltpu.CompilerParams(dimension_semantics=("arbitrary",), vmem_limit_bytes=VMEM_LIMIT),
    )(*[a for a, _ in blocks], *rowvecs, *consts, *shared, *params)


def _scan_bwd(name, chunk_fn, blocks, rowvecs, consts, states, do, into, nc, shared=(), params=(), kept=None):
    n_blk, n_rv, n_c, n_p = len(blocks), len(rowvecs), len(consts) + len(shared), len(params)
    n_kept = 0 if kept is None else 1
    packed = list(into[2]) if into else []
    fresh = [k for k in range(n_blk) if k not in packed]
    n_bo = len(fresh) + (1 if into else 0)

    def body(*refs):
        blk = refs[:n_blk]
        rvs = refs[n_blk:n_blk + n_rv]
        cst = refs[n_blk + n_rv:n_blk + n_rv + n_c]
        par = refs[n_blk + n_rv + n_c:n_blk + n_rv + n_c + n_p]
        st_ref, do_ref = refs[n_blk + n_rv + n_c + n_p:n_blk + n_rv + n_c + n_p + 2]
        n_in = n_blk + n_rv + n_c + n_p + 2 + n_kept + (1 if into else 0)
        known = [refs[n_blk + n_rv + n_c + n_p + 2][0]] if n_kept else []
        o_refs = refs[n_in:n_in + n_bo]
        rv_refs = refs[n_in + n_bo:n_in + n_bo + n_rv]
        p_refs = refs[n_in + n_bo + n_rv:n_in + n_bo + n_rv + n_p]
        ds_scr = refs[n_in + n_bo + n_rv + n_p]

        @pl.when(pl.program_id(0) == 0)
        def _():
            ds_scr[...] = jnp.zeros_like(ds_scr)
            for r in p_refs:
                r[...] = jnp.zeros_like(r)

        cv = [r[...] for r in cst]
        n_d = n_blk + n_rv + 1
        _, vjp = jax.vjp(lambda *a: chunk_fn(*a[:n_d], *cv, *a[n_d:], *known)[:2], *[_head_blocks(r) for r in blk],
                         *[_head_rows(r) for r in rvs], st_ref[0], *[r[...] for r in par])
        grads = vjp((_head_blocks(do_ref), ds_scr[...]))
        for h in range(H):
            for pos, k in enumerate(fresh):
                o_refs[pos][:, h * DH:(h + 1) * DH] = grads[k][h]
            for pos, k in enumerate(packed):
                col = pos * H * DH + h * DH
                o_refs[-1][:, col:col + DH] = grads[k][h].astype(o_refs[-1].dtype)
            for r, g in zip(rv_refs, grads[n_blk:n_blk + n_rv]):
                r[0, h:h + 1, :] = g[h]
        ds_scr[...] = grads[n_blk + n_rv]
        for r, g in zip(p_refs, grads[n_d:]):
            r[...] += g

    rc = lambda c: nc - 1 - c
    in_specs = [pl.BlockSpec((CH, H * DH), lambda c, f=f: (rc(c), f)) for _, f in blocks]
    in_specs += [pl.BlockSpec((1, H, CH), lambda c: (rc(c), 0, 0)) for _ in rowvecs]
    in_specs += [pl.BlockSpec(a.shape, lambda c: (0, 0, 0)) for a in consts]
    in_specs += [pl.BlockSpec((CH, a.shape[1]), lambda c: (rc(c), 0)) for a in shared]
    in_specs += [pl.BlockSpec(a.shape, lambda c, nd=a.ndim: (0,) * nd) for a in params]
    in_specs += [pl.BlockSpec((1, H, DH, DH), lambda c: (rc(c), 0, 0, 0)), pl.BlockSpec((CH, H * DH), lambda c: (rc(c), 0))]
    ins = [a for a, _ in blocks] + list(rowvecs) + list(consts) + list(shared) + list(params) + [states, do]
    if n_kept:
        in_specs.append(pl.BlockSpec((1,) + kept.shape[1:], lambda c: (rc(c), 0, 0, 0)))
        ins.append(kept)
    out_specs, out_shapes, aliases = [], [], {}
    for _ in fresh:
        out_specs.append(pl.BlockSpec((CH, H * DH), lambda c: (rc(c), 0)))
        out_shapes.append(jax.ShapeDtypeStruct((nc * CH, H * DH), F32))
    if into:
        arr, f, _ = into
        in_specs.append(pl.BlockSpec(memory_space=pl.ANY))
        aliases[len(ins)] = len(fresh)
        ins.append(arr)
        out_specs.append(pl.BlockSpec((CH, len(packed) * H * DH), lambda c: (rc(c), f)))
        out_shapes.append(jax.ShapeDtypeStruct(arr.shape, arr.dtype))
    for _ in rowvecs:
        out_specs.append(pl.BlockSpec((1, H, CH), lambda c: (rc(c), 0, 0)))
        out_shapes.append(jax.ShapeDtypeStruct((nc, H, CH), F32))
    for a in params:
        out_specs.append(pl.BlockSpec(a.shape, lambda c, nd=a.ndim: (0,) * nd))
        out_shapes.append(jax.ShapeDtypeStruct(a.shape, F32))
    return pl.pallas_call(
        body, name=name, grid=(nc,), in_specs=in_specs, out_specs=out_specs, out_shape=out_shapes,
        input_output_aliases=aliases, scratch_shapes=[pltpu.VMEM((H, DH, DH), F32)],
        compiler_params=pltpu.CompilerParams(dimension_semantics=("arbitrary",), vmem_limit_bytes=VMEM_LIMIT),
    )(*ins)


def _gdn_pre_specs(p, cws, alog_row, dtb_row, tm, pos):
    sub = tm // 8
    rows = [pl.BlockSpec((tm, 3 * D), lambda i: (pos(i), 0)),
            pl.BlockSpec((8, 3 * D), lambda i: (jnp.maximum(pos(i) * sub - 1, 0), 0)),
            pl.BlockSpec((tm, LANES), lambda i: (pos(i), OFF_BA // LANES))]
    pars = [pl.BlockSpec(a.shape, lambda i: (0, 0)) for a in (*cws, alog_row, dtb_row)]
    return rows + pars, [p, p, p, *cws, alog_row, dtb_row]


def gdn_pre_fwd(p, cws, alog_row, dtb_row, lp, tm):
    def body(x_ref, prev_ref, ba_ref, c0, c1, c2, c3, al_ref, dt_ref, q_ref, k_ref, v_ref, g_ref, b_ref, gc_ref):
        i = pl.program_id(0)
        prev = jnp.where(i > 0, prev_ref[...], 0.0)
        xw = jnp.concatenate([prev, x_ref[...]], axis=0)
        c = _conv_taps(xw, [c0[...], c1[...], c2[...], c3[...]], tm)
        q, k, v = _gdn_qkv(c)
        mask = (_row_ids(i, tm) >= PAD).astype(F32)
        g, b, gc = _gdn_gates(ba_ref[...], al_ref[...], dt_ref[...], mask)
        q_ref[...] = q
        k_ref[...] = k
        v_ref[...] = v
        g_ref[...] = g
        b_ref[...] = b
        gc_ref[...] = gc

    in_specs, ins = _gdn_pre_specs(p, cws, alog_row, dtb_row, tm, lambda i: i)
    o_spec = pl.BlockSpec((tm, D), lambda i: (i, 0))
    return pl.pallas_call(
        body, name="gdn_pre_fwd", grid=(lp // tm,), in_specs=in_specs,
        out_specs=[o_spec] * 5 + [pl.BlockSpec((tm, LANES), lambda i: (i, 0))],
        out_shape=[jax.ShapeDtypeStruct((lp, D), F32)] * 5 + [jax.ShapeDtypeStruct((lp, LANES), F32)],
        compiler_params=pltpu.CompilerParams(dimension_semantics=("parallel",), vmem_limit_bytes=VMEM_LIMIT),
    )(*ins)


def gdn_pre_bwd(p, cws, alog_row, dtb_row, dq, dk, dv, dg, db, dgc, dp, lp, tm):
    nt = lp // tm
    pos = lambda i: nt - 1 - i

    def body(x_ref, prev_ref, ba_ref, c0, c1, c2, c3, al_ref, dt_ref, dq_ref, dk_ref, dv_ref, dg_ref, db_ref, dgc_ref,
             dp_any, dx_ref, dba_ref, dcw_ref, dpar_ref, carry):
        i = pl.program_id(0)
        t = pos(i)

        @pl.when(i == 0)
        def _():
            carry[...] = jnp.zeros_like(carry)
            dcw_ref[...] = jnp.zeros_like(dcw_ref)
            dpar_ref[...] = jnp.zeros_like(dpar_ref)

        cws_v = [c0[...], c1[...], c2[...], c3[...]]
        prev = jnp.where(t > 0, prev_ref[...], 0.0)
        xw = jnp.concatenate([prev, x_ref[...]], axis=0)
        xs = [xw[5 + j:5 + j + tm] for j in range(CONV_K)]
        c = sum(cws_v[j] * xs[j] for j in range(CONV_K))
        _, vjp_qkv = jax.vjp(_gdn_qkv, c)
        (dc,) = vjp_qkv((dq_ref[...], dk_ref[...], dv_ref[...]))
        zeros8 = jnp.zeros((8, 3 * D), F32)
        dcp = jnp.concatenate([zeros8, dc, zeros8], axis=0)
        dxw = sum(cws_v[j] * dcp[3 - j:3 - j + tm + 8] for j in range(CONV_K))
        dx_ref[...] = jnp.concatenate([dxw[8:tm], dxw[tm:] + carry[...]], axis=0).astype(dx_ref.dtype)
        carry[...] = dxw[:8]
        for j in range(CONV_K):
            dcw_ref[j:j + 1, :] += jnp.sum(dc * xs[j], axis=0, keepdims=True)
        mask = (_row_ids(t, tm) >= PAD).astype(F32)
        _, vjp_g = jax.vjp(lambda ba, al, dt: _gdn_gates(ba, al, dt, mask), ba_ref[...], al_ref[...], dt_ref[...])
        dba, dal, ddt = vjp_g((dg_ref[...], db_ref[...], dgc_ref[...]))
        dba_ref[...] = dba
        dpar_ref[0:1, :] += dal
        dpar_ref[1:2, :] += ddt

    in_specs, ins = _gdn_pre_specs(p, cws, alog_row, dtb_row, tm, pos)
    g_spec = pl.BlockSpec((tm, D), lambda i: (pos(i), 0))
    s_spec = pl.BlockSpec((tm, LANES), lambda i: (pos(i), 0))
    in_specs += [g_spec] * 5 + [s_spec, pl.BlockSpec(memory_space=pl.ANY)]
    ins += [dq, dk, dv, dg, db, dgc, dp]
    return pl.pallas_call(
        body, name="gdn_pre_bwd", grid=(nt,), in_specs=in_specs,
        out_specs=[pl.BlockSpec((tm, 3 * D), lambda i: (pos(i), 0)), s_spec,
                   pl.BlockSpec((8, 3 * D), lambda i: (0, 0)), pl.BlockSpec((8, LANES), lambda i: (0, 0))],
        out_shape=[jax.ShapeDtypeStruct(dp.shape, dp.dtype), jax.ShapeDtypeStruct((lp, LANES), F32),
                   jax.ShapeDtypeStruct((8, 3 * D), F32), jax.ShapeDtypeStruct((8, LANES), F32)],
        input_output_aliases={len(ins) - 1: 0},
        scratch_shapes=[pltpu.VMEM((8, 3 * D), F32)],
        compiler_params=pltpu.CompilerParams(dimension_semantics=("arbitrary",), vmem_limit_bytes=VMEM_LIMIT),
    )(*ins)


def _me():
    return lax.axis_index("x"), lax.axis_index("y"), lax.axis_index("c")


def _any_specs(n):
    return [pl.BlockSpec(memory_space=pl.ANY)] * n


Exchange = collections.namedtuple("Exchange", "ins out_shapes scratch start finish")


def _dma_sems(*shape):
    return pltpu.SemaphoreType.DMA(shape)


def run_exchange(name, ex):
    n_i, n_o = len(ex.ins), len(ex.out_shapes)

    def body(*refs):
        parts = (refs[:n_i], refs[n_i:n_i + n_o], refs[n_i + n_o:])
        ex.start(*parts)
        ex.finish(*parts)

    return pl.pallas_call(body, name=name, out_shape=list(ex.out_shapes), in_specs=_any_specs(n_i), out_specs=_any_specs(n_o),
                          scratch_shapes=list(ex.scratch))(*ex.ins)


def gather_exchange(ws):
    n = len(ws)

    def copies(w_refs, o_refs, sems):
        send_sems, recv_sems, fsend_sems, frecv_sems, osend_sems, orecv_sems = sems
        x, y, c = _me()
        me = 2 * x + y
        chips = [(1 - x, y), (x, 1 - y), (1 - x, 1 - y)]

        def own(a):
            return pltpu.make_async_remote_copy(src_ref=w_refs[a], dst_ref=o_refs[a].at[me], send_sem=osend_sems.at[a],
                                                recv_sem=orecv_sems.at[a], device_id=(x, y, 1 - c), device_id_type=MESH)

        def rows(a, cc):
            rh = ws[a].shape[0] // 2
            return pl.ds(pl.multiple_of(cc * rh, 8), rh)

        def ici(a, j, slot):
            px, py = chips[j]
            return pltpu.make_async_remote_copy(
                src_ref=w_refs[a].at[rows(a, c)], dst_ref=o_refs[a].at[slot, rows(a, c)], send_sem=send_sems.at[a, j],
                recv_sem=recv_sems.at[a, j], device_id=(px, py, c), device_id_type=MESH)

        def d2d(a, j, cc):
            px, py = chips[j]
            blk = o_refs[a].at[2 * px + py, rows(a, cc)]
            return pltpu.make_async_remote_copy(src_ref=blk, dst_ref=blk, send_sem=fsend_sems.at[a, j],
                                                recv_sem=frecv_sems.at[a, j], device_id=(x, y, 1 - c), device_id_type=MESH)
        mine = [own(a) for a in range(n)]
        sends = [ici(a, j, me) for a in range(n) for j in range(3)]
        arrivals = [ici(a, j, 2 * px + py) for a in range(n) for j, (px, py) in enumerate(chips)]
        passes = [d2d(a, j, c) for a in range(n) for j in range(3)]
        passed_to_me = [d2d(a, j, 1 - c) for a in range(n) for j in range(3)]
        return mine, sends, arrivals, passes, passed_to_me

    def start(w_refs, o_refs, sems):
        mine, sends, _, _, _ = copies(w_refs, o_refs, sems)
        for cp in mine + sends:
            cp.start()

    def finish(w_refs, o_refs, sems):
        mine, sends, arrivals, passes, passed_to_me = copies(w_refs, o_refs, sems)
        for arrival, onward in zip(arrivals, passes):
            arrival.wait_recv()
            onward.start()
        for cp in passed_to_me + mine:
            cp.wait_recv()
        for cp in sends + passes + mine:
            cp.wait_send()

    return Exchange(list(ws), [jax.ShapeDtypeStruct((N_CHIPS,) + w.shape, w.dtype) for w in ws],
                    [_dma_sems(n, 3), _dma_sems(n, 3), _dma_sems(n, 3), _dma_sems(n, 3), _dma_sems(n), _dma_sems(n)], start, finish)


def _simple_exchange(ins, out_shapes, sem_shape, copies):
    def start(i_refs, o_refs, sems):
        for cp in copies(i_refs, o_refs, *sems):
            cp.start()

    def finish(i_refs, o_refs, sems):
        cps = copies(i_refs, o_refs, *sems)
        for cp in cps:
            cp.wait_recv()
        for cp in cps:
            cp.wait_send()

    return Exchange(list(ins), out_shapes, [_dma_sems(*sem_shape), _dma_sems(*sem_shape)], start, finish)


def sibling_halves_exchange(gs):
    def copies(g_refs, o_refs, send_sems, recv_sems):
        x, y, c = _me()
        cps = []
        for a in range(len(gs)):
            rh = gs[a].shape[1] // 2
            for q in range(N_CHIPS):
                cps.append(pltpu.make_async_remote_copy(
                    src_ref=g_refs[a].at[q, pl.ds(pl.multiple_of((1 - c) * rh, 8), rh)], dst_ref=o_refs[a].at[q],
                    send_sem=send_sems.at[a, q], recv_sem=recv_sems.at[a, q], device_id=(x, y, 1 - c), device_id_type=MESH))
        return cps

    return _simple_exchange(gs, [jax.ShapeDtypeStruct((N_CHIPS, g.shape[1] // 2, g.shape[2]), g.dtype) for g in gs],
                            (len(gs), N_CHIPS), copies)


def scatter_chips_exchange(css):
    def copies(c_refs, o_refs, send_sems, recv_sems):
        x, y, c = _me()
        chips = [(1 - x, y), (x, 1 - y), (1 - x, 1 - y)]
        return [pltpu.make_async_remote_copy(
            src_ref=c_refs[a].at[2 * px + py], dst_ref=o_refs[a].at[j], send_sem=send_sems.at[a, j],
            recv_sem=recv_sems.at[a, j], device_id=(px, py, c), device_id_type=MESH)
            for a in range(len(css)) for j, (px, py) in enumerate(chips)]

    return _simple_exchange(css, [jax.ShapeDtypeStruct((3,) + cs.shape[1:], cs.dtype) for cs in css], (len(css), 3), copies)


def sibling_swap(name, halves):
    n = len(halves)

    def body(*refs):
        h_refs, o_refs = refs[:n], refs[n:2 * n]
        send_sems, recv_sems = refs[2 * n:]
        x, y, c = _me()
        cps = [pltpu.make_async_remote_copy(src_ref=h_refs[a], dst_ref=o_refs[a], send_sem=send_sems.at[a],
                                            recv_sem=recv_sems.at[a], device_id=(x, y, 1 - c), device_id_type=MESH)
               for a in range(n)]
        for cp in cps:
            cp.start()
        for cp in cps:
            cp.wait_recv()
        for cp in cps:
            cp.wait_send()

    return pl.pallas_call(
        body, name=name, out_shape=[jax.ShapeDtypeStruct(h.shape, h.dtype) for h in halves],
        in_specs=_any_specs(n), out_specs=_any_specs(n),
        scratch_shapes=[pltpu.SemaphoreType.DMA((n,)), pltpu.SemaphoreType.DMA((n,))],
    )(*halves)


def allgather_all(name, s):
    def body(s_ref, out_ref, send_sems, recv_sems, local_sem):
        x, y, c = _me()
        peers = [(x ^ ((m >> 2) & 1), y ^ ((m >> 1) & 1), c ^ (m & 1)) for m in range(1, 8)]
        mine = pltpu.make_async_copy(s_ref, out_ref.at[4 * x + 2 * y + c], local_sem)
        mine.start()

        def copy(j, slot):
            return pltpu.make_async_remote_copy(src_ref=s_ref, dst_ref=out_ref.at[slot], send_sem=send_sems.at[j],
                                                recv_sem=recv_sems.at[j], device_id=peers[j], device_id_type=MESH)
        sends = [copy(j, 4 * x + 2 * y + c) for j in range(7)]
        for cp in sends:
            cp.start()
        for j, (px, py, pc) in enumerate(peers):
            copy(j, 4 * px + 2 * py + pc).wait_recv()
        for cp in sends:
            cp.wait_send()
        mine.wait()

    return pl.pallas_call(
        body, name=name, out_shape=jax.ShapeDtypeStruct((8,) + s.shape, s.dtype),
        in_specs=_any_specs(1), out_specs=pl.BlockSpec(memory_space=pl.ANY),
        scratch_shapes=[pltpu.SemaphoreType.DMA((7,)), pltpu.SemaphoreType.DMA((7,)), pltpu.SemaphoreType.DMA(())],
    )(s)


def sum_slots(name, parts):
    n = len(parts)
    R, W = parts[0][0].shape[1:]
    tm = _tile_rows(R, W)
    idx = jnp.stack([jnp.asarray(s, jnp.int32) for _, s in parts])

    def body(idx_ref, *refs):
        acc = refs[0][...].astype(F32)
        for r in refs[1:n]:
            acc = acc + r[...].astype(F32)
        refs[n][...] = acc

    grid_spec = pltpu.PrefetchScalarGridSpec(
        num_scalar_prefetch=1, grid=(R // tm,),
        in_specs=[pl.BlockSpec((None, tm, W), lambda i, idx, k=k: (idx[k], i, 0)) for k in range(n)],
        out_specs=pl.BlockSpec((tm, W), lambda i, idx: (i, 0)))
    return pl.pallas_call(body, name=name, grid_spec=grid_spec, out_shape=jax.ShapeDtypeStruct((R, W), F32),
                          compiler_params=pltpu.CompilerParams(dimension_semantics=("parallel",)))(idx, *[a for a, _ in parts])


def chip_sums(name, g, recv, c):
    _, R, W = g.shape
    rh = R // 2
    tm = _tile_rows(rh, W)
    g8 = g.reshape(2 * N_CHIPS, rh, W)
    idx = jnp.asarray(c, jnp.int32).reshape(1)

    def body(idx_ref, g_ref, r_ref, o_ref):
        o_ref[...] = (g_ref[...] + r_ref[...]).astype(o_ref.dtype)

    grid_spec = pltpu.PrefetchScalarGridSpec(
        num_scalar_prefetch=1, grid=(N_CHIPS, rh // tm),
        in_specs=[pl.BlockSpec((None, tm, W), lambda q, i, idx: (2 * q + idx[0], i, 0)),
                  pl.BlockSpec((None, tm, W), lambda q, i, idx: (q, i, 0))],
        out_specs=pl.BlockSpec((None, tm, W), lambda q, i, idx: (q, i, 0)))
    return pl.pallas_call(body, name=name, grid_spec=grid_spec, out_shape=jax.ShapeDtypeStruct((N_CHIPS, rh, W), BF16),
                          compiler_params=pltpu.CompilerParams(dimension_semantics=("parallel", "parallel")))(idx, g8, recv)


def _adamw_update(w, g, m, v):
    m = ADAM_B1 * m + (1.0 - ADAM_B1) * g
    v = ADAM_B2 * v + (1.0 - ADAM_B2) * (g * g)
    m_hat = m / (1.0 - ADAM_B1 ** ADAM_STEP)
    v_hat = v / (1.0 - ADAM_B2 ** ADAM_STEP)
    return -ADAM_LR * (m_hat / (jnp.sqrt(v_hat) + ADAM_EPS) + ADAM_WD * w), m, v


def adamw(name, w, g, m, v):
    R, W = w.shape
    return rowwise(name, lambda i, w, g, m, v: _adamw_update(w, g, m, v), [(a, W, 0) for a in (w, g, m, v)], [],
                   [("new", W, F32)] * 3, n_rows=R, tm=_tile_rows(R, W, 2 ** 20))


def adamw_halves(name, w, g_mine, g_other, m, v, c):
    R, W = w.shape
    rh = R // 2
    tm = _tile_rows(rh, W, 2 ** 20)
    nh = rh // tm
    idx = jnp.asarray(c, jnp.int32).reshape(1)

    def body(idx_ref, w_ref, ga_ref, gb_ref, m_ref, v_ref, g_ref, d_ref, mo_ref, vo_ref):
        mine = (pl.program_id(0) // nh) == idx_ref[0]
        g = jnp.where(mine, ga_ref[...], gb_ref[...])
        d, m, v = _adamw_update(w_ref[...], g, m_ref[...], v_ref[...])
        g_ref[...] = g
        d_ref[...] = d
        mo_ref[...] = m
        vo_ref[...] = v

    full = pl.BlockSpec((tm, W), lambda i, idx: (i, 0))
    half = pl.BlockSpec((tm, W), lambda i, idx: (i % nh, 0))
    grid_spec = pltpu.PrefetchScalarGridSpec(num_scalar_prefetch=1, grid=(R // tm,), in_specs=[full, half, half, full, full],
                                             out_specs=[full] * 4)
    return pl.pallas_call(body, name=name, grid_spec=grid_spec, out_shape=[jax.ShapeDtypeStruct((R, W), F32)] * 4,
                          compiler_params=pltpu.CompilerParams(dimension_semantics=("parallel",)))(idx, w, g_mine, g_other, m, v)


W_IN_BLOCK = D_PROJ // N_CHIPS
BA_REF = OFF_RQ


def _w_in_padded_from_blocks(g):
    lo, hi = BA_REF - W_IN_BLOCK, BA_REF + 16 - W_IN_BLOCK
    return jnp.concatenate([g[0], g[1][:, :lo], g[1][:, hi:], g[2], g[3], g[1][:, lo:hi],
                            jnp.zeros((D, PW - D_PROJ), g.dtype)], axis=1)


def _w_in_grad_blocks(g):
    s = W_IN_BLOCK
    second = jnp.concatenate([g[:, s:BA_REF], g[:, OFF_BA:OFF_BA + 16], g[:, BA_REF:2 * s - 16]], axis=1)
    return jnp.stack([g[:, :s], second, g[:, 2 * s - 16:3 * s - 16], g[:, 3 * s - 16:4 * s - 16]])


def _ret_consts():
    f = F32
    log_gamma = jnp.log1p(-jnp.exp2(-5.0 - jnp.arange(H, dtype=f)))
    pos = jnp.arange(CH, dtype=f)
    causal = jnp.tril(jnp.ones((CH, CH), dtype=bool))
    diff = pos[:, None] - pos[None, :]
    decay = jnp.where(causal, jnp.exp(jnp.where(causal, diff, 0.0) * log_gamma[:, None, None]), 0.0)
    xi = jnp.broadcast_to(jnp.exp((pos + 1.0) * log_gamma[:, None])[:, :, None], (H, CH, DH))
    zeta = jnp.broadcast_to(jnp.exp((CH - 1.0 - pos) * log_gamma[:, None])[:, :, None], (H, CH, DH))
    cd = jnp.broadcast_to(jnp.exp(CH * log_gamma)[:, None, None], (H, 1, DH))
    return decay, xi, zeta, cd


def _rope_tables(lp):
    pos = jnp.arange(lp, dtype=F32) - float(PAD)
    inv = 1.0 / (ROPE_BASE ** jnp.linspace(0.0, 1.0, DH // 2, dtype=F32))
    ang = pos[:, None] * inv[None, :]
    cos, sin = jnp.cos(ang), jnp.sin(ang)
    ct = jnp.repeat(cos, 2, axis=1)
    st = jnp.stack([-sin, sin], axis=-1).reshape(lp, DH)
    return ct, st


def local_step(x, tgt, w, small, hooks=None):
    hooks = hooks or {}
    seq = x.shape[0]
    lp = HEAD_ROWS + seq
    nc = lp // CH
    tm = _pick(lp, 192, CH)
    row = functools.partial(rowwise, n_rows=lp, tm=tm)
    gw = {}

    def mm(name, *args, **kw):
        if name not in hooks:
            return matmul(name, *args, **kw)
        make_exchange, take = hooks[name]
        out, results = matmul(name, *args, side=make_exchange(gw), **kw)
        take(results)
        return out

    h0 = jnp.concatenate([jnp.zeros((PAD, D), F32), w["meta_tokens"], x], axis=0)
    tgt_p = jnp.concatenate([jnp.zeros((HEAD_ROWS, D), F32), tgt], axis=0)

    def ffn_fwd(tag, h, wn, n=None, next_norm=None):
        if n is None:
            n = row(f"{tag}_norm", lambda i, h, wn: _rms(h, wn), [(h, D, 0)], [wn], [("new", D, BF16)])[0]
        gu = mm(f"{tag}_up", n, w[f"{tag}_w_in"], "nn", b_split=True, ti_cap=1376)
        mid = row(f"{tag}_act", lambda i, gu: _swiglu(gu), [(gu, 2 * DFF, 0)], [], [("new", DFF, BF16)])[0]
        out = mm(f"{tag}_down", mid, w[f"{tag}_w_out"], "nn", resid=h, alpha=0.5, tj_cap=1024, tr_cap=2816, norm_w=next_norm)
        out, n_next = out if next_norm is not None else (out, None)
        return out, (h, n, gu, mid), n_next

    def ffn_bwd(tag, dh, saved, wn):
        h, n, gu, mid = saved
        w_in_full, w_out = w[f"{tag}_w_in_full"], w[f"{tag}_w_out"]
        dmid = mm(f"{tag}_dmid", dh, w_out, "nt", alpha=0.5, ti_cap=1376, tj_cap=1408)
        dw_out = mm(f"{tag}_dwout", mid, dh, "tn", alpha=0.5, ti_cap=1408, tj_cap=1024, tr_cap=1376)

        def act_bwd(i, gu, dmid):
            _, vjp = jax.vjp(_swiglu, gu)
            return vjp(dmid)[0]
        dgu = row(f"{tag}_dact", act_bwd, [(gu, 2 * DFF, 0), (dmid, DFF, 0)], [], [("new", 2 * DFF, BF16)])[0]
        dn = mm(f"{tag}_dn", dgu, w_in_full, "nt", tj_cap=1024, tr_cap=2 * DFF)
        dw_in = mm(f"{tag}_dwin", n, dgu, "tn", ti_cap=1024, tr_cap=2752, o_split=True)
        dh_in, dwn = norm_bwd(f"{tag}_dnorm", h, wn, dn, dh)
        return dh_in, dw_in, dw_out, dwn

    def norm_bwd(name, h, wn, dn, dres):
        def fn(i, h, dn, dres, wn):
            _, vjp = jax.vjp(_rms, h, wn)
            dh, dw = vjp(dn)
            return dres + dh, dw
        return row(name, fn, [(h, D, 0), (dn, D, 0), (dres, D, 0)], [wn], [("new", D, F32)], [(1, D)])

    h1, ffn1_saved, n2 = ffn_fwd("ffn1", h0, small["ffn1_norm"], next_norm=small["mix_norm"])
    p = mm("mix_proj", n2, w["w_in_p"], "nn", ti_cap=1376, tj_cap=1152)

    cws = [w["gdn_conv_w"][j:j + 1] for j in range(CONV_K)]
    alog_row = jnp.zeros((1, LANES), F32).at[:, H:2 * H].set(small["gdn_a_log"])
    dtb_row = jnp.zeros((1, LANES), F32).at[:, H:2 * H].set(small["gdn_dt_bias"])
    q, k, v, gcb, bb, gc = gdn_pre_fwd(p, cws, alog_row, dtb_row, lp, tm)
    gc_rows = gc[:, H:2 * H].reshape(nc, CH, H).transpose(0, 2, 1)
    gdn_blocks = [(q, 0), (k, 0), (v, 0), (gcb, 0), (bb, 0), (p, OFF_Z // D)]
    y_a, gdn_states, gdn_inv = _scan_fwd("gdn_scan_fwd", _gdn_chunk, gdn_blocks, [gc_rows], [], nc,
                                         params=[small["gdn_out_norm"]], out_dtype=BF16, keep=(CH, CH))

    ct, st = _rope_tables(lp)
    ret_consts = list(_ret_consts())
    ret_blocks = [(p, OFF_RQ // D), (p, OFF_RK // D), (p, OFF_RV // D), (p, OFF_RG // D)]
    ret_w = small["ret_out_norm"].reshape(H, 1, DH)
    y_b, ret_states = _scan_fwd("ret_scan_fwd", _ret_chunk, ret_blocks, [], ret_consts, nc, shared=[ct, st],
                                params=[ret_w], out_dtype=BF16)

    br_a = matmul("branch_gdn", y_a, w["w_branch_gdn"], "nn", ti_cap=1376, tj_cap=1024)
    br_b = matmul("branch_ret", y_b, w["w_branch_ret"], "nn", ti_cap=1376, tj_cap=1024)
    merged = row("merge", lambda i, a, b, ga, gb_: _merge(a, b, ga, gb_),
                 [(br_a, D, 0), (br_b, D, 0), (p, D, OFF_GA // D), (p, D, OFF_GB // D)], [], [("new", D, BF16)])[0]
    h2, n3 = matmul("mix_out", merged, w["w_out"], "nn", resid=h1, ti_cap=1376, tj_cap=1024, norm_w=small["ffn2_norm"])
    h3, ffn2_saved, _ = ffn_fwd("ffn2", h2, small["ffn2_norm"], n=n3)

    def head(i, h, t, wn):
        mask = (_row_ids(i, tm) >= HEAD_ROWS).astype(F32)
        y, vjp = jax.vjp(_rms, h, wn)
        err = (y - t) * mask
        dh, dw = vjp(err * (1.0 / D))
        return dh, dw, jnp.sum(err * err, keepdims=True).reshape(1, 1) * (0.5 / D) * jnp.ones((1, LANES), F32)
    dh3, d_final, loss_row = row("loss_head", head, [(h3, D, 0), (tgt_p, D, 0)], [small["final_norm"]], [("new", D, F32)],
                                 [(1, D), (1, LANES)])

    gs = {"final_norm": d_final}
    dh2, gw["ffn2_w_in"], gw["ffn2_w_out"], gs["ffn2_norm"] = ffn_bwd("ffn2", dh3, ffn2_saved, small["ffn2_norm"])
    dmerged = matmul("mix_out_dx", dh2, w["w_out"], "nt", ti_cap=1376, tj_cap=1024)
    gw["w_out"] = matmul("mix_out_dw", merged, dh2, "tn", ti_cap=1024, tj_cap=1024, tr_cap=2752)
    dp = lax.empty((lp, PW), BF16)

    def merge_bwd(i, dm, a, b, ga, gb_):
        _, vjp = jax.vjp(_merge, a, b, ga, gb_)
        da, db, dga, dgb = vjp(dm)
        return da, db, jnp.concatenate([dga, dgb], axis=1)
    da, db_, dp = row("merge_bwd", merge_bwd,
                      [(dmerged, D, 0), (br_a, D, 0), (br_b, D, 0), (p, D, OFF_GA // D), (p, D, OFF_GB // D)], [],
                      [("new", D, BF16), ("new", D, BF16), ("into", dp, 2 * D, OFF_GA // (2 * D))])
    dy_a = matmul("branch_gdn_dx", da, w["w_branch_gdn"], "nt", ti_cap=1376, tj_cap=1024)
    gw["w_branch_gdn"] = matmul("branch_gdn_dw", y_a, da, "tn", ti_cap=1024, tj_cap=1024, tr_cap=2752)
    dy_b = matmul("branch_ret_dx", db_, w["w_branch_ret"], "nt", ti_cap=1376, tj_cap=1024)
    gw["w_branch_ret"] = matmul("branch_ret_dw", y_b, db_, "tn", ti_cap=1024, tj_cap=1024, tr_cap=2752)

    dp, d_ret_w = _scan_bwd("ret_scan_bwd", _ret_chunk, ret_blocks, [], ret_consts, ret_states, dy_b,
                            (dp, OFF_RQ // (4 * D), [0, 1, 2, 3]), nc, shared=[ct, st], params=[ret_w])
    gs["ret_out_norm"] = d_ret_w.reshape(1, D)
    dq, dk, dv, dgcb, dbb, dp, dgc_rows, gs["gdn_out_norm"] = _scan_bwd(
        "gdn_scan_bwd", _gdn_chunk, gdn_blocks, [gc_rows], [], gdn_states, dy_a, (dp, OFF_Z // D, [5]), nc,
        params=[small["gdn_out_norm"]], kept=gdn_inv)
    dgc = jnp.pad(dgc_rows.transpose(0, 2, 1).reshape(lp, H), ((0, 0), (H, LANES - 2 * H)))
    dp, dba, dcw, dgate = gdn_pre_bwd(p, cws, alog_row, dtb_row, dq, dk, dv, dgcb, dbb, dgc, dp, lp, tm)
    dp = row("dp_ba", lambda i, t: t, [(dba, LANES, 0)], [], [("into", dp, LANES, OFF_BA // LANES)])[0]
    gw["gdn_conv_w"] = dcw[:CONV_K]
    gs["gdn_a_log"] = dgate[0:1, H:2 * H]
    gs["gdn_dt_bias"] = dgate[1:2, H:2 * H]

    dn2 = matmul("mix_proj_dx", dp, w["w_in_p"], "nt", tj_cap=1024, tr_cap=3456)
    gw["w_in_p"] = matmul("mix_proj_dw", n2, dp, "tn", ti_cap=1024, tj_cap=1152, tr_cap=2752)
    dh1, gs["mix_norm"] = norm_bwd("mix_dnorm", h1, small["mix_norm"], dn2, dh2)
    dh0, gw["ffn1_w_in"], gw["ffn1_w_out"], gs["ffn1_norm"] = ffn_bwd("ffn1", dh1, ffn1_saved, small["ffn1_norm"])
    gw["meta_tokens"] = dh0[PAD:HEAD_ROWS]
    return loss_row, dh0[HEAD_ROWS:], gw, gs


BIG = ("ffn1_w_in", "ffn1_w_out", "w_in", "w_branch_gdn", "w_branch_ret", "w_out", "ffn2_w_in", "ffn2_w_out")
COL_SHARDED = ("ffn1_w_in", "w_in", "ffn2_w_in")
WITH_FFN1_UP = ("ffn1_w_out", "w_in")
WITH_MIX_PROJ = ("w_branch_gdn", "w_branch_ret", "w_out", "ffn2_w_in", "ffn2_w_out")
EARLY_GRADS = ("ffn2_w_in", "ffn2_w_out", "w_in", "w_branch_gdn", "w_branch_ret", "w_out")
LATE_GRADS = ("ffn1_w_in", "ffn1_w_out")
SMALL = ("ffn1_norm", "mix_norm", "ret_out_norm", "ffn2_norm", "final_norm", "gdn_out_norm", "gdn_a_log", "gdn_dt_bias")
WEIGHTS = ("meta_tokens", "ffn1_norm", "ffn1_w_in", "ffn1_w_out", "mix_norm", "w_in", "gdn_conv_w", "gdn_a_log", "gdn_dt_bias",
           "gdn_out_norm", "ret_out_norm", "w_branch_gdn", "w_branch_ret", "w_out", "ffn2_norm", "ffn2_w_in", "ffn2_w_out",
           "final_norm")
LOSS_ROW = 6
CONV_ROW0, META_ROW0, SMALL_ROWS = 8, 24, 40


def pack_small(vals):
    rows = [vals[n].reshape(1, D) for n in SMALL[:5]]
    r5 = jnp.concatenate([vals["gdn_out_norm"].reshape(1, DH), vals["gdn_a_log"].reshape(1, H), vals["gdn_dt_bias"].reshape(1, H),
                          jnp.zeros((1, D - DH - 2 * H), F32)], axis=1)
    return jnp.concatenate(rows + [r5, jnp.zeros((2, D), F32)], axis=0)


def unpack_small(packed, shapes):
    out = {n: packed[j].reshape(shapes[n]) for j, n in enumerate(SMALL[:5])}
    out["gdn_out_norm"] = packed[5, :DH].reshape(shapes["gdn_out_norm"])
    out["gdn_a_log"] = packed[5, DH:DH + H].reshape(shapes["gdn_a_log"])
    out["gdn_dt_bias"] = packed[5, DH + H:DH + 2 * H].reshape(shapes["gdn_dt_bias"])
    return out


def kernel(x, meta_tokens, ffn1_norm, ffn1_w_in, ffn1_w_out, mix_norm, w_in, gdn_conv_w, gdn_a_log, gdn_dt_bias, gdn_out_norm, ret_out_norm, w_branch_gdn, w_branch_ret, w_out, ffn2_norm, ffn2_w_in, ffn2_w_out, final_norm, loss_target, m_meta_tokens, m_ffn1_norm, m_ffn1_w_in, m_ffn1_w_out, m_mix_norm, m_w_in, m_gdn_conv_w, m_gdn_a_log, m_gdn_dt_bias, m_gdn_out_norm, m_ret_out_norm, m_w_branch_gdn, m_w_branch_ret, m_w_out, m_ffn2_norm, m_ffn2_w_in, m_ffn2_w_out, m_final_norm, v_meta_tokens, v_ffn1_norm, v_ffn1_w_in, v_ffn1_w_out, v_mix_norm, v_w_in, v_gdn_conv_w, v_gdn_a_log, v_gdn_dt_bias, v_gdn_out_norm, v_ret_out_norm, v_w_branch_gdn, v_w_branch_ret, v_w_out, v_ffn2_norm, v_ffn2_w_in, v_ffn2_w_out, v_final_norm):
    a = dict(locals())
    wts = {n: a[n] for n in WEIGHTS}
    mom_m = {n: a["m_" + n] for n in WEIGHTS}
    mom_v = {n: a["v_" + n] for n in WEIGHTS}
    shapes = {n: wts[n].shape for n in WEIGHTS}
    flat = lambda t: t.reshape(t.shape[-2:])
    c = lax.axis_index("c")
    chip = 2 * lax.axis_index("x") + lax.axis_index("y")

    exact = jnp.zeros((16, D), F32).at[0:3].set(wts["gdn_conv_w"].reshape(3, D)).at[3:7].set(wts["meta_tokens"].reshape(4, D))
    bf16_block = lambda n: flat(wts[n]).astype(BF16)
    w = {}

    def take_weights(names):
        def take(gathered):
            for n, g in zip(names, gathered):
                if n == "w_in":
                    w["w_in_p"] = _w_in_padded_from_blocks(g)
                elif n in COL_SHARDED:
                    w[n], w[n + "_full"] = g, jnp.concatenate([g[q] for q in range(N_CHIPS)], axis=1)
                else:
                    w[n] = g.reshape(N_CHIPS * g.shape[1], D)
        return take

    first = run_exchange("gather_first", gather_exchange([bf16_block("ffn1_w_in"), exact]))
    take_weights(["ffn1_w_in"])(first[:1])
    exact = first[1]
    w["gdn_conv_w"] = jnp.concatenate([exact[q, 0:3].reshape(CONV_K, 3 * D // 4) for q in range(N_CHIPS)], axis=1)
    w["meta_tokens"] = jnp.concatenate([exact[q, 3:7].reshape(N_META, D // 4) for q in range(N_CHIPS)], axis=1)
    small = {n: wts[n].reshape(1, -1) for n in SMALL}
    hooks = {"ffn1_up": (lambda gw: gather_exchange([bf16_block(n) for n in WITH_FFN1_UP]), take_weights(WITH_FFN1_UP)),
             "mix_proj": (lambda gw: gather_exchange([bf16_block(n) for n in WITH_MIX_PROJ]), take_weights(WITH_MIX_PROJ))}

    def blocks(gw, n):
        if n == "w_in":
            return _w_in_grad_blocks(gw["w_in_p"])
        return gw[n] if n in COL_SHARDED else gw[n].reshape(N_CHIPS, gw[n].shape[0] // N_CHIPS, D)

    chip_sum, from_chips = {}, {}

    def swap_early(gw):
        chip_sum["g"] = [blocks(gw, n) for n in EARLY_GRADS]
        return sibling_halves_exchange(chip_sum["g"])

    def sum_early(from_sib):
        for n, g, r in zip(EARLY_GRADS, chip_sum.pop("g"), from_sib):
            chip_sum[n] = chip_sums(f"grads_chip_sum_{n}", g, r, c)

    def scatter_of(names):
        return (lambda gw: scatter_chips_exchange([chip_sum[n] for n in names])), (lambda got: from_chips.update(zip(names, got)))
    hooks["ffn1_dmid"] = (swap_early, sum_early)
    hooks["ffn1_dn"] = scatter_of(["w_in"])
    hooks["ffn1_dwin"] = scatter_of([n for n in EARLY_GRADS if n != "w_in"])

    loss_row, gx, gw, gs = local_step(x[0], loss_target[0], w, small, hooks)

    late = [blocks(gw, n) for n in LATE_GRADS]
    from_sib = run_exchange("grads_sibling_late", sibling_halves_exchange(late))
    for n, g, r in zip(LATE_GRADS, late, from_sib):
        chip_sum[n] = chip_sums(f"grads_chip_sum_{n}", g, r, c)
    from_chips.update(zip(LATE_GRADS, run_exchange("grads_scatter_late", scatter_chips_exchange([chip_sum[n] for n in LATE_GRADS]))))
    halves = [sum_slots(f"grads_sum_{n}", [(chip_sum[n], chip), (from_chips[n], 0), (from_chips[n], 1), (from_chips[n], 2)])
              for n in BIG]
    others = sibling_swap("grads_swap", halves)
    grads, delta, new_m, new_v = {}, {}, {}, {}
    for n, mine, other in zip(BIG, halves, others):
        res = adamw_halves(f"adamw_{n}", flat(wts[n]), mine, other, flat(mom_m[n]), flat(mom_v[n]), c)
        grads[n], delta[n], new_m[n], new_v[n] = (t.reshape(shapes[n]) for t in res)

    sm = jnp.concatenate([pack_small(gs).at[LOSS_ROW, :LANES].set(loss_row[0]),
                          gw["gdn_conv_w"].reshape(3 * CONV_K, D), jnp.zeros((META_ROW0 - CONV_ROW0 - 3 * CONV_K, D), F32),
                          gw["meta_tokens"]], axis=0)
    every = allgather_all("small_gather", sm)
    sm_sum = sum_slots("small_sum", [(every, s) for s in range(8)])
    d_s, m_s, v_s = adamw("adamw_small", pack_small(small), sm_sum[:8], pack_small({n: mom_m[n].reshape(1, -1) for n in SMALL}),
                          pack_small({n: mom_v[n].reshape(1, -1) for n in SMALL}))
    grads.update(unpack_small(sm_sum, shapes))
    delta.update(unpack_small(d_s, shapes))
    new_m.update(unpack_small(m_s, shapes))
    new_v.update(unpack_small(v_s, shapes))
    g_conv = lax.dynamic_slice_in_dim(sm_sum[CONV_ROW0:CONV_ROW0 + 3 * CONV_K].reshape(CONV_K, 3 * D), chip * (3 * D // 4), 3 * D // 4, 1)
    g_meta = lax.dynamic_slice_in_dim(sm_sum[META_ROW0:META_ROW0 + N_META], chip * (D // 4), D // 4, 1)
    for n, g in (("gdn_conv_w", g_conv), ("meta_tokens", g_meta)):
        d_, m_, v_ = adamw(f"adamw_{n}", flat(wts[n]), g, flat(mom_m[n]), flat(mom_v[n]))
        grads[n], delta[n], new_m[n], new_v[n] = (t.reshape(shapes[n]) for t in (g, d_, m_, v_))
    loss = sm_sum[LOSS_ROW, 0]

    return (loss, gx[None], *[grads[n] for n in WEIGHTS], *[delta[n] for n in WEIGHTS], *[new_m[n] for n in WEIGHTS],
            *[new_v[n] for n in WEIGHTS])
```

```python
import collections
import functools

import jax
import jax.numpy as jnp
from jax import lax
from jax.experimental import pallas as pl
from jax.experimental.pallas import tpu as pltpu

F32 = jnp.float32
BF16 = jnp.bfloat16
HI = lax.Precision.HIGHEST
MESH = pl.DeviceIdType.MESH

D = 1024
N_META = 16
PAD = 48
HEAD_ROWS = PAD + N_META
CH = 64
H = 8
DH = 128
DFF = 2816
CONV_K = 4
EPS = 1e-6
ROPE_BASE = 10000.0
LANES = 128
N_CHIPS = 4
VMEM_LIMIT = 56 * 2 ** 20

OFF_QKV, OFF_Z, OFF_RQ, OFF_RK, OFF_RV, OFF_RG, OFF_GA, OFF_GB, OFF_BA = 0, 3072, 4096, 5120, 6144, 7168, 8192, 9216, 10240
PW = 10368
D_PROJ = 10256

ADAM_LR, ADAM_B1, ADAM_B2, ADAM_EPS, ADAM_WD, ADAM_STEP = 0.001, 0.9, 0.999, 1e-08, 0.01, 10


def _pick(n, cap, mult):
    best = None
    for t in range(mult, min(n, cap) + 1, mult):
        if n % t == 0:
            best = t
    return best if best is not None else n


def _tile_rows(rows, cols, block_bytes=3 * 2 ** 19):
    return _pick(rows, max(8, block_bytes // (4 * cols)), 8)


def _dot(a, b, dims, prec=None):
    return lax.dot_general(a, b, (dims, ((), ())), precision=prec, preferred_element_type=F32)


def _mmh(a, b):
    return _dot(a, b, ((1,), (0,)), HI)


def _split(a):
    hi = a.astype(BF16)
    return hi, (a - hi.astype(F32)).astype(BF16)


_NN, _NT, _TN = ((2,), (1,)), ((2,), (2,)), ((1,), (1,))


def _bdot(a, b, dims):
    return lax.dot_general(a, b, (dims, ((0,), (0,))), preferred_element_type=F32)


def _bmm(a, b, dims):
    return _bdot(a.astype(BF16), b.astype(BF16), dims)


def _bdot3(a, b, dims):
    ah, al = _split(a)
    bh, bl = _split(b)
    if dims == _NN:
        m = a.shape[1]
        both = _bdot(jnp.concatenate([ah, al], axis=1), bh, dims)
        return both[:, :m] + (both[:, m:] + _bdot(ah, bl, dims))
    return _bdot(ah, bh, dims) + (_bdot(ah, bl, dims) + _bdot(al, bh, dims))


def matmul(name, a, b, mode, *, ti_cap=688, tj_cap=512, tr_cap=1408, b_split=False, o_split=False, pair=False,
           rows=(), pars=(), epilogue=None, outs=(F32,), accs=(), side=None):
    if mode == "nn":
        I, R = a.shape
        J = N_CHIPS * b.shape[2] if b_split else b.shape[1]
    elif mode == "nt":
        (I, R), J = a.shape, b.shape[0]
    else:
        (R, I), J = a.shape, b.shape[1]
    ti = _pick(I, ti_cap, 16) if mode != "tn" else _pick(I, ti_cap, LANES)
    tj = _pick(J, tj_cap, LANES)
    tr = _pick(R, tr_cap, LANES) if mode != "tn" else _pick(R, tr_cap, 16)
    half = N_CHIPS // 2
    if mode == "nn":
        a_spec = pl.BlockSpec((ti, tr), lambda i, j, r: (i, r))
        if b_split:
            tj = J // N_CHIPS
            b_spec = pl.BlockSpec((None, tr, tj), lambda i, j, r: (j, r, 0))
            b2_spec = pl.BlockSpec((None, tr, tj), lambda i, j, r: (j + half, r, 0))
        else:
            b_spec = pl.BlockSpec((tr, tj), lambda i, j, r: (r, j))
        dims = ((1,), (0,))
    elif mode == "nt":
        a_spec = pl.BlockSpec((ti, tr), lambda i, j, r: (i, r))
        b_spec = pl.BlockSpec((tj, tr), lambda i, j, r: (j, r))
        dims = ((1,), (1,))
    else:
        if o_split:
            tj = J // N_CHIPS
        a_spec = pl.BlockSpec((tr, ti), lambda i, j, r: (r, i))
        b_spec = pl.BlockSpec((tr, tj), lambda i, j, r: (r, j))
        dims = ((0,), (0,))
    assert not pair or (mode == "nn" and b_split), name
    j_out = J // 2 if pair else J
    nr = R // tr
    assert I % ti == 0 and j_out % tj == 0 and R % tr == 0, (name, I, J, R, ti, tj, tr)
    grid = (I // ti, j_out // tj, nr)
    if o_split:
        assert epilogue is None, name
        o_spec = pl.BlockSpec((None, ti, tj), lambda i, j, r: (j, i, 0))
        out_shapes = [jax.ShapeDtypeStruct((N_CHIPS, I, tj), outs[0])]
    else:
        o_spec = pl.BlockSpec((ti, tj), lambda i, j, r: (i, j))
        out_shapes = [jax.ShapeDtypeStruct((I, j_out), dt) for dt in outs]
    n_prod = 2 if pair else 1
    n_out, n_acc, n_rows, n_pars = len(out_shapes), len(accs), len(rows), len(pars)
    n_si = len(side.ins) if side else 0
    n_so = len(side.out_shapes) if side else 0
    n_in = 1 + n_prod + n_rows + n_pars + n_si
    n_steps = grid[0] * grid[1] * grid[2]

    def body(*refs):
        a_ref, b_refs = refs[0], refs[1:1 + n_prod]
        row_refs = refs[1 + n_prod:1 + n_prod + n_rows]
        par_refs = refs[1 + n_prod + n_rows:1 + n_prod + n_rows + n_pars]
        out_refs = refs[n_in:n_in + n_out]
        acc_refs = refs[n_in + n_out:n_in + n_out + n_acc]
        scr = refs[n_in + n_out + n_acc + n_so:]
        partial = scr[:n_prod] if nr > 1 else ()
        step = (pl.program_id(0) * grid[1] + pl.program_id(1)) * grid[2] + pl.program_id(2)
        if side:
            s_refs = (refs[n_in - n_si:n_in], refs[n_in + n_out + n_acc:n_in + n_out + n_acc + n_so], scr[len(partial):])

        if side or n_acc:
            @pl.when(step == 0)
            def _():
                if side:
                    side.start(*s_refs)
                for r in acc_refs:
                    r[...] = jnp.zeros_like(r)

        lhs = a_ref[...].astype(BF16)
        prods = [_dot(lhs, r[...].astype(BF16), dims) for r in b_refs]

        def finish(prods):
            res = epilogue(prods, *[r[...] for r in row_refs], *[r[...] for r in par_refs]) if epilogue else prods
            for r, v in zip(out_refs, res[:n_out]):
                r[...] = v.astype(r.dtype)
            for r, v in zip(acc_refs, res[n_out:]):
                r[...] += v

        if nr == 1:
            finish(prods)
        else:
            k = pl.program_id(2)

            @pl.when(k == 0)
            def _():
                for r, v in zip(partial, prods):
                    r[...] = v

            @pl.when(k > 0)
            def _():
                for r, v in zip(partial, prods):
                    r[...] += v

            @pl.when(k == nr - 1)
            def _():
                finish([r[...] for r in partial])

        if side:
            @pl.when(step == n_steps - 1)
            def _():
                side.finish(*s_refs)

    ins = [a, b] + ([b] if pair else []) + list(rows) + list(pars)
    in_specs = [a_spec, b_spec] + ([b2_spec] if pair else []) + [o_spec] * n_rows
    in_specs += [pl.BlockSpec(p.shape, lambda i, j, r, nd=p.ndim: (0,) * nd) for p in pars]
    out_specs = [o_spec] * n_out + [pl.BlockSpec(s, lambda i, j, r: (0, 0)) for s in accs]
    out_shapes += [jax.ShapeDtypeStruct(s, F32) for s in accs]
    scratch = [pltpu.VMEM((ti, tj), F32)] * n_prod if nr > 1 else []
    ordered = bool(side) or n_acc > 0
    res = pl.pallas_call(
        body, name=name, grid=grid, in_specs=in_specs + _any_specs(n_si), out_specs=out_specs + _any_specs(n_so),
        out_shape=out_shapes + (list(side.out_shapes) if side else []), scratch_shapes=scratch + (list(side.scratch) if side else []),
        compiler_params=pltpu.CompilerParams(
            dimension_semantics=("arbitrary",) * 3 if ordered else ("parallel", "parallel", "arbitrary"), vmem_limit_bytes=VMEM_LIMIT),
    )(*ins, *(side.ins if side else []))
    own = res[:n_out + n_acc]
    own = own[0] if len(own) == 1 else tuple(own)
    return (own, list(res[n_out + n_acc:])) if side else own


def rowwise(name, fn, rows, pars, outs, accs=(), *, n_rows, tm):
    nt = n_rows // tm
    assert nt * tm == n_rows
    in_specs, ins = [], []
    for arr, w, cb in rows:
        in_specs.append(pl.BlockSpec((tm, w), lambda i, cb=cb: (i, cb)))
        ins.append(arr)
    for p in pars:
        in_specs.append(pl.BlockSpec(p.shape, lambda i, nd=p.ndim: (0,) * nd))
        ins.append(p)
    n_in = len(ins)
    out_specs, out_shapes, aliases = [], [], {}
    for k, o in enumerate(outs):
        if o[0] == "new":
            _, w, dt = o
            out_specs.append(pl.BlockSpec((tm, w), lambda i: (i, 0)))
            out_shapes.append(jax.ShapeDtypeStruct((n_rows, w), dt))
        else:
            _, arr, w, cb = o
            in_specs.append(pl.BlockSpec(memory_space=pl.ANY))
            aliases[len(ins)] = k
            ins.append(arr)
            out_specs.append(pl.BlockSpec((tm, w), lambda i, cb=cb: (i, cb)))
            out_shapes.append(jax.ShapeDtypeStruct(arr.shape, arr.dtype))
    for r, w in accs:
        out_specs.append(pl.BlockSpec((r, w), lambda i: (0, 0)))
        out_shapes.append(jax.ShapeDtypeStruct((r, w), F32))
    n_all_in, n_out, n_acc = len(ins), len(outs), len(accs)

    def body(*refs):
        i = pl.program_id(0)
        vals = [r[...] for r in refs[:n_in]]
        res = fn(i, *vals)
        if not isinstance(res, (tuple, list)):
            res = (res,)
        o_refs = refs[n_all_in:n_all_in + n_out]
        a_refs = refs[n_all_in + n_out:]
        for r, v in zip(o_refs, res[:n_out]):
            r[...] = v.astype(r.dtype)
        if n_acc:
            @pl.when(i == 0)
            def _():
                for r in a_refs:
                    r[...] = jnp.zeros_like(r)
            for r, v in zip(a_refs, res[n_out:]):
                r[...] += v

    return pl.pallas_call(
        body, name=name, grid=(nt,), in_specs=in_specs, out_specs=out_specs, out_shape=out_shapes,
        input_output_aliases=aliases,
        compiler_params=pltpu.CompilerParams(dimension_semantics=("arbitrary",), vmem_limit_bytes=VMEM_LIMIT),
    )(*ins)


def _row_ids(i, tm):
    return i * tm + lax.broadcasted_iota(jnp.int32, (tm, 1), 0)


def _sigmoid(x):
    return 1.0 / (1.0 + jnp.exp(-x))


def _silu(x):
    return x * _sigmoid(x)


def _softplus(x):
    return jnp.maximum(x, 0.0) + jnp.log(1.0 + jnp.exp(-jnp.abs(x)))


def _rms(x, w):
    return x * lax.rsqrt(jnp.mean(x * x, axis=-1, keepdims=True) + EPS) * w


def _heads(fn, *xs):
    return jnp.concatenate([fn(h, *[x[:, h * DH:(h + 1) * DH] for x in xs]) for h in range(H)], axis=1)


def _merge(a, b, ga, gb):
    return _sigmoid(ga) * a + _sigmoid(gb) * b


def _select_matrix(first_lane):
    r = lax.broadcasted_iota(jnp.int32, (LANES, H * DH), 0)
    c = lax.broadcasted_iota(jnp.int32, (LANES, H * DH), 1)
    return (r == first_lane + (c >> 7)).astype(F32)


def _chunk_tri(tm):
    r = lax.broadcasted_iota(jnp.int32, (tm, tm), 0)
    c = lax.broadcasted_iota(jnp.int32, (tm, tm), 1)
    return jnp.logical_and((r >> 6) == (c >> 6), r >= c).astype(F32)


def _gdn_gates(ba, alog_row, dtb_row, mask):
    g = -jnp.exp(alog_row) * _softplus(ba + dtb_row) * mask
    gc = _mmh(_chunk_tri(ba.shape[0]), g)
    beta = _sigmoid(ba) * mask
    return _mmh(gc, _select_matrix(H)), _mmh(beta, _select_matrix(0)), gc


def _gdn_qkv(c):
    a = _silu(c)

    def l2(scale):
        return lambda h, t: t * lax.rsqrt(jnp.sum(t * t, axis=-1, keepdims=True) + EPS) * scale
    q = _heads(l2(DH ** -0.5), a[:, :D])
    k = _heads(l2(1.0), a[:, D:2 * D])
    return q, k, a[:, 2 * D:]


def _conv_taps(xw, cws, tm):
    return sum(cws[i] * xw[5 + i:5 + i + tm] for i in range(CONV_K))


def _swap_pairs(t):
    n = t.shape[1]
    lane = lax.broadcasted_iota(jnp.int32, t.shape, 1)
    return jnp.where((lane & 1) == 0, pltpu.roll(t, n - 1, 1), pltpu.roll(t, 1, 1))


@jax.custom_vjp
def _unit_lower_inv(a):
    n = -a
    eye = (lax.broadcasted_iota(jnp.int32, a.shape, 1) == lax.broadcasted_iota(jnp.int32, a.shape, 2)).astype(F32)
    p = eye + n
    for _ in range(5):
        n = _bdot3(n, n, _NN)
        p = p + _bdot3(p, n, _NN)
    return p


def _inv_fwd(a):
    t = _unit_lower_inv(a)
    return t, t


def _inv_bwd(t, dt):
    x = _bdot3(t, dt, _TN)
    return (-_bdot3(x, t, _NT),)


_unit_lower_inv.defvjp(_inv_fwd, _inv_bwd)


@jax.custom_vjp
def _unit_lower_inv_known(a, t):
    return t


_unit_lower_inv_known.defvjp(lambda a, t: (t, t), lambda t, dt: (_inv_bwd(t, dt)[0], jnp.zeros_like(t)))


def _gdn_chunk(q, k, v, gc, bb, z, gr, s, norm_w, t_known=None):
    ri = lax.broadcasted_iota(jnp.int32, (H, CH, CH), 1)
    ci = lax.broadcasted_iota(jnp.int32, (H, CH, CH), 2)
    causal = ri >= ci
    gc1 = jnp.sum(gc, axis=2, keepdims=True) * (1.0 / LANES)
    diff = jnp.broadcast_to(gc1, (H, CH, CH)) - jnp.broadcast_to(gr, (H, CH, CH))
    decay = jnp.where(causal, jnp.exp(jnp.where(causal, diff, 0.0)), 0.0)
    kb = k * bb
    sc = _bmm(jnp.concatenate([q, kb], axis=1), k, _NT)
    qk = sc[:, :CH] * decay
    a = jnp.where(ri > ci, sc[:, CH:] * decay, 0.0)
    t = _unit_lower_inv(a) if t_known is None else _unit_lower_inv_known(a, t_known)
    eg = jnp.exp(gc)
    uw = _bmm(t, jnp.concatenate([v * bb, kb * eg], axis=2), _NN)
    g_last = gc[:, CH - 1:CH, :]
    ws = _bmm(jnp.concatenate([uw[:, :, DH:], q * eg], axis=1), s, _NN)
    v_new = uw[:, :, :DH] - ws[:, :CH]
    o = ws[:, CH:] + _bmm(qk, v_new, _NN)
    s_new = s * jnp.exp(g_last) + _bmm(k * jnp.exp(g_last - gc), v_new, _TN)
    y = o * lax.rsqrt(jnp.mean(o * o, axis=-1, keepdims=True) + EPS) * norm_w * _silu(z)
    return y, s_new, t


def _swap_pairs_heads(t):
    return _swap_pairs(t.reshape(t.shape[0] * CH, DH)).reshape(t.shape)


@jax.custom_vjp
def _rope(t, ct, st):
    return t * ct + _swap_pairs_heads(t) * st


def _rope_fwd(t, ct, st):
    return _rope(t, ct, st), (ct, st)


def _rope_bwd(res, d):
    ct, st = res
    return d * ct + _swap_pairs_heads(d * st), jnp.zeros_like(ct), jnp.zeros_like(st)


_rope.defvjp(_rope_fwd, _rope_bwd)


def _ret_chunk(q, k, v, rg, s, decay, xi, zeta, cd, ct, st, norm_w):
    q = _rope(q, ct, st)
    k = _rope(k, ct, st) * (DH ** -0.5)
    scores = _bmm(q, k, _NT) * decay
    o = _bmm(scores, v, _NN) + _bmm(q * xi, s, _NN)
    s_new = s * cd + _bmm(k * zeta, v, _TN)
    xc = o - jnp.mean(o, axis=-1, keepdims=True)
    var = jnp.mean(xc * xc, axis=-1, keepdims=True)
    y = _silu(rg) * (xc * lax.rsqrt(var + EPS) * norm_w)
    return y, s_new


def _head_blocks(ref):
    return jnp.stack([ref[:, h * DH:(h + 1) * DH] for h in range(H)])


def _head_rows(ref):
    return jnp.stack([ref[0, h:h + 1, :] for h in range(H)])


def _scan_fwd(name, chunk_fn, blocks, rowvecs, consts, nc, shared=(), params=(), out_dtype=F32, keep=None):
    n_blk, n_rv, n_c = len(blocks), len(rowvecs), len(consts) + len(shared) + len(params)

    def body(*refs):
        blk = refs[:n_blk]
        rvs = refs[n_blk:n_blk + n_rv]
        cst = refs[n_blk + n_rv:n_blk + n_rv + n_c]
        o_ref, st_ref = refs[n_blk + n_rv + n_c:n_blk + n_rv + n_c + 2]
        s_scr = refs[-1]

        @pl.when(pl.program_id(0) == 0)
        def _():
            s_scr[...] = jnp.zeros_like(s_scr)

        s = s_scr[...]
        st_ref[0] = s
        res = chunk_fn(*[_head_blocks(r) for r in blk], *[_head_rows(r) for r in rvs], s, *[r[...] for r in cst])
        for h in range(H):
            o_ref[:, h * DH:(h + 1) * DH] = res[0][h].astype(o_ref.dtype)
        s_scr[...] = res[1]
        if keep is not None:
            refs[-2][0] = res[2]

    in_specs = [pl.BlockSpec((CH, H * DH), lambda c, f=f: (c, f)) for _, f in blocks]
    in_specs += [pl.BlockSpec((1, H, CH), lambda c: (c, 0, 0)) for _ in rowvecs]
    in_specs += [pl.BlockSpec(a.shape, lambda c: (0, 0, 0)) for a in consts]
    in_specs += [pl.BlockSpec((CH, a.shape[1]), lambda c: (c, 0)) for a in shared]
    in_specs += [pl.BlockSpec(a.shape, lambda c, nd=a.ndim: (0,) * nd) for a in params]
    out_specs = [pl.BlockSpec((CH, H * DH), lambda c: (c, 0)), pl.BlockSpec((1, H, DH, DH), lambda c: (c, 0, 0, 0))]
    out_shape = [jax.ShapeDtypeStruct((nc * CH, H * DH), out_dtype), jax.ShapeDtypeStruct((nc, H, DH, DH), F32)]
    if keep is not None:
        out_specs.append(pl.BlockSpec((1, H) + keep, lambda c: (c, 0, 0, 0)))
        out_shape.append(jax.ShapeDtypeStruct((nc, H) + keep, F32))
    return pl.pallas_call(
        body, name=name, grid=(nc,), in_specs=in_specs, out_specs=out_specs, out_shape=out_shape,
        scratch_shapes=[pltpu.VMEM((H, DH, DH), F32)],
        compiler_params=pltpu.CompilerParams(dimension_semantics=("arbitrary",), vmem_limit_bytes=VMEM_LIMIT),
    )(*[a for a, _ in blocks], *rowvecs, *consts, *shared, *params)


def _scan_bwd(name, chunk_fn, blocks, rowvecs, consts, states, do, into, nc, shared=(), params=(), kept=None):
    n_blk, n_rv, n_c, n_p = len(blocks), len(rowvecs), len(consts) + len(shared), len(params)
    n_kept = 0 if kept is None else 1
    packed = list(into[2]) if into else []
    fresh = [k for k in range(n_blk) if k not in packed]
    n_bo = len(fresh) + (1 if into else 0)

    def body(*refs):
        blk = refs[:n_blk]
        rvs = refs[n_blk:n_blk + n_rv]
        cst = refs[n_blk + n_rv:n_blk + n_rv + n_c]
        par = refs[n_blk + n_rv + n_c:n_blk + n_rv + n_c + n_p]
        st_ref, do_ref = refs[n_blk + n_rv + n_c + n_p:n_blk + n_rv + n_c + n_p + 2]
        n_in = n_blk + n_rv + n_c + n_p + 2 + n_kept + (1 if into else 0)
        known = [refs[n_blk + n_rv + n_c + n_p + 2][0]] if n_kept else []
        o_refs = refs[n_in:n_in + n_bo]
        rv_refs = refs[n_in + n_bo:n_in + n_bo + n_rv]
        p_refs = refs[n_in + n_bo + n_rv:n_in + n_bo + n_rv + n_p]
        ds_scr = refs[n_in + n_bo + n_rv + n_p]

        @pl.when(pl.program_id(0) == 0)
        def _():
            ds_scr[...] = jnp.zeros_like(ds_scr)
            for r in p_refs:
                r[...] = jnp.zeros_like(r)

        cv = [r[...] for r in cst]
        n_d = n_blk + n_rv + 1
        _, vjp = jax.vjp(lambda *a: chunk_fn(*a[:n_d], *cv, *a[n_d:], *known)[:2], *[_head_blocks(r) for r in blk],
                         *[_head_rows(r) for r in rvs], st_ref[0], *[r[...] for r in par])
        grads = vjp((_head_blocks(do_ref), ds_scr[...]))
        for h in range(H):
            for pos, k in enumerate(fresh):
                o_refs[pos][:, h * DH:(h + 1) * DH] = grads[k][h]
            for pos, k in enumerate(packed):
                col = pos * H * DH + h * DH
                o_refs[-1][:, col:col + DH] = grads[k][h].astype(o_refs[-1].dtype)
            for r, g in zip(rv_refs, grads[n_blk:n_blk + n_rv]):
                r[0, h:h + 1, :] = g[h]
        ds_scr[...] = grads[n_blk + n_rv]
        for r, g in zip(p_refs, grads[n_d:]):
            r[...] += g

    rc = lambda c: nc - 1 - c
    in_specs = [pl.BlockSpec((CH, H * DH), lambda c, f=f: (rc(c), f)) for _, f in blocks]
    in_specs += [pl.BlockSpec((1, H, CH), lambda c: (rc(c), 0, 0)) for _ in rowvecs]
    in_specs += [pl.BlockSpec(a.shape, lambda c: (0, 0, 0)) for a in consts]
    in_specs += [pl.BlockSpec((CH, a.shape[1]), lambda c: (rc(c), 0)) for a in shared]
    in_specs += [pl.BlockSpec(a.shape, lambda c, nd=a.ndim: (0,) * nd) for a in params]
    in_specs += [pl.BlockSpec((1, H, DH, DH), lambda c: (rc(c), 0, 0, 0)), pl.BlockSpec((CH, H * DH), lambda c: (rc(c), 0))]
    ins = [a for a, _ in blocks] + list(rowvecs) + list(consts) + list(shared) + list(params) + [states, do]
    if n_kept:
        in_specs.append(pl.BlockSpec((1,) + kept.shape[1:], lambda c: (rc(c), 0, 0, 0)))
        ins.append(kept)
    out_specs, out_shapes, aliases = [], [], {}
    for _ in fresh:
        out_specs.append(pl.BlockSpec((CH, H * DH), lambda c: (rc(c), 0)))
        out_shapes.append(jax.ShapeDtypeStruct((nc * CH, H * DH), F32))
    if into:
        arr, f, _ = into
        in_specs.append(pl.BlockSpec(memory_space=pl.ANY))
        aliases[len(ins)] = len(fresh)
        ins.append(arr)
        out_specs.append(pl.BlockSpec((CH, len(packed) * H * DH), lambda c: (rc(c), f)))
        out_shapes.append(jax.ShapeDtypeStruct(arr.shape, arr.dtype))
    for _ in rowvecs:
        out_specs.append(pl.BlockSpec((1, H, CH), lambda c: (rc(c), 0, 0)))
        out_shapes.append(jax.ShapeDtypeStruct((nc, H, CH), F32))
    for a in params:
        out_specs.append(pl.BlockSpec(a.shape, lambda c, nd=a.ndim: (0,) * nd))
        out_shapes.append(jax.ShapeDtypeStruct(a.shape, F32))
    return pl.pallas_call(
        body, name=name, grid=(nc,), in_specs=in_specs, out_specs=out_specs, out_shape=out_shapes,
        input_output_aliases=aliases, scratch_shapes=[pltpu.VMEM((H, DH, DH), F32)],
        compiler_params=pltpu.CompilerParams(dimension_semantics=("arbitrary",), vmem_limit_bytes=VMEM_LIMIT),
    )(*ins)


def _gdn_pre_specs(p, cws, alog_row, dtb_row, tm, pos):
    sub = tm // 8
    rows = [pl.BlockSpec((tm, 3 * D), lambda i: (pos(i), 0)),
            pl.BlockSpec((8, 3 * D), lambda i: (jnp.maximum(pos(i) * sub - 1, 0), 0)),
            pl.BlockSpec((tm, LANES), lambda i: (pos(i), OFF_BA // LANES))]
    pars = [pl.BlockSpec(a.shape, lambda i: (0, 0)) for a in (*cws, alog_row, dtb_row)]
    return rows + pars, [p, p, p, *cws, alog_row, dtb_row]


def gdn_pre_fwd(p, cws, alog_row, dtb_row, lp, tm):
    def body(x_ref, prev_ref, ba_ref, c0, c1, c2, c3, al_ref, dt_ref, q_ref, k_ref, v_ref, g_ref, b_ref, gc_ref):
        i = pl.program_id(0)
        prev = jnp.where(i > 0, prev_ref[...], 0.0)
        xw = jnp.concatenate([prev, x_ref[...]], axis=0)
        c = _conv_taps(xw, [c0[...], c1[...], c2[...], c3[...]], tm)
        q, k, v = _gdn_qkv(c)
        mask = (_row_ids(i, tm) >= PAD).astype(F32)
        g, b, gc = _gdn_gates(ba_ref[...], al_ref[...], dt_ref[...], mask)
        q_ref[...] = q
        k_ref[...] = k
        v_ref[...] = v
        g_ref[...] = g
        b_ref[...] = b
        gc_ref[...] = gc

    in_specs, ins = _gdn_pre_specs(p, cws, alog_row, dtb_row, tm, lambda i: i)
    o_spec = pl.BlockSpec((tm, D), lambda i: (i, 0))
    return pl.pallas_call(
        body, name="gdn_pre_fwd", grid=(lp // tm,), in_specs=in_specs,
        out_specs=[o_spec] * 5 + [pl.BlockSpec((tm, LANES), lambda i: (i, 0))],
        out_shape=[jax.ShapeDtypeStruct((lp, D), F32)] * 5 + [jax.ShapeDtypeStruct((lp, LANES), F32)],
        compiler_params=pltpu.CompilerParams(dimension_semantics=("parallel",), vmem_limit_bytes=VMEM_LIMIT),
    )(*ins)


def gdn_pre_bwd(p, cws, alog_row, dtb_row, dq, dk, dv, dg, db, dgc, dp, lp, tm):
    nt = lp // tm
    pos = lambda i: nt - 1 - i

    def body(x_ref, prev_ref, ba_ref, c0, c1, c2, c3, al_ref, dt_ref, dq_ref, dk_ref, dv_ref, dg_ref, db_ref, dgc_ref,
             dp_any, dx_ref, dba_ref, dcw_ref, dpar_ref, carry):
        i = pl.program_id(0)
        t = pos(i)

        @pl.when(i == 0)
        def _():
            carry[...] = jnp.zeros_like(carry)
            dcw_ref[...] = jnp.zeros_like(dcw_ref)
            dpar_ref[...] = jnp.zeros_like(dpar_ref)

        cws_v = [c0[...], c1[...], c2[...], c3[...]]
        prev = jnp.where(t > 0, prev_ref[...], 0.0)
        xw = jnp.concatenate([prev, x_ref[...]], axis=0)
        xs = [xw[5 + j:5 + j + tm] for j in range(CONV_K)]
        c = sum(cws_v[j] * xs[j] for j in range(CONV_K))
        _, vjp_qkv = jax.vjp(_gdn_qkv, c)
        (dc,) = vjp_qkv((dq_ref[...], dk_ref[...], dv_ref[...]))
        zeros8 = jnp.zeros((8, 3 * D), F32)
        dcp = jnp.concatenate([zeros8, dc, zeros8], axis=0)
        dxw = sum(cws_v[j] * dcp[3 - j:3 - j + tm + 8] for j in range(CONV_K))
        dx_ref[...] = jnp.concatenate([dxw[8:tm], dxw[tm:] + carry[...]], axis=0).astype(dx_ref.dtype)
        carry[...] = dxw[:8]
        for j in range(CONV_K):
            dcw_ref[j:j + 1, :] += jnp.sum(dc * xs[j], axis=0, keepdims=True)
        mask = (_row_ids(t, tm) >= PAD).astype(F32)
        _, vjp_g = jax.vjp(lambda ba, al, dt: _gdn_gates(ba, al, dt, mask), ba_ref[...], al_ref[...], dt_ref[...])
        dba, dal, ddt = vjp_g((dg_ref[...], db_ref[...], dgc_ref[...]))
        dba_ref[...] = dba
        dpar_ref[0:1, :] += dal
        dpar_ref[1:2, :] += ddt

    in_specs, ins = _gdn_pre_specs(p, cws, alog_row, dtb_row, tm, pos)
    g_spec = pl.BlockSpec((tm, D), lambda i: (pos(i), 0))
    s_spec = pl.BlockSpec((tm, LANES), lambda i: (pos(i), 0))
    in_specs += [g_spec] * 5 + [s_spec, pl.BlockSpec(memory_space=pl.ANY)]
    ins += [dq, dk, dv, dg, db, dgc, dp]
    return pl.pallas_call(
        body, name="gdn_pre_bwd", grid=(nt,), in_specs=in_specs,
        out_specs=[pl.BlockSpec((tm, 3 * D), lambda i: (pos(i), 0)), s_spec,
                   pl.BlockSpec((8, 3 * D), lambda i: (0, 0)), pl.BlockSpec((8, LANES), lambda i: (0, 0))],
        out_shape=[jax.ShapeDtypeStruct(dp.shape, dp.dtype), jax.ShapeDtypeStruct((lp, LANES), F32),
                   jax.ShapeDtypeStruct((8, 3 * D), F32), jax.ShapeDtypeStruct((8, LANES), F32)],
        input_output_aliases={len(ins) - 1: 0},
        scratch_shapes=[pltpu.VMEM((8, 3 * D), F32)],
        compiler_params=pltpu.CompilerParams(dimension_semantics=("arbitrary",), vmem_limit_bytes=VMEM_LIMIT),
    )(*ins)


def _me():
    return lax.axis_index("x"), lax.axis_index("y"), lax.axis_index("c")


def _any_specs(n):
    return [pl.BlockSpec(memory_space=pl.ANY)] * n


Exchange = collections.namedtuple("Exchange", "ins out_shapes scratch start finish")


def _dma_sems(*shape):
    return pltpu.SemaphoreType.DMA(shape)


def run_exchange(name, ex):
    n_i, n_o = len(ex.ins), len(ex.out_shapes)

    def body(*refs):
        parts = (refs[:n_i], refs[n_i:n_i + n_o], refs[n_i + n_o:])
        ex.start(*parts)
        ex.finish(*parts)

    return pl.pallas_call(body, name=name, out_shape=list(ex.out_shapes), in_specs=_any_specs(n_i), out_specs=_any_specs(n_o),
                          scratch_shapes=list(ex.scratch))(*ex.ins)


def gather_exchange(ws):
    n = len(ws)

    def copies(w_refs, o_refs, sems):
        send_sems, recv_sems, fsend_sems, frecv_sems, osend_sems, orecv_sems = sems
        x, y, c = _me()
        me = 2 * x + y
        chips = [(1 - x, y), (x, 1 - y), (1 - x, 1 - y)]

        def own(a):
            return pltpu.make_async_remote_copy(src_ref=w_refs[a], dst_ref=o_refs[a].at[me], send_sem=osend_sems.at[a],
                                                recv_sem=orecv_sems.at[a], device_id=(x, y, 1 - c), device_id_type=MESH)

        def rows(a, cc):
            rh = ws[a].shape[0] // 2
            return pl.ds(pl.multiple_of(cc * rh, 8), rh)

        def ici(a, j, slot):
            px, py = chips[j]
            return pltpu.make_async_remote_copy(
                src_ref=w_refs[a].at[rows(a, c)], dst_ref=o_refs[a].at[slot, rows(a, c)], send_sem=send_sems.at[a, j],
                recv_sem=recv_sems.at[a, j], device_id=(px, py, c), device_id_type=MESH)

        def d2d(a, j, cc):
            px, py = chips[j]
            blk = o_refs[a].at[2 * px + py, rows(a, cc)]
            return pltpu.make_async_remote_copy(src_ref=blk, dst_ref=blk, send_sem=fsend_sems.at[a, j],
                                                recv_sem=frecv_sems.at[a, j], device_id=(x, y, 1 - c), device_id_type=MESH)
        mine = [own(a) for a in range(n)]
        sends = [ici(a, j, me) for a in range(n) for j in range(3)]
        arrivals = [ici(a, j, 2 * px + py) for a in range(n) for j, (px, py) in enumerate(chips)]
        passes = [d2d(a, j, c) for a in range(n) for j in range(3)]
        passed_to_me = [d2d(a, j, 1 - c) for a in range(n) for j in range(3)]
        return mine, sends, arrivals, passes, passed_to_me

    def start(w_refs, o_refs, sems):
        mine, sends, _, _, _ = copies(w_refs, o_refs, sems)
        for cp in mine + sends:
            cp.start()

    def finish(w_refs, o_refs, sems):
        mine, sends, arrivals, passes, passed_to_me = copies(w_refs, o_refs, sems)
        for arrival, onward in zip(arrivals, passes):
            arrival.wait_recv()
            onward.start()
        for cp in passed_to_me + mine:
            cp.wait_recv()
        for cp in sends + passes + mine:
            cp.wait_send()

    return Exchange(list(ws), [jax.ShapeDtypeStruct((N_CHIPS,) + w.shape, w.dtype) for w in ws],
                    [_dma_sems(n, 3), _dma_sems(n, 3), _dma_sems(n, 3), _dma_sems(n, 3), _dma_sems(n), _dma_sems(n)], start, finish)


def _simple_exchange(ins, out_shapes, sem_shape, copies):
    def start(i_refs, o_refs, sems):
        for cp in copies(i_refs, o_refs, *sems):
            cp.start()

    def finish(i_refs, o_refs, sems):
        cps = copies(i_refs, o_refs, *sems)
        for cp in cps:
            cp.wait_recv()
        for cp in cps:
            cp.wait_send()

    return Exchange(list(ins), out_shapes, [_dma_sems(*sem_shape), _dma_sems(*sem_shape)], start, finish)


def sibling_halves_exchange(gs):
    def copies(g_refs, o_refs, send_sems, recv_sems):
        x, y, c = _me()
        cps = []
        for a in range(len(gs)):
            rh = gs[a].shape[1] // 2
            for q in range(N_CHIPS):
                cps.append(pltpu.make_async_remote_copy(
                    src_ref=g_refs[a].at[q, pl.ds(pl.multiple_of((1 - c) * rh, 8), rh)], dst_ref=o_refs[a].at[q],
                    send_sem=send_sems.at[a, q], recv_sem=recv_sems.at[a, q], device_id=(x, y, 1 - c), device_id_type=MESH))
        return cps

    return _simple_exchange(gs, [jax.ShapeDtypeStruct((N_CHIPS, g.shape[1] // 2, g.shape[2]), g.dtype) for g in gs],
                            (len(gs), N_CHIPS), copies)


def scatter_chips_exchange(css):
    def copies(c_refs, o_refs, send_sems, recv_sems):
        x, y, c = _me()
        chips = [(1 - x, y), (x, 1 - y), (1 - x, 1 - y)]
        return [pltpu.make_async_remote_copy(
            src_ref=c_refs[a].at[2 * px + py], dst_ref=o_refs[a].at[j], send_sem=send_sems.at[a, j],
            recv_sem=recv_sems.at[a, j], device_id=(px, py, c), device_id_type=MESH)
            for a in range(len(css)) for j, (px, py) in enumerate(chips)]

    return _simple_exchange(css, [jax.ShapeDtypeStruct((3,) + cs.shape[1:], cs.dtype) for cs in css], (len(css), 3), copies)


def sibling_swap(name, halves):
    n = len(halves)

    def body(*refs):
        h_refs, o_refs = refs[:n], refs[n:2 * n]
        send_sems, recv_sems = refs[2 * n:]
        x, y, c = _me()
        cps = [pltpu.make_async_remote_copy(src_ref=h_refs[a], dst_ref=o_refs[a], send_sem=send_sems.at[a],
                                            recv_sem=recv_sems.at[a], device_id=(x, y, 1 - c), device_id_type=MESH)
               for a in range(n)]
        for cp in cps:
            cp.start()
        for cp in cps:
            cp.wait_recv()
        for cp in cps:
            cp.wait_send()

    return pl.pallas_call(
        body, name=name, out_shape=[jax.ShapeDtypeStruct(h.shape, h.dtype) for h in halves],
        in_specs=_any_specs(n), out_specs=_any_specs(n),
        scratch_shapes=[pltpu.SemaphoreType.DMA((n,)), pltpu.SemaphoreType.DMA((n,))],
    )(*halves)


def allgather_all(name, s):
    def body(s_ref, out_ref, send_sems, recv_sems, local_sem):
        x, y, c = _me()
        peers = [(x ^ ((m >> 2) & 1), y ^ ((m >> 1) & 1), c ^ (m & 1)) for m in range(1, 8)]
        mine = pltpu.make_async_copy(s_ref, out_ref.at[4 * x + 2 * y + c], local_sem)
        mine.start()

        def copy(j, slot):
            return pltpu.make_async_remote_copy(src_ref=s_ref, dst_ref=out_ref.at[slot], send_sem=send_sems.at[j],
                                                recv_sem=recv_sems.at[j], device_id=peers[j], device_id_type=MESH)
        sends = [copy(j, 4 * x + 2 * y + c) for j in range(7)]
        for cp in sends:
            cp.start()
        for j, (px, py, pc) in enumerate(peers):
            copy(j, 4 * px + 2 * py + pc).wait_recv()
        for cp in sends:
            cp.wait_send()
        mine.wait()

    return pl.pallas_call(
        body, name=name, out_shape=jax.ShapeDtypeStruct((8,) + s.shape, s.dtype),
        in_specs=_any_specs(1), out_specs=pl.BlockSpec(memory_space=pl.ANY),
        scratch_shapes=[pltpu.SemaphoreType.DMA((7,)), pltpu.SemaphoreType.DMA((7,)), pltpu.SemaphoreType.DMA(())],
    )(s)


def sum_slots(name, parts):
    n = len(parts)
    R, W = parts[0][0].shape[1:]
    tm = _tile_rows(R, W)
    idx = jnp.stack([jnp.asarray(s, jnp.int32) for _, s in parts])

    def body(idx_ref, *refs):
        acc = refs[0][...].astype(F32)
        for r in refs[1:n]:
            acc = acc + r[...].astype(F32)
        refs[n][...] = acc

    grid_spec = pltpu.PrefetchScalarGridSpec(
        num_scalar_prefetch=1, grid=(R // tm,),
        in_specs=[pl.BlockSpec((None, tm, W), lambda i, idx, k=k: (idx[k], i, 0)) for k in range(n)],
        out_specs=pl.BlockSpec((tm, W), lambda i, idx: (i, 0)))
    return pl.pallas_call(body, name=name, grid_spec=grid_spec, out_shape=jax.ShapeDtypeStruct((R, W), F32),
                          compiler_params=pltpu.CompilerParams(dimension_semantics=("parallel",)))(idx, *[a for a, _ in parts])


def chip_sums(name, g, recv, c):
    _, R, W = g.shape
    rh = R // 2
    tm = _tile_rows(rh, W)
    g8 = g.reshape(2 * N_CHIPS, rh, W)
    idx = jnp.asarray(c, jnp.int32).reshape(1)

    def body(idx_ref, g_ref, r_ref, o_ref):
        o_ref[...] = (g_ref[...] + r_ref[...]).astype(o_ref.dtype)

    grid_spec = pltpu.PrefetchScalarGridSpec(
        num_scalar_prefetch=1, grid=(N_CHIPS, rh // tm),
        in_specs=[pl.BlockSpec((None, tm, W), lambda q, i, idx: (2 * q + idx[0], i, 0)),
                  pl.BlockSpec((None, tm, W), lambda q, i, idx: (q, i, 0))],
        out_specs=pl.BlockSpec((None, tm, W), lambda q, i, idx: (q, i, 0)))
    return pl.pallas_call(body, name=name, grid_spec=grid_spec, out_shape=jax.ShapeDtypeStruct((N_CHIPS, rh, W), BF16),
                          compiler_params=pltpu.CompilerParams(dimension_semantics=("parallel", "parallel")))(idx, g8, recv)


def _adamw_update(w, g, m, v):
    m = ADAM_B1 * m + (1.0 - ADAM_B1) * g
    v = ADAM_B2 * v + (1.0 - ADAM_B2) * (g * g)
    m_hat = m / (1.0 - ADAM_B1 ** ADAM_STEP)
    v_hat = v / (1.0 - ADAM_B2 ** ADAM_STEP)
    return -ADAM_LR * (m_hat / (jnp.sqrt(v_hat) + ADAM_EPS) + ADAM_WD * w), m, v


def adamw(name, w, g, m, v):
    R, W = w.shape
    return rowwise(name, lambda i, w, g, m, v: _adamw_update(w, g, m, v), [(a, W, 0) for a in (w, g, m, v)], [],
                   [("new", W, F32)] * 3, n_rows=R, tm=_tile_rows(R, W, 2 ** 20))


def adamw_halves(name, w, g_mine, g_other, m, v, c):
    R, W = w.shape
    rh = R // 2
    tm = _tile_rows(rh, W, 2 ** 20)
    nh = rh // tm
    idx = jnp.asarray(c, jnp.int32).reshape(1)

    def body(idx_ref, w_ref, ga_ref, gb_ref, m_ref, v_ref, g_ref, d_ref, mo_ref, vo_ref):
        mine = (pl.program_id(0) // nh) == idx_ref[0]
        g = jnp.where(mine, ga_ref[...], gb_ref[...])
        d, m, v = _adamw_update(w_ref[...], g, m_ref[...], v_ref[...])
        g_ref[...] = g
        d_ref[...] = d
        mo_ref[...] = m
        vo_ref[...] = v

    full = pl.BlockSpec((tm, W), lambda i, idx: (i, 0))
    half = pl.BlockSpec((tm, W), lambda i, idx: (i % nh, 0))
    grid_spec = pltpu.PrefetchScalarGridSpec(num_scalar_prefetch=1, grid=(R // tm,), in_specs=[full, half, half, full, full],
                                             out_specs=[full] * 4)
    return pl.pallas_call(body, name=name, grid_spec=grid_spec, out_shape=[jax.ShapeDtypeStruct((R, W), F32)] * 4,
                          compiler_params=pltpu.CompilerParams(dimension_semantics=("parallel",)))(idx, w, g_mine, g_other, m, v)


W_IN_BLOCK = D_PROJ // N_CHIPS
BA_REF = OFF_RQ


def _w_in_padded_from_blocks(g):
    lo, hi = BA_REF - W_IN_BLOCK, BA_REF + 16 - W_IN_BLOCK
    return jnp.concatenate([g[0], g[1][:, :lo], g[1][:, hi:], g[2], g[3], g[1][:, lo:hi],
                            jnp.zeros((D, PW - D_PROJ), g.dtype)], axis=1)


def _w_in_grad_blocks(g):
    s = W_IN_BLOCK
    second = jnp.concatenate([g[:, s:BA_REF], g[:, OFF_BA:OFF_BA + 16], g[:, BA_REF:2 * s - 16]], axis=1)
    return jnp.stack([g[:, :s], second, g[:, 2 * s - 16:3 * s - 16], g[:, 3 * s - 16:4 * s - 16]])


def _ret_consts():
    f = F32
    log_gamma = jnp.log1p(-jnp.exp2(-5.0 - jnp.arange(H, dtype=f)))
    pos = jnp.arange(CH, dtype=f)
    causal = jnp.tril(jnp.ones((CH, CH), dtype=bool))
    diff = pos[:, None] - pos[None, :]
    decay = jnp.where(causal, jnp.exp(jnp.where(causal, diff, 0.0) * log_gamma[:, None, None]), 0.0)
    xi = jnp.broadcast_to(jnp.exp((pos + 1.0) * log_gamma[:, None])[:, :, None], (H, CH, DH))
    zeta = jnp.broadcast_to(jnp.exp((CH - 1.0 - pos) * log_gamma[:, None])[:, :, None], (H, CH, DH))
    cd = jnp.broadcast_to(jnp.exp(CH * log_gamma)[:, None, None], (H, 1, DH))
    return decay, xi, zeta, cd


def _rope_tables(lp):
    pos = jnp.arange(lp, dtype=F32) - float(PAD)
    inv = 1.0 / (ROPE_BASE ** jnp.linspace(0.0, 1.0, DH // 2, dtype=F32))
    ang = pos[:, None] * inv[None, :]
    cos, sin = jnp.cos(ang), jnp.sin(ang)
    ct = jnp.repeat(cos, 2, axis=1)
    st = jnp.stack([-sin, sin], axis=-1).reshape(lp, DH)
    return ct, st


def local_step(x, tgt, w, small, hooks=None):
    hooks = hooks or {}
    seq = x.shape[0]
    lp = HEAD_ROWS + seq
    nc = lp // CH
    tm = _pick(lp, 192, CH)
    row = functools.partial(rowwise, n_rows=lp, tm=tm)
    gw = {}

    def mm(name, *args, **kw):
        if name not in hooks:
            return matmul(name, *args, **kw)
        make_exchange, take = hooks[name]
        out, results = matmul(name, *args, side=make_exchange(gw), **kw)
        take(results)
        return out

    h0 = jnp.concatenate([jnp.zeros((PAD, D), F32), w["meta_tokens"], x], axis=0)
    tgt_p = jnp.concatenate([jnp.zeros((HEAD_ROWS, D), F32), tgt], axis=0)

    def ffn_fwd(tag, h, wn, n=None, next_norm=None):
        if n is None:
            n = row(f"{tag}_norm", lambda i, h, wn: _rms(h, wn), [(h, D, 0)], [wn], [("new", D, BF16)])[0]
        g, u, mid = mm(f"{tag}_up", n, w[f"{tag}_w_in"], "nn", b_split=True, pair=True, outs=(F32, F32, BF16),
                       epilogue=lambda pr: (pr[0], pr[1], _silu(pr[0]) * pr[1]))
        down = functools.partial(mm, f"{tag}_down", mid, w[f"{tag}_w_out"], "nn", tj_cap=1024, tr_cap=2816, rows=[h])
        if next_norm is None:
            return down(epilogue=lambda pr, h: (h + 0.5 * pr[0],)), (h, n, g, u, mid), None
        out, n_next = down(pars=[next_norm], outs=(F32, BF16), epilogue=lambda pr, h, nw: _with_norm(h + 0.5 * pr[0], nw))
        return out, (h, n, g, u, mid), n_next

    def _with_norm(out, nw):
        return out, _rms(out, nw)

    def norm_bwd_epilogue(pr, h, dres, wn):
        _, vjp = jax.vjp(_rms, h, wn)
        dh, dw = vjp(pr[0])
        return dres + dh, dw

    def ffn_bwd(tag, dh, saved, wn):
        h, n, g, u, mid = saved
        w_in_full, w_out = w[f"{tag}_w_in_full"], w[f"{tag}_w_out"]
        halve = lambda pr: (0.5 * pr[0],)
        dmid = mm(f"{tag}_dmid", dh, w_out, "nt", ti_cap=1376, tj_cap=1408, epilogue=halve)
        dw_out = mm(f"{tag}_dwout", mid, dh, "tn", ti_cap=1408, tj_cap=1024, tr_cap=1376, epilogue=halve)

        def act_bwd(i, g, u, dmid):
            _, vjp = jax.vjp(lambda g, u: _silu(g) * u, g, u)
            return jnp.concatenate(vjp(dmid), axis=1)
        dgu = row(f"{tag}_dact", act_bwd, [(g, DFF, 0), (u, DFF, 0), (dmid, DFF, 0)], [], [("new", 2 * DFF, BF16)])[0]
        dh_in, dwn = mm(f"{tag}_dn", dgu, w_in_full, "nt", ti_cap=192, tj_cap=1024, tr_cap=2 * DFF, rows=[h, dh], pars=[wn],
                        accs=[(1, D)], epilogue=norm_bwd_epilogue)
        dw_in = mm(f"{tag}_dwin", n, dgu, "tn", ti_cap=1024, tr_cap=2752, o_split=True)
        return dh_in, dw_in, dw_out, dwn

    h1, ffn1_saved, n2 = ffn_fwd("ffn1", h0, small["ffn1_norm"], next_norm=small["mix_norm"])
    p = mm("mix_proj", n2, w["w_in_p"], "nn", ti_cap=1376, tj_cap=1152)

    cws = [w["gdn_conv_w"][j:j + 1] for j in range(CONV_K)]
    alog_row = jnp.zeros((1, LANES), F32).at[:, H:2 * H].set(small["gdn_a_log"])
    dtb_row = jnp.zeros((1, LANES), F32).at[:, H:2 * H].set(small["gdn_dt_bias"])
    q, k, v, gcb, bb, gc = gdn_pre_fwd(p, cws, alog_row, dtb_row, lp, tm)
    gc_rows = gc[:, H:2 * H].reshape(nc, CH, H).transpose(0, 2, 1)
    gdn_blocks = [(q, 0), (k, 0), (v, 0), (gcb, 0), (bb, 0), (p, OFF_Z // D)]
    y_a, gdn_states, gdn_inv = _scan_fwd("gdn_scan_fwd", _gdn_chunk, gdn_blocks, [gc_rows], [], nc,
                                         params=[small["gdn_out_norm"]], out_dtype=BF16, keep=(CH, CH))

    ct, st = _rope_tables(lp)
    ret_consts = list(_ret_consts())
    ret_blocks = [(p, OFF_RQ // D), (p, OFF_RK // D), (p, OFF_RV // D), (p, OFF_RG // D)]
    ret_w = small["ret_out_norm"].reshape(H, 1, DH)
    y_b, ret_states = _scan_fwd("ret_scan_fwd", _ret_chunk, ret_blocks, [], ret_consts, nc, shared=[ct, st],
                                params=[ret_w], out_dtype=BF16)

    br_a = matmul("branch_gdn", y_a, w["w_branch_gdn"], "nn", ti_cap=1376, tj_cap=1024)
    br_b = matmul("branch_ret", y_b, w["w_branch_ret"], "nn", ti_cap=1376, tj_cap=1024)
    merged = row("merge", lambda i, a, b, ga, gb_: _merge(a, b, ga, gb_),
                 [(br_a, D, 0), (br_b, D, 0), (p, D, OFF_GA // D), (p, D, OFF_GB // D)], [], [("new", D, BF16)])[0]
    h2, n3 = matmul("mix_out", merged, w["w_out"], "nn", ti_cap=1376, tj_cap=1024, rows=[h1], pars=[small["ffn2_norm"]],
                    outs=(F32, BF16), epilogue=lambda pr, h, nw: _with_norm(h + pr[0], nw))
    h3, ffn2_saved, _ = ffn_fwd("ffn2", h2, small["ffn2_norm"], n=n3)

    def head(i, h, t, wn):
        mask = (_row_ids(i, tm) >= HEAD_ROWS).astype(F32)
        y, vjp = jax.vjp(_rms, h, wn)
        err = (y - t) * mask
        dh, dw = vjp(err * (1.0 / D))
        return dh, dw, jnp.sum(err * err, keepdims=True).reshape(1, 1) * (0.5 / D) * jnp.ones((1, LANES), F32)
    dh3, d_final, loss_row = row("loss_head", head, [(h3, D, 0), (tgt_p, D, 0)], [small["final_norm"]], [("new", D, F32)],
                                 [(1, D), (1, LANES)])

    gs = {"final_norm": d_final}
    dh2, gw["ffn2_w_in"], gw["ffn2_w_out"], gs["ffn2_norm"] = ffn_bwd("ffn2", dh3, ffn2_saved, small["ffn2_norm"])
    dmerged = matmul("mix_out_dx", dh2, w["w_out"], "nt", ti_cap=1376, tj_cap=1024)
    gw["w_out"] = matmul("mix_out_dw", merged, dh2, "tn", ti_cap=1024, tj_cap=1024, tr_cap=2752)
    dp = lax.empty((lp, PW), BF16)

    def merge_bwd(i, dm, a, b, ga, gb_):
        _, vjp = jax.vjp(_merge, a, b, ga, gb_)
        da, db, dga, dgb = vjp(dm)
        return da, db, jnp.concatenate([dga, dgb], axis=1)
    da, db_, dp = row("merge_bwd", merge_bwd,
                      [(dmerged, D, 0), (br_a, D, 0), (br_b, D, 0), (p, D, OFF_GA // D), (p, D, OFF_GB // D)], [],
                      [("new", D, BF16), ("new", D, BF16), ("into", dp, 2 * D, OFF_GA // (2 * D))])
    dy_a = matmul("branch_gdn_dx", da, w["w_branch_gdn"], "nt", ti_cap=1376, tj_cap=1024)
    gw["w_branch_gdn"] = matmul("branch_gdn_dw", y_a, da, "tn", ti_cap=1024, tj_cap=1024, tr_cap=2752)
    dy_b = matmul("branch_ret_dx", db_, w["w_branch_ret"], "nt", ti_cap=1376, tj_cap=1024)
    gw["w_branch_ret"] = matmul("branch_ret_dw", y_b, db_, "tn", ti_cap=1024, tj_cap=1024, tr_cap=2752)

    dp, d_ret_w = _scan_bwd("ret_scan_bwd", _ret_chunk, ret_blocks, [], ret_consts, ret_states, dy_b,
                            (dp, OFF_RQ // (4 * D), [0, 1, 2, 3]), nc, shared=[ct, st], params=[ret_w])
    gs["ret_out_norm"] = d_ret_w.reshape(1, D)
    dq, dk, dv, dgcb, dbb, dp, dgc_rows, gs["gdn_out_norm"] = _scan_bwd(
        "gdn_scan_bwd", _gdn_chunk, gdn_blocks, [gc_rows], [], gdn_states, dy_a, (dp, OFF_Z // D, [5]), nc,
        params=[small["gdn_out_norm"]], kept=gdn_inv)
    dgc = jnp.pad(dgc_rows.transpose(0, 2, 1).reshape(lp, H), ((0, 0), (H, LANES - 2 * H)))
    dp, dba, dcw, dgate = gdn_pre_bwd(p, cws, alog_row, dtb_row, dq, dk, dv, dgcb, dbb, dgc, dp, lp, tm)
    dp = row("dp_ba", lambda i, t: t, [(dba, LANES, 0)], [], [("into", dp, LANES, OFF_BA // LANES)])[0]
    gw["gdn_conv_w"] = dcw[:CONV_K]
    gs["gdn_a_log"] = dgate[0:1, H:2 * H]
    gs["gdn_dt_bias"] = dgate[1:2, H:2 * H]

    dh1, gs["mix_norm"] = matmul("mix_proj_dx", dp, w["w_in_p"], "nt", tj_cap=1024, tr_cap=3456, rows=[h1, dh2],
                                 pars=[small["mix_norm"]], accs=[(1, D)], epilogue=norm_bwd_epilogue)
    gw["w_in_p"] = matmul("mix_proj_dw", n2, dp, "tn", ti_cap=1024, tj_cap=1152, tr_cap=2752)
    dh0, gw["ffn1_w_in"], gw["ffn1_w_out"], gs["ffn1_norm"] = ffn_bwd("ffn1", dh1, ffn1_saved, small["ffn1_norm"])
    gw["meta_tokens"] = dh0[PAD:HEAD_ROWS]
    return loss_row, dh0[HEAD_ROWS:], gw, gs


BIG = ("ffn1_w_in", "ffn1_w_out", "w_in", "w_branch_gdn", "w_branch_ret", "w_out", "ffn2_w_in", "ffn2_w_out")
COL_SHARDED = ("ffn1_w_in", "w_in", "ffn2_w_in")
WITH_FFN1_UP = ("ffn1_w_out", "w_in")
WITH_MIX_PROJ = ("w_branch_gdn", "w_branch_ret", "w_out", "ffn2_w_in", "ffn2_w_out")
EARLY_GRADS = ("ffn2_w_in", "ffn2_w_out", "w_in", "w_branch_gdn", "w_branch_ret", "w_out")
LATE_GRADS = ("ffn1_w_in", "ffn1_w_out")
SMALL = ("ffn1_norm", "mix_norm", "ret_out_norm", "ffn2_norm", "final_norm", "gdn_out_norm", "gdn_a_log", "gdn_dt_bias")
WEIGHTS = ("meta_tokens", "ffn1_norm", "ffn1_w_in", "ffn1_w_out", "mix_norm", "w_in", "gdn_conv_w", "gdn_a_log", "gdn_dt_bias",
           "gdn_out_norm", "ret_out_norm", "w_branch_gdn", "w_branch_ret", "w_out", "ffn2_norm", "ffn2_w_in", "ffn2_w_out",
           "final_norm")
LOSS_ROW = 6
CONV_ROW0, META_ROW0, SMALL_ROWS = 8, 24, 40


def pack_small(vals):
    rows = [vals[n].reshape(1, D) for n in SMALL[:5]]
    r5 = jnp.concatenate([vals["gdn_out_norm"].reshape(1, DH), vals["gdn_a_log"].reshape(1, H), vals["gdn_dt_bias"].reshape(1, H),
                          jnp.zeros((1, D - DH - 2 * H), F32)], axis=1)
    return jnp.concatenate(rows + [r5, jnp.zeros((2, D), F32)], axis=0)


def unpack_small(packed, shapes):
    out = {n: packed[j].reshape(shapes[n]) for j, n in enumerate(SMALL[:5])}
    out["gdn_out_norm"] = packed[5, :DH].reshape(shapes["gdn_out_norm"])
    out["gdn_a_log"] = packed[5, DH:DH + H].reshape(shapes["gdn_a_log"])
    out["gdn_dt_bias"] = packed[5, DH + H:DH + 2 * H].reshape(shapes["gdn_dt_bias"])
    return out


def kernel(x, meta_tokens, ffn1_norm, ffn1_w_in, ffn1_w_out, mix_norm, w_in, gdn_conv_w, gdn_a_log, gdn_dt_bias, gdn_out_norm, ret_out_norm, w_branch_gdn, w_branch_ret, w_out, ffn2_norm, ffn2_w_in, ffn2_w_out, final_norm, loss_target, m_meta_tokens, m_ffn1_norm, m_ffn1_w_in, m_ffn1_w_out, m_mix_norm, m_w_in, m_gdn_conv_w, m_gdn_a_log, m_gdn_dt_bias, m_gdn_out_norm, m_ret_out_norm, m_w_branch_gdn, m_w_branch_ret, m_w_out, m_ffn2_norm, m_ffn2_w_in, m_ffn2_w_out, m_final_norm, v_meta_tokens, v_ffn1_norm, v_ffn1_w_in, v_ffn1_w_out, v_mix_norm, v_w_in, v_gdn_conv_w, v_gdn_a_log, v_gdn_dt_bias, v_gdn_out_norm, v_ret_out_norm, v_w_branch_gdn, v_w_branch_ret, v_w_out, v_ffn2_norm, v_ffn2_w_in, v_ffn2_w_out, v_final_norm):
    a = dict(locals())
    wts = {n: a[n] for n in WEIGHTS}
    mom_m = {n: a["m_" + n] for n in WEIGHTS}
    mom_v = {n: a["v_" + n] for n in WEIGHTS}
    shapes = {n: wts[n].shape for n in WEIGHTS}
    flat = lambda t: t.reshape(t.shape[-2:])
    c = lax.axis_index("c")
    chip = 2 * lax.axis_index("x") + lax.axis_index("y")

    exact = jnp.zeros((16, D), F32).at[0:3].set(wts["gdn_conv_w"].reshape(3, D)).at[3:7].set(wts["meta_tokens"].reshape(4, D))
    bf16_block = lambda n: flat(wts[n]).astype(BF16)
    w = {}

    def take_weights(names):
        def take(gathered):
            for n, g in zip(names, gathered):
                if n == "w_in":
                    w["w_in_p"] = _w_in_padded_from_blocks(g)
                elif n in COL_SHARDED:
                    w[n], w[n + "_full"] = g, jnp.concatenate([g[q] for q in range(N_CHIPS)], axis=1)
                else:
                    w[n] = g.reshape(N_CHIPS * g.shape[1], D)
        return take

    first = run_exchange("gather_first", gather_exchange([bf16_block("ffn1_w_in"), exact]))
    take_weights(["ffn1_w_in"])(first[:1])
    exact = first[1]
    w["gdn_conv_w"] = jnp.concatenate([exact[q, 0:3].reshape(CONV_K, 3 * D // 4) for q in range(N_CHIPS)], axis=1)
    w["meta_tokens"] = jnp.concatenate([exact[q, 3:7].reshape(N_META, D // 4) for q in range(N_CHIPS)], axis=1)
    small = {n: wts[n].reshape(1, -1) for n in SMALL}
    hooks = {"ffn1_up": (lambda gw: gather_exchange([bf16_block(n) for n in WITH_FFN1_UP]), take_weights(WITH_FFN1_UP)),
             "mix_proj": (lambda gw: gather_exchange([bf16_block(n) for n in WITH_MIX_PROJ]), take_weights(WITH_MIX_PROJ))}

    def blocks(gw, n):
        if n == "w_in":
            return _w_in_grad_blocks(gw["w_in_p"])
        return gw[n] if n in COL_SHARDED else gw[n].reshape(N_CHIPS, gw[n].shape[0] // N_CHIPS, D)

    chip_sum, from_chips = {}, {}

    def swap_early(gw):
        chip_sum["g"] = [blocks(gw, n) for n in EARLY_GRADS]
        return sibling_halves_exchange(chip_sum["g"])

    def sum_early(from_sib):
        for n, g, r in zip(EARLY_GRADS, chip_sum.pop("g"), from_sib):
            chip_sum[n] = chip_sums(f"grads_chip_sum_{n}", g, r, c)

    def scatter_of(names):
        return (lambda gw: scatter_chips_exchange([chip_sum[n] for n in names])), (lambda got: from_chips.update(zip(names, got)))
    hooks["ffn1_dmid"] = (swap_early, sum_early)
    hooks["ffn1_dn"] = scatter_of(["w_in"])
    hooks["ffn1_dwin"] = scatter_of([n for n in EARLY_GRADS if n != "w_in"])

    loss_row, gx, gw, gs = local_step(x[0], loss_target[0], w, small, hooks)

    late = [blocks(gw, n) for n in LATE_GRADS]
    from_sib = run_exchange("grads_sibling_late", sibling_halves_exchange(late))
    for n, g, r in zip(LATE_GRADS, late, from_sib):
        chip_sum[n] = chip_sums(f"grads_chip_sum_{n}", g, r, c)
    from_chips.update(zip(LATE_GRADS, run_exchange("grads_scatter_late", scatter_chips_exchange([chip_sum[n] for n in LATE_GRADS]))))
    halves = [sum_slots(f"grads_sum_{n}", [(chip_sum[n], chip), (from_chips[n], 0), (from_chips[n], 1), (from_chips[n], 2)])
              for n in BIG]
    others = sibling_swap("grads_swap", halves)
    grads, delta, new_m, new_v = {}, {}, {}, {}
    for n, mine, other in zip(BIG, halves, others):
        res = adamw_halves(f"adamw_{n}", flat(wts[n]), mine, other, flat(mom_m[n]), flat(mom_v[n]), c)
        grads[n], delta[n], new_m[n], new_v[n] = (t.reshape(shapes[n]) for t in res)

    sm = jnp.concatenate([pack_small(gs).at[LOSS_ROW, :LANES].set(loss_row[0]),
                          gw["gdn_conv_w"].reshape(3 * CONV_K, D), jnp.zeros((META_ROW0 - CONV_ROW0 - 3 * CONV_K, D), F32),
                          gw["meta_tokens"]], axis=0)
    every = allgather_all("small_gather", sm)
    sm_sum = sum_slots("small_sum", [(every, s) for s in range(8)])
    d_s, m_s, v_s = adamw("adamw_small", pack_small(small), sm_sum[:8], pack_small({n: mom_m[n].reshape(1, -1) for n in SMALL}),
                          pack_small({n: mom_v[n].reshape(1, -1) for n in SMALL}))
    grads.update(unpack_small(sm_sum, shapes))
    delta.update(unpack_small(d_s, shapes))
    new_m.update(unpack_small(m_s, shapes))
    new_v.update(unpack_small(v_s, shapes))
    g_conv = lax.dynamic_slice_in_dim(sm_sum[CONV_ROW0:CONV_ROW0 + 3 * CONV_K].reshape(CONV_K, 3 * D), chip * (3 * D // 4), 3 * D // 4, 1)
    g_meta = lax.dynamic_slice_in_dim(sm_sum[META_ROW0:META_ROW0 + N_META], chip * (D // 4), D // 4, 1)
    for n, g in (("gdn_conv_w", g_conv), ("meta_tokens", g_meta)):
        d_, m_, v_ = adamw(f"adamw_{n}", flat(wts[n]), g, flat(mom_m[n]), flat(mom_v[n]))
        grads[n], delta[n], new_m[n], new_v[n] = (t.reshape(shapes[n]) for t in (g, d_, m_, v_))
    loss = sm_sum[LOSS_ROW, 0]

    return (loss, gx[None], *[grads[n] for n in WEIGHTS], *[delta[n] for n in WEIGHTS], *[new_m[n] for n in WEIGHTS],
            *[new_v[n] for n in WEIGHTS])
```

```python
import collections
import functools

import jax
import jax.numpy as jnp
from jax import lax
from jax.experimental import pallas as pl
from jax.experimental.pallas import tpu as pltpu

F32 = jnp.float32
BF16 = jnp.bfloat16
HI = lax.Precision.HIGHEST
MESH = pl.DeviceIdType.MESH

D = 1024
N_META = 16
PAD = 48
HEAD_ROWS = PAD + N_META
CH = 64
H = 8
DH = 128
DFF = 2816
CONV_K = 4
EPS = 1e-6
ROPE_BASE = 10000.0
LANES = 128
N_CHIPS = 4
VMEM_LIMIT = 56 * 2 ** 20
DOWN_ROWS = 688

OFF_QKV, OFF_Z, OFF_RQ, OFF_RK, OFF_RV, OFF_RG, OFF_GA, OFF_GB, OFF_BA = 0, 3072, 4096, 5120, 6144, 7168, 8192, 9216, 10240
PW = 10368
D_PROJ = 10256

ADAM_LR, ADAM_B1, ADAM_B2, ADAM_EPS, ADAM_WD, ADAM_STEP = 0.001, 0.9, 0.999, 1e-08, 0.01, 10


def _pick(n, cap, mult):
    best = None
    for t in range(mult, min(n, cap) + 1, mult):
        if n % t == 0:
            best = t
    return best if best is not None else n


def _tile_rows(rows, cols, block_bytes=3 * 2 ** 19):
    return _pick(rows, max(8, block_bytes // (4 * cols)), 8)


def _dot(a, b, dims, prec=None):
    return lax.dot_general(a, b, (dims, ((), ())), precision=prec, preferred_element_type=F32)


def _mmh(a, b):
    return _dot(a, b, ((1,), (0,)), HI)


def _split(a):
    hi = a.astype(BF16)
    return hi, (a - hi.astype(F32)).astype(BF16)


_NN, _NT, _TN = ((2,), (1,)), ((2,), (2,)), ((1,), (1,))


def _bdot(a, b, dims):
    return lax.dot_general(a, b, (dims, ((0,), (0,))), preferred_element_type=F32)


def _bmm(a, b, dims):
    return _bdot(a.astype(BF16), b.astype(BF16), dims)


def _bdot3(a, b, dims):
    ah, al = _split(a)
    bh, bl = _split(b)
    if dims == _NN:
        m = a.shape[1]
        both = _bdot(jnp.concatenate([ah, al], axis=1), bh, dims)
        return both[:, :m] + (both[:, m:] + _bdot(ah, bl, dims))
    return _bdot(ah, bh, dims) + (_bdot(ah, bl, dims) + _bdot(al, bh, dims))


def matmul(name, a, b, mode, *, ti_cap=688, tj_cap=512, tr_cap=1408, b_split=False, o_split=False, pair=False,
           rows=(), pars=(), epilogue=None, outs=(F32,), accs=(), side=None):
    if mode == "nn":
        I, R = a.shape
        J = N_CHIPS * b.shape[2] if b_split else b.shape[1]
    elif mode == "nt":
        (I, R), J = a.shape, b.shape[0]
    else:
        (R, I), J = a.shape, b.shape[1]
    ti = _pick(I, ti_cap, 16) if mode != "tn" else _pick(I, ti_cap, LANES)
    tj = _pick(J, tj_cap, LANES)
    tr = _pick(R, tr_cap, LANES) if mode != "tn" else _pick(R, tr_cap, 16)
    half = N_CHIPS // 2
    if mode == "nn":
        a_spec = pl.BlockSpec((ti, tr), lambda i, j, r: (i, r))
        if b_split:
            tj = J // N_CHIPS
            b_spec = pl.BlockSpec((None, tr, tj), lambda i, j, r: (j, r, 0))
            b2_spec = pl.BlockSpec((None, tr, tj), lambda i, j, r: (j + half, r, 0))
        else:
            b_spec = pl.BlockSpec((tr, tj), lambda i, j, r: (r, j))
        dims = ((1,), (0,))
    elif mode == "nt":
        a_spec = pl.BlockSpec((ti, tr), lambda i, j, r: (i, r))
        b_spec = pl.BlockSpec((tj, tr), lambda i, j, r: (j, r))
        dims = ((1,), (1,))
    else:
        if o_split:
            tj = J // N_CHIPS
        a_spec = pl.BlockSpec((tr, ti), lambda i, j, r: (r, i))
        b_spec = pl.BlockSpec((tr, tj), lambda i, j, r: (r, j))
        dims = ((0,), (0,))
    assert not pair or (mode == "nn" and b_split), name
    j_out = J // 2 if pair else J
    nr = R // tr
    assert I % ti == 0 and j_out % tj == 0 and R % tr == 0, (name, I, J, R, ti, tj, tr)
    grid = (I // ti, j_out // tj, nr)
    if o_split:
        assert epilogue is None, name
        o_spec = pl.BlockSpec((None, ti, tj), lambda i, j, r: (j, i, 0))
        out_shapes = [jax.ShapeDtypeStruct((N_CHIPS, I, tj), outs[0])]
    else:
        o_spec = pl.BlockSpec((ti, tj), lambda i, j, r: (i, j))
        out_shapes = [jax.ShapeDtypeStruct((I, j_out), dt) for dt in outs]
    n_prod = 2 if pair else 1
    n_out, n_acc, n_rows, n_pars = len(out_shapes), len(accs), len(rows), len(pars)
    n_si = len(side.ins) if side else 0
    n_so = len(side.out_shapes) if side else 0
    n_in = 1 + n_prod + n_rows + n_pars + n_si
    n_steps = grid[0] * grid[1] * grid[2]

    def body(*refs):
        a_ref, b_refs = refs[0], refs[1:1 + n_prod]
        row_refs = refs[1 + n_prod:1 + n_prod + n_rows]
        par_refs = refs[1 + n_prod + n_rows:1 + n_prod + n_rows + n_pars]
        out_refs = refs[n_in:n_in + n_out]
        acc_refs = refs[n_in + n_out:n_in + n_out + n_acc]
        scr = refs[n_in + n_out + n_acc + n_so:]
        partial = scr[:n_prod] if nr > 1 else ()
        step = (pl.program_id(0) * grid[1] + pl.program_id(1)) * grid[2] + pl.program_id(2)
        if side:
            s_refs = (refs[n_in - n_si:n_in], refs[n_in + n_out + n_acc:n_in + n_out + n_acc + n_so], scr[len(partial):])

        if side or n_acc:
            @pl.when(step == 0)
            def _():
                if side:
                    side.start(*s_refs)
                for r in acc_refs:
                    r[...] = jnp.zeros_like(r)

        lhs = a_ref[...].astype(BF16)
        prods = [_dot(lhs, r[...].astype(BF16), dims) for r in b_refs]

        def finish(prods):
            res = epilogue(prods, *[r[...] for r in row_refs], *[r[...] for r in par_refs]) if epilogue else prods
            for r, v in zip(out_refs, res[:n_out]):
                r[...] = v.astype(r.dtype)
            for r, v in zip(acc_refs, res[n_out:]):
                r[...] += v

        if nr == 1:
            finish(prods)
        else:
            k = pl.program_id(2)

            @pl.when(k == 0)
            def _():
                for r, v in zip(partial, prods):
                    r[...] = v

            @pl.when(k > 0)
            def _():
                for r, v in zip(partial, prods):
                    r[...] += v

            @pl.when(k == nr - 1)
            def _():
                finish([r[...] for r in partial])

        if side:
            @pl.when(step == n_steps - 1)
            def _():
                side.finish(*s_refs)

    ins = [a, b] + ([b] if pair else []) + list(rows) + list(pars)
    in_specs = [a_spec, b_spec] + ([b2_spec] if pair else []) + [o_spec] * n_rows
    in_specs += [pl.BlockSpec(p.shape, lambda i, j, r, nd=p.ndim: (0,) * nd) for p in pars]
    out_specs = [o_spec] * n_out + [pl.BlockSpec(s, lambda i, j, r: (0, 0)) for s in accs]
    out_shapes += [jax.ShapeDtypeStruct(s, F32) for s in accs]
    scratch = [pltpu.VMEM((ti, tj), F32)] * n_prod if nr > 1 else []
    ordered = bool(side) or n_acc > 0
    res = pl.pallas_call(
        body, name=name, grid=grid, in_specs=in_specs + _any_specs(n_si), out_specs=out_specs + _any_specs(n_so),
        out_shape=out_shapes + (list(side.out_shapes) if side else []), scratch_shapes=scratch + (list(side.scratch) if side else []),
        compiler_params=pltpu.CompilerParams(
            dimension_semantics=("arbitrary",) * 3 if ordered else ("parallel", "parallel", "arbitrary"), vmem_limit_bytes=VMEM_LIMIT),
    )(*ins, *(side.ins if side else []))
    own = res[:n_out + n_acc]
    own = own[0] if len(own) == 1 else tuple(own)
    return (own, list(res[n_out + n_acc:])) if side else own


def rowwise(name, fn, rows, pars, outs, accs=(), *, n_rows, tm):
    nt = n_rows // tm
    assert nt * tm == n_rows
    in_specs, ins = [], []
    for arr, w, cb in rows:
        in_specs.append(pl.BlockSpec((tm, w), lambda i, cb=cb: (i, cb)))
        ins.append(arr)
    for p in pars:
        in_specs.append(pl.BlockSpec(p.shape, lambda i, nd=p.ndim: (0,) * nd))
        ins.append(p)
    n_in = len(ins)
    out_specs, out_shapes, aliases = [], [], {}
    for k, o in enumerate(outs):
        if o[0] == "new":
            _, w, dt = o
            out_specs.append(pl.BlockSpec((tm, w), lambda i: (i, 0)))
            out_shapes.append(jax.ShapeDtypeStruct((n_rows, w), dt))
        else:
            _, arr, w, cb = o
            in_specs.append(pl.BlockSpec(memory_space=pl.ANY))
            aliases[len(ins)] = k
            ins.append(arr)
            out_specs.append(pl.BlockSpec((tm, w), lambda i, cb=cb: (i, cb)))
            out_shapes.append(jax.ShapeDtypeStruct(arr.shape, arr.dtype))
    for r, w in accs:
        out_specs.append(pl.BlockSpec((r, w), lambda i: (0, 0)))
        out_shapes.append(jax.ShapeDtypeStruct((r, w), F32))
    n_all_in, n_out, n_acc = len(ins), len(outs), len(accs)

    def body(*refs):
        i = pl.program_id(0)
        vals = [r[...] for r in refs[:n_in]]
        res = fn(i, *vals)
        if not isinstance(res, (tuple, list)):
            res = (res,)
        o_refs = refs[n_all_in:n_all_in + n_out]
        a_refs = refs[n_all_in + n_out:]
        for r, v in zip(o_refs, res[:n_out]):
            r[...] = v.astype(r.dtype)
        if n_acc:
            @pl.when(i == 0)
            def _():
                for r in a_refs:
                    r[...] = jnp.zeros_like(r)
            for r, v in zip(a_refs, res[n_out:]):
                r[...] += v

    return pl.pallas_call(
        body, name=name, grid=(nt,), in_specs=in_specs, out_specs=out_specs, out_shape=out_shapes,
        input_output_aliases=aliases,
        compiler_params=pltpu.CompilerParams(dimension_semantics=("arbitrary",), vmem_limit_bytes=VMEM_LIMIT),
    )(*ins)


def _row_ids(i, tm):
    return i * tm + lax.broadcasted_iota(jnp.int32, (tm, 1), 0)


def _sigmoid(x):
    return 1.0 / (1.0 + jnp.exp(-x))


def _silu(x):
    return x * _sigmoid(x)


def _softplus(x):
    return jnp.maximum(x, 0.0) + jnp.log(1.0 + jnp.exp(-jnp.abs(x)))


def _rms(x, w):
    return x * lax.rsqrt(jnp.mean(x * x, axis=-1, keepdims=True) + EPS) * w


def _heads(fn, *xs):
    return jnp.concatenate([fn(h, *[x[:, h * DH:(h + 1) * DH] for x in xs]) for h in range(H)], axis=1)


def _merge(a, b, ga, gb):
    return _sigmoid(ga) * a + _sigmoid(gb) * b


def _select_matrix(first_lane):
    r = lax.broadcasted_iota(jnp.int32, (LANES, H * DH), 0)
    c = lax.broadcasted_iota(jnp.int32, (LANES, H * DH), 1)
    return (r == first_lane + (c >> 7)).astype(F32)


def _chunk_tri(tm):
    r = lax.broadcasted_iota(jnp.int32, (tm, tm), 0)
    c = lax.broadcasted_iota(jnp.int32, (tm, tm), 1)
    return jnp.logical_and((r >> 6) == (c >> 6), r >= c).astype(F32)


def _gdn_gates(ba, alog_row, dtb_row, mask):
    g = -jnp.exp(alog_row) * _softplus(ba + dtb_row) * mask
    gc = _mmh(_chunk_tri(ba.shape[0]), g)
    beta = _sigmoid(ba) * mask
    return _mmh(gc, _select_matrix(H)), _mmh(beta, _select_matrix(0)), gc


def _gdn_qkv(c):
    a = _silu(c)

    def l2(scale):
        return lambda h, t: t * lax.rsqrt(jnp.sum(t * t, axis=-1, keepdims=True) + EPS) * scale
    q = _heads(l2(DH ** -0.5), a[:, :D])
    k = _heads(l2(1.0), a[:, D:2 * D])
    return q, k, a[:, 2 * D:]


def _conv_taps(xw, cws, tm):
    return sum(cws[i] * xw[5 + i:5 + i + tm] for i in range(CONV_K))


def _swap_pairs(t):
    n = t.shape[1]
    lane = lax.broadcasted_iota(jnp.int32, t.shape, 1)
    return jnp.where((lane & 1) == 0, pltpu.roll(t, n - 1, 1), pltpu.roll(t, 1, 1))


@jax.custom_vjp
def _unit_lower_inv(a):
    n = -a
    eye = (lax.broadcasted_iota(jnp.int32, a.shape, 1) == lax.broadcasted_iota(jnp.int32, a.shape, 2)).astype(F32)
    p = eye + n
    for _ in range(5):
        n = _bdot3(n, n, _NN)
        p = p + _bdot3(p, n, _NN)
    return p


def _inv_fwd(a):
    t = _unit_lower_inv(a)
    return t, t


def _inv_bwd(t, dt):
    x = _bdot3(t, dt, _TN)
    return (-_bdot3(x, t, _NT),)


_unit_lower_inv.defvjp(_inv_fwd, _inv_bwd)


@jax.custom_vjp
def _unit_lower_inv_known(a, t):
    return t


_unit_lower_inv_known.defvjp(lambda a, t: (t, t), lambda t, dt: (_inv_bwd(t, dt)[0], jnp.zeros_like(t)))


def _gdn_chunk(q, k, v, gc, bb, z, gr, s, norm_w, t_known=None):
    ri = lax.broadcasted_iota(jnp.int32, (H, CH, CH), 1)
    ci = lax.broadcasted_iota(jnp.int32, (H, CH, CH), 2)
    causal = ri >= ci
    gc1 = jnp.sum(gc, axis=2, keepdims=True) * (1.0 / LANES)
    diff = jnp.broadcast_to(gc1, (H, CH, CH)) - jnp.broadcast_to(gr, (H, CH, CH))
    decay = jnp.where(causal, jnp.exp(jnp.where(causal, diff, 0.0)), 0.0)
    kb = k * bb
    sc = _bmm(jnp.concatenate([q, kb], axis=1), k, _NT)
    qk = sc[:, :CH] * decay
    a = jnp.where(ri > ci, sc[:, CH:] * decay, 0.0)
    t = _unit_lower_inv(a) if t_known is None else _unit_lower_inv_known(a, t_known)
    eg = jnp.exp(gc)
    uw = _bmm(t, jnp.concatenate([v * bb, kb * eg], axis=2), _NN)
    g_last = gc[:, CH - 1:CH, :]
    ws = _bmm(jnp.concatenate([uw[:, :, DH:], q * eg], axis=1), s, _NN)
    v_new = uw[:, :, :DH] - ws[:, :CH]
    o = ws[:, CH:] + _bmm(qk, v_new, _NN)
    s_new = s * jnp.exp(g_last) + _bmm(k * jnp.exp(g_last - gc), v_new, _TN)
    y = o * lax.rsqrt(jnp.mean(o * o, axis=-1, keepdims=True) + EPS) * norm_w * _silu(z)
    return y, s_new, t


def _swap_pairs_heads(t):
    return _swap_pairs(t.reshape(t.shape[0] * CH, DH)).reshape(t.shape)


@jax.custom_vjp
def _rope(t, ct, st):
    return t * ct + _swap_pairs_heads(t) * st


def _rope_fwd(t, ct, st):
    return _rope(t, ct, st), (ct, st)


def _rope_bwd(res, d):
    ct, st = res
    return d * ct + _swap_pairs_heads(d * st), jnp.zeros_like(ct), jnp.zeros_like(st)


_rope.defvjp(_rope_fwd, _rope_bwd)


def _ret_chunk(q, k, v, rg, s, decay, xi, zeta, cd, ct, st, norm_w):
    q = _rope(q, ct, st)
    k = _rope(k, ct, st) * (DH ** -0.5)
    scores = _bmm(q, k, _NT) * decay
    o = _bmm(scores, v, _NN) + _bmm(q * xi, s, _NN)
    s_new = s * cd + _bmm(k * zeta, v, _TN)
    xc = o - jnp.mean(o, axis=-1, keepdims=True)
    var = jnp.mean(xc * xc, axis=-1, keepdims=True)
    y = _silu(rg) * (xc * lax.rsqrt(var + EPS) * norm_w)
    return y, s_new


def _head_blocks(ref):
    return jnp.stack([ref[:, h * DH:(h + 1) * DH] for h in range(H)])


def _head_rows(ref):
    return jnp.stack([ref[0, h:h + 1, :] for h in range(H)])


def _scan_fwd(name, chunk_fn, blocks, rowvecs, consts, nc, shared=(), params=(), out_dtype=F32, keep=None):
    n_blk, n_rv, n_c = len(blocks), len(rowvecs), len(consts) + len(shared) + len(params)

    def body(*refs):
        blk = refs[:n_blk]
        rvs = refs[n_blk:n_blk + n_rv]
        cst = refs[n_blk + n_rv:n_blk + n_rv + n_c]
        o_ref, st_ref = refs[n_blk + n_rv + n_c:n_blk + n_rv + n_c + 2]
        s_scr = refs[-1]

        @pl.when(pl.program_id(0) == 0)
        def _():
            s_scr[...] = jnp.zeros_like(s_scr)

        s = s_scr[...]
        st_ref[0] = s
        res = chunk_fn(*[_head_blocks(r) for r in blk], *[_head_rows(r) for r in rvs], s, *[r[...] for r in cst])
        for h in range(H):
            o_ref[:, h * DH:(h + 1) * DH] = res[0][h].astype(o_ref.dtype)
        s_scr[...] = res[1]
        if keep is not None:
            refs[-2][0] = res[2]

    in_specs = [pl.BlockSpec((CH, H * DH), lambda c, f=f: (c, f)) for _, f in blocks]
    in_specs += [pl.BlockSpec((1, H, CH), lambda c: (c, 0, 0)) for _ in rowvecs]
    in_specs += [pl.BlockSpec(a.shape, lambda c: (0, 0, 0)) for a in consts]
    in_specs += [pl.BlockSpec((CH, a.shape[1]), lambda c: (c, 0)) for a in shared]
    in_specs += [pl.BlockSpec(a.shape, lambda c, nd=a.ndim: (0,) * nd) for a in params]
    out_specs = [pl.BlockSpec((CH, H * DH), lambda c: (c, 0)), pl.BlockSpec((1, H, DH, DH), lambda c: (c, 0, 0, 0))]
    out_shape = [jax.ShapeDtypeStruct((nc * CH, H * DH), out_dtype), jax.ShapeDtypeStruct((nc, H, DH, DH), F32)]
    if keep is not None:
        out_specs.append(pl.BlockSpec((1, H) + keep, lambda c: (c, 0, 0, 0)))
        out_shape.append(jax.ShapeDtypeStruct((nc, H) + keep, F32))
    return pl.pallas_call(
        body, name=name, grid=(nc,), in_specs=in_specs, out_specs=out_specs, out_shape=out_shape,
        scratch_shapes=[pltpu.VMEM((H, DH, DH), F32)],
        compiler_params=pltpu.CompilerParams(dimension_semantics=("arbitrary",), vmem_limit_bytes=VMEM_LIMIT),
    )(*[a for a, _ in blocks], *rowvecs, *consts, *shared, *params)


def _scan_bwd(name, chunk_fn, blocks, rowvecs, consts, states, do, into, nc, shared=(), params=(), kept=None):
    n_blk, n_rv, n_c, n_p = len(blocks), len(rowvecs), len(consts) + len(shared), len(params)
    n_kept = 0 if kept is None else 1
    packed = list(into[2]) if into else []
    fresh = [k for k in range(n_blk) if k not in packed]
    n_bo = len(fresh) + (1 if into else 0)

    def body(*refs):
        blk = refs[:n_blk]
        rvs = refs[n_blk:n_blk + n_rv]
        cst = refs[n_blk + n_rv:n_blk + n_rv + n_c]
        par = refs[n_blk + n_rv + n_c:n_blk + n_rv + n_c + n_p]
        st_ref, do_ref = refs[n_blk + n_rv + n_c + n_p:n_blk + n_rv + n_c + n_p + 2]
        n_in = n_blk + n_rv + n_c + n_p + 2 + n_kept + (1 if into else 0)
        known = [refs[n_blk + n_rv + n_c + n_p + 2][0]] if n_kept else []
        o_refs = refs[n_in:n_in + n_bo]
        rv_refs = refs[n_in + n_bo:n_in + n_bo + n_rv]
        p_refs = refs[n_in + n_bo + n_rv:n_in + n_bo + n_rv + n_p]
        ds_scr = refs[n_in + n_bo + n_rv + n_p]

        @pl.when(pl.program_id(0) == 0)
        def _():
            ds_scr[...] = jnp.zeros_like(ds_scr)
            for r in p_refs:
                r[...] = jnp.zeros_like(r)

        cv = [r[...] for r in cst]
        n_d = n_blk + n_rv + 1
        _, vjp = jax.vjp(lambda *a: chunk_fn(*a[:n_d], *cv, *a[n_d:], *known)[:2], *[_head_blocks(r) for r in blk],
                         *[_head_rows(r) for r in rvs], st_ref[0], *[r[...] for r in par])
        grads = vjp((_head_blocks(do_ref), ds_scr[...]))
        for h in range(H):
            for pos, k in enumerate(fresh):
                o_refs[pos][:, h * DH:(h + 1) * DH] = grads[k][h]
            for pos, k in enumerate(packed):
                col = pos * H * DH + h * DH
                o_refs[-1][:, col:col + DH] = grads[k][h].astype(o_refs[-1].dtype)
            for r, g in zip(rv_refs, grads[n_blk:n_blk + n_rv]):
                r[0, h:h + 1, :] = g[h]
        ds_scr[...] = grads[n_blk + n_rv]
        for r, g in zip(p_refs, grads[n_d:]):
            r[...] += g

    rc = lambda c: nc - 1 - c
    in_specs = [pl.BlockSpec((CH, H * DH), lambda c, f=f: (rc(c), f)) for _, f in blocks]
    in_specs += [pl.BlockSpec((1, H, CH), lambda c: (rc(c), 0, 0)) for _ in rowvecs]
    in_specs += [pl.BlockSpec(a.shape, lambda c: (0, 0, 0)) for a in consts]
    in_specs += [pl.BlockSpec((CH, a.shape[1]), lambda c: (rc(c), 0)) for a in shared]
    in_specs += [pl.BlockSpec(a.shape, lambda c, nd=a.ndim: (0,) * nd) for a in params]
    in_specs += [pl.BlockSpec((1, H, DH, DH), lambda c: (rc(c), 0, 0, 0)), pl.BlockSpec((CH, H * DH), lambda c: (rc(c), 0))]
    ins = [a for a, _ in blocks] + list(rowvecs) + list(consts) + list(shared) + list(params) + [states, do]
    if n_kept:
        in_specs.append(pl.BlockSpec((1,) + kept.shape[1:], lambda c: (rc(c), 0, 0, 0)))
        ins.append(kept)
    out_specs, out_shapes, aliases = [], [], {}
    for _ in fresh:
        out_specs.append(pl.BlockSpec((CH, H * DH), lambda c: (rc(c), 0)))
        out_shapes.append(jax.ShapeDtypeStruct((nc * CH, H * DH), F32))
    if into:
        arr, f, _ = into
        in_specs.append(pl.BlockSpec(memory_space=pl.ANY))
        aliases[len(ins)] = len(fresh)
        ins.append(arr)
        out_specs.append(pl.BlockSpec((CH, len(packed) * H * DH), lambda c: (rc(c), f)))
        out_shapes.append(jax.ShapeDtypeStruct(arr.shape, arr.dtype))
    for _ in rowvecs:
        out_specs.append(pl.BlockSpec((1, H, CH), lambda c: (rc(c), 0, 0)))
        out_shapes.append(jax.ShapeDtypeStruct((nc, H, CH), F32))
    for a in params:
        out_specs.append(pl.BlockSpec(a.shape, lambda c, nd=a.ndim: (0,) * nd))
        out_shapes.append(jax.ShapeDtypeStruct(a.shape, F32))
    return pl.pallas_call(
        body, name=name, grid=(nc,), in_specs=in_specs, out_specs=out_specs, out_shape=out_shapes,
        input_output_aliases=aliases, scratch_shapes=[pltpu.VMEM((H, DH, DH), F32)],
        compiler_params=pltpu.CompilerParams(dimension_semantics=("arbitrary",), vmem_limit_bytes=VMEM_LIMIT),
    )(*ins)


def _gdn_pre_specs(p, cws, alog_row, dtb_row, tm, pos):
    sub = tm // 8
    rows = [pl.BlockSpec((tm, 3 * D), lambda i: (pos(i), 0)),
            pl.BlockSpec((8, 3 * D), lambda i: (jnp.maximum(pos(i) * sub - 1, 0), 0)),
            pl.BlockSpec((tm, LANES), lambda i: (pos(i), OFF_BA // LANES))]
    pars = [pl.BlockSpec(a.shape, lambda i: (0, 0)) for a in (*cws, alog_row, dtb_row)]
    return rows + pars, [p, p, p, *cws, alog_row, dtb_row]


def gdn_pre_fwd(p, cws, alog_row, dtb_row, lp, tm):
    def body(x_ref, prev_ref, ba_ref, c0, c1, c2, c3, al_ref, dt_ref, q_ref, k_ref, v_ref, g_ref, b_ref, gc_ref):
        i = pl.program_id(0)
        prev = jnp.where(i > 0, prev_ref[...], 0.0)
        xw = jnp.concatenate([prev, x_ref[...]], axis=0)
        c = _conv_taps(xw, [c0[...], c1[...], c2[...], c3[...]], tm)
        q, k, v = _gdn_qkv(c)
        mask = (_row_ids(i, tm) >= PAD).astype(F32)
        g, b, gc = _gdn_gates(ba_ref[...], al_ref[...], dt_ref[...], mask)
        q_ref[...] = q
        k_ref[...] = k
        v_ref[...] = v
        g_ref[...] = g
        b_ref[...] = b
        gc_ref[...] = gc

    in_specs, ins = _gdn_pre_specs(p, cws, alog_row, dtb_row, tm, lambda i: i)
    o_spec = pl.BlockSpec((tm, D), lambda i: (i, 0))
    return pl.pallas_call(
        body, name="gdn_pre_fwd", grid=(lp // tm,), in_specs=in_specs,
        out_specs=[o_spec] * 5 + [pl.BlockSpec((tm, LANES), lambda i: (i, 0))],
        out_shape=[jax.ShapeDtypeStruct((lp, D), F32)] * 5 + [jax.ShapeDtypeStruct((lp, LANES), F32)],
        compiler_params=pltpu.CompilerParams(dimension_semantics=("parallel",), vmem_limit_bytes=VMEM_LIMIT),
    )(*ins)


def gdn_pre_bwd(p, cws, alog_row, dtb_row, dq, dk, dv, dg, db, dgc, dp, lp, tm):
    nt = lp // tm
    pos = lambda i: nt - 1 - i

    def body(x_ref, prev_ref, ba_ref, c0, c1, c2, c3, al_ref, dt_ref, dq_ref, dk_ref, dv_ref, dg_ref, db_ref, dgc_ref,
             dp_any, dx_ref, dba_ref, dcw_ref, dpar_ref, carry):
        i = pl.program_id(0)
        t = pos(i)

        @pl.when(i == 0)
        def _():
            carry[...] = jnp.zeros_like(carry)
            dcw_ref[...] = jnp.zeros_like(dcw_ref)
            dpar_ref[...] = jnp.zeros_like(dpar_ref)

        cws_v = [c0[...], c1[...], c2[...], c3[...]]
        prev = jnp.where(t > 0, prev_ref[...], 0.0)
        xw = jnp.concatenate([prev, x_ref[...]], axis=0)
        xs = [xw[5 + j:5 + j + tm] for j in range(CONV_K)]
        c = sum(cws_v[j] * xs[j] for j in range(CONV_K))
        _, vjp_qkv = jax.vjp(_gdn_qkv, c)
        (dc,) = vjp_qkv((dq_ref[...], dk_ref[...], dv_ref[...]))
        zeros8 = jnp.zeros((8, 3 * D), F32)
        dcp = jnp.concatenate([zeros8, dc, zeros8], axis=0)
        dxw = sum(cws_v[j] * dcp[3 - j:3 - j + tm + 8] for j in range(CONV_K))
        dx_ref[...] = jnp.concatenate([dxw[8:tm], dxw[tm:] + carry[...]], axis=0).astype(dx_ref.dtype)
        carry[...] = dxw[:8]
        for j in range(CONV_K):
            dcw_ref[j:j + 1, :] += jnp.sum(dc * xs[j], axis=0, keepdims=True)
        mask = (_row_ids(t, tm) >= PAD).astype(F32)
        _, vjp_g = jax.vjp(lambda ba, al, dt: _gdn_gates(ba, al, dt, mask), ba_ref[...], al_ref[...], dt_ref[...])
        dba, dal, ddt = vjp_g((dg_ref[...], db_ref[...], dgc_ref[...]))
        dba_ref[...] = dba
        dpar_ref[0:1, :] += dal
        dpar_ref[1:2, :] += ddt

    in_specs, ins = _gdn_pre_specs(p, cws, alog_row, dtb_row, tm, pos)
    g_spec = pl.BlockSpec((tm, D), lambda i: (pos(i), 0))
    s_spec = pl.BlockSpec((tm, LANES), lambda i: (pos(i), 0))
    in_specs += [g_spec] * 5 + [s_spec, pl.BlockSpec(memory_space=pl.ANY)]
    ins += [dq, dk, dv, dg, db, dgc, dp]
    return pl.pallas_call(
        body, name="gdn_pre_bwd", grid=(nt,), in_specs=in_specs,
        out_specs=[pl.BlockSpec((tm, 3 * D), lambda i: (pos(i), 0)), s_spec,
                   pl.BlockSpec((8, 3 * D), lambda i: (0, 0)), pl.BlockSpec((8, LANES), lambda i: (0, 0))],
        out_shape=[jax.ShapeDtypeStruct(dp.shape, dp.dtype), jax.ShapeDtypeStruct((lp, LANES), F32),
                   jax.ShapeDtypeStruct((8, 3 * D), F32), jax.ShapeDtypeStruct((8, LANES), F32)],
        input_output_aliases={len(ins) - 1: 0},
        scratch_shapes=[pltpu.VMEM((8, 3 * D), F32)],
        compiler_params=pltpu.CompilerParams(dimension_semantics=("arbitrary",), vmem_limit_bytes=VMEM_LIMIT),
    )(*ins)


def _me():
    return lax.axis_index("x"), lax.axis_index("y"), lax.axis_index("c")


def _any_specs(n):
    return [pl.BlockSpec(memory_space=pl.ANY)] * n


Exchange = collections.namedtuple("Exchange", "ins out_shapes scratch start finish")


def _dma_sems(*shape):
    return pltpu.SemaphoreType.DMA(shape)


def run_exchange(name, ex):
    n_i, n_o = len(ex.ins), len(ex.out_shapes)

    def body(*refs):
        parts = (refs[:n_i], refs[n_i:n_i + n_o], refs[n_i + n_o:])
        ex.start(*parts)
        ex.finish(*parts)

    return pl.pallas_call(body, name=name, out_shape=list(ex.out_shapes), in_specs=_any_specs(n_i), out_specs=_any_specs(n_o),
                          scratch_shapes=list(ex.scratch))(*ex.ins)


def gather_exchange(ws):
    n = len(ws)

    def copies(w_refs, o_refs, sems):
        send_sems, recv_sems, fsend_sems, frecv_sems, osend_sems, orecv_sems = sems
        x, y, c = _me()
        me = 2 * x + y
        chips = [(1 - x, y), (x, 1 - y), (1 - x, 1 - y)]

        def own(a):
            return pltpu.make_async_remote_copy(src_ref=w_refs[a], dst_ref=o_refs[a].at[me], send_sem=osend_sems.at[a],
                                                recv_sem=orecv_sems.at[a], device_id=(x, y, 1 - c), device_id_type=MESH)

        def rows(a, cc):
            rh = ws[a].shape[0] // 2
            return pl.ds(pl.multiple_of(cc * rh, 8), rh)

        def ici(a, j, slot):
            px, py = chips[j]
            return pltpu.make_async_remote_copy(
                src_ref=w_refs[a].at[rows(a, c)], dst_ref=o_refs[a].at[slot, rows(a, c)], send_sem=send_sems.at[a, j],
                recv_sem=recv_sems.at[a, j], device_id=(px, py, c), device_id_type=MESH)

        def d2d(a, j, cc):
            px, py = chips[j]
            blk = o_refs[a].at[2 * px + py, rows(a, cc)]
            return pltpu.make_async_remote_copy(src_ref=blk, dst_ref=blk, send_sem=fsend_sems.at[a, j],
                                                recv_sem=frecv_sems.at[a, j], device_id=(x, y, 1 - c), device_id_type=MESH)
        mine = [own(a) for a in range(n)]
        sends = [ici(a, j, me) for a in range(n) for j in range(3)]
        arrivals = [ici(a, j, 2 * px + py) for a in range(n) for j, (px, py) in enumerate(chips)]
        passes = [d2d(a, j, c) for a in range(n) for j in range(3)]
        passed_to_me = [d2d(a, j, 1 - c) for a in range(n) for j in range(3)]
        return mine, sends, arrivals, passes, passed_to_me

    def start(w_refs, o_refs, sems):
        mine, sends, _, _, _ = copies(w_refs, o_refs, sems)
        for cp in mine + sends:
            cp.start()

    def finish(w_refs, o_refs, sems):
        mine, sends, arrivals, passes, passed_to_me = copies(w_refs, o_refs, sems)
        for arrival, onward in zip(arrivals, passes):
            arrival.wait_recv()
            onward.start()
        for cp in passed_to_me + mine:
            cp.wait_recv()
        for cp in sends + passes + mine:
            cp.wait_send()

    return Exchange(list(ws), [jax.ShapeDtypeStruct((N_CHIPS,) + w.shape, w.dtype) for w in ws],
                    [_dma_sems(n, 3), _dma_sems(n, 3), _dma_sems(n, 3), _dma_sems(n, 3), _dma_sems(n), _dma_sems(n)], start, finish)


def _simple_exchange(ins, out_shapes, sem_shape, copies):
    def start(i_refs, o_refs, sems):
        for cp in copies(i_refs, o_refs, *sems):
            cp.start()

    def finish(i_refs, o_refs, sems):
        cps = copies(i_refs, o_refs, *sems)
        for cp in cps:
            cp.wait_recv()
        for cp in cps:
            cp.wait_send()

    return Exchange(list(ins), out_shapes, [_dma_sems(*sem_shape), _dma_sems(*sem_shape)], start, finish)


def sibling_halves_exchange(gs):
    def copies(g_refs, o_refs, send_sems, recv_sems):
        x, y, c = _me()
        cps = []
        for a in range(len(gs)):
            rh = gs[a].shape[1] // 2
            for q in range(N_CHIPS):
                cps.append(pltpu.make_async_remote_copy(
                    src_ref=g_refs[a].at[q, pl.ds(pl.multiple_of((1 - c) * rh, 8), rh)], dst_ref=o_refs[a].at[q],
                    send_sem=send_sems.at[a, q], recv_sem=recv_sems.at[a, q], device_id=(x, y, 1 - c), device_id_type=MESH))
        return cps

    return _simple_exchange(gs, [jax.ShapeDtypeStruct((N_CHIPS, g.shape[1] // 2, g.shape[2]), g.dtype) for g in gs],
                            (len(gs), N_CHIPS), copies)


def scatter_chips_exchange(css):
    def copies(c_refs, o_refs, send_sems, recv_sems):
        x, y, c = _me()
        chips = [(1 - x, y), (x, 1 - y), (1 - x, 1 - y)]
        return [pltpu.make_async_remote_copy(
            src_ref=c_refs[a].at[2 * px + py], dst_ref=o_refs[a].at[j], send_sem=send_sems.at[a, j],
            recv_sem=recv_sems.at[a, j], device_id=(px, py, c), device_id_type=MESH)
            for a in range(len(css)) for j, (px, py) in enumerate(chips)]

    return _simple_exchange(css, [jax.ShapeDtypeStruct((3,) + cs.shape[1:], cs.dtype) for cs in css], (len(css), 3), copies)


def sibling_swap(name, halves):
    n = len(halves)

    def body(*refs):
        h_refs, o_refs = refs[:n], refs[n:2 * n]
        send_sems, recv_sems = refs[2 * n:]
        x, y, c = _me()
        cps = [pltpu.make_async_remote_copy(src_ref=h_refs[a], dst_ref=o_refs[a], send_sem=send_sems.at[a],
                                            recv_sem=recv_sems.at[a], device_id=(x, y, 1 - c), device_id_type=MESH)
               for a in range(n)]
        for cp in cps:
            cp.start()
        for cp in cps:
            cp.wait_recv()
        for cp in cps:
            cp.wait_send()

    return pl.pallas_call(
        body, name=name, out_shape=[jax.ShapeDtypeStruct(h.shape, h.dtype) for h in halves],
        in_specs=_any_specs(n), out_specs=_any_specs(n),
        scratch_shapes=[pltpu.SemaphoreType.DMA((n,)), pltpu.SemaphoreType.DMA((n,))],
    )(*halves)


def allgather_all(name, s):
    def body(s_ref, out_ref, send_sems, recv_sems, local_sem):
        x, y, c = _me()
        peers = [(x ^ ((m >> 2) & 1), y ^ ((m >> 1) & 1), c ^ (m & 1)) for m in range(1, 8)]
        mine = pltpu.make_async_copy(s_ref, out_ref.at[4 * x + 2 * y + c], local_sem)
        mine.start()

        def copy(j, slot):
            return pltpu.make_async_remote_copy(src_ref=s_ref, dst_ref=out_ref.at[slot], send_sem=send_sems.at[j],
                                                recv_sem=recv_sems.at[j], device_id=peers[j], device_id_type=MESH)
        sends = [copy(j, 4 * x + 2 * y + c) for j in range(7)]
        for cp in sends:
            cp.start()
        for j, (px, py, pc) in enumerate(peers):
            copy(j, 4 * px + 2 * py + pc).wait_recv()
        for cp in sends:
            cp.wait_send()
        mine.wait()

    return pl.pallas_call(
        body, name=name, out_shape=jax.ShapeDtypeStruct((8,) + s.shape, s.dtype),
        in_specs=_any_specs(1), out_specs=pl.BlockSpec(memory_space=pl.ANY),
        scratch_shapes=[pltpu.SemaphoreType.DMA((7,)), pltpu.SemaphoreType.DMA((7,)), pltpu.SemaphoreType.DMA(())],
    )(s)


def sum_slots(name, parts):
    n = len(parts)
    R, W = parts[0][0].shape[1:]
    tm = _tile_rows(R, W)
    idx = jnp.stack([jnp.asarray(s, jnp.int32) for _, s in parts])

    def body(idx_ref, *refs):
        acc = refs[0][...].astype(F32)
        for r in refs[1:n]:
            acc = acc + r[...].astype(F32)
        refs[n][...] = acc

    grid_spec = pltpu.PrefetchScalarGridSpec(
        num_scalar_prefetch=1, grid=(R // tm,),
        in_specs=[pl.BlockSpec((None, tm, W), lambda i, idx, k=k: (idx[k], i, 0)) for k in range(n)],
        out_specs=pl.BlockSpec((tm, W), lambda i, idx: (i, 0)))
    return pl.pallas_call(body, name=name, grid_spec=grid_spec, out_shape=jax.ShapeDtypeStruct((R, W), F32),
                          compiler_params=pltpu.CompilerParams(dimension_semantics=("parallel",)))(idx, *[a for a, _ in parts])


def chip_sums(name, g, recv, c):
    _, R, W = g.shape
    rh = R // 2
    tm = _tile_rows(rh, W)
    g8 = g.reshape(2 * N_CHIPS, rh, W)
    idx = jnp.asarray(c, jnp.int32).reshape(1)

    def body(idx_ref, g_ref, r_ref, o_ref):
        o_ref[...] = (g_ref[...] + r_ref[...]).astype(o_ref.dtype)

    grid_spec = pltpu.PrefetchScalarGridSpec(
        num_scalar_prefetch=1, grid=(N_CHIPS, rh // tm),
        in_specs=[pl.BlockSpec((None, tm, W), lambda q, i, idx: (2 * q + idx[0], i, 0)),
                  pl.BlockSpec((None, tm, W), lambda q, i, idx: (q, i, 0))],
        out_specs=pl.BlockSpec((None, tm, W), lambda q, i, idx: (q, i, 0)))
    return pl.pallas_call(body, name=name, grid_spec=grid_spec, out_shape=jax.ShapeDtypeStruct((N_CHIPS, rh, W), BF16),
                          compiler_params=pltpu.CompilerParams(dimension_semantics=("parallel", "parallel")))(idx, g8, recv)


def _adamw_update(w, g, m, v):
    m = ADAM_B1 * m + (1.0 - ADAM_B1) * g
    v = ADAM_B2 * v + (1.0 - ADAM_B2) * (g * g)
    m_hat = m / (1.0 - ADAM_B1 ** ADAM_STEP)
    v_hat = v / (1.0 - ADAM_B2 ** ADAM_STEP)
    return -ADAM_LR * (m_hat / (jnp.sqrt(v_hat) + ADAM_EPS) + ADAM_WD * w), m, v


def adamw(name, w, g, m, v):
    R, W = w.shape
    return rowwise(name, lambda i, w, g, m, v: _adamw_update(w, g, m, v), [(a, W, 0) for a in (w, g, m, v)], [],
                   [("new", W, F32)] * 3, n_rows=R, tm=_tile_rows(R, W, 2 ** 20))


def adamw_halves(name, w, g_mine, g_other, m, v, c):
    R, W = w.shape
    rh = R // 2
    tm = _tile_rows(rh, W, 2 ** 20)
    nh = rh // tm
    idx = jnp.asarray(c, jnp.int32).reshape(1)

    def body(idx_ref, w_ref, ga_ref, gb_ref, m_ref, v_ref, g_ref, d_ref, mo_ref, vo_ref):
        mine = (pl.program_id(0) // nh) == idx_ref[0]
        g = jnp.where(mine, ga_ref[...], gb_ref[...])
        d, m, v = _adamw_update(w_ref[...], g, m_ref[...], v_ref[...])
        g_ref[...] = g
        d_ref[...] = d
        mo_ref[...] = m
        vo_ref[...] = v

    full = pl.BlockSpec((tm, W), lambda i, idx: (i, 0))
    half = pl.BlockSpec((tm, W), lambda i, idx: (i % nh, 0))
    grid_spec = pltpu.PrefetchScalarGridSpec(num_scalar_prefetch=1, grid=(R // tm,), in_specs=[full, half, half, full, full],
                                             out_specs=[full] * 4)
    return pl.pallas_call(body, name=name, grid_spec=grid_spec, out_shape=[jax.ShapeDtypeStruct((R, W), F32)] * 4,
                          compiler_params=pltpu.CompilerParams(dimension_semantics=("parallel",)))(idx, w, g_mine, g_other, m, v)


W_IN_BLOCK = D_PROJ // N_CHIPS
BA_REF = OFF_RQ


def _w_in_padded_from_blocks(g):
    lo, hi = BA_REF - W_IN_BLOCK, BA_REF + 16 - W_IN_BLOCK
    return jnp.concatenate([g[0], g[1][:, :lo], g[1][:, hi:], g[2], g[3], g[1][:, lo:hi],
                            jnp.zeros((D, PW - D_PROJ), g.dtype)], axis=1)


def _w_in_grad_blocks(g):
    s = W_IN_BLOCK
    second = jnp.concatenate([g[:, s:BA_REF], g[:, OFF_BA:OFF_BA + 16], g[:, BA_REF:2 * s - 16]], axis=1)
    return jnp.stack([g[:, :s], second, g[:, 2 * s - 16:3 * s - 16], g[:, 3 * s - 16:4 * s - 16]])


def _ret_consts():
    f = F32
    log_gamma = jnp.log1p(-jnp.exp2(-5.0 - jnp.arange(H, dtype=f)))
    pos = jnp.arange(CH, dtype=f)
    causal = jnp.tril(jnp.ones((CH, CH), dtype=bool))
    diff = pos[:, None] - pos[None, :]
    decay = jnp.where(causal, jnp.exp(jnp.where(causal, diff, 0.0) * log_gamma[:, None, None]), 0.0)
    xi = jnp.broadcast_to(jnp.exp((pos + 1.0) * log_gamma[:, None])[:, :, None], (H, CH, DH))
    zeta = jnp.broadcast_to(jnp.exp((CH - 1.0 - pos) * log_gamma[:, None])[:, :, None], (H, CH, DH))
    cd = jnp.broadcast_to(jnp.exp(CH * log_gamma)[:, None, None], (H, 1, DH))
    return decay, xi, zeta, cd


def _rope_tables(lp):
    pos = jnp.arange(lp, dtype=F32) - float(PAD)
    inv = 1.0 / (ROPE_BASE ** jnp.linspace(0.0, 1.0, DH // 2, dtype=F32))
    ang = pos[:, None] * inv[None, :]
    cos, sin = jnp.cos(ang), jnp.sin(ang)
    ct = jnp.repeat(cos, 2, axis=1)
    st = jnp.stack([-sin, sin], axis=-1).reshape(lp, DH)
    return ct, st


def local_step(x, tgt, w, small, hooks=None):
    hooks = hooks or {}
    seq = x.shape[0]
    lp = HEAD_ROWS + seq
    nc = lp // CH
    tm = _pick(lp, 192, CH)
    row = functools.partial(rowwise, n_rows=lp, tm=tm)
    gw = {}

    def mm(name, *args, **kw):
        if name not in hooks:
            return matmul(name, *args, **kw)
        make_exchange, take = hooks[name]
        out, results = matmul(name, *args, side=make_exchange(gw), **kw)
        take(results)
        return out

    h0 = jnp.concatenate([jnp.zeros((PAD, D), F32), w["meta_tokens"], x], axis=0)
    tgt_p = jnp.concatenate([jnp.zeros((HEAD_ROWS, D), F32), tgt], axis=0)

    def ffn_fwd(tag, h, wn, n=None, next_norm=None, loss=None):
        if n is None:
            n = row(f"{tag}_norm", lambda i, h, wn: _rms(h, wn), [(h, D, 0)], [wn], [("new", D, BF16)])[0]
        g, u, mid = mm(f"{tag}_up", n, w[f"{tag}_w_in"], "nn", b_split=True, pair=True, outs=(F32, F32, BF16),
                       epilogue=lambda pr: (pr[0], pr[1], _silu(pr[0]) * pr[1]))
        down = functools.partial(mm, f"{tag}_down", mid, w[f"{tag}_w_out"], "nn", ti_cap=DOWN_ROWS, tj_cap=1024, tr_cap=2816,
                                 rows=[h])
        if loss is not None:
            target, final_w = loss
            ti = _pick(lp, DOWN_ROWS, 16)

            def head(pr, h, t, nw):
                mask = (_row_ids(pl.program_id(0), ti) >= HEAD_ROWS).astype(F32)
                y, vjp = jax.vjp(_rms, h + 0.5 * pr[0], nw)
                err = (y - t) * mask
                dh, dw = vjp(err * (1.0 / D))
                return dh, dw, jnp.sum(err * err, keepdims=True).reshape(1, 1) * (0.5 / D) * jnp.ones((1, LANES), F32)
            return down(rows=[h, target], pars=[final_w], accs=[(1, D), (1, LANES)], epilogue=head), (h, n, g, u, mid), None
        if next_norm is None:
            return down(epilogue=lambda pr, h: (h + 0.5 * pr[0],)), (h, n, g, u, mid), None
        out, n_next = down(pars=[next_norm], outs=(F32, BF16), epilogue=lambda pr, h, nw: _with_norm(h + 0.5 * pr[0], nw))
        return out, (h, n, g, u, mid), n_next

    def _with_norm(out, nw):
        return out, _rms(out, nw)

    def norm_bwd_epilogue(pr, h, dres, wn):
        _, vjp = jax.vjp(_rms, h, wn)
        dh, dw = vjp(pr[0])
        return dres + dh, dw

    def ffn_bwd(tag, dh, saved, wn):
        h, n, g, u, mid = saved
        w_in_full, w_out = w[f"{tag}_w_in_full"], w[f"{tag}_w_out"]
        halve = lambda pr: (0.5 * pr[0],)
        dmid = mm(f"{tag}_dmid", dh, w_out, "nt", ti_cap=1376, tj_cap=1408, epilogue=halve)
        dw_out = mm(f"{tag}_dwout", mid, dh, "tn", ti_cap=1408, tj_cap=1024, tr_cap=1376, epilogue=halve)

        def act_bwd(i, g, u, dmid):
            _, vjp = jax.vjp(lambda g, u: _silu(g) * u, g, u)
            return jnp.concatenate(vjp(dmid), axis=1)
        dgu = row(f"{tag}_dact", act_bwd, [(g, DFF, 0), (u, DFF, 0), (dmid, DFF, 0)], [], [("new", 2 * DFF, BF16)])[0]
        dh_in, dwn = mm(f"{tag}_dn", dgu, w_in_full, "nt", tj_cap=1024, tr_cap=DFF, rows=[h, dh], pars=[wn],
                        accs=[(1, D)], epilogue=norm_bwd_epilogue)
        dw_in = mm(f"{tag}_dwin", n, dgu, "tn", ti_cap=1024, tr_cap=2752, o_split=True)
        return dh_in, dw_in, dw_out, dwn

    h1, ffn1_saved, n2 = ffn_fwd("ffn1", h0, small["ffn1_norm"], next_norm=small["mix_norm"])
    p = mm("mix_proj", n2, w["w_in_p"], "nn", ti_cap=1376, tj_cap=1152)

    cws = [w["gdn_conv_w"][j:j + 1] for j in range(CONV_K)]
    alog_row = jnp.zeros((1, LANES), F32).at[:, H:2 * H].set(small["gdn_a_log"])
    dtb_row = jnp.zeros((1, LANES), F32).at[:, H:2 * H].set(small["gdn_dt_bias"])
    q, k, v, gcb, bb, gc = gdn_pre_fwd(p, cws, alog_row, dtb_row, lp, tm)
    gc_rows = gc[:, H:2 * H].reshape(nc, CH, H).transpose(0, 2, 1)
    gdn_blocks = [(q, 0), (k, 0), (v, 0), (gcb, 0), (bb, 0), (p, OFF_Z // D)]
    y_a, gdn_states, gdn_inv = _scan_fwd("gdn_scan_fwd", _gdn_chunk, gdn_blocks, [gc_rows], [], nc,
                                         params=[small["gdn_out_norm"]], out_dtype=BF16, keep=(CH, CH))

    ct, st = _rope_tables(lp)
    ret_consts = list(_ret_consts())
    ret_blocks = [(p, OFF_RQ // D), (p, OFF_RK // D), (p, OFF_RV // D), (p, OFF_RG // D)]
    ret_w = small["ret_out_norm"].reshape(H, 1, DH)
    y_b, ret_states = _scan_fwd("ret_scan_fwd", _ret_chunk, ret_blocks, [], ret_consts, nc, shared=[ct, st],
                                params=[ret_w], out_dtype=BF16)

    br_a = matmul("branch_gdn", y_a, w["w_branch_gdn"], "nn", ti_cap=1376, tj_cap=1024)
    br_b = matmul("branch_ret", y_b, w["w_branch_ret"], "nn", ti_cap=1376, tj_cap=1024)
    merged = row("merge", lambda i, a, b, ga, gb_: _merge(a, b, ga, gb_),
                 [(br_a, D, 0), (br_b, D, 0), (p, D, OFF_GA // D), (p, D, OFF_GB // D)], [], [("new", D, BF16)])[0]
    h2, n3 = matmul("mix_out", merged, w["w_out"], "nn", ti_cap=1376, tj_cap=1024, rows=[h1], pars=[small["ffn2_norm"]],
                    outs=(F32, BF16), epilogue=lambda pr, h, nw: _with_norm(h + pr[0], nw))
    (dh3, d_final, loss_row), ffn2_saved, _ = ffn_fwd("ffn2", h2, small["ffn2_norm"], n=n3, loss=(tgt_p, small["final_norm"]))

    gs = {"final_norm": d_final}
    dh2, gw["ffn2_w_in"], gw["ffn2_w_out"], gs["ffn2_norm"] = ffn_bwd("ffn2", dh3, ffn2_saved, small["ffn2_norm"])
    dmerged = matmul("mix_out_dx", dh2, w["w_out"], "nt", ti_cap=1376, tj_cap=1024)
    gw["w_out"] = matmul("mix_out_dw", merged, dh2, "tn", ti_cap=1024, tj_cap=1024, tr_cap=2752)
    dp = lax.empty((lp, PW), BF16)

    def merge_bwd(i, dm, a, b, ga, gb_):
        _, vjp = jax.vjp(_merge, a, b, ga, gb_)
        da, db, dga, dgb = vjp(dm)
        return da, db, jnp.concatenate([dga, dgb], axis=1)
    da, db_, dp = row("merge_bwd", merge_bwd,
                      [(dmerged, D, 0), (br_a, D, 0), (br_b, D, 0), (p, D, OFF_GA // D), (p, D, OFF_GB // D)], [],
                      [("new", D, BF16), ("new", D, BF16), ("into", dp, 2 * D, OFF_GA // (2 * D))])
    dy_a = matmul("branch_gdn_dx", da, w["w_branch_gdn"], "nt", ti_cap=1376, tj_cap=1024)
    gw["w_branch_gdn"] = matmul("branch_gdn_dw", y_a, da, "tn", ti_cap=1024, tj_cap=1024, tr_cap=2752)
    dy_b = matmul("branch_ret_dx", db_, w["w_branch_ret"], "nt", ti_cap=1376, tj_cap=1024)
    gw["w_branch_ret"] = matmul("branch_ret_dw", y_b, db_, "tn", ti_cap=1024, tj_cap=1024, tr_cap=2752)

    dp, d_ret_w = _scan_bwd("ret_scan_bwd", _ret_chunk, ret_blocks, [], ret_consts, ret_states, dy_b,
                            (dp, OFF_RQ // (4 * D), [0, 1, 2, 3]), nc, shared=[ct, st], params=[ret_w])
    gs["ret_out_norm"] = d_ret_w.reshape(1, D)
    dq, dk, dv, dgcb, dbb, dp, dgc_rows, gs["gdn_out_norm"] = _scan_bwd(
        "gdn_scan_bwd", _gdn_chunk, gdn_blocks, [gc_rows], [], gdn_states, dy_a, (dp, OFF_Z // D, [5]), nc,
        params=[small["gdn_out_norm"]], kept=gdn_inv)
    dgc = jnp.pad(dgc_rows.transpose(0, 2, 1).reshape(lp, H), ((0, 0), (H, LANES - 2 * H)))
    dp, dba, dcw, dgate = gdn_pre_bwd(p, cws, alog_row, dtb_row, dq, dk, dv, dgcb, dbb, dgc, dp, lp, tm)
    dp = row("dp_ba", lambda i, t: t, [(dba, LANES, 0)], [], [("into", dp, LANES, OFF_BA // LANES)])[0]
    gw["gdn_conv_w"] = dcw[:CONV_K]
    gs["gdn_a_log"] = dgate[0:1, H:2 * H]
    gs["gdn_dt_bias"] = dgate[1:2, H:2 * H]

    dh1, gs["mix_norm"] = matmul("mix_proj_dx", dp, w["w_in_p"], "nt", tj_cap=1024, tr_cap=3456, rows=[h1, dh2],
                                 pars=[small["mix_norm"]], accs=[(1, D)], epilogue=norm_bwd_epilogue)
    gw["w_in_p"] = matmul("mix_proj_dw", n2, dp, "tn", ti_cap=1024, tj_cap=1152, tr_cap=2752)
    dh0, gw["ffn1_w_in"], gw["ffn1_w_out"], gs["ffn1_norm"] = ffn_bwd("ffn1", dh1, ffn1_saved, small["ffn1_norm"])
    gw["meta_tokens"] = dh0[PAD:HEAD_ROWS]
    return loss_row, dh0[HEAD_ROWS:], gw, gs


BIG = ("ffn1_w_in", "ffn1_w_out", "w_in", "w_branch_gdn", "w_branch_ret", "w_out", "ffn2_w_in", "ffn2_w_out")
COL_SHARDED = ("ffn1_w_in", "w_in", "ffn2_w_in")
WITH_FFN1_UP = ("ffn1_w_out", "w_in")
WITH_MIX_PROJ = ("w_branch_gdn", "w_branch_ret", "w_out", "ffn2_w_in", "ffn2_w_out")
EARLY_GRADS = ("ffn2_w_in", "ffn2_w_out", "w_in", "w_branch_gdn", "w_branch_ret", "w_out")
LATE_GRADS = ("ffn1_w_in", "ffn1_w_out")
SMALL = ("ffn1_norm", "mix_norm", "ret_out_norm", "ffn2_norm", "final_norm", "gdn_out_norm", "gdn_a_log", "gdn_dt_bias")
WEIGHTS = ("meta_tokens", "ffn1_norm", "ffn1_w_in", "ffn1_w_out", "mix_norm", "w_in", "gdn_conv_w", "gdn_a_log", "gdn_dt_bias",
           "gdn_out_norm", "ret_out_norm", "w_branch_gdn", "w_branch_ret", "w_out", "ffn2_norm", "ffn2_w_in", "ffn2_w_out",
           "final_norm")
LOSS_ROW = 6
CONV_ROW0, META_ROW0, SMALL_ROWS = 8, 24, 40


def pack_small(vals):
    rows = [vals[n].reshape(1, D) for n in SMALL[:5]]
    r5 = jnp.concatenate([vals["gdn_out_norm"].reshape(1, DH), vals["gdn_a_log"].reshape(1, H), vals["gdn_dt_bias"].reshape(1, H),
                          jnp.zeros((1, D - DH - 2 * H), F32)], axis=1)
    return jnp.concatenate(rows + [r5, jnp.zeros((2, D), F32)], axis=0)


def unpack_small(packed, shapes):
    out = {n: packed[j].reshape(shapes[n]) for j, n in enumerate(SMALL[:5])}
    out["gdn_out_norm"] = packed[5, :DH].reshape(shapes["gdn_out_norm"])
    out["gdn_a_log"] = packed[5, DH:DH + H].reshape(shapes["gdn_a_log"])
    out["gdn_dt_bias"] = packed[5, DH + H:DH + 2 * H].reshape(shapes["gdn_dt_bias"])
    return out


def kernel(x, meta_tokens, ffn1_norm, ffn1_w_in, ffn1_w_out, mix_norm, w_in, gdn_conv_w, gdn_a_log, gdn_dt_bias, gdn_out_norm, ret_out_norm, w_branch_gdn, w_branch_ret, w_out, ffn2_norm, ffn2_w_in, ffn2_w_out, final_norm, loss_target, m_meta_tokens, m_ffn1_norm, m_ffn1_w_in, m_ffn1_w_out, m_mix_norm, m_w_in, m_gdn_conv_w, m_gdn_a_log, m_gdn_dt_bias, m_gdn_out_norm, m_ret_out_norm, m_w_branch_gdn, m_w_branch_ret, m_w_out, m_ffn2_norm, m_ffn2_w_in, m_ffn2_w_out, m_final_norm, v_meta_tokens, v_ffn1_norm, v_ffn1_w_in, v_ffn1_w_out, v_mix_norm, v_w_in, v_gdn_conv_w, v_gdn_a_log, v_gdn_dt_bias, v_gdn_out_norm, v_ret_out_norm, v_w_branch_gdn, v_w_branch_ret, v_w_out, v_ffn2_norm, v_ffn2_w_in, v_ffn2_w_out, v_final_norm):
    a = dict(locals())
    wts = {n: a[n] for n in WEIGHTS}
    mom_m = {n: a["m_" + n] for n in WEIGHTS}
    mom_v = {n: a["v_" + n] for n in WEIGHTS}
    shapes = {n: wts[n].shape for n in WEIGHTS}
    flat = lambda t: t.reshape(t.shape[-2:])
    c = lax.axis_index("c")
    chip = 2 * lax.axis_index("x") + lax.axis_index("y")

    exact = jnp.zeros((16, D), F32).at[0:3].set(wts["gdn_conv_w"].reshape(3, D)).at[3:7].set(wts["meta_tokens"].reshape(4, D))
    bf16_block = lambda n: flat(wts[n]).astype(BF16)
    w = {}

    def take_weights(names):
        def take(gathered):
            for n, g in zip(names, gathered):
                if n == "w_in":
                    w["w_in_p"] = _w_in_padded_from_blocks(g)
                elif n in COL_SHARDED:
                    w[n], w[n + "_full"] = g, jnp.concatenate([g[q] for q in range(N_CHIPS)], axis=1)
                else:
                    w[n] = g.reshape(N_CHIPS * g.shape[1], D)
        return take

    first = run_exchange("gather_first", gather_exchange([bf16_block("ffn1_w_in"), exact]))
    take_weights(["ffn1_w_in"])(first[:1])
    exact = first[1]
    w["gdn_conv_w"] = jnp.concatenate([exact[q, 0:3].reshape(CONV_K, 3 * D // 4) for q in range(N_CHIPS)], axis=1)
    w["meta_tokens"] = jnp.concatenate([exact[q, 3:7].reshape(N_META, D // 4) for q in range(N_CHIPS)], axis=1)
    small = {n: wts[n].reshape(1, -1) for n in SMALL}
    hooks = {"ffn1_up": (lambda gw: gather_exchange([bf16_block(n) for n in WITH_FFN1_UP]), take_weights(WITH_FFN1_UP)),
             "mix_proj": (lambda gw: gather_exchange([bf16_block(n) for n in WITH_MIX_PROJ]), take_weights(WITH_MIX_PROJ))}

    def blocks(gw, n):
        if n == "w_in":
            return _w_in_grad_blocks(gw["w_in_p"])
        return gw[n] if n in COL_SHARDED else gw[n].reshape(N_CHIPS, gw[n].shape[0] // N_CHIPS, D)

    chip_sum, from_chips = {}, {}

    def swap_early(gw):
        chip_sum["g"] = [blocks(gw, n) for n in EARLY_GRADS]
        return sibling_halves_exchange(chip_sum["g"])

    def sum_early(from_sib):
        for n, g, r in zip(EARLY_GRADS, chip_sum.pop("g"), from_sib):
            chip_sum[n] = chip_sums(f"grads_chip_sum_{n}", g, r, c)

    def scatter_of(names):
        return (lambda gw: scatter_chips_exchange([chip_sum[n] for n in names])), (lambda got: from_chips.update(zip(names, got)))
    hooks["ffn1_dmid"] = (swap_early, sum_early)
    hooks["ffn1_dn"] = scatter_of(["w_in"])
    hooks["ffn1_dwin"] = scatter_of([n for n in EARLY_GRADS if n != "w_in"])

    loss_row, gx, gw, gs = local_step(x[0], loss_target[0], w, small, hooks)

    late = [blocks(gw, n) for n in LATE_GRADS]
    from_sib = run_exchange("grads_sibling_late", sibling_halves_exchange(late))
    for n, g, r in zip(LATE_GRADS, late, from_sib):
        chip_sum[n] = chip_sums(f"grads_chip_sum_{n}", g, r, c)
    from_chips.update(zip(LATE_GRADS, run_exchange("grads_scatter_late", scatter_chips_exchange([chip_sum[n] for n in LATE_GRADS]))))
    halves = [sum_slots(f"grads_sum_{n}", [(chip_sum[n], chip), (from_chips[n], 0), (from_chips[n], 1), (from_chips[n], 2)])
              for n in BIG]
    others = sibling_swap("grads_swap", halves)
    grads, delta, new_m, new_v = {}, {}, {}, {}
    for n, mine, other in zip(BIG, halves, others):
        res = adamw_halves(f"adamw_{n}", flat(wts[n]), mine, other, flat(mom_m[n]), flat(mom_v[n]), c)
        grads[n], delta[n], new_m[n], new_v[n] = (t.reshape(shapes[n]) for t in res)

    sm = jnp.concatenate([pack_small(gs).at[LOSS_ROW, :LANES].set(loss_row[0]),
                          gw["gdn_conv_w"].reshape(3 * CONV_K, D), jnp.zeros((META_ROW0 - CONV_ROW0 - 3 * CONV_K, D), F32),
                          gw["meta_tokens"]], axis=0)
    every = allgather_all("small_gather", sm)
    sm_sum = sum_slots("small_sum", [(every, s) for s in range(8)])
    d_s, m_s, v_s = adamw("adamw_small", pack_small(small), sm_sum[:8], pack_small({n: mom_m[n].reshape(1, -1) for n in SMALL}),
                          pack_small({n: mom_v[n].reshape(1, -1) for n in SMALL}))
    grads.update(unpack_small(sm_sum, shapes))
    delta.update(unpack_small(d_s, shapes))
    new_m.update(unpack_small(m_s, shapes))
    new_v.update(unpack_small(v_s, shapes))
    g_conv = lax.dynamic_slice_in_dim(sm_sum[CONV_ROW0:CONV_ROW0 + 3 * CONV_K].reshape(CONV_K, 3 * D), chip * (3 * D // 4), 3 * D // 4, 1)
    g_meta = lax.dynamic_slice_in_dim(sm_sum[META_ROW0:META_ROW0 + N_META], chip * (D // 4), D // 4, 1)
    for n, g in (("gdn_conv_w", g_conv), ("meta_tokens", g_meta)):
        d_, m_, v_ = adamw(f"adamw_{n}", flat(wts[n]), g, flat(mom_m[n]), flat(mom_v[n]))
        grads[n], delta[n], new_m[n], new_v[n] = (t.reshape(shapes[n]) for t in (g, d_, m_, v_))
    loss = sm_sum[LOSS_ROW, 0]

    return (loss, gx[None], *[grads[n] for n in WEIGHTS], *[delta[n] for n in WEIGHTS], *[new_m[n] for n in WEIGHTS],
            *[new_v[n] for n in WEIGHTS])
```

```python
import collections
import functools

import jax
import jax.numpy as jnp
from jax import lax
from jax.experimental import pallas as pl
from jax.experimental.pallas import tpu as pltpu

F32 = jnp.float32
BF16 = jnp.bfloat16
HI = lax.Precision.HIGHEST
MESH = pl.DeviceIdType.MESH

D = 1024
N_META = 16
PAD = 48
HEAD_ROWS = PAD + N_META
CH = 64
H = 8
DH = 128
DFF = 2816
CONV_K = 4
EPS = 1e-6
ROPE_BASE = 10000.0
LANES = 128
N_CHIPS = 4
VMEM_LIMIT = 56 * 2 ** 20
DOWN_ROWS = 688
INTERLEAVED = (0, 2, 1, 3)

OFF_QKV, OFF_Z, OFF_RQ, OFF_RK, OFF_RV, OFF_RG, OFF_GA, OFF_GB, OFF_BA = 0, 3072, 4096, 5120, 6144, 7168, 8192, 9216, 10240
PW = 10368
D_PROJ = 10256

ADAM_LR, ADAM_B1, ADAM_B2, ADAM_EPS, ADAM_WD, ADAM_STEP = 0.001, 0.9, 0.999, 1e-08, 0.01, 10


def _pick(n, cap, mult):
    best = None
    for t in range(mult, min(n, cap) + 1, mult):
        if n % t == 0:
            best = t
    return best if best is not None else n


def _tile_rows(rows, cols, block_bytes=3 * 2 ** 19):
    return _pick(rows, max(8, block_bytes // (4 * cols)), 8)


def _dot(a, b, dims, prec=None):
    return lax.dot_general(a, b, (dims, ((), ())), precision=prec, preferred_element_type=F32)


def _mmh(a, b):
    return _dot(a, b, ((1,), (0,)), HI)


def _split(a):
    hi = a.astype(BF16)
    return hi, (a - hi.astype(F32)).astype(BF16)


_NN, _NT, _TN = ((2,), (1,)), ((2,), (2,)), ((1,), (1,))


def _bdot(a, b, dims):
    return lax.dot_general(a, b, (dims, ((0,), (0,))), preferred_element_type=F32)


def _bmm(a, b, dims):
    return _bdot(a.astype(BF16), b.astype(BF16), dims)


def _bdot3(a, b, dims):
    ah, al = _split(a)
    bh, bl = _split(b)
    if dims == _NN:
        m = a.shape[1]
        both = _bdot(jnp.concatenate([ah, al], axis=1), bh, dims)
        return both[:, :m] + (both[:, m:] + _bdot(ah, bl, dims))
    return _bdot(ah, bh, dims) + (_bdot(ah, bl, dims) + _bdot(al, bh, dims))


def matmul(name, a, b, mode, *, ti_cap=688, tj_cap=512, tr_cap=1408, b_split=False, o_split=False, pair=False,
           rows=(), pars=(), epilogue=None, outs=(F32,), accs=(), side=None, wide=1, o_interleaved=False):
    if mode == "nn":
        I, R = a.shape
        J = N_CHIPS * b.shape[2] if b_split else b.shape[1]
    elif mode == "nt":
        (I, R), J = a.shape, b.shape[0]
    else:
        (R, I), J = a.shape, b.shape[1]
    ti = _pick(I, ti_cap, 16) if mode != "tn" else _pick(I, ti_cap, LANES)
    tj = _pick(J, tj_cap, LANES)
    tr = _pick(R, tr_cap, LANES) if mode != "tn" else _pick(R, tr_cap, 16)
    half = N_CHIPS // 2
    if mode == "nn":
        a_spec = pl.BlockSpec((ti, tr), lambda i, j, r: (i, r))
        if b_split:
            tj = J // N_CHIPS
            b_spec = pl.BlockSpec((None, tr, tj), lambda i, j, r: (j, r, 0))
            b2_spec = pl.BlockSpec((None, tr, tj), lambda i, j, r: (j + half, r, 0))
        else:
            b_spec = pl.BlockSpec((tr, tj), lambda i, j, r: (r, j))
        dims = ((1,), (0,))
    elif mode == "nt":
        a_spec = pl.BlockSpec((ti, tr), lambda i, j, r: (i, r))
        b_spec = pl.BlockSpec((tj, tr), lambda i, j, r: (j, r))
        dims = ((1,), (1,))
    else:
        if o_split:
            tj = J // N_CHIPS
        a_spec = pl.BlockSpec((tr, ti), lambda i, j, r: (r, i))
        b_spec = pl.BlockSpec((tr, tj), lambda i, j, r: (r, j))
        dims = ((0,), (0,))
    assert not pair or (mode == "nn" and b_split), name
    j_out = J // 2 if pair else J
    nr = R // tr
    assert I % ti == 0 and j_out % tj == 0 and R % tr == 0, (name, I, J, R, ti, tj, tr)
    grid = (I // ti, j_out // tj, nr)
    w_spec = o_spec = pl.BlockSpec((ti, tj), lambda i, j, r: (i, j))
    if o_split:
        assert epilogue is None, name
        blk = (lambda j: (j % 2) * 2 + j // 2) if o_interleaved else (lambda j: j)
        w_spec = pl.BlockSpec((None, ti, tj), lambda i, j, r: (blk(j), i, 0))
        out_shapes = [jax.ShapeDtypeStruct((N_CHIPS, I, tj), outs[0])]
    else:
        w_spec = pl.BlockSpec((ti, wide * tj), lambda i, j, r: (i, j))
        out_shapes = [jax.ShapeDtypeStruct((I, wide * j_out), dt) for dt in outs]
    n_prod = 2 if pair else 1
    n_out, n_acc, n_rows, n_pars = len(out_shapes), len(accs), len(rows), len(pars)
    n_si = len(side.ins) if side else 0
    n_so = len(side.out_shapes) if side else 0
    n_in = 1 + n_prod + n_rows + n_pars + n_si
    n_steps = grid[0] * grid[1] * grid[2]

    def body(*refs):
        a_ref, b_refs = refs[0], refs[1:1 + n_prod]
        row_refs = refs[1 + n_prod:1 + n_prod + n_rows]
        par_refs = refs[1 + n_prod + n_rows:1 + n_prod + n_rows + n_pars]
        out_refs = refs[n_in:n_in + n_out]
        acc_refs = refs[n_in + n_out:n_in + n_out + n_acc]
        scr = refs[n_in + n_out + n_acc + n_so:]
        partial = scr[:n_prod] if nr > 1 else ()
        step = (pl.program_id(0) * grid[1] + pl.program_id(1)) * grid[2] + pl.program_id(2)
        if side:
            s_refs = (refs[n_in - n_si:n_in], refs[n_in + n_out + n_acc:n_in + n_out + n_acc + n_so], scr[len(partial):])

        if side or n_acc:
            @pl.when(step == 0)
            def _():
                if side:
                    side.start(*s_refs)
                for r in acc_refs:
                    r[...] = jnp.zeros_like(r)

        lhs = a_ref[...].astype(BF16)
        prods = [_dot(lhs, r[...].astype(BF16), dims) for r in b_refs]

        def finish(prods):
            res = epilogue(prods, *[r[...] for r in row_refs], *[r[...] for r in par_refs]) if epilogue else prods
            for r, v in zip(out_refs, res[:n_out]):
                r[...] = v.astype(r.dtype)
            for r, v in zip(acc_refs, res[n_out:]):
                r[...] += v

        if nr == 1:
            finish(prods)
        else:
            k = pl.program_id(2)

            @pl.when(k == 0)
            def _():
                for r, v in zip(partial, prods):
                    r[...] = v

            @pl.when(k > 0)
            def _():
                for r, v in zip(partial, prods):
                    r[...] += v

            @pl.when(k == nr - 1)
            def _():
                finish([r[...] for r in partial])

        if side:
            @pl.when(step == n_steps - 1)
            def _():
                side.finish(*s_refs)

    ins = [a, b] + ([b] if pair else []) + list(rows) + list(pars)
    in_specs = [a_spec, b_spec] + ([b2_spec] if pair else []) + [o_spec] * n_rows
    in_specs += [pl.BlockSpec(p.shape, lambda i, j, r, nd=p.ndim: (0,) * nd) for p in pars]
    out_specs = [w_spec] * n_out + [pl.BlockSpec(s, lambda i, j, r: (0, 0)) for s in accs]
    out_shapes += [jax.ShapeDtypeStruct(s, F32) for s in accs]
    scratch = [pltpu.VMEM((ti, tj), F32)] * n_prod if nr > 1 else []
    ordered = bool(side) or n_acc > 0
    res = pl.pallas_call(
        body, name=name, grid=grid, in_specs=in_specs + _any_specs(n_si), out_specs=out_specs + _any_specs(n_so),
        out_shape=out_shapes + (list(side.out_shapes) if side else []), scratch_shapes=scratch + (list(side.scratch) if side else []),
        compiler_params=pltpu.CompilerParams(
            dimension_semantics=("arbitrary",) * 3 if ordered else ("parallel", "parallel", "arbitrary"), vmem_limit_bytes=VMEM_LIMIT),
    )(*ins, *(side.ins if side else []))
    own = res[:n_out + n_acc]
    own = own[0] if len(own) == 1 else tuple(own)
    return (own, list(res[n_out + n_acc:])) if side else own


def rowwise(name, fn, rows, pars, outs, accs=(), *, n_rows, tm):
    nt = n_rows // tm
    assert nt * tm == n_rows
    in_specs, ins = [], []
    for arr, w, cb in rows:
        in_specs.append(pl.BlockSpec((tm, w), lambda i, cb=cb: (i, cb)))
        ins.append(arr)
    for p in pars:
        in_specs.append(pl.BlockSpec(p.shape, lambda i, nd=p.ndim: (0,) * nd))
        ins.append(p)
    n_in = len(ins)
    out_specs, out_shapes, aliases = [], [], {}
    for k, o in enumerate(outs):
        if o[0] == "new":
            _, w, dt = o
            out_specs.append(pl.BlockSpec((tm, w), lambda i: (i, 0)))
            out_shapes.append(jax.ShapeDtypeStruct((n_rows, w), dt))
        else:
            _, arr, w, cb = o
            in_specs.append(pl.BlockSpec(memory_space=pl.ANY))
            aliases[len(ins)] = k
            ins.append(arr)
            out_specs.append(pl.BlockSpec((tm, w), lambda i, cb=cb: (i, cb)))
            out_shapes.append(jax.ShapeDtypeStruct(arr.shape, arr.dtype))
    for r, w in accs:
        out_specs.append(pl.BlockSpec((r, w), lambda i: (0, 0)))
        out_shapes.append(jax.ShapeDtypeStruct((r, w), F32))
    n_all_in, n_out, n_acc = len(ins), len(outs), len(accs)

    def body(*refs):
        i = pl.program_id(0)
        vals = [r[...] for r in refs[:n_in]]
        res = fn(i, *vals)
        if not isinstance(res, (tuple, list)):
            res = (res,)
        o_refs = refs[n_all_in:n_all_in + n_out]
        a_refs = refs[n_all_in + n_out:]
        for r, v in zip(o_refs, res[:n_out]):
            r[...] = v.astype(r.dtype)
        if n_acc:
            @pl.when(i == 0)
            def _():
                for r in a_refs:
                    r[...] = jnp.zeros_like(r)
            for r, v in zip(a_refs, res[n_out:]):
                r[...] += v

    return pl.pallas_call(
        body, name=name, grid=(nt,), in_specs=in_specs, out_specs=out_specs, out_shape=out_shapes,
        input_output_aliases=aliases,
        compiler_params=pltpu.CompilerParams(dimension_semantics=("arbitrary",), vmem_limit_bytes=VMEM_LIMIT),
    )(*ins)


def _row_ids(i, tm):
    return i * tm + lax.broadcasted_iota(jnp.int32, (tm, 1), 0)


def _sigmoid(x):
    return 1.0 / (1.0 + jnp.exp(-x))


def _silu(x):
    return x * _sigmoid(x)


def _softplus(x):
    return jnp.maximum(x, 0.0) + jnp.log(1.0 + jnp.exp(-jnp.abs(x)))


def _rms(x, w):
    return x * lax.rsqrt(jnp.mean(x * x, axis=-1, keepdims=True) + EPS) * w


def _heads(fn, *xs):
    return jnp.concatenate([fn(h, *[x[:, h * DH:(h + 1) * DH] for x in xs]) for h in range(H)], axis=1)


def _merge(a, b, ga, gb):
    return _sigmoid(ga) * a + _sigmoid(gb) * b


def _select_matrix(first_lane):
    r = lax.broadcasted_iota(jnp.int32, (LANES, H * DH), 0)
    c = lax.broadcasted_iota(jnp.int32, (LANES, H * DH), 1)
    return (r == first_lane + (c >> 7)).astype(F32)


def _chunk_tri(tm):
    r = lax.broadcasted_iota(jnp.int32, (tm, tm), 0)
    c = lax.broadcasted_iota(jnp.int32, (tm, tm), 1)
    return jnp.logical_and((r >> 6) == (c >> 6), r >= c).astype(F32)


def _gdn_gates(ba, alog_row, dtb_row, mask):
    g = -jnp.exp(alog_row) * _softplus(ba + dtb_row) * mask
    gc = _mmh(_chunk_tri(ba.shape[0]), g)
    beta = _sigmoid(ba) * mask
    return _mmh(gc, _select_matrix(H)), _mmh(beta, _select_matrix(0)), gc


def _gdn_qkv(c):
    a = _silu(c)

    def l2(scale):
        return lambda h, t: t * lax.rsqrt(jnp.sum(t * t, axis=-1, keepdims=True) + EPS) * scale
    q = _heads(l2(DH ** -0.5), a[:, :D])
    k = _heads(l2(1.0), a[:, D:2 * D])
    return q, k, a[:, 2 * D:]


def _conv_taps(xw, cws, tm):
    return sum(cws[i] * xw[5 + i:5 + i + tm] for i in range(CONV_K))


def _swap_pairs(t):
    n = t.shape[1]
    lane = lax.broadcasted_iota(jnp.int32, t.shape, 1)
    return jnp.where((lane & 1) == 0, pltpu.roll(t, n - 1, 1), pltpu.roll(t, 1, 1))


@jax.custom_vjp
def _unit_lower_inv(a):
    n = -a
    eye = (lax.broadcasted_iota(jnp.int32, a.shape, 1) == lax.broadcasted_iota(jnp.int32, a.shape, 2)).astype(F32)
    p = eye + n
    for _ in range(5):
        n = _bdot3(n, n, _NN)
        p = p + _bdot3(p, n, _NN)
    return p


def _inv_fwd(a):
    t = _unit_lower_inv(a)
    return t, t


def _inv_bwd(t, dt):
    x = _bdot3(t, dt, _TN)
    return (-_bdot3(x, t, _NT),)


_unit_lower_inv.defvjp(_inv_fwd, _inv_bwd)


@jax.custom_vjp
def _unit_lower_inv_known(a, t):
    return t


_unit_lower_inv_known.defvjp(lambda a, t: (t, t), lambda t, dt: (_inv_bwd(t, dt)[0], jnp.zeros_like(t)))


def _gdn_chunk(q, k, v, gc, bb, z, gr, s, norm_w, t_known=None):
    ri = lax.broadcasted_iota(jnp.int32, (H, CH, CH), 1)
    ci = lax.broadcasted_iota(jnp.int32, (H, CH, CH), 2)
    causal = ri >= ci
    gc1 = jnp.sum(gc, axis=2, keepdims=True) * (1.0 / LANES)
    diff = jnp.broadcast_to(gc1, (H, CH, CH)) - jnp.broadcast_to(gr, (H, CH, CH))
    decay = jnp.where(causal, jnp.exp(jnp.where(causal, diff, 0.0)), 0.0)
    kb = k * bb
    sc = _bmm(jnp.concatenate([q, kb], axis=1), k, _NT)
    qk = sc[:, :CH] * decay
    a = jnp.where(ri > ci, sc[:, CH:] * decay, 0.0)
    t = _unit_lower_inv(a) if t_known is None else _unit_lower_inv_known(a, t_known)
    eg = jnp.exp(gc)
    uw = _bmm(t, jnp.concatenate([v * bb, kb * eg], axis=2), _NN)
    g_last = gc[:, CH - 1:CH, :]
    ws = _bmm(jnp.concatenate([uw[:, :, DH:], q * eg], axis=1), s, _NN)
    v_new = uw[:, :, :DH] - ws[:, :CH]
    o = ws[:, CH:] + _bmm(qk, v_new, _NN)
    s_new = s * jnp.exp(g_last) + _bmm(k * jnp.exp(g_last - gc), v_new, _TN)
    y = o * lax.rsqrt(jnp.mean(o * o, axis=-1, keepdims=True) + EPS) * norm_w * _silu(z)
    return y, s_new, t


def _swap_pairs_heads(t):
    return _swap_pairs(t.reshape(t.shape[0] * CH, DH)).reshape(t.shape)


@jax.custom_vjp
def _rope(t, ct, st):
    return t * ct + _swap_pairs_heads(t) * st


def _rope_fwd(t, ct, st):
    return _rope(t, ct, st), (ct, st)


def _rope_bwd(res, d):
    ct, st = res
    return d * ct + _swap_pairs_heads(d * st), jnp.zeros_like(ct), jnp.zeros_like(st)


_rope.defvjp(_rope_fwd, _rope_bwd)


def _ret_chunk(q, k, v, rg, s, decay, xi, zeta, cd, ct, st, norm_w):
    q = _rope(q, ct, st)
    k = _rope(k, ct, st) * (DH ** -0.5)
    scores = _bmm(q, k, _NT) * decay
    o = _bmm(scores, v, _NN) + _bmm(q * xi, s, _NN)
    s_new = s * cd + _bmm(k * zeta, v, _TN)
    xc = o - jnp.mean(o, axis=-1, keepdims=True)
    var = jnp.mean(xc * xc, axis=-1, keepdims=True)
    y = _silu(rg) * (xc * lax.rsqrt(var + EPS) * norm_w)
    return y, s_new


def _head_blocks(ref):
    return jnp.stack([ref[:, h * DH:(h + 1) * DH] for h in range(H)])


def _head_rows(ref):
    return jnp.stack([ref[0, h:h + 1, :] for h in range(H)])


def _scan_fwd(name, chunk_fn, blocks, rowvecs, consts, nc, shared=(), params=(), out_dtype=F32, keep=None):
    n_blk, n_rv, n_c = len(blocks), len(rowvecs), len(consts) + len(shared) + len(params)

    def body(*refs):
        blk = refs[:n_blk]
        rvs = refs[n_blk:n_blk + n_rv]
        cst = refs[n_blk + n_rv:n_blk + n_rv + n_c]
        o_ref, st_ref = refs[n_blk + n_rv + n_c:n_blk + n_rv + n_c + 2]
        s_scr = refs[-1]

        @pl.when(pl.program_id(0) == 0)
        def _():
            s_scr[...] = jnp.zeros_like(s_scr)

        s = s_scr[...]
        st_ref[0] = s
        res = chunk_fn(*[_head_blocks(r) for r in blk], *[_head_rows(r) for r in rvs], s, *[r[...] for r in cst])
        for h in range(H):
            o_ref[:, h * DH:(h + 1) * DH] = res[0][h].astype(o_ref.dtype)
        s_scr[...] = res[1]
        if keep is not None:
            refs[-2][0] = res[2]

    in_specs = [pl.BlockSpec((CH, H * DH), lambda c, f=f: (c, f)) for _, f in blocks]
    in_specs += [pl.BlockSpec((1, H, CH), lambda c: (c, 0, 0)) for _ in rowvecs]
    in_specs += [pl.BlockSpec(a.shape, lambda c: (0, 0, 0)) for a in consts]
    in_specs += [pl.BlockSpec((CH, a.shape[1]), lambda c: (c, 0)) for a in shared]
    in_specs += [pl.BlockSpec(a.shape, lambda c, nd=a.ndim: (0,) * nd) for a in params]
    out_specs = [pl.BlockSpec((CH, H * DH), lambda c: (c, 0)), pl.BlockSpec((1, H, DH, DH), lambda c: (c, 0, 0, 0))]
    out_shape = [jax.ShapeDtypeStruct((nc * CH, H * DH), out_dtype), jax.ShapeDtypeStruct((nc, H, DH, DH), F32)]
    if keep is not None:
        out_specs.append(pl.BlockSpec((1, H) + keep, lambda c: (c, 0, 0, 0)))
        out_shape.append(jax.ShapeDtypeStruct((nc, H) + keep, F32))
    return pl.pallas_call(
        body, name=name, grid=(nc,), in_specs=in_specs, out_specs=out_specs, out_shape=out_shape,
        scratch_shapes=[pltpu.VMEM((H, DH, DH), F32)],
        compiler_params=pltpu.CompilerParams(dimension_semantics=("arbitrary",), vmem_limit_bytes=VMEM_LIMIT),
    )(*[a for a, _ in blocks], *rowvecs, *consts, *shared, *params)


def _scan_bwd(name, chunk_fn, blocks, rowvecs, consts, states, do, into, nc, shared=(), params=(), kept=None):
    n_blk, n_rv, n_c, n_p = len(blocks), len(rowvecs), len(consts) + len(shared), len(params)
    n_kept = 0 if kept is None else 1
    packed = list(into[2]) if into else []
    fresh = [k for k in range(n_blk) if k not in packed]
    n_bo = len(fresh) + (1 if into else 0)

    def body(*refs):
        blk = refs[:n_blk]
        rvs = refs[n_blk:n_blk + n_rv]
        cst = refs[n_blk + n_rv:n_blk + n_rv + n_c]
        par = refs[n_blk + n_rv + n_c:n_blk + n_rv + n_c + n_p]
        st_ref, do_ref = refs[n_blk + n_rv + n_c + n_p:n_blk + n_rv + n_c + n_p + 2]
        n_in = n_blk + n_rv + n_c + n_p + 2 + n_kept + (1 if into else 0)
        known = [refs[n_blk + n_rv + n_c + n_p + 2][0]] if n_kept else []
        o_refs = refs[n_in:n_in + n_bo]
        rv_refs = refs[n_in + n_bo:n_in + n_bo + n_rv]
        p_refs = refs[n_in + n_bo + n_rv:n_in + n_bo + n_rv + n_p]
        ds_scr = refs[n_in + n_bo + n_rv + n_p]

        @pl.when(pl.program_id(0) == 0)
        def _():
            ds_scr[...] = jnp.zeros_like(ds_scr)
            for r in p_refs:
                r[...] = jnp.zeros_like(r)

        cv = [r[...] for r in cst]
        n_d = n_blk + n_rv + 1
        _, vjp = jax.vjp(lambda *a: chunk_fn(*a[:n_d], *cv, *a[n_d:], *known)[:2], *[_head_blocks(r) for r in blk],
                         *[_head_rows(r) for r in rvs], st_ref[0], *[r[...] for r in par])
        grads = vjp((_head_blocks(do_ref), ds_scr[...]))
        for h in range(H):
            for pos, k in enumerate(fresh):
                o_refs[pos][:, h * DH:(h + 1) * DH] = grads[k][h]
            for pos, k in enumerate(packed):
                col = pos * H * DH + h * DH
                o_refs[-1][:, col:col + DH] = grads[k][h].astype(o_refs[-1].dtype)
            for r, g in zip(rv_refs, grads[n_blk:n_blk + n_rv]):
                r[0, h:h + 1, :] = g[h]
        ds_scr[...] = grads[n_blk + n_rv]
        for r, g in zip(p_refs, grads[n_d:]):
            r[...] += g

    rc = lambda c: nc - 1 - c
    in_specs = [pl.BlockSpec((CH, H * DH), lambda c, f=f: (rc(c), f)) for _, f in blocks]
    in_specs += [pl.BlockSpec((1, H, CH), lambda c: (rc(c), 0, 0)) for _ in rowvecs]
    in_specs += [pl.BlockSpec(a.shape, lambda c: (0, 0, 0)) for a in consts]
    in_specs += [pl.BlockSpec((CH, a.shape[1]), lambda c: (rc(c), 0)) for a in shared]
    in_specs += [pl.BlockSpec(a.shape, lambda c, nd=a.ndim: (0,) * nd) for a in params]
    in_specs += [pl.BlockSpec((1, H, DH, DH), lambda c: (rc(c), 0, 0, 0)), pl.BlockSpec((CH, H * DH), lambda c: (rc(c), 0))]
    ins = [a for a, _ in blocks] + list(rowvecs) + list(consts) + list(shared) + list(params) + [states, do]
    if n_kept:
        in_specs.append(pl.BlockSpec((1,) + kept.shape[1:], lambda c: (rc(c), 0, 0, 0)))
        ins.append(kept)
    out_specs, out_shapes, aliases = [], [], {}
    for _ in fresh:
        out_specs.append(pl.BlockSpec((CH, H * DH), lambda c: (rc(c), 0)))
        out_shapes.append(jax.ShapeDtypeStruct((nc * CH, H * DH), F32))
    if into:
        arr, f, _ = into
        in_specs.append(pl.BlockSpec(memory_space=pl.ANY))
        aliases[len(ins)] = len(fresh)
        ins.append(arr)
        out_specs.append(pl.BlockSpec((CH, len(packed) * H * DH), lambda c: (rc(c), f)))
        out_shapes.append(jax.ShapeDtypeStruct(arr.shape, arr.dtype))
    for _ in rowvecs:
        out_specs.append(pl.BlockSpec((1, H, CH), lambda c: (rc(c), 0, 0)))
        out_shapes.append(jax.ShapeDtypeStruct((nc, H, CH), F32))
    for a in params:
        out_specs.append(pl.BlockSpec(a.shape, lambda c, nd=a.ndim: (0,) * nd))
        out_shapes.append(jax.ShapeDtypeStruct(a.shape, F32))
    return pl.pallas_call(
        body, name=name, grid=(nc,), in_specs=in_specs, out_specs=out_specs, out_shape=out_shapes,
        input_output_aliases=aliases, scratch_shapes=[pltpu.VMEM((H, DH, DH), F32)],
        compiler_params=pltpu.CompilerParams(dimension_semantics=("arbitrary",), vmem_limit_bytes=VMEM_LIMIT),
    )(*ins)


def _gdn_pre_specs(p, cws, alog_row, dtb_row, tm, pos):
    sub = tm // 8
    rows = [pl.BlockSpec((tm, 3 * D), lambda i: (pos(i), 0)),
            pl.BlockSpec((8, 3 * D), lambda i: (jnp.maximum(pos(i) * sub - 1, 0), 0)),
            pl.BlockSpec((tm, LANES), lambda i: (pos(i), OFF_BA // LANES))]
    pars = [pl.BlockSpec(a.shape, lambda i: (0, 0)) for a in (*cws, alog_row, dtb_row)]
    return rows + pars, [p, p, p, *cws, alog_row, dtb_row]


def gdn_pre_fwd(p, cws, alog_row, dtb_row, lp, tm):
    def body(x_ref, prev_ref, ba_ref, c0, c1, c2, c3, al_ref, dt_ref, q_ref, k_ref, v_ref, g_ref, b_ref, gc_ref):
        i = pl.program_id(0)
        prev = jnp.where(i > 0, prev_ref[...], 0.0)
        xw = jnp.concatenate([prev, x_ref[...]], axis=0)
        c = _conv_taps(xw, [c0[...], c1[...], c2[...], c3[...]], tm)
        q, k, v = _gdn_qkv(c)
        mask = (_row_ids(i, tm) >= PAD).astype(F32)
        g, b, gc = _gdn_gates(ba_ref[...], al_ref[...], dt_ref[...], mask)
        q_ref[...] = q
        k_ref[...] = k
        v_ref[...] = v
        g_ref[...] = g
        b_ref[...] = b
        gc_ref[...] = gc

    in_specs, ins = _gdn_pre_specs(p, cws, alog_row, dtb_row, tm, lambda i: i)
    o_spec = pl.BlockSpec((tm, D), lambda i: (i, 0))
    return pl.pallas_call(
        body, name="gdn_pre_fwd", grid=(lp // tm,), in_specs=in_specs,
        out_specs=[o_spec] * 5 + [pl.BlockSpec((tm, LANES), lambda i: (i, 0))],
        out_shape=[jax.ShapeDtypeStruct((lp, D), F32)] * 5 + [jax.ShapeDtypeStruct((lp, LANES), F32)],
        compiler_params=pltpu.CompilerParams(dimension_semantics=("parallel",), vmem_limit_bytes=VMEM_LIMIT),
    )(*ins)


def gdn_pre_bwd(p, cws, alog_row, dtb_row, dq, dk, dv, dg, db, dgc, dp, lp, tm):
    nt = lp // tm
    pos = lambda i: nt - 1 - i

    def body(x_ref, prev_ref, ba_ref, c0, c1, c2, c3, al_ref, dt_ref, dq_ref, dk_ref, dv_ref, dg_ref, db_ref, dgc_ref,
             dp_any, dx_ref, dba_ref, dcw_ref, dpar_ref, carry):
        i = pl.program_id(0)
        t = pos(i)

        @pl.when(i == 0)
        def _():
            carry[...] = jnp.zeros_like(carry)
            dcw_ref[...] = jnp.zeros_like(dcw_ref)
            dpar_ref[...] = jnp.zeros_like(dpar_ref)

        cws_v = [c0[...], c1[...], c2[...], c3[...]]
        prev = jnp.where(t > 0, prev_ref[...], 0.0)
        xw = jnp.concatenate([prev, x_ref[...]], axis=0)
        xs = [xw[5 + j:5 + j + tm] for j in range(CONV_K)]
        c = sum(cws_v[j] * xs[j] for j in range(CONV_K))
        _, vjp_qkv = jax.vjp(_gdn_qkv, c)
        (dc,) = vjp_qkv((dq_ref[...], dk_ref[...], dv_ref[...]))
        zeros8 = jnp.zeros((8, 3 * D), F32)
        dcp = jnp.concatenate([zeros8, dc, zeros8], axis=0)
        dxw = sum(cws_v[j] * dcp[3 - j:3 - j + tm + 8] for j in range(CONV_K))
        dx_ref[...] = jnp.concatenate([dxw[8:tm], dxw[tm:] + carry[...]], axis=0).astype(dx_ref.dtype)
        carry[...] = dxw[:8]
        for j in range(CONV_K):
            dcw_ref[j:j + 1, :] += jnp.sum(dc * xs[j], axis=0, keepdims=True)
        mask = (_row_ids(t, tm) >= PAD).astype(F32)
        _, vjp_g = jax.vjp(lambda ba, al, dt: _gdn_gates(ba, al, dt, mask), ba_ref[...], al_ref[...], dt_ref[...])
        dba, dal, ddt = vjp_g((dg_ref[...], db_ref[...], dgc_ref[...]))
        dba_ref[...] = dba
        dpar_ref[0:1, :] += dal
        dpar_ref[1:2, :] += ddt

    in_specs, ins = _gdn_pre_specs(p, cws, alog_row, dtb_row, tm, pos)
    g_spec = pl.BlockSpec((tm, D), lambda i: (pos(i), 0))
    s_spec = pl.BlockSpec((tm, LANES), lambda i: (pos(i), 0))
    in_specs += [g_spec] * 5 + [s_spec, pl.BlockSpec(memory_space=pl.ANY)]
    ins += [dq, dk, dv, dg, db, dgc, dp]
    return pl.pallas_call(
        body, name="gdn_pre_bwd", grid=(nt,), in_specs=in_specs,
        out_specs=[pl.BlockSpec((tm, 3 * D), lambda i: (pos(i), 0)), s_spec,
                   pl.BlockSpec((8, 3 * D), lambda i: (0, 0)), pl.BlockSpec((8, LANES), lambda i: (0, 0))],
        out_shape=[jax.ShapeDtypeStruct(dp.shape, dp.dtype), jax.ShapeDtypeStruct((lp, LANES), F32),
                   jax.ShapeDtypeStruct((8, 3 * D), F32), jax.ShapeDtypeStruct((8, LANES), F32)],
        input_output_aliases={len(ins) - 1: 0},
        scratch_shapes=[pltpu.VMEM((8, 3 * D), F32)],
        compiler_params=pltpu.CompilerParams(dimension_semantics=("arbitrary",), vmem_limit_bytes=VMEM_LIMIT),
    )(*ins)


def _me():
    return lax.axis_index("x"), lax.axis_index("y"), lax.axis_index("c")


def _any_specs(n):
    return [pl.BlockSpec(memory_space=pl.ANY)] * n


Exchange = collections.namedtuple("Exchange", "ins out_shapes scratch start finish")


def _dma_sems(*shape):
    return pltpu.SemaphoreType.DMA(shape)


def run_exchange(name, ex):
    n_i, n_o = len(ex.ins), len(ex.out_shapes)

    def body(*refs):
        parts = (refs[:n_i], refs[n_i:n_i + n_o], refs[n_i + n_o:])
        ex.start(*parts)
        ex.finish(*parts)

    return pl.pallas_call(body, name=name, out_shape=list(ex.out_shapes), in_specs=_any_specs(n_i), out_specs=_any_specs(n_o),
                          scratch_shapes=list(ex.scratch))(*ex.ins)


def gather_exchange(ws):
    n = len(ws)

    def copies(w_refs, o_refs, sems):
        send_sems, recv_sems, fsend_sems, frecv_sems, osend_sems, orecv_sems = sems
        x, y, c = _me()
        me = 2 * x + y
        chips = [(1 - x, y), (x, 1 - y), (1 - x, 1 - y)]

        def own(a):
            return pltpu.make_async_remote_copy(src_ref=w_refs[a], dst_ref=o_refs[a].at[me], send_sem=osend_sems.at[a],
                                                recv_sem=orecv_sems.at[a], device_id=(x, y, 1 - c), device_id_type=MESH)

        def rows(a, cc):
            rh = ws[a].shape[0] // 2
            return pl.ds(pl.multiple_of(cc * rh, 8), rh)

        def ici(a, j, slot):
            px, py = chips[j]
            return pltpu.make_async_remote_copy(
                src_ref=w_refs[a].at[rows(a, c)], dst_ref=o_refs[a].at[slot, rows(a, c)], send_sem=send_sems.at[a, j],
                recv_sem=recv_sems.at[a, j], device_id=(px, py, c), device_id_type=MESH)

        def d2d(a, j, cc):
            px, py = chips[j]
            blk = o_refs[a].at[2 * px + py, rows(a, cc)]
            return pltpu.make_async_remote_copy(src_ref=blk, dst_ref=blk, send_sem=fsend_sems.at[a, j],
                                                recv_sem=frecv_sems.at[a, j], device_id=(x, y, 1 - c), device_id_type=MESH)
        mine = [own(a) for a in range(n)]
        sends = [ici(a, j, me) for a in range(n) for j in range(3)]
        arrivals = [ici(a, j, 2 * px + py) for a in range(n) for j, (px, py) in enumerate(chips)]
        passes = [d2d(a, j, c) for a in range(n) for j in range(3)]
        passed_to_me = [d2d(a, j, 1 - c) for a in range(n) for j in range(3)]
        return mine, sends, arrivals, passes, passed_to_me

    def start(w_refs, o_refs, sems):
        mine, sends, _, _, _ = copies(w_refs, o_refs, sems)
        for cp in mine + sends:
            cp.start()

    def finish(w_refs, o_refs, sems):
        mine, sends, arrivals, passes, passed_to_me = copies(w_refs, o_refs, sems)
        for arrival, onward in zip(arrivals, passes):
            arrival.wait_recv()
            onward.start()
        for cp in passed_to_me + mine:
            cp.wait_recv()
        for cp in sends + passes + mine:
            cp.wait_send()

    return Exchange(list(ws), [jax.ShapeDtypeStruct((N_CHIPS,) + w.shape, w.dtype) for w in ws],
                    [_dma_sems(n, 3), _dma_sems(n, 3), _dma_sems(n, 3), _dma_sems(n, 3), _dma_sems(n), _dma_sems(n)], start, finish)


def _simple_exchange(ins, out_shapes, sem_shape, copies):
    def start(i_refs, o_refs, sems):
        for cp in copies(i_refs, o_refs, *sems):
            cp.start()

    def finish(i_refs, o_refs, sems):
        cps = copies(i_refs, o_refs, *sems)
        for cp in cps:
            cp.wait_recv()
        for cp in cps:
            cp.wait_send()

    return Exchange(list(ins), out_shapes, [_dma_sems(*sem_shape), _dma_sems(*sem_shape)], start, finish)


def sibling_halves_exchange(gs):
    def copies(g_refs, o_refs, send_sems, recv_sems):
        x, y, c = _me()
        cps = []
        for a in range(len(gs)):
            rh = gs[a].shape[1] // 2
            for q in range(N_CHIPS):
                cps.append(pltpu.make_async_remote_copy(
                    src_ref=g_refs[a].at[q, pl.ds(pl.multiple_of((1 - c) * rh, 8), rh)], dst_ref=o_refs[a].at[q],
                    send_sem=send_sems.at[a, q], recv_sem=recv_sems.at[a, q], device_id=(x, y, 1 - c), device_id_type=MESH))
        return cps

    return _simple_exchange(gs, [jax.ShapeDtypeStruct((N_CHIPS, g.shape[1] // 2, g.shape[2]), g.dtype) for g in gs],
                            (len(gs), N_CHIPS), copies)


def scatter_chips_exchange(css):
    def copies(c_refs, o_refs, send_sems, recv_sems):
        x, y, c = _me()
        chips = [(1 - x, y), (x, 1 - y), (1 - x, 1 - y)]
        return [pltpu.make_async_remote_copy(
            src_ref=c_refs[a].at[2 * px + py], dst_ref=o_refs[a].at[j], send_sem=send_sems.at[a, j],
            recv_sem=recv_sems.at[a, j], device_id=(px, py, c), device_id_type=MESH)
            for a in range(len(css)) for j, (px, py) in enumerate(chips)]

    return _simple_exchange(css, [jax.ShapeDtypeStruct((3,) + cs.shape[1:], cs.dtype) for cs in css], (len(css), 3), copies)


def sibling_swap(name, halves):
    n = len(halves)

    def body(*refs):
        h_refs, o_refs = refs[:n], refs[n:2 * n]
        send_sems, recv_sems = refs[2 * n:]
        x, y, c = _me()
        cps = [pltpu.make_async_remote_copy(src_ref=h_refs[a], dst_ref=o_refs[a], send_sem=send_sems.at[a],
                                            recv_sem=recv_sems.at[a], device_id=(x, y, 1 - c), device_id_type=MESH)
               for a in range(n)]
        for cp in cps:
            cp.start()
        for cp in cps:
            cp.wait_recv()
        for cp in cps:
            cp.wait_send()

    return pl.pallas_call(
        body, name=name, out_shape=[jax.ShapeDtypeStruct(h.shape, h.dtype) for h in halves],
        in_specs=_any_specs(n), out_specs=_any_specs(n),
        scratch_shapes=[pltpu.SemaphoreType.DMA((n,)), pltpu.SemaphoreType.DMA((n,))],
    )(*halves)


def allgather_all(name, s):
    def body(s_ref, out_ref, send_sems, recv_sems, local_sem):
        x, y, c = _me()
        peers = [(x ^ ((m >> 2) & 1), y ^ ((m >> 1) & 1), c ^ (m & 1)) for m in range(1, 8)]
        mine = pltpu.make_async_copy(s_ref, out_ref.at[4 * x + 2 * y + c], local_sem)
        mine.start()

        def copy(j, slot):
            return pltpu.make_async_remote_copy(src_ref=s_ref, dst_ref=out_ref.at[slot], send_sem=send_sems.at[j],
                                                recv_sem=recv_sems.at[j], device_id=peers[j], device_id_type=MESH)
        sends = [copy(j, 4 * x + 2 * y + c) for j in range(7)]
        for cp in sends:
            cp.start()
        for j, (px, py, pc) in enumerate(peers):
            copy(j, 4 * px + 2 * py + pc).wait_recv()
        for cp in sends:
            cp.wait_send()
        mine.wait()

    return pl.pallas_call(
        body, name=name, out_shape=jax.ShapeDtypeStruct((8,) + s.shape, s.dtype),
        in_specs=_any_specs(1), out_specs=pl.BlockSpec(memory_space=pl.ANY),
        scratch_shapes=[pltpu.SemaphoreType.DMA((7,)), pltpu.SemaphoreType.DMA((7,)), pltpu.SemaphoreType.DMA(())],
    )(s)


def sum_slots(name, parts):
    n = len(parts)
    R, W = parts[0][0].shape[1:]
    tm = _tile_rows(R, W)
    idx = jnp.stack([jnp.asarray(s, jnp.int32) for _, s in parts])

    def body(idx_ref, *refs):
        acc = refs[0][...].astype(F32)
        for r in refs[1:n]:
            acc = acc + r[...].astype(F32)
        refs[n][...] = acc

    grid_spec = pltpu.PrefetchScalarGridSpec(
        num_scalar_prefetch=1, grid=(R // tm,),
        in_specs=[pl.BlockSpec((None, tm, W), lambda i, idx, k=k: (idx[k], i, 0)) for k in range(n)],
        out_specs=pl.BlockSpec((tm, W), lambda i, idx: (i, 0)))
    return pl.pallas_call(body, name=name, grid_spec=grid_spec, out_shape=jax.ShapeDtypeStruct((R, W), F32),
                          compiler_params=pltpu.CompilerParams(dimension_semantics=("parallel",)))(idx, *[a for a, _ in parts])


def chip_sums(name, g, recv, c):
    _, R, W = g.shape
    rh = R // 2
    tm = _tile_rows(rh, W)
    g8 = g.reshape(2 * N_CHIPS, rh, W)
    idx = jnp.asarray(c, jnp.int32).reshape(1)

    def body(idx_ref, g_ref, r_ref, o_ref):
        o_ref[...] = (g_ref[...] + r_ref[...]).astype(o_ref.dtype)

    grid_spec = pltpu.PrefetchScalarGridSpec(
        num_scalar_prefetch=1, grid=(N_CHIPS, rh // tm),
        in_specs=[pl.BlockSpec((None, tm, W), lambda q, i, idx: (2 * q + idx[0], i, 0)),
                  pl.BlockSpec((None, tm, W), lambda q, i, idx: (q, i, 0))],
        out_specs=pl.BlockSpec((None, tm, W), lambda q, i, idx: (q, i, 0)))
    return pl.pallas_call(body, name=name, grid_spec=grid_spec, out_shape=jax.ShapeDtypeStruct((N_CHIPS, rh, W), BF16),
                          compiler_params=pltpu.CompilerParams(dimension_semantics=("parallel", "parallel")))(idx, g8, recv)


def _adamw_update(w, g, m, v):
    m = ADAM_B1 * m + (1.0 - ADAM_B1) * g
    v = ADAM_B2 * v + (1.0 - ADAM_B2) * (g * g)
    m_hat = m / (1.0 - ADAM_B1 ** ADAM_STEP)
    v_hat = v / (1.0 - ADAM_B2 ** ADAM_STEP)
    return -ADAM_LR * (m_hat / (jnp.sqrt(v_hat) + ADAM_EPS) + ADAM_WD * w), m, v


def adamw(name, w, g, m, v):
    R, W = w.shape
    return rowwise(name, lambda i, w, g, m, v: _adamw_update(w, g, m, v), [(a, W, 0) for a in (w, g, m, v)], [],
                   [("new", W, F32)] * 3, n_rows=R, tm=_tile_rows(R, W, 2 ** 20))


def adamw_halves(name, w, g_mine, g_other, m, v, c):
    R, W = w.shape
    rh = R // 2
    tm = _tile_rows(rh, W, 2 ** 20)
    nh = rh // tm
    idx = jnp.asarray(c, jnp.int32).reshape(1)

    def body(idx_ref, w_ref, ga_ref, gb_ref, m_ref, v_ref, g_ref, d_ref, mo_ref, vo_ref):
        mine = (pl.program_id(0) // nh) == idx_ref[0]
        g = jnp.where(mine, ga_ref[...], gb_ref[...])
        d, m, v = _adamw_update(w_ref[...], g, m_ref[...], v_ref[...])
        g_ref[...] = g
        d_ref[...] = d
        mo_ref[...] = m
        vo_ref[...] = v

    full = pl.BlockSpec((tm, W), lambda i, idx: (i, 0))
    half = pl.BlockSpec((tm, W), lambda i, idx: (i % nh, 0))
    grid_spec = pltpu.PrefetchScalarGridSpec(num_scalar_prefetch=1, grid=(R // tm,), in_specs=[full, half, half, full, full],
                                             out_specs=[full] * 4)
    return pl.pallas_call(body, name=name, grid_spec=grid_spec, out_shape=[jax.ShapeDtypeStruct((R, W), F32)] * 4,
                          compiler_params=pltpu.CompilerParams(dimension_semantics=("parallel",)))(idx, w, g_mine, g_other, m, v)


W_IN_BLOCK = D_PROJ // N_CHIPS
BA_REF = OFF_RQ


def _w_in_padded_from_blocks(g):
    lo, hi = BA_REF - W_IN_BLOCK, BA_REF + 16 - W_IN_BLOCK
    return jnp.concatenate([g[0], g[1][:, :lo], g[1][:, hi:], g[2], g[3], g[1][:, lo:hi],
                            jnp.zeros((D, PW - D_PROJ), g.dtype)], axis=1)


def _w_in_grad_blocks(g):
    s = W_IN_BLOCK
    second = jnp.concatenate([g[:, s:BA_REF], g[:, OFF_BA:OFF_BA + 16], g[:, BA_REF:2 * s - 16]], axis=1)
    return jnp.stack([g[:, :s], second, g[:, 2 * s - 16:3 * s - 16], g[:, 3 * s - 16:4 * s - 16]])


def _ret_consts():
    f = F32
    log_gamma = jnp.log1p(-jnp.exp2(-5.0 - jnp.arange(H, dtype=f)))
    pos = jnp.arange(CH, dtype=f)
    causal = jnp.tril(jnp.ones((CH, CH), dtype=bool))
    diff = pos[:, None] - pos[None, :]
    decay = jnp.where(causal, jnp.exp(jnp.where(causal, diff, 0.0) * log_gamma[:, None, None]), 0.0)
    xi = jnp.broadcast_to(jnp.exp((pos + 1.0) * log_gamma[:, None])[:, :, None], (H, CH, DH))
    zeta = jnp.broadcast_to(jnp.exp((CH - 1.0 - pos) * log_gamma[:, None])[:, :, None], (H, CH, DH))
    cd = jnp.broadcast_to(jnp.exp(CH * log_gamma)[:, None, None], (H, 1, DH))
    return decay, xi, zeta, cd


def _rope_tables(lp):
    pos = jnp.arange(lp, dtype=F32) - float(PAD)
    inv = 1.0 / (ROPE_BASE ** jnp.linspace(0.0, 1.0, DH // 2, dtype=F32))
    ang = pos[:, None] * inv[None, :]
    cos, sin = jnp.cos(ang), jnp.sin(ang)
    ct = jnp.repeat(cos, 2, axis=1)
    st = jnp.stack([-sin, sin], axis=-1).reshape(lp, DH)
    return ct, st


def local_step(x, tgt, w, small, hooks=None):
    hooks = hooks or {}
    seq = x.shape[0]
    lp = HEAD_ROWS + seq
    nc = lp // CH
    tm = _pick(lp, 192, CH)
    row = functools.partial(rowwise, n_rows=lp, tm=tm)
    gw = {}

    def mm(name, *args, **kw):
        if name not in hooks:
            return matmul(name, *args, **kw)
        make_exchange, take = hooks[name]
        out, results = matmul(name, *args, side=make_exchange(gw), **kw)
        take(results)
        return out

    h0 = jnp.concatenate([jnp.zeros((PAD, D), F32), w["meta_tokens"], x], axis=0)
    tgt_p = jnp.concatenate([jnp.zeros((HEAD_ROWS, D), F32), tgt], axis=0)

    def ffn_fwd(tag, h, wn, n=None, next_norm=None, loss=None):
        if n is None:
            n = row(f"{tag}_norm", lambda i, h, wn: _rms(h, wn), [(h, D, 0)], [wn], [("new", D, BF16)])[0]
        g, u, mid = mm(f"{tag}_up", n, w[f"{tag}_w_in"], "nn", b_split=True, pair=True, outs=(F32, F32, BF16),
                       epilogue=lambda pr: (pr[0], pr[1], _silu(pr[0]) * pr[1]))
        down = functools.partial(mm, f"{tag}_down", mid, w[f"{tag}_w_out"], "nn", ti_cap=DOWN_ROWS, tj_cap=1024, tr_cap=2816,
                                 rows=[h])
        if loss is not None:
            target, final_w = loss
            ti = _pick(lp, DOWN_ROWS, 16)

            def head(pr, h, t, nw):
                mask = (_row_ids(pl.program_id(0), ti) >= HEAD_ROWS).astype(F32)
                y, vjp = jax.vjp(_rms, h + 0.5 * pr[0], nw)
                err = (y - t) * mask
                dh, dw = vjp(err * (1.0 / D))
                return dh, dw, jnp.sum(err * err, keepdims=True).reshape(1, 1) * (0.5 / D) * jnp.ones((1, LANES), F32)
            return down(rows=[h, target], pars=[final_w], accs=[(1, D), (1, LANES)], epilogue=head), (h, n, g, u, mid), None
        if next_norm is None:
            return down(epilogue=lambda pr, h: (h + 0.5 * pr[0],)), (h, n, g, u, mid), None
        out, n_next = down(pars=[next_norm], outs=(F32, BF16), epilogue=lambda pr, h, nw: _with_norm(h + 0.5 * pr[0], nw))
        return out, (h, n, g, u, mid), n_next

    def _with_norm(out, nw):
        return out, _rms(out, nw)

    def norm_bwd_epilogue(pr, h, dres, wn):
        _, vjp = jax.vjp(_rms, h, wn)
        dh, dw = vjp(pr[0])
        return dres + dh, dw

    def ffn_bwd(tag, dh, saved, wn):
        h, n, g, u, mid = saved
        w_in_il, w_out = w[f"{tag}_w_in_il"], w[f"{tag}_w_out"]
        dw_out = mm(f"{tag}_dwout", mid, dh, "tn", ti_cap=1408, tj_cap=1024, tr_cap=1376, epilogue=lambda pr: (0.5 * pr[0],))

        def act_bwd(pr, g, u):
            _, vjp = jax.vjp(lambda g, u: _silu(g) * u, g, u)
            return (jnp.concatenate(vjp(0.5 * pr[0]), axis=1),)
        dgu = mm(f"{tag}_dmid", dh, w_out, "nt", tj_cap=2 * DFF // N_CHIPS, rows=[g, u], outs=(BF16,), wide=2, epilogue=act_bwd)
        dh_in, dwn = mm(f"{tag}_dn", dgu, w_in_il, "nt", tj_cap=1024, tr_cap=DFF, rows=[h, dh], pars=[wn],
                        accs=[(1, D)], epilogue=norm_bwd_epilogue)
        dw_in = mm(f"{tag}_dwin", n, dgu, "tn", ti_cap=1024, tr_cap=2752, o_split=True, o_interleaved=True)
        return dh_in, dw_in, dw_out, dwn

    h1, ffn1_saved, n2 = ffn_fwd("ffn1", h0, small["ffn1_norm"], next_norm=small["mix_norm"])
    p = mm("mix_proj", n2, w["w_in_p"], "nn", ti_cap=1376, tj_cap=1152)

    cws = [w["gdn_conv_w"][j:j + 1] for j in range(CONV_K)]
    alog_row = jnp.zeros((1, LANES), F32).at[:, H:2 * H].set(small["gdn_a_log"])
    dtb_row = jnp.zeros((1, LANES), F32).at[:, H:2 * H].set(small["gdn_dt_bias"])
    q, k, v, gcb, bb, gc = gdn_pre_fwd(p, cws, alog_row, dtb_row, lp, tm)
    gc_rows = gc[:, H:2 * H].reshape(nc, CH, H).transpose(0, 2, 1)
    gdn_blocks = [(q, 0), (k, 0), (v, 0), (gcb, 0), (bb, 0), (p, OFF_Z // D)]
    y_a, gdn_states, gdn_inv = _scan_fwd("gdn_scan_fwd", _gdn_chunk, gdn_blocks, [gc_rows], [], nc,
                                         params=[small["gdn_out_norm"]], out_dtype=BF16, keep=(CH, CH))

    ct, st = _rope_tables(lp)
    ret_consts = list(_ret_consts())
    ret_blocks = [(p, OFF_RQ // D), (p, OFF_RK // D), (p, OFF_RV // D), (p, OFF_RG // D)]
    ret_w = small["ret_out_norm"].reshape(H, 1, DH)
    y_b, ret_states = _scan_fwd("ret_scan_fwd", _ret_chunk, ret_blocks, [], ret_consts, nc, shared=[ct, st],
                                params=[ret_w], out_dtype=BF16)

    br_a = matmul("branch_gdn", y_a, w["w_branch_gdn"], "nn", ti_cap=1376, tj_cap=1024)
    br_b = matmul("branch_ret", y_b, w["w_branch_ret"], "nn", ti_cap=1376, tj_cap=1024)
    merged = row("merge", lambda i, a, b, ga, gb_: _merge(a, b, ga, gb_),
                 [(br_a, D, 0), (br_b, D, 0), (p, D, OFF_GA // D), (p, D, OFF_GB // D)], [], [("new", D, BF16)])[0]
    h2, n3 = matmul("mix_out", merged, w["w_out"], "nn", ti_cap=1376, tj_cap=1024, rows=[h1], pars=[small["ffn2_norm"]],
                    outs=(F32, BF16), epilogue=lambda pr, h, nw: _with_norm(h + pr[0], nw))
    (dh3, d_final, loss_row), ffn2_saved, _ = ffn_fwd("ffn2", h2, small["ffn2_norm"], n=n3, loss=(tgt_p, small["final_norm"]))

    gs = {"final_norm": d_final}
    dh2, gw["ffn2_w_in"], gw["ffn2_w_out"], gs["ffn2_norm"] = ffn_bwd("ffn2", dh3, ffn2_saved, small["ffn2_norm"])
    dmerged = matmul("mix_out_dx", dh2, w["w_out"], "nt", ti_cap=1376, tj_cap=1024)
    gw["w_out"] = matmul("mix_out_dw", merged, dh2, "tn", ti_cap=1024, tj_cap=1024, tr_cap=2752)
    dp = lax.empty((lp, PW), BF16)

    def merge_bwd(i, dm, a, b, ga, gb_):
        _, vjp = jax.vjp(_merge, a, b, ga, gb_)
        da, db, dga, dgb = vjp(dm)
        return da, db, jnp.concatenate([dga, dgb], axis=1)
    da, db_, dp = row("merge_bwd", merge_bwd,
                      [(dmerged, D, 0), (br_a, D, 0), (br_b, D, 0), (p, D, OFF_GA // D), (p, D, OFF_GB // D)], [],
                      [("new", D, BF16), ("new", D, BF16), ("into", dp, 2 * D, OFF_GA // (2 * D))])
    dy_a = matmul("branch_gdn_dx", da, w["w_branch_gdn"], "nt", ti_cap=1376, tj_cap=1024)
    gw["w_branch_gdn"] = matmul("branch_gdn_dw", y_a, da, "tn", ti_cap=1024, tj_cap=1024, tr_cap=2752)
    dy_b = matmul("branch_ret_dx", db_, w["w_branch_ret"], "nt", ti_cap=1376, tj_cap=1024)
    gw["w_branch_ret"] = matmul("branch_ret_dw", y_b, db_, "tn", ti_cap=1024, tj_cap=1024, tr_cap=2752)

    dp, d_ret_w = _scan_bwd("ret_scan_bwd", _ret_chunk, ret_blocks, [], ret_consts, ret_states, dy_b,
                            (dp, OFF_RQ // (4 * D), [0, 1, 2, 3]), nc, shared=[ct, st], params=[ret_w])
    gs["ret_out_norm"] = d_ret_w.reshape(1, D)
    dq, dk, dv, dgcb, dbb, dp, dgc_rows, gs["gdn_out_norm"] = _scan_bwd(
        "gdn_scan_bwd", _gdn_chunk, gdn_blocks, [gc_rows], [], gdn_states, dy_a, (dp, OFF_Z // D, [5]), nc,
        params=[small["gdn_out_norm"]], kept=gdn_inv)
    dgc = jnp.pad(dgc_rows.transpose(0, 2, 1).reshape(lp, H), ((0, 0), (H, LANES - 2 * H)))
    dp, dba, dcw, dgate = gdn_pre_bwd(p, cws, alog_row, dtb_row, dq, dk, dv, dgcb, dbb, dgc, dp, lp, tm)
    dp = row("dp_ba", lambda i, t: t, [(dba, LANES, 0)], [], [("into", dp, LANES, OFF_BA // LANES)])[0]
    gw["gdn_conv_w"] = dcw[:CONV_K]
    gs["gdn_a_log"] = dgate[0:1, H:2 * H]
    gs["gdn_dt_bias"] = dgate[1:2, H:2 * H]

    dh1, gs["mix_norm"] = matmul("mix_proj_dx", dp, w["w_in_p"], "nt", tj_cap=1024, tr_cap=3456, rows=[h1, dh2],
                                 pars=[small["mix_norm"]], accs=[(1, D)], epilogue=norm_bwd_epilogue)
    gw["w_in_p"] = matmul("mix_proj_dw", n2, dp, "tn", ti_cap=1024, tj_cap=1152, tr_cap=2752)
    dh0, gw["ffn1_w_in"], gw["ffn1_w_out"], gs["ffn1_norm"] = ffn_bwd("ffn1", dh1, ffn1_saved, small["ffn1_norm"])
    gw["meta_tokens"] = dh0[PAD:HEAD_ROWS]
    return loss_row, dh0[HEAD_ROWS:], gw, gs


BIG = ("ffn1_w_in", "ffn1_w_out", "w_in", "w_branch_gdn", "w_branch_ret", "w_out", "ffn2_w_in", "ffn2_w_out")
COL_SHARDED = ("ffn1_w_in", "w_in", "ffn2_w_in")
WITH_FFN1_UP = ("ffn1_w_out", "w_in")
WITH_MIX_PROJ = ("w_branch_gdn", "w_branch_ret", "w_out", "ffn2_w_in", "ffn2_w_out")
EARLY_GRADS = ("ffn2_w_in", "ffn2_w_out", "w_in", "w_branch_gdn", "w_branch_ret", "w_out")
LATE_GRADS = ("ffn1_w_in", "ffn1_w_out")
SMALL = ("ffn1_norm", "mix_norm", "ret_out_norm", "ffn2_norm", "final_norm", "gdn_out_norm", "gdn_a_log", "gdn_dt_bias")
WEIGHTS = ("meta_tokens", "ffn1_norm", "ffn1_w_in", "ffn1_w_out", "mix_norm", "w_in", "gdn_conv_w", "gdn_a_log", "gdn_dt_bias",
           "gdn_out_norm", "ret_out_norm", "w_branch_gdn", "w_branch_ret", "w_out", "ffn2_norm", "ffn2_w_in", "ffn2_w_out",
           "final_norm")
LOSS_ROW = 6
CONV_ROW0, META_ROW0, SMALL_ROWS = 8, 24, 40


def pack_small(vals):
    rows = [vals[n].reshape(1, D) for n in SMALL[:5]]
    r5 = jnp.concatenate([vals["gdn_out_norm"].reshape(1, DH), vals["gdn_a_log"].reshape(1, H), vals["gdn_dt_bias"].reshape(1, H),
                          jnp.zeros((1, D - DH - 2 * H), F32)], axis=1)
    return jnp.concatenate(rows + [r5, jnp.zeros((2, D), F32)], axis=0)


def unpack_small(packed, shapes):
    out = {n: packed[j].reshape(shapes[n]) for j, n in enumerate(SMALL[:5])}
    out["gdn_out_norm"] = packed[5, :DH].reshape(shapes["gdn_out_norm"])
    out["gdn_a_log"] = packed[5, DH:DH + H].reshape(shapes["gdn_a_log"])
    out["gdn_dt_bias"] = packed[5, DH + H:DH + 2 * H].reshape(shapes["gdn_dt_bias"])
    return out


def kernel(x, meta_tokens, ffn1_norm, ffn1_w_in, ffn1_w_out, mix_norm, w_in, gdn_conv_w, gdn_a_log, gdn_dt_bias, gdn_out_norm, ret_out_norm, w_branch_gdn, w_branch_ret, w_out, ffn2_norm, ffn2_w_in, ffn2_w_out, final_norm, loss_target, m_meta_tokens, m_ffn1_norm, m_ffn1_w_in, m_ffn1_w_out, m_mix_norm, m_w_in, m_gdn_conv_w, m_gdn_a_log, m_gdn_dt_bias, m_gdn_out_norm, m_ret_out_norm, m_w_branch_gdn, m_w_branch_ret, m_w_out, m_ffn2_norm, m_ffn2_w_in, m_ffn2_w_out, m_final_norm, v_meta_tokens, v_ffn1_norm, v_ffn1_w_in, v_ffn1_w_out, v_mix_norm, v_w_in, v_gdn_conv_w, v_gdn_a_log, v_gdn_dt_bias, v_gdn_out_norm, v_ret_out_norm, v_w_branch_gdn, v_w_branch_ret, v_w_out, v_ffn2_norm, v_ffn2_w_in, v_ffn2_w_out, v_final_norm):
    a = dict(locals())
    wts = {n: a[n] for n in WEIGHTS}
    mom_m = {n: a["m_" + n] for n in WEIGHTS}
    mom_v = {n: a["v_" + n] for n in WEIGHTS}
    shapes = {n: wts[n].shape for n in WEIGHTS}
    flat = lambda t: t.reshape(t.shape[-2:])
    c = lax.axis_index("c")
    chip = 2 * lax.axis_index("x") + lax.axis_index("y")

    exact = jnp.zeros((16, D), F32).at[0:3].set(wts["gdn_conv_w"].reshape(3, D)).at[3:7].set(wts["meta_tokens"].reshape(4, D))
    bf16_block = lambda n: flat(wts[n]).astype(BF16)
    w = {}

    def take_weights(names):
        def take(gathered):
            for n, g in zip(names, gathered):
                if n == "w_in":
                    w["w_in_p"] = _w_in_padded_from_blocks(g)
                elif n in COL_SHARDED:
                    w[n], w[n + "_il"] = g, jnp.concatenate([g[q] for q in INTERLEAVED], axis=1)
                else:
                    w[n] = g.reshape(N_CHIPS * g.shape[1], D)
        return take

    first = run_exchange("gather_first", gather_exchange([bf16_block("ffn1_w_in"), exact]))
    take_weights(["ffn1_w_in"])(first[:1])
    exact = first[1]
    w["gdn_conv_w"] = jnp.concatenate([exact[q, 0:3].reshape(CONV_K, 3 * D // 4) for q in range(N_CHIPS)], axis=1)
    w["meta_tokens"] = jnp.concatenate([exact[q, 3:7].reshape(N_META, D // 4) for q in range(N_CHIPS)], axis=1)
    small = {n: wts[n].reshape(1, -1) for n in SMALL}
    hooks = {"ffn1_up": (lambda gw: gather_exchange([bf16_block(n) for n in WITH_FFN1_UP]), take_weights(WITH_FFN1_UP)),
             "mix_proj": (lambda gw: gather_exchange([bf16_block(n) for n in WITH_MIX_PROJ]), take_weights(WITH_MIX_PROJ))}

    def blocks(gw, n):
        if n == "w_in":
            return _w_in_grad_blocks(gw["w_in_p"])
        return gw[n] if n in COL_SHARDED else gw[n].reshape(N_CHIPS, gw[n].shape[0] // N_CHIPS, D)

    chip_sum, from_chips = {}, {}

    def swap_early(gw):
        chip_sum["g"] = [blocks(gw, n) for n in EARLY_GRADS]
        return sibling_halves_exchange(chip_sum["g"])

    def sum_early(from_sib):
        for n, g, r in zip(EARLY_GRADS, chip_sum.pop("g"), from_sib):
            chip_sum[n] = chip_sums(f"grads_chip_sum_{n}", g, r, c)

    def scatter_of(names):
        return (lambda gw: scatter_chips_exchange([chip_sum[n] for n in names])), (lambda got: from_chips.update(zip(names, got)))
    hooks["ffn1_dmid"] = (swap_early, sum_early)
    hooks["ffn1_dn"] = scatter_of(["w_in"])
    hooks["ffn1_dwin"] = scatter_of([n for n in EARLY_GRADS if n != "w_in"])

    loss_row, gx, gw, gs = local_step(x[0], loss_target[0], w, small, hooks)

    late = [blocks(gw, n) for n in LATE_GRADS]
    from_sib = run_exchange("grads_sibling_late", sibling_halves_exchange(late))
    for n, g, r in zip(LATE_GRADS, late, from_sib):
        chip_sum[n] = chip_sums(f"grads_chip_sum_{n}", g, r, c)
    from_chips.update(zip(LATE_GRADS, run_exchange("grads_scatter_late", scatter_chips_exchange([chip_sum[n] for n in LATE_GRADS]))))
    halves = [sum_slots(f"grads_sum_{n}", [(chip_sum[n], chip), (from_chips[n], 0), (from_chips[n], 1), (from_chips[n], 2)])
              for n in BIG]
    others = sibling_swap("grads_swap", halves)
    grads, delta, new_m, new_v = {}, {}, {}, {}
    for n, mine, other in zip(BIG, halves, others):
        res = adamw_halves(f"adamw_{n}", flat(wts[n]), mine, other, flat(mom_m[n]), flat(mom_v[n]), c)
        grads[n], delta[n], new_m[n], new_v[n] = (t.reshape(shapes[n]) for t in res)

    sm = jnp.concatenate([pack_small(gs).at[LOSS_ROW, :LANES].set(loss_row[0]),
                          gw["gdn_conv_w"].reshape(3 * CONV_K, D), jnp.zeros((META_ROW0 - CONV_ROW0 - 3 * CONV_K, D), F32),
                          gw["meta_tokens"]], axis=0)
    every = allgather_all("small_gather", sm)
    sm_sum = sum_slots("small_sum", [(every, s) for s in range(8)])
    d_s, m_s, v_s = adamw("adamw_small", pack_small(small), sm_sum[:8], pack_small({n: mom_m[n].reshape(1, -1) for n in SMALL}),
                          pack_small({n: mom_v[n].reshape(1, -1) for n in SMALL}))
    grads.update(unpack_small(sm_sum, shapes))
    delta.update(unpack_small(d_s, shapes))
    new_m.update(unpack_small(m_s, shapes))
    new_v.update(unpack_small(v_s, shapes))
    g_conv = lax.dynamic_slice_in_dim(sm_sum[CONV_ROW0:CONV_ROW0 + 3 * CONV_K].reshape(CONV_K, 3 * D), chip * (3 * D // 4), 3 * D // 4, 1)
    g_meta = lax.dynamic_slice_in_dim(sm_sum[META_ROW0:META_ROW0 + N_META], chip * (D // 4), D // 4, 1)
    for n, g in (("gdn_conv_w", g_conv), ("meta_tokens", g_meta)):
        d_, m_, v_ = adamw(f"adamw_{n}", flat(wts[n]), g, flat(mom_m[n]), flat(mom_v[n]))
        grads[n], delta[n], new_m[n], new_v[n] = (t.reshape(shapes[n]) for t in (g, d_, m_, v_))
    loss = sm_sum[LOSS_ROW, 0]

    return (loss, gx[None], *[grads[n] for n in WEIGHTS], *[delta[n] for n in WEIGHTS], *[new_m[n] for n in WEIGHTS],
            *[new_v[n] for n in WEIGHTS])
```

```python
import collections
import functools

import jax
import jax.numpy as jnp
from jax import lax
from jax.experimental import pallas as pl
from jax.experimental.pallas import tpu as pltpu

F32 = jnp.float32
BF16 = jnp.bfloat16
HI = lax.Precision.HIGHEST
MESH = pl.DeviceIdType.MESH

D = 1024
N_META = 16
PAD = 48
HEAD_ROWS = PAD + N_META
CH = 64
H = 8
DH = 128
DFF = 2816
CONV_K = 4
EPS = 1e-6
ROPE_BASE = 10000.0
LANES = 128
N_CHIPS = 4
VMEM_LIMIT = 56 * 2 ** 20
DOWN_ROWS = 688
INTERLEAVED = (0, 2, 1, 3)

OFF_QKV, OFF_Z, OFF_RQ, OFF_RK, OFF_RV, OFF_RG, OFF_GA, OFF_GB, OFF_BA = 0, 3072, 4096, 5120, 6144, 7168, 8192, 9216, 10240
PW = 10368
D_PROJ = 10256

ADAM_LR, ADAM_B1, ADAM_B2, ADAM_EPS, ADAM_WD, ADAM_STEP = 0.001, 0.9, 0.999, 1e-08, 0.01, 10


def _pick(n, cap, mult):
    best = None
    for t in range(mult, min(n, cap) + 1, mult):
        if n % t == 0:
            best = t
    return best if best is not None else n


def _tile_rows(rows, cols, block_bytes=3 * 2 ** 19):
    return _pick(rows, max(8, block_bytes // (4 * cols)), 8)


def _dot(a, b, dims, prec=None):
    return lax.dot_general(a, b, (dims, ((), ())), precision=prec, preferred_element_type=F32)


def _mmh(a, b):
    return _dot(a, b, ((1,), (0,)), HI)


def _split(a):
    hi = a.astype(BF16)
    return hi, (a - hi.astype(F32)).astype(BF16)


_NN, _NT, _TN = ((2,), (1,)), ((2,), (2,)), ((1,), (1,))


def _bdot(a, b, dims):
    return lax.dot_general(a, b, (dims, ((0,), (0,))), preferred_element_type=F32)


def _bmm(a, b, dims):
    return _bdot(a.astype(BF16), b.astype(BF16), dims)


def _bdot3(a, b, dims):
    ah, al = _split(a)
    bh, bl = _split(b)
    if dims == _NN:
        m = a.shape[1]
        both = _bdot(jnp.concatenate([ah, al], axis=1), bh, dims)
        return both[:, :m] + (both[:, m:] + _bdot(ah, bl, dims))
    return _bdot(ah, bh, dims) + (_bdot(ah, bl, dims) + _bdot(al, bh, dims))


def matmul(name, a, b, mode, *, ti_cap=688, tj_cap=512, tr_cap=1408, b_split=False, o_split=False, pair=False,
           rows=(), pars=(), epilogue=None, outs=(F32,), accs=(), side=None, wide=1, o_interleaved=False):
    if mode == "nn":
        I, R = a.shape
        J = N_CHIPS * b.shape[2] if b_split else b.shape[1]
    elif mode == "nt":
        (I, R), J = a.shape, b.shape[0]
    else:
        (R, I), J = a.shape, b.shape[1]
    ti = _pick(I, ti_cap, 16) if mode != "tn" else _pick(I, ti_cap, LANES)
    tj = _pick(J, tj_cap, LANES)
    tr = _pick(R, tr_cap, LANES) if mode != "tn" else _pick(R, tr_cap, 16)
    half = N_CHIPS // 2
    if mode == "nn":
        a_spec = pl.BlockSpec((ti, tr), lambda i, j, r: (i, r))
        if b_split:
            tj = J // N_CHIPS
            b_spec = pl.BlockSpec((None, tr, tj), lambda i, j, r: (j, r, 0))
            b2_spec = pl.BlockSpec((None, tr, tj), lambda i, j, r: (j + half, r, 0))
        else:
            b_spec = pl.BlockSpec((tr, tj), lambda i, j, r: (r, j))
        dims = ((1,), (0,))
    elif mode == "nt":
        a_spec = pl.BlockSpec((ti, tr), lambda i, j, r: (i, r))
        b_spec = pl.BlockSpec((tj, tr), lambda i, j, r: (j, r))
        dims = ((1,), (1,))
    else:
        if o_split:
            tj = J // N_CHIPS
        a_spec = pl.BlockSpec((tr, ti), lambda i, j, r: (r, i))
        b_spec = pl.BlockSpec((tr, tj), lambda i, j, r: (r, j))
        dims = ((0,), (0,))
    assert not pair or (mode == "nn" and b_split), name
    j_out = J // 2 if pair else J
    nr = R // tr
    assert I % ti == 0 and j_out % tj == 0 and R % tr == 0, (name, I, J, R, ti, tj, tr)
    grid = (I // ti, j_out // tj, nr)
    w_spec = o_spec = pl.BlockSpec((ti, tj), lambda i, j, r: (i, j))
    if o_split:
        assert epilogue is None, name
        blk = (lambda j: (j % 2) * 2 + j // 2) if o_interleaved else (lambda j: j)
        w_spec = pl.BlockSpec((None, ti, tj), lambda i, j, r: (blk(j), i, 0))
        out_shapes = [jax.ShapeDtypeStruct((N_CHIPS, I, tj), outs[0])]
    else:
        w_spec = pl.BlockSpec((ti, wide * tj), lambda i, j, r: (i, j))
        out_shapes = [jax.ShapeDtypeStruct((I, wide * j_out), dt) for dt in outs]
    n_prod = 2 if pair else 1
    n_out, n_acc, n_rows, n_pars = len(out_shapes), len(accs), len(rows), len(pars)
    n_si = len(side.ins) if side else 0
    n_so = len(side.out_shapes) if side else 0
    n_in = 1 + n_prod + n_rows + n_pars + n_si
    n_steps = grid[0] * grid[1] * grid[2]

    def body(*refs):
        a_ref, b_refs = refs[0], refs[1:1 + n_prod]
        row_refs = refs[1 + n_prod:1 + n_prod + n_rows]
        par_refs = refs[1 + n_prod + n_rows:1 + n_prod + n_rows + n_pars]
        out_refs = refs[n_in:n_in + n_out]
        acc_refs = refs[n_in + n_out:n_in + n_out + n_acc]
        scr = refs[n_in + n_out + n_acc + n_so:]
        partial = scr[:n_prod] if nr > 1 else ()
        step = (pl.program_id(0) * grid[1] + pl.program_id(1)) * grid[2] + pl.program_id(2)
        if side:
            s_refs = (refs[n_in - n_si:n_in], refs[n_in + n_out + n_acc:n_in + n_out + n_acc + n_so], scr[len(partial):])

        if side or n_acc:
            @pl.when(step == 0)
            def _():
                if side:
                    side.start(*s_refs)
                for r in acc_refs:
                    r[...] = jnp.zeros_like(r)

        lhs = a_ref[...].astype(BF16)
        prods = [_dot(lhs, r[...].astype(BF16), dims) for r in b_refs]

        def finish(prods):
            res = epilogue(prods, *[r[...] for r in row_refs], *[r[...] for r in par_refs]) if epilogue else prods
            for r, v in zip(out_refs, res[:n_out]):
                r[...] = v.astype(r.dtype)
            for r, v in zip(acc_refs, res[n_out:]):
                r[...] += v

        if nr == 1:
            finish(prods)
        else:
            k = pl.program_id(2)

            @pl.when(k == 0)
            def _():
                for r, v in zip(partial, prods):
                    r[...] = v

            @pl.when(k > 0)
            def _():
                for r, v in zip(partial, prods):
                    r[...] += v

            @pl.when(k == nr - 1)
            def _():
                finish([r[...] for r in partial])

        if side:
            @pl.when(step == n_steps - 1)
            def _():
                side.finish(*s_refs)

    ins = [a, b] + ([b] if pair else []) + list(rows) + list(pars)
    in_specs = [a_spec, b_spec] + ([b2_spec] if pair else []) + [o_spec] * n_rows
    in_specs += [pl.BlockSpec(p.shape, lambda i, j, r, nd=p.ndim: (0,) * nd) for p in pars]
    out_specs = [w_spec] * n_out + [pl.BlockSpec(s, lambda i, j, r: (0, 0)) for s in accs]
    out_shapes += [jax.ShapeDtypeStruct(s, F32) for s in accs]
    scratch = [pltpu.VMEM((ti, tj), F32)] * n_prod if nr > 1 else []
    ordered = bool(side) or n_acc > 0
    res = pl.pallas_call(
        body, name=name, grid=grid, in_specs=in_specs + _any_specs(n_si), out_specs=out_specs + _any_specs(n_so),
        out_shape=out_shapes + (list(side.out_shapes) if side else []), scratch_shapes=scratch + (list(side.scratch) if side else []),
        compiler_params=pltpu.CompilerParams(
            dimension_semantics=("arbitrary",) * 3 if ordered else ("parallel", "parallel", "arbitrary"), vmem_limit_bytes=VMEM_LIMIT),
    )(*ins, *(side.ins if side else []))
    own = res[:n_out + n_acc]
    own = own[0] if len(own) == 1 else tuple(own)
    return (own, list(res[n_out + n_acc:])) if side else own


def rowwise(name, fn, rows, pars, outs, accs=(), *, n_rows, tm):
    nt = n_rows // tm
    assert nt * tm == n_rows
    in_specs, ins = [], []
    for arr, w, cb in rows:
        in_specs.append(pl.BlockSpec((tm, w), lambda i, cb=cb: (i, cb)))
        ins.append(arr)
    for p in pars:
        in_specs.append(pl.BlockSpec(p.shape, lambda i, nd=p.ndim: (0,) * nd))
        ins.append(p)
    n_in = len(ins)
    out_specs, out_shapes, aliases = [], [], {}
    for k, o in enumerate(outs):
        if o[0] == "new":
            _, w, dt = o
            out_specs.append(pl.BlockSpec((tm, w), lambda i: (i, 0)))
            out_shapes.append(jax.ShapeDtypeStruct((n_rows, w), dt))
        else:
            _, arr, w, cb = o
            in_specs.append(pl.BlockSpec(memory_space=pl.ANY))
            aliases[len(ins)] = k
            ins.append(arr)
            out_specs.append(pl.BlockSpec((tm, w), lambda i, cb=cb: (i, cb)))
            out_shapes.append(jax.ShapeDtypeStruct(arr.shape, arr.dtype))
    for r, w in accs:
        out_specs.append(pl.BlockSpec((r, w), lambda i: (0, 0)))
        out_shapes.append(jax.ShapeDtypeStruct((r, w), F32))
    n_all_in, n_out, n_acc = len(ins), len(outs), len(accs)

    def body(*refs):
        i = pl.program_id(0)
        vals = [r[...] for r in refs[:n_in]]
        res = fn(i, *vals)
        if not isinstance(res, (tuple, list)):
            res = (res,)
        o_refs = refs[n_all_in:n_all_in + n_out]
        a_refs = refs[n_all_in + n_out:]
        for r, v in zip(o_refs, res[:n_out]):
            r[...] = v.astype(r.dtype)
        if n_acc:
            @pl.when(i == 0)
            def _():
                for r in a_refs:
                    r[...] = jnp.zeros_like(r)
            for r, v in zip(a_refs, res[n_out:]):
                r[...] += v

    return pl.pallas_call(
        body, name=name, grid=(nt,), in_specs=in_specs, out_specs=out_specs, out_shape=out_shapes,
        input_output_aliases=aliases,
        compiler_params=pltpu.CompilerParams(dimension_semantics=("arbitrary",), vmem_limit_bytes=VMEM_LIMIT),
    )(*ins)


def _row_ids(i, tm):
    return i * tm + lax.broadcasted_iota(jnp.int32, (tm, 1), 0)


def _sigmoid(x):
    return 1.0 / (1.0 + jnp.exp(-x))


def _silu(x):
    return x * _sigmoid(x)


def _softplus(x):
    return jnp.maximum(x, 0.0) + jnp.log(1.0 + jnp.exp(-jnp.abs(x)))


def _rms(x, w):
    return x * lax.rsqrt(jnp.mean(x * x, axis=-1, keepdims=True) + EPS) * w


def _heads(fn, *xs):
    return jnp.concatenate([fn(h, *[x[:, h * DH:(h + 1) * DH] for x in xs]) for h in range(H)], axis=1)


def _merge(a, b, ga, gb):
    return _sigmoid(ga) * a + _sigmoid(gb) * b


def _select_matrix(first_lane):
    r = lax.broadcasted_iota(jnp.int32, (LANES, H * DH), 0)
    c = lax.broadcasted_iota(jnp.int32, (LANES, H * DH), 1)
    return (r == first_lane + (c >> 7)).astype(F32)


def _chunk_tri(tm):
    r = lax.broadcasted_iota(jnp.int32, (tm, tm), 0)
    c = lax.broadcasted_iota(jnp.int32, (tm, tm), 1)
    return jnp.logical_and((r >> 6) == (c >> 6), r >= c).astype(F32)


def _gdn_gates(ba, alog_row, dtb_row, mask):
    g = -jnp.exp(alog_row) * _softplus(ba + dtb_row) * mask
    gc = _mmh(_chunk_tri(ba.shape[0]), g)
    beta = _sigmoid(ba) * mask
    return _mmh(gc, _select_matrix(H)), _mmh(beta, _select_matrix(0)), gc


def _gdn_qkv(c):
    a = _silu(c)

    def l2(scale):
        return lambda h, t: t * lax.rsqrt(jnp.sum(t * t, axis=-1, keepdims=True) + EPS) * scale
    q = _heads(l2(DH ** -0.5), a[:, :D])
    k = _heads(l2(1.0), a[:, D:2 * D])
    return q, k, a[:, 2 * D:]


def _rows_from(x, start, n):
    base = start - start % 8
    if base == start:
        return x[start:start + n]
    return pltpu.roll(x, x.shape[0] - start % 8, 0)[base:base + n]


def _conv_shifted(xw, tm):
    return [_rows_from(xw, 5 + j, tm) for j in range(CONV_K)]


def _conv_taps(xw, cws, tm):
    return sum(w * x for w, x in zip(cws, _conv_shifted(xw, tm)))


def _swap_pairs(t):
    n = t.shape[1]
    lane = lax.broadcasted_iota(jnp.int32, t.shape, 1)
    return jnp.where((lane & 1) == 0, pltpu.roll(t, n - 1, 1), pltpu.roll(t, 1, 1))


@jax.custom_vjp
def _unit_lower_inv(a):
    n = -a
    eye = (lax.broadcasted_iota(jnp.int32, a.shape, 1) == lax.broadcasted_iota(jnp.int32, a.shape, 2)).astype(F32)
    p = eye + n
    for _ in range(5):
        n = _bdot3(n, n, _NN)
        p = p + _bdot3(p, n, _NN)
    return p


def _inv_fwd(a):
    t = _unit_lower_inv(a)
    return t, t


def _inv_bwd(t, dt):
    x = _bdot3(t, dt, _TN)
    return (-_bdot3(x, t, _NT),)


_unit_lower_inv.defvjp(_inv_fwd, _inv_bwd)


@jax.custom_vjp
def _unit_lower_inv_known(a, t):
    return t


_unit_lower_inv_known.defvjp(lambda a, t: (t, t), lambda t, dt: (_inv_bwd(t, dt)[0], jnp.zeros_like(t)))


def _gdn_chunk(q, k, v, gc, bb, z, gr, s, norm_w, t_known=None):
    ri = lax.broadcasted_iota(jnp.int32, (H, CH, CH), 1)
    ci = lax.broadcasted_iota(jnp.int32, (H, CH, CH), 2)
    causal = ri >= ci
    gc1 = jnp.sum(gc, axis=2, keepdims=True) * (1.0 / LANES)
    diff = jnp.broadcast_to(gc1, (H, CH, CH)) - jnp.broadcast_to(gr, (H, CH, CH))
    decay = jnp.where(causal, jnp.exp(jnp.where(causal, diff, 0.0)), 0.0)
    kb = k * bb
    sc = _bmm(jnp.concatenate([q, kb], axis=1), k, _NT)
    qk = sc[:, :CH] * decay
    a = jnp.where(ri > ci, sc[:, CH:] * decay, 0.0)
    t = _unit_lower_inv(a) if t_known is None else _unit_lower_inv_known(a, t_known)
    eg = jnp.exp(gc)
    uw = _bmm(t, jnp.concatenate([v * bb, kb * eg], axis=2), _NN)
    g_last = gc[:, CH - 1:CH, :]
    ws = _bmm(jnp.concatenate([uw[:, :, DH:], q * eg], axis=1), s, _NN)
    v_new = uw[:, :, :DH] - ws[:, :CH]
    o = ws[:, CH:] + _bmm(qk, v_new, _NN)
    s_new = s * jnp.exp(g_last) + _bmm(k * jnp.exp(g_last - gc), v_new, _TN)
    y = o * lax.rsqrt(jnp.mean(o * o, axis=-1, keepdims=True) + EPS) * norm_w * _silu(z)
    return y, s_new, t


def _swap_pairs_heads(t):
    return _swap_pairs(t.reshape(t.shape[0] * CH, DH)).reshape(t.shape)


@jax.custom_vjp
def _rope(t, ct, st):
    return t * ct + _swap_pairs_heads(t) * st


def _rope_fwd(t, ct, st):
    return _rope(t, ct, st), (ct, st)


def _rope_bwd(res, d):
    ct, st = res
    return d * ct + _swap_pairs_heads(d * st), jnp.zeros_like(ct), jnp.zeros_like(st)


_rope.defvjp(_rope_fwd, _rope_bwd)


def _ret_chunk(q, k, v, rg, s, decay, xi, zeta, cd, ct, st, norm_w):
    q = _rope(q, ct, st)
    k = _rope(k, ct, st) * (DH ** -0.5)
    scores = _bmm(q, k, _NT) * decay
    o = _bmm(scores, v, _NN) + _bmm(q * xi, s, _NN)
    s_new = s * cd + _bmm(k * zeta, v, _TN)
    xc = o - jnp.mean(o, axis=-1, keepdims=True)
    var = jnp.mean(xc * xc, axis=-1, keepdims=True)
    y = _silu(rg) * (xc * lax.rsqrt(var + EPS) * norm_w)
    return y, s_new


def _head_blocks(ref):
    return jnp.stack([ref[:, h * DH:(h + 1) * DH] for h in range(H)])


def _head_rows(ref):
    return jnp.stack([ref[0, h:h + 1, :] for h in range(H)])


def _scan_fwd(name, chunk_fn, blocks, rowvecs, consts, nc, shared=(), params=(), out_dtype=F32, keep=None):
    n_blk, n_rv, n_c = len(blocks), len(rowvecs), len(consts) + len(shared) + len(params)

    def body(*refs):
        blk = refs[:n_blk]
        rvs = refs[n_blk:n_blk + n_rv]
        cst = refs[n_blk + n_rv:n_blk + n_rv + n_c]
        o_ref, st_ref = refs[n_blk + n_rv + n_c:n_blk + n_rv + n_c + 2]
        s_scr = refs[-1]

        @pl.when(pl.program_id(0) == 0)
        def _():
            s_scr[...] = jnp.zeros_like(s_scr)

        s = s_scr[...]
        st_ref[0] = s
        res = chunk_fn(*[_head_blocks(r) for r in blk], *[_head_rows(r) for r in rvs], s, *[r[...] for r in cst])
        for h in range(H):
            o_ref[:, h * DH:(h + 1) * DH] = res[0][h].astype(o_ref.dtype)
        s_scr[...] = res[1]
        if keep is not None:
            refs[-2][0] = res[2]

    in_specs = [pl.BlockSpec((CH, H * DH), lambda c, f=f: (c, f)) for _, f in blocks]
    in_specs += [pl.BlockSpec((1, H, CH), lambda c: (c, 0, 0)) for _ in rowvecs]
    in_specs += [pl.BlockSpec(a.shape, lambda c: (0, 0, 0)) for a in consts]
    in_specs += [pl.BlockSpec((CH, a.shape[1]), lambda c: (c, 0)) for a in shared]
    in_specs += [pl.BlockSpec(a.shape, lambda c, nd=a.ndim: (0,) * nd) for a in params]
    out_specs = [pl.BlockSpec((CH, H * DH), lambda c: (c, 0)), pl.BlockSpec((1, H, DH, DH), lambda c: (c, 0, 0, 0))]
    out_shape = [jax.ShapeDtypeStruct((nc * CH, H * DH), out_dtype), jax.ShapeDtypeStruct((nc, H, DH, DH), F32)]
    if keep is not None:
        out_specs.append(pl.BlockSpec((1, H) + keep, lambda c: (c, 0, 0, 0)))
        out_shape.append(jax.ShapeDtypeStruct((nc, H) + keep, F32))
    return pl.pallas_call(
        body, name=name, grid=(nc,), in_specs=in_specs, out_specs=out_specs, out_shape=out_shape,
        scratch_shapes=[pltpu.VMEM((H, DH, DH), F32)],
        compiler_params=pltpu.CompilerParams(dimension_semantics=("arbitrary",), vmem_limit_bytes=VMEM_LIMIT),
    )(*[a for a, _ in blocks], *rowvecs, *consts, *shared, *params)


def _scan_bwd(name, chunk_fn, blocks, rowvecs, consts, states, do, into, nc, shared=(), params=(), kept=None):
    n_blk, n_rv, n_c, n_p = len(blocks), len(rowvecs), len(consts) + len(shared), len(params)
    n_kept = 0 if kept is None else 1
    packed = list(into[2]) if into else []
    fresh = [k for k in range(n_blk) if k not in packed]
    n_bo = len(fresh) + (1 if into else 0)

    def body(*refs):
        blk = refs[:n_blk]
        rvs = refs[n_blk:n_blk + n_rv]
        cst = refs[n_blk + n_rv:n_blk + n_rv + n_c]
        par = refs[n_blk + n_rv + n_c:n_blk + n_rv + n_c + n_p]
        st_ref, do_ref = refs[n_blk + n_rv + n_c + n_p:n_blk + n_rv + n_c + n_p + 2]
        n_in = n_blk + n_rv + n_c + n_p + 2 + n_kept + (1 if into else 0)
        known = [refs[n_blk + n_rv + n_c + n_p + 2][0]] if n_kept else []
        o_refs = refs[n_in:n_in + n_bo]
        rv_refs = refs[n_in + n_bo:n_in + n_bo + n_rv]
        p_refs = refs[n_in + n_bo + n_rv:n_in + n_bo + n_rv + n_p]
        ds_scr = refs[n_in + n_bo + n_rv + n_p]

        @pl.when(pl.program_id(0) == 0)
        def _():
            ds_scr[...] = jnp.zeros_like(ds_scr)
            for r in p_refs:
                r[...] = jnp.zeros_like(r)

        cv = [r[...] for r in cst]
        n_d = n_blk + n_rv + 1
        _, vjp = jax.vjp(lambda *a: chunk_fn(*a[:n_d], *cv, *a[n_d:], *known)[:2], *[_head_blocks(r) for r in blk],
                         *[_head_rows(r) for r in rvs], st_ref[0], *[r[...] for r in par])
        grads = vjp((_head_blocks(do_ref), ds_scr[...]))
        for h in range(H):
            for pos, k in enumerate(fresh):
                o_refs[pos][:, h * DH:(h + 1) * DH] = grads[k][h]
            for pos, k in enumerate(packed):
                col = pos * H * DH + h * DH
                o_refs[-1][:, col:col + DH] = grads[k][h].astype(o_refs[-1].dtype)
            for r, g in zip(rv_refs, grads[n_blk:n_blk + n_rv]):
                r[0, h:h + 1, :] = g[h]
        ds_scr[...] = grads[n_blk + n_rv]
        for r, g in zip(p_refs, grads[n_d:]):
            r[...] += g

    rc = lambda c: nc - 1 - c
    in_specs = [pl.BlockSpec((CH, H * DH), lambda c, f=f: (rc(c), f)) for _, f in blocks]
    in_specs += [pl.BlockSpec((1, H, CH), lambda c: (rc(c), 0, 0)) for _ in rowvecs]
    in_specs += [pl.BlockSpec(a.shape, lambda c: (0, 0, 0)) for a in consts]
    in_specs += [pl.BlockSpec((CH, a.shape[1]), lambda c: (rc(c), 0)) for a in shared]
    in_specs += [pl.BlockSpec(a.shape, lambda c, nd=a.ndim: (0,) * nd) for a in params]
    in_specs += [pl.BlockSpec((1, H, DH, DH), lambda c: (rc(c), 0, 0, 0)), pl.BlockSpec((CH, H * DH), lambda c: (rc(c), 0))]
    ins = [a for a, _ in blocks] + list(rowvecs) + list(consts) + list(shared) + list(params) + [states, do]
    if n_kept:
        in_specs.append(pl.BlockSpec((1,) + kept.shape[1:], lambda c: (rc(c), 0, 0, 0)))
        ins.append(kept)
    out_specs, out_shapes, aliases = [], [], {}
    for _ in fresh:
        out_specs.append(pl.BlockSpec((CH, H * DH), lambda c: (rc(c), 0)))
        out_shapes.append(jax.ShapeDtypeStruct((nc * CH, H * DH), F32))
    if into:
        arr, f, _ = into
        in_specs.append(pl.BlockSpec(memory_space=pl.ANY))
        aliases[len(ins)] = len(fresh)
        ins.append(arr)
        out_specs.append(pl.BlockSpec((CH, len(packed) * H * DH), lambda c: (rc(c), f)))
        out_shapes.append(jax.ShapeDtypeStruct(arr.shape, arr.dtype))
    for _ in rowvecs:
        out_specs.append(pl.BlockSpec((1, H, CH), lambda c: (rc(c), 0, 0)))
        out_shapes.append(jax.ShapeDtypeStruct((nc, H, CH), F32))
    for a in params:
        out_specs.append(pl.BlockSpec(a.shape, lambda c, nd=a.ndim: (0,) * nd))
        out_shapes.append(jax.ShapeDtypeStruct(a.shape, F32))
    return pl.pallas_call(
        body, name=name, grid=(nc,), in_specs=in_specs, out_specs=out_specs, out_shape=out_shapes,
        input_output_aliases=aliases, scratch_shapes=[pltpu.VMEM((H, DH, DH), F32)],
        compiler_params=pltpu.CompilerParams(dimension_semantics=("arbitrary",), vmem_limit_bytes=VMEM_LIMIT),
    )(*ins)


def _gdn_pre_specs(p, cws, alog_row, dtb_row, tm, pos):
    sub = tm // 8
    rows = [pl.BlockSpec((tm, 3 * D), lambda i: (pos(i), 0)),
            pl.BlockSpec((8, 3 * D), lambda i: (jnp.maximum(pos(i) * sub - 1, 0), 0)),
            pl.BlockSpec((tm, LANES), lambda i: (pos(i), OFF_BA // LANES))]
    pars = [pl.BlockSpec(a.shape, lambda i: (0, 0)) for a in (*cws, alog_row, dtb_row)]
    return rows + pars, [p, p, p, *cws, alog_row, dtb_row]


def gdn_pre_fwd(p, cws, alog_row, dtb_row, lp, tm):
    def body(x_ref, prev_ref, ba_ref, c0, c1, c2, c3, al_ref, dt_ref, q_ref, k_ref, v_ref, g_ref, b_ref, gc_ref):
        i = pl.program_id(0)
        prev = jnp.where(i > 0, prev_ref[...], 0.0)
        xw = jnp.concatenate([prev, x_ref[...]], axis=0)
        c = _conv_taps(xw, [c0[...], c1[...], c2[...], c3[...]], tm)
        q, k, v = _gdn_qkv(c)
        mask = (_row_ids(i, tm) >= PAD).astype(F32)
        g, b, gc = _gdn_gates(ba_ref[...], al_ref[...], dt_ref[...], mask)
        q_ref[...] = q
        k_ref[...] = k
        v_ref[...] = v
        g_ref[...] = g
        b_ref[...] = b
        gc_ref[...] = gc

    in_specs, ins = _gdn_pre_specs(p, cws, alog_row, dtb_row, tm, lambda i: i)
    o_spec = pl.BlockSpec((tm, D), lambda i: (i, 0))
    return pl.pallas_call(
        body, name="gdn_pre_fwd", grid=(lp // tm,), in_specs=in_specs,
        out_specs=[o_spec] * 5 + [pl.BlockSpec((tm, LANES), lambda i: (i, 0))],
        out_shape=[jax.ShapeDtypeStruct((lp, D), F32)] * 5 + [jax.ShapeDtypeStruct((lp, LANES), F32)],
        compiler_params=pltpu.CompilerParams(dimension_semantics=("parallel",), vmem_limit_bytes=VMEM_LIMIT),
    )(*ins)


def gdn_pre_bwd(p, cws, alog_row, dtb_row, dq, dk, dv, dg, db, dgc, dp, lp, tm):
    nt = lp // tm
    pos = lambda i: nt - 1 - i

    def body(x_ref, prev_ref, ba_ref, c0, c1, c2, c3, al_ref, dt_ref, dq_ref, dk_ref, dv_ref, dg_ref, db_ref, dgc_ref,
             dp_any, dx_ref, dba_ref, dcw_ref, dpar_ref, carry):
        i = pl.program_id(0)
        t = pos(i)

        @pl.when(i == 0)
        def _():
            carry[...] = jnp.zeros_like(carry)
            dcw_ref[...] = jnp.zeros_like(dcw_ref)
            dpar_ref[...] = jnp.zeros_like(dpar_ref)

        cws_v = [c0[...], c1[...], c2[...], c3[...]]
        prev = jnp.where(t > 0, prev_ref[...], 0.0)
        xw = jnp.concatenate([prev, x_ref[...]], axis=0)
        xs = _conv_shifted(xw, tm)
        c = sum(cws_v[j] * xs[j] for j in range(CONV_K))
        _, vjp_qkv = jax.vjp(_gdn_qkv, c)
        (dc,) = vjp_qkv((dq_ref[...], dk_ref[...], dv_ref[...]))
        zeros8 = jnp.zeros((8, 3 * D), F32)
        dcp = jnp.concatenate([zeros8, dc, zeros8], axis=0)
        dxw = sum(cws_v[j] * _rows_from(dcp, 3 - j, tm + 8) for j in range(CONV_K))
        dx_ref[...] = jnp.concatenate([dxw[8:tm], dxw[tm:] + carry[...]], axis=0).astype(dx_ref.dtype)
        carry[...] = dxw[:8]
        for j in range(CONV_K):
            dcw_ref[j:j + 1, :] += jnp.sum(dc * xs[j], axis=0, keepdims=True)
        mask = (_row_ids(t, tm) >= PAD).astype(F32)
        _, vjp_g = jax.vjp(lambda ba, al, dt: _gdn_gates(ba, al, dt, mask), ba_ref[...], al_ref[...], dt_ref[...])
        dba, dal, ddt = vjp_g((dg_ref[...], db_ref[...], dgc_ref[...]))
        dba_ref[...] = dba
        dpar_ref[0:1, :] += dal
        dpar_ref[1:2, :] += ddt

    in_specs, ins = _gdn_pre_specs(p, cws, alog_row, dtb_row, tm, pos)
    g_spec = pl.BlockSpec((tm, D), lambda i: (pos(i), 0))
    s_spec = pl.BlockSpec((tm, LANES), lambda i: (pos(i), 0))
    in_specs += [g_spec] * 5 + [s_spec, pl.BlockSpec(memory_space=pl.ANY)]
    ins += [dq, dk, dv, dg, db, dgc, dp]
    return pl.pallas_call(
        body, name="gdn_pre_bwd", grid=(nt,), in_specs=in_specs,
        out_specs=[pl.BlockSpec((tm, 3 * D), lambda i: (pos(i), 0)), s_spec,
                   pl.BlockSpec((8, 3 * D), lambda i: (0, 0)), pl.BlockSpec((8, LANES), lambda i: (0, 0))],
        out_shape=[jax.ShapeDtypeStruct(dp.shape, dp.dtype), jax.ShapeDtypeStruct((lp, LANES), F32),
                   jax.ShapeDtypeStruct((8, 3 * D), F32), jax.ShapeDtypeStruct((8, LANES), F32)],
        input_output_aliases={len(ins) - 1: 0},
        scratch_shapes=[pltpu.VMEM((8, 3 * D), F32)],
        compiler_params=pltpu.CompilerParams(dimension_semantics=("arbitrary",), vmem_limit_bytes=VMEM_LIMIT),
    )(*ins)


def _me():
    return lax.axis_index("x"), lax.axis_index("y"), lax.axis_index("c")


def _any_specs(n):
    return [pl.BlockSpec(memory_space=pl.ANY)] * n


Exchange = collections.namedtuple("Exchange", "ins out_shapes scratch start finish")


def _dma_sems(*shape):
    return pltpu.SemaphoreType.DMA(shape)


def run_exchange(name, ex):
    n_i, n_o = len(ex.ins), len(ex.out_shapes)

    def body(*refs):
        parts = (refs[:n_i], refs[n_i:n_i + n_o], refs[n_i + n_o:])
        ex.start(*parts)
        ex.finish(*parts)

    return pl.pallas_call(body, name=name, out_shape=list(ex.out_shapes), in_specs=_any_specs(n_i), out_specs=_any_specs(n_o),
                          scratch_shapes=list(ex.scratch))(*ex.ins)


def gather_exchange(ws):
    n = len(ws)

    def copies(w_refs, o_refs, sems):
        send_sems, recv_sems, fsend_sems, frecv_sems, osend_sems, orecv_sems = sems
        x, y, c = _me()
        me = 2 * x + y
        chips = [(1 - x, y), (x, 1 - y), (1 - x, 1 - y)]

        def own(a):
            return pltpu.make_async_remote_copy(src_ref=w_refs[a], dst_ref=o_refs[a].at[me], send_sem=osend_sems.at[a],
                                                recv_sem=orecv_sems.at[a], device_id=(x, y, 1 - c), device_id_type=MESH)

        def rows(a, cc):
            rh = ws[a].shape[0] // 2
            return pl.ds(pl.multiple_of(cc * rh, 8), rh)

        def ici(a, j, slot):
            px, py = chips[j]
            return pltpu.make_async_remote_copy(
                src_ref=w_refs[a].at[rows(a, c)], dst_ref=o_refs[a].at[slot, rows(a, c)], send_sem=send_sems.at[a, j],
                recv_sem=recv_sems.at[a, j], device_id=(px, py, c), device_id_type=MESH)

        def d2d(a, j, cc):
            px, py = chips[j]
            blk = o_refs[a].at[2 * px + py, rows(a, cc)]
            return pltpu.make_async_remote_copy(src_ref=blk, dst_ref=blk, send_sem=fsend_sems.at[a, j],
                                                recv_sem=frecv_sems.at[a, j], device_id=(x, y, 1 - c), device_id_type=MESH)
        mine = [own(a) for a in range(n)]
        sends = [ici(a, j, me) for a in range(n) for j in range(3)]
        arrivals = [ici(a, j, 2 * px + py) for a in range(n) for j, (px, py) in enumerate(chips)]
        passes = [d2d(a, j, c) for a in range(n) for j in range(3)]
        passed_to_me = [d2d(a, j, 1 - c) for a in range(n) for j in range(3)]
        return mine, sends, arrivals, passes, passed_to_me

    def start(w_refs, o_refs, sems):
        mine, sends, _, _, _ = copies(w_refs, o_refs, sems)
        for cp in mine + sends:
            cp.start()

    def finish(w_refs, o_refs, sems):
        mine, sends, arrivals, passes, passed_to_me = copies(w_refs, o_refs, sems)
        for arrival, onward in zip(arrivals, passes):
            arrival.wait_recv()
            onward.start()
        for cp in passed_to_me + mine:
            cp.wait_recv()
        for cp in sends + passes + mine:
            cp.wait_send()

    return Exchange(list(ws), [jax.ShapeDtypeStruct((N_CHIPS,) + w.shape, w.dtype) for w in ws],
                    [_dma_sems(n, 3), _dma_sems(n, 3), _dma_sems(n, 3), _dma_sems(n, 3), _dma_sems(n), _dma_sems(n)], start, finish)


def _simple_exchange(ins, out_shapes, sem_shape, copies):
    def start(i_refs, o_refs, sems):
        for cp in copies(i_refs, o_refs, *sems):
            cp.start()

    def finish(i_refs, o_refs, sems):
        cps = copies(i_refs, o_refs, *sems)
        for cp in cps:
            cp.wait_recv()
        for cp in cps:
            cp.wait_send()

    return Exchange(list(ins), out_shapes, [_dma_sems(*sem_shape), _dma_sems(*sem_shape)], start, finish)


def sibling_halves_exchange(gs):
    def copies(g_refs, o_refs, send_sems, recv_sems):
        x, y, c = _me()
        cps = []
        for a in range(len(gs)):
            rh = gs[a].shape[1] // 2
            for q in range(N_CHIPS):
                cps.append(pltpu.make_async_remote_copy(
                    src_ref=g_refs[a].at[q, pl.ds(pl.multiple_of((1 - c) * rh, 8), rh)], dst_ref=o_refs[a].at[q],
                    send_sem=send_sems.at[a, q], recv_sem=recv_sems.at[a, q], device_id=(x, y, 1 - c), device_id_type=MESH))
        return cps

    return _simple_exchange(gs, [jax.ShapeDtypeStruct((N_CHIPS, g.shape[1] // 2, g.shape[2]), g.dtype) for g in gs],
                            (len(gs), N_CHIPS), copies)


def scatter_chips_exchange(css):
    def copies(c_refs, o_refs, send_sems, recv_sems):
        x, y, c = _me()
        chips = [(1 - x, y), (x, 1 - y), (1 - x, 1 - y)]
        return [pltpu.make_async_remote_copy(
            src_ref=c_refs[a].at[2 * px + py], dst_ref=o_refs[a].at[j], send_sem=send_sems.at[a, j],
            recv_sem=recv_sems.at[a, j], device_id=(px, py, c), device_id_type=MESH)
            for a in range(len(css)) for j, (px, py) in enumerate(chips)]

    return _simple_exchange(css, [jax.ShapeDtypeStruct((3,) + cs.shape[1:], cs.dtype) for cs in css], (len(css), 3), copies)


def sibling_swap(name, halves):
    n = len(halves)

    def body(*refs):
        h_refs, o_refs = refs[:n], refs[n:2 * n]
        send_sems, recv_sems = refs[2 * n:]
        x, y, c = _me()
        cps = [pltpu.make_async_remote_copy(src_ref=h_refs[a], dst_ref=o_refs[a], send_sem=send_sems.at[a],
                                            recv_sem=recv_sems.at[a], device_id=(x, y, 1 - c), device_id_type=MESH)
               for a in range(n)]
        for cp in cps:
            cp.start()
        for cp in cps:
            cp.wait_recv()
        for cp in cps:
            cp.wait_send()

    return pl.pallas_call(
        body, name=name, out_shape=[jax.ShapeDtypeStruct(h.shape, h.dtype) for h in halves],
        in_specs=_any_specs(n), out_specs=_any_specs(n),
        scratch_shapes=[pltpu.SemaphoreType.DMA((n,)), pltpu.SemaphoreType.DMA((n,))],
    )(*halves)


def allgather_all(name, s):
    def body(s_ref, out_ref, send_sems, recv_sems, local_sem):
        x, y, c = _me()
        peers = [(x ^ ((m >> 2) & 1), y ^ ((m >> 1) & 1), c ^ (m & 1)) for m in range(1, 8)]
        mine = pltpu.make_async_copy(s_ref, out_ref.at[4 * x + 2 * y + c], local_sem)
        mine.start()

        def copy(j, slot):
            return pltpu.make_async_remote_copy(src_ref=s_ref, dst_ref=out_ref.at[slot], send_sem=send_sems.at[j],
                                                recv_sem=recv_sems.at[j], device_id=peers[j], device_id_type=MESH)
        sends = [copy(j, 4 * x + 2 * y + c) for j in range(7)]
        for cp in sends:
            cp.start()
        for j, (px, py, pc) in enumerate(peers):
            copy(j, 4 * px + 2 * py + pc).wait_recv()
        for cp in sends:
            cp.wait_send()
        mine.wait()

    return pl.pallas_call(
        body, name=name, out_shape=jax.ShapeDtypeStruct((8,) + s.shape, s.dtype),
        in_specs=_any_specs(1), out_specs=pl.BlockSpec(memory_space=pl.ANY),
        scratch_shapes=[pltpu.SemaphoreType.DMA((7,)), pltpu.SemaphoreType.DMA((7,)), pltpu.SemaphoreType.DMA(())],
    )(s)


def sum_slots(name, parts):
    n = len(parts)
    R, W = parts[0][0].shape[1:]
    tm = _tile_rows(R, W)
    idx = jnp.stack([jnp.asarray(s, jnp.int32) for _, s in parts])

    def body(idx_ref, *refs):
        acc = refs[0][...].astype(F32)
        for r in refs[1:n]:
            acc = acc + r[...].astype(F32)
        refs[n][...] = acc

    grid_spec = pltpu.PrefetchScalarGridSpec(
        num_scalar_prefetch=1, grid=(R // tm,),
        in_specs=[pl.BlockSpec((None, tm, W), lambda i, idx, k=k: (idx[k], i, 0)) for k in range(n)],
        out_specs=pl.BlockSpec((tm, W), lambda i, idx: (i, 0)))
    return pl.pallas_call(body, name=name, grid_spec=grid_spec, out_shape=jax.ShapeDtypeStruct((R, W), F32),
                          compiler_params=pltpu.CompilerParams(dimension_semantics=("parallel",)))(idx, *[a for a, _ in parts])


def chip_sums(name, g, recv, c):
    _, R, W = g.shape
    rh = R // 2
    tm = _tile_rows(rh, W)
    g8 = g.reshape(2 * N_CHIPS, rh, W)
    idx = jnp.asarray(c, jnp.int32).reshape(1)

    def body(idx_ref, g_ref, r_ref, o_ref):
        o_ref[...] = (g_ref[...] + r_ref[...]).astype(o_ref.dtype)

    grid_spec = pltpu.PrefetchScalarGridSpec(
        num_scalar_prefetch=1, grid=(N_CHIPS, rh // tm),
        in_specs=[pl.BlockSpec((None, tm, W), lambda q, i, idx: (2 * q + idx[0], i, 0)),
                  pl.BlockSpec((None, tm, W), lambda q, i, idx: (q, i, 0))],
        out_specs=pl.BlockSpec((None, tm, W), lambda q, i, idx: (q, i, 0)))
    return pl.pallas_call(body, name=name, grid_spec=grid_spec, out_shape=jax.ShapeDtypeStruct((N_CHIPS, rh, W), BF16),
                          compiler_params=pltpu.CompilerParams(dimension_semantics=("parallel", "parallel")))(idx, g8, recv)


def _adamw_update(w, g, m, v):
    m = ADAM_B1 * m + (1.0 - ADAM_B1) * g
    v = ADAM_B2 * v + (1.0 - ADAM_B2) * (g * g)
    m_hat = m / (1.0 - ADAM_B1 ** ADAM_STEP)
    v_hat = v / (1.0 - ADAM_B2 ** ADAM_STEP)
    return -ADAM_LR * (m_hat / (jnp.sqrt(v_hat) + ADAM_EPS) + ADAM_WD * w), m, v


def adamw(name, w, g, m, v):
    R, W = w.shape
    return rowwise(name, lambda i, w, g, m, v: _adamw_update(w, g, m, v), [(a, W, 0) for a in (w, g, m, v)], [],
                   [("new", W, F32)] * 3, n_rows=R, tm=_tile_rows(R, W, 2 ** 20))


def adamw_halves(name, w, g_mine, g_other, m, v, c):
    R, W = w.shape
    rh = R // 2
    tm = _tile_rows(rh, W, 2 ** 20)
    nh = rh // tm
    idx = jnp.asarray(c, jnp.int32).reshape(1)

    def body(idx_ref, w_ref, ga_ref, gb_ref, m_ref, v_ref, g_ref, d_ref, mo_ref, vo_ref):
        mine = (pl.program_id(0) // nh) == idx_ref[0]
        g = jnp.where(mine, ga_ref[...], gb_ref[...])
        d, m, v = _adamw_update(w_ref[...], g, m_ref[...], v_ref[...])
        g_ref[...] = g
        d_ref[...] = d
        mo_ref[...] = m
        vo_ref[...] = v

    full = pl.BlockSpec((tm, W), lambda i, idx: (i, 0))
    half = pl.BlockSpec((tm, W), lambda i, idx: (i % nh, 0))
    grid_spec = pltpu.PrefetchScalarGridSpec(num_scalar_prefetch=1, grid=(R // tm,), in_specs=[full, half, half, full, full],
                                             out_specs=[full] * 4)
    return pl.pallas_call(body, name=name, grid_spec=grid_spec, out_shape=[jax.ShapeDtypeStruct((R, W), F32)] * 4,
                          compiler_params=pltpu.CompilerParams(dimension_semantics=("parallel",)))(idx, w, g_mine, g_other, m, v)


W_IN_BLOCK = D_PROJ // N_CHIPS
BA_REF = OFF_RQ


def _w_in_padded_from_blocks(g):
    lo, hi = BA_REF - W_IN_BLOCK, BA_REF + 16 - W_IN_BLOCK
    return jnp.concatenate([g[0], g[1][:, :lo], g[1][:, hi:], g[2], g[3], g[1][:, lo:hi],
                            jnp.zeros((D, PW - D_PROJ), g.dtype)], axis=1)


def _w_in_grad_blocks(g):
    s = W_IN_BLOCK
    second = jnp.concatenate([g[:, s:BA_REF], g[:, OFF_BA:OFF_BA + 16], g[:, BA_REF:2 * s - 16]], axis=1)
    return jnp.stack([g[:, :s], second, g[:, 2 * s - 16:3 * s - 16], g[:, 3 * s - 16:4 * s - 16]])


def _ret_consts():
    f = F32
    log_gamma = jnp.log1p(-jnp.exp2(-5.0 - jnp.arange(H, dtype=f)))
    pos = jnp.arange(CH, dtype=f)
    causal = jnp.tril(jnp.ones((CH, CH), dtype=bool))
    diff = pos[:, None] - pos[None, :]
    decay = jnp.where(causal, jnp.exp(jnp.where(causal, diff, 0.0) * log_gamma[:, None, None]), 0.0)
    xi = jnp.broadcast_to(jnp.exp((pos + 1.0) * log_gamma[:, None])[:, :, None], (H, CH, DH))
    zeta = jnp.broadcast_to(jnp.exp((CH - 1.0 - pos) * log_gamma[:, None])[:, :, None], (H, CH, DH))
    cd = jnp.broadcast_to(jnp.exp(CH * log_gamma)[:, None, None], (H, 1, DH))
    return decay, xi, zeta, cd


def _rope_tables(lp):
    pos = jnp.arange(lp, dtype=F32) - float(PAD)
    inv = 1.0 / (ROPE_BASE ** jnp.linspace(0.0, 1.0, DH // 2, dtype=F32))
    ang = pos[:, None] * inv[None, :]
    cos, sin = jnp.cos(ang), jnp.sin(ang)
    ct = jnp.repeat(cos, 2, axis=1)
    st = jnp.stack([-sin, sin], axis=-1).reshape(lp, DH)
    return ct, st


def local_step(x, tgt, w, small, hooks=None):
    hooks = hooks or {}
    seq = x.shape[0]
    lp = HEAD_ROWS + seq
    nc = lp // CH
    tm = _pick(lp, 192, CH)
    row = functools.partial(rowwise, n_rows=lp, tm=tm)
    gw = {}

    def mm(name, *args, **kw):
        if name not in hooks:
            return matmul(name, *args, **kw)
        make_exchange, take = hooks[name]
        out, results = matmul(name, *args, side=make_exchange(gw), **kw)
        take(results)
        return out

    h0 = jnp.concatenate([jnp.zeros((PAD, D), F32), w["meta_tokens"], x], axis=0)
    tgt_p = jnp.concatenate([jnp.zeros((HEAD_ROWS, D), F32), tgt], axis=0)

    def ffn_fwd(tag, h, wn, n=None, next_norm=None, loss=None):
        if n is None:
            n = row(f"{tag}_norm", lambda i, h, wn: _rms(h, wn), [(h, D, 0)], [wn], [("new", D, BF16)])[0]
        g, u, mid = mm(f"{tag}_up", n, w[f"{tag}_w_in"], "nn", b_split=True, pair=True, outs=(F32, F32, BF16),
                       epilogue=lambda pr: (pr[0], pr[1], _silu(pr[0]) * pr[1]))
        down = functools.partial(mm, f"{tag}_down", mid, w[f"{tag}_w_out"], "nn", ti_cap=DOWN_ROWS, tj_cap=1024, tr_cap=2816,
                                 rows=[h])
        if loss is not None:
            target, final_w = loss
            ti = _pick(lp, DOWN_ROWS, 16)

            def head(pr, h, t, nw):
                mask = (_row_ids(pl.program_id(0), ti) >= HEAD_ROWS).astype(F32)
                y, vjp = jax.vjp(_rms, h + 0.5 * pr[0], nw)
                err = (y - t) * mask
                dh, dw = vjp(err * (1.0 / D))
                return dh, dw, jnp.sum(err * err, keepdims=True).reshape(1, 1) * (0.5 / D) * jnp.ones((1, LANES), F32)
            return down(rows=[h, target], pars=[final_w], accs=[(1, D), (1, LANES)], epilogue=head), (h, n, g, u, mid), None
        if next_norm is None:
            return down(epilogue=lambda pr, h: (h + 0.5 * pr[0],)), (h, n, g, u, mid), None
        out, n_next = down(pars=[next_norm], outs=(F32, BF16), epilogue=lambda pr, h, nw: _with_norm(h + 0.5 * pr[0], nw))
        return out, (h, n, g, u, mid), n_next

    def _with_norm(out, nw):
        return out, _rms(out, nw)

    def norm_bwd_epilogue(pr, h, dres, wn):
        _, vjp = jax.vjp(_rms, h, wn)
        dh, dw = vjp(pr[0])
        return dres + dh, dw

    def ffn_bwd(tag, dh, saved, wn):
        h, n, g, u, mid = saved
        w_in_il, w_out = w[f"{tag}_w_in_il"], w[f"{tag}_w_out"]
        dw_out = mm(f"{tag}_dwout", mid, dh, "tn", ti_cap=1408, tj_cap=1024, tr_cap=1376, epilogue=lambda pr: (0.5 * pr[0],))

        def act_bwd(pr, g, u):
            _, vjp = jax.vjp(lambda g, u: _silu(g) * u, g, u)
            return (jnp.concatenate(vjp(0.5 * pr[0]), axis=1),)
        dgu = mm(f"{tag}_dmid", dh, w_out, "nt", tj_cap=2 * DFF // N_CHIPS, rows=[g, u], outs=(BF16,), wide=2, epilogue=act_bwd)
        dh_in, dwn = mm(f"{tag}_dn", dgu, w_in_il, "nt", tj_cap=1024, tr_cap=DFF, rows=[h, dh], pars=[wn],
                        accs=[(1, D)], epilogue=norm_bwd_epilogue)
        dw_in = mm(f"{tag}_dwin", n, dgu, "tn", ti_cap=1024, tr_cap=2752, o_split=True, o_interleaved=True)
        return dh_in, dw_in, dw_out, dwn

    h1, ffn1_saved, n2 = ffn_fwd("ffn1", h0, small["ffn1_norm"], next_norm=small["mix_norm"])
    p = mm("mix_proj", n2, w["w_in_p"], "nn", ti_cap=1376, tj_cap=1152)

    cws = [w["gdn_conv_w"][j:j + 1] for j in range(CONV_K)]
    alog_row = jnp.zeros((1, LANES), F32).at[:, H:2 * H].set(small["gdn_a_log"])
    dtb_row = jnp.zeros((1, LANES), F32).at[:, H:2 * H].set(small["gdn_dt_bias"])
    q, k, v, gcb, bb, gc = gdn_pre_fwd(p, cws, alog_row, dtb_row, lp, tm)
    gc_rows = gc[:, H:2 * H].reshape(nc, CH, H).transpose(0, 2, 1)
    gdn_blocks = [(q, 0), (k, 0), (v, 0), (gcb, 0), (bb, 0), (p, OFF_Z // D)]
    y_a, gdn_states, gdn_inv = _scan_fwd("gdn_scan_fwd", _gdn_chunk, gdn_blocks, [gc_rows], [], nc,
                                         params=[small["gdn_out_norm"]], out_dtype=BF16, keep=(CH, CH))

    ct, st = _rope_tables(lp)
    ret_consts = list(_ret_consts())
    ret_blocks = [(p, OFF_RQ // D), (p, OFF_RK // D), (p, OFF_RV // D), (p, OFF_RG // D)]
    ret_w = small["ret_out_norm"].reshape(H, 1, DH)
    y_b, ret_states = _scan_fwd("ret_scan_fwd", _ret_chunk, ret_blocks, [], ret_consts, nc, shared=[ct, st],
                                params=[ret_w], out_dtype=BF16)

    br_a = matmul("branch_gdn", y_a, w["w_branch_gdn"], "nn", ti_cap=1376, tj_cap=1024)
    br_b = matmul("branch_ret", y_b, w["w_branch_ret"], "nn", ti_cap=1376, tj_cap=1024)
    merged = row("merge", lambda i, a, b, ga, gb_: _merge(a, b, ga, gb_),
                 [(br_a, D, 0), (br_b, D, 0), (p, D, OFF_GA // D), (p, D, OFF_GB // D)], [], [("new", D, BF16)])[0]
    h2, n3 = matmul("mix_out", merged, w["w_out"], "nn", ti_cap=1376, tj_cap=1024, rows=[h1], pars=[small["ffn2_norm"]],
                    outs=(F32, BF16), epilogue=lambda pr, h, nw: _with_norm(h + pr[0], nw))
    (dh3, d_final, loss_row), ffn2_saved, _ = ffn_fwd("ffn2", h2, small["ffn2_norm"], n=n3, loss=(tgt_p, small["final_norm"]))

    gs = {"final_norm": d_final}
    dh2, gw["ffn2_w_in"], gw["ffn2_w_out"], gs["ffn2_norm"] = ffn_bwd("ffn2", dh3, ffn2_saved, small["ffn2_norm"])
    dmerged = matmul("mix_out_dx", dh2, w["w_out"], "nt", ti_cap=1376, tj_cap=1024)
    gw["w_out"] = matmul("mix_out_dw", merged, dh2, "tn", ti_cap=1024, tj_cap=1024, tr_cap=2752)
    dp = lax.empty((lp, PW), BF16)

    def merge_bwd(i, dm, a, b, ga, gb_):
        _, vjp = jax.vjp(_merge, a, b, ga, gb_)
        da, db, dga, dgb = vjp(dm)
        return da, db, jnp.concatenate([dga, dgb], axis=1)
    da, db_, dp = row("merge_bwd", merge_bwd,
                      [(dmerged, D, 0), (br_a, D, 0), (br_b, D, 0), (p, D, OFF_GA // D), (p, D, OFF_GB // D)], [],
                      [("new", D, BF16), ("new", D, BF16), ("into", dp, 2 * D, OFF_GA // (2 * D))])
    dy_a = matmul("branch_gdn_dx", da, w["w_branch_gdn"], "nt", ti_cap=1376, tj_cap=1024)
    gw["w_branch_gdn"] = matmul("branch_gdn_dw", y_a, da, "tn", ti_cap=1024, tj_cap=1024, tr_cap=2752)
    dy_b = matmul("branch_ret_dx", db_, w["w_branch_ret"], "nt", ti_cap=1376, tj_cap=1024)
    gw["w_branch_ret"] = matmul("branch_ret_dw", y_b, db_, "tn", ti_cap=1024, tj_cap=1024, tr_cap=2752)

    dp, d_ret_w = _scan_bwd("ret_scan_bwd", _ret_chunk, ret_blocks, [], ret_consts, ret_states, dy_b,
                            (dp, OFF_RQ // (4 * D), [0, 1, 2, 3]), nc, shared=[ct, st], params=[ret_w])
    gs["ret_out_norm"] = d_ret_w.reshape(1, D)
    dq, dk, dv, dgcb, dbb, dp, dgc_rows, gs["gdn_out_norm"] = _scan_bwd(
        "gdn_scan_bwd", _gdn_chunk, gdn_blocks, [gc_rows], [], gdn_states, dy_a, (dp, OFF_Z // D, [5]), nc,
        params=[small["gdn_out_norm"]], kept=gdn_inv)
    dgc = jnp.pad(dgc_rows.transpose(0, 2, 1).reshape(lp, H), ((0, 0), (H, LANES - 2 * H)))
    dp, dba, dcw, dgate = gdn_pre_bwd(p, cws, alog_row, dtb_row, dq, dk, dv, dgcb, dbb, dgc, dp, lp, tm)
    dp = row("dp_ba", lambda i, t: t, [(dba, LANES, 0)], [], [("into", dp, LANES, OFF_BA // LANES)])[0]
    gw["gdn_conv_w"] = dcw[:CONV_K]
    gs["gdn_a_log"] = dgate[0:1, H:2 * H]
    gs["gdn_dt_bias"] = dgate[1:2, H:2 * H]

    dh1, gs["mix_norm"] = matmul("mix_proj_dx", dp, w["w_in_p"], "nt", tj_cap=1024, tr_cap=3456, rows=[h1, dh2],
                                 pars=[small["mix_norm"]], accs=[(1, D)], epilogue=norm_bwd_epilogue)
    gw["w_in_p"] = matmul("mix_proj_dw", n2, dp, "tn", ti_cap=1024, tj_cap=1152, tr_cap=2752)
    dh0, gw["ffn1_w_in"], gw["ffn1_w_out"], gs["ffn1_norm"] = ffn_bwd("ffn1", dh1, ffn1_saved, small["ffn1_norm"])
    gw["meta_tokens"] = dh0[PAD:HEAD_ROWS]
    return loss_row, dh0[HEAD_ROWS:], gw, gs


BIG = ("ffn1_w_in", "ffn1_w_out", "w_in", "w_branch_gdn", "w_branch_ret", "w_out", "ffn2_w_in", "ffn2_w_out")
COL_SHARDED = ("ffn1_w_in", "w_in", "ffn2_w_in")
WITH_FFN1_UP = ("ffn1_w_out", "w_in")
WITH_MIX_PROJ = ("w_branch_gdn", "w_branch_ret", "w_out", "ffn2_w_in", "ffn2_w_out")
EARLY_GRADS = ("ffn2_w_in", "ffn2_w_out", "w_in", "w_branch_gdn", "w_branch_ret", "w_out")
LATE_GRADS = ("ffn1_w_in", "ffn1_w_out")
SMALL = ("ffn1_norm", "mix_norm", "ret_out_norm", "ffn2_norm", "final_norm", "gdn_out_norm", "gdn_a_log", "gdn_dt_bias")
WEIGHTS = ("meta_tokens", "ffn1_norm", "ffn1_w_in", "ffn1_w_out", "mix_norm", "w_in", "gdn_conv_w", "gdn_a_log", "gdn_dt_bias",
           "gdn_out_norm", "ret_out_norm", "w_branch_gdn", "w_branch_ret", "w_out", "ffn2_norm", "ffn2_w_in", "ffn2_w_out",
           "final_norm")
LOSS_ROW = 6
CONV_ROW0, META_ROW0, SMALL_ROWS = 8, 24, 40


def pack_small(vals):
    rows = [vals[n].reshape(1, D) for n in SMALL[:5]]
    r5 = jnp.concatenate([vals["gdn_out_norm"].reshape(1, DH), vals["gdn_a_log"].reshape(1, H), vals["gdn_dt_bias"].reshape(1, H),
                          jnp.zeros((1, D - DH - 2 * H), F32)], axis=1)
    return jnp.concatenate(rows + [r5, jnp.zeros((2, D), F32)], axis=0)


def unpack_small(packed, shapes):
    out = {n: packed[j].reshape(shapes[n]) for j, n in enumerate(SMALL[:5])}
    out["gdn_out_norm"] = packed[5, :DH].reshape(shapes["gdn_out_norm"])
    out["gdn_a_log"] = packed[5, DH:DH + H].reshape(shapes["gdn_a_log"])
    out["gdn_dt_bias"] = packed[5, DH + H:DH + 2 * H].reshape(shapes["gdn_dt_bias"])
    return out


def kernel(x, meta_tokens, ffn1_norm, ffn1_w_in, ffn1_w_out, mix_norm, w_in, gdn_conv_w, gdn_a_log, gdn_dt_bias, gdn_out_norm, ret_out_norm, w_branch_gdn, w_branch_ret, w_out, ffn2_norm, ffn2_w_in, ffn2_w_out, final_norm, loss_target, m_meta_tokens, m_ffn1_norm, m_ffn1_w_in, m_ffn1_w_out, m_mix_norm, m_w_in, m_gdn_conv_w, m_gdn_a_log, m_gdn_dt_bias, m_gdn_out_norm, m_ret_out_norm, m_w_branch_gdn, m_w_branch_ret, m_w_out, m_ffn2_norm, m_ffn2_w_in, m_ffn2_w_out, m_final_norm, v_meta_tokens, v_ffn1_norm, v_ffn1_w_in, v_ffn1_w_out, v_mix_norm, v_w_in, v_gdn_conv_w, v_gdn_a_log, v_gdn_dt_bias, v_gdn_out_norm, v_ret_out_norm, v_w_branch_gdn, v_w_branch_ret, v_w_out, v_ffn2_norm, v_ffn2_w_in, v_ffn2_w_out, v_final_norm):
    a = dict(locals())
    wts = {n: a[n] for n in WEIGHTS}
    mom_m = {n: a["m_" + n] for n in WEIGHTS}
    mom_v = {n: a["v_" + n] for n in WEIGHTS}
    shapes = {n: wts[n].shape for n in WEIGHTS}
    flat = lambda t: t.reshape(t.shape[-2:])
    c = lax.axis_index("c")
    chip = 2 * lax.axis_index("x") + lax.axis_index("y")

    exact = jnp.zeros((16, D), F32).at[0:3].set(wts["gdn_conv_w"].reshape(3, D)).at[3:7].set(wts["meta_tokens"].reshape(4, D))
    bf16_block = lambda n: flat(wts[n]).astype(BF16)
    w = {}

    def take_weights(names):
        def take(gathered):
            for n, g in zip(names, gathered):
                if n == "w_in":
                    w["w_in_p"] = _w_in_padded_from_blocks(g)
                elif n in COL_SHARDED:
                    w[n], w[n + "_il"] = g, jnp.concatenate([g[q] for q in INTERLEAVED], axis=1)
                else:
                    w[n] = g.reshape(N_CHIPS * g.shape[1], D)
        return take

    first = run_exchange("gather_first", gather_exchange([bf16_block("ffn1_w_in"), exact]))
    take_weights(["ffn1_w_in"])(first[:1])
    exact = first[1]
    w["gdn_conv_w"] = jnp.concatenate([exact[q, 0:3].reshape(CONV_K, 3 * D // 4) for q in range(N_CHIPS)], axis=1)
    w["meta_tokens"] = jnp.concatenate([exact[q, 3:7].reshape(N_META, D // 4) for q in range(N_CHIPS)], axis=1)
    small = {n: wts[n].reshape(1, -1) for n in SMALL}
    hooks = {"ffn1_up": (lambda gw: gather_exchange([bf16_block(n) for n in WITH_FFN1_UP]), take_weights(WITH_FFN1_UP)),
             "mix_proj": (lambda gw: gather_exchange([bf16_block(n) for n in WITH_MIX_PROJ]), take_weights(WITH_MIX_PROJ))}

    def blocks(gw, n):
        if n == "w_in":
            return _w_in_grad_blocks(gw["w_in_p"])
        return gw[n] if n in COL_SHARDED else gw[n].reshape(N_CHIPS, gw[n].shape[0] // N_CHIPS, D)

    chip_sum, from_chips = {}, {}

    def swap_early(gw):
        chip_sum["g"] = [blocks(gw, n) for n in EARLY_GRADS]
        return sibling_halves_exchange(chip_sum["g"])

    def sum_early(from_sib):
        for n, g, r in zip(EARLY_GRADS, chip_sum.pop("g"), from_sib):
            chip_sum[n] = chip_sums(f"grads_chip_sum_{n}", g, r, c)

    def scatter_of(names):
        return (lambda gw: scatter_chips_exchange([chip_sum[n] for n in names])), (lambda got: from_chips.update(zip(names, got)))
    hooks["ffn1_dmid"] = (swap_early, sum_early)
    hooks["ffn1_dn"] = scatter_of(["w_in"])
    hooks["ffn1_dwin"] = scatter_of([n for n in EARLY_GRADS if n != "w_in"])

    loss_row, gx, gw, gs = local_step(x[0], loss_target[0], w, small, hooks)

    late = [blocks(gw, n) for n in LATE_GRADS]
    from_sib = run_exchange("grads_sibling_late", sibling_halves_exchange(late))
    for n, g, r in zip(LATE_GRADS, late, from_sib):
        chip_sum[n] = chip_sums(f"grads_chip_sum_{n}", g, r, c)
    from_chips.update(zip(LATE_GRADS, run_exchange("grads_scatter_late", scatter_chips_exchange([chip_sum[n] for n in LATE_GRADS]))))
    halves = [sum_slots(f"grads_sum_{n}", [(chip_sum[n], chip), (from_chips[n], 0), (from_chips[n], 1), (from_chips[n], 2)])
              for n in BIG]
    others = sibling_swap("grads_swap", halves)
    grads, delta, new_m, new_v = {}, {}, {}, {}
    for n, mine, other in zip(BIG, halves, others):
        res = adamw_halves(f"adamw_{n}", flat(wts[n]), mine, other, flat(mom_m[n]), flat(mom_v[n]), c)
        grads[n], delta[n], new_m[n], new_v[n] = (t.reshape(shapes[n]) for t in res)

    sm = jnp.concatenate([pack_small(gs).at[LOSS_ROW, :LANES].set(loss_row[0]),
                          gw["gdn_conv_w"].reshape(3 * CONV_K, D), jnp.zeros((META_ROW0 - CONV_ROW0 - 3 * CONV_K, D), F32),
                          gw["meta_tokens"]], axis=0)
    every = allgather_all("small_gather", sm)
    sm_sum = sum_slots("small_sum", [(every, s) for s in range(8)])
    d_s, m_s, v_s = adamw("adamw_small", pack_small(small), sm_sum[:8], pack_small({n: mom_m[n].reshape(1, -1) for n in SMALL}),
                          pack_small({n: mom_v[n].reshape(1, -1) for n in SMALL}))
    grads.update(unpack_small(sm_sum, shapes))
    delta.update(unpack_small(d_s, shapes))
    new_m.update(unpack_small(m_s, shapes))
    new_v.update(unpack_small(v_s, shapes))
    g_conv = lax.dynamic_slice_in_dim(sm_sum[CONV_ROW0:CONV_ROW0 + 3 * CONV_K].reshape(CONV_K, 3 * D), chip * (3 * D // 4), 3 * D // 4, 1)
    g_meta = lax.dynamic_slice_in_dim(sm_sum[META_ROW0:META_ROW0 + N_META], chip * (D // 4), D // 4, 1)
    for n, g in (("gdn_conv_w", g_conv), ("meta_tokens", g_meta)):
        d_, m_, v_ = adamw(f"adamw_{n}", flat(wts[n]), g, flat(mom_m[n]), flat(mom_v[n]))
        grads[n], delta[n], new_m[n], new_v[n] = (t.reshape(shapes[n]) for t in (g, d_, m_, v_))
    loss = sm_sum[LOSS_ROW, 0]

    return (loss, gx[None], *[grads[n] for n in WEIGHTS], *[delta[n] for n in WEIGHTS], *[new_m[n] for n in WEIGHTS],
            *[new_v[n] for n in WEIGHTS])
```

```python
import collections
import functools

import jax
import jax.numpy as jnp
from jax import lax
from jax.experimental import pallas as pl
from jax.experimental.pallas import tpu as pltpu

F32 = jnp.float32
BF16 = jnp.bfloat16
HI = lax.Precision.HIGHEST
MESH = pl.DeviceIdType.MESH

D = 1024
N_META = 16
PAD = 48
HEAD_ROWS = PAD + N_META
CH = 64
H = 8
DH = 128
DFF = 2816
CONV_K = 4
EPS = 1e-6
ROPE_BASE = 10000.0
LANES = 128
N_CHIPS = 4
VMEM_LIMIT = 56 * 2 ** 20
DOWN_ROWS = 688
INTERLEAVED = (0, 2, 1, 3)

OFF_QKV, OFF_Z, OFF_RQ, OFF_RK, OFF_RV, OFF_RG, OFF_GA, OFF_GB, OFF_BA = 0, 3072, 4096, 5120, 6144, 7168, 8192, 9216, 10240
PW = 10368
D_PROJ = 10256

ADAM_LR, ADAM_B1, ADAM_B2, ADAM_EPS, ADAM_WD, ADAM_STEP = 0.001, 0.9, 0.999, 1e-08, 0.01, 10


def _pick(n, cap, mult):
    best = None
    for t in range(mult, min(n, cap) + 1, mult):
        if n % t == 0:
            best = t
    return best if best is not None else n


def _tile_rows(rows, cols, block_bytes=3 * 2 ** 19):
    return _pick(rows, max(8, block_bytes // (4 * cols)), 8)


def _dot(a, b, dims, prec=None):
    return lax.dot_general(a, b, (dims, ((), ())), precision=prec, preferred_element_type=F32)


def _mmh(a, b):
    return _dot(a, b, ((1,), (0,)), HI)


def _split(a):
    hi = a.astype(BF16)
    return hi, (a - hi.astype(F32)).astype(BF16)


_NN, _NT, _TN = ((2,), (1,)), ((2,), (2,)), ((1,), (1,))


def _bdot(a, b, dims):
    return lax.dot_general(a, b, (dims, ((0,), (0,))), preferred_element_type=F32)


def _bmm(a, b, dims):
    return _bdot(a.astype(BF16), b.astype(BF16), dims)


def _bdot3(a, b, dims):
    ah, al = _split(a)
    bh, bl = _split(b)
    if dims == _NN:
        m = a.shape[1]
        both = _bdot(jnp.concatenate([ah, al], axis=1), bh, dims)
        return both[:, :m] + (both[:, m:] + _bdot(ah, bl, dims))
    return _bdot(ah, bh, dims) + (_bdot(ah, bl, dims) + _bdot(al, bh, dims))


def matmul(name, a, b, mode, *, ti_cap=688, tj_cap=512, tr_cap=1408, b_split=False, o_split=False, pair=False,
           rows=(), pars=(), epilogue=None, outs=(F32,), accs=(), side=None, wide=1, o_interleaved=False):
    if mode == "nn":
        I, R = a.shape
        J = N_CHIPS * b.shape[2] if b_split else b.shape[1]
    elif mode == "nt":
        (I, R), J = a.shape, b.shape[0]
    else:
        (R, I), J = a.shape, b.shape[1]
    ti = _pick(I, ti_cap, 16) if mode != "tn" else _pick(I, ti_cap, LANES)
    tj = _pick(J, tj_cap, LANES)
    tr = _pick(R, tr_cap, LANES) if mode != "tn" else _pick(R, tr_cap, 16)
    half = N_CHIPS // 2
    if mode == "nn":
        a_spec = pl.BlockSpec((ti, tr), lambda i, j, r: (i, r))
        if b_split:
            tj = J // N_CHIPS
            b_spec = pl.BlockSpec((None, tr, tj), lambda i, j, r: (j, r, 0))
            b2_spec = pl.BlockSpec((None, tr, tj), lambda i, j, r: (j + half, r, 0))
        else:
            b_spec = pl.BlockSpec((tr, tj), lambda i, j, r: (r, j))
        dims = ((1,), (0,))
    elif mode == "nt":
        a_spec = pl.BlockSpec((ti, tr), lambda i, j, r: (i, r))
        b_spec = pl.BlockSpec((tj, tr), lambda i, j, r: (j, r))
        dims = ((1,), (1,))
    else:
        if o_split:
            tj = J // N_CHIPS
        a_spec = pl.BlockSpec((tr, ti), lambda i, j, r: (r, i))
        b_spec = pl.BlockSpec((tr, tj), lambda i, j, r: (r, j))
        dims = ((0,), (0,))
    assert not pair or (mode == "nn" and b_split), name
    j_out = J // 2 if pair else J
    nr = R // tr
    assert I % ti == 0 and j_out % tj == 0 and R % tr == 0, (name, I, J, R, ti, tj, tr)
    grid = (I // ti, j_out // tj, nr)
    w_spec = o_spec = pl.BlockSpec((ti, tj), lambda i, j, r: (i, j))
    if o_split:
        assert epilogue is None, name
        blk = (lambda j: (j % 2) * 2 + j // 2) if o_interleaved else (lambda j: j)
        w_spec = pl.BlockSpec((None, ti, tj), lambda i, j, r: (blk(j), i, 0))
        out_shapes = [jax.ShapeDtypeStruct((N_CHIPS, I, tj), outs[0])]
    else:
        w_spec = pl.BlockSpec((ti, wide * tj), lambda i, j, r: (i, j))
        out_shapes = [jax.ShapeDtypeStruct((I, wide * j_out), dt) for dt in outs]
    n_prod = 2 if pair else 1
    n_out, n_acc, n_rows, n_pars = len(out_shapes), len(accs), len(rows), len(pars)
    n_si = len(side.ins) if side else 0
    n_so = len(side.out_shapes) if side else 0
    n_in = 1 + n_prod + n_rows + n_pars + n_si
    n_steps = grid[0] * grid[1] * grid[2]

    def body(*refs):
        a_ref, b_refs = refs[0], refs[1:1 + n_prod]
        row_refs = refs[1 + n_prod:1 + n_prod + n_rows]
        par_refs = refs[1 + n_prod + n_rows:1 + n_prod + n_rows + n_pars]
        out_refs = refs[n_in:n_in + n_out]
        acc_refs = refs[n_in + n_out:n_in + n_out + n_acc]
        scr = refs[n_in + n_out + n_acc + n_so:]
        partial = scr[:n_prod] if nr > 1 else ()
        step = (pl.program_id(0) * grid[1] + pl.program_id(1)) * grid[2] + pl.program_id(2)
        if side:
            s_refs = (refs[n_in - n_si:n_in], refs[n_in + n_out + n_acc:n_in + n_out + n_acc + n_so], scr[len(partial):])

        if side or n_acc:
            @pl.when(step == 0)
            def _():
                if side:
                    side.start(*s_refs)
                for r in acc_refs:
                    r[...] = jnp.zeros_like(r)

        lhs = a_ref[...].astype(BF16)
        prods = [_dot(lhs, r[...].astype(BF16), dims) for r in b_refs]

        def finish(prods):
            res = epilogue(prods, *[r[...] for r in row_refs], *[r[...] for r in par_refs]) if epilogue else prods
            for r, v in zip(out_refs, res[:n_out]):
                r[...] = v.astype(r.dtype)
            for r, v in zip(acc_refs, res[n_out:]):
                r[...] += v

        if nr == 1:
            finish(prods)
        else:
            k = pl.program_id(2)

            @pl.when(k == 0)
            def _():
                for r, v in zip(partial, prods):
                    r[...] = v

            @pl.when(k > 0)
            def _():
                for r, v in zip(partial, prods):
                    r[...] += v

            @pl.when(k == nr - 1)
            def _():
                finish([r[...] for r in partial])

        if side:
            @pl.when(step == n_steps - 1)
            def _():
                side.finish(*s_refs)

    ins = [a, b] + ([b] if pair else []) + list(rows) + list(pars)
    in_specs = [a_spec, b_spec] + ([b2_spec] if pair else []) + [o_spec] * n_rows
    in_specs += [pl.BlockSpec(p.shape, lambda i, j, r, nd=p.ndim: (0,) * nd) for p in pars]
    out_specs = [w_spec] * n_out + [pl.BlockSpec(s, lambda i, j, r: (0, 0)) for s in accs]
    out_shapes += [jax.ShapeDtypeStruct(s, F32) for s in accs]
    scratch = [pltpu.VMEM((ti, tj), F32)] * n_prod if nr > 1 else []
    ordered = bool(side) or n_acc > 0
    res = pl.pallas_call(
        body, name=name, grid=grid, in_specs=in_specs + _any_specs(n_si), out_specs=out_specs + _any_specs(n_so),
        out_shape=out_shapes + (list(side.out_shapes) if side else []), scratch_shapes=scratch + (list(side.scratch) if side else []),
        compiler_params=pltpu.CompilerParams(
            dimension_semantics=("arbitrary",) * 3 if ordered else ("parallel", "parallel", "arbitrary"), vmem_limit_bytes=VMEM_LIMIT),
    )(*ins, *(side.ins if side else []))
    own = res[:n_out + n_acc]
    own = own[0] if len(own) == 1 else tuple(own)
    return (own, list(res[n_out + n_acc:])) if side else own


def rowwise(name, fn, rows, pars, outs, accs=(), *, n_rows, tm):
    nt = n_rows // tm
    assert nt * tm == n_rows
    in_specs, ins = [], []
    for arr, w, cb in rows:
        in_specs.append(pl.BlockSpec((tm, w), lambda i, cb=cb: (i, cb)))
        ins.append(arr)
    for p in pars:
        in_specs.append(pl.BlockSpec(p.shape, lambda i, nd=p.ndim: (0,) * nd))
        ins.append(p)
    n_in = len(ins)
    out_specs, out_shapes, aliases = [], [], {}
    for k, o in enumerate(outs):
        if o[0] == "new":
            _, w, dt = o
            out_specs.append(pl.BlockSpec((tm, w), lambda i: (i, 0)))
            out_shapes.append(jax.ShapeDtypeStruct((n_rows, w), dt))
        else:
            _, arr, w, cb = o
            in_specs.append(pl.BlockSpec(memory_space=pl.ANY))
            aliases[len(ins)] = k
            ins.append(arr)
            out_specs.append(pl.BlockSpec((tm, w), lambda i, cb=cb: (i, cb)))
            out_shapes.append(jax.ShapeDtypeStruct(arr.shape, arr.dtype))
    for r, w in accs:
        out_specs.append(pl.BlockSpec((r, w), lambda i: (0, 0)))
        out_shapes.append(jax.ShapeDtypeStruct((r, w), F32))
    n_all_in, n_out, n_acc = len(ins), len(outs), len(accs)

    def body(*refs):
        i = pl.program_id(0)
        vals = [r[...] for r in refs[:n_in]]
        res = fn(i, *vals)
        if not isinstance(res, (tuple, list)):
            res = (res,)
        o_refs = refs[n_all_in:n_all_in + n_out]
        a_refs = refs[n_all_in + n_out:]
        for r, v in zip(o_refs, res[:n_out]):
            r[...] = v.astype(r.dtype)
        if n_acc:
            @pl.when(i == 0)
            def _():
                for r in a_refs:
                    r[...] = jnp.zeros_like(r)
            for r, v in zip(a_refs, res[n_out:]):
                r[...] += v

    return pl.pallas_call(
        body, name=name, grid=(nt,), in_specs=in_specs, out_specs=out_specs, out_shape=out_shapes,
        input_output_aliases=aliases,
        compiler_params=pltpu.CompilerParams(dimension_semantics=("arbitrary",), vmem_limit_bytes=VMEM_LIMIT),
    )(*ins)


def _row_ids(i, tm):
    return i * tm + lax.broadcasted_iota(jnp.int32, (tm, 1), 0)


def _sigmoid(x):
    return 1.0 / (1.0 + jnp.exp(-x))


def _silu(x):
    return x * _sigmoid(x)


def _softplus(x):
    return jnp.maximum(x, 0.0) + jnp.log(1.0 + jnp.exp(-jnp.abs(x)))


def _rms(x, w):
    return x * lax.rsqrt(jnp.mean(x * x, axis=-1, keepdims=True) + EPS) * w


def _heads(fn, *xs):
    return jnp.concatenate([fn(h, *[x[:, h * DH:(h + 1) * DH] for x in xs]) for h in range(H)], axis=1)


def _merge(a, b, ga, gb):
    return _sigmoid(ga) * a + _sigmoid(gb) * b


def _select_matrix(first_lane):
    r = lax.broadcasted_iota(jnp.int32, (LANES, H * DH), 0)
    c = lax.broadcasted_iota(jnp.int32, (LANES, H * DH), 1)
    return (r == first_lane + (c >> 7)).astype(F32)


def _chunk_tri(tm):
    r = lax.broadcasted_iota(jnp.int32, (tm, tm), 0)
    c = lax.broadcasted_iota(jnp.int32, (tm, tm), 1)
    return jnp.logical_and((r >> 6) == (c >> 6), r >= c).astype(F32)


def _gdn_gates(ba, alog_row, dtb_row, mask):
    g = -jnp.exp(alog_row) * _softplus(ba + dtb_row) * mask
    gc = _mmh(_chunk_tri(ba.shape[0]), g)
    beta = _sigmoid(ba) * mask
    return _mmh(gc, _select_matrix(H)), _mmh(beta, _select_matrix(0)), gc


def _gdn_qkv(c):
    a = _silu(c)

    def l2(scale):
        return lambda h, t: t * lax.rsqrt(jnp.sum(t * t, axis=-1, keepdims=True) + EPS) * scale
    q = _heads(l2(DH ** -0.5), a[:, :D])
    k = _heads(l2(1.0), a[:, D:2 * D])
    return q, k, a[:, 2 * D:]


def _rows_from(x, start, n):
    base = start - start % 8
    if base == start:
        return x[start:start + n]
    return pltpu.roll(x, x.shape[0] - start % 8, 0)[base:base + n]


def _conv_shifted(xw, tm):
    return [_rows_from(xw, 5 + j, tm) for j in range(CONV_K)]


def _conv_taps(xw, cws, tm):
    return sum(w * x for w, x in zip(cws, _conv_shifted(xw, tm)))


def _swap_pairs(t):
    n = t.shape[1]
    lane = lax.broadcasted_iota(jnp.int32, t.shape, 1)
    return jnp.where((lane & 1) == 0, pltpu.roll(t, n - 1, 1), pltpu.roll(t, 1, 1))


@jax.custom_vjp
def _unit_lower_inv(a):
    n = -a
    eye = (lax.broadcasted_iota(jnp.int32, a.shape, 1) == lax.broadcasted_iota(jnp.int32, a.shape, 2)).astype(F32)
    p = eye + n
    for _ in range(5):
        n = _bdot3(n, n, _NN)
        p = p + _bdot3(p, n, _NN)
    return p


def _inv_fwd(a):
    t = _unit_lower_inv(a)
    return t, t


def _inv_bwd(t, dt):
    x = _bdot3(t, dt, _TN)
    return (-_bdot3(x, t, _NT),)


_unit_lower_inv.defvjp(_inv_fwd, _inv_bwd)


@jax.custom_vjp
def _unit_lower_inv_known(a, t):
    return t


_unit_lower_inv_known.defvjp(lambda a, t: (t, t), lambda t, dt: (_inv_bwd(t, dt)[0], jnp.zeros_like(t)))


def _gdn_chunk(q, k, v, gc, bb, z, gr, s, norm_w, t_known=None):
    ri = lax.broadcasted_iota(jnp.int32, (H, CH, CH), 1)
    ci = lax.broadcasted_iota(jnp.int32, (H, CH, CH), 2)
    causal = ri >= ci
    gc1 = jnp.sum(gc, axis=2, keepdims=True) * (1.0 / LANES)
    diff = jnp.broadcast_to(gc1, (H, CH, CH)) - jnp.broadcast_to(gr, (H, CH, CH))
    decay = jnp.where(causal, jnp.exp(jnp.where(causal, diff, 0.0)), 0.0)
    kb = k * bb
    sc = _bmm(jnp.concatenate([q, kb], axis=1), k, _NT)
    qk = sc[:, :CH] * decay
    a = jnp.where(ri > ci, sc[:, CH:] * decay, 0.0)
    t = _unit_lower_inv(a) if t_known is None else _unit_lower_inv_known(a, t_known)
    eg = jnp.exp(gc)
    uw = _bmm(t, jnp.concatenate([v * bb, kb * eg], axis=2), _NN)
    g_last = gc[:, CH - 1:CH, :]
    ws = _bmm(jnp.concatenate([uw[:, :, DH:], q * eg], axis=1), s, _NN)
    v_new = uw[:, :, :DH] - ws[:, :CH]
    o = ws[:, CH:] + _bmm(qk, v_new, _NN)
    s_new = s * jnp.exp(g_last) + _bmm(k * jnp.exp(g_last - gc), v_new, _TN)
    y = o * lax.rsqrt(jnp.mean(o * o, axis=-1, keepdims=True) + EPS) * norm_w * _silu(z)
    return y, s_new, t


def _swap_pairs_heads(t):
    return _swap_pairs(t.reshape(t.shape[0] * CH, DH)).reshape(t.shape)


@jax.custom_vjp
def _rope(t, ct, st):
    return t * ct + _swap_pairs_heads(t) * st


def _rope_fwd(t, ct, st):
    return _rope(t, ct, st), (ct, st)


def _rope_bwd(res, d):
    ct, st = res
    return d * ct + _swap_pairs_heads(d * st), jnp.zeros_like(ct), jnp.zeros_like(st)


_rope.defvjp(_rope_fwd, _rope_bwd)


def _ret_chunk(q, k, v, rg, s, decay, xi, zeta, cd, ct, st, norm_w):
    q = _rope(q, ct, st)
    k = _rope(k, ct, st) * (DH ** -0.5)
    scores = _bmm(q, k, _NT) * decay
    o = _bmm(scores, v, _NN) + _bmm(q * xi, s, _NN)
    s_new = s * cd + _bmm(k * zeta, v, _TN)
    xc = o - jnp.mean(o, axis=-1, keepdims=True)
    var = jnp.mean(xc * xc, axis=-1, keepdims=True)
    y = _silu(rg) * (xc * lax.rsqrt(var + EPS) * norm_w)
    return y, s_new


def _head_blocks(ref):
    return jnp.stack([ref[:, h * DH:(h + 1) * DH] for h in range(H)])


def _head_rows(ref):
    return jnp.stack([ref[0, h:h + 1, :] for h in range(H)])


def _scan_fwd(name, chunk_fn, blocks, rowvecs, consts, nc, shared=(), params=(), out_dtype=F32, keep=None):
    n_blk, n_rv, n_c = len(blocks), len(rowvecs), len(consts) + len(shared) + len(params)

    def body(*refs):
        blk = refs[:n_blk]
        rvs = refs[n_blk:n_blk + n_rv]
        cst = refs[n_blk + n_rv:n_blk + n_rv + n_c]
        o_ref, st_ref = refs[n_blk + n_rv + n_c:n_blk + n_rv + n_c + 2]
        s_scr = refs[-1]

        @pl.when(pl.program_id(0) == 0)
        def _():
            s_scr[...] = jnp.zeros_like(s_scr)

        s = s_scr[...]
        st_ref[0] = s
        res = chunk_fn(*[_head_blocks(r) for r in blk], *[_head_rows(r) for r in rvs], s, *[r[...] for r in cst])
        for h in range(H):
            o_ref[:, h * DH:(h + 1) * DH] = res[0][h].astype(o_ref.dtype)
        s_scr[...] = res[1]
        if keep is not None:
            refs[-2][0] = res[2]

    in_specs = [pl.BlockSpec((CH, H * DH), lambda c, f=f: (c, f)) for _, f in blocks]
    in_specs += [pl.BlockSpec((1, H, CH), lambda c: (c, 0, 0)) for _ in rowvecs]
    in_specs += [pl.BlockSpec(a.shape, lambda c: (0, 0, 0)) for a in consts]
    in_specs += [pl.BlockSpec((CH, a.shape[1]), lambda c: (c, 0)) for a in shared]
    in_specs += [pl.BlockSpec(a.shape, lambda c, nd=a.ndim: (0,) * nd) for a in params]
    out_specs = [pl.BlockSpec((CH, H * DH), lambda c: (c, 0)), pl.BlockSpec((1, H, DH, DH), lambda c: (c, 0, 0, 0))]
    out_shape = [jax.ShapeDtypeStruct((nc * CH, H * DH), out_dtype), jax.ShapeDtypeStruct((nc, H, DH, DH), F32)]
    if keep is not None:
        out_specs.append(pl.BlockSpec((1, H) + keep, lambda c: (c, 0, 0, 0)))
        out_shape.append(jax.ShapeDtypeStruct((nc, H) + keep, F32))
    return pl.pallas_call(
        body, name=name, grid=(nc,), in_specs=in_specs, out_specs=out_specs, out_shape=out_shape,
        scratch_shapes=[pltpu.VMEM((H, DH, DH), F32)],
        compiler_params=pltpu.CompilerParams(dimension_semantics=("arbitrary",), vmem_limit_bytes=VMEM_LIMIT),
    )(*[a for a, _ in blocks], *rowvecs, *consts, *shared, *params)


def _scan_bwd(name, chunk_fn, blocks, rowvecs, consts, states, do, into, nc, shared=(), params=(), kept=None):
    n_blk, n_rv, n_c, n_p = len(blocks), len(rowvecs), len(consts) + len(shared), len(params)
    n_kept = 0 if kept is None else 1
    packed = list(into[2]) if into else []
    fresh = [k for k in range(n_blk) if k not in packed]
    n_bo = len(fresh) + (1 if into else 0)

    def body(*refs):
        blk = refs[:n_blk]
        rvs = refs[n_blk:n_blk + n_rv]
        cst = refs[n_blk + n_rv:n_blk + n_rv + n_c]
        par = refs[n_blk + n_rv + n_c:n_blk + n_rv + n_c + n_p]
        st_ref, do_ref = refs[n_blk + n_rv + n_c + n_p:n_blk + n_rv + n_c + n_p + 2]
        n_in = n_blk + n_rv + n_c + n_p + 2 + n_kept + (1 if into else 0)
        known = [refs[n_blk + n_rv + n_c + n_p + 2][0]] if n_kept else []
        o_refs = refs[n_in:n_in + n_bo]
        rv_refs = refs[n_in + n_bo:n_in + n_bo + n_rv]
        p_refs = refs[n_in + n_bo + n_rv:n_in + n_bo + n_rv + n_p]
        ds_scr = refs[n_in + n_bo + n_rv + n_p]

        @pl.when(pl.program_id(0) == 0)
        def _():
            ds_scr[...] = jnp.zeros_like(ds_scr)
            for r in p_refs:
                r[...] = jnp.zeros_like(r)

        cv = [r[...] for r in cst]
        n_d = n_blk + n_rv + 1
        _, vjp = jax.vjp(lambda *a: chunk_fn(*a[:n_d], *cv, *a[n_d:], *known)[:2], *[_head_blocks(r) for r in blk],
                         *[_head_rows(r) for r in rvs], st_ref[0], *[r[...] for r in par])
        grads = vjp((_head_blocks(do_ref), ds_scr[...]))
        for h in range(H):
            for pos, k in enumerate(fresh):
                o_refs[pos][:, h * DH:(h + 1) * DH] = grads[k][h]
            for pos, k in enumerate(packed):
                col = pos * H * DH + h * DH
                o_refs[-1][:, col:col + DH] = grads[k][h].astype(o_refs[-1].dtype)
            for r, g in zip(rv_refs, grads[n_blk:n_blk + n_rv]):
                r[0, h:h + 1, :] = g[h]
        ds_scr[...] = grads[n_blk + n_rv]
        for r, g in zip(p_refs, grads[n_d:]):
            r[...] += g

    rc = lambda c: nc - 1 - c
    in_specs = [pl.BlockSpec((CH, H * DH), lambda c, f=f: (rc(c), f)) for _, f in blocks]
    in_specs += [pl.BlockSpec((1, H, CH), lambda c: (rc(c), 0, 0)) for _ in rowvecs]
    in_specs += [pl.BlockSpec(a.shape, lambda c: (0, 0, 0)) for a in consts]
    in_specs += [pl.BlockSpec((CH, a.shape[1]), lambda c: (rc(c), 0)) for a in shared]
    in_specs += [pl.BlockSpec(a.shape, lambda c, nd=a.ndim: (0,) * nd) for a in params]
    in_specs += [pl.BlockSpec((1, H, DH, DH), lambda c: (rc(c), 0, 0, 0)), pl.BlockSpec((CH, H * DH), lambda c: (rc(c), 0))]
    ins = [a for a, _ in blocks] + list(rowvecs) + list(consts) + list(shared) + list(params) + [states, do]
    if n_kept:
        in_specs.append(pl.BlockSpec((1,) + kept.shape[1:], lambda c: (rc(c), 0, 0, 0)))
        ins.append(kept)
    out_specs, out_shapes, aliases = [], [], {}
    for _ in fresh:
        out_specs.append(pl.BlockSpec((CH, H * DH), lambda c: (rc(c), 0)))
        out_shapes.append(jax.ShapeDtypeStruct((nc * CH, H * DH), F32))
    if into:
        arr, f, _ = into
        in_specs.append(pl.BlockSpec(memory_space=pl.ANY))
        aliases[len(ins)] = len(fresh)
        ins.append(arr)
        out_specs.append(pl.BlockSpec((CH, len(packed) * H * DH), lambda c: (rc(c), f)))
        out_shapes.append(jax.ShapeDtypeStruct(arr.shape, arr.dtype))
    for _ in rowvecs:
        out_specs.append(pl.BlockSpec((1, H, CH), lambda c: (rc(c), 0, 0)))
        out_shapes.append(jax.ShapeDtypeStruct((nc, H, CH), F32))
    for a in params:
        out_specs.append(pl.BlockSpec(a.shape, lambda c, nd=a.ndim: (0,) * nd))
        out_shapes.append(jax.ShapeDtypeStruct(a.shape, F32))
    return pl.pallas_call(
        body, name=name, grid=(nc,), in_specs=in_specs, out_specs=out_specs, out_shape=out_shapes,
        input_output_aliases=aliases, scratch_shapes=[pltpu.VMEM((H, DH, DH), F32)],
        compiler_params=pltpu.CompilerParams(dimension_semantics=("arbitrary",), vmem_limit_bytes=VMEM_LIMIT),
    )(*ins)


def _gdn_pre_specs(p, cws, alog_row, dtb_row, tm, pos):
    sub = tm // 8
    rows = [pl.BlockSpec((tm, 3 * D), lambda i: (pos(i), 0)),
            pl.BlockSpec((8, 3 * D), lambda i: (jnp.maximum(pos(i) * sub - 1, 0), 0)),
            pl.BlockSpec((tm, LANES), lambda i: (pos(i), OFF_BA // LANES))]
    pars = [pl.BlockSpec(a.shape, lambda i: (0, 0)) for a in (*cws, alog_row, dtb_row)]
    return rows + pars, [p, p, p, *cws, alog_row, dtb_row]


def gdn_pre_fwd(p, cws, alog_row, dtb_row, lp, tm):
    def body(x_ref, prev_ref, ba_ref, c0, c1, c2, c3, al_ref, dt_ref, q_ref, k_ref, v_ref, g_ref, b_ref, gc_ref):
        i = pl.program_id(0)
        prev = jnp.where(i > 0, prev_ref[...], 0.0)
        xw = jnp.concatenate([prev, x_ref[...]], axis=0)
        c = _conv_taps(xw, [c0[...], c1[...], c2[...], c3[...]], tm)
        q, k, v = _gdn_qkv(c)
        mask = (_row_ids(i, tm) >= PAD).astype(F32)
        g, b, gc = _gdn_gates(ba_ref[...], al_ref[...], dt_ref[...], mask)
        q_ref[...] = q
        k_ref[...] = k
        v_ref[...] = v
        g_ref[...] = g
        b_ref[...] = b
        gc_ref[...] = gc

    in_specs, ins = _gdn_pre_specs(p, cws, alog_row, dtb_row, tm, lambda i: i)
    o_spec = pl.BlockSpec((tm, D), lambda i: (i, 0))
    return pl.pallas_call(
        body, name="gdn_pre_fwd", grid=(lp // tm,), in_specs=in_specs,
        out_specs=[o_spec] * 5 + [pl.BlockSpec((tm, LANES), lambda i: (i, 0))],
        out_shape=[jax.ShapeDtypeStruct((lp, D), F32)] * 5 + [jax.ShapeDtypeStruct((lp, LANES), F32)],
        compiler_params=pltpu.CompilerParams(dimension_semantics=("parallel",), vmem_limit_bytes=VMEM_LIMIT),
    )(*ins)


def gdn_pre_bwd(p, cws, alog_row, dtb_row, dq, dk, dv, dg, db, dgc, dp, lp, tm):
    nt = lp // tm
    pos = lambda i: nt - 1 - i

    def body(x_ref, prev_ref, ba_ref, c0, c1, c2, c3, al_ref, dt_ref, dq_ref, dk_ref, dv_ref, dg_ref, db_ref, dgc_ref,
             dp_any, dx_ref, dba_ref, dcw_ref, dpar_ref, carry):
        i = pl.program_id(0)
        t = pos(i)

        @pl.when(i == 0)
        def _():
            carry[...] = jnp.zeros_like(carry)
            dcw_ref[...] = jnp.zeros_like(dcw_ref)
            dpar_ref[...] = jnp.zeros_like(dpar_ref)

        cws_v = [c0[...], c1[...], c2[...], c3[...]]
        prev = jnp.where(t > 0, prev_ref[...], 0.0)
        xw = jnp.concatenate([prev, x_ref[...]], axis=0)
        xs = _conv_shifted(xw, tm)
        c = sum(cws_v[j] * xs[j] for j in range(CONV_K))
        _, vjp_qkv = jax.vjp(_gdn_qkv, c)
        (dc,) = vjp_qkv((dq_ref[...], dk_ref[...], dv_ref[...]))
        zeros8 = jnp.zeros((8, 3 * D), F32)
        dcp = jnp.concatenate([zeros8, dc, zeros8], axis=0)
        dxw = sum(cws_v[j] * _rows_from(dcp, 3 - j, tm + 8) for j in range(CONV_K))
        dx_ref[...] = jnp.concatenate([dxw[8:tm], dxw[tm:] + carry[...]], axis=0).astype(dx_ref.dtype)
        carry[...] = dxw[:8]
        for j in range(CONV_K):
            dcw_ref[j:j + 1, :] += jnp.sum(dc * xs[j], axis=0, keepdims=True)
        mask = (_row_ids(t, tm) >= PAD).astype(F32)
        _, vjp_g = jax.vjp(lambda ba, al, dt: _gdn_gates(ba, al, dt, mask), ba_ref[...], al_ref[...], dt_ref[...])
        dba, dal, ddt = vjp_g((dg_ref[...], db_ref[...], dgc_ref[...]))
        dba_ref[...] = dba
        dpar_ref[0:1, :] += dal
        dpar_ref[1:2, :] += ddt

    in_specs, ins = _gdn_pre_specs(p, cws, alog_row, dtb_row, tm, pos)
    g_spec = pl.BlockSpec((tm, D), lambda i: (pos(i), 0))
    s_spec = pl.BlockSpec((tm, LANES), lambda i: (pos(i), 0))
    in_specs += [g_spec] * 5 + [s_spec, pl.BlockSpec(memory_space=pl.ANY)]
    ins += [dq, dk, dv, dg, db, dgc, dp]
    return pl.pallas_call(
        body, name="gdn_pre_bwd", grid=(nt,), in_specs=in_specs,
        out_specs=[pl.BlockSpec((tm, 3 * D), lambda i: (pos(i), 0)), s_spec,
                   pl.BlockSpec((8, 3 * D), lambda i: (0, 0)), pl.BlockSpec((8, LANES), lambda i: (0, 0))],
        out_shape=[jax.ShapeDtypeStruct(dp.shape, dp.dtype), jax.ShapeDtypeStruct((lp, LANES), F32),
                   jax.ShapeDtypeStruct((8, 3 * D), F32), jax.ShapeDtypeStruct((8, LANES), F32)],
        input_output_aliases={len(ins) - 1: 0},
        scratch_shapes=[pltpu.VMEM((8, 3 * D), F32)],
        compiler_params=pltpu.CompilerParams(dimension_semantics=("arbitrary",), vmem_limit_bytes=VMEM_LIMIT),
    )(*ins)


def _me():
    return lax.axis_index("x"), lax.axis_index("y"), lax.axis_index("c")


def _any_specs(n):
    return [pl.BlockSpec(memory_space=pl.ANY)] * n


Exchange = collections.namedtuple("Exchange", "ins out_shapes scratch start finish")


def _dma_sems(*shape):
    return pltpu.SemaphoreType.DMA(shape)


def run_exchange(name, ex):
    n_i, n_o = len(ex.ins), len(ex.out_shapes)

    def body(*refs):
        parts = (refs[:n_i], refs[n_i:n_i + n_o], refs[n_i + n_o:])
        ex.start(*parts)
        ex.finish(*parts)

    return pl.pallas_call(body, name=name, out_shape=list(ex.out_shapes), in_specs=_any_specs(n_i), out_specs=_any_specs(n_o),
                          scratch_shapes=list(ex.scratch))(*ex.ins)


def gather_exchange(ws):
    n = len(ws)

    def copies(w_refs, o_refs, sems):
        send_sems, recv_sems, fsend_sems, frecv_sems, osend_sems, orecv_sems = sems
        x, y, c = _me()
        me = 2 * x + y
        chips = [(1 - x, y), (x, 1 - y), (1 - x, 1 - y)]

        def own(a):
            return pltpu.make_async_remote_copy(src_ref=w_refs[a], dst_ref=o_refs[a].at[me], send_sem=osend_sems.at[a],
                                                recv_sem=orecv_sems.at[a], device_id=(x, y, 1 - c), device_id_type=MESH)

        def rows(a, cc):
            rh = ws[a].shape[0] // 2
            return pl.ds(pl.multiple_of(cc * rh, 8), rh)

        def ici(a, j, slot):
            px, py = chips[j]
            return pltpu.make_async_remote_copy(
                src_ref=w_refs[a].at[rows(a, c)], dst_ref=o_refs[a].at[slot, rows(a, c)], send_sem=send_sems.at[a, j],
                recv_sem=recv_sems.at[a, j], device_id=(px, py, c), device_id_type=MESH)

        def d2d(a, j, cc):
            px, py = chips[j]
            blk = o_refs[a].at[2 * px + py, rows(a, cc)]
            return pltpu.make_async_remote_copy(src_ref=blk, dst_ref=blk, send_sem=fsend_sems.at[a, j],
                                                recv_sem=frecv_sems.at[a, j], device_id=(x, y, 1 - c), device_id_type=MESH)
        mine = [own(a) for a in range(n)]
        sends = [ici(a, j, me) for a in range(n) for j in range(3)]
        arrivals = [ici(a, j, 2 * px + py) for a in range(n) for j, (px, py) in enumerate(chips)]
        passes = [d2d(a, j, c) for a in range(n) for j in range(3)]
        passed_to_me = [d2d(a, j, 1 - c) for a in range(n) for j in range(3)]
        return mine, sends, arrivals, passes, passed_to_me

    def start(w_refs, o_refs, sems):
        mine, sends, _, _, _ = copies(w_refs, o_refs, sems)
        for cp in mine + sends:
            cp.start()

    def finish(w_refs, o_refs, sems):
        mine, sends, arrivals, passes, passed_to_me = copies(w_refs, o_refs, sems)
        for arrival, onward in zip(arrivals, passes):
            arrival.wait_recv()
            onward.start()
        for cp in passed_to_me + mine:
            cp.wait_recv()
        for cp in sends + passes + mine:
            cp.wait_send()

    return Exchange(list(ws), [jax.ShapeDtypeStruct((N_CHIPS,) + w.shape, w.dtype) for w in ws],
                    [_dma_sems(n, 3), _dma_sems(n, 3), _dma_sems(n, 3), _dma_sems(n, 3), _dma_sems(n), _dma_sems(n)], start, finish)


def _simple_exchange(ins, out_shapes, sem_shape, copies):
    def start(i_refs, o_refs, sems):
        for cp in copies(i_refs, o_refs, *sems):
            cp.start()

    def finish(i_refs, o_refs, sems):
        cps = copies(i_refs, o_refs, *sems)
        for cp in cps:
            cp.wait_recv()
        for cp in cps:
            cp.wait_send()

    return Exchange(list(ins), out_shapes, [_dma_sems(*sem_shape), _dma_sems(*sem_shape)], start, finish)


def sibling_halves_exchange(gs):
    def copies(g_refs, o_refs, send_sems, recv_sems):
        x, y, c = _me()
        cps = []
        for a in range(len(gs)):
            rh = gs[a].shape[1] // 2
            for q in range(N_CHIPS):
                cps.append(pltpu.make_async_remote_copy(
                    src_ref=g_refs[a].at[q, pl.ds(pl.multiple_of((1 - c) * rh, 8), rh)], dst_ref=o_refs[a].at[q],
                    send_sem=send_sems.at[a, q], recv_sem=recv_sems.at[a, q], device_id=(x, y, 1 - c), device_id_type=MESH))
        return cps

    return _simple_exchange(gs, [jax.ShapeDtypeStruct((N_CHIPS, g.shape[1] // 2, g.shape[2]), g.dtype) for g in gs],
                            (len(gs), N_CHIPS), copies)


def scatter_chips_exchange(css):
    def copies(c_refs, o_refs, send_sems, recv_sems):
        x, y, c = _me()
        chips = [(1 - x, y), (x, 1 - y), (1 - x, 1 - y)]
        return [pltpu.make_async_remote_copy(
            src_ref=c_refs[a].at[2 * px + py], dst_ref=o_refs[a].at[j], send_sem=send_sems.at[a, j],
            recv_sem=recv_sems.at[a, j], device_id=(px, py, c), device_id_type=MESH)
            for a in range(len(css)) for j, (px, py) in enumerate(chips)]

    return _simple_exchange(css, [jax.ShapeDtypeStruct((3,) + cs.shape[1:], cs.dtype) for cs in css], (len(css), 3), copies)


def sibling_swap(name, halves):
    n = len(halves)

    def body(*refs):
        h_refs, o_refs = refs[:n], refs[n:2 * n]
        send_sems, recv_sems = refs[2 * n:]
        x, y, c = _me()
        cps = [pltpu.make_async_remote_copy(src_ref=h_refs[a], dst_ref=o_refs[a], send_sem=send_sems.at[a],
                                            recv_sem=recv_sems.at[a], device_id=(x, y, 1 - c), device_id_type=MESH)
               for a in range(n)]
        for cp in cps:
            cp.start()
        for cp in cps:
            cp.wait_recv()
        for cp in cps:
            cp.wait_send()

    return pl.pallas_call(
        body, name=name, out_shape=[jax.ShapeDtypeStruct(h.shape, h.dtype) for h in halves],
        in_specs=_any_specs(n), out_specs=_any_specs(n),
        scratch_shapes=[pltpu.SemaphoreType.DMA((n,)), pltpu.SemaphoreType.DMA((n,))],
    )(*halves)


def allgather_all(name, s):
    def body(s_ref, out_ref, send_sems, recv_sems, local_sem):
        x, y, c = _me()
        peers = [(x ^ ((m >> 2) & 1), y ^ ((m >> 1) & 1), c ^ (m & 1)) for m in range(1, 8)]
        mine = pltpu.make_async_copy(s_ref, out_ref.at[4 * x + 2 * y + c], local_sem)
        mine.start()

        def copy(j, slot):
            return pltpu.make_async_remote_copy(src_ref=s_ref, dst_ref=out_ref.at[slot], send_sem=send_sems.at[j],
                                                recv_sem=recv_sems.at[j], device_id=peers[j], device_id_type=MESH)
        sends = [copy(j, 4 * x + 2 * y + c) for j in range(7)]
        for cp in sends:
            cp.start()
        for j, (px, py, pc) in enumerate(peers):
            copy(j, 4 * px + 2 * py + pc).wait_recv()
        for cp in sends:
            cp.wait_send()
        mine.wait()

    return pl.pallas_call(
        body, name=name, out_shape=jax.ShapeDtypeStruct((8,) + s.shape, s.dtype),
        in_specs=_any_specs(1), out_specs=pl.BlockSpec(memory_space=pl.ANY),
        scratch_shapes=[pltpu.SemaphoreType.DMA((7,)), pltpu.SemaphoreType.DMA((7,)), pltpu.SemaphoreType.DMA(())],
    )(s)


def sum_slots(name, parts):
    n = len(parts)
    R, W = parts[0][0].shape[1:]
    tm = _tile_rows(R, W)
    idx = jnp.stack([jnp.asarray(s, jnp.int32) for _, s in parts])

    def body(idx_ref, *refs):
        acc = refs[0][...].astype(F32)
        for r in refs[1:n]:
            acc = acc + r[...].astype(F32)
        refs[n][...] = acc

    grid_spec = pltpu.PrefetchScalarGridSpec(
        num_scalar_prefetch=1, grid=(R // tm,),
        in_specs=[pl.BlockSpec((None, tm, W), lambda i, idx, k=k: (idx[k], i, 0)) for k in range(n)],
        out_specs=pl.BlockSpec((tm, W), lambda i, idx: (i, 0)))
    return pl.pallas_call(body, name=name, grid_spec=grid_spec, out_shape=jax.ShapeDtypeStruct((R, W), F32),
                          compiler_params=pltpu.CompilerParams(dimension_semantics=("parallel",)))(idx, *[a for a, _ in parts])


def chip_sums(name, g, recv, c):
    _, R, W = g.shape
    rh = R // 2
    tm = _tile_rows(rh, W)
    g8 = g.reshape(2 * N_CHIPS, rh, W)
    idx = jnp.asarray(c, jnp.int32).reshape(1)

    def body(idx_ref, g_ref, r_ref, o_ref):
        o_ref[...] = (g_ref[...] + r_ref[...]).astype(o_ref.dtype)

    grid_spec = pltpu.PrefetchScalarGridSpec(
        num_scalar_prefetch=1, grid=(N_CHIPS, rh // tm),
        in_specs=[pl.BlockSpec((None, tm, W), lambda q, i, idx: (2 * q + idx[0], i, 0)),
                  pl.BlockSpec((None, tm, W), lambda q, i, idx: (q, i, 0))],
        out_specs=pl.BlockSpec((None, tm, W), lambda q, i, idx: (q, i, 0)))
    return pl.pallas_call(body, name=name, grid_spec=grid_spec, out_shape=jax.ShapeDtypeStruct((N_CHIPS, rh, W), BF16),
                          compiler_params=pltpu.CompilerParams(dimension_semantics=("parallel", "parallel")))(idx, g8, recv)


def _adamw_update(w, g, m, v):
    m = ADAM_B1 * m + (1.0 - ADAM_B1) * g
    v = ADAM_B2 * v + (1.0 - ADAM_B2) * (g * g)
    m_hat = m / (1.0 - ADAM_B1 ** ADAM_STEP)
    v_hat = v / (1.0 - ADAM_B2 ** ADAM_STEP)
    return -ADAM_LR * (m_hat / (jnp.sqrt(v_hat) + ADAM_EPS) + ADAM_WD * w), m, v


def adamw(name, w, g, m, v):
    R, W = w.shape
    return rowwise(name, lambda i, w, g, m, v: _adamw_update(w, g, m, v), [(a, W, 0) for a in (w, g, m, v)], [],
                   [("new", W, F32)] * 3, n_rows=R, tm=_tile_rows(R, W, 2 ** 20))


def adamw_halves(name, w, g_mine, g_other, m, v, c):
    R, W = w.shape
    rh = R // 2
    tm = _tile_rows(rh, W, 2 ** 20)
    nh = rh // tm
    idx = jnp.asarray(c, jnp.int32).reshape(1)

    def body(idx_ref, w_ref, ga_ref, gb_ref, m_ref, v_ref, g_ref, d_ref, mo_ref, vo_ref):
        mine = (pl.program_id(0) // nh) == idx_ref[0]
        g = jnp.where(mine, ga_ref[...], gb_ref[...])
        d, m, v = _adamw_update(w_ref[...], g, m_ref[...], v_ref[...])
        g_ref[...] = g
        d_ref[...] = d
        mo_ref[...] = m
        vo_ref[...] = v

    full = pl.BlockSpec((tm, W), lambda i, idx: (i, 0))
    half = pl.BlockSpec((tm, W), lambda i, idx: (i % nh, 0))
    grid_spec = pltpu.PrefetchScalarGridSpec(num_scalar_prefetch=1, grid=(R // tm,), in_specs=[full, half, half, full, full],
                                             out_specs=[full] * 4)
    return pl.pallas_call(body, name=name, grid_spec=grid_spec, out_shape=[jax.ShapeDtypeStruct((R, W), F32)] * 4,
                          compiler_params=pltpu.CompilerParams(dimension_semantics=("parallel",)))(idx, w, g_mine, g_other, m, v)


W_IN_BLOCK = D_PROJ // N_CHIPS
BA_REF = OFF_RQ


def _w_in_padded_from_blocks(g):
    lo, hi = BA_REF - W_IN_BLOCK, BA_REF + 16 - W_IN_BLOCK
    return jnp.concatenate([g[0], g[1][:, :lo], g[1][:, hi:], g[2], g[3], g[1][:, lo:hi],
                            jnp.zeros((D, PW - D_PROJ), g.dtype)], axis=1)


def _w_in_grad_blocks(g):
    s = W_IN_BLOCK
    second = jnp.concatenate([g[:, s:BA_REF], g[:, OFF_BA:OFF_BA + 16], g[:, BA_REF:2 * s - 16]], axis=1)
    return jnp.stack([g[:, :s], second, g[:, 2 * s - 16:3 * s - 16], g[:, 3 * s - 16:4 * s - 16]])


def _ret_consts():
    f = F32
    log_gamma = jnp.log1p(-jnp.exp2(-5.0 - jnp.arange(H, dtype=f)))
    pos = jnp.arange(CH, dtype=f)
    causal = jnp.tril(jnp.ones((CH, CH), dtype=bool))
    diff = pos[:, None] - pos[None, :]
    decay = jnp.where(causal, jnp.exp(jnp.where(causal, diff, 0.0) * log_gamma[:, None, None]), 0.0)
    xi = jnp.broadcast_to(jnp.exp((pos + 1.0) * log_gamma[:, None])[:, :, None], (H, CH, DH))
    zeta = jnp.broadcast_to(jnp.exp((CH - 1.0 - pos) * log_gamma[:, None])[:, :, None], (H, CH, DH))
    cd = jnp.broadcast_to(jnp.exp(CH * log_gamma)[:, None, None], (H, 1, DH))
    return decay, xi, zeta, cd


def _rope_tables(lp):
    pos = jnp.arange(lp, dtype=F32) - float(PAD)
    inv = 1.0 / (ROPE_BASE ** jnp.linspace(0.0, 1.0, DH // 2, dtype=F32))
    ang = pos[:, None] * inv[None, :]
    cos, sin = jnp.cos(ang), jnp.sin(ang)
    ct = jnp.repeat(cos, 2, axis=1)
    st = jnp.stack([-sin, sin], axis=-1).reshape(lp, DH)
    return ct, st


def local_step(x, tgt, w, small, hooks=None):
    hooks = hooks or {}
    seq = x.shape[0]
    lp = HEAD_ROWS + seq
    nc = lp // CH
    tm = _pick(lp, 192, CH)
    row = functools.partial(rowwise, n_rows=lp, tm=tm)
    gw = {}

    def mm(name, *args, **kw):
        if name not in hooks:
            return matmul(name, *args, **kw)
        make_exchange, take = hooks[name]
        out, results = matmul(name, *args, side=make_exchange(gw), **kw)
        take(results)
        return out

    h0 = jnp.concatenate([jnp.zeros((PAD, D), F32), w["meta_tokens"], x], axis=0)
    tgt_p = jnp.concatenate([jnp.zeros((HEAD_ROWS, D), F32), tgt], axis=0)

    def ffn_fwd(tag, h, wn, n=None, next_norm=None, loss=None):
        if n is None:
            n = row(f"{tag}_norm", lambda i, h, wn: _rms(h, wn), [(h, D, 0)], [wn], [("new", D, BF16)])[0]
        g, u, mid = mm(f"{tag}_up", n, w[f"{tag}_w_in"], "nn", b_split=True, pair=True, outs=(F32, F32, BF16),
                       epilogue=lambda pr: (pr[0], pr[1], _silu(pr[0]) * pr[1]))
        down = functools.partial(mm, f"{tag}_down", mid, w[f"{tag}_w_out"], "nn", ti_cap=DOWN_ROWS, tj_cap=1024, tr_cap=2816,
                                 rows=[h])
        if loss is not None:
            target, final_w = loss
            ti = _pick(lp, DOWN_ROWS, 16)

            def head(pr, h, t, nw):
                mask = (_row_ids(pl.program_id(0), ti) >= HEAD_ROWS).astype(F32)
                y, vjp = jax.vjp(_rms, h + 0.5 * pr[0], nw)
                err = (y - t) * mask
                dh, dw = vjp(err * (1.0 / D))
                return dh, dw, jnp.sum(err * err, keepdims=True).reshape(1, 1) * (0.5 / D) * jnp.ones((1, LANES), F32)
            return down(rows=[h, target], pars=[final_w], accs=[(1, D), (1, LANES)], epilogue=head), (h, n, g, u, mid), None
        if next_norm is None:
            return down(epilogue=lambda pr, h: (h + 0.5 * pr[0],)), (h, n, g, u, mid), None
        out, n_next = down(pars=[next_norm], outs=(F32, BF16), epilogue=lambda pr, h, nw: _with_norm(h + 0.5 * pr[0], nw))
        return out, (h, n, g, u, mid), n_next

    def _with_norm(out, nw):
        return out, _rms(out, nw)

    def norm_bwd_epilogue(pr, h, dres, wn):
        _, vjp = jax.vjp(_rms, h, wn)
        dh, dw = vjp(pr[0])
        return dres + dh, dw

    def ffn_bwd(tag, dh, saved, wn):
        h, n, g, u, mid = saved
        w_in_il, w_out = w[f"{tag}_w_in_il"], w[f"{tag}_w_out"]
        dw_out = mm(f"{tag}_dwout", mid, dh, "tn", ti_cap=1408, tj_cap=1024, tr_cap=1376, epilogue=lambda pr: (0.5 * pr[0],))
        gw[f"{tag}_w_out"] = dw_out

        def act_bwd(pr, g, u):
            _, vjp = jax.vjp(lambda g, u: _silu(g) * u, g, u)
            return (jnp.concatenate(vjp(0.5 * pr[0]), axis=1),)
        dgu = mm(f"{tag}_dmid", dh, w_out, "nt", tj_cap=2 * DFF // N_CHIPS, rows=[g, u], outs=(BF16,), wide=2, epilogue=act_bwd)
        dh_in, dwn = mm(f"{tag}_dn", dgu, w_in_il, "nt", tj_cap=1024, tr_cap=DFF, rows=[h, dh], pars=[wn],
                        accs=[(1, D)], epilogue=norm_bwd_epilogue)
        dw_in = mm(f"{tag}_dwin", n, dgu, "tn", ti_cap=1024, tr_cap=2752, o_split=True, o_interleaved=True)
        return dh_in, dw_in, dw_out, dwn

    h1, ffn1_saved, n2 = ffn_fwd("ffn1", h0, small["ffn1_norm"], next_norm=small["mix_norm"])
    p = mm("mix_proj", n2, w["w_in_p"], "nn", ti_cap=1376, tj_cap=1152)

    cws = [w["gdn_conv_w"][j:j + 1] for j in range(CONV_K)]
    alog_row = jnp.zeros((1, LANES), F32).at[:, H:2 * H].set(small["gdn_a_log"])
    dtb_row = jnp.zeros((1, LANES), F32).at[:, H:2 * H].set(small["gdn_dt_bias"])
    q, k, v, gcb, bb, gc = gdn_pre_fwd(p, cws, alog_row, dtb_row, lp, tm)
    gc_rows = gc[:, H:2 * H].reshape(nc, CH, H).transpose(0, 2, 1)
    gdn_blocks = [(q, 0), (k, 0), (v, 0), (gcb, 0), (bb, 0), (p, OFF_Z // D)]
    y_a, gdn_states, gdn_inv = _scan_fwd("gdn_scan_fwd", _gdn_chunk, gdn_blocks, [gc_rows], [], nc,
                                         params=[small["gdn_out_norm"]], out_dtype=BF16, keep=(CH, CH))

    ct, st = _rope_tables(lp)
    ret_consts = list(_ret_consts())
    ret_blocks = [(p, OFF_RQ // D), (p, OFF_RK // D), (p, OFF_RV // D), (p, OFF_RG // D)]
    ret_w = small["ret_out_norm"].reshape(H, 1, DH)
    y_b, ret_states = _scan_fwd("ret_scan_fwd", _ret_chunk, ret_blocks, [], ret_consts, nc, shared=[ct, st],
                                params=[ret_w], out_dtype=BF16)

    br_a = matmul("branch_gdn", y_a, w["w_branch_gdn"], "nn", ti_cap=1376, tj_cap=1024)
    br_b = matmul("branch_ret", y_b, w["w_branch_ret"], "nn", ti_cap=1376, tj_cap=1024)
    merged = row("merge", lambda i, a, b, ga, gb_: _merge(a, b, ga, gb_),
                 [(br_a, D, 0), (br_b, D, 0), (p, D, OFF_GA // D), (p, D, OFF_GB // D)], [], [("new", D, BF16)])[0]
    h2, n3 = matmul("mix_out", merged, w["w_out"], "nn", ti_cap=1376, tj_cap=1024, rows=[h1], pars=[small["ffn2_norm"]],
                    outs=(F32, BF16), epilogue=lambda pr, h, nw: _with_norm(h + pr[0], nw))
    (dh3, d_final, loss_row), ffn2_saved, _ = ffn_fwd("ffn2", h2, small["ffn2_norm"], n=n3, loss=(tgt_p, small["final_norm"]))

    gs = {"final_norm": d_final}
    dh2, gw["ffn2_w_in"], gw["ffn2_w_out"], gs["ffn2_norm"] = ffn_bwd("ffn2", dh3, ffn2_saved, small["ffn2_norm"])
    dmerged = matmul("mix_out_dx", dh2, w["w_out"], "nt", ti_cap=1376, tj_cap=1024)
    gw["w_out"] = matmul("mix_out_dw", merged, dh2, "tn", ti_cap=1024, tj_cap=1024, tr_cap=2752)
    dp = lax.empty((lp, PW), BF16)

    def merge_bwd(i, dm, a, b, ga, gb_):
        _, vjp = jax.vjp(_merge, a, b, ga, gb_)
        da, db, dga, dgb = vjp(dm)
        return da, db, jnp.concatenate([dga, dgb], axis=1)
    da, db_, dp = row("merge_bwd", merge_bwd,
                      [(dmerged, D, 0), (br_a, D, 0), (br_b, D, 0), (p, D, OFF_GA // D), (p, D, OFF_GB // D)], [],
                      [("new", D, BF16), ("new", D, BF16), ("into", dp, 2 * D, OFF_GA // (2 * D))])
    dy_a = matmul("branch_gdn_dx", da, w["w_branch_gdn"], "nt", ti_cap=1376, tj_cap=1024)
    gw["w_branch_gdn"] = matmul("branch_gdn_dw", y_a, da, "tn", ti_cap=1024, tj_cap=1024, tr_cap=2752)
    dy_b = matmul("branch_ret_dx", db_, w["w_branch_ret"], "nt", ti_cap=1376, tj_cap=1024)
    gw["w_branch_ret"] = matmul("branch_ret_dw", y_b, db_, "tn", ti_cap=1024, tj_cap=1024, tr_cap=2752)

    dp, d_ret_w = _scan_bwd("ret_scan_bwd", _ret_chunk, ret_blocks, [], ret_consts, ret_states, dy_b,
                            (dp, OFF_RQ // (4 * D), [0, 1, 2, 3]), nc, shared=[ct, st], params=[ret_w])
    gs["ret_out_norm"] = d_ret_w.reshape(1, D)
    dq, dk, dv, dgcb, dbb, dp, dgc_rows, gs["gdn_out_norm"] = _scan_bwd(
        "gdn_scan_bwd", _gdn_chunk, gdn_blocks, [gc_rows], [], gdn_states, dy_a, (dp, OFF_Z // D, [5]), nc,
        params=[small["gdn_out_norm"]], kept=gdn_inv)
    dgc = jnp.pad(dgc_rows.transpose(0, 2, 1).reshape(lp, H), ((0, 0), (H, LANES - 2 * H)))
    dp, dba, dcw, dgate = gdn_pre_bwd(p, cws, alog_row, dtb_row, dq, dk, dv, dgcb, dbb, dgc, dp, lp, tm)
    dp = row("dp_ba", lambda i, t: t, [(dba, LANES, 0)], [], [("into", dp, LANES, OFF_BA // LANES)])[0]
    gw["gdn_conv_w"] = dcw[:CONV_K]
    gs["gdn_a_log"] = dgate[0:1, H:2 * H]
    gs["gdn_dt_bias"] = dgate[1:2, H:2 * H]

    dh1, gs["mix_norm"] = matmul("mix_proj_dx", dp, w["w_in_p"], "nt", tj_cap=1024, tr_cap=3456, rows=[h1, dh2],
                                 pars=[small["mix_norm"]], accs=[(1, D)], epilogue=norm_bwd_epilogue)
    gw["w_in_p"] = matmul("mix_proj_dw", n2, dp, "tn", ti_cap=1024, tj_cap=1152, tr_cap=2752)
    dh0, gw["ffn1_w_in"], gw["ffn1_w_out"], gs["ffn1_norm"] = ffn_bwd("ffn1", dh1, ffn1_saved, small["ffn1_norm"])
    gw["meta_tokens"] = dh0[PAD:HEAD_ROWS]
    return loss_row, dh0[HEAD_ROWS:], gw, gs


BIG = ("ffn1_w_in", "ffn1_w_out", "w_in", "w_branch_gdn", "w_branch_ret", "w_out", "ffn2_w_in", "ffn2_w_out")
COL_SHARDED = ("ffn1_w_in", "w_in", "ffn2_w_in")
WITH_FFN1_UP = ("ffn1_w_out", "w_in")
WITH_MIX_PROJ = ("w_branch_gdn", "w_branch_ret", "w_out", "ffn2_w_in", "ffn2_w_out")
EARLY_GRADS = ("ffn2_w_in", "ffn2_w_out", "w_in", "w_branch_gdn", "w_branch_ret", "w_out", "ffn1_w_out")
LATE_GRADS = ("ffn1_w_in",)
SMALL = ("ffn1_norm", "mix_norm", "ret_out_norm", "ffn2_norm", "final_norm", "gdn_out_norm", "gdn_a_log", "gdn_dt_bias")
WEIGHTS = ("meta_tokens", "ffn1_norm", "ffn1_w_in", "ffn1_w_out", "mix_norm", "w_in", "gdn_conv_w", "gdn_a_log", "gdn_dt_bias",
           "gdn_out_norm", "ret_out_norm", "w_branch_gdn", "w_branch_ret", "w_out", "ffn2_norm", "ffn2_w_in", "ffn2_w_out",
           "final_norm")
LOSS_ROW = 6
CONV_ROW0, META_ROW0, SMALL_ROWS = 8, 24, 40


def pack_small(vals):
    rows = [vals[n].reshape(1, D) for n in SMALL[:5]]
    r5 = jnp.concatenate([vals["gdn_out_norm"].reshape(1, DH), vals["gdn_a_log"].reshape(1, H), vals["gdn_dt_bias"].reshape(1, H),
                          jnp.zeros((1, D - DH - 2 * H), F32)], axis=1)
    return jnp.concatenate(rows + [r5, jnp.zeros((2, D), F32)], axis=0)


def unpack_small(packed, shapes):
    out = {n: packed[j].reshape(shapes[n]) for j, n in enumerate(SMALL[:5])}
    out["gdn_out_norm"] = packed[5, :DH].reshape(shapes["gdn_out_norm"])
    out["gdn_a_log"] = packed[5, DH:DH + H].reshape(shapes["gdn_a_log"])
    out["gdn_dt_bias"] = packed[5, DH + H:DH + 2 * H].reshape(shapes["gdn_dt_bias"])
    return out


def kernel(x, meta_tokens, ffn1_norm, ffn1_w_in, ffn1_w_out, mix_norm, w_in, gdn_conv_w, gdn_a_log, gdn_dt_bias, gdn_out_norm, ret_out_norm, w_branch_gdn, w_branch_ret, w_out, ffn2_norm, ffn2_w_in, ffn2_w_out, final_norm, loss_target, m_meta_tokens, m_ffn1_norm, m_ffn1_w_in, m_ffn1_w_out, m_mix_norm, m_w_in, m_gdn_conv_w, m_gdn_a_log, m_gdn_dt_bias, m_gdn_out_norm, m_ret_out_norm, m_w_branch_gdn, m_w_branch_ret, m_w_out, m_ffn2_norm, m_ffn2_w_in, m_ffn2_w_out, m_final_norm, v_meta_tokens, v_ffn1_norm, v_ffn1_w_in, v_ffn1_w_out, v_mix_norm, v_w_in, v_gdn_conv_w, v_gdn_a_log, v_gdn_dt_bias, v_gdn_out_norm, v_ret_out_norm, v_w_branch_gdn, v_w_branch_ret, v_w_out, v_ffn2_norm, v_ffn2_w_in, v_ffn2_w_out, v_final_norm):
    a = dict(locals())
    wts = {n: a[n] for n in WEIGHTS}
    mom_m = {n: a["m_" + n] for n in WEIGHTS}
    mom_v = {n: a["v_" + n] for n in WEIGHTS}
    shapes = {n: wts[n].shape for n in WEIGHTS}
    flat = lambda t: t.reshape(t.shape[-2:])
    c = lax.axis_index("c")
    chip = 2 * lax.axis_index("x") + lax.axis_index("y")

    exact = jnp.zeros((16, D), F32).at[0:3].set(wts["gdn_conv_w"].reshape(3, D)).at[3:7].set(wts["meta_tokens"].reshape(4, D))
    bf16_block = lambda n: flat(wts[n]).astype(BF16)
    w = {}

    def take_weights(names):
        def take(gathered):
            for n, g in zip(names, gathered):
                if n == "w_in":
                    w["w_in_p"] = _w_in_padded_from_blocks(g)
                elif n in COL_SHARDED:
                    w[n], w[n + "_il"] = g, jnp.concatenate([g[q] for q in INTERLEAVED], axis=1)
                else:
                    w[n] = g.reshape(N_CHIPS * g.shape[1], D)
        return take

    first = run_exchange("gather_first", gather_exchange([bf16_block("ffn1_w_in"), exact]))
    take_weights(["ffn1_w_in"])(first[:1])
    exact = first[1]
    w["gdn_conv_w"] = jnp.concatenate([exact[q, 0:3].reshape(CONV_K, 3 * D // 4) for q in range(N_CHIPS)], axis=1)
    w["meta_tokens"] = jnp.concatenate([exact[q, 3:7].reshape(N_META, D // 4) for q in range(N_CHIPS)], axis=1)
    small = {n: wts[n].reshape(1, -1) for n in SMALL}
    hooks = {"ffn1_up": (lambda gw: gather_exchange([bf16_block(n) for n in WITH_FFN1_UP]), take_weights(WITH_FFN1_UP)),
             "mix_proj": (lambda gw: gather_exchange([bf16_block(n) for n in WITH_MIX_PROJ]), take_weights(WITH_MIX_PROJ))}

    def blocks(gw, n):
        if n == "w_in":
            return _w_in_grad_blocks(gw["w_in_p"])
        return gw[n] if n in COL_SHARDED else gw[n].reshape(N_CHIPS, gw[n].shape[0] // N_CHIPS, D)

    chip_sum, from_chips = {}, {}

    def swap_early(gw):
        chip_sum["g"] = [blocks(gw, n) for n in EARLY_GRADS]
        return sibling_halves_exchange(chip_sum["g"])

    def sum_early(from_sib):
        for n, g, r in zip(EARLY_GRADS, chip_sum.pop("g"), from_sib):
            chip_sum[n] = chip_sums(f"grads_chip_sum_{n}", g, r, c)

    def scatter_of(names):
        return (lambda gw: scatter_chips_exchange([chip_sum[n] for n in names])), (lambda got: from_chips.update(zip(names, got)))
    hooks["ffn1_dmid"] = (swap_early, sum_early)
    hooks["ffn1_dn"] = scatter_of(["w_in"])
    hooks["ffn1_dwin"] = scatter_of([n for n in EARLY_GRADS if n != "w_in"])

    loss_row, gx, gw, gs = local_step(x[0], loss_target[0], w, small, hooks)

    late = [blocks(gw, n) for n in LATE_GRADS]
    from_sib = run_exchange("grads_sibling_late", sibling_halves_exchange(late))
    for n, g, r in zip(LATE_GRADS, late, from_sib):
        chip_sum[n] = chip_sums(f"grads_chip_sum_{n}", g, r, c)
    from_chips.update(zip(LATE_GRADS, run_exchange("grads_scatter_late", scatter_chips_exchange([chip_sum[n] for n in LATE_GRADS]))))
    halves = [sum_slots(f"grads_sum_{n}", [(chip_sum[n], chip), (from_chips[n], 0), (from_chips[n], 1), (from_chips[n], 2)])
              for n in BIG]
    others = sibling_swap("grads_swap", halves)
    grads, delta, new_m, new_v = {}, {}, {}, {}
    for n, mine, other in zip(BIG, halves, others):
        res = adamw_halves(f"adamw_{n}", flat(wts[n]), mine, other, flat(mom_m[n]), flat(mom_v[n]), c)
        grads[n], delta[n], new_m[n], new_v[n] = (t.reshape(shapes[n]) for t in res)

    sm = jnp.concatenate([pack_small(gs).at[LOSS_ROW, :LANES].set(loss_row[0]),
                          gw["gdn_conv_w"].reshape(3 * CONV_K, D), jnp.zeros((META_ROW0 - CONV_ROW0 - 3 * CONV_K, D), F32),
                          gw["meta_tokens"]], axis=0)
    every = allgather_all("small_gather", sm)
    sm_sum = sum_slots("small_sum", [(every, s) for s in range(8)])
    d_s, m_s, v_s = adamw("adamw_small", pack_small(small), sm_sum[:8], pack_small({n: mom_m[n].reshape(1, -1) for n in SMALL}),
                          pack_small({n: mom_v[n].reshape(1, -1) for n in SMALL}))
    grads.update(unpack_small(sm_sum, shapes))
    delta.update(unpack_small(d_s, shapes))
    new_m.update(unpack_small(m_s, shapes))
    new_v.update(unpack_small(v_s, shapes))
    g_conv = lax.dynamic_slice_in_dim(sm_sum[CONV_ROW0:CONV_ROW0 + 3 * CONV_K].reshape(CONV_K, 3 * D), chip * (3 * D // 4), 3 * D // 4, 1)
    g_meta = lax.dynamic_slice_in_dim(sm_sum[META_ROW0:META_ROW0 + N_META], chip * (D // 4), D // 4, 1)
    for n, g in (("gdn_conv_w", g_conv), ("meta_tokens", g_meta)):
        d_, m_, v_ = adamw(f"adamw_{n}", flat(wts[n]), g, flat(mom_m[n]), flat(mom_v[n]))
        grads[n], delta[n], new_m[n], new_v[n] = (t.reshape(shapes[n]) for t in (g, d_, m_, v_))
    loss = sm_sum[LOSS_ROW, 0]

    return (loss, gx[None], *[grads[n] for n in WEIGHTS], *[delta[n] for n in WEIGHTS], *[new_m[n] for n in WEIGHTS],
            *[new_v[n] for n in WEIGHTS])
```

```python
import collections
import functools

import jax
import jax.numpy as jnp
from jax import lax
from jax.experimental import pallas as pl
from jax.experimental.pallas import tpu as pltpu

F32 = jnp.float32
BF16 = jnp.bfloat16
HI = lax.Precision.HIGHEST
MESH = pl.DeviceIdType.MESH

D = 1024
N_META = 16
PAD = 48
HEAD_ROWS = PAD + N_META
CH = 64
H = 8
DH = 128
DFF = 2816
CONV_K = 4
EPS = 1e-6
ROPE_BASE = 10000.0
LANES = 128
N_CHIPS = 4
VMEM_LIMIT = 56 * 2 ** 20
DOWN_ROWS = 688
INTERLEAVED = (0, 2, 1, 3)

OFF_QKV, OFF_Z, OFF_RQ, OFF_RK, OFF_RV, OFF_RG, OFF_GA, OFF_GB, OFF_BA = 0, 3072, 4096, 5120, 6144, 7168, 8192, 9216, 10240
PW = 10368
D_PROJ = 10256

ADAM_LR, ADAM_B1, ADAM_B2, ADAM_EPS, ADAM_WD, ADAM_STEP = 0.001, 0.9, 0.999, 1e-08, 0.01, 10


def _pick(n, cap, mult):
    best = None
    for t in range(mult, min(n, cap) + 1, mult):
        if n % t == 0:
            best = t
    return best if best is not None else n


def _tile_rows(rows, cols, block_bytes=3 * 2 ** 19):
    return _pick(rows, max(8, block_bytes // (4 * cols)), 8)


def _dot(a, b, dims, prec=None):
    return lax.dot_general(a, b, (dims, ((), ())), precision=prec, preferred_element_type=F32)


def _mmh(a, b):
    return _dot(a, b, ((1,), (0,)), HI)


def _split(a):
    hi = a.astype(BF16)
    return hi, (a - hi.astype(F32)).astype(BF16)


_NN, _NT, _TN = ((2,), (1,)), ((2,), (2,)), ((1,), (1,))


def _bdot(a, b, dims):
    return lax.dot_general(a, b, (dims, ((0,), (0,))), preferred_element_type=F32)


def _bmm(a, b, dims):
    return _bdot(a.astype(BF16), b.astype(BF16), dims)


def _bdot3(a, b, dims):
    ah, al = _split(a)
    bh, bl = _split(b)
    if dims == _NN:
        m = a.shape[1]
        both = _bdot(jnp.concatenate([ah, al], axis=1), bh, dims)
        return both[:, :m] + (both[:, m:] + _bdot(ah, bl, dims))
    return _bdot(ah, bh, dims) + (_bdot(ah, bl, dims) + _bdot(al, bh, dims))


def matmul(name, a, b, mode, *, ti_cap=688, tj_cap=512, tr_cap=1408, b_split=False, o_split=False, pair=False,
           rows=(), pars=(), epilogue=None, outs=(F32,), accs=(), side=None, wide=1, o_interleaved=False):
    if mode == "nn":
        I, R = a.shape
        J = N_CHIPS * b.shape[2] if b_split else b.shape[1]
    elif mode == "nt":
        (I, R), J = a.shape, b.shape[0]
    else:
        (R, I), J = a.shape, b.shape[1]
    ti = _pick(I, ti_cap, 16) if mode != "tn" else _pick(I, ti_cap, LANES)
    tj = _pick(J, tj_cap, LANES)
    tr = _pick(R, tr_cap, LANES) if mode != "tn" else _pick(R, tr_cap, 16)
    half = N_CHIPS // 2
    if mode == "nn":
        a_spec = pl.BlockSpec((ti, tr), lambda i, j, r: (i, r))
        if b_split:
            tj = J // N_CHIPS
            b_spec = pl.BlockSpec((None, tr, tj), lambda i, j, r: (j, r, 0))
            b2_spec = pl.BlockSpec((None, tr, tj), lambda i, j, r: (j + half, r, 0))
        else:
            b_spec = pl.BlockSpec((tr, tj), lambda i, j, r: (r, j))
        dims = ((1,), (0,))
    elif mode == "nt":
        a_spec = pl.BlockSpec((ti, tr), lambda i, j, r: (i, r))
        b_spec = pl.BlockSpec((tj, tr), lambda i, j, r: (j, r))
        dims = ((1,), (1,))
    else:
        if o_split:
            tj = J // N_CHIPS
        a_spec = pl.BlockSpec((tr, ti), lambda i, j, r: (r, i))
        b_spec = pl.BlockSpec((tr, tj), lambda i, j, r: (r, j))
        dims = ((0,), (0,))
    assert not pair or (mode == "nn" and b_split), name
    j_out = J // 2 if pair else J
    nr = R // tr
    assert I % ti == 0 and j_out % tj == 0 and R % tr == 0, (name, I, J, R, ti, tj, tr)
    grid = (I // ti, j_out // tj, nr)
    w_spec = o_spec = pl.BlockSpec((ti, tj), lambda i, j, r: (i, j))
    if o_split:
        assert epilogue is None, name
        blk = (lambda j: (j % 2) * 2 + j // 2) if o_interleaved else (lambda j: j)
        w_spec = pl.BlockSpec((None, ti, tj), lambda i, j, r: (blk(j), i, 0))
        out_shapes = [jax.ShapeDtypeStruct((N_CHIPS, I, tj), outs[0])]
    else:
        w_spec = pl.BlockSpec((ti, wide * tj), lambda i, j, r: (i, j))
        out_shapes = [jax.ShapeDtypeStruct((I, wide * j_out), dt) for dt in outs]
    n_prod = 2 if pair else 1
    n_out, n_acc, n_rows, n_pars = len(out_shapes), len(accs), len(rows), len(pars)
    n_si = len(side.ins) if side else 0
    n_so = len(side.out_shapes) if side else 0
    n_in = 1 + n_prod + n_rows + n_pars + n_si
    n_steps = grid[0] * grid[1] * grid[2]

    def body(*refs):
        a_ref, b_refs = refs[0], refs[1:1 + n_prod]
        row_refs = refs[1 + n_prod:1 + n_prod + n_rows]
        par_refs = refs[1 + n_prod + n_rows:1 + n_prod + n_rows + n_pars]
        out_refs = refs[n_in:n_in + n_out]
        acc_refs = refs[n_in + n_out:n_in + n_out + n_acc]
        scr = refs[n_in + n_out + n_acc + n_so:]
        partial = scr[:n_prod] if nr > 1 else ()
        step = (pl.program_id(0) * grid[1] + pl.program_id(1)) * grid[2] + pl.program_id(2)
        if side:
            s_refs = (refs[n_in - n_si:n_in], refs[n_in + n_out + n_acc:n_in + n_out + n_acc + n_so], scr[len(partial):])

        if side or n_acc:
            @pl.when(step == 0)
            def _():
                if side:
                    side.start(*s_refs)
                for r in acc_refs:
                    r[...] = jnp.zeros_like(r)

        lhs = a_ref[...].astype(BF16)
        prods = [_dot(lhs, r[...].astype(BF16), dims) for r in b_refs]

        def finish(prods):
            res = epilogue(prods, *[r[...] for r in row_refs], *[r[...] for r in par_refs]) if epilogue else prods
            for r, v in zip(out_refs, res[:n_out]):
                r[...] = v.astype(r.dtype)
            for r, v in zip(acc_refs, res[n_out:]):
                r[...] += v

        if nr == 1:
            finish(prods)
        else:
            k = pl.program_id(2)

            @pl.when(k == 0)
            def _():
                for r, v in zip(partial, prods):
                    r[...] = v

            @pl.when(k > 0)
            def _():
                for r, v in zip(partial, prods):
                    r[...] += v

            @pl.when(k == nr - 1)
            def _():
                finish([r[...] for r in partial])

        if side:
            @pl.when(step == n_steps - 1)
            def _():
                side.finish(*s_refs)

    ins = [a, b] + ([b] if pair else []) + list(rows) + list(pars)
    in_specs = [a_spec, b_spec] + ([b2_spec] if pair else []) + [o_spec] * n_rows
    in_specs += [pl.BlockSpec(p.shape, lambda i, j, r, nd=p.ndim: (0,) * nd) for p in pars]
    out_specs = [w_spec] * n_out + [pl.BlockSpec(s, lambda i, j, r: (0, 0)) for s in accs]
    out_shapes += [jax.ShapeDtypeStruct(s, F32) for s in accs]
    scratch = [pltpu.VMEM((ti, tj), F32)] * n_prod if nr > 1 else []
    ordered = bool(side) or n_acc > 0
    res = pl.pallas_call(
        body, name=name, grid=grid, in_specs=in_specs + _any_specs(n_si), out_specs=out_specs + _any_specs(n_so),
        out_shape=out_shapes + (list(side.out_shapes) if side else []), scratch_shapes=scratch + (list(side.scratch) if side else []),
        compiler_params=pltpu.CompilerParams(
            dimension_semantics=("arbitrary",) * 3 if ordered else ("parallel", "parallel", "arbitrary"), vmem_limit_bytes=VMEM_LIMIT),
    )(*ins, *(side.ins if side else []))
    own = res[:n_out + n_acc]
    own = own[0] if len(own) == 1 else tuple(own)
    return (own, list(res[n_out + n_acc:])) if side else own


def rowwise(name, fn, rows, pars, outs, accs=(), *, n_rows, tm):
    nt = n_rows // tm
    assert nt * tm == n_rows
    in_specs, ins = [], []
    for arr, w, cb in rows:
        in_specs.append(pl.BlockSpec((tm, w), lambda i, cb=cb: (i, cb)))
        ins.append(arr)
    for p in pars:
        in_specs.append(pl.BlockSpec(p.shape, lambda i, nd=p.ndim: (0,) * nd))
        ins.append(p)
    n_in = len(ins)
    out_specs, out_shapes, aliases = [], [], {}
    for k, o in enumerate(outs):
        if o[0] == "new":
            _, w, dt = o
            out_specs.append(pl.BlockSpec((tm, w), lambda i: (i, 0)))
            out_shapes.append(jax.ShapeDtypeStruct((n_rows, w), dt))
        else:
            _, arr, w, cb = o
            in_specs.append(pl.BlockSpec(memory_space=pl.ANY))
            aliases[len(ins)] = k
            ins.append(arr)
            out_specs.append(pl.BlockSpec((tm, w), lambda i, cb=cb: (i, cb)))
            out_shapes.append(jax.ShapeDtypeStruct(arr.shape, arr.dtype))
    for r, w in accs:
        out_specs.append(pl.BlockSpec((r, w), lambda i: (0, 0)))
        out_shapes.append(jax.ShapeDtypeStruct((r, w), F32))
    n_all_in, n_out, n_acc = len(ins), len(outs), len(accs)

    def body(*refs):
        i = pl.program_id(0)
        vals = [r[...] for r in refs[:n_in]]
        res = fn(i, *vals)
        if not isinstance(res, (tuple, list)):
            res = (res,)
        o_refs = refs[n_all_in:n_all_in + n_out]
        a_refs = refs[n_all_in + n_out:]
        for r, v in zip(o_refs, res[:n_out]):
            r[...] = v.astype(r.dtype)
        if n_acc:
            @pl.when(i == 0)
            def _():
                for r in a_refs:
                    r[...] = jnp.zeros_like(r)
            for r, v in zip(a_refs, res[n_out:]):
                r[...] += v

    return pl.pallas_call(
        body, name=name, grid=(nt,), in_specs=in_specs, out_specs=out_specs, out_shape=out_shapes,
        input_output_aliases=aliases,
        compiler_params=pltpu.CompilerParams(dimension_semantics=("arbitrary",), vmem_limit_bytes=VMEM_LIMIT),
    )(*ins)


def _row_ids(i, tm):
    return i * tm + lax.broadcasted_iota(jnp.int32, (tm, 1), 0)


def _sigmoid(x):
    return 1.0 / (1.0 + jnp.exp(-x))


def _silu(x):
    return x * _sigmoid(x)


def _softplus(x):
    return jnp.maximum(x, 0.0) + jnp.log(1.0 + jnp.exp(-jnp.abs(x)))


def _rms(x, w):
    return x * lax.rsqrt(jnp.mean(x * x, axis=-1, keepdims=True) + EPS) * w


def _heads(fn, *xs):
    return jnp.concatenate([fn(h, *[x[:, h * DH:(h + 1) * DH] for x in xs]) for h in range(H)], axis=1)


def _merge(a, b, ga, gb):
    return _sigmoid(ga) * a + _sigmoid(gb) * b


def _select_matrix(first_lane):
    r = lax.broadcasted_iota(jnp.int32, (LANES, H * DH), 0)
    c = lax.broadcasted_iota(jnp.int32, (LANES, H * DH), 1)
    return (r == first_lane + (c >> 7)).astype(F32)


def _chunk_tri(tm):
    r = lax.broadcasted_iota(jnp.int32, (tm, tm), 0)
    c = lax.broadcasted_iota(jnp.int32, (tm, tm), 1)
    return jnp.logical_and((r >> 6) == (c >> 6), r >= c).astype(F32)


def _gdn_gates(ba, alog_row, dtb_row, mask):
    g = -jnp.exp(alog_row) * _softplus(ba + dtb_row) * mask
    gc = _mmh(_chunk_tri(ba.shape[0]), g)
    beta = _sigmoid(ba) * mask
    return _mmh(gc, _select_matrix(H)), _mmh(beta, _select_matrix(0)), gc


def _gdn_qkv(c):
    a = _silu(c)

    def l2(scale):
        return lambda h, t: t * lax.rsqrt(jnp.sum(t * t, axis=-1, keepdims=True) + EPS) * scale
    q = _heads(l2(DH ** -0.5), a[:, :D])
    k = _heads(l2(1.0), a[:, D:2 * D])
    return q, k, a[:, 2 * D:]


def _rows_from(x, start, n):
    base = start - start % 8
    if base == start:
        return x[start:start + n]
    return pltpu.roll(x, x.shape[0] - start % 8, 0)[base:base + n]


def _conv_shifted(xw, tm):
    return [_rows_from(xw, 5 + j, tm) for j in range(CONV_K)]


def _conv_taps(xw, cws, tm):
    return sum(w * x for w, x in zip(cws, _conv_shifted(xw, tm)))


def _swap_pairs(t):
    n = t.shape[1]
    lane = lax.broadcasted_iota(jnp.int32, t.shape, 1)
    return jnp.where((lane & 1) == 0, pltpu.roll(t, n - 1, 1), pltpu.roll(t, 1, 1))


@jax.custom_vjp
def _unit_lower_inv(a):
    n = -a
    eye = (lax.broadcasted_iota(jnp.int32, a.shape, 1) == lax.broadcasted_iota(jnp.int32, a.shape, 2)).astype(F32)
    p = eye + n
    for _ in range(5):
        n = _bdot3(n, n, _NN)
        p = p + _bdot3(p, n, _NN)
    return p


def _inv_fwd(a):
    t = _unit_lower_inv(a)
    return t, t


def _inv_bwd(t, dt):
    x = _bdot3(t, dt, _TN)
    return (-_bdot3(x, t, _NT),)


_unit_lower_inv.defvjp(_inv_fwd, _inv_bwd)


@jax.custom_vjp
def _unit_lower_inv_known(a, t):
    return t


_unit_lower_inv_known.defvjp(lambda a, t: (t, t), lambda t, dt: (_inv_bwd(t, dt)[0], jnp.zeros_like(t)))


def _gdn_chunk(q, k, v, gc, bb, z, gr, s, norm_w, t_known=None):
    ri = lax.broadcasted_iota(jnp.int32, (H, CH, CH), 1)
    ci = lax.broadcasted_iota(jnp.int32, (H, CH, CH), 2)
    causal = ri >= ci
    gc1 = jnp.sum(gc, axis=2, keepdims=True) * (1.0 / LANES)
    diff = jnp.broadcast_to(gc1, (H, CH, CH)) - jnp.broadcast_to(gr, (H, CH, CH))
    decay = jnp.where(causal, jnp.exp(jnp.where(causal, diff, 0.0)), 0.0)
    kb = k * bb
    sc = _bmm(jnp.concatenate([q, kb], axis=1), k, _NT)
    qk = sc[:, :CH] * decay
    a = jnp.where(ri > ci, sc[:, CH:] * decay, 0.0)
    t = _unit_lower_inv(a) if t_known is None else _unit_lower_inv_known(a, t_known)
    eg = jnp.exp(gc)
    uw = _bmm(t, jnp.concatenate([v * bb, kb * eg], axis=2), _NN)
    g_last = gc[:, CH - 1:CH, :]
    ws = _bmm(jnp.concatenate([uw[:, :, DH:], q * eg], axis=1), s, _NN)
    v_new = uw[:, :, :DH] - ws[:, :CH]
    o = ws[:, CH:] + _bmm(qk, v_new, _NN)
    s_new = s * jnp.exp(g_last) + _bmm(k * jnp.exp(g_last - gc), v_new, _TN)
    y = o * lax.rsqrt(jnp.mean(o * o, axis=-1, keepdims=True) + EPS) * norm_w * _silu(z)
    return y, s_new, t


def _swap_pairs_heads(t):
    return _swap_pairs(t.reshape(t.shape[0] * CH, DH)).reshape(t.shape)


@jax.custom_vjp
def _rope(t, ct, st):
    return t * ct + _swap_pairs_heads(t) * st


def _rope_fwd(t, ct, st):
    return _rope(t, ct, st), (ct, st)


def _rope_bwd(res, d):
    ct, st = res
    return d * ct + _swap_pairs_heads(d * st), jnp.zeros_like(ct), jnp.zeros_like(st)


_rope.defvjp(_rope_fwd, _rope_bwd)


def _ret_chunk(q, k, v, rg, s, decay, xi, zeta, cd, ct, st, norm_w):
    q = _rope(q, ct, st)
    k = _rope(k, ct, st) * (DH ** -0.5)
    scores = _bmm(q, k, _NT) * decay
    o = _bmm(scores, v, _NN) + _bmm(q * xi, s, _NN)
    s_new = s * cd + _bmm(k * zeta, v, _TN)
    xc = o - jnp.mean(o, axis=-1, keepdims=True)
    var = jnp.mean(xc * xc, axis=-1, keepdims=True)
    y = _silu(rg) * (xc * lax.rsqrt(var + EPS) * norm_w)
    return y, s_new


def _head_blocks(ref):
    return jnp.stack([ref[:, h * DH:(h + 1) * DH] for h in range(H)])


def _head_rows(ref):
    return jnp.stack([ref[0, h:h + 1, :] for h in range(H)])


def _scan_fwd(name, chunk_fn, blocks, rowvecs, consts, nc, shared=(), params=(), out_dtype=F32, keep=None):
    n_blk, n_rv, n_c = len(blocks), len(rowvecs), len(consts) + len(shared) + len(params)

    def body(*refs):
        blk = refs[:n_blk]
        rvs = refs[n_blk:n_blk + n_rv]
        cst = refs[n_blk + n_rv:n_blk + n_rv + n_c]
        o_ref, st_ref = refs[n_blk + n_rv + n_c:n_blk + n_rv + n_c + 2]
        s_scr = refs[-1]

        @pl.when(pl.program_id(0) == 0)
        def _():
            s_scr[...] = jnp.zeros_like(s_scr)

        s = s_scr[...]
        st_ref[0] = s
        res = chunk_fn(*[_head_blocks(r) for r in blk], *[_head_rows(r) for r in rvs], s, *[r[...] for r in cst])
        for h in range(H):
            o_ref[:, h * DH:(h + 1) * DH] = res[0][h].astype(o_ref.dtype)
        s_scr[...] = res[1]
        if keep is not None:
            refs[-2][0] = res[2]

    in_specs = [pl.BlockSpec((CH, H * DH), lambda c, f=f: (c, f)) for _, f in blocks]
    in_specs += [pl.BlockSpec((1, H, CH), lambda c: (c, 0, 0)) for _ in rowvecs]
    in_specs += [pl.BlockSpec(a.shape, lambda c: (0, 0, 0)) for a in consts]
    in_specs += [pl.BlockSpec((CH, a.shape[1]), lambda c: (c, 0)) for a in shared]
    in_specs += [pl.BlockSpec(a.shape, lambda c, nd=a.ndim: (0,) * nd) for a in params]
    out_specs = [pl.BlockSpec((CH, H * DH), lambda c: (c, 0)), pl.BlockSpec((1, H, DH, DH), lambda c: (c, 0, 0, 0))]
    out_shape = [jax.ShapeDtypeStruct((nc * CH, H * DH), out_dtype), jax.ShapeDtypeStruct((nc, H, DH, DH), F32)]
    if keep is not None:
        out_specs.append(pl.BlockSpec((1, H) + keep, lambda c: (c, 0, 0, 0)))
        out_shape.append(jax.ShapeDtypeStruct((nc, H) + keep, F32))
    return pl.pallas_call(
        body, name=name, grid=(nc,), in_specs=in_specs, out_specs=out_specs, out_shape=out_shape,
        scratch_shapes=[pltpu.VMEM((H, DH, DH), F32)],
        compiler_params=pltpu.CompilerParams(dimension_semantics=("arbitrary",), vmem_limit_bytes=VMEM_LIMIT),
    )(*[a for a, _ in blocks], *rowvecs, *consts, *shared, *params)


def _scan_bwd(name, chunk_fn, blocks, rowvecs, consts, states, do, into, nc, shared=(), params=(), kept=None):
    n_blk, n_rv, n_c, n_p = len(blocks), len(rowvecs), len(consts) + len(shared), len(params)
    n_kept = 0 if kept is None else 1
    packed = list(into[2]) if into else []
    fresh = [k for k in range(n_blk) if k not in packed]
    n_bo = len(fresh) + (1 if into else 0)

    def body(*refs):
        blk = refs[:n_blk]
        rvs = refs[n_blk:n_blk + n_rv]
        cst = refs[n_blk + n_rv:n_blk + n_rv + n_c]
        par = refs[n_blk + n_rv + n_c:n_blk + n_rv + n_c + n_p]
        st_ref, do_ref = refs[n_blk + n_rv + n_c + n_p:n_blk + n_rv + n_c + n_p + 2]
        n_in = n_blk + n_rv + n_c + n_p + 2 + n_kept + (1 if into else 0)
        known = [refs[n_blk + n_rv + n_c + n_p + 2][0]] if n_kept else []
        o_refs = refs[n_in:n_in + n_bo]
        rv_refs = refs[n_in + n_bo:n_in + n_bo + n_rv]
        p_refs = refs[n_in + n_bo + n_rv:n_in + n_bo + n_rv + n_p]
        ds_scr = refs[n_in + n_bo + n_rv + n_p]

        @pl.when(pl.program_id(0) == 0)
        def _():
            ds_scr[...] = jnp.zeros_like(ds_scr)
            for r in p_refs:
                r[...] = jnp.zeros_like(r)

        cv = [r[...] for r in cst]
        n_d = n_blk + n_rv + 1
        _, vjp = jax.vjp(lambda *a: chunk_fn(*a[:n_d], *cv, *a[n_d:], *known)[:2], *[_head_blocks(r) for r in blk],
                         *[_head_rows(r) for r in rvs], st_ref[0], *[r[...] for r in par])
        grads = vjp((_head_blocks(do_ref), ds_scr[...]))
        for h in range(H):
            for pos, k in enumerate(fresh):
                o_refs[pos][:, h * DH:(h + 1) * DH] = grads[k][h]
            for pos, k in enumerate(packed):
                col = pos * H * DH + h * DH
                o_refs[-1][:, col:col + DH] = grads[k][h].astype(o_refs[-1].dtype)
            for r, g in zip(rv_refs, grads[n_blk:n_blk + n_rv]):
                r[0, h:h + 1, :] = g[h]
        ds_scr[...] = grads[n_blk + n_rv]
        for r, g in zip(p_refs, grads[n_d:]):
            r[...] += g

    rc = lambda c: nc - 1 - c
    in_specs = [pl.BlockSpec((CH, H * DH), lambda c, f=f: (rc(c), f)) for _, f in blocks]
    in_specs += [pl.BlockSpec((1, H, CH), lambda c: (rc(c), 0, 0)) for _ in rowvecs]
    in_specs += [pl.BlockSpec(a.shape, lambda c: (0, 0, 0)) for a in consts]
    in_specs += [pl.BlockSpec((CH, a.shape[1]), lambda c: (rc(c), 0)) for a in shared]
    in_specs += [pl.BlockSpec(a.shape, lambda c, nd=a.ndim: (0,) * nd) for a in params]
    in_specs += [pl.BlockSpec((1, H, DH, DH), lambda c: (rc(c), 0, 0, 0)), pl.BlockSpec((CH, H * DH), lambda c: (rc(c), 0))]
    ins = [a for a, _ in blocks] + list(rowvecs) + list(consts) + list(shared) + list(params) + [states, do]
    if n_kept:
        in_specs.append(pl.BlockSpec((1,) + kept.shape[1:], lambda c: (rc(c), 0, 0, 0)))
        ins.append(kept)
    out_specs, out_shapes, aliases = [], [], {}
    for _ in fresh:
        out_specs.append(pl.BlockSpec((CH, H * DH), lambda c: (rc(c), 0)))
        out_shapes.append(jax.ShapeDtypeStruct((nc * CH, H * DH), F32))
    if into:
        arr, f, _ = into
        in_specs.append(pl.BlockSpec(memory_space=pl.ANY))
        aliases[len(ins)] = len(fresh)
        ins.append(arr)
        out_specs.append(pl.BlockSpec((CH, len(packed) * H * DH), lambda c: (rc(c), f)))
        out_shapes.append(jax.ShapeDtypeStruct(arr.shape, arr.dtype))
    for _ in rowvecs:
        out_specs.append(pl.BlockSpec((1, H, CH), lambda c: (rc(c), 0, 0)))
        out_shapes.append(jax.ShapeDtypeStruct((nc, H, CH), F32))
    for a in params:
        out_specs.append(pl.BlockSpec(a.shape, lambda c, nd=a.ndim: (0,) * nd))
        out_shapes.append(jax.ShapeDtypeStruct(a.shape, F32))
    return pl.pallas_call(
        body, name=name, grid=(nc,), in_specs=in_specs, out_specs=out_specs, out_shape=out_shapes,
        input_output_aliases=aliases, scratch_shapes=[pltpu.VMEM((H, DH, DH), F32)],
        compiler_params=pltpu.CompilerParams(dimension_semantics=("arbitrary",), vmem_limit_bytes=VMEM_LIMIT),
    )(*ins)


def _gdn_pre_specs(p, cws, alog_row, dtb_row, tm, pos):
    sub = tm // 8
    rows = [pl.BlockSpec((tm, 3 * D), lambda i: (pos(i), 0)),
            pl.BlockSpec((8, 3 * D), lambda i: (jnp.maximum(pos(i) * sub - 1, 0), 0)),
            pl.BlockSpec((tm, LANES), lambda i: (pos(i), OFF_BA // LANES))]
    pars = [pl.BlockSpec(a.shape, lambda i: (0, 0)) for a in (*cws, alog_row, dtb_row)]
    return rows + pars, [p, p, p, *cws, alog_row, dtb_row]


def gdn_pre_fwd(p, cws, alog_row, dtb_row, lp, tm):
    def body(x_ref, prev_ref, ba_ref, c0, c1, c2, c3, al_ref, dt_ref, q_ref, k_ref, v_ref, g_ref, b_ref, gc_ref):
        i = pl.program_id(0)
        prev = jnp.where(i > 0, prev_ref[...], 0.0)
        xw = jnp.concatenate([prev, x_ref[...]], axis=0)
        c = _conv_taps(xw, [c0[...], c1[...], c2[...], c3[...]], tm)
        q, k, v = _gdn_qkv(c)
        mask = (_row_ids(i, tm) >= PAD).astype(F32)
        g, b, gc = _gdn_gates(ba_ref[...], al_ref[...], dt_ref[...], mask)
        q_ref[...] = q
        k_ref[...] = k
        v_ref[...] = v
        g_ref[...] = g
        b_ref[...] = b
        gc_ref[...] = gc

    in_specs, ins = _gdn_pre_specs(p, cws, alog_row, dtb_row, tm, lambda i: i)
    o_spec = pl.BlockSpec((tm, D), lambda i: (i, 0))
    return pl.pallas_call(
        body, name="gdn_pre_fwd", grid=(lp // tm,), in_specs=in_specs,
        out_specs=[o_spec] * 5 + [pl.BlockSpec((tm, LANES), lambda i: (i, 0))],
        out_shape=[jax.ShapeDtypeStruct((lp, D), F32)] * 5 + [jax.ShapeDtypeStruct((lp, LANES), F32)],
        compiler_params=pltpu.CompilerParams(dimension_semantics=("parallel",), vmem_limit_bytes=VMEM_LIMIT),
    )(*ins)


def gdn_pre_bwd(p, cws, alog_row, dtb_row, dq, dk, dv, dg, db, dgc, dp, lp, tm):
    nt = lp // tm
    pos = lambda i: nt - 1 - i

    def body(x_ref, prev_ref, ba_ref, c0, c1, c2, c3, al_ref, dt_ref, dq_ref, dk_ref, dv_ref, dg_ref, db_ref, dgc_ref,
             dp_any, dx_ref, dba_ref, dcw_ref, dpar_ref, carry):
        i = pl.program_id(0)
        t = pos(i)

        @pl.when(i == 0)
        def _():
            carry[...] = jnp.zeros_like(carry)
            dcw_ref[...] = jnp.zeros_like(dcw_ref)
            dpar_ref[...] = jnp.zeros_like(dpar_ref)

        cws_v = [c0[...], c1[...], c2[...], c3[...]]
        prev = jnp.where(t > 0, prev_ref[...], 0.0)
        xw = jnp.concatenate([prev, x_ref[...]], axis=0)
        xs = _conv_shifted(xw, tm)
        c = sum(cws_v[j] * xs[j] for j in range(CONV_K))
        _, vjp_qkv = jax.vjp(_gdn_qkv, c)
        (dc,) = vjp_qkv((dq_ref[...], dk_ref[...], dv_ref[...]))
        zeros8 = jnp.zeros((8, 3 * D), F32)
        dcp = jnp.concatenate([zeros8, dc, zeros8], axis=0)
        dxw = sum(cws_v[j] * _rows_from(dcp, 3 - j, tm + 8) for j in range(CONV_K))
        dx_ref[...] = jnp.concatenate([dxw[8:tm], dxw[tm:] + carry[...]], axis=0).astype(dx_ref.dtype)
        carry[...] = dxw[:8]
        for j in range(CONV_K):
            dcw_ref[j:j + 1, :] += jnp.sum(dc * xs[j], axis=0, keepdims=True)
        mask = (_row_ids(t, tm) >= PAD).astype(F32)
        _, vjp_g = jax.vjp(lambda ba, al, dt: _gdn_gates(ba, al, dt, mask), ba_ref[...], al_ref[...], dt_ref[...])
        dba, dal, ddt = vjp_g((dg_ref[...], db_ref[...], dgc_ref[...]))
        dba_ref[...] = dba
        dpar_ref[0:1, :] += dal
        dpar_ref[1:2, :] += ddt

    in_specs, ins = _gdn_pre_specs(p, cws, alog_row, dtb_row, tm, pos)
    g_spec = pl.BlockSpec((tm, D), lambda i: (pos(i), 0))
    s_spec = pl.BlockSpec((tm, LANES), lambda i: (pos(i), 0))
    in_specs += [g_spec] * 5 + [s_spec, pl.BlockSpec(memory_space=pl.ANY)]
    ins += [dq, dk, dv, dg, db, dgc, dp]
    return pl.pallas_call(
        body, name="gdn_pre_bwd", grid=(nt,), in_specs=in_specs,
        out_specs=[pl.BlockSpec((tm, 3 * D), lambda i: (pos(i), 0)), s_spec,
                   pl.BlockSpec((8, 3 * D), lambda i: (0, 0)), pl.BlockSpec((8, LANES), lambda i: (0, 0))],
        out_shape=[jax.ShapeDtypeStruct(dp.shape, dp.dtype), jax.ShapeDtypeStruct((lp, LANES), F32),
                   jax.ShapeDtypeStruct((8, 3 * D), F32), jax.ShapeDtypeStruct((8, LANES), F32)],
        input_output_aliases={len(ins) - 1: 0},
        scratch_shapes=[pltpu.VMEM((8, 3 * D), F32)],
        compiler_params=pltpu.CompilerParams(dimension_semantics=("arbitrary",), vmem_limit_bytes=VMEM_LIMIT),
    )(*ins)


def _me():
    return lax.axis_index("x"), lax.axis_index("y"), lax.axis_index("c")


def _any_specs(n):
    return [pl.BlockSpec(memory_space=pl.ANY)] * n


Exchange = collections.namedtuple("Exchange", "ins out_shapes scratch start finish")


def _dma_sems(*shape):
    return pltpu.SemaphoreType.DMA(shape)


def run_exchange(name, ex):
    n_i, n_o = len(ex.ins), len(ex.out_shapes)

    def body(*refs):
        parts = (refs[:n_i], refs[n_i:n_i + n_o], refs[n_i + n_o:])
        ex.start(*parts)
        ex.finish(*parts)

    return pl.pallas_call(body, name=name, out_shape=list(ex.out_shapes), in_specs=_any_specs(n_i), out_specs=_any_specs(n_o),
                          scratch_shapes=list(ex.scratch))(*ex.ins)


def gather_exchange(ws):
    n = len(ws)

    def copies(w_refs, o_refs, sems):
        send_sems, recv_sems, fsend_sems, frecv_sems, osend_sems, orecv_sems = sems
        x, y, c = _me()
        me = 2 * x + y
        chips = [(1 - x, y), (x, 1 - y), (1 - x, 1 - y)]

        def own(a):
            return pltpu.make_async_remote_copy(src_ref=w_refs[a], dst_ref=o_refs[a].at[me], send_sem=osend_sems.at[a],
                                                recv_sem=orecv_sems.at[a], device_id=(x, y, 1 - c), device_id_type=MESH)

        def rows(a, cc):
            rh = ws[a].shape[0] // 2
            return pl.ds(pl.multiple_of(cc * rh, 8), rh)

        def ici(a, j, slot):
            px, py = chips[j]
            return pltpu.make_async_remote_copy(
                src_ref=w_refs[a].at[rows(a, c)], dst_ref=o_refs[a].at[slot, rows(a, c)], send_sem=send_sems.at[a, j],
                recv_sem=recv_sems.at[a, j], device_id=(px, py, c), device_id_type=MESH)

        def d2d(a, j, cc):
            px, py = chips[j]
            blk = o_refs[a].at[2 * px + py, rows(a, cc)]
            return pltpu.make_async_remote_copy(src_ref=blk, dst_ref=blk, send_sem=fsend_sems.at[a, j],
                                                recv_sem=frecv_sems.at[a, j], device_id=(x, y, 1 - c), device_id_type=MESH)
        mine = [own(a) for a in range(n)]
        sends = [ici(a, j, me) for a in range(n) for j in range(3)]
        arrivals = [ici(a, j, 2 * px + py) for a in range(n) for j, (px, py) in enumerate(chips)]
        passes = [d2d(a, j, c) for a in range(n) for j in range(3)]
        passed_to_me = [d2d(a, j, 1 - c) for a in range(n) for j in range(3)]
        return mine, sends, arrivals, passes, passed_to_me

    def start(w_refs, o_refs, sems):
        mine, sends, _, _, _ = copies(w_refs, o_refs, sems)
        for cp in mine + sends:
            cp.start()

    def finish(w_refs, o_refs, sems):
        mine, sends, arrivals, passes, passed_to_me = copies(w_refs, o_refs, sems)
        for arrival, onward in zip(arrivals, passes):
            arrival.wait_recv()
            onward.start()
        for cp in passed_to_me + mine:
            cp.wait_recv()
        for cp in sends + passes + mine:
            cp.wait_send()

    return Exchange(list(ws), [jax.ShapeDtypeStruct((N_CHIPS,) + w.shape, w.dtype) for w in ws],
                    [_dma_sems(n, 3), _dma_sems(n, 3), _dma_sems(n, 3), _dma_sems(n, 3), _dma_sems(n), _dma_sems(n)], start, finish)


def _simple_exchange(ins, out_shapes, sem_shape, copies):
    def start(i_refs, o_refs, sems):
        for cp in copies(i_refs, o_refs, *sems):
            cp.start()

    def finish(i_refs, o_refs, sems):
        cps = copies(i_refs, o_refs, *sems)
        for cp in cps:
            cp.wait_recv()
        for cp in cps:
            cp.wait_send()

    return Exchange(list(ins), out_shapes, [_dma_sems(*sem_shape), _dma_sems(*sem_shape)], start, finish)


def sibling_halves_exchange(gs):
    def copies(g_refs, o_refs, send_sems, recv_sems):
        x, y, c = _me()
        cps = []
        for a in range(len(gs)):
            rh = gs[a].shape[1] // 2
            for q in range(N_CHIPS):
                cps.append(pltpu.make_async_remote_copy(
                    src_ref=g_refs[a].at[q, pl.ds(pl.multiple_of((1 - c) * rh, 8), rh)], dst_ref=o_refs[a].at[q],
                    send_sem=send_sems.at[a, q], recv_sem=recv_sems.at[a, q], device_id=(x, y, 1 - c), device_id_type=MESH))
        return cps

    return _simple_exchange(gs, [jax.ShapeDtypeStruct((N_CHIPS, g.shape[1] // 2, g.shape[2]), g.dtype) for g in gs],
                            (len(gs), N_CHIPS), copies)


def scatter_chips_exchange(css):
    def copies(c_refs, o_refs, send_sems, recv_sems):
        x, y, c = _me()
        chips = [(1 - x, y), (x, 1 - y), (1 - x, 1 - y)]
        return [pltpu.make_async_remote_copy(
            src_ref=c_refs[a].at[2 * px + py], dst_ref=o_refs[a].at[j], send_sem=send_sems.at[a, j],
            recv_sem=recv_sems.at[a, j], device_id=(px, py, c), device_id_type=MESH)
            for a in range(len(css)) for j, (px, py) in enumerate(chips)]

    return _simple_exchange(css, [jax.ShapeDtypeStruct((3,) + cs.shape[1:], cs.dtype) for cs in css], (len(css), 3), copies)


def sibling_swap(name, halves):
    n = len(halves)

    def body(*refs):
        h_refs, o_refs = refs[:n], refs[n:2 * n]
        send_sems, recv_sems = refs[2 * n:]
        x, y, c = _me()
        cps = [pltpu.make_async_remote_copy(src_ref=h_refs[a], dst_ref=o_refs[a], send_sem=send_sems.at[a],
                                            recv_sem=recv_sems.at[a], device_id=(x, y, 1 - c), device_id_type=MESH)
               for a in range(n)]
        for cp in cps:
            cp.start()
        for cp in cps:
            cp.wait_recv()
        for cp in cps:
            cp.wait_send()

    return pl.pallas_call(
        body, name=name, out_shape=[jax.ShapeDtypeStruct(h.shape, h.dtype) for h in halves],
        in_specs=_any_specs(n), out_specs=_any_specs(n),
        scratch_shapes=[pltpu.SemaphoreType.DMA((n,)), pltpu.SemaphoreType.DMA((n,))],
    )(*halves)


def allgather_all(name, s):
    def body(s_ref, out_ref, send_sems, recv_sems, local_sem):
        x, y, c = _me()
        peers = [(x ^ ((m >> 2) & 1), y ^ ((m >> 1) & 1), c ^ (m & 1)) for m in range(1, 8)]
        mine = pltpu.make_async_copy(s_ref, out_ref.at[4 * x + 2 * y + c], local_sem)
        mine.start()

        def copy(j, slot):
            return pltpu.make_async_remote_copy(src_ref=s_ref, dst_ref=out_ref.at[slot], send_sem=send_sems.at[j],
                                                recv_sem=recv_sems.at[j], device_id=peers[j], device_id_type=MESH)
        sends = [copy(j, 4 * x + 2 * y + c) for j in range(7)]
        for cp in sends:
            cp.start()
        for j, (px, py, pc) in enumerate(peers):
            copy(j, 4 * px + 2 * py + pc).wait_recv()
        for cp in sends:
            cp.wait_send()
        mine.wait()

    return pl.pallas_call(
        body, name=name, out_shape=jax.ShapeDtypeStruct((8,) + s.shape, s.dtype),
        in_specs=_any_specs(1), out_specs=pl.BlockSpec(memory_space=pl.ANY),
        scratch_shapes=[pltpu.SemaphoreType.DMA((7,)), pltpu.SemaphoreType.DMA((7,)), pltpu.SemaphoreType.DMA(())],
    )(s)


def sum_slots(name, parts):
    n = len(parts)
    R, W = parts[0][0].shape[1:]
    tm = _tile_rows(R, W)
    idx = jnp.stack([jnp.asarray(s, jnp.int32) for _, s in parts])

    def body(idx_ref, *refs):
        acc = refs[0][...].astype(F32)
        for r in refs[1:n]:
            acc = acc + r[...].astype(F32)
        refs[n][...] = acc

    grid_spec = pltpu.PrefetchScalarGridSpec(
        num_scalar_prefetch=1, grid=(R // tm,),
        in_specs=[pl.BlockSpec((None, tm, W), lambda i, idx, k=k: (idx[k], i, 0)) for k in range(n)],
        out_specs=pl.BlockSpec((tm, W), lambda i, idx: (i, 0)))
    return pl.pallas_call(body, name=name, grid_spec=grid_spec, out_shape=jax.ShapeDtypeStruct((R, W), F32),
                          compiler_params=pltpu.CompilerParams(dimension_semantics=("parallel",)))(idx, *[a for a, _ in parts])


def chip_sums(name, g, recv, c):
    _, R, W = g.shape
    rh = R // 2
    tm = _tile_rows(rh, W)
    g8 = g.reshape(2 * N_CHIPS, rh, W)
    idx = jnp.asarray(c, jnp.int32).reshape(1)

    def body(idx_ref, g_ref, r_ref, o_ref):
        o_ref[...] = (g_ref[...] + r_ref[...]).astype(o_ref.dtype)

    grid_spec = pltpu.PrefetchScalarGridSpec(
        num_scalar_prefetch=1, grid=(N_CHIPS, rh // tm),
        in_specs=[pl.BlockSpec((None, tm, W), lambda q, i, idx: (2 * q + idx[0], i, 0)),
                  pl.BlockSpec((None, tm, W), lambda q, i, idx: (q, i, 0))],
        out_specs=pl.BlockSpec((None, tm, W), lambda q, i, idx: (q, i, 0)))
    return pl.pallas_call(body, name=name, grid_spec=grid_spec, out_shape=jax.ShapeDtypeStruct((N_CHIPS, rh, W), BF16),
                          compiler_params=pltpu.CompilerParams(dimension_semantics=("parallel", "parallel")))(idx, g8, recv)


def _adamw_update(w, g, m, v):
    m = ADAM_B1 * m + (1.0 - ADAM_B1) * g
    v = ADAM_B2 * v + (1.0 - ADAM_B2) * (g * g)
    m_hat = m / (1.0 - ADAM_B1 ** ADAM_STEP)
    v_hat = v / (1.0 - ADAM_B2 ** ADAM_STEP)
    return -ADAM_LR * (m_hat / (jnp.sqrt(v_hat) + ADAM_EPS) + ADAM_WD * w), m, v


def adamw(name, w, g, m, v):
    R, W = w.shape
    return rowwise(name, lambda i, w, g, m, v: _adamw_update(w, g, m, v), [(a, W, 0) for a in (w, g, m, v)], [],
                   [("new", W, F32)] * 3, n_rows=R, tm=_tile_rows(R, W, 2 ** 20))


def adamw_halves(name, w, g_mine, g_other, m, v, c):
    R, W = w.shape
    rh = R // 2
    tm = _tile_rows(rh, W, 2 ** 20)
    nh = rh // tm
    idx = jnp.asarray(c, jnp.int32).reshape(1)

    def body(idx_ref, w_ref, ga_ref, gb_ref, m_ref, v_ref, g_ref, d_ref, mo_ref, vo_ref):
        mine = (pl.program_id(0) // nh) == idx_ref[0]
        g = jnp.where(mine, ga_ref[...], gb_ref[...])
        d, m, v = _adamw_update(w_ref[...], g, m_ref[...], v_ref[...])
        g_ref[...] = g
        d_ref[...] = d
        mo_ref[...] = m
        vo_ref[...] = v

    full = pl.BlockSpec((tm, W), lambda i, idx: (i, 0))
    half = pl.BlockSpec((tm, W), lambda i, idx: (i % nh, 0))
    grid_spec = pltpu.PrefetchScalarGridSpec(num_scalar_prefetch=1, grid=(R // tm,), in_specs=[full, half, half, full, full],
                                             out_specs=[full] * 4)
    return pl.pallas_call(body, name=name, grid_spec=grid_spec, out_shape=[jax.ShapeDtypeStruct((R, W), F32)] * 4,
                          compiler_params=pltpu.CompilerParams(dimension_semantics=("parallel",)))(idx, w, g_mine, g_other, m, v)


W_IN_BLOCK = D_PROJ // N_CHIPS
BA_REF = OFF_RQ


def _w_in_padded_from_blocks(g):
    lo, hi = BA_REF - W_IN_BLOCK, BA_REF + 16 - W_IN_BLOCK
    return jnp.concatenate([g[0], g[1][:, :lo], g[1][:, hi:], g[2], g[3], g[1][:, lo:hi],
                            jnp.zeros((D, PW - D_PROJ), g.dtype)], axis=1)


def _w_in_grad_blocks(g):
    s = W_IN_BLOCK
    second = jnp.concatenate([g[:, s:BA_REF], g[:, OFF_BA:OFF_BA + 16], g[:, BA_REF:2 * s - 16]], axis=1)
    return jnp.stack([g[:, :s], second, g[:, 2 * s - 16:3 * s - 16], g[:, 3 * s - 16:4 * s - 16]])


def _ret_consts():
    f = F32
    log_gamma = jnp.log1p(-jnp.exp2(-5.0 - jnp.arange(H, dtype=f)))
    pos = jnp.arange(CH, dtype=f)
    causal = jnp.tril(jnp.ones((CH, CH), dtype=bool))
    diff = pos[:, None] - pos[None, :]
    decay = jnp.where(causal, jnp.exp(jnp.where(causal, diff, 0.0) * log_gamma[:, None, None]), 0.0)
    xi = jnp.broadcast_to(jnp.exp((pos + 1.0) * log_gamma[:, None])[:, :, None], (H, CH, DH))
    zeta = jnp.broadcast_to(jnp.exp((CH - 1.0 - pos) * log_gamma[:, None])[:, :, None], (H, CH, DH))
    cd = jnp.broadcast_to(jnp.exp(CH * log_gamma)[:, None, None], (H, 1, DH))
    return decay, xi, zeta, cd


def _rope_tables(lp):
    pos = jnp.arange(lp, dtype=F32) - float(PAD)
    inv = 1.0 / (ROPE_BASE ** jnp.linspace(0.0, 1.0, DH // 2, dtype=F32))
    ang = pos[:, None] * inv[None, :]
    cos, sin = jnp.cos(ang), jnp.sin(ang)
    ct = jnp.repeat(cos, 2, axis=1)
    st = jnp.stack([-sin, sin], axis=-1).reshape(lp, DH)
    return ct, st


def local_step(x, tgt, w, small, hooks=None):
    hooks = hooks or {}
    seq = x.shape[0]
    lp = HEAD_ROWS + seq
    nc = lp // CH
    tm = _pick(lp, 192, CH)
    row = functools.partial(rowwise, n_rows=lp, tm=tm)
    gw = {}

    def mm(name, *args, **kw):
        if name not in hooks:
            return matmul(name, *args, **kw)
        make_exchange, take = hooks[name]
        out, results = matmul(name, *args, side=make_exchange(gw), **kw)
        take(results)
        return out

    h0 = jnp.concatenate([jnp.zeros((PAD, D), F32), w["meta_tokens"], x], axis=0)
    tgt_p = jnp.concatenate([jnp.zeros((HEAD_ROWS, D), F32), tgt], axis=0)

    def ffn_fwd(tag, h, wn, n=None, next_norm=None, loss=None):
        if n is None:
            n = row(f"{tag}_norm", lambda i, h, wn: _rms(h, wn), [(h, D, 0)], [wn], [("new", D, BF16)])[0]
        g, u, mid = mm(f"{tag}_up", n, w[f"{tag}_w_in"], "nn", b_split=True, pair=True, outs=(F32, F32, BF16),
                       epilogue=lambda pr: (pr[0], pr[1], _silu(pr[0]) * pr[1]))
        down = functools.partial(mm, f"{tag}_down", mid, w[f"{tag}_w_out"], "nn", ti_cap=DOWN_ROWS, tj_cap=1024, tr_cap=2816,
                                 rows=[h])
        if loss is not None:
            target, final_w = loss
            ti = _pick(lp, DOWN_ROWS, 16)

            def head(pr, h, t, nw):
                mask = (_row_ids(pl.program_id(0), ti) >= HEAD_ROWS).astype(F32)
                y, vjp = jax.vjp(_rms, h + 0.5 * pr[0], nw)
                err = (y - t) * mask
                dh, dw = vjp(err * (1.0 / D))
                return dh, dw, jnp.sum(err * err, keepdims=True).reshape(1, 1) * (0.5 / D) * jnp.ones((1, LANES), F32)
            return down(rows=[h, target], pars=[final_w], accs=[(1, D), (1, LANES)], epilogue=head), (h, n, g, u, mid), None
        if next_norm is None:
            return down(epilogue=lambda pr, h: (h + 0.5 * pr[0],)), (h, n, g, u, mid), None
        out, n_next = down(pars=[next_norm], outs=(F32, BF16), epilogue=lambda pr, h, nw: _with_norm(h + 0.5 * pr[0], nw))
        return out, (h, n, g, u, mid), n_next

    def _with_norm(out, nw):
        return out, _rms(out, nw)

    def norm_bwd_epilogue(pr, h, dres, wn):
        _, vjp = jax.vjp(_rms, h, wn)
        dh, dw = vjp(pr[0])
        return dres + dh, dw

    def ffn_bwd(tag, dh, saved, wn):
        h, n, g, u, mid = saved
        w_in_il, w_out = w[f"{tag}_w_in_il"], w[f"{tag}_w_out"]
        dw_out = mm(f"{tag}_dwout", mid, dh, "tn", ti_cap=1408, tj_cap=1024, tr_cap=1376, epilogue=lambda pr: (0.5 * pr[0],))
        gw[f"{tag}_w_out"] = dw_out

        def act_bwd(pr, g, u):
            _, vjp = jax.vjp(lambda g, u: _silu(g) * u, g, u)
            return (jnp.concatenate(vjp(0.5 * pr[0]), axis=1),)
        dgu = mm(f"{tag}_dmid", dh, w_out, "nt", tj_cap=2 * DFF // N_CHIPS, rows=[g, u], outs=(BF16,), wide=2, epilogue=act_bwd)
        dh_in, dwn = mm(f"{tag}_dn", dgu, w_in_il, "nt", tj_cap=1024, tr_cap=DFF, rows=[h, dh], pars=[wn],
                        accs=[(1, D)], epilogue=norm_bwd_epilogue)
        dw_in = mm(f"{tag}_dwin", n, dgu, "tn", ti_cap=1024, tr_cap=2752, o_split=True, o_interleaved=True)
        return dh_in, dw_in, dw_out, dwn

    h1, ffn1_saved, n2 = ffn_fwd("ffn1", h0, small["ffn1_norm"], next_norm=small["mix_norm"])
    p = mm("mix_proj", n2, w["w_in_p"], "nn", ti_cap=1376, tj_cap=1152)

    cws = [w["gdn_conv_w"][j:j + 1] for j in range(CONV_K)]
    alog_row = jnp.zeros((1, LANES), F32).at[:, H:2 * H].set(small["gdn_a_log"])
    dtb_row = jnp.zeros((1, LANES), F32).at[:, H:2 * H].set(small["gdn_dt_bias"])
    q, k, v, gcb, bb, gc = gdn_pre_fwd(p, cws, alog_row, dtb_row, lp, tm)
    gc_rows = gc[:, H:2 * H].reshape(nc, CH, H).transpose(0, 2, 1)
    gdn_blocks = [(q, 0), (k, 0), (v, 0), (gcb, 0), (bb, 0), (p, OFF_Z // D)]
    y_a, gdn_states, gdn_inv = _scan_fwd("gdn_scan_fwd", _gdn_chunk, gdn_blocks, [gc_rows], [], nc,
                                         params=[small["gdn_out_norm"]], out_dtype=BF16, keep=(CH, CH))

    ct, st = _rope_tables(lp)
    ret_consts = list(_ret_consts())
    ret_blocks = [(p, OFF_RQ // D), (p, OFF_RK // D), (p, OFF_RV // D), (p, OFF_RG // D)]
    ret_w = small["ret_out_norm"].reshape(H, 1, DH)
    y_b, ret_states = _scan_fwd("ret_scan_fwd", _ret_chunk, ret_blocks, [], ret_consts, nc, shared=[ct, st],
                                params=[ret_w], out_dtype=BF16)

    br_a = matmul("branch_gdn", y_a, w["w_branch_gdn"], "nn", ti_cap=1376, tj_cap=1024)
    br_b = matmul("branch_ret", y_b, w["w_branch_ret"], "nn", ti_cap=1376, tj_cap=1024)
    merged = row("merge", lambda i, a, b, ga, gb_: _merge(a, b, ga, gb_),
                 [(br_a, D, 0), (br_b, D, 0), (p, D, OFF_GA // D), (p, D, OFF_GB // D)], [], [("new", D, BF16)])[0]
    h2, n3 = matmul("mix_out", merged, w["w_out"], "nn", ti_cap=1376, tj_cap=1024, rows=[h1], pars=[small["ffn2_norm"]],
                    outs=(F32, BF16), epilogue=lambda pr, h, nw: _with_norm(h + pr[0], nw))
    (dh3, d_final, loss_row), ffn2_saved, _ = ffn_fwd("ffn2", h2, small["ffn2_norm"], n=n3, loss=(tgt_p, small["final_norm"]))

    gs = {"final_norm": d_final}
    dh2, gw["ffn2_w_in"], gw["ffn2_w_out"], gs["ffn2_norm"] = ffn_bwd("ffn2", dh3, ffn2_saved, small["ffn2_norm"])
    dmerged = matmul("mix_out_dx", dh2, w["w_out"], "nt", ti_cap=1376, tj_cap=1024)
    gw["w_out"] = matmul("mix_out_dw", merged, dh2, "tn", ti_cap=1024, tj_cap=1024, tr_cap=2752)
    dp = lax.empty((lp, PW), BF16)

    def merge_bwd(i, dm, a, b, ga, gb_):
        _, vjp = jax.vjp(_merge, a, b, ga, gb_)
        da, db, dga, dgb = vjp(dm)
        return da, db, jnp.concatenate([dga, dgb], axis=1)
    da, db_, dp = row("merge_bwd", merge_bwd,
                      [(dmerged, D, 0), (br_a, D, 0), (br_b, D, 0), (p, D, OFF_GA // D), (p, D, OFF_GB // D)], [],
                      [("new", D, BF16), ("new", D, BF16), ("into", dp, 2 * D, OFF_GA // (2 * D))])
    dy_a = matmul("branch_gdn_dx", da, w["w_branch_gdn"], "nt", ti_cap=1376, tj_cap=1024)
    gw["w_branch_gdn"] = matmul("branch_gdn_dw", y_a, da, "tn", ti_cap=1024, tj_cap=1024, tr_cap=2752)
    dy_b = matmul("branch_ret_dx", db_, w["w_branch_ret"], "nt", ti_cap=1376, tj_cap=1024)
    gw["w_branch_ret"] = matmul("branch_ret_dw", y_b, db_, "tn", ti_cap=1024, tj_cap=1024, tr_cap=2752)

    dp, d_ret_w = _scan_bwd("ret_scan_bwd", _ret_chunk, ret_blocks, [], ret_consts, ret_states, dy_b,
                            (dp, OFF_RQ // (4 * D), [0, 1, 2, 3]), nc, shared=[ct, st], params=[ret_w])
    gs["ret_out_norm"] = d_ret_w.reshape(1, D)
    dq, dk, dv, dgcb, dbb, dp, dgc_rows, gs["gdn_out_norm"] = _scan_bwd(
        "gdn_scan_bwd", _gdn_chunk, gdn_blocks, [gc_rows], [], gdn_states, dy_a, (dp, OFF_Z // D, [5]), nc,
        params=[small["gdn_out_norm"]], kept=gdn_inv)
    dgc = jnp.pad(dgc_rows.transpose(0, 2, 1).reshape(lp, H), ((0, 0), (H, LANES - 2 * H)))
    dp, dba, dcw, dgate = gdn_pre_bwd(p, cws, alog_row, dtb_row, dq, dk, dv, dgcb, dbb, dgc, dp, lp, tm)
    dp = row("dp_ba", lambda i, t: t, [(dba, LANES, 0)], [], [("into", dp, LANES, OFF_BA // LANES)])[0]
    gw["gdn_conv_w"] = dcw[:CONV_K]
    gs["gdn_a_log"] = dgate[0:1, H:2 * H]
    gs["gdn_dt_bias"] = dgate[1:2, H:2 * H]

    dh1, gs["mix_norm"] = mm("mix_proj_dx", dp, w["w_in_p"], "nt", tj_cap=1024, tr_cap=3456, rows=[h1, dh2],
                                 pars=[small["mix_norm"]], accs=[(1, D)], epilogue=norm_bwd_epilogue)
    gw["w_in_p"] = mm("mix_proj_dw", n2, dp, "tn", ti_cap=1024, tj_cap=1152, tr_cap=2752)
    dh0, gw["ffn1_w_in"], gw["ffn1_w_out"], gs["ffn1_norm"] = ffn_bwd("ffn1", dh1, ffn1_saved, small["ffn1_norm"])
    gw["meta_tokens"] = dh0[PAD:HEAD_ROWS]
    return loss_row, dh0[HEAD_ROWS:], gw, gs


BIG = ("ffn1_w_in", "ffn1_w_out", "w_in", "w_branch_gdn", "w_branch_ret", "w_out", "ffn2_w_in", "ffn2_w_out")
COL_SHARDED = ("ffn1_w_in", "w_in", "ffn2_w_in")
WITH_FFN1_UP = ("ffn1_w_out", "w_in")
WITH_MIX_PROJ = ("w_branch_gdn", "w_branch_ret", "w_out", "ffn2_w_in", "ffn2_w_out")
MIXER_GRADS = ("ffn2_w_in", "ffn2_w_out", "w_branch_gdn", "w_branch_ret", "w_out")
FFN1_EARLY_GRADS = ("w_in", "ffn1_w_out")
LATE_GRADS = ("ffn1_w_in",)
SMALL = ("ffn1_norm", "mix_norm", "ret_out_norm", "ffn2_norm", "final_norm", "gdn_out_norm", "gdn_a_log", "gdn_dt_bias")
WEIGHTS = ("meta_tokens", "ffn1_norm", "ffn1_w_in", "ffn1_w_out", "mix_norm", "w_in", "gdn_conv_w", "gdn_a_log", "gdn_dt_bias",
           "gdn_out_norm", "ret_out_norm", "w_branch_gdn", "w_branch_ret", "w_out", "ffn2_norm", "ffn2_w_in", "ffn2_w_out",
           "final_norm")
LOSS_ROW = 6
CONV_ROW0, META_ROW0, SMALL_ROWS = 8, 24, 40


def pack_small(vals):
    rows = [vals[n].reshape(1, D) for n in SMALL[:5]]
    r5 = jnp.concatenate([vals["gdn_out_norm"].reshape(1, DH), vals["gdn_a_log"].reshape(1, H), vals["gdn_dt_bias"].reshape(1, H),
                          jnp.zeros((1, D - DH - 2 * H), F32)], axis=1)
    return jnp.concatenate(rows + [r5, jnp.zeros((2, D), F32)], axis=0)


def unpack_small(packed, shapes):
    out = {n: packed[j].reshape(shapes[n]) for j, n in enumerate(SMALL[:5])}
    out["gdn_out_norm"] = packed[5, :DH].reshape(shapes["gdn_out_norm"])
    out["gdn_a_log"] = packed[5, DH:DH + H].reshape(shapes["gdn_a_log"])
    out["gdn_dt_bias"] = packed[5, DH + H:DH + 2 * H].reshape(shapes["gdn_dt_bias"])
    return out


def kernel(x, meta_tokens, ffn1_norm, ffn1_w_in, ffn1_w_out, mix_norm, w_in, gdn_conv_w, gdn_a_log, gdn_dt_bias, gdn_out_norm, ret_out_norm, w_branch_gdn, w_branch_ret, w_out, ffn2_norm, ffn2_w_in, ffn2_w_out, final_norm, loss_target, m_meta_tokens, m_ffn1_norm, m_ffn1_w_in, m_ffn1_w_out, m_mix_norm, m_w_in, m_gdn_conv_w, m_gdn_a_log, m_gdn_dt_bias, m_gdn_out_norm, m_ret_out_norm, m_w_branch_gdn, m_w_branch_ret, m_w_out, m_ffn2_norm, m_ffn2_w_in, m_ffn2_w_out, m_final_norm, v_meta_tokens, v_ffn1_norm, v_ffn1_w_in, v_ffn1_w_out, v_mix_norm, v_w_in, v_gdn_conv_w, v_gdn_a_log, v_gdn_dt_bias, v_gdn_out_norm, v_ret_out_norm, v_w_branch_gdn, v_w_branch_ret, v_w_out, v_ffn2_norm, v_ffn2_w_in, v_ffn2_w_out, v_final_norm):
    a = dict(locals())
    wts = {n: a[n] for n in WEIGHTS}
    mom_m = {n: a["m_" + n] for n in WEIGHTS}
    mom_v = {n: a["v_" + n] for n in WEIGHTS}
    shapes = {n: wts[n].shape for n in WEIGHTS}
    flat = lambda t: t.reshape(t.shape[-2:])
    c = lax.axis_index("c")
    chip = 2 * lax.axis_index("x") + lax.axis_index("y")

    exact = jnp.zeros((16, D), F32).at[0:3].set(wts["gdn_conv_w"].reshape(3, D)).at[3:7].set(wts["meta_tokens"].reshape(4, D))
    bf16_block = lambda n: flat(wts[n]).astype(BF16)
    w = {}

    def take_weights(names):
        def take(gathered):
            for n, g in zip(names, gathered):
                if n == "w_in":
                    w["w_in_p"] = _w_in_padded_from_blocks(g)
                elif n in COL_SHARDED:
                    w[n], w[n + "_il"] = g, jnp.concatenate([g[q] for q in INTERLEAVED], axis=1)
                else:
                    w[n] = g.reshape(N_CHIPS * g.shape[1], D)
        return take

    first = run_exchange("gather_first", gather_exchange([bf16_block("ffn1_w_in"), exact]))
    take_weights(["ffn1_w_in"])(first[:1])
    exact = first[1]
    w["gdn_conv_w"] = jnp.concatenate([exact[q, 0:3].reshape(CONV_K, 3 * D // 4) for q in range(N_CHIPS)], axis=1)
    w["meta_tokens"] = jnp.concatenate([exact[q, 3:7].reshape(N_META, D // 4) for q in range(N_CHIPS)], axis=1)
    small = {n: wts[n].reshape(1, -1) for n in SMALL}
    hooks = {"ffn1_up": (lambda gw: gather_exchange([bf16_block(n) for n in WITH_FFN1_UP]), take_weights(WITH_FFN1_UP)),
             "mix_proj": (lambda gw: gather_exchange([bf16_block(n) for n in WITH_MIX_PROJ]), take_weights(WITH_MIX_PROJ))}

    def blocks(gw, n):
        if n == "w_in":
            return _w_in_grad_blocks(gw["w_in_p"])
        return gw[n] if n in COL_SHARDED else gw[n].reshape(N_CHIPS, gw[n].shape[0] // N_CHIPS, D)

    chip_sum, from_chips = {}, {}

    def swap_of(names):
        def make(gw):
            chip_sum["g"] = [blocks(gw, n) for n in names]
            return sibling_halves_exchange(chip_sum["g"])

        def take(from_sib):
            for n, g, r in zip(names, chip_sum.pop("g"), from_sib):
                chip_sum[n] = chip_sums(f"grads_chip_sum_{n}", g, r, c)
        return make, take

    def scatter_of(names):
        return (lambda gw: scatter_chips_exchange([chip_sum[n] for n in names])), (lambda got: from_chips.update(zip(names, got)))
    hooks["mix_proj_dx"], hooks["mix_proj_dw"] = swap_of(MIXER_GRADS), scatter_of(MIXER_GRADS)
    hooks["ffn1_dmid"] = swap_of(FFN1_EARLY_GRADS)
    hooks["ffn1_dn"], hooks["ffn1_dwin"] = scatter_of(FFN1_EARLY_GRADS[:1]), scatter_of(FFN1_EARLY_GRADS[1:])

    loss_row, gx, gw, gs = local_step(x[0], loss_target[0], w, small, hooks)

    late = [blocks(gw, n) for n in LATE_GRADS]
    from_sib = run_exchange("grads_sibling_late", sibling_halves_exchange(late))
    for n, g, r in zip(LATE_GRADS, late, from_sib):
        chip_sum[n] = chip_sums(f"grads_chip_sum_{n}", g, r, c)
    from_chips.update(zip(LATE_GRADS, run_exchange("grads_scatter_late", scatter_chips_exchange([chip_sum[n] for n in LATE_GRADS]))))
    halves = [sum_slots(f"grads_sum_{n}", [(chip_sum[n], chip), (from_chips[n], 0), (from_chips[n], 1), (from_chips[n], 2)])
              for n in BIG]
    others = sibling_swap("grads_swap", halves)
    grads, delta, new_m, new_v = {}, {}, {}, {}
    for n, mine, other in zip(BIG, halves, others):
        res = adamw_halves(f"adamw_{n}", flat(wts[n]), mine, other, flat(mom_m[n]), flat(mom_v[n]), c)
        grads[n], delta[n], new_m[n], new_v[n] = (t.reshape(shapes[n]) for t in res)

    sm = jnp.concatenate([pack_small(gs).at[LOSS_ROW, :LANES].set(loss_row[0]),
                          gw["gdn_conv_w"].reshape(3 * CONV_K, D), jnp.zeros((META_ROW0 - CONV_ROW0 - 3 * CONV_K, D), F32),
                          gw["meta_tokens"]], axis=0)
    every = allgather_all("small_gather", sm)
    sm_sum = sum_slots("small_sum", [(every, s) for s in range(8)])
    d_s, m_s, v_s = adamw("adamw_small", pack_small(small), sm_sum[:8], pack_small({n: mom_m[n].reshape(1, -1) for n in SMALL}),
                          pack_small({n: mom_v[n].reshape(1, -1) for n in SMALL}))
    grads.update(unpack_small(sm_sum, shapes))
    delta.update(unpack_small(d_s, shapes))
    new_m.update(unpack_small(m_s, shapes))
    new_v.update(unpack_small(v_s, shapes))
    g_conv = lax.dynamic_slice_in_dim(sm_sum[CONV_ROW0:CONV_ROW0 + 3 * CONV_K].reshape(CONV_K, 3 * D), chip * (3 * D // 4), 3 * D // 4, 1)
    g_meta = lax.dynamic_slice_in_dim(sm_sum[META_ROW0:META_ROW0 + N_META], chip * (D // 4), D // 4, 1)
    for n, g in (("gdn_conv_w", g_conv), ("meta_tokens", g_meta)):
        d_, m_, v_ = adamw(f"adamw_{n}", flat(wts[n]), g, flat(mom_m[n]), flat(mom_v[n]))
        grads[n], delta[n], new_m[n], new_v[n] = (t.reshape(shapes[n]) for t in (g, d_, m_, v_))
    loss = sm_sum[LOSS_ROW, 0]

    return (loss, gx[None], *[grads[n] for n in WEIGHTS], *[delta[n] for n in WEIGHTS], *[new_m[n] for n in WEIGHTS],
            *[new_v[n] for n in WEIGHTS])
```

```python
import collections
import functools

import jax
import jax.numpy as jnp
from jax import lax
from jax.experimental import pallas as pl
from jax.experimental.pallas import tpu as pltpu

F32 = jnp.float32
BF16 = jnp.bfloat16
HI = lax.Precision.HIGHEST
MESH = pl.DeviceIdType.MESH

D = 1024
N_META = 16
PAD = 48
HEAD_ROWS = PAD + N_META
CH = 64
H = 8
DH = 128
DFF = 2816
CONV_K = 4
EPS = 1e-6
ROPE_BASE = 10000.0
LANES = 128
N_CHIPS = 4
VMEM_LIMIT = 56 * 2 ** 20
DOWN_ROWS = 688
INTERLEAVED = (0, 2, 1, 3)

OFF_QKV, OFF_Z, OFF_RQ, OFF_RK, OFF_RV, OFF_RG, OFF_GA, OFF_GB, OFF_BA = 0, 3072, 4096, 5120, 6144, 7168, 8192, 9216, 10240
PW = 10368
D_PROJ = 10256

ADAM_LR, ADAM_B1, ADAM_B2, ADAM_EPS, ADAM_WD, ADAM_STEP = 0.001, 0.9, 0.999, 1e-08, 0.01, 10


def _pick(n, cap, mult):
    best = None
    for t in range(mult, min(n, cap) + 1, mult):
        if n % t == 0:
            best = t
    return best if best is not None else n


def _tile_rows(rows, cols, block_bytes=3 * 2 ** 19):
    return _pick(rows, max(8, block_bytes // (4 * cols)), 8)


def _dot(a, b, dims, prec=None):
    return lax.dot_general(a, b, (dims, ((), ())), precision=prec, preferred_element_type=F32)


def _mmh(a, b):
    return _dot(a, b, ((1,), (0,)), HI)


def _split(a):
    hi = a.astype(BF16)
    return hi, (a - hi.astype(F32)).astype(BF16)


_NN, _NT, _TN = ((2,), (1,)), ((2,), (2,)), ((1,), (1,))


def _bdot(a, b, dims):
    return lax.dot_general(a, b, (dims, ((0,), (0,))), preferred_element_type=F32)


def _bmm(a, b, dims):
    return _bdot(a.astype(BF16), b.astype(BF16), dims)


def _bdot3(a, b, dims):
    ah, al = _split(a)
    bh, bl = _split(b)
    if dims == _NN:
        m = a.shape[1]
        both = _bdot(jnp.concatenate([ah, al], axis=1), bh, dims)
        return both[:, :m] + (both[:, m:] + _bdot(ah, bl, dims))
    return _bdot(ah, bh, dims) + (_bdot(ah, bl, dims) + _bdot(al, bh, dims))


def matmul(name, a, b, mode, *, ti_cap=688, tj_cap=512, tr_cap=1408, b_split=False, o_split=False, pair=False,
           rows=(), pars=(), epilogue=None, outs=(F32,), accs=(), side=None, wide=1, o_interleaved=False):
    if mode == "nn":
        I, R = a.shape
        J = N_CHIPS * b.shape[2] if b_split else b.shape[1]
    elif mode == "nt":
        (I, R), J = a.shape, b.shape[0]
    else:
        (R, I), J = a.shape, b.shape[1]
    ti = _pick(I, ti_cap, 16) if mode != "tn" else _pick(I, ti_cap, LANES)
    tj = _pick(J, tj_cap, LANES)
    tr = _pick(R, tr_cap, LANES) if mode != "tn" else _pick(R, tr_cap, 16)
    half = N_CHIPS // 2
    if mode == "nn":
        a_spec = pl.BlockSpec((ti, tr), lambda i, j, r: (i, r))
        if b_split:
            tj = J // N_CHIPS
            b_spec = pl.BlockSpec((None, tr, tj), lambda i, j, r: (j, r, 0))
            b2_spec = pl.BlockSpec((None, tr, tj), lambda i, j, r: (j + half, r, 0))
        else:
            b_spec = pl.BlockSpec((tr, tj), lambda i, j, r: (r, j))
        dims = ((1,), (0,))
    elif mode == "nt":
        a_spec = pl.BlockSpec((ti, tr), lambda i, j, r: (i, r))
        b_spec = pl.BlockSpec((tj, tr), lambda i, j, r: (j, r))
        dims = ((1,), (1,))
    else:
        if o_split:
            tj = J // N_CHIPS
        a_spec = pl.BlockSpec((tr, ti), lambda i, j, r: (r, i))
        b_spec = pl.BlockSpec((tr, tj), lambda i, j, r: (r, j))
        dims = ((0,), (0,))
    assert not pair or (mode == "nn" and b_split), name
    j_out = J // 2 if pair else J
    nr = R // tr
    assert I % ti == 0 and j_out % tj == 0 and R % tr == 0, (name, I, J, R, ti, tj, tr)
    grid = (I // ti, j_out // tj, nr)
    w_spec = o_spec = pl.BlockSpec((ti, tj), lambda i, j, r: (i, j))
    if o_split:
        assert epilogue is None, name
        blk = (lambda j: (j % 2) * 2 + j // 2) if o_interleaved else (lambda j: j)
        w_spec = pl.BlockSpec((None, ti, tj), lambda i, j, r: (blk(j), i, 0))
        out_shapes = [jax.ShapeDtypeStruct((N_CHIPS, I, tj), outs[0])]
    else:
        w_spec = pl.BlockSpec((ti, wide * tj), lambda i, j, r: (i, j))
        out_shapes = [jax.ShapeDtypeStruct((I, wide * j_out), dt) for dt in outs]
    n_prod = 2 if pair else 1
    n_out, n_acc, n_rows, n_pars = len(out_shapes), len(accs), len(rows), len(pars)
    n_si = len(side.ins) if side else 0
    n_so = len(side.out_shapes) if side else 0
    n_in = 1 + n_prod + n_rows + n_pars + n_si
    n_steps = grid[0] * grid[1] * grid[2]

    def body(*refs):
        a_ref, b_refs = refs[0], refs[1:1 + n_prod]
        row_refs = refs[1 + n_prod:1 + n_prod + n_rows]
        par_refs = refs[1 + n_prod + n_rows:1 + n_prod + n_rows + n_pars]
        out_refs = refs[n_in:n_in + n_out]
        acc_refs = refs[n_in + n_out:n_in + n_out + n_acc]
        scr = refs[n_in + n_out + n_acc + n_so:]
        partial = scr[:n_prod] if nr > 1 else ()
        step = (pl.program_id(0) * grid[1] + pl.program_id(1)) * grid[2] + pl.program_id(2)
        if side:
            s_refs = (refs[n_in - n_si:n_in], refs[n_in + n_out + n_acc:n_in + n_out + n_acc + n_so], scr[len(partial):])

        if side or n_acc:
            @pl.when(step == 0)
            def _():
                if side:
                    side.start(*s_refs)
                for r in acc_refs:
                    r[...] = jnp.zeros_like(r)

        lhs = a_ref[...].astype(BF16)
        prods = [_dot(lhs, r[...].astype(BF16), dims) for r in b_refs]

        def finish(prods):
            res = epilogue(prods, *[r[...] for r in row_refs], *[r[...] for r in par_refs]) if epilogue else prods
            for r, v in zip(out_refs, res[:n_out]):
                r[...] = v.astype(r.dtype)
            for r, v in zip(acc_refs, res[n_out:]):
                r[...] += v

        if nr == 1:
            finish(prods)
        else:
            k = pl.program_id(2)

            @pl.when(k == 0)
            def _():
                for r, v in zip(partial, prods):
                    r[...] = v

            @pl.when(k > 0)
            def _():
                for r, v in zip(partial, prods):
                    r[...] += v

            @pl.when(k == nr - 1)
            def _():
                finish([r[...] for r in partial])

        if side:
            @pl.when(step == n_steps - 1)
            def _():
                side.finish(*s_refs)

    ins = [a, b] + ([b] if pair else []) + list(rows) + list(pars)
    in_specs = [a_spec, b_spec] + ([b2_spec] if pair else []) + [o_spec] * n_rows
    in_specs += [pl.BlockSpec(p.shape, lambda i, j, r, nd=p.ndim: (0,) * nd) for p in pars]
    out_specs = [w_spec] * n_out + [pl.BlockSpec(s, lambda i, j, r: (0, 0)) for s in accs]
    out_shapes += [jax.ShapeDtypeStruct(s, F32) for s in accs]
    scratch = [pltpu.VMEM((ti, tj), F32)] * n_prod if nr > 1 else []
    ordered = bool(side) or n_acc > 0
    res = pl.pallas_call(
        body, name=name, grid=grid, in_specs=in_specs + _any_specs(n_si), out_specs=out_specs + _any_specs(n_so),
        out_shape=out_shapes + (list(side.out_shapes) if side else []), scratch_shapes=scratch + (list(side.scratch) if side else []),
        compiler_params=pltpu.CompilerParams(
            dimension_semantics=("arbitrary",) * 3 if ordered else ("parallel", "parallel", "arbitrary"), vmem_limit_bytes=VMEM_LIMIT),
    )(*ins, *(side.ins if side else []))
    own = res[:n_out + n_acc]
    own = own[0] if len(own) == 1 else tuple(own)
    return (own, list(res[n_out + n_acc:])) if side else own


def rowwise(name, fn, rows, pars, outs, accs=(), *, n_rows, tm, side=None):
    nt = n_rows // tm
    assert nt * tm == n_rows
    in_specs, ins = [], []
    for arr, w, cb in rows:
        in_specs.append(pl.BlockSpec((tm, w), lambda i, cb=cb: (i, cb)))
        ins.append(arr)
    for p in pars:
        in_specs.append(pl.BlockSpec(p.shape, lambda i, nd=p.ndim: (0,) * nd))
        ins.append(p)
    n_in = len(ins)
    out_specs, out_shapes, aliases = [], [], {}
    for k, o in enumerate(outs):
        if o[0] == "new":
            _, w, dt = o
            out_specs.append(pl.BlockSpec((tm, w), lambda i: (i, 0)))
            out_shapes.append(jax.ShapeDtypeStruct((n_rows, w), dt))
        else:
            _, arr, w, cb = o
            in_specs.append(pl.BlockSpec(memory_space=pl.ANY))
            aliases[len(ins)] = k
            ins.append(arr)
            out_specs.append(pl.BlockSpec((tm, w), lambda i, cb=cb: (i, cb)))
            out_shapes.append(jax.ShapeDtypeStruct(arr.shape, arr.dtype))
    for r, w in accs:
        out_specs.append(pl.BlockSpec((r, w), lambda i: (0, 0)))
        out_shapes.append(jax.ShapeDtypeStruct((r, w), F32))
    n_own_in, n_out, n_acc = len(ins), len(outs), len(accs)
    n_si = len(side.ins) if side else 0
    n_so = len(side.out_shapes) if side else 0
    n_all_in = n_own_in + n_si

    def body(*refs):
        i = pl.program_id(0)
        o_refs = refs[n_all_in:n_all_in + n_out]
        a_refs = refs[n_all_in + n_out:n_all_in + n_out + n_acc]
        if side:
            s_refs = (refs[n_own_in:n_all_in], refs[n_all_in + n_out + n_acc:n_all_in + n_out + n_acc + n_so],
                      refs[n_all_in + n_out + n_acc + n_so:])

            @pl.when(i == 0)
            def _():
                side.start(*s_refs)

        vals = [r[...] for r in refs[:n_in]]
        res = fn(i, *vals)
        if not isinstance(res, (tuple, list)):
            res = (res,)
        for r, v in zip(o_refs, res[:n_out]):
            r[...] = v.astype(r.dtype)
        if n_acc:
            @pl.when(i == 0)
            def _():
                for r in a_refs:
                    r[...] = jnp.zeros_like(r)
            for r, v in zip(a_refs, res[n_out:]):
                r[...] += v
        if side:
            @pl.when(i == nt - 1)
            def _():
                side.finish(*s_refs)

    res = pl.pallas_call(
        body, name=name, grid=(nt,), in_specs=in_specs + _any_specs(n_si), out_specs=out_specs + _any_specs(n_so),
        out_shape=out_shapes + (list(side.out_shapes) if side else []), input_output_aliases=aliases,
        scratch_shapes=list(side.scratch) if side else [],
        compiler_params=pltpu.CompilerParams(dimension_semantics=("arbitrary",), vmem_limit_bytes=VMEM_LIMIT),
    )(*ins, *(side.ins if side else []))
    return (res[:n_out + n_acc], list(res[n_out + n_acc:])) if side else res


def _row_ids(i, tm):
    return i * tm + lax.broadcasted_iota(jnp.int32, (tm, 1), 0)


def _sigmoid(x):
    return 1.0 / (1.0 + jnp.exp(-x))


def _silu(x):
    return x * _sigmoid(x)


def _softplus(x):
    return jnp.maximum(x, 0.0) + jnp.log(1.0 + jnp.exp(-jnp.abs(x)))


def _rms(x, w):
    return x * lax.rsqrt(jnp.mean(x * x, axis=-1, keepdims=True) + EPS) * w


def _heads(fn, *xs):
    return jnp.concatenate([fn(h, *[x[:, h * DH:(h + 1) * DH] for x in xs]) for h in range(H)], axis=1)


def _merge(a, b, ga, gb):
    return _sigmoid(ga) * a + _sigmoid(gb) * b


def _select_matrix(first_lane):
    r = lax.broadcasted_iota(jnp.int32, (LANES, H * DH), 0)
    c = lax.broadcasted_iota(jnp.int32, (LANES, H * DH), 1)
    return (r == first_lane + (c >> 7)).astype(F32)


def _chunk_tri(tm):
    r = lax.broadcasted_iota(jnp.int32, (tm, tm), 0)
    c = lax.broadcasted_iota(jnp.int32, (tm, tm), 1)
    return jnp.logical_and((r >> 6) == (c >> 6), r >= c).astype(F32)


def _gdn_gates(ba, alog_row, dtb_row, mask):
    g = -jnp.exp(alog_row) * _softplus(ba + dtb_row) * mask
    gc = _mmh(_chunk_tri(ba.shape[0]), g)
    beta = _sigmoid(ba) * mask
    return _mmh(gc, _select_matrix(H)), _mmh(beta, _select_matrix(0)), gc


def _gdn_qkv(c):
    a = _silu(c)

    def l2(scale):
        return lambda h, t: t * lax.rsqrt(jnp.sum(t * t, axis=-1, keepdims=True) + EPS) * scale
    q = _heads(l2(DH ** -0.5), a[:, :D])
    k = _heads(l2(1.0), a[:, D:2 * D])
    return q, k, a[:, 2 * D:]


def _rows_from(x, start, n):
    base = start - start % 8
    if base == start:
        return x[start:start + n]
    return pltpu.roll(x, x.shape[0] - start % 8, 0)[base:base + n]


def _conv_shifted(xw, tm):
    return [_rows_from(xw, 5 + j, tm) for j in range(CONV_K)]


def _conv_taps(xw, cws, tm):
    return sum(w * x for w, x in zip(cws, _conv_shifted(xw, tm)))


def _swap_pairs(t):
    n = t.shape[1]
    lane = lax.broadcasted_iota(jnp.int32, t.shape, 1)
    return jnp.where((lane & 1) == 0, pltpu.roll(t, n - 1, 1), pltpu.roll(t, 1, 1))


@jax.custom_vjp
def _unit_lower_inv(a):
    n = -a
    eye = (lax.broadcasted_iota(jnp.int32, a.shape, 1) == lax.broadcasted_iota(jnp.int32, a.shape, 2)).astype(F32)
    p = eye + n
    for _ in range(5):
        n = _bdot3(n, n, _NN)
        p = p + _bdot3(p, n, _NN)
    return p


def _inv_fwd(a):
    t = _unit_lower_inv(a)
    return t, t


def _inv_bwd(t, dt):
    x = _bdot3(t, dt, _TN)
    return (-_bdot3(x, t, _NT),)


_unit_lower_inv.defvjp(_inv_fwd, _inv_bwd)


@jax.custom_vjp
def _unit_lower_inv_known(a, t):
    return t


_unit_lower_inv_known.defvjp(lambda a, t: (t, t), lambda t, dt: (_inv_bwd(t, dt)[0], jnp.zeros_like(t)))


def _gdn_chunk(q, k, v, gc, bb, z, gr, s, norm_w, t_known=None):
    ri = lax.broadcasted_iota(jnp.int32, (H, CH, CH), 1)
    ci = lax.broadcasted_iota(jnp.int32, (H, CH, CH), 2)
    causal = ri >= ci
    gc1 = jnp.sum(gc, axis=2, keepdims=True) * (1.0 / LANES)
    diff = jnp.broadcast_to(gc1, (H, CH, CH)) - jnp.broadcast_to(gr, (H, CH, CH))
    decay = jnp.where(causal, jnp.exp(jnp.where(causal, diff, 0.0)), 0.0)
    kb = k * bb
    sc = _bmm(jnp.concatenate([q, kb], axis=1), k, _NT)
    qk = sc[:, :CH] * decay
    a = jnp.where(ri > ci, sc[:, CH:] * decay, 0.0)
    t = _unit_lower_inv(a) if t_known is None else _unit_lower_inv_known(a, t_known)
    eg = jnp.exp(gc)
    uw = _bmm(t, jnp.concatenate([v * bb, kb * eg], axis=2), _NN)
    g_last = gc[:, CH - 1:CH, :]
    ws = _bmm(jnp.concatenate([uw[:, :, DH:], q * eg], axis=1), s, _NN)
    v_new = uw[:, :, :DH] - ws[:, :CH]
    o = ws[:, CH:] + _bmm(qk, v_new, _NN)
    s_new = s * jnp.exp(g_last) + _bmm(k * jnp.exp(g_last - gc), v_new, _TN)
    y = o * lax.rsqrt(jnp.mean(o * o, axis=-1, keepdims=True) + EPS) * norm_w * _silu(z)
    return y, s_new, t


def _swap_pairs_heads(t):
    return _swap_pairs(t.reshape(t.shape[0] * CH, DH)).reshape(t.shape)


@jax.custom_vjp
def _rope(t, ct, st):
    return t * ct + _swap_pairs_heads(t) * st


def _rope_fwd(t, ct, st):
    return _rope(t, ct, st), (ct, st)


def _rope_bwd(res, d):
    ct, st = res
    return d * ct + _swap_pairs_heads(d * st), jnp.zeros_like(ct), jnp.zeros_like(st)


_rope.defvjp(_rope_fwd, _rope_bwd)


def _ret_chunk(q, k, v, rg, s, decay, xi, zeta, cd, ct, st, norm_w):
    q = _rope(q, ct, st)
    k = _rope(k, ct, st) * (DH ** -0.5)
    scores = _bmm(q, k, _NT) * decay
    o = _bmm(scores, v, _NN) + _bmm(q * xi, s, _NN)
    s_new = s * cd + _bmm(k * zeta, v, _TN)
    xc = o - jnp.mean(o, axis=-1, keepdims=True)
    var = jnp.mean(xc * xc, axis=-1, keepdims=True)
    y = _silu(rg) * (xc * lax.rsqrt(var + EPS) * norm_w)
    return y, s_new


def _head_blocks(ref):
    return jnp.stack([ref[:, h * DH:(h + 1) * DH] for h in range(H)])


def _head_rows(ref):
    return jnp.stack([ref[0, h:h + 1, :] for h in range(H)])


def _scan_fwd(name, chunk_fn, blocks, rowvecs, consts, nc, shared=(), params=(), out_dtype=F32, keep=None):
    n_blk, n_rv, n_c = len(blocks), len(rowvecs), len(consts) + len(shared) + len(params)

    def body(*refs):
        blk = refs[:n_blk]
        rvs = refs[n_blk:n_blk + n_rv]
        cst = refs[n_blk + n_rv:n_blk + n_rv + n_c]
        o_ref, st_ref = refs[n_blk + n_rv + n_c:n_blk + n_rv + n_c + 2]
        s_scr = refs[-1]

        @pl.when(pl.program_id(0) == 0)
        def _():
            s_scr[...] = jnp.zeros_like(s_scr)

        s = s_scr[...]
        st_ref[0] = s
        res = chunk_fn(*[_head_blocks(r) for r in blk], *[_head_rows(r) for r in rvs], s, *[r[...] for r in cst])
        for h in range(H):
            o_ref[:, h * DH:(h + 1) * DH] = res[0][h].astype(o_ref.dtype)
        s_scr[...] = res[1]
        if keep is not None:
            refs[-2][0] = res[2]

    in_specs = [pl.BlockSpec((CH, H * DH), lambda c, f=f: (c, f)) for _, f in blocks]
    in_specs += [pl.BlockSpec((1, H, CH), lambda c: (c, 0, 0)) for _ in rowvecs]
    in_specs += [pl.BlockSpec(a.shape, lambda c: (0, 0, 0)) for a in consts]
    in_specs += [pl.BlockSpec((CH, a.shape[1]), lambda c: (c, 0)) for a in shared]
    in_specs += [pl.BlockSpec(a.shape, lambda c, nd=a.ndim: (0,) * nd) for a in params]
    out_specs = [pl.BlockSpec((CH, H * DH), lambda c: (c, 0)), pl.BlockSpec((1, H, DH, DH), lambda c: (c, 0, 0, 0))]
    out_shape = [jax.ShapeDtypeStruct((nc * CH, H * DH), out_dtype), jax.ShapeDtypeStruct((nc, H, DH, DH), F32)]
    if keep is not None:
        out_specs.append(pl.BlockSpec((1, H) + keep, lambda c: (c, 0, 0, 0)))
        out_shape.append(jax.ShapeDtypeStruct((nc, H) + keep, F32))
    return pl.pallas_call(
        body, name=name, grid=(nc,), in_specs=in_specs, out_specs=out_specs, out_shape=out_shape,
        scratch_shapes=[pltpu.VMEM((H, DH, DH), F32)],
        compiler_params=pltpu.CompilerParams(dimension_semantics=("arbitrary",), vmem_limit_bytes=VMEM_LIMIT),
    )(*[a for a, _ in blocks], *rowvecs, *consts, *shared, *params)


def _scan_bwd(name, chunk_fn, blocks, rowvecs, consts, states, do, into, nc, shared=(), params=(), kept=None):
    n_blk, n_rv, n_c, n_p = len(blocks), len(rowvecs), len(consts) + len(shared), len(params)
    n_kept = 0 if kept is None else 1
    packed = list(into[2]) if into else []
    fresh = [k for k in range(n_blk) if k not in packed]
    n_bo = len(fresh) + (1 if into else 0)

    def body(*refs):
        blk = refs[:n_blk]
        rvs = refs[n_blk:n_blk + n_rv]
        cst = refs[n_blk + n_rv:n_blk + n_rv + n_c]
        par = refs[n_blk + n_rv + n_c:n_blk + n_rv + n_c + n_p]
        st_ref, do_ref = refs[n_blk + n_rv + n_c + n_p:n_blk + n_rv + n_c + n_p + 2]
        n_in = n_blk + n_rv + n_c + n_p + 2 + n_kept + (1 if into else 0)
        known = [refs[n_blk + n_rv + n_c + n_p + 2][0]] if n_kept else []
        o_refs = refs[n_in:n_in + n_bo]
        rv_refs = refs[n_in + n_bo:n_in + n_bo + n_rv]
        p_refs = refs[n_in + n_bo + n_rv:n_in + n_bo + n_rv + n_p]
        ds_scr = refs[n_in + n_bo + n_rv + n_p]

        @pl.when(pl.program_id(0) == 0)
        def _():
            ds_scr[...] = jnp.zeros_like(ds_scr)
            for r in p_refs:
                r[...] = jnp.zeros_like(r)

        cv = [r[...] for r in cst]
        n_d = n_blk + n_rv + 1
        _, vjp = jax.vjp(lambda *a: chunk_fn(*a[:n_d], *cv, *a[n_d:], *known)[:2], *[_head_blocks(r) for r in blk],
                         *[_head_rows(r) for r in rvs], st_ref[0], *[r[...] for r in par])
        grads = vjp((_head_blocks(do_ref), ds_scr[...]))
        for h in range(H):
            for pos, k in enumerate(fresh):
                o_refs[pos][:, h * DH:(h + 1) * DH] = grads[k][h]
            for pos, k in enumerate(packed):
                col = pos * H * DH + h * DH
                o_refs[-1][:, col:col + DH] = grads[k][h].astype(o_refs[-1].dtype)
            for r, g in zip(rv_refs, grads[n_blk:n_blk + n_rv]):
                r[0, h:h + 1, :] = g[h]
        ds_scr[...] = grads[n_blk + n_rv]
        for r, g in zip(p_refs, grads[n_d:]):
            r[...] += g

    rc = lambda c: nc - 1 - c
    in_specs = [pl.BlockSpec((CH, H * DH), lambda c, f=f: (rc(c), f)) for _, f in blocks]
    in_specs += [pl.BlockSpec((1, H, CH), lambda c: (rc(c), 0, 0)) for _ in rowvecs]
    in_specs += [pl.BlockSpec(a.shape, lambda c: (0, 0, 0)) for a in consts]
    in_specs += [pl.BlockSpec((CH, a.shape[1]), lambda c: (rc(c), 0)) for a in shared]
    in_specs += [pl.BlockSpec(a.shape, lambda c, nd=a.ndim: (0,) * nd) for a in params]
    in_specs += [pl.BlockSpec((1, H, DH, DH), lambda c: (rc(c), 0, 0, 0)), pl.BlockSpec((CH, H * DH), lambda c: (rc(c), 0))]
    ins = [a for a, _ in blocks] + list(rowvecs) + list(consts) + list(shared) + list(params) + [states, do]
    if n_kept:
        in_specs.append(pl.BlockSpec((1,) + kept.shape[1:], lambda c: (rc(c), 0, 0, 0)))
        ins.append(kept)
    out_specs, out_shapes, aliases = [], [], {}
    for _ in fresh:
        out_specs.append(pl.BlockSpec((CH, H * DH), lambda c: (rc(c), 0)))
        out_shapes.append(jax.ShapeDtypeStruct((nc * CH, H * DH), F32))
    if into:
        arr, f, _ = into
        in_specs.append(pl.BlockSpec(memory_space=pl.ANY))
        aliases[len(ins)] = len(fresh)
        ins.append(arr)
        out_specs.append(pl.BlockSpec((CH, len(packed) * H * DH), lambda c: (rc(c), f)))
        out_shapes.append(jax.ShapeDtypeStruct(arr.shape, arr.dtype))
    for _ in rowvecs:
        out_specs.append(pl.BlockSpec((1, H, CH), lambda c: (rc(c), 0, 0)))
        out_shapes.append(jax.ShapeDtypeStruct((nc, H, CH), F32))
    for a in params:
        out_specs.append(pl.BlockSpec(a.shape, lambda c, nd=a.ndim: (0,) * nd))
        out_shapes.append(jax.ShapeDtypeStruct(a.shape, F32))
    return pl.pallas_call(
        body, name=name, grid=(nc,), in_specs=in_specs, out_specs=out_specs, out_shape=out_shapes,
        input_output_aliases=aliases, scratch_shapes=[pltpu.VMEM((H, DH, DH), F32)],
        compiler_params=pltpu.CompilerParams(dimension_semantics=("arbitrary",), vmem_limit_bytes=VMEM_LIMIT),
    )(*ins)


def _gdn_pre_specs(p, cws, alog_row, dtb_row, tm, pos):
    sub = tm // 8
    rows = [pl.BlockSpec((tm, 3 * D), lambda i: (pos(i), 0)),
            pl.BlockSpec((8, 3 * D), lambda i: (jnp.maximum(pos(i) * sub - 1, 0), 0)),
            pl.BlockSpec((tm, LANES), lambda i: (pos(i), OFF_BA // LANES))]
    pars = [pl.BlockSpec(a.shape, lambda i: (0, 0)) for a in (*cws, alog_row, dtb_row)]
    return rows + pars, [p, p, p, *cws, alog_row, dtb_row]


def gdn_pre_fwd(p, cws, alog_row, dtb_row, lp, tm):
    def body(x_ref, prev_ref, ba_ref, c0, c1, c2, c3, al_ref, dt_ref, q_ref, k_ref, v_ref, g_ref, b_ref, gc_ref):
        i = pl.program_id(0)
        prev = jnp.where(i > 0, prev_ref[...], 0.0)
        xw = jnp.concatenate([prev, x_ref[...]], axis=0)
        c = _conv_taps(xw, [c0[...], c1[...], c2[...], c3[...]], tm)
        q, k, v = _gdn_qkv(c)
        mask = (_row_ids(i, tm) >= PAD).astype(F32)
        g, b, gc = _gdn_gates(ba_ref[...], al_ref[...], dt_ref[...], mask)
        q_ref[...] = q
        k_ref[...] = k
        v_ref[...] = v
        g_ref[...] = g
        b_ref[...] = b
        gc_ref[...] = gc

    in_specs, ins = _gdn_pre_specs(p, cws, alog_row, dtb_row, tm, lambda i: i)
    o_spec = pl.BlockSpec((tm, D), lambda i: (i, 0))
    return pl.pallas_call(
        body, name="gdn_pre_fwd", grid=(lp // tm,), in_specs=in_specs,
        out_specs=[o_spec] * 5 + [pl.BlockSpec((tm, LANES), lambda i: (i, 0))],
        out_shape=[jax.ShapeDtypeStruct((lp, D), F32)] * 5 + [jax.ShapeDtypeStruct((lp, LANES), F32)],
        compiler_params=pltpu.CompilerParams(dimension_semantics=("parallel",), vmem_limit_bytes=VMEM_LIMIT),
    )(*ins)


def gdn_pre_bwd(p, cws, alog_row, dtb_row, dq, dk, dv, dg, db, dgc, dp, lp, tm):
    nt = lp // tm
    pos = lambda i: nt - 1 - i

    def body(x_ref, prev_ref, ba_ref, c0, c1, c2, c3, al_ref, dt_ref, dq_ref, dk_ref, dv_ref, dg_ref, db_ref, dgc_ref,
             dp_any, dx_ref, dba_ref, dcw_ref, dpar_ref, carry):
        i = pl.program_id(0)
        t = pos(i)

        @pl.when(i == 0)
        def _():
            carry[...] = jnp.zeros_like(carry)
            dcw_ref[...] = jnp.zeros_like(dcw_ref)
            dpar_ref[...] = jnp.zeros_like(dpar_ref)

        cws_v = [c0[...], c1[...], c2[...], c3[...]]
        prev = jnp.where(t > 0, prev_ref[...], 0.0)
        xw = jnp.concatenate([prev, x_ref[...]], axis=0)
        xs = _conv_shifted(xw, tm)
        c = sum(cws_v[j] * xs[j] for j in range(CONV_K))
        _, vjp_qkv = jax.vjp(_gdn_qkv, c)
        (dc,) = vjp_qkv((dq_ref[...], dk_ref[...], dv_ref[...]))
        zeros8 = jnp.zeros((8, 3 * D), F32)
        dcp = jnp.concatenate([zeros8, dc, zeros8], axis=0)
        dxw = sum(cws_v[j] * _rows_from(dcp, 3 - j, tm + 8) for j in range(CONV_K))
        dx_ref[...] = jnp.concatenate([dxw[8:tm], dxw[tm:] + carry[...]], axis=0).astype(dx_ref.dtype)
        carry[...] = dxw[:8]
        for j in range(CONV_K):
            dcw_ref[j:j + 1, :] += jnp.sum(dc * xs[j], axis=0, keepdims=True)
        mask = (_row_ids(t, tm) >= PAD).astype(F32)
        _, vjp_g = jax.vjp(lambda ba, al, dt: _gdn_gates(ba, al, dt, mask), ba_ref[...], al_ref[...], dt_ref[...])
        dba, dal, ddt = vjp_g((dg_ref[...], db_ref[...], dgc_ref[...]))
        dba_ref[...] = dba
        dpar_ref[0:1, :] += dal
        dpar_ref[1:2, :] += ddt

    in_specs, ins = _gdn_pre_specs(p, cws, alog_row, dtb_row, tm, pos)
    g_spec = pl.BlockSpec((tm, D), lambda i: (pos(i), 0))
    s_spec = pl.BlockSpec((tm, LANES), lambda i: (pos(i), 0))
    in_specs += [g_spec] * 5 + [s_spec, pl.BlockSpec(memory_space=pl.ANY)]
    ins += [dq, dk, dv, dg, db, dgc, dp]
    return pl.pallas_call(
        body, name="gdn_pre_bwd", grid=(nt,), in_specs=in_specs,
        out_specs=[pl.BlockSpec((tm, 3 * D), lambda i: (pos(i), 0)), s_spec,
                   pl.BlockSpec((8, 3 * D), lambda i: (0, 0)), pl.BlockSpec((8, LANES), lambda i: (0, 0))],
        out_shape=[jax.ShapeDtypeStruct(dp.shape, dp.dtype), jax.ShapeDtypeStruct((lp, LANES), F32),
                   jax.ShapeDtypeStruct((8, 3 * D), F32), jax.ShapeDtypeStruct((8, LANES), F32)],
        input_output_aliases={len(ins) - 1: 0},
        scratch_shapes=[pltpu.VMEM((8, 3 * D), F32)],
        compiler_params=pltpu.CompilerParams(dimension_semantics=("arbitrary",), vmem_limit_bytes=VMEM_LIMIT),
    )(*ins)


def _me():
    return lax.axis_index("x"), lax.axis_index("y"), lax.axis_index("c")


def _any_specs(n):
    return [pl.BlockSpec(memory_space=pl.ANY)] * n


Exchange = collections.namedtuple("Exchange", "ins out_shapes scratch start finish")


def _dma_sems(*shape):
    return pltpu.SemaphoreType.DMA(shape)


def run_exchange(name, ex):
    n_i, n_o = len(ex.ins), len(ex.out_shapes)

    def body(*refs):
        parts = (refs[:n_i], refs[n_i:n_i + n_o], refs[n_i + n_o:])
        ex.start(*parts)
        ex.finish(*parts)

    return pl.pallas_call(body, name=name, out_shape=list(ex.out_shapes), in_specs=_any_specs(n_i), out_specs=_any_specs(n_o),
                          scratch_shapes=list(ex.scratch))(*ex.ins)


def gather_exchange(ws):
    n = len(ws)

    def copies(w_refs, o_refs, sems):
        send_sems, recv_sems, fsend_sems, frecv_sems, osend_sems, orecv_sems = sems
        x, y, c = _me()
        me = 2 * x + y
        chips = [(1 - x, y), (x, 1 - y), (1 - x, 1 - y)]

        def own(a):
            return pltpu.make_async_remote_copy(src_ref=w_refs[a], dst_ref=o_refs[a].at[me], send_sem=osend_sems.at[a],
                                                recv_sem=orecv_sems.at[a], device_id=(x, y, 1 - c), device_id_type=MESH)

        def rows(a, cc):
            rh = ws[a].shape[0] // 2
            return pl.ds(pl.multiple_of(cc * rh, 8), rh)

        def ici(a, j, slot):
            px, py = chips[j]
            return pltpu.make_async_remote_copy(
                src_ref=w_refs[a].at[rows(a, c)], dst_ref=o_refs[a].at[slot, rows(a, c)], send_sem=send_sems.at[a, j],
                recv_sem=recv_sems.at[a, j], device_id=(px, py, c), device_id_type=MESH)

        def d2d(a, j, cc):
            px, py = chips[j]
            blk = o_refs[a].at[2 * px + py, rows(a, cc)]
            return pltpu.make_async_remote_copy(src_ref=blk, dst_ref=blk, send_sem=fsend_sems.at[a, j],
                                                recv_sem=frecv_sems.at[a, j], device_id=(x, y, 1 - c), device_id_type=MESH)
        mine = [own(a) for a in range(n)]
        sends = [ici(a, j, me) for a in range(n) for j in range(3)]
        arrivals = [ici(a, j, 2 * px + py) for a in range(n) for j, (px, py) in enumerate(chips)]
        passes = [d2d(a, j, c) for a in range(n) for j in range(3)]
        passed_to_me = [d2d(a, j, 1 - c) for a in range(n) for j in range(3)]
        return mine, sends, arrivals, passes, passed_to_me

    def start(w_refs, o_refs, sems):
        mine, sends, _, _, _ = copies(w_refs, o_refs, sems)
        for cp in mine + sends:
            cp.start()

    def finish(w_refs, o_refs, sems):
        mine, sends, arrivals, passes, passed_to_me = copies(w_refs, o_refs, sems)
        for arrival, onward in zip(arrivals, passes):
            arrival.wait_recv()
            onward.start()
        for cp in passed_to_me + mine:
            cp.wait_recv()
        for cp in sends + passes + mine:
            cp.wait_send()

    return Exchange(list(ws), [jax.ShapeDtypeStruct((N_CHIPS,) + w.shape, w.dtype) for w in ws],
                    [_dma_sems(n, 3), _dma_sems(n, 3), _dma_sems(n, 3), _dma_sems(n, 3), _dma_sems(n), _dma_sems(n)], start, finish)


def _simple_exchange(ins, out_shapes, sem_shape, copies):
    def start(i_refs, o_refs, sems):
        for cp in copies(i_refs, o_refs, *sems):
            cp.start()

    def finish(i_refs, o_refs, sems):
        cps = copies(i_refs, o_refs, *sems)
        for cp in cps:
            cp.wait_recv()
        for cp in cps:
            cp.wait_send()

    return Exchange(list(ins), out_shapes, [_dma_sems(*sem_shape), _dma_sems(*sem_shape)], start, finish)


def sibling_halves_exchange(gs):
    def copies(g_refs, o_refs, send_sems, recv_sems):
        x, y, c = _me()
        cps = []
        for a in range(len(gs)):
            rh = gs[a].shape[1] // 2
            for q in range(N_CHIPS):
                cps.append(pltpu.make_async_remote_copy(
                    src_ref=g_refs[a].at[q, pl.ds(pl.multiple_of((1 - c) * rh, 8), rh)], dst_ref=o_refs[a].at[q],
                    send_sem=send_sems.at[a, q], recv_sem=recv_sems.at[a, q], device_id=(x, y, 1 - c), device_id_type=MESH))
        return cps

    return _simple_exchange(gs, [jax.ShapeDtypeStruct((N_CHIPS, g.shape[1] // 2, g.shape[2]), g.dtype) for g in gs],
                            (len(gs), N_CHIPS), copies)


def scatter_chips_exchange(css):
    def copies(c_refs, o_refs, send_sems, recv_sems):
        x, y, c = _me()
        chips = [(1 - x, y), (x, 1 - y), (1 - x, 1 - y)]
        return [pltpu.make_async_remote_copy(
            src_ref=c_refs[a].at[2 * px + py], dst_ref=o_refs[a].at[j], send_sem=send_sems.at[a, j],
            recv_sem=recv_sems.at[a, j], device_id=(px, py, c), device_id_type=MESH)
            for a in range(len(css)) for j, (px, py) in enumerate(chips)]

    return _simple_exchange(css, [jax.ShapeDtypeStruct((3,) + cs.shape[1:], cs.dtype) for cs in css], (len(css), 3), copies)


def sibling_swap(name, halves):
    n = len(halves)

    def body(*refs):
        h_refs, o_refs = refs[:n], refs[n:2 * n]
        send_sems, recv_sems = refs[2 * n:]
        x, y, c = _me()
        cps = [pltpu.make_async_remote_copy(src_ref=h_refs[a], dst_ref=o_refs[a], send_sem=send_sems.at[a],
                                            recv_sem=recv_sems.at[a], device_id=(x, y, 1 - c), device_id_type=MESH)
               for a in range(n)]
        for cp in cps:
            cp.start()
        for cp in cps:
            cp.wait_recv()
        for cp in cps:
            cp.wait_send()

    return pl.pallas_call(
        body, name=name, out_shape=[jax.ShapeDtypeStruct(h.shape, h.dtype) for h in halves],
        in_specs=_any_specs(n), out_specs=_any_specs(n),
        scratch_shapes=[pltpu.SemaphoreType.DMA((n,)), pltpu.SemaphoreType.DMA((n,))],
    )(*halves)


def allgather_all(name, s):
    def body(s_ref, out_ref, send_sems, recv_sems, local_sem):
        x, y, c = _me()
        peers = [(x ^ ((m >> 2) & 1), y ^ ((m >> 1) & 1), c ^ (m & 1)) for m in range(1, 8)]
        mine = pltpu.make_async_copy(s_ref, out_ref.at[4 * x + 2 * y + c], local_sem)
        mine.start()

        def copy(j, slot):
            return pltpu.make_async_remote_copy(src_ref=s_ref, dst_ref=out_ref.at[slot], send_sem=send_sems.at[j],
                                                recv_sem=recv_sems.at[j], device_id=peers[j], device_id_type=MESH)
        sends = [copy(j, 4 * x + 2 * y + c) for j in range(7)]
        for cp in sends:
            cp.start()
        for j, (px, py, pc) in enumerate(peers):
            copy(j, 4 * px + 2 * py + pc).wait_recv()
        for cp in sends:
            cp.wait_send()
        mine.wait()

    return pl.pallas_call(
        body, name=name, out_shape=jax.ShapeDtypeStruct((8,) + s.shape, s.dtype),
        in_specs=_any_specs(1), out_specs=pl.BlockSpec(memory_space=pl.ANY),
        scratch_shapes=[pltpu.SemaphoreType.DMA((7,)), pltpu.SemaphoreType.DMA((7,)), pltpu.SemaphoreType.DMA(())],
    )(s)


def sum_slots(name, parts):
    n = len(parts)
    R, W = parts[0][0].shape[1:]
    tm = _tile_rows(R, W)
    idx = jnp.stack([jnp.asarray(s, jnp.int32) for _, s in parts])

    def body(idx_ref, *refs):
        acc = refs[0][...].astype(F32)
        for r in refs[1:n]:
            acc = acc + r[...].astype(F32)
        refs[n][...] = acc

    grid_spec = pltpu.PrefetchScalarGridSpec(
        num_scalar_prefetch=1, grid=(R // tm,),
        in_specs=[pl.BlockSpec((None, tm, W), lambda i, idx, k=k: (idx[k], i, 0)) for k in range(n)],
        out_specs=pl.BlockSpec((tm, W), lambda i, idx: (i, 0)))
    return pl.pallas_call(body, name=name, grid_spec=grid_spec, out_shape=jax.ShapeDtypeStruct((R, W), F32),
                          compiler_params=pltpu.CompilerParams(dimension_semantics=("parallel",)))(idx, *[a for a, _ in parts])


def chip_sums(name, g, recv, c):
    _, R, W = g.shape
    rh = R // 2
    tm = _tile_rows(rh, W)
    g8 = g.reshape(2 * N_CHIPS, rh, W)
    idx = jnp.asarray(c, jnp.int32).reshape(1)

    def body(idx_ref, g_ref, r_ref, o_ref):
        o_ref[...] = (g_ref[...] + r_ref[...]).astype(o_ref.dtype)

    grid_spec = pltpu.PrefetchScalarGridSpec(
        num_scalar_prefetch=1, grid=(N_CHIPS, rh // tm),
        in_specs=[pl.BlockSpec((None, tm, W), lambda q, i, idx: (2 * q + idx[0], i, 0)),
                  pl.BlockSpec((None, tm, W), lambda q, i, idx: (q, i, 0))],
        out_specs=pl.BlockSpec((None, tm, W), lambda q, i, idx: (q, i, 0)))
    return pl.pallas_call(body, name=name, grid_spec=grid_spec, out_shape=jax.ShapeDtypeStruct((N_CHIPS, rh, W), BF16),
                          compiler_params=pltpu.CompilerParams(dimension_semantics=("parallel", "parallel")))(idx, g8, recv)


def _adamw_update(w, g, m, v):
    m = ADAM_B1 * m + (1.0 - ADAM_B1) * g
    v = ADAM_B2 * v + (1.0 - ADAM_B2) * (g * g)
    m_hat = m / (1.0 - ADAM_B1 ** ADAM_STEP)
    v_hat = v / (1.0 - ADAM_B2 ** ADAM_STEP)
    return -ADAM_LR * (m_hat / (jnp.sqrt(v_hat) + ADAM_EPS) + ADAM_WD * w), m, v


def adamw(name, w, g, m, v):
    R, W = w.shape
    return rowwise(name, lambda i, w, g, m, v: _adamw_update(w, g, m, v), [(a, W, 0) for a in (w, g, m, v)], [],
                   [("new", W, F32)] * 3, n_rows=R, tm=_tile_rows(R, W, 2 ** 20))


def adamw_halves(name, w, g_mine, g_other, m, v, c):
    R, W = w.shape
    rh = R // 2
    tm = _tile_rows(rh, W, 2 ** 20)
    nh = rh // tm
    idx = jnp.asarray(c, jnp.int32).reshape(1)

    def body(idx_ref, w_ref, ga_ref, gb_ref, m_ref, v_ref, g_ref, d_ref, mo_ref, vo_ref):
        mine = (pl.program_id(0) // nh) == idx_ref[0]
        g = jnp.where(mine, ga_ref[...], gb_ref[...])
        d, m, v = _adamw_update(w_ref[...], g, m_ref[...], v_ref[...])
        g_ref[...] = g
        d_ref[...] = d
        mo_ref[...] = m
        vo_ref[...] = v

    full = pl.BlockSpec((tm, W), lambda i, idx: (i, 0))
    half = pl.BlockSpec((tm, W), lambda i, idx: (i % nh, 0))
    grid_spec = pltpu.PrefetchScalarGridSpec(num_scalar_prefetch=1, grid=(R // tm,), in_specs=[full, half, half, full, full],
                                             out_specs=[full] * 4)
    return pl.pallas_call(body, name=name, grid_spec=grid_spec, out_shape=[jax.ShapeDtypeStruct((R, W), F32)] * 4,
                          compiler_params=pltpu.CompilerParams(dimension_semantics=("parallel",)))(idx, w, g_mine, g_other, m, v)


W_IN_BLOCK = D_PROJ // N_CHIPS
BA_REF = OFF_RQ


def _w_in_padded_from_blocks(g):
    lo, hi = BA_REF - W_IN_BLOCK, BA_REF + 16 - W_IN_BLOCK
    return jnp.concatenate([g[0], g[1][:, :lo], g[1][:, hi:], g[2], g[3], g[1][:, lo:hi],
                            jnp.zeros((D, PW - D_PROJ), g.dtype)], axis=1)


def _w_in_grad_blocks(g):
    s = W_IN_BLOCK
    second = jnp.concatenate([g[:, s:BA_REF], g[:, OFF_BA:OFF_BA + 16], g[:, BA_REF:2 * s - 16]], axis=1)
    return jnp.stack([g[:, :s], second, g[:, 2 * s - 16:3 * s - 16], g[:, 3 * s - 16:4 * s - 16]])


def _ret_consts():
    f = F32
    log_gamma = jnp.log1p(-jnp.exp2(-5.0 - jnp.arange(H, dtype=f)))
    pos = jnp.arange(CH, dtype=f)
    causal = jnp.tril(jnp.ones((CH, CH), dtype=bool))
    diff = pos[:, None] - pos[None, :]
    decay = jnp.where(causal, jnp.exp(jnp.where(causal, diff, 0.0) * log_gamma[:, None, None]), 0.0)
    xi = jnp.broadcast_to(jnp.exp((pos + 1.0) * log_gamma[:, None])[:, :, None], (H, CH, DH))
    zeta = jnp.broadcast_to(jnp.exp((CH - 1.0 - pos) * log_gamma[:, None])[:, :, None], (H, CH, DH))
    cd = jnp.broadcast_to(jnp.exp(CH * log_gamma)[:, None, None], (H, 1, DH))
    return decay, xi, zeta, cd


def _rope_tables(lp):
    pos = jnp.arange(lp, dtype=F32) - float(PAD)
    inv = 1.0 / (ROPE_BASE ** jnp.linspace(0.0, 1.0, DH // 2, dtype=F32))
    ang = pos[:, None] * inv[None, :]
    cos, sin = jnp.cos(ang), jnp.sin(ang)
    ct = jnp.repeat(cos, 2, axis=1)
    st = jnp.stack([-sin, sin], axis=-1).reshape(lp, DH)
    return ct, st


def local_step(x, tgt, w, small, hooks=None):
    hooks = hooks or {}
    seq = x.shape[0]
    lp = HEAD_ROWS + seq
    nc = lp // CH
    tm = _pick(lp, 192, CH)
    row = functools.partial(rowwise, n_rows=lp, tm=tm)
    gw = {}

    def mm(name, *args, **kw):
        if name not in hooks:
            return matmul(name, *args, **kw)
        make_exchange, take = hooks[name]
        out, results = matmul(name, *args, side=make_exchange(gw), **kw)
        take(results)
        return out

    h0 = jnp.concatenate([jnp.zeros((PAD, D), F32), w["meta_tokens"], x], axis=0)
    tgt_p = jnp.concatenate([jnp.zeros((HEAD_ROWS, D), F32), tgt], axis=0)

    def ffn_fwd(tag, h, wn, n=None, next_norm=None, loss=None):
        if n is None:
            norm_args = (f"{tag}_norm", lambda i, h, wn: _rms(h, wn), [(h, D, 0)], [wn], [("new", D, BF16)])
            if norm_args[0] in hooks:
                make_exchange, take = hooks[norm_args[0]]
                (n,), results = row(*norm_args, side=make_exchange(gw))
                take(results)
            else:
                n = row(*norm_args)[0]
        g, u, mid = mm(f"{tag}_up", n, w[f"{tag}_w_in"], "nn", b_split=True, pair=True, outs=(F32, F32, BF16),
                       epilogue=lambda pr: (pr[0], pr[1], _silu(pr[0]) * pr[1]))
        down = functools.partial(mm, f"{tag}_down", mid, w[f"{tag}_w_out"], "nn", ti_cap=DOWN_ROWS, tj_cap=1024, tr_cap=2816,
                                 rows=[h])
        if loss is not None:
            target, final_w = loss
            ti = _pick(lp, DOWN_ROWS, 16)

            def head(pr, h, t, nw):
                mask = (_row_ids(pl.program_id(0), ti) >= HEAD_ROWS).astype(F32)
                y, vjp = jax.vjp(_rms, h + 0.5 * pr[0], nw)
                err = (y - t) * mask
                dh, dw = vjp(err * (1.0 / D))
                return dh, dw, jnp.sum(err * err, keepdims=True).reshape(1, 1) * (0.5 / D) * jnp.ones((1, LANES), F32)
            return down(rows=[h, target], pars=[final_w], accs=[(1, D), (1, LANES)], epilogue=head), (h, n, g, u, mid), None
        if next_norm is None:
            return down(epilogue=lambda pr, h: (h + 0.5 * pr[0],)), (h, n, g, u, mid), None
        out, n_next = down(pars=[next_norm], outs=(F32, BF16), epilogue=lambda pr, h, nw: _with_norm(h + 0.5 * pr[0], nw))
        return out, (h, n, g, u, mid), n_next

    def _with_norm(out, nw):
        return out, _rms(out, nw)

    def norm_bwd_epilogue(pr, h, dres, wn):
        _, vjp = jax.vjp(_rms, h, wn)
        dh, dw = vjp(pr[0])
        return dres + dh, dw

    def ffn_bwd(tag, dh, saved, wn):
        h, n, g, u, mid = saved
        w_in_il, w_out = w[f"{tag}_w_in_il"], w[f"{tag}_w_out"]
        dw_out = mm(f"{tag}_dwout", mid, dh, "tn", ti_cap=1408, tj_cap=1024, tr_cap=1376, epilogue=lambda pr: (0.5 * pr[0],))
        gw[f"{tag}_w_out"] = dw_out

        def act_bwd(pr, g, u):
            _, vjp = jax.vjp(lambda g, u: _silu(g) * u, g, u)
            return (jnp.concatenate(vjp(0.5 * pr[0]), axis=1),)
        dgu = mm(f"{tag}_dmid", dh, w_out, "nt", tj_cap=2 * DFF // N_CHIPS, rows=[g, u], outs=(BF16,), wide=2, epilogue=act_bwd)
        dh_in, dwn = mm(f"{tag}_dn", dgu, w_in_il, "nt", tj_cap=1024, tr_cap=DFF, rows=[h, dh], pars=[wn],
                        accs=[(1, D)], epilogue=norm_bwd_epilogue)
        dw_in = mm(f"{tag}_dwin", n, dgu, "tn", ti_cap=1024, tr_cap=2752, o_split=True, o_interleaved=True)
        return dh_in, dw_in, dw_out, dwn

    h1, ffn1_saved, n2 = ffn_fwd("ffn1", h0, small["ffn1_norm"], next_norm=small["mix_norm"])
    p = mm("mix_proj", n2, w["w_in_p"], "nn", ti_cap=1376, tj_cap=1152)

    cws = [w["gdn_conv_w"][j:j + 1] for j in range(CONV_K)]
    alog_row = jnp.zeros((1, LANES), F32).at[:, H:2 * H].set(small["gdn_a_log"])
    dtb_row = jnp.zeros((1, LANES), F32).at[:, H:2 * H].set(small["gdn_dt_bias"])
    q, k, v, gcb, bb, gc = gdn_pre_fwd(p, cws, alog_row, dtb_row, lp, tm)
    gc_rows = gc[:, H:2 * H].reshape(nc, CH, H).transpose(0, 2, 1)
    gdn_blocks = [(q, 0), (k, 0), (v, 0), (gcb, 0), (bb, 0), (p, OFF_Z // D)]
    y_a, gdn_states, gdn_inv = _scan_fwd("gdn_scan_fwd", _gdn_chunk, gdn_blocks, [gc_rows], [], nc,
                                         params=[small["gdn_out_norm"]], out_dtype=BF16, keep=(CH, CH))

    ct, st = _rope_tables(lp)
    ret_consts = list(_ret_consts())
    ret_blocks = [(p, OFF_RQ // D), (p, OFF_RK // D), (p, OFF_RV // D), (p, OFF_RG // D)]
    ret_w = small["ret_out_norm"].reshape(H, 1, DH)
    y_b, ret_states = _scan_fwd("ret_scan_fwd", _ret_chunk, ret_blocks, [], ret_consts, nc, shared=[ct, st],
                                params=[ret_w], out_dtype=BF16)

    br_a = matmul("branch_gdn", y_a, w["w_branch_gdn"], "nn", ti_cap=1376, tj_cap=1024)
    br_b = matmul("branch_ret", y_b, w["w_branch_ret"], "nn", ti_cap=1376, tj_cap=1024)
    merged = row("merge", lambda i, a, b, ga, gb_: _merge(a, b, ga, gb_),
                 [(br_a, D, 0), (br_b, D, 0), (p, D, OFF_GA // D), (p, D, OFF_GB // D)], [], [("new", D, BF16)])[0]
    h2, n3 = matmul("mix_out", merged, w["w_out"], "nn", ti_cap=1376, tj_cap=1024, rows=[h1], pars=[small["ffn2_norm"]],
                    outs=(F32, BF16), epilogue=lambda pr, h, nw: _with_norm(h + pr[0], nw))
    (dh3, d_final, loss_row), ffn2_saved, _ = ffn_fwd("ffn2", h2, small["ffn2_norm"], n=n3, loss=(tgt_p, small["final_norm"]))

    gs = {"final_norm": d_final}
    dh2, gw["ffn2_w_in"], gw["ffn2_w_out"], gs["ffn2_norm"] = ffn_bwd("ffn2", dh3, ffn2_saved, small["ffn2_norm"])
    dmerged = matmul("mix_out_dx", dh2, w["w_out"], "nt", ti_cap=1376, tj_cap=1024)
    gw["w_out"] = matmul("mix_out_dw", merged, dh2, "tn", ti_cap=1024, tj_cap=1024, tr_cap=2752)
    dp = lax.empty((lp, PW), BF16)

    def merge_bwd(i, dm, a, b, ga, gb_):
        _, vjp = jax.vjp(_merge, a, b, ga, gb_)
        da, db, dga, dgb = vjp(dm)
        return da, db, jnp.concatenate([dga, dgb], axis=1)
    da, db_, dp = row("merge_bwd", merge_bwd,
                      [(dmerged, D, 0), (br_a, D, 0), (br_b, D, 0), (p, D, OFF_GA // D), (p, D, OFF_GB // D)], [],
                      [("new", D, BF16), ("new", D, BF16), ("into", dp, 2 * D, OFF_GA // (2 * D))])
    dy_a = matmul("branch_gdn_dx", da, w["w_branch_gdn"], "nt", ti_cap=1376, tj_cap=1024)
    gw["w_branch_gdn"] = matmul("branch_gdn_dw", y_a, da, "tn", ti_cap=1024, tj_cap=1024, tr_cap=2752)
    dy_b = matmul("branch_ret_dx", db_, w["w_branch_ret"], "nt", ti_cap=1376, tj_cap=1024)
    gw["w_branch_ret"] = matmul("branch_ret_dw", y_b, db_, "tn", ti_cap=1024, tj_cap=1024, tr_cap=2752)

    dp, d_ret_w = _scan_bwd("ret_scan_bwd", _ret_chunk, ret_blocks, [], ret_consts, ret_states, dy_b,
                            (dp, OFF_RQ // (4 * D), [0, 1, 2, 3]), nc, shared=[ct, st], params=[ret_w])
    gs["ret_out_norm"] = d_ret_w.reshape(1, D)
    dq, dk, dv, dgcb, dbb, dp, dgc_rows, gs["gdn_out_norm"] = _scan_bwd(
        "gdn_scan_bwd", _gdn_chunk, gdn_blocks, [gc_rows], [], gdn_states, dy_a, (dp, OFF_Z // D, [5]), nc,
        params=[small["gdn_out_norm"]], kept=gdn_inv)
    dgc = jnp.pad(dgc_rows.transpose(0, 2, 1).reshape(lp, H), ((0, 0), (H, LANES - 2 * H)))
    dp, dba, dcw, dgate = gdn_pre_bwd(p, cws, alog_row, dtb_row, dq, dk, dv, dgcb, dbb, dgc, dp, lp, tm)
    dp = row("dp_ba", lambda i, t: t, [(dba, LANES, 0)], [], [("into", dp, LANES, OFF_BA // LANES)])[0]
    gw["gdn_conv_w"] = dcw[:CONV_K]
    gs["gdn_a_log"] = dgate[0:1, H:2 * H]
    gs["gdn_dt_bias"] = dgate[1:2, H:2 * H]

    dh1, gs["mix_norm"] = mm("mix_proj_dx", dp, w["w_in_p"], "nt", tj_cap=1024, tr_cap=3456, rows=[h1, dh2],
                                 pars=[small["mix_norm"]], accs=[(1, D)], epilogue=norm_bwd_epilogue)
    gw["w_in_p"] = mm("mix_proj_dw", n2, dp, "tn", ti_cap=1024, tj_cap=1152, tr_cap=2752)
    dh0, gw["ffn1_w_in"], gw["ffn1_w_out"], gs["ffn1_norm"] = ffn_bwd("ffn1", dh1, ffn1_saved, small["ffn1_norm"])
    gw["meta_tokens"] = dh0[PAD:HEAD_ROWS]
    return loss_row, dh0[HEAD_ROWS:], gw, gs


BIG = ("ffn1_w_in", "ffn1_w_out", "w_in", "w_branch_gdn", "w_branch_ret", "w_out", "ffn2_w_in", "ffn2_w_out")
COL_SHARDED = ("ffn1_w_in", "w_in", "ffn2_w_in")
WITH_FFN1_UP = ("ffn1_w_out", "w_in")
WITH_MIX_PROJ = ("w_branch_gdn", "w_branch_ret", "w_out", "ffn2_w_in", "ffn2_w_out")
MIXER_GRADS = ("ffn2_w_in", "ffn2_w_out", "w_branch_gdn", "w_branch_ret", "w_out")
FFN1_EARLY_GRADS = ("w_in", "ffn1_w_out")
LATE_GRADS = ("ffn1_w_in",)
SMALL = ("ffn1_norm", "mix_norm", "ret_out_norm", "ffn2_norm", "final_norm", "gdn_out_norm", "gdn_a_log", "gdn_dt_bias")
WEIGHTS = ("meta_tokens", "ffn1_norm", "ffn1_w_in", "ffn1_w_out", "mix_norm", "w_in", "gdn_conv_w", "gdn_a_log", "gdn_dt_bias",
           "gdn_out_norm", "ret_out_norm", "w_branch_gdn", "w_branch_ret", "w_out", "ffn2_norm", "ffn2_w_in", "ffn2_w_out",
           "final_norm")
LOSS_ROW = 6
CONV_ROW0, META_ROW0, SMALL_ROWS = 8, 24, 40


def pack_small(vals):
    rows = [vals[n].reshape(1, D) for n in SMALL[:5]]
    r5 = jnp.concatenate([vals["gdn_out_norm"].reshape(1, DH), vals["gdn_a_log"].reshape(1, H), vals["gdn_dt_bias"].reshape(1, H),
                          jnp.zeros((1, D - DH - 2 * H), F32)], axis=1)
    return jnp.concatenate(rows + [r5, jnp.zeros((2, D), F32)], axis=0)


def unpack_small(packed, shapes):
    out = {n: packed[j].reshape(shapes[n]) for j, n in enumerate(SMALL[:5])}
    out["gdn_out_norm"] = packed[5, :DH].reshape(shapes["gdn_out_norm"])
    out["gdn_a_log"] = packed[5, DH:DH + H].reshape(shapes["gdn_a_log"])
    out["gdn_dt_bias"] = packed[5, DH + H:DH + 2 * H].reshape(shapes["gdn_dt_bias"])
    return out


def kernel(x, meta_tokens, ffn1_norm, ffn1_w_in, ffn1_w_out, mix_norm, w_in, gdn_conv_w, gdn_a_log, gdn_dt_bias, gdn_out_norm, ret_out_norm, w_branch_gdn, w_branch_ret, w_out, ffn2_norm, ffn2_w_in, ffn2_w_out, final_norm, loss_target, m_meta_tokens, m_ffn1_norm, m_ffn1_w_in, m_ffn1_w_out, m_mix_norm, m_w_in, m_gdn_conv_w, m_gdn_a_log, m_gdn_dt_bias, m_gdn_out_norm, m_ret_out_norm, m_w_branch_gdn, m_w_branch_ret, m_w_out, m_ffn2_norm, m_ffn2_w_in, m_ffn2_w_out, m_final_norm, v_meta_tokens, v_ffn1_norm, v_ffn1_w_in, v_ffn1_w_out, v_mix_norm, v_w_in, v_gdn_conv_w, v_gdn_a_log, v_gdn_dt_bias, v_gdn_out_norm, v_ret_out_norm, v_w_branch_gdn, v_w_branch_ret, v_w_out, v_ffn2_norm, v_ffn2_w_in, v_ffn2_w_out, v_final_norm):
    a = dict(locals())
    wts = {n: a[n] for n in WEIGHTS}
    mom_m = {n: a["m_" + n] for n in WEIGHTS}
    mom_v = {n: a["v_" + n] for n in WEIGHTS}
    shapes = {n: wts[n].shape for n in WEIGHTS}
    flat = lambda t: t.reshape(t.shape[-2:])
    c = lax.axis_index("c")
    chip = 2 * lax.axis_index("x") + lax.axis_index("y")

    exact = jnp.zeros((16, D), F32).at[0:3].set(wts["gdn_conv_w"].reshape(3, D)).at[3:7].set(wts["meta_tokens"].reshape(4, D))
    bf16_block = lambda n: flat(wts[n]).astype(BF16)
    w = {}

    def take_weights(names):
        def take(gathered):
            for n, g in zip(names, gathered):
                if n == "w_in":
                    w["w_in_p"] = _w_in_padded_from_blocks(g)
                elif n in COL_SHARDED:
                    w[n], w[n + "_il"] = g, jnp.concatenate([g[q] for q in INTERLEAVED], axis=1)
                else:
                    w[n] = g.reshape(N_CHIPS * g.shape[1], D)
        return take

    exact = run_exchange("gather_first", gather_exchange([exact]))[0]
    w["gdn_conv_w"] = jnp.concatenate([exact[q, 0:3].reshape(CONV_K, 3 * D // 4) for q in range(N_CHIPS)], axis=1)
    w["meta_tokens"] = jnp.concatenate([exact[q, 3:7].reshape(N_META, D // 4) for q in range(N_CHIPS)], axis=1)
    small = {n: wts[n].reshape(1, -1) for n in SMALL}
    hooks = {"ffn1_norm": (lambda gw: gather_exchange([bf16_block("ffn1_w_in")]), take_weights(["ffn1_w_in"])),
             "ffn1_up": (lambda gw: gather_exchange([bf16_block(n) for n in WITH_FFN1_UP]), take_weights(WITH_FFN1_UP)),
             "mix_proj": (lambda gw: gather_exchange([bf16_block(n) for n in WITH_MIX_PROJ]), take_weights(WITH_MIX_PROJ))}

    def blocks(gw, n):
        if n == "w_in":
            return _w_in_grad_blocks(gw["w_in_p"])
        return gw[n] if n in COL_SHARDED else gw[n].reshape(N_CHIPS, gw[n].shape[0] // N_CHIPS, D)

    chip_sum, from_chips = {}, {}

    def swap_of(names):
        def make(gw):
            chip_sum["g"] = [blocks(gw, n) for n in names]
            return sibling_halves_exchange(chip_sum["g"])

        def take(from_sib):
            for n, g, r in zip(names, chip_sum.pop("g"), from_sib):
                chip_sum[n] = chip_sums(f"grads_chip_sum_{n}", g, r, c)
        return make, take

    def scatter_of(names):
        return (lambda gw: scatter_chips_exchange([chip_sum[n] for n in names])), (lambda got: from_chips.update(zip(names, got)))
    hooks["mix_proj_dx"], hooks["mix_proj_dw"] = swap_of(MIXER_GRADS), scatter_of(MIXER_GRADS)
    hooks["ffn1_dmid"] = swap_of(FFN1_EARLY_GRADS)
    hooks["ffn1_dn"], hooks["ffn1_dwin"] = scatter_of(FFN1_EARLY_GRADS[:1]), scatter_of(FFN1_EARLY_GRADS[1:])

    loss_row, gx, gw, gs = local_step(x[0], loss_target[0], w, small, hooks)

    late = [blocks(gw, n) for n in LATE_GRADS]
    from_sib = run_exchange("grads_sibling_late", sibling_halves_exchange(late))
    for n, g, r in zip(LATE_GRADS, late, from_sib):
        chip_sum[n] = chip_sums(f"grads_chip_sum_{n}", g, r, c)
    from_chips.update(zip(LATE_GRADS, run_exchange("grads_scatter_late", scatter_chips_exchange([chip_sum[n] for n in LATE_GRADS]))))
    halves = [sum_slots(f"grads_sum_{n}", [(chip_sum[n], chip), (from_chips[n], 0), (from_chips[n], 1), (from_chips[n], 2)])
              for n in BIG]
    others = sibling_swap("grads_swap", halves)
    grads, delta, new_m, new_v = {}, {}, {}, {}
    for n, mine, other in zip(BIG, halves, others):
        res = adamw_halves(f"adamw_{n}", flat(wts[n]), mine, other, flat(mom_m[n]), flat(mom_v[n]), c)
        grads[n], delta[n], new_m[n], new_v[n] = (t.reshape(shapes[n]) for t in res)

    sm = jnp.concatenate([pack_small(gs).at[LOSS_ROW, :LANES].set(loss_row[0]),
                          gw["gdn_conv_w"].reshape(3 * CONV_K, D), jnp.zeros((META_ROW0 - CONV_ROW0 - 3 * CONV_K, D), F32),
                          gw["meta_tokens"]], axis=0)
    every = allgather_all("small_gather", sm)
    sm_sum = sum_slots("small_sum", [(every, s) for s in range(8)])
    d_s, m_s, v_s = adamw("adamw_small", pack_small(small), sm_sum[:8], pack_small({n: mom_m[n].reshape(1, -1) for n in SMALL}),
                          pack_small({n: mom_v[n].reshape(1, -1) for n in SMALL}))
    grads.update(unpack_small(sm_sum, shapes))
    delta.update(unpack_small(d_s, shapes))
    new_m.update(unpack_small(m_s, shapes))
    new_v.update(unpack_small(v_s, shapes))
    g_conv = lax.dynamic_slice_in_dim(sm_sum[CONV_ROW0:CONV_ROW0 + 3 * CONV_K].reshape(CONV_K, 3 * D), chip * (3 * D // 4), 3 * D // 4, 1)
    g_meta = lax.dynamic_slice_in_dim(sm_sum[META_ROW0:META_ROW0 + N_META], chip * (D // 4), D // 4, 1)
    for n, g in (("gdn_conv_w", g_conv), ("meta_tokens", g_meta)):
        d_, m_, v_ = adamw(f"adamw_{n}", flat(wts[n]), g, flat(mom_m[n]), flat(mom_v[n]))
        grads[n], delta[n], new_m[n], new_v[n] = (t.reshape(shapes[n]) for t in (g, d_, m_, v_))
    loss = sm_sum[LOSS_ROW, 0]

    return (loss, gx[None], *[grads[n] for n in WEIGHTS], *[delta[n] for n in WEIGHTS], *[new_m[n] for n in WEIGHTS],
            *[new_v[n] for n in WEIGHTS])
```
